```python
import math
import jax
import jax.numpy as jnp
from jax import lax
import numpy as np

D_MODEL = 1024
BATCH = 16
SEQ = 2048
DEPTH = 1
DEC_BATCH = 128
DEC_SEQ = 8
PAST_LEN = 16384
PAGE_SIZE = 128

N_HEADS = 8
N_KV_HEADS = 2
HEAD_DIM = 64
GROUP = N_HEADS // N_KV_HEADS
WINDOW = 128
Q_BLOCK = WINDOW
D_ATTN = N_HEADS * HEAD_DIM
D_KV = N_KV_HEADS * HEAD_DIM
N_BUCKETS = 32
MAX_DISTANCE = 128
D_RNN = D_MODEL
RNN_BLOCK = 256
N_RNN_BLOCKS = D_RNN // RNN_BLOCK
CONV_W = 4
RG_C = 8.0
N_EXPERTS = 256
TOP_K = 8
N_GROUPS = 8
TOPK_GROUPS = 4
D_EXPERT = D_MODEL // 4
D_SHARED = D_EXPERT
ROUTED_SCALE = 2.5
EXPERT_BLOCK = 128
LN_EPS = 1e-5
ALPHA = (2 * DEPTH) ** 0.25
BETA = (8 * DEPTH) ** -0.25
NEG_INF = -1e30
IN_OFFSETS = (D_ATTN, D_ATTN + D_KV, D_ATTN + 2 * D_KV, D_ATTN + 2 * D_KV + D_RNN, D_ATTN + 2 * D_KV + 2 * D_RNN, D_ATTN + 2 * D_KV + 2 * D_RNN + D_MODEL)
D_IN = D_ATTN + 2 * D_KV + 2 * D_RNN + 2 * D_MODEL

kernel_name = 'hawk_swa_sink_moe_decoder_step'


def layer_norm(x, g, b):
    xf = x.astype(jnp.float32)
    mu = xf.mean(-1, keepdims=True)
    var = jnp.square(xf - mu).mean(-1, keepdims=True)
    return ((xf - mu) * lax.rsqrt(var + LN_EPS) * g + b).astype(x.dtype)


def t5_bucket(dist):
    n = jnp.maximum(dist, 0)
    max_exact = N_BUCKETS // 2
    nf = jnp.maximum(n, 1).astype(jnp.float32)
    large = max_exact + (jnp.log(nf / max_exact) / math.log(MAX_DISTANCE / max_exact) * (N_BUCKETS - max_exact)).astype(jnp.int32)
    large = jnp.minimum(large, N_BUCKETS - 1)
    return jnp.where(n < max_exact, n, large)


def rel_bias_heads(rel_bias, dist):
    b = rel_bias[t5_bucket(dist)].astype(jnp.float32)
    return jnp.moveaxis(b, -1, 0).reshape(N_KV_HEADS, GROUP, *dist.shape)


def sink_softmax(s, sink):
    m = jnp.maximum(s.max(-1, keepdims=True), sink)
    p = jnp.exp(s - m)
    return p / (p.sum(-1, keepdims=True) + jnp.exp(sink - m))


def banded_window_attention(q, k, v, rel_bias, sinks):
    B, L = q.shape[:2]
    nb = L // Q_BLOCK
    qb = q.reshape(B, nb, Q_BLOCK, N_KV_HEADS, GROUP, HEAD_DIM)
    kb = k.reshape(B, nb, Q_BLOCK, N_KV_HEADS, HEAD_DIM)
    vb = v.reshape(B, nb, Q_BLOCK, N_KV_HEADS, HEAD_DIM)
    shift = lambda t: jnp.pad(t, ((0, 0), (1, 0), (0, 0), (0, 0), (0, 0)))[:, :-1]
    kk = jnp.concatenate([shift(kb), kb], axis=2)
    vv = jnp.concatenate([shift(vb), vb], axis=2)
    s = jnp.einsum('bnqkgd,bnskd->bnkgqs', qb, kk, preferred_element_type=jnp.float32) * HEAD_DIM ** -0.5
    dist = jnp.arange(Q_BLOCK)[:, None] + Q_BLOCK - jnp.arange(2 * Q_BLOCK)[None, :]
    band = (dist >= 0) & (dist <= WINDOW)
    has_prev = (jnp.arange(nb)[:, None] > 0) | (jnp.arange(2 * Q_BLOCK)[None, :] >= Q_BLOCK)
    valid = band[None] & has_prev[:, None, :]
    s = jnp.where(valid[None, :, None, None], s + rel_bias_heads(rel_bias, dist), NEG_INF)
    p = sink_softmax(s, sinks.astype(jnp.float32).reshape(N_KV_HEADS, GROUP, 1, 1))
    o = jnp.einsum('bnkgqs,bnskd->bnqkgd', p.astype(vv.dtype), vv)
    return o.reshape(B, L, D_ATTN)


def window_decode_attention(q, k, v, k_cache, v_cache, rel_bias, sinks):
    B, S = q.shape[:2]
    W = k_cache.shape[1]
    kk = jnp.concatenate([k_cache, k], axis=1)
    vv = jnp.concatenate([v_cache, v], axis=1)
    qg = q.reshape(B, S, N_KV_HEADS, GROUP, HEAD_DIM)
    s = jnp.einsum('bqkgd,bskd->bkgqs', qg, kk, preferred_element_type=jnp.float32) * HEAD_DIM ** -0.5
    dist = jnp.arange(S)[:, None] + W - jnp.arange(W + S)[None, :]
    valid = (dist >= 0) & (dist <= WINDOW)
    s = jnp.where(valid, s + rel_bias_heads(rel_bias, dist), NEG_INF)
    p = sink_softmax(s, sinks.astype(jnp.float32).reshape(N_KV_HEADS, GROUP, 1, 1))
    o = jnp.einsum('bkgqs,bskd->bqkgd', p.astype(vv.dtype), vv).reshape(B, S, D_ATTN)
    return o, kk[:, -W:], vv[:, -W:]


def lru_combine(c1, c2):
    a1, b1 = c1
    a2, b2 = c2
    return a1 * a2, a2 * b1 + b2


def rglru_branch(xr, yr, conv_hist, h0, conv_w, conv_b, w_gate_a, b_gate_a, w_gate_x, b_gate_x, lru_lambda):
    B, L, _ = xr.shape
    if conv_hist is None:
        xp = jnp.pad(xr, ((0, 0), (CONV_W - 1, 0), (0, 0)))
    else:
        xp = jnp.concatenate([conv_hist.astype(xr.dtype), xr], axis=1)
    xc = conv_b + sum(conv_w[j] * xp[:, j:j + L] for j in range(CONV_W))
    new_conv = xp[:, L:]
    xf = xc.astype(jnp.float32)
    blk = xf.reshape(B, L, N_RNN_BLOCKS, RNN_BLOCK)
    r = jax.nn.sigmoid(jnp.einsum('blnc,ncd->blnd', blk, w_gate_a.astype(jnp.float32)).reshape(B, L, D_RNN) + b_gate_a)
    i = jax.nn.sigmoid(jnp.einsum('blnc,ncd->blnd', blk, w_gate_x.astype(jnp.float32)).reshape(B, L, D_RNN) + b_gate_x)
    log_a = -RG_C * r * jax.nn.softplus(-lru_lambda.astype(jnp.float32))
    a = jnp.exp(log_a)
    mult = jnp.sqrt(-jnp.expm1(2.0 * log_a))
    if h0 is None:
        mult = mult.at[:, 0].set(1.0)
    b = mult * i * xf
    if h0 is not None:
        b = b.at[:, 0].add(a[:, 0] * h0.astype(jnp.float32))
    _, h = lax.associative_scan(lru_combine, (a, b), axis=1)
    out = (h * jax.nn.gelu(yr.astype(jnp.float32))).astype(xr.dtype)
    return out, new_conv, h[:, -1].astype(xr.dtype)


def route(h, w_router, router_bias):
    T = h.shape[0]
    scores = jax.nn.sigmoid(h.astype(jnp.float32) @ w_router.astype(jnp.float32))
    grp = (scores + router_bias.astype(jnp.float32)).reshape(T, N_GROUPS, N_EXPERTS // N_GROUPS)
    gscore = lax.top_k(grp, 2)[0].sum(-1)
    _, gidx = lax.top_k(gscore, TOPK_GROUPS)
    gmask = (gidx[:, :, None] == jnp.arange(N_GROUPS)).any(axis=1)
    masked = jnp.where(gmask[:, :, None], grp, -jnp.inf).reshape(T, N_EXPERTS)
    _, idx = lax.top_k(masked, TOP_K)
    w = jnp.take_along_axis(scores, idx, axis=-1)
    w = w / w.sum(-1, keepdims=True) * ROUTED_SCALE
    return idx, w


def routed_experts(h, idx, w, w_e_gate, w_e_up, w_e_down):
    T, D = h.shape
    N = T * TOP_K
    flat_e = idx.reshape(N)
    flat_tok = jnp.repeat(jnp.arange(T, dtype=jnp.int32), TOP_K)
    flat_w = w.reshape(N)
    order = jnp.argsort(flat_e)
    sorted_e = flat_e[order]
    counts = jnp.bincount(flat_e, length=N_EXPERTS)
    padded = (counts + EXPERT_BLOCK - 1) // EXPERT_BLOCK * EXPERT_BLOCK
    start = jnp.cumsum(counts) - counts
    pend = jnp.cumsum(padded)
    pstart = pend - padded
    dest = pstart[sorted_e] + jnp.arange(N, dtype=jnp.int32) - start[sorted_e]
    n_blocks = (N + N_EXPERTS * (EXPERT_BLOCK - 1) + EXPERT_BLOCK - 1) // EXPERT_BLOCK
    R = n_blocks * EXPERT_BLOCK
    row_tok = jnp.full((R,), T, jnp.int32).at[dest].set(flat_tok[order])
    row_w = jnp.zeros((R,), jnp.float32).at[dest].set(flat_w[order])
    block_start = jnp.arange(n_blocks, dtype=jnp.int32) * EXPERT_BLOCK
    block_e = jnp.minimum(jnp.searchsorted(pend, block_start, side='right'), N_EXPERTS - 1)
    block_used = block_start < pend[-1]
    h_pad = jnp.concatenate([h, jnp.zeros((1, D), h.dtype)], axis=0)

    def compute(tok, rw, e):
        xb = h_pad[tok]
        u = jax.nn.silu(xb @ w_e_gate[e]) * (xb @ w_e_up[e])
        return ((u @ w_e_down[e]) * rw[:, None]).astype(h.dtype)

    def empty(tok, rw, e):
        return jnp.zeros((EXPERT_BLOCK, D), h.dtype)

    def expert_block(args):
        tok, rw, e, used = args
        return lax.cond(used, compute, empty, tok, rw, e)

    out = lax.map(expert_block, (row_tok.reshape(n_blocks, EXPERT_BLOCK), row_w.reshape(n_blocks, EXPERT_BLOCK), block_e, block_used))
    y = jnp.zeros((T + 1, D), h.dtype).at[row_tok].add(out.reshape(R, D))
    return y[:T]


def moe(x, w_router, router_bias, w_e_gate, w_e_up, w_e_down, w_s_gate, w_s_up, w_s_down):
    B, L, D = x.shape
    h = x.reshape(B * L, D)
    idx, w = route(h, w_router, router_bias)
    shared = (jax.nn.silu(h @ w_s_gate) * (h @ w_s_up)) @ w_s_down
    return (routed_experts(h, idx, w, w_e_gate, w_e_up, w_e_down) + shared).reshape(B, L, D)


def decoder_layer(x, hist, rel_bias, w_in, conv_w, conv_b, w_gate_a, b_gate_a, w_gate_x, b_gate_x, lru_lambda, sinks, w_o_attn, w_o_rnn, w_out, ln1_g, ln1_b, w_router, router_bias, w_e_gate, w_e_up, w_e_down, w_s_gate, w_s_up, w_s_down, ln2_g, ln2_b):
    B, L, _ = x.shape
    q, k, v, xr, yr, ga, gr = jnp.split(x @ w_in, IN_OFFSETS, axis=-1)
    q = q.reshape(B, L, N_HEADS, HEAD_DIM)
    k = k.reshape(B, L, N_KV_HEADS, HEAD_DIM)
    v = v.reshape(B, L, N_KV_HEADS, HEAD_DIM)
    if hist is None:
        attn = banded_window_attention(q, k, v, rel_bias, sinks)
        new_k, new_v = k[:, -WINDOW:], v[:, -WINDOW:]
        rnn, new_conv, new_h = rglru_branch(xr, yr, None, None, conv_w, conv_b, w_gate_a, b_gate_a, w_gate_x, b_gate_x, lru_lambda)
    else:
        k_cache, v_cache, conv_state, rnn_state = hist
        attn, new_k, new_v = window_decode_attention(q, k, v, k_cache, v_cache, rel_bias, sinks)
        rnn, new_conv, new_h = rglru_branch(xr, yr, conv_state, rnn_state, conv_w, conv_b, w_gate_a, b_gate_a, w_gate_x, b_gate_x, lru_lambda)
    merged = jax.nn.sigmoid(ga) * (attn @ w_o_attn) + jax.nn.sigmoid(gr) * (rnn @ w_o_rnn)
    x1 = layer_norm(ALPHA * x + merged @ w_out, ln1_g, ln1_b)
    x2 = layer_norm(ALPHA * x1 + moe(x1, w_router, router_bias, w_e_gate, w_e_up, w_e_down, w_s_gate, w_s_up, w_s_down), ln2_g, ln2_b)
    return x2, (new_k, new_v, new_conv, new_h)


def setup_inputs(seed: int = 0) -> dict:
    key = jax.random.key(seed)
    ks = iter(jax.random.split(key, 48))
    f32 = jnp.float32
    nrm = lambda shape, scale: jax.random.normal(next(ks), shape, f32) * scale
    w_cache = min(WINDOW, PAST_LEN)
    x_prompt = nrm((BATCH, SEQ, D_MODEL), 1.0)
    x_sample = nrm((DEC_BATCH, DEC_SEQ, D_MODEL), 1.0)
    cache_k = nrm((DEPTH, DEC_BATCH, w_cache, N_KV_HEADS, HEAD_DIM), 1.0)
    cache_v = nrm((DEPTH, DEC_BATCH, w_cache, N_KV_HEADS, HEAD_DIM), BETA)
    state_conv = nrm((DEPTH, DEC_BATCH, CONV_W - 1, D_RNN), 1.0)
    state_rnn = nrm((DEPTH, DEC_BATCH, D_RNN), 1.0)
    s_in = D_MODEL ** -0.5
    w_in = jnp.concatenate([
        nrm((DEPTH, D_MODEL, D_ATTN), s_in),
        nrm((DEPTH, D_MODEL, D_KV), s_in),
        nrm((DEPTH, D_MODEL, D_KV), s_in * BETA),
        nrm((DEPTH, D_MODEL, 2 * D_RNN + 2 * D_MODEL), s_in)], axis=-1)
    conv_w = nrm((DEPTH, CONV_W, D_RNN), CONV_W ** -0.5)
    conv_b = nrm((DEPTH, D_RNN), 0.01)
    w_gate_a = nrm((DEPTH, N_RNN_BLOCKS, RNN_BLOCK, RNN_BLOCK), RNN_BLOCK ** -0.5)
    b_gate_a = nrm((DEPTH, D_RNN), 0.01)
    w_gate_x = nrm((DEPTH, N_RNN_BLOCKS, RNN_BLOCK, RNN_BLOCK), RNN_BLOCK ** -0.5)
    b_gate_x = nrm((DEPTH, D_RNN), 0.01)
    a_base = jax.random.uniform(next(ks), (DEPTH, D_RNN), f32, minval=0.9, maxval=0.999) ** 0.125
    lru_lambda = jnp.log(a_base) - jnp.log1p(-a_base)
    rel_bias = nrm((N_BUCKETS, N_HEADS), 0.5)
    sinks = nrm((DEPTH, N_HEADS), 1.0)
    w_o_attn = nrm((DEPTH, D_ATTN, D_MODEL), D_ATTN ** -0.5 * BETA)
    w_o_rnn = nrm((DEPTH, D_RNN, D_MODEL), D_RNN ** -0.5 * BETA)
    w_out = nrm((DEPTH, D_MODEL, D_MODEL), s_in * BETA)
    ln1_g = 1.0 + nrm((DEPTH, D_MODEL), 0.01)
    ln1_b = nrm((DEPTH, D_MODEL), 0.01)
    w_router = nrm((DEPTH, D_MODEL, N_EXPERTS), s_in)
    router_bias = nrm((DEPTH, N_EXPERTS), 0.01)
    w_e_gate = nrm((DEPTH, N_EXPERTS, D_MODEL, D_EXPERT), s_in * BETA)
    w_e_up = nrm((DEPTH, N_EXPERTS, D_MODEL, D_EXPERT), s_in * BETA)
    w_e_down = nrm((DEPTH, N_EXPERTS, D_EXPERT, D_MODEL), D_EXPERT ** -0.5 * BETA)
    w_s_gate = nrm((DEPTH, D_MODEL, D_SHARED), s_in * BETA)
    w_s_up = nrm((DEPTH, D_MODEL, D_SHARED), s_in * BETA)
    w_s_down = nrm((DEPTH, D_SHARED, D_MODEL), D_SHARED ** -0.5 * BETA)
    ln2_g = 1.0 + nrm((DEPTH, D_MODEL), 0.01)
    ln2_b = nrm((DEPTH, D_MODEL), 0.01)
    return {'x_prompt': x_prompt, 'x_sample': x_sample, 'cache_k': cache_k, 'cache_v': cache_v,
            'state_conv': state_conv, 'state_rnn': state_rnn, 'w_in': w_in, 'conv_w': conv_w, 'conv_b': conv_b,
            'w_gate_a': w_gate_a, 'b_gate_a': b_gate_a, 'w_gate_x': w_gate_x, 'b_gate_x': b_gate_x,
            'lru_lambda': lru_lambda, 'rel_bias': rel_bias, 'sinks': sinks, 'w_o_attn': w_o_attn,
            'w_o_rnn': w_o_rnn, 'w_out': w_out, 'ln1_g': ln1_g, 'ln1_b': ln1_b, 'w_router': w_router,
            'router_bias': router_bias, 'w_e_gate': w_e_gate, 'w_e_up': w_e_up, 'w_e_down': w_e_down,
            'w_s_gate': w_s_gate, 'w_s_up': w_s_up, 'w_s_down': w_s_down, 'ln2_g': ln2_g, 'ln2_b': ln2_b}


def reference(x_prompt, x_sample, cache_k, cache_v, state_conv, state_rnn, w_in, conv_w, conv_b, w_gate_a, b_gate_a, w_gate_x, b_gate_x, lru_lambda, rel_bias, sinks, w_o_attn, w_o_rnn, w_out, ln1_g, ln1_b, w_router, router_bias, w_e_gate, w_e_up, w_e_down, w_s_gate, w_s_up, w_s_down, ln2_g, ln2_b):
    per_layer = (w_in, conv_w, conv_b, w_gate_a, b_gate_a, w_gate_x, b_gate_x, lru_lambda, sinks, w_o_attn, w_o_rnn, w_out, ln1_g, ln1_b, w_router, router_bias, w_e_gate, w_e_up, w_e_down, w_s_gate, w_s_up, w_s_down, ln2_g, ln2_b)
    y_prompt, y_sample = x_prompt, x_sample
    prompt_rows, sample_rows = [], []
    for layer in range(DEPTH):
        lw = [wt[layer] for wt in per_layer]
        y_prompt, rows_p = decoder_layer(y_prompt, None, rel_bias, *lw)
        y_sample, rows_s = decoder_layer(y_sample, (cache_k[layer], cache_v[layer], state_conv[layer], state_rnn[layer]), rel_bias, *lw)
        prompt_rows.append(rows_p)
        sample_rows.append(rows_s)
    k_p, v_p, conv_p, h_p = [jnp.stack(t) for t in zip(*prompt_rows)]
    k_s, v_s, conv_s, h_s = [jnp.stack(t) for t in zip(*sample_rows)]
    return (y_prompt, y_sample, k_p, v_p, conv_p, h_p, k_s, v_s, conv_s, h_s)
```

```python
import functools
import math

import jax
import jax.numpy as jnp
from jax import lax
from jax.experimental import pallas as pl
from jax.experimental.pallas import tpu as pltpu

F32 = jnp.float32
BF16 = jnp.bfloat16

D_MODEL = 1024
N_HEADS = 8
N_KV_HEADS = 2
HEAD_DIM = 64
GROUP = N_HEADS // N_KV_HEADS
WINDOW = 128
D_ATTN = N_HEADS * HEAD_DIM
D_KV = N_KV_HEADS * HEAD_DIM
N_BUCKETS = 32
MAX_DISTANCE = 128
D_RNN = D_MODEL
RNN_BLOCK = 256
N_RNN_BLOCKS = D_RNN // RNN_BLOCK
CONV_W = 4
RG_C = 8.0
N_EXPERTS = 256
TOP_K = 8
N_GROUPS = 8
GROUP_SIZE = N_EXPERTS // N_GROUPS
TOPK_GROUPS = 4
D_EXPERT = D_MODEL // 4
ROUTED_SCALE = 2.5
LN_EPS = 1e-5
DEPTH = 1
ALPHA = (2 * DEPTH) ** 0.25
NEG_INF = -1e30
SM_SCALE = HEAD_DIM ** -0.5

O_Q = 0
O_K = D_ATTN
O_V = O_K + D_KV
O_XR = O_V + D_KV
O_YR = O_XR + D_RNN
O_GA = O_YR + D_RNN
O_GR = O_GA + D_MODEL
D_IN = O_GR + D_MODEL

SUBLANES = 8
VMEM_LIMIT_BYTES = 56 * 1024 * 1024
EXPERT_ROWS = 128


def _params(sem):
    return pltpu.CompilerParams(dimension_semantics=sem, vmem_limit_bytes=VMEM_LIMIT_BYTES)


def _pick_tile(n, candidates):
    for c in candidates:
        if n % c == 0:
            return c
    raise ValueError(f"no tile for {n}")


def _const_spec(shape):
    nd = len(shape)
    return pl.BlockSpec(shape, lambda *_: (0,) * nd)


def _inproj_kernel(x_ref, w_ref, q_ref, k_ref, v_ref, xr_ref, gy_ref, sga_ref, sgr_ref):
    x = x_ref[...].astype(BF16)

    def seg(lo, hi):
        return jnp.dot(x, w_ref[:, lo:hi], preferred_element_type=F32)

    q_ref[...] = seg(O_Q, O_K).astype(BF16)
    k_ref[...] = seg(O_K, O_V)
    v_ref[...] = seg(O_V, O_XR)
    xr_ref[...] = seg(O_XR, O_YR)
    gy_ref[...] = jax.nn.gelu(seg(O_YR, O_GA))
    sga_ref[...] = jax.nn.sigmoid(seg(O_GA, O_GR))
    sgr_ref[...] = jax.nn.sigmoid(seg(O_GR, D_IN))


def _inproj(x_all, w_in_bf16):
    t = x_all.shape[0]
    tm = _pick_tile(t, (256, 128, 64, 32, 16, 8))
    row = lambda width: pl.BlockSpec((tm, width), lambda i: (i, 0))
    out_shape = (
        jax.ShapeDtypeStruct((t, D_ATTN), BF16),
        jax.ShapeDtypeStruct((t, D_KV), F32),
        jax.ShapeDtypeStruct((t, D_KV), F32),
        jax.ShapeDtypeStruct((t, D_RNN), F32),
        jax.ShapeDtypeStruct((t, D_RNN), F32),
        jax.ShapeDtypeStruct((t, D_MODEL), F32),
        jax.ShapeDtypeStruct((t, D_MODEL), F32),
    )
    return pl.pallas_call(
        _inproj_kernel,
        grid=(t // tm,),
        in_specs=[row(D_MODEL), _const_spec((D_MODEL, D_IN))],
        out_specs=(row(D_ATTN), row(D_KV), row(D_KV), row(D_RNN), row(D_RNN), row(D_MODEL), row(D_MODEL)),
        out_shape=out_shape,
        compiler_params=_params(("parallel",)),
        name="inproj",
    )(x_all, w_in_bf16)


def _t5_bucket(dist):
    n = jnp.maximum(dist, 0)
    max_exact = N_BUCKETS // 2
    nf = jnp.maximum(n, 1).astype(F32)
    large = max_exact + (jnp.log(nf / max_exact) / math.log(MAX_DISTANCE / max_exact) * (N_BUCKETS - max_exact)).astype(jnp.int32)
    large = jnp.minimum(large, N_BUCKETS - 1)
    return jnp.where(n < max_exact, n, large)


def _bias_table(rel_bias, dist):
    return jnp.moveaxis(rel_bias[_t5_bucket(dist)].astype(F32), -1, 0)


def _softmax_pv(s, sink, v):
    m = jnp.maximum(jnp.max(s, axis=-1, keepdims=True), sink)
    p = jnp.exp(s - m)
    denom = jnp.sum(p, axis=-1, keepdims=True) + jnp.exp(sink - m)
    return jnp.dot(p.astype(BF16), v, preferred_element_type=F32), denom


def _attn_prompt_kernel(sink_ref, q_ref, kc_ref, kp_ref, vc_ref, vp_ref, bias_ref, o_ref):
    n = pl.program_id(1)
    kk = jnp.concatenate([kp_ref[...], kc_ref[...]], axis=0).astype(BF16)
    vv = jnp.concatenate([vp_ref[...], vc_ref[...]], axis=0).astype(BF16)
    q = q_ref[...]
    rows = lax.broadcasted_iota(jnp.int32, (WINDOW, 2 * WINDOW), 0)
    cols = lax.broadcasted_iota(jnp.int32, (WINDOW, 2 * WINDOW), 1)
    dist = rows + WINDOW - cols
    valid = (dist >= 0) & (dist <= WINDOW) & ((n > 0) | (cols >= WINDOW))
    for g in range(N_KV_HEADS):
        kg = kk[:, g * HEAD_DIM:(g + 1) * HEAD_DIM]
        vg = vv[:, g * HEAD_DIM:(g + 1) * HEAD_DIM]
        for h in range(GROUP):
            hh = g * GROUP + h
            qh = q[:, hh * HEAD_DIM:(hh + 1) * HEAD_DIM]
            s = lax.dot_general(qh, kg, (((1,), (1,)), ((), ())), preferred_element_type=F32) * SM_SCALE
            s = jnp.where(valid, s + bias_ref[hh], NEG_INF)
            o, denom = _softmax_pv(s, sink_ref[0, hh], vg)
            o_ref[:, hh * HEAD_DIM:(hh + 1) * HEAD_DIM] = (o / denom).astype(BF16)


def _attn_prompt(q_all, k_all, v_all, bias, sinks, batch, seq):
    t_all = q_all.shape[0]
    nb = seq // WINDOW
    cur = lambda width: pl.BlockSpec((WINDOW, width), lambda b, n: (b * nb + n, 0))
    prev = lambda width: pl.BlockSpec((WINDOW, width), lambda b, n: (b * nb + jnp.maximum(n - 1, 0), 0))
    return pl.pallas_call(
        _attn_prompt_kernel,
        grid=(batch, nb),
        in_specs=[
            pl.BlockSpec(memory_space=pltpu.SMEM),
            cur(D_ATTN), cur(D_KV), prev(D_KV), cur(D_KV), prev(D_KV),
            _const_spec((N_HEADS, WINDOW, 2 * WINDOW)),
        ],
        out_specs=cur(D_ATTN),
        out_shape=jax.ShapeDtypeStruct((t_all, D_ATTN), BF16),
        compiler_params=_params(("parallel", "arbitrary")),
        name="attn_prompt",
    )(sinks.reshape(1, N_HEADS).astype(F32), q_all, k_all, k_all, v_all, v_all, bias)


def _attn_sample_kernel(q_ref, kn_ref, vn_ref, kc_ref, vc_ref, bc_ref, bn_ref, sink_ref, prev_o_ref,
                        o_ref, ko_ref, vo_ref, *, seqs, s_len):
    del prev_o_ref
    w = kc_ref.shape[1]
    rows_c = lax.broadcasted_iota(jnp.int32, (GROUP * s_len, w), 0) % s_len
    cols_c = lax.broadcasted_iota(jnp.int32, (GROUP * s_len, w), 1)
    dist_c = rows_c + w - cols_c
    valid_c = (dist_c >= 0) & (dist_c <= WINDOW)
    rows_n = lax.broadcasted_iota(jnp.int32, (GROUP * s_len, s_len), 0) % s_len
    cols_n = lax.broadcasted_iota(jnp.int32, (GROUP * s_len, s_len), 1)
    dist_n = rows_n - cols_n
    valid_n = (dist_n >= 0) & (dist_n <= WINDOW)
    for j in range(seqs):
        r0 = j * s_len
        qj = q_ref[r0:r0 + s_len, :]
        kc = kc_ref[j]
        vc = vc_ref[j]
        kn = kn_ref[r0:r0 + s_len, :]
        vn = vn_ref[r0:r0 + s_len, :]
        ko_ref[j, 0:w - s_len, :] = kc[s_len:, :]
        ko_ref[j, w - s_len:w, :] = kn
        vo_ref[j, 0:w - s_len, :] = vc[s_len:, :]
        vo_ref[j, w - s_len:w, :] = vn
        kcb, vcb, knb, vnb = kc.astype(BF16), vc.astype(BF16), kn.astype(BF16), vn.astype(BF16)
        for g in range(N_KV_HEADS):
            lo, hi = g * HEAD_DIM, (g + 1) * HEAD_DIM
            qs = jnp.concatenate(
                [qj[:, (g * GROUP + h) * HEAD_DIM:(g * GROUP + h + 1) * HEAD_DIM] for h in range(GROUP)], axis=0)
            nt = (((1,), (1,)), ((), ()))
            s_c = lax.dot_general(qs, kcb[:, lo:hi], nt, preferred_element_type=F32) * SM_SCALE
            s_n = lax.dot_general(qs, knb[:, lo:hi], nt, preferred_element_type=F32) * SM_SCALE
            s_c = jnp.where(valid_c, s_c + bc_ref[g], NEG_INF)
            s_n = jnp.where(valid_n, s_n + bn_ref[g], NEG_INF)
            sink = sink_ref[g]
            m = jnp.maximum(jnp.maximum(jnp.max(s_c, axis=-1, keepdims=True), jnp.max(s_n, axis=-1, keepdims=True)), sink)
            p_c = jnp.exp(s_c - m)
            p_n = jnp.exp(s_n - m)
            denom = jnp.sum(p_c, axis=-1, keepdims=True) + jnp.sum(p_n, axis=-1, keepdims=True) + jnp.exp(sink - m)
            o = jnp.dot(p_c.astype(BF16), vcb[:, lo:hi], preferred_element_type=F32)
            o = o + jnp.dot(p_n.astype(BF16), vnb[:, lo:hi], preferred_element_type=F32)
            o = (o / denom).astype(BF16)
            for h in range(GROUP):
                hh = g * GROUP + h
                o_ref[r0:r0 + s_len, hh * HEAD_DIM:(hh + 1) * HEAD_DIM] = o[h * s_len:(h + 1) * s_len, :]


def _attn_sample(q_all, k_all, v_all, cache_k, cache_v, rel_bias, sinks, attn_o, row0, dec_batch, s_len):
    w = cache_k.shape[1]
    seqs = _pick_tile(dec_batch, (16, 8, 4, 2, 1))
    rows = seqs * s_len
    blk0 = row0 // rows
    assert row0 % rows == 0
    qi = jnp.arange(s_len)
    dist_c = qi[:, None] + w - jnp.arange(w)[None, :]
    dist_n = qi[:, None] - jnp.arange(s_len)[None, :]
    b_c = _bias_table(rel_bias, dist_c).reshape(N_KV_HEADS, GROUP * s_len, w)
    b_n = _bias_table(rel_bias, dist_n).reshape(N_KV_HEADS, GROUP * s_len, s_len)
    sink = jnp.broadcast_to(sinks.astype(F32).reshape(N_KV_HEADS, GROUP, 1, 1), (N_KV_HEADS, GROUP, s_len, 1))
    sink = sink.reshape(N_KV_HEADS, GROUP * s_len, 1)
    tok = lambda width: pl.BlockSpec((rows, width), lambda i: (blk0 + i, 0))
    cache = pl.BlockSpec((seqs, w, D_KV), lambda i: (i, 0, 0))
    t_all = q_all.shape[0]
    return pl.pallas_call(
        functools.partial(_attn_sample_kernel, seqs=seqs, s_len=s_len),
        grid=(dec_batch // seqs,),
        in_specs=[
            tok(D_ATTN), tok(D_KV), tok(D_KV), cache, cache,
            _const_spec(b_c.shape), _const_spec(b_n.shape), _const_spec(sink.shape),
            pl.BlockSpec(memory_space=pl.ANY),
        ],
        out_specs=(tok(D_ATTN), cache, cache),
        out_shape=(
            jax.ShapeDtypeStruct((t_all, D_ATTN), BF16),
            jax.ShapeDtypeStruct((dec_batch, w, D_KV), F32),
            jax.ShapeDtypeStruct((dec_batch, w, D_KV), F32),
        ),
        input_output_aliases={8: 0},
        compiler_params=_params(("parallel",)),
        name="attn_sample",
    )(q_all, k_all, v_all, cache_k, cache_v, b_c, b_n, sink, attn_o)


def _softplus(z):
    return jnp.maximum(z, 0.0) + jnp.log1p(jnp.exp(-jnp.abs(z)))


def _block_gate(xcb, w_ref, b_ref):
    parts = [jnp.dot(xcb[:, n * RNN_BLOCK:(n + 1) * RNN_BLOCK], w_ref[n], preferred_element_type=F32)
             for n in range(N_RNN_BLOCKS)]
    return jax.nn.sigmoid(jnp.concatenate(parts, axis=-1) + b_ref[...])


def _lru_coeffs(xc, wa_ref, ba_ref, wx_ref, bx_ref, lam_ref, first_row_unnormalised):
    xcb = xc.astype(BF16)
    r = _block_gate(xcb, wa_ref, ba_ref)
    i = _block_gate(xcb, wx_ref, bx_ref)
    log_a = -RG_C * r * _softplus(-lam_ref[...])
    a = jnp.exp(log_a)
    mult = jnp.sqrt(-jnp.tanh(log_a) * (a * a + 1.0))
    if first_row_unnormalised is not None:
        mult = jnp.where(first_row_unnormalised, 1.0, mult)
    return a, mult * i * xc


def _scan8(a, b):
    r8 = lax.broadcasted_iota(jnp.int32, a.shape, 0) % SUBLANES
    d = 1
    while d < SUBLANES:
        keep = r8 >= d
        a_sh = jnp.where(keep, pltpu.roll(a, d, 0), 1.0)
        b_sh = jnp.where(keep, pltpu.roll(b, d, 0), 0.0)
        b = a * b_sh + b
        a = a * a_sh
        d *= 2
    return a, b


def _rnn_prompt_kernel(xr_ref, gy_ref, cw_ref, cb_ref, wa_ref, ba_ref, wx_ref, bx_ref, lam_ref,
                       o_ref, nh_ref, ext_ref, a_ref, b_ref, hc_ref, *, tl):
    l = pl.program_id(1)

    @pl.when(l == 0)
    def _():
        ext_ref[0:SUBLANES, :] = jnp.zeros((SUBLANES, D_RNN), F32)
        hc_ref[...] = jnp.zeros((1, D_RNN), F32)

    x = xr_ref[...]
    ext_ref[SUBLANES:, :] = x
    xc = cb_ref[...] + cw_ref[CONV_W - 1:CONV_W, :] * x
    for j in range(1, CONV_W):
        xc = xc + cw_ref[CONV_W - 1 - j:CONV_W - j, :] * ext_ref[SUBLANES - j:SUBLANES - j + tl, :]
    ext_ref[0:SUBLANES, :] = x[tl - SUBLANES:, :]

    row = lax.broadcasted_iota(jnp.int32, (tl, D_RNN), 0)
    a, b = _lru_coeffs(xc, wa_ref, ba_ref, wx_ref, bx_ref, lam_ref, (row == 0) & (l == 0))
    a, b = _scan8(a, b)
    a_ref[...] = a
    b_ref[...] = b

    def chunk(c, h):
        sl = pl.ds(pl.multiple_of(c * SUBLANES, SUBLANES), SUBLANES)
        hc = b_ref[sl, :] + a_ref[sl, :] * h
        b_ref[sl, :] = hc
        return hc[SUBLANES - 1:SUBLANES, :]

    h = lax.fori_loop(0, tl // SUBLANES, chunk, hc_ref[...])
    hc_ref[...] = h
    nh_ref[0] = h
    o_ref[...] = (b_ref[...] * gy_ref[...]).astype(BF16)


def _rnn_prompt(xr_all, gy_all, rnn_w, batch, seq):
    t_all = xr_all.shape[0]
    tl = _pick_tile(seq, (256, 128, 64, 32, 16, 8))
    nl = seq // tl
    tok = pl.BlockSpec((tl, D_RNN), lambda b, l: (b * nl + l, 0))
    return pl.pallas_call(
        functools.partial(_rnn_prompt_kernel, tl=tl),
        grid=(batch, nl),
        in_specs=[tok, tok] + [_const_spec(w.shape) for w in rnn_w],
        out_specs=(tok, pl.BlockSpec((1, 1, D_RNN), lambda b, l: (b, 0, 0))),
        out_shape=(jax.ShapeDtypeStruct((t_all, D_RNN), BF16), jax.ShapeDtypeStruct((batch, 1, D_RNN), F32)),
        scratch_shapes=[
            pltpu.VMEM((tl + SUBLANES, D_RNN), F32),
            pltpu.VMEM((tl, D_RNN), F32),
            pltpu.VMEM((tl, D_RNN), F32),
            pltpu.VMEM((1, D_RNN), F32),
        ],
        compiler_params=_params(("parallel", "arbitrary")),
        name="rnn_prompt",
    )(xr_all, gy_all, *rnn_w)


def _rnn_sample_kernel(xr_ref, gy_ref, hp_ref, h0_ref, cw_ref, cb_ref, wa_ref, ba_ref, wx_ref, bx_ref, lam_ref,
                       prev_o_ref, o_ref, nh_ref, *, seqs):
    del prev_o_ref
    rows = seqs * SUBLANES
    x = xr_ref[...]
    hp = hp_ref[...]
    r8 = lax.broadcasted_iota(jnp.int32, (rows, D_RNN), 0) % SUBLANES
    xc = cb_ref[...] + cw_ref[CONV_W - 1:CONV_W, :] * x
    for j in range(1, CONV_W):
        shifted = jnp.where(r8 >= j, pltpu.roll(x, j, 0), pltpu.roll(hp, rows - (SUBLANES - j), 0))
        xc = xc + cw_ref[CONV_W - 1 - j:CONV_W - j, :] * shifted
    a, b = _lru_coeffs(xc, wa_ref, ba_ref, wx_ref, bx_ref, lam_ref, None)
    a, b = _scan8(a, b)
    h0 = jnp.broadcast_to(h0_ref[...][:, None, :], (seqs, SUBLANES, D_RNN)).reshape(rows, D_RNN)
    h = b + a * h0
    last = jnp.where(r8 == SUBLANES - 1, h, 0.0).reshape(seqs, SUBLANES, D_RNN)
    nh_ref[...] = jnp.sum(last, axis=1)
    o_ref[...] = (h * gy_ref[...]).astype(BF16)


def _rnn_sample(xr_all, gy_all, hist_pad, h0, rnn_w, rnn_o, row0, dec_batch):
    t_all = xr_all.shape[0]
    seqs = _pick_tile(dec_batch, (16, 8))
    rows = seqs * SUBLANES
    assert row0 % rows == 0
    blk0 = row0 // rows
    tok = pl.BlockSpec((rows, D_RNN), lambda i: (blk0 + i, 0))
    return pl.pallas_call(
        functools.partial(_rnn_sample_kernel, seqs=seqs),
        grid=(dec_batch // seqs,),
        in_specs=[tok, tok, pl.BlockSpec((rows, D_RNN), lambda i: (i, 0)), pl.BlockSpec((seqs, D_RNN), lambda i: (i, 0))]
        + [_const_spec(w.shape) for w in rnn_w] + [pl.BlockSpec(memory_space=pl.ANY)],
        out_specs=(tok, pl.BlockSpec((seqs, D_RNN), lambda i: (i, 0))),
        out_shape=(jax.ShapeDtypeStruct((t_all, D_RNN), BF16), jax.ShapeDtypeStruct((dec_batch, D_RNN), F32)),
        input_output_aliases={11: 0},
        compiler_params=_params(("parallel",)),
        name="rnn_sample",
    )(xr_all, gy_all, hist_pad, h0, *rnn_w, rnn_o)


def _layer_norm(z, g, b):
    mu = jnp.mean(z, axis=-1, keepdims=True)
    zc = z - mu
    var = jnp.mean(zc * zc, axis=-1, keepdims=True)
    return zc * lax.rsqrt(var + LN_EPS) * g + b


def _first_index_of_max(vals, iota, axis, sentinel):
    mx = jnp.max(vals, axis=axis, keepdims=True)
    return mx, jnp.min(jnp.where(vals == mx, iota, sentinel), axis=axis, keepdims=True)


def _route(scores, bias):
    t = scores.shape[1]
    grp = scores + bias
    g3 = grp.reshape(N_GROUPS, GROUP_SIZE, t)
    e_in_g = lax.broadcasted_iota(jnp.int32, g3.shape, 1)
    m1, first = _first_index_of_max(g3, e_in_g, 1, GROUP_SIZE)
    m2 = jnp.max(jnp.where(e_in_g == first, -jnp.inf, g3), axis=1, keepdims=True)
    gscore = (m1 + m2).reshape(N_GROUPS, t)
    g_iota = lax.broadcasted_iota(jnp.int32, gscore.shape, 0)
    gmask = jnp.zeros(gscore.shape, jnp.bool_)
    for _ in range(TOPK_GROUPS):
        _, gi = _first_index_of_max(gscore, g_iota, 0, N_GROUPS)
        hit = g_iota == gi
        gmask = gmask | hit
        gscore = jnp.where(hit, -jnp.inf, gscore)
    masked = jnp.where(gmask[:, None, :], g3, -jnp.inf).reshape(N_EXPERTS, t)
    e_iota = lax.broadcasted_iota(jnp.int32, masked.shape, 0)
    idx, wts = [], []
    for _ in range(TOP_K):
        _, ei = _first_index_of_max(masked, e_iota, 0, N_EXPERTS)
        hit = e_iota == ei
        idx.append(ei)
        wts.append(jnp.sum(jnp.where(hit, scores, 0.0), axis=0, keepdims=True))
        masked = jnp.where(hit, -jnp.inf, masked)
    idx = jnp.concatenate(idx, axis=0)
    w = jnp.concatenate(wts, axis=0)
    w = w / jnp.sum(w, axis=0, keepdims=True) * ROUTED_SCALE
    return idx, w


def _merge_kernel(x_ref, ao_ref, ro_ref, sga_ref, sgr_ref, woa_ref, wor_ref, wout_ref, g1_ref, b1_ref,
                  wrt_ref, rb_ref, wsg_ref, wsu_ref, wsd_ref,
                  x1b_ref, base_ref, idx_ref, wt_ref):
    pa = jnp.dot(ao_ref[...], woa_ref[...], preferred_element_type=F32)
    pr = jnp.dot(ro_ref[...], wor_ref[...], preferred_element_type=F32)
    merged = sga_ref[...] * pa + sgr_ref[...] * pr
    z = ALPHA * x_ref[...] + jnp.dot(merged.astype(BF16), wout_ref[...], preferred_element_type=F32)
    x1 = _layer_norm(z, g1_ref[...], b1_ref[...])
    x1b = x1.astype(BF16)
    x1b_ref[...] = x1b
    u = jax.nn.silu(jnp.dot(x1b, wsg_ref[...], preferred_element_type=F32)) * jnp.dot(x1b, wsu_ref[...], preferred_element_type=F32)
    shared = jnp.dot(u.astype(BF16), wsd_ref[...], preferred_element_type=F32)
    base_ref[...] = ALPHA * x1 + shared
    logits = lax.dot_general(wrt_ref[...], x1, (((1,), (1,)), ((), ())), preferred_element_type=F32)
    idx, w = _route(jax.nn.sigmoid(logits), rb_ref[...])
    idx_ref[...] = idx
    wt_ref[...] = w


def _merge(x_all, attn_o, rnn_o, sga, sgr, weights):
    t = x_all.shape[0]
    tm = _pick_tile(t, (256, 128))
    row = lambda width: pl.BlockSpec((tm, width), lambda i: (i, 0))
    col = pl.BlockSpec((TOP_K, tm), lambda i: (0, i))
    return pl.pallas_call(
        _merge_kernel,
        grid=(t // tm,),
        in_specs=[row(D_MODEL), row(D_ATTN), row(D_RNN), row(D_MODEL), row(D_MODEL)]
        + [_const_spec(w.shape) for w in weights],
        out_specs=(row(D_MODEL), row(D_MODEL), col, col),
        out_shape=(
            jax.ShapeDtypeStruct((t, D_MODEL), BF16),
            jax.ShapeDtypeStruct((t, D_MODEL), F32),
            jax.ShapeDtypeStruct((TOP_K, t), jnp.int32),
            jax.ShapeDtypeStruct((TOP_K, t), F32),
        ),
        compiler_params=_params(("parallel",)),
        name="merge_ln1_route",
    )(x_all, attn_o, rnn_o, sga, sgr, *weights)


def _expert_kernel(be_ref, bu_ref, xs_ref, rw_ref, wg_ref, wu_ref, wd_ref, o_ref):
    del be_ref
    i = pl.program_id(0)

    @pl.when(bu_ref[i] != 0)
    def _():
        xb = xs_ref[...]
        g = jnp.dot(xb, wg_ref[...].astype(BF16), preferred_element_type=F32)
        u = jnp.dot(xb, wu_ref[...].astype(BF16), preferred_element_type=F32)
        act = (jax.nn.silu(g) * u).astype(BF16)
        o_ref[...] = jnp.dot(act, wd_ref[...].astype(BF16), preferred_element_type=F32) * rw_ref[...]

    @pl.when(bu_ref[i] == 0)
    def _():
        o_ref[...] = jnp.zeros(o_ref.shape, F32)


def _expert_ffn(xs, row_w, block_e, block_used, w_e_gate, w_e_up, w_e_down):
    r = xs.shape[0]
    n_blocks = r // EXPERT_ROWS
    grid_spec = pltpu.PrefetchScalarGridSpec(
        num_scalar_prefetch=2,
        grid=(n_blocks,),
        in_specs=[
            pl.BlockSpec((EXPERT_ROWS, D_MODEL), lambda i, be, bu: (i, 0)),
            pl.BlockSpec((EXPERT_ROWS, 1), lambda i, be, bu: (i, 0)),
            pl.BlockSpec((None, D_MODEL, D_EXPERT), lambda i, be, bu: (be[i], 0, 0)),
            pl.BlockSpec((None, D_MODEL, D_EXPERT), lambda i, be, bu: (be[i], 0, 0)),
            pl.BlockSpec((None, D_EXPERT, D_MODEL), lambda i, be, bu: (be[i], 0, 0)),
        ],
        out_specs=pl.BlockSpec((EXPERT_ROWS, D_MODEL), lambda i, be, bu: (i, 0)),
    )
    return pl.pallas_call(
        _expert_kernel,
        grid_spec=grid_spec,
        out_shape=jax.ShapeDtypeStruct((r, D_MODEL), F32),
        compiler_params=_params(("arbitrary",)),
        name="expert_ffn",
    )(block_e, block_used, xs, row_w, w_e_gate, w_e_up, w_e_down)


def _dispatch_plan(idx, w):
    t = idx.shape[0]
    n = t * TOP_K
    flat_e = idx.reshape(n)
    flat_tok = jnp.repeat(jnp.arange(t, dtype=jnp.int32), TOP_K)
    flat_w = w.reshape(n)
    order = jnp.argsort(flat_e)
    sorted_e = flat_e[order]
    counts = jnp.bincount(flat_e, length=N_EXPERTS)
    padded = (counts + EXPERT_ROWS - 1) // EXPERT_ROWS * EXPERT_ROWS
    start = jnp.cumsum(counts) - counts
    pend = jnp.cumsum(padded)
    pstart = pend - padded
    dest = pstart[sorted_e] + jnp.arange(n, dtype=jnp.int32) - start[sorted_e]
    n_blocks = (n + N_EXPERTS * (EXPERT_ROWS - 1) + EXPERT_ROWS - 1) // EXPERT_ROWS
    r = n_blocks * EXPERT_ROWS
    row_tok = jnp.full((r,), t, jnp.int32).at[dest].set(flat_tok[order])
    row_w = jnp.zeros((r,), F32).at[dest].set(flat_w[order])
    block_start = jnp.arange(n_blocks, dtype=jnp.int32) * EXPERT_ROWS
    block_e = jnp.minimum(jnp.searchsorted(pend, block_start, side='right'), N_EXPERTS - 1).astype(jnp.int32)
    block_used = (block_start < pend[-1]).astype(jnp.int32)
    return row_tok, row_w, block_e, block_used


def _final_kernel(base_ref, y_ref, g_ref, b_ref, o_ref):
    o_ref[...] = _layer_norm(base_ref[...] + y_ref[...], g_ref[...], b_ref[...])


def _final(base, routed, g2, b2):
    t = base.shape[0]
    tm = _pick_tile(t, (512, 256, 128))
    row = pl.BlockSpec((tm, D_MODEL), lambda i: (i, 0))
    return pl.pallas_call(
        _final_kernel,
        grid=(t // tm,),
        in_specs=[row, row, _const_spec((1, D_MODEL)), _const_spec((1, D_MODEL))],
        out_specs=row,
        out_shape=jax.ShapeDtypeStruct((t, D_MODEL), F32),
        compiler_params=_params(("parallel",)),
        name="combine_ln2",
    )(base, routed, g2, b2)


def kernel(x_prompt, x_sample, cache_k, cache_v, state_conv, state_rnn, w_in, conv_w, conv_b, w_gate_a, b_gate_a, w_gate_x, b_gate_x, lru_lambda, rel_bias, sinks, w_o_attn, w_o_rnn, w_out, ln1_g, ln1_b, w_router, router_bias, w_e_gate, w_e_up, w_e_down, w_s_gate, w_s_up, w_s_down, ln2_g, ln2_b):
    assert w_in.shape[0] == DEPTH == 1
    batch, seq, _ = x_prompt.shape
    dec_batch, s_len, _ = x_sample.shape
    w_cache = cache_k.shape[2]
    assert s_len == SUBLANES and seq % WINDOW == 0 and w_cache == WINDOW
    t_p = batch * seq
    t_s = dec_batch * s_len
    vec = lambda a: a[0].reshape(1, -1).astype(F32)

    x_all = jnp.concatenate([x_prompt.reshape(t_p, D_MODEL), x_sample.reshape(t_s, D_MODEL)], axis=0)
    q, k, v, xr, gy, sga, sgr = _inproj(x_all, w_in[0].astype(BF16))

    qi = jnp.arange(WINDOW)
    dist = qi[:, None] + WINDOW - jnp.arange(2 * WINDOW)[None, :]
    bias_p = _bias_table(rel_bias, dist)
    attn_o = _attn_prompt(q, k, v, bias_p, sinks[0], batch, seq)
    attn_o, k_s, v_s = _attn_sample(
        q, k, v, cache_k[0].reshape(dec_batch, w_cache, D_KV), cache_v[0].reshape(dec_batch, w_cache, D_KV),
        rel_bias, sinks[0], attn_o, t_p, dec_batch, s_len)

    rnn_w = (conv_w[0], vec(conv_b), w_gate_a[0].astype(BF16), vec(b_gate_a), w_gate_x[0].astype(BF16),
             vec(b_gate_x), vec(lru_lambda))
    rnn_o, h_p = _rnn_prompt(xr, gy, rnn_w, batch, seq)
    hist_pad = jnp.pad(state_conv[0], ((0, 0), (SUBLANES - (CONV_W - 1), 0), (0, 0))).reshape(t_s, D_RNN)
    rnn_o, h_s = _rnn_sample(xr, gy, hist_pad, state_rnn[0], rnn_w, rnn_o, t_p, dec_batch)

    merge_w = (w_o_attn[0].astype(BF16), w_o_rnn[0].astype(BF16), w_out[0].astype(BF16), vec(ln1_g), vec(ln1_b),
               w_router[0].T, router_bias[0].reshape(N_EXPERTS, 1), w_s_gate[0].astype(BF16),
               w_s_up[0].astype(BF16), w_s_down[0].astype(BF16))
    x1b, base, idx_t, wt_t = _merge(x_all, attn_o, rnn_o, sga, sgr, merge_w)

    t_all = t_p + t_s
    row_tok, row_w, block_e, block_used = _dispatch_plan(idx_t.T, wt_t.T)
    x1b_pad = jnp.concatenate([x1b, jnp.zeros((1, D_MODEL), BF16)], axis=0)
    xs = x1b_pad[row_tok]
    out_sorted = _expert_ffn(xs, row_w.reshape(-1, 1), block_e, block_used, w_e_gate[0], w_e_up[0], w_e_down[0])
    routed = jnp.zeros((t_all + 1, D_MODEL), F32).at[row_tok].add(out_sorted)[:t_all]

    y = _final(base, routed, vec(ln2_g), vec(ln2_b))

    y_p = y[:t_p].reshape(batch, seq, D_MODEL)
    y_s = y[t_p:].reshape(dec_batch, s_len, D_MODEL)
    kv5 = lambda a, b: a.reshape(1, b, WINDOW, N_KV_HEADS, HEAD_DIM)
    k_p = kv5(k[:t_p].reshape(batch, seq, D_KV)[:, seq - WINDOW:], batch)
    v_p = kv5(v[:t_p].reshape(batch, seq, D_KV)[:, seq - WINDOW:], batch)
    conv_p = xr[:t_p].reshape(batch, seq, D_RNN)[:, seq - (CONV_W - 1):][None]
    conv_s = xr[t_p:].reshape(dec_batch, s_len, D_RNN)[:, s_len - (CONV_W - 1):][None]
    return (y_p, y_s, k_p, v_p, conv_p, h_p.reshape(1, batch, D_RNN),
            kv5(k_s, dec_batch), kv5(v_s, dec_batch), conv_s, h_s.reshape(1, dec_batch, D_RNN))
```

```python
import functools
import math

import jax
import jax.numpy as jnp
from jax import lax
from jax.experimental import pallas as pl
from jax.experimental.pallas import tpu as pltpu

F32 = jnp.float32
BF16 = jnp.bfloat16

D_MODEL = 1024
N_HEADS = 8
N_KV_HEADS = 2
HEAD_DIM = 64
GROUP = N_HEADS // N_KV_HEADS
WINDOW = 128
D_ATTN = N_HEADS * HEAD_DIM
D_KV = N_KV_HEADS * HEAD_DIM
N_BUCKETS = 32
MAX_DISTANCE = 128
D_RNN = D_MODEL
RNN_BLOCK = 256
N_RNN_BLOCKS = D_RNN // RNN_BLOCK
CONV_W = 4
RG_C = 8.0
N_EXPERTS = 256
TOP_K = 8
N_GROUPS = 8
GROUP_SIZE = N_EXPERTS // N_GROUPS
TOPK_GROUPS = 4
D_EXPERT = D_MODEL // 4
ROUTED_SCALE = 2.5
LN_EPS = 1e-5
DEPTH = 1
ALPHA = (2 * DEPTH) ** 0.25
NEG_INF = -1e30
SM_SCALE = HEAD_DIM ** -0.5

O_Q = 0
O_K = D_ATTN
O_V = O_K + D_KV
O_XR = O_V + D_KV
O_YR = O_XR + D_RNN
O_GA = O_YR + D_RNN
O_GR = O_GA + D_MODEL
D_IN = O_GR + D_MODEL

SUBLANES = 8
VMEM_LIMIT_BYTES = 56 * 1024 * 1024
EXPERT_ROWS = 128


def _params(sem):
    return pltpu.CompilerParams(dimension_semantics=sem, vmem_limit_bytes=VMEM_LIMIT_BYTES)


def _pick_tile(n, candidates):
    for c in candidates:
        if n % c == 0:
            return c
    raise ValueError(f"no tile for {n}")


def _const_spec(shape):
    nd = len(shape)
    return pl.BlockSpec(shape, lambda *_: (0,) * nd)


LANES = 128
ROW_TILE = D_MODEL // LANES
assert ROW_TILE == SUBLANES


def _store_token_rows(ref, mat, n, row0=0):
    for s in range(ROW_TILE):
        ref[pl.ds(row0 * ROW_TILE + s, n, stride=ROW_TILE), :] = mat[:, s * LANES:(s + 1) * LANES]


def _load_token_rows(ref, n, row0=0):
    return [ref[pl.ds(row0 * ROW_TILE + s, n, stride=ROW_TILE), :] for s in range(ROW_TILE)]


def _two_source_specs(t_p, t_s, tm, width):
    n_p = t_p // tm
    assert t_p % tm == 0 and t_s % tm == 0
    return (pl.BlockSpec((tm, width), lambda i, *_: (jnp.minimum(i, n_p - 1), 0)),
            pl.BlockSpec((tm, width), lambda i, *_: (jnp.maximum(i - n_p, 0), 0)))


def _inproj_kernel(xp_ref, xs_ref, w_ref, q_ref, k_ref, v_ref, xr_ref, gy_ref, sga_ref, sgr_ref, *, n_p):
    x = jnp.where(pl.program_id(0) < n_p, xp_ref[...], xs_ref[...]).astype(BF16)

    def seg(lo, hi):
        return jnp.dot(x, w_ref[:, lo:hi], preferred_element_type=F32)

    q_ref[...] = seg(O_Q, O_K).astype(BF16)
    k_ref[...] = seg(O_K, O_V)
    v_ref[...] = seg(O_V, O_XR)
    xr_ref[...] = seg(O_XR, O_YR)
    gy_ref[...] = jax.nn.gelu(seg(O_YR, O_GA))
    sga_ref[...] = jax.nn.sigmoid(seg(O_GA, O_GR))
    sgr_ref[...] = jax.nn.sigmoid(seg(O_GR, D_IN))


def _inproj(x_p, x_s, w_in_bf16):
    t_p, t_s = x_p.shape[0], x_s.shape[0]
    t = t_p + t_s
    tm = _pick_tile(math.gcd(t_p, t_s), (256, 128, 64, 32, 16, 8))
    row = lambda width: pl.BlockSpec((tm, width), lambda i: (i, 0))
    out_shape = (
        jax.ShapeDtypeStruct((t, D_ATTN), BF16),
        jax.ShapeDtypeStruct((t, D_KV), F32),
        jax.ShapeDtypeStruct((t, D_KV), F32),
        jax.ShapeDtypeStruct((t, D_RNN), F32),
        jax.ShapeDtypeStruct((t, D_RNN), F32),
        jax.ShapeDtypeStruct((t, D_MODEL), F32),
        jax.ShapeDtypeStruct((t, D_MODEL), F32),
    )
    return pl.pallas_call(
        functools.partial(_inproj_kernel, n_p=t_p // tm),
        grid=(t // tm,),
        in_specs=[*_two_source_specs(t_p, t_s, tm, D_MODEL), _const_spec((D_MODEL, D_IN))],
        out_specs=(row(D_ATTN), row(D_KV), row(D_KV), row(D_RNN), row(D_RNN), row(D_MODEL), row(D_MODEL)),
        out_shape=out_shape,
        compiler_params=_params(("parallel",)),
        name="inproj",
    )(x_p, x_s, w_in_bf16)


def _t5_bucket(dist):
    n = jnp.maximum(dist, 0)
    max_exact = N_BUCKETS // 2
    nf = jnp.maximum(n, 1).astype(F32)
    large = max_exact + (jnp.log(nf / max_exact) / math.log(MAX_DISTANCE / max_exact) * (N_BUCKETS - max_exact)).astype(jnp.int32)
    large = jnp.minimum(large, N_BUCKETS - 1)
    return jnp.where(n < max_exact, n, large)


def _bias_table(rel_bias, dist):
    return jnp.moveaxis(rel_bias[_t5_bucket(dist)].astype(F32), -1, 0)


def _softmax_pv(s, sink, v):
    m = jnp.maximum(jnp.max(s, axis=-1, keepdims=True), sink)
    p = jnp.exp(s - m)
    denom = jnp.sum(p, axis=-1, keepdims=True) + jnp.exp(sink - m)
    return jnp.dot(p.astype(BF16), v, preferred_element_type=F32), denom


def _attn_prompt_kernel(sink_ref, q_ref, kc_ref, kp_ref, vc_ref, vp_ref, bias_ref, o_ref):
    n = pl.program_id(1)
    kk = jnp.concatenate([kp_ref[...], kc_ref[...]], axis=0).astype(BF16)
    vv = jnp.concatenate([vp_ref[...], vc_ref[...]], axis=0).astype(BF16)
    q = q_ref[...]
    rows = lax.broadcasted_iota(jnp.int32, (WINDOW, 2 * WINDOW), 0)
    cols = lax.broadcasted_iota(jnp.int32, (WINDOW, 2 * WINDOW), 1)
    dist = rows + WINDOW - cols
    valid = (dist >= 0) & (dist <= WINDOW) & ((n > 0) | (cols >= WINDOW))
    for g in range(N_KV_HEADS):
        kg = kk[:, g * HEAD_DIM:(g + 1) * HEAD_DIM]
        vg = vv[:, g * HEAD_DIM:(g + 1) * HEAD_DIM]
        for h in range(GROUP):
            hh = g * GROUP + h
            qh = q[:, hh * HEAD_DIM:(hh + 1) * HEAD_DIM]
            s = lax.dot_general(qh, kg, (((1,), (1,)), ((), ())), preferred_element_type=F32) * SM_SCALE
            s = jnp.where(valid, s + bias_ref[hh], NEG_INF)
            o, denom = _softmax_pv(s, sink_ref[0, hh], vg)
            o_ref[:, hh * HEAD_DIM:(hh + 1) * HEAD_DIM] = (o / denom).astype(BF16)


def _attn_prompt(q_all, k_all, v_all, bias, sinks, batch, seq):
    nb = seq // WINDOW
    cur = lambda width: pl.BlockSpec((WINDOW, width), lambda b, n: (b * nb + n, 0))
    prev = lambda width: pl.BlockSpec((WINDOW, width), lambda b, n: (b * nb + jnp.maximum(n - 1, 0), 0))
    return pl.pallas_call(
        _attn_prompt_kernel,
        grid=(batch, nb),
        in_specs=[
            pl.BlockSpec(memory_space=pltpu.SMEM),
            cur(D_ATTN), cur(D_KV), prev(D_KV), cur(D_KV), prev(D_KV),
            _const_spec((N_HEADS, WINDOW, 2 * WINDOW)),
        ],
        out_specs=cur(D_ATTN),
        out_shape=jax.ShapeDtypeStruct((batch * seq, D_ATTN), BF16),
        compiler_params=_params(("parallel", "arbitrary")),
        name="attn_prompt",
    )(sinks.reshape(1, N_HEADS).astype(F32), q_all, k_all, k_all, v_all, v_all, bias)


def _attn_sample_kernel(q_ref, kn_ref, vn_ref, kc_ref, vc_ref, bc_ref, bn_ref, sink_ref,
                        o_ref, ko_ref, vo_ref, *, seqs, s_len):
    w = kc_ref.shape[1]
    rows_c = lax.broadcasted_iota(jnp.int32, (GROUP * s_len, w), 0) % s_len
    cols_c = lax.broadcasted_iota(jnp.int32, (GROUP * s_len, w), 1)
    dist_c = rows_c + w - cols_c
    valid_c = (dist_c >= 0) & (dist_c <= WINDOW)
    rows_n = lax.broadcasted_iota(jnp.int32, (GROUP * s_len, s_len), 0) % s_len
    cols_n = lax.broadcasted_iota(jnp.int32, (GROUP * s_len, s_len), 1)
    dist_n = rows_n - cols_n
    valid_n = (dist_n >= 0) & (dist_n <= WINDOW)
    for j in range(seqs):
        r0 = j * s_len
        qj = q_ref[r0:r0 + s_len, :]
        kc = kc_ref[j]
        vc = vc_ref[j]
        kn = kn_ref[r0:r0 + s_len, :]
        vn = vn_ref[r0:r0 + s_len, :]
        ko_ref[j, 0:w - s_len, :] = kc[s_len:, :]
        ko_ref[j, w - s_len:w, :] = kn
        vo_ref[j, 0:w - s_len, :] = vc[s_len:, :]
        vo_ref[j, w - s_len:w, :] = vn
        kcb, vcb, knb, vnb = kc.astype(BF16), vc.astype(BF16), kn.astype(BF16), vn.astype(BF16)
        for g in range(N_KV_HEADS):
            lo, hi = g * HEAD_DIM, (g + 1) * HEAD_DIM
            qs = jnp.concatenate(
                [qj[:, (g * GROUP + h) * HEAD_DIM:(g * GROUP + h + 1) * HEAD_DIM] for h in range(GROUP)], axis=0)
            nt = (((1,), (1,)), ((), ()))
            s_c = lax.dot_general(qs, kcb[:, lo:hi], nt, preferred_element_type=F32) * SM_SCALE
            s_n = lax.dot_general(qs, knb[:, lo:hi], nt, preferred_element_type=F32) * SM_SCALE
            s_c = jnp.where(valid_c, s_c + bc_ref[g], NEG_INF)
            s_n = jnp.where(valid_n, s_n + bn_ref[g], NEG_INF)
            sink = sink_ref[g]
            m = jnp.maximum(jnp.maximum(jnp.max(s_c, axis=-1, keepdims=True), jnp.max(s_n, axis=-1, keepdims=True)), sink)
            p_c = jnp.exp(s_c - m)
            p_n = jnp.exp(s_n - m)
            denom = jnp.sum(p_c, axis=-1, keepdims=True) + jnp.sum(p_n, axis=-1, keepdims=True) + jnp.exp(sink - m)
            o = jnp.dot(p_c.astype(BF16), vcb[:, lo:hi], preferred_element_type=F32)
            o = o + jnp.dot(p_n.astype(BF16), vnb[:, lo:hi], preferred_element_type=F32)
            o = (o / denom).astype(BF16)
            for h in range(GROUP):
                hh = g * GROUP + h
                o_ref[r0:r0 + s_len, hh * HEAD_DIM:(hh + 1) * HEAD_DIM] = o[h * s_len:(h + 1) * s_len, :]


def _attn_sample(q_all, k_all, v_all, cache_k, cache_v, rel_bias, sinks, row0, dec_batch, s_len):
    w = cache_k.shape[1]
    seqs = _pick_tile(dec_batch, (16, 8, 4, 2, 1))
    rows = seqs * s_len
    blk0 = row0 // rows
    assert row0 % rows == 0
    qi = jnp.arange(s_len)
    dist_c = qi[:, None] + w - jnp.arange(w)[None, :]
    dist_n = qi[:, None] - jnp.arange(s_len)[None, :]
    b_c = _bias_table(rel_bias, dist_c).reshape(N_KV_HEADS, GROUP * s_len, w)
    b_n = _bias_table(rel_bias, dist_n).reshape(N_KV_HEADS, GROUP * s_len, s_len)
    sink = jnp.broadcast_to(sinks.astype(F32).reshape(N_KV_HEADS, GROUP, 1, 1), (N_KV_HEADS, GROUP, s_len, 1))
    sink = sink.reshape(N_KV_HEADS, GROUP * s_len, 1)
    tok = lambda width: pl.BlockSpec((rows, width), lambda i: (blk0 + i, 0))
    cache = pl.BlockSpec((seqs, w, D_KV), lambda i: (i, 0, 0))
    return pl.pallas_call(
        functools.partial(_attn_sample_kernel, seqs=seqs, s_len=s_len),
        grid=(dec_batch // seqs,),
        in_specs=[
            tok(D_ATTN), tok(D_KV), tok(D_KV), cache, cache,
            _const_spec(b_c.shape), _const_spec(b_n.shape), _const_spec(sink.shape),
        ],
        out_specs=(pl.BlockSpec((rows, D_ATTN), lambda i: (i, 0)), cache, cache),
        out_shape=(
            jax.ShapeDtypeStruct((dec_batch * s_len, D_ATTN), BF16),
            jax.ShapeDtypeStruct((dec_batch, w, D_KV), F32),
            jax.ShapeDtypeStruct((dec_batch, w, D_KV), F32),
        ),
        compiler_params=_params(("parallel",)),
        name="attn_sample",
    )(q_all, k_all, v_all, cache_k, cache_v, b_c, b_n, sink)


def _softplus(z):
    return jnp.maximum(z, 0.0) + jnp.log1p(jnp.exp(-jnp.abs(z)))


def _block_gate(xcb, w_ref, b_ref):
    parts = [jnp.dot(xcb[:, n * RNN_BLOCK:(n + 1) * RNN_BLOCK], w_ref[n], preferred_element_type=F32)
             for n in range(N_RNN_BLOCKS)]
    return jax.nn.sigmoid(jnp.concatenate(parts, axis=-1) + b_ref[...])


def _lru_coeffs(xc, wa_ref, ba_ref, wx_ref, bx_ref, lam_ref, first_row_unnormalised):
    xcb = xc.astype(BF16)
    r = _block_gate(xcb, wa_ref, ba_ref)
    i = _block_gate(xcb, wx_ref, bx_ref)
    log_a = -RG_C * r * _softplus(-lam_ref[...])
    a = jnp.exp(log_a)
    mult = jnp.sqrt(-jnp.tanh(log_a) * (a * a + 1.0))
    if first_row_unnormalised is not None:
        mult = jnp.where(first_row_unnormalised, 1.0, mult)
    return a, mult * i * xc


def _scan8(a, b):
    r8 = lax.broadcasted_iota(jnp.int32, a.shape, 0) % SUBLANES
    d = 1
    while d < SUBLANES:
        keep = r8 >= d
        a_sh = jnp.where(keep, pltpu.roll(a, d, 0), 1.0)
        b_sh = jnp.where(keep, pltpu.roll(b, d, 0), 0.0)
        b = a * b_sh + b
        a = a * a_sh
        d *= 2
    return a, b


def _rnn_prompt_kernel(xr_ref, gy_ref, cw_ref, cb_ref, wa_ref, ba_ref, wx_ref, bx_ref, lam_ref,
                       o_ref, nh_ref, ext_ref, a_ref, b_ref, hc_ref, *, tl):
    l = pl.program_id(1)

    @pl.when(l == 0)
    def _():
        ext_ref[0:SUBLANES, :] = jnp.zeros((SUBLANES, D_RNN), F32)
        hc_ref[...] = jnp.zeros((1, D_RNN), F32)

    x = xr_ref[...]
    ext_ref[SUBLANES:, :] = x
    xc = cb_ref[...] + cw_ref[CONV_W - 1:CONV_W, :] * x
    for j in range(1, CONV_W):
        xc = xc + cw_ref[CONV_W - 1 - j:CONV_W - j, :] * ext_ref[SUBLANES - j:SUBLANES - j + tl, :]
    ext_ref[0:SUBLANES, :] = x[tl - SUBLANES:, :]

    row = lax.broadcasted_iota(jnp.int32, (tl, D_RNN), 0)
    a, b = _lru_coeffs(xc, wa_ref, ba_ref, wx_ref, bx_ref, lam_ref, (row == 0) & (l == 0))
    a, b = _scan8(a, b)
    a_ref[...] = a
    b_ref[...] = b

    def chunk(c, h):
        sl = pl.ds(pl.multiple_of(c * SUBLANES, SUBLANES), SUBLANES)
        hc = b_ref[sl, :] + a_ref[sl, :] * h
        b_ref[sl, :] = hc
        return hc[SUBLANES - 1:SUBLANES, :]

    h = lax.fori_loop(0, tl // SUBLANES, chunk, hc_ref[...])
    hc_ref[...] = h
    nh_ref[0] = h
    o_ref[...] = (b_ref[...] * gy_ref[...]).astype(BF16)


def _rnn_prompt(xr_all, gy_all, rnn_w, batch, seq):
    tl =_pick_tile(seq, (256, 128, 64, 32, 16, 8))
    nl = seq // tl
    tok = pl.BlockSpec((tl, D_RNN), lambda b, l: (b * nl + l, 0))
    return pl.pallas_call(
        functools.partial(_rnn_prompt_kernel, tl=tl),
        grid=(batch, nl),
        in_specs=[tok, tok] + [_const_spec(w.shape) for w in rnn_w],
        out_specs=(tok, pl.BlockSpec((1, 1, D_RNN), lambda b, l: (b, 0, 0))),
        out_shape=(jax.ShapeDtypeStruct((batch * seq, D_RNN), BF16), jax.ShapeDtypeStruct((batch, 1, D_RNN), F32)),
        scratch_shapes=[
            pltpu.VMEM((tl + SUBLANES, D_RNN), F32),
            pltpu.VMEM((tl, D_RNN), F32),
            pltpu.VMEM((tl, D_RNN), F32),
            pltpu.VMEM((1, D_RNN), F32),
        ],
        compiler_params=_params(("parallel", "arbitrary")),
        name="rnn_prompt",
    )(xr_all, gy_all, *rnn_w)


def _rnn_sample_kernel(xr_ref, gy_ref, hp_ref, h0_ref, cw_ref, cb_ref, wa_ref, ba_ref, wx_ref, bx_ref, lam_ref,
                       o_ref, nh_ref, *, seqs):
    rows = seqs * SUBLANES
    x = xr_ref[...]
    hp = hp_ref[...]
    r8 = lax.broadcasted_iota(jnp.int32, (rows, D_RNN), 0) % SUBLANES
    xc = cb_ref[...] + cw_ref[CONV_W - 1:CONV_W, :] * x
    for j in range(1, CONV_W):
        shifted = jnp.where(r8 >= j, pltpu.roll(x, j, 0), pltpu.roll(hp, rows - (SUBLANES - j), 0))
        xc = xc + cw_ref[CONV_W - 1 - j:CONV_W - j, :] * shifted
    a, b = _lru_coeffs(xc, wa_ref, ba_ref, wx_ref, bx_ref, lam_ref, None)
    a, b = _scan8(a, b)
    h0 = jnp.broadcast_to(h0_ref[...][:, None, :], (seqs, SUBLANES, D_RNN)).reshape(rows, D_RNN)
    h = b + a * h0
    last = jnp.where(r8 == SUBLANES - 1, h, 0.0).reshape(seqs, SUBLANES, D_RNN)
    nh_ref[...] = jnp.sum(last, axis=1)
    o_ref[...] = (h * gy_ref[...]).astype(BF16)


def _rnn_sample(xr_all, gy_all, hist_pad, h0, rnn_w, row0, dec_batch):
    seqs = _pick_tile(dec_batch, (16, 8))
    rows = seqs * SUBLANES
    assert row0 % rows == 0
    blk0 = row0 // rows
    tok = pl.BlockSpec((rows, D_RNN), lambda i: (blk0 + i, 0))
    return pl.pallas_call(
        functools.partial(_rnn_sample_kernel, seqs=seqs),
        grid=(dec_batch // seqs,),
        in_specs=[tok, tok, pl.BlockSpec((rows, D_RNN), lambda i: (i, 0)), pl.BlockSpec((seqs, D_RNN), lambda i: (i, 0))]
        + [_const_spec(w.shape) for w in rnn_w],
        out_specs=(pl.BlockSpec((rows, D_RNN), lambda i: (i, 0)), pl.BlockSpec((seqs, D_RNN), lambda i: (i, 0))),
        out_shape=(jax.ShapeDtypeStruct((dec_batch * SUBLANES, D_RNN), BF16),
                   jax.ShapeDtypeStruct((dec_batch, D_RNN), F32)),
        compiler_params=_params(("parallel",)),
        name="rnn_sample",
    )(xr_all, gy_all, hist_pad, h0, *rnn_w)


def _layer_norm(z, g, b):
    mu = jnp.mean(z, axis=-1, keepdims=True)
    zc = z - mu
    var = jnp.mean(zc * zc, axis=-1, keepdims=True)
    return zc * lax.rsqrt(var + LN_EPS) * g + b


def _first_index_of_max(vals, iota, axis, sentinel):
    mx = jnp.max(vals, axis=axis, keepdims=True)
    return mx, jnp.min(jnp.where(vals == mx, iota, sentinel), axis=axis, keepdims=True)


def _route(scores, bias):
    t = scores.shape[1]
    grp = scores + bias
    g3 = grp.reshape(N_GROUPS, GROUP_SIZE, t)
    e_in_g = lax.broadcasted_iota(jnp.int32, g3.shape, 1)
    m1, first = _first_index_of_max(g3, e_in_g, 1, GROUP_SIZE)
    m2 = jnp.max(jnp.where(e_in_g == first, -jnp.inf, g3), axis=1, keepdims=True)
    gscore = (m1 + m2).reshape(N_GROUPS, t)
    g_iota = lax.broadcasted_iota(jnp.int32, gscore.shape, 0)
    gmask = jnp.zeros(gscore.shape, jnp.bool_)
    for _ in range(TOPK_GROUPS):
        _, gi = _first_index_of_max(gscore, g_iota, 0, N_GROUPS)
        hit = g_iota == gi
        gmask = gmask | hit
        gscore = jnp.where(hit, -jnp.inf, gscore)
    masked = jnp.where(gmask[:, None, :], g3, -jnp.inf).reshape(N_EXPERTS, t)
    e_iota = lax.broadcasted_iota(jnp.int32, masked.shape, 0)
    idx, wts, hits = [], [], []
    for _ in range(TOP_K):
        _, ei = _first_index_of_max(masked, e_iota, 0, N_EXPERTS)
        hit = e_iota == ei
        idx.append(ei)
        hits.append(hit)
        wts.append(jnp.sum(jnp.where(hit, scores, 0.0), axis=0, keepdims=True))
        masked = jnp.where(hit, -jnp.inf, masked)
    idx = jnp.concatenate(idx, axis=0)
    w = jnp.concatenate(wts, axis=0)
    w = w / jnp.sum(w, axis=0, keepdims=True) * ROUTED_SCALE
    return idx, w, hits


def _merge_kernel(xp_ref, xs_ref, aop_ref, aos_ref, rop_ref, ros_ref, sga_ref, sgr_ref, woa_ref, wor_ref, wout_ref, g1_ref, b1_ref,
                  wrt_ref, rb_ref, wsg_ref, wsu_ref, wsd_ref,
                  x1r_ref, base_ref, idx_ref, wt_ref, rank_ref, cnt_ref, carry_ref, *, n_p, tm):
    i = pl.program_id(0)

    @pl.when(i == 0)
    def _():
        carry_ref[...] = jnp.zeros(carry_ref.shape, F32)

    is_prompt = i < n_p
    x = jnp.where(is_prompt, xp_ref[...], xs_ref[...])
    pa = jnp.dot(jnp.where(is_prompt, aop_ref[...], aos_ref[...]), woa_ref[...], preferred_element_type=F32)
    pr = jnp.dot(jnp.where(is_prompt, rop_ref[...], ros_ref[...]), wor_ref[...], preferred_element_type=F32)
    merged = sga_ref[...] * pa + sgr_ref[...] * pr
    z = ALPHA * x + jnp.dot(merged.astype(BF16), wout_ref[...], preferred_element_type=F32)
    x1 = _layer_norm(z, g1_ref[...], b1_ref[...])
    _store_token_rows(x1r_ref, x1, tm)
    x1b = x1.astype(BF16)
    u = jax.nn.silu(jnp.dot(x1b, wsg_ref[...], preferred_element_type=F32)) * jnp.dot(x1b, wsu_ref[...], preferred_element_type=F32)
    shared = jnp.dot(u.astype(BF16), wsd_ref[...], preferred_element_type=F32)
    base_ref[...] = ALPHA * x1 + shared
    logits = lax.dot_general(wrt_ref[...], x1, (((1,), (1,)), ((), ())), preferred_element_type=F32)
    idx, w, hits = _route(jax.nn.sigmoid(logits), rb_ref[...])
    idx_ref[...] = idx
    wt_ref[...] = w

    chosen = functools.reduce(jnp.logical_or, hits)
    chosen_f = jnp.where(chosen, 1.0, 0.0)
    earlier = (lax.broadcasted_iota(jnp.int32, (tm, tm), 0) < lax.broadcasted_iota(jnp.int32, (tm, tm), 1))
    prefix = jnp.dot(chosen_f.astype(BF16), jnp.where(earlier, 1.0, 0.0).astype(BF16), preferred_element_type=F32)
    before = prefix + carry_ref[...]
    ranks = [jnp.sum(jnp.where(hit, before, 0.0), axis=0, keepdims=True) for hit in hits]
    rank_ref[...] = jnp.concatenate(ranks, axis=0).astype(jnp.int32)
    carry_ref[...] = carry_ref[...] + jnp.sum(chosen_f, axis=1, keepdims=True)
    cnt_ref[...] = carry_ref[...]


def _merge(x_p, x_s, attn_p, attn_s, rnn_p, rnn_s, sga, sgr, weights):
    t_p, t_s = x_p.shape[0], x_s.shape[0]
    t = t_p + t_s
    tm = _pick_tile(math.gcd(t_p, t_s), (256, 128))
    row = lambda width: pl.BlockSpec((tm, width), lambda i: (i, 0))
    col = pl.BlockSpec((TOP_K, tm), lambda i: (0, i))

    return pl.pallas_call(
        functools.partial(_merge_kernel, n_p=t_p // tm, tm=tm),
        grid=(t // tm,),
        in_specs=[*_two_source_specs(t_p, t_s, tm, D_MODEL), *_two_source_specs(t_p, t_s, tm, D_ATTN),
                  *_two_source_specs(t_p, t_s, tm, D_RNN), row(D_MODEL), row(D_MODEL)]
        + [_const_spec(w.shape) for w in weights],
        out_specs=(pl.BlockSpec((tm * ROW_TILE, LANES), lambda i: (i, 0)), row(D_MODEL), col, col, col,
                   _const_spec((N_EXPERTS, 1))),
        out_shape=(
            jax.ShapeDtypeStruct((t * ROW_TILE, LANES), F32),
            jax.ShapeDtypeStruct((t, D_MODEL), F32),
            jax.ShapeDtypeStruct((TOP_K, t), jnp.int32),
            jax.ShapeDtypeStruct((TOP_K, t), F32),
            jax.ShapeDtypeStruct((TOP_K, t), jnp.int32),
            jax.ShapeDtypeStruct((N_EXPERTS, 1), F32),
        ),
        scratch_shapes=[pltpu.VMEM((N_EXPERTS, 1), F32)],
        compiler_params=_params(("arbitrary",)),
        name="merge_ln1_route",
    )(x_p, x_s, attn_p, attn_s, rnn_p, rnn_s, sga, sgr, *weights)


def _expert_layout(counts, n_assign):
    counts = counts.reshape(N_EXPERTS).astype(jnp.int32)
    padded = (counts + EXPERT_ROWS - 1) // EXPERT_ROWS * EXPERT_ROWS
    pend = jnp.cumsum(padded)
    pstart = pend - padded
    n_blocks = (n_assign + N_EXPERTS * (EXPERT_ROWS - 1) + EXPERT_ROWS - 1) // EXPERT_ROWS
    block_start = jnp.arange(n_blocks, dtype=jnp.int32) * EXPERT_ROWS
    block_e = jnp.minimum(jnp.searchsorted(pend, block_start, side='right'), N_EXPERTS - 1).astype(jnp.int32)
    block_used = (block_start < pend[-1]).astype(jnp.int32)
    block_first = jnp.concatenate([jnp.ones((1,), jnp.int32), (block_e[1:] != block_e[:-1]).astype(jnp.int32)])
    pad_start = jnp.concatenate([pstart + counts, pend[-1:] // EXPERT_ROWS]).astype(jnp.int32)
    return pstart, pad_start, n_blocks, block_e, block_used, block_first


def _token_rows(r, n=1):
    return pl.ds(pl.multiple_of(r * ROW_TILE, ROW_TILE), n * ROW_TILE)


def _dispatch_kernel(pad_start_ref, dest_ref, x_ref, xs_ref, zero_ref, sem, *, tm, n_blocks):
    @pl.when(pl.program_id(0) == 0)
    def _():
        zero_ref[...] = jnp.zeros(zero_ref.shape, F32)

        def zero_rows(row):
            cp = pltpu.make_async_copy(zero_ref, xs_ref.at[_token_rows(row, EXPERT_ROWS)], sem)
            cp.start()
            cp.wait()

        def zero_padding(e, c):
            zero_rows(pad_start_ref[e])
            return c

        def zero_unused_block(b, c):
            zero_rows(b * EXPERT_ROWS)
            return c

        lax.fori_loop(0, N_EXPERTS, zero_padding, 0)
        lax.fori_loop(pad_start_ref[N_EXPERTS], n_blocks + 1, zero_unused_block, 0)

    def issue(t, c):
        for k in range(TOP_K):
            pltpu.make_async_copy(x_ref.at[_token_rows(t)], xs_ref.at[_token_rows(dest_ref[k, t])], sem).start()
        return c

    lax.fori_loop(0, tm, issue, 0)
    for _ in range(TOP_K):
        pltpu.make_async_copy(x_ref, xs_ref.at[_token_rows(0, tm)], sem).wait()


def _dispatch(x1r, dest, pad_start, n_blocks):
    t = dest.shape[1]
    tm = _pick_tile(t, (512, 256, 128))
    rows = (n_blocks + 1) * EXPERT_ROWS
    grid_spec = pltpu.PrefetchScalarGridSpec(
        num_scalar_prefetch=1,
        grid=(t // tm,),
        in_specs=[
            pl.BlockSpec((TOP_K, tm), lambda i, ps: (0, i), memory_space=pltpu.SMEM),
            pl.BlockSpec((tm * ROW_TILE, LANES), lambda i, ps: (i, 0)),
        ],
        out_specs=pl.BlockSpec(memory_space=pl.ANY),
        scratch_shapes=[pltpu.VMEM((EXPERT_ROWS * ROW_TILE, LANES), F32), pltpu.SemaphoreType.DMA],
    )
    return pl.pallas_call(
        functools.partial(_dispatch_kernel, tm=tm, n_blocks=n_blocks),
        grid_spec=grid_spec,
        out_shape=jax.ShapeDtypeStruct((rows * ROW_TILE, LANES), F32),
        compiler_params=_params(("arbitrary",)),
        name="dispatch",
    )(pad_start, dest, x1r)


def _expert_kernel(be_ref, bu_ref, bf_ref, xs_ref, wg_ref, wu_ref, wd_ref, o_ref, wgb_ref, wub_ref, wdb_ref):
    del be_ref
    i = pl.program_id(0)

    @pl.when(bf_ref[i] != 0)
    def _():
        wgb_ref[...] = wg_ref[...].astype(BF16)
        wub_ref[...] = wu_ref[...].astype(BF16)
        wdb_ref[...] = wd_ref[...].astype(BF16)

    @pl.when(bu_ref[i] != 0)
    def _():
        xb = jnp.concatenate(_load_token_rows(xs_ref, EXPERT_ROWS), axis=1).astype(BF16)
        g = jnp.dot(xb, wgb_ref[...], preferred_element_type=F32)
        u = jnp.dot(xb, wub_ref[...], preferred_element_type=F32)
        act = (jax.nn.silu(g) * u).astype(BF16)
        _store_token_rows(o_ref, jnp.dot(act, wdb_ref[...], preferred_element_type=F32), EXPERT_ROWS)

    @pl.when(bu_ref[i] == 0)
    def _():
        o_ref[...] = jnp.zeros(o_ref.shape, F32)


def _expert_ffn(xs, n_blocks, block_e, block_used, block_first, w_e_gate, w_e_up, w_e_down):
    rows_blk = pl.BlockSpec((EXPERT_ROWS * ROW_TILE, LANES), lambda i, be, bu, bf: (i, 0))
    weight = lambda shape: pl.BlockSpec((None, *shape), lambda i, be, bu, bf: (be[i], 0, 0))
    grid_spec = pltpu.PrefetchScalarGridSpec(
        num_scalar_prefetch=3,
        grid=(n_blocks,),
        in_specs=[rows_blk, weight((D_MODEL, D_EXPERT)), weight((D_MODEL, D_EXPERT)), weight((D_EXPERT, D_MODEL))],
        out_specs=rows_blk,
        scratch_shapes=[pltpu.VMEM((D_MODEL, D_EXPERT), BF16), pltpu.VMEM((D_MODEL, D_EXPERT), BF16),
                        pltpu.VMEM((D_EXPERT, D_MODEL), BF16)],
    )
    return pl.pallas_call(
        _expert_kernel,
        grid_spec=grid_spec,
        out_shape=jax.ShapeDtypeStruct((n_blocks * EXPERT_ROWS * ROW_TILE, LANES), F32),
        compiler_params=_params(("arbitrary",)),
        name="expert_ffn",
    )(block_e, block_used, block_first, xs, w_e_gate, w_e_up, w_e_down)


def _combine_kernel(dest_ref, w_ref, base_ref, g_ref, b_ref, outs_ref, yp_ref, ys_ref, buf_ref, sem, *, tm, n_p):
    i = pl.program_id(0)

    def issue(t, c):
        for k in range(TOP_K):
            pltpu.make_async_copy(outs_ref.at[_token_rows(dest_ref[k, t])], buf_ref.at[_token_rows(k * tm + t)], sem).start()
        return c

    lax.fori_loop(0, tm, issue, 0)
    for k in range(TOP_K):
        pltpu.make_async_copy(outs_ref.at[_token_rows(0, tm)], buf_ref.at[_token_rows(k * tm, tm)], sem).wait()

    w = w_ref[...]
    chunks = [None] * ROW_TILE
    for k in range(TOP_K):
        wk = w[:, k:k + 1]
        for s, rows in enumerate(_load_token_rows(buf_ref, tm, row0=k * tm)):
            chunks[s] = wk * rows if chunks[s] is None else chunks[s] + wk * rows
    y = _layer_norm(base_ref[...] + jnp.concatenate(chunks, axis=1), g_ref[...], b_ref[...])

    @pl.when(i < n_p)
    def _():
        yp_ref[...] = y

    @pl.when(i >= n_p)
    def _():
        ys_ref[...] = y


def _combine(out_sorted, dest, w_tok, base, g2, b2, t_p, t_s):
    t = t_p + t_s
    tm = _pick_tile(math.gcd(t_p, t_s), (256, 128))
    row = pl.BlockSpec((tm, D_MODEL), lambda i: (i, 0))
    out_p, out_s = _two_source_specs(t_p, t_s, tm, D_MODEL)
    return pl.pallas_call(
        functools.partial(_combine_kernel, tm=tm, n_p=t_p // tm),
        grid=(t // tm,),
        in_specs=[
            pl.BlockSpec((TOP_K, tm), lambda i: (0, i), memory_space=pltpu.SMEM),
            pl.BlockSpec((tm, TOP_K), lambda i: (i, 0)),
            row, _const_spec((1, D_MODEL)), _const_spec((1, D_MODEL)),
            pl.BlockSpec(memory_space=pl.ANY),
        ],
        out_specs=(out_p, out_s),
        out_shape=(jax.ShapeDtypeStruct((t_p, D_MODEL), F32), jax.ShapeDtypeStruct((t_s, D_MODEL), F32)),
        scratch_shapes=[pltpu.VMEM((TOP_K * tm * ROW_TILE, LANES), F32), pltpu.SemaphoreType.DMA],
        compiler_params=_params(("arbitrary",)),
        name="combine_ln2",
    )(dest, w_tok, base, g2, b2, out_sorted)


def kernel(x_prompt, x_sample, cache_k, cache_v, state_conv, state_rnn, w_in, conv_w, conv_b, w_gate_a, b_gate_a, w_gate_x, b_gate_x, lru_lambda, rel_bias, sinks, w_o_attn, w_o_rnn, w_out, ln1_g, ln1_b, w_router, router_bias, w_e_gate, w_e_up, w_e_down, w_s_gate, w_s_up, w_s_down, ln2_g, ln2_b):
    assert w_in.shape[0] == DEPTH == 1
    batch, seq, _ = x_prompt.shape
    dec_batch, s_len, _ = x_sample.shape
    w_cache = cache_k.shape[2]
    assert s_len == SUBLANES and seq % WINDOW == 0 and w_cache == WINDOW
    t_p = batch * seq
    t_s = dec_batch * s_len
    vec = lambda a: a[0].reshape(1, -1).astype(F32)

    x_p = x_prompt.reshape(t_p, D_MODEL)
    x_s = x_sample.reshape(t_s, D_MODEL)
    q, k, v, xr, gy, sga, sgr = _inproj(x_p, x_s, w_in[0].astype(BF16))

    qi = jnp.arange(WINDOW)
    dist = qi[:, None] + WINDOW - jnp.arange(2 * WINDOW)[None, :]
    bias_p = _bias_table(rel_bias, dist)
    attn_p = _attn_prompt(q, k, v, bias_p, sinks[0], batch, seq)
    attn_s, k_s, v_s = _attn_sample(
        q, k, v, cache_k[0].reshape(dec_batch, w_cache, D_KV), cache_v[0].reshape(dec_batch, w_cache, D_KV),
        rel_bias, sinks[0], t_p, dec_batch, s_len)

    rnn_w = (conv_w[0], vec(conv_b), w_gate_a[0].astype(BF16), vec(b_gate_a), w_gate_x[0].astype(BF16),
             vec(b_gate_x), vec(lru_lambda))
    rnn_p, h_p = _rnn_prompt(xr, gy, rnn_w, batch, seq)
    hist_pad = jnp.pad(state_conv[0], ((0, 0), (SUBLANES - (CONV_W - 1), 0), (0, 0))).reshape(t_s, D_RNN)
    rnn_s, h_s = _rnn_sample(xr, gy, hist_pad, state_rnn[0], rnn_w, t_p, dec_batch)

    merge_w = (w_o_attn[0].astype(BF16), w_o_rnn[0].astype(BF16), w_out[0].astype(BF16), vec(ln1_g), vec(ln1_b),
               w_router[0].T, router_bias[0].reshape(N_EXPERTS, 1), w_s_gate[0].astype(BF16),
               w_s_up[0].astype(BF16), w_s_down[0].astype(BF16))
    x1r, base, idx_t, wt_t, rank_t, counts = _merge(x_p, x_s, attn_p, attn_s, rnn_p, rnn_s, sga, sgr, merge_w)

    pstart, pad_start, n_blocks, block_e, block_used, block_first = _expert_layout(counts, (t_p + t_s) * TOP_K)
    dest = pstart[idx_t] + rank_t
    xs = _dispatch(x1r, dest, pad_start, n_blocks)
    out_sorted = _expert_ffn(xs, n_blocks, block_e, block_used, block_first, w_e_gate[0], w_e_up[0], w_e_down[0])
    y_p, y_s = _combine(out_sorted, dest, wt_t.T, base, vec(ln2_g), vec(ln2_b), t_p, t_s)
    y_p = y_p.reshape(batch, seq, D_MODEL)
    y_s = y_s.reshape(dec_batch, s_len, D_MODEL)
    kv5 = lambda a, b: a.reshape(1, b, WINDOW, N_KV_HEADS, HEAD_DIM)
    k_p = kv5(k[:t_p].reshape(batch, seq, D_KV)[:, seq - WINDOW:], batch)
    v_p = kv5(v[:t_p].reshape(batch, seq, D_KV)[:, seq - WINDOW:], batch)
    conv_p = xr[:t_p].reshape(batch, seq, D_RNN)[:, seq - (CONV_W - 1):][None]
    conv_s = xr[t_p:].reshape(dec_batch, s_len, D_RNN)[:, s_len - (CONV_W - 1):][None]
    return (y_p, y_s, k_p, v_p, conv_p, h_p.reshape(1, batch, D_RNN),
            kv5(k_s, dec_batch), kv5(v_s, dec_batch), conv_s, h_s.reshape(1, dec_batch, D_RNN))
```

```python
import functools
import math

import jax
import jax.numpy as jnp
from jax import lax
from jax.experimental import pallas as pl
from jax.experimental.pallas import tpu as pltpu

F32 = jnp.float32
BF16 = jnp.bfloat16

D_MODEL = 1024
N_HEADS = 8
N_KV_HEADS = 2
HEAD_DIM = 64
GROUP = N_HEADS // N_KV_HEADS
WINDOW = 128
D_ATTN = N_HEADS * HEAD_DIM
D_KV = N_KV_HEADS * HEAD_DIM
N_BUCKETS = 32
MAX_DISTANCE = 128
D_RNN = D_MODEL
RNN_BLOCK = 256
N_RNN_BLOCKS = D_RNN // RNN_BLOCK
CONV_W = 4
RG_C = 8.0
N_EXPERTS = 256
TOP_K = 8
N_GROUPS = 8
GROUP_SIZE = N_EXPERTS // N_GROUPS
TOPK_GROUPS = 4
D_EXPERT = D_MODEL // 4
ROUTED_SCALE = 2.5
LN_EPS = 1e-5
DEPTH = 1
ALPHA = (2 * DEPTH) ** 0.25
NEG_INF = -1e30
SM_SCALE = HEAD_DIM ** -0.5

O_Q = 0
O_K = D_ATTN
O_V = O_K + D_KV
O_XR = O_V + D_KV
O_YR = O_XR + D_RNN
O_GA = O_YR + D_RNN
O_GR = O_GA + D_MODEL
D_IN = O_GR + D_MODEL

SUBLANES = 8
VMEM_LIMIT_BYTES = 56 * 1024 * 1024
EXPERT_ROWS = 256


def _params(sem):
    return pltpu.CompilerParams(dimension_semantics=sem, vmem_limit_bytes=VMEM_LIMIT_BYTES)


def _pick_tile(n, candidates):
    for c in candidates:
        if n % c == 0:
            return c
    raise ValueError(f"no tile for {n}")


def _const_spec(shape):
    nd = len(shape)
    return pl.BlockSpec(shape, lambda *_: (0,) * nd)


LANES = 128
ROW_TILE = D_MODEL // LANES
assert ROW_TILE == SUBLANES


def _store_token_rows(ref, mat, n, row0=0):
    for s in range(ROW_TILE):
        ref[pl.ds(row0 * ROW_TILE + s, n, stride=ROW_TILE), :] = mat[:, s * LANES:(s + 1) * LANES]


def _load_token_rows(ref, n, row0=0):
    return [ref[pl.ds(row0 * ROW_TILE + s, n, stride=ROW_TILE), :] for s in range(ROW_TILE)]


def _two_source_specs(t_p, t_s, tm, width):
    n_p = t_p // tm
    assert t_p % tm == 0 and t_s % tm == 0
    return (pl.BlockSpec((tm, width), lambda i, *_: (jnp.minimum(i, n_p - 1), 0)),
            pl.BlockSpec((tm, width), lambda i, *_: (jnp.maximum(i - n_p, 0), 0)))


def _inproj_kernel(xp_ref, xs_ref, w_ref, q_ref, k_ref, v_ref, xr_ref, gy_ref, sga_ref, sgr_ref, *, n_p):
    x = jnp.where(pl.program_id(0) < n_p, xp_ref[...], xs_ref[...]).astype(BF16)

    def seg(lo, hi):
        return jnp.dot(x, w_ref[:, lo:hi], preferred_element_type=F32)

    q_ref[...] = seg(O_Q, O_K).astype(BF16)
    k_ref[...] = seg(O_K, O_V)
    v_ref[...] = seg(O_V, O_XR)
    xr_ref[...] = seg(O_XR, O_YR)
    gy_ref[...] = jax.nn.gelu(seg(O_YR, O_GA)).astype(BF16)
    sga_ref[...] = jax.nn.sigmoid(seg(O_GA, O_GR)).astype(BF16)
    sgr_ref[...] = jax.nn.sigmoid(seg(O_GR, D_IN)).astype(BF16)


def _inproj(x_p, x_s, w_in_bf16):
    t_p, t_s = x_p.shape[0], x_s.shape[0]
    t = t_p + t_s
    tm = _pick_tile(math.gcd(t_p, t_s), (256, 128, 64, 32, 16, 8))
    row = lambda width: pl.BlockSpec((tm, width), lambda i: (i, 0))
    out_shape = (
        jax.ShapeDtypeStruct((t, D_ATTN), BF16),
        jax.ShapeDtypeStruct((t, D_KV), F32),
        jax.ShapeDtypeStruct((t, D_KV), F32),
        jax.ShapeDtypeStruct((t, D_RNN), F32),
        jax.ShapeDtypeStruct((t, D_RNN), BF16),
        jax.ShapeDtypeStruct((t, D_MODEL), BF16),
        jax.ShapeDtypeStruct((t, D_MODEL), BF16),
    )
    return pl.pallas_call(
        functools.partial(_inproj_kernel, n_p=t_p // tm),
        grid=(t // tm,),
        in_specs=[*_two_source_specs(t_p, t_s, tm, D_MODEL), _const_spec((D_MODEL, D_IN))],
        out_specs=(row(D_ATTN), row(D_KV), row(D_KV), row(D_RNN), row(D_RNN), row(D_MODEL), row(D_MODEL)),
        out_shape=out_shape,
        compiler_params=_params(("parallel",)),
        name="inproj",
    )(x_p, x_s, w_in_bf16)


def _t5_bucket(dist):
    n = jnp.maximum(dist, 0)
    max_exact = N_BUCKETS // 2
    nf = jnp.maximum(n, 1).astype(F32)
    large = max_exact + (jnp.log(nf / max_exact) / math.log(MAX_DISTANCE / max_exact) * (N_BUCKETS - max_exact)).astype(jnp.int32)
    large = jnp.minimum(large, N_BUCKETS - 1)
    return jnp.where(n < max_exact, n, large)


def _bias_table(rel_bias, dist):
    bucket = _t5_bucket(dist)
    rb = rel_bias.astype(F32)
    out = jnp.zeros((N_HEADS, *dist.shape), F32)
    for j in range(N_BUCKETS):
        out = jnp.where(bucket[None] == j, rb[j][:, None, None], out)
    return out


def _softmax_pv(s, sink, v):
    m = jnp.maximum(jnp.max(s, axis=-1, keepdims=True), sink)
    p = jnp.exp(s - m)
    denom = jnp.sum(p, axis=-1, keepdims=True) + jnp.exp(sink - m)
    return jnp.dot(p.astype(BF16), v, preferred_element_type=F32), denom


def _attn_prompt_kernel(sink_ref, q_ref, kc_ref, kp_ref, vc_ref, vp_ref, bias_ref, o_ref):
    n = pl.program_id(1)
    kk = jnp.concatenate([kp_ref[...], kc_ref[...]], axis=0).astype(BF16)
    vv = jnp.concatenate([vp_ref[...], vc_ref[...]], axis=0).astype(BF16)
    q = q_ref[...]
    rows = lax.broadcasted_iota(jnp.int32, (WINDOW, 2 * WINDOW), 0)
    cols = lax.broadcasted_iota(jnp.int32, (WINDOW, 2 * WINDOW), 1)
    dist = rows + WINDOW - cols
    valid = (dist >= 0) & (dist <= WINDOW) & ((n > 0) | (cols >= WINDOW))
    for g in range(N_KV_HEADS):
        kg = kk[:, g * HEAD_DIM:(g + 1) * HEAD_DIM]
        vg = vv[:, g * HEAD_DIM:(g + 1) * HEAD_DIM]
        for h in range(GROUP):
            hh = g * GROUP + h
            qh = q[:, hh * HEAD_DIM:(hh + 1) * HEAD_DIM]
            s = lax.dot_general(qh, kg, (((1,), (1,)), ((), ())), preferred_element_type=F32) * SM_SCALE
            s = jnp.where(valid, s + bias_ref[hh], NEG_INF)
            o, denom = _softmax_pv(s, sink_ref[0, hh], vg)
            o_ref[:, hh * HEAD_DIM:(hh + 1) * HEAD_DIM] = (o / denom).astype(BF16)


def _attn_prompt(q_all, k_all, v_all, bias, sinks, batch, seq):
    nb = seq // WINDOW
    cur = lambda width: pl.BlockSpec((WINDOW, width), lambda b, n: (b * nb + n, 0))
    prev = lambda width: pl.BlockSpec((WINDOW, width), lambda b, n: (b * nb + jnp.maximum(n - 1, 0), 0))
    return pl.pallas_call(
        _attn_prompt_kernel,
        grid=(batch, nb),
        in_specs=[
            pl.BlockSpec(memory_space=pltpu.SMEM),
            cur(D_ATTN), cur(D_KV), prev(D_KV), cur(D_KV), prev(D_KV),
            _const_spec((N_HEADS, WINDOW, 2 * WINDOW)),
        ],
        out_specs=cur(D_ATTN),
        out_shape=jax.ShapeDtypeStruct((batch * seq, D_ATTN), BF16),
        compiler_params=_params(("parallel", "arbitrary")),
        name="attn_prompt",
    )(sinks.reshape(1, N_HEADS).astype(F32), q_all, k_all, k_all, v_all, v_all, bias)


def _attn_sample_kernel(q_ref, kn_ref, vn_ref, kc_ref, vc_ref, bc_ref, bn_ref, sink_ref,
                        o_ref, ko_ref, vo_ref, *, seqs, s_len):
    w = kc_ref.shape[1]
    rows_c = lax.broadcasted_iota(jnp.int32, (GROUP * s_len, w), 0) % s_len
    cols_c = lax.broadcasted_iota(jnp.int32, (GROUP * s_len, w), 1)
    dist_c = rows_c + w - cols_c
    valid_c = (dist_c >= 0) & (dist_c <= WINDOW)
    rows_n = lax.broadcasted_iota(jnp.int32, (GROUP * s_len, s_len), 0) % s_len
    cols_n = lax.broadcasted_iota(jnp.int32, (GROUP * s_len, s_len), 1)
    dist_n = rows_n - cols_n
    valid_n = (dist_n >= 0) & (dist_n <= WINDOW)
    for j in range(seqs):
        r0 = j * s_len
        qj = q_ref[r0:r0 + s_len, :]
        kc = kc_ref[j]
        vc = vc_ref[j]
        kn = kn_ref[r0:r0 + s_len, :]
        vn = vn_ref[r0:r0 + s_len, :]
        ko_ref[j, 0:w - s_len, :] = kc[s_len:, :]
        ko_ref[j, w - s_len:w, :] = kn
        vo_ref[j, 0:w - s_len, :] = vc[s_len:, :]
        vo_ref[j, w - s_len:w, :] = vn
        kcb, vcb, knb, vnb = kc.astype(BF16), vc.astype(BF16), kn.astype(BF16), vn.astype(BF16)
        for g in range(N_KV_HEADS):
            lo, hi = g * HEAD_DIM, (g + 1) * HEAD_DIM
            qs = jnp.concatenate(
                [qj[:, (g * GROUP + h) * HEAD_DIM:(g * GROUP + h + 1) * HEAD_DIM] for h in range(GROUP)], axis=0)
            nt = (((1,), (1,)), ((), ()))
            s_c = lax.dot_general(qs, kcb[:, lo:hi], nt, preferred_element_type=F32) * SM_SCALE
            s_n = lax.dot_general(qs, knb[:, lo:hi], nt, preferred_element_type=F32) * SM_SCALE
            s_c = jnp.where(valid_c, s_c + bc_ref[g], NEG_INF)
            s_n = jnp.where(valid_n, s_n + bn_ref[g], NEG_INF)
            sink = sink_ref[g]
            m = jnp.maximum(jnp.maximum(jnp.max(s_c, axis=-1, keepdims=True), jnp.max(s_n, axis=-1, keepdims=True)), sink)
            p_c = jnp.exp(s_c - m)
            p_n = jnp.exp(s_n - m)
            denom = jnp.sum(p_c, axis=-1, keepdims=True) + jnp.sum(p_n, axis=-1, keepdims=True) + jnp.exp(sink - m)
            o = jnp.dot(p_c.astype(BF16), vcb[:, lo:hi], preferred_element_type=F32)
            o = o + jnp.dot(p_n.astype(BF16), vnb[:, lo:hi], preferred_element_type=F32)
            o = (o / denom).astype(BF16)
            for h in range(GROUP):
                hh = g * GROUP + h
                o_ref[r0:r0 + s_len, hh * HEAD_DIM:(hh + 1) * HEAD_DIM] = o[h * s_len:(h + 1) * s_len, :]


def _attn_sample(q_all, k_all, v_all, cache_k, cache_v, rel_bias, sinks, row0, dec_batch, s_len):
    w = cache_k.shape[1]
    seqs = _pick_tile(dec_batch, (16, 8, 4, 2, 1))
    rows = seqs * s_len
    blk0 = row0 // rows
    assert row0 % rows == 0
    qi = jnp.arange(s_len)
    dist_c = qi[:, None] + w - jnp.arange(w)[None, :]
    dist_n = qi[:, None] - jnp.arange(s_len)[None, :]
    b_c = _bias_table(rel_bias, dist_c).reshape(N_KV_HEADS, GROUP * s_len, w)
    b_n = _bias_table(rel_bias, dist_n).reshape(N_KV_HEADS, GROUP * s_len, s_len)
    sink = jnp.broadcast_to(sinks.astype(F32).reshape(N_KV_HEADS, GROUP, 1, 1), (N_KV_HEADS, GROUP, s_len, 1))
    sink = sink.reshape(N_KV_HEADS, GROUP * s_len, 1)
    tok = lambda width: pl.BlockSpec((rows, width), lambda i: (blk0 + i, 0))
    cache = pl.BlockSpec((seqs, w, D_KV), lambda i: (i, 0, 0))
    return pl.pallas_call(
        functools.partial(_attn_sample_kernel, seqs=seqs, s_len=s_len),
        grid=(dec_batch // seqs,),
        in_specs=[
            tok(D_ATTN), tok(D_KV), tok(D_KV), cache, cache,
            _const_spec(b_c.shape), _const_spec(b_n.shape), _const_spec(sink.shape),
        ],
        out_specs=(pl.BlockSpec((rows, D_ATTN), lambda i: (i, 0)), cache, cache),
        out_shape=(
            jax.ShapeDtypeStruct((dec_batch * s_len, D_ATTN), BF16),
            jax.ShapeDtypeStruct((dec_batch, w, D_KV), F32),
            jax.ShapeDtypeStruct((dec_batch, w, D_KV), F32),
        ),
        compiler_params=_params(("parallel",)),
        name="attn_sample",
    )(q_all, k_all, v_all, cache_k, cache_v, b_c, b_n, sink)


def _softplus(z):
    return jnp.maximum(z, 0.0) + jnp.log1p(jnp.exp(-jnp.abs(z)))


def _block_gate(xcb, w_ref, b_ref):
    parts = [jnp.dot(xcb[:, n * RNN_BLOCK:(n + 1) * RNN_BLOCK], w_ref[n], preferred_element_type=F32)
             for n in range(N_RNN_BLOCKS)]
    return jax.nn.sigmoid(jnp.concatenate(parts, axis=-1) + b_ref[...])


def _lru_coeffs(xc, wa_ref, ba_ref, wx_ref, bx_ref, lam_ref, first_row_unnormalised):
    xcb = xc.astype(BF16)
    r = _block_gate(xcb, wa_ref, ba_ref)
    i = _block_gate(xcb, wx_ref, bx_ref)
    log_a = -RG_C * r * _softplus(-lam_ref[...])
    a = jnp.exp(log_a)
    mult = jnp.sqrt(-jnp.tanh(log_a) * (a * a + 1.0))
    if first_row_unnormalised is not None:
        mult = jnp.where(first_row_unnormalised, 1.0, mult)
    return a, mult * i * xc


def _scan8(a, b):
    r8 = lax.broadcasted_iota(jnp.int32, a.shape, 0) % SUBLANES
    d = 1
    while d < SUBLANES:
        keep = r8 >= d
        a_sh = jnp.where(keep, pltpu.roll(a, d, 0), 1.0)
        b_sh = jnp.where(keep, pltpu.roll(b, d, 0), 0.0)
        b = a * b_sh + b
        a = a * a_sh
        d *= 2
    return a, b


def _rnn_prompt_kernel(xr_ref, gy_ref, cw_ref, cb_ref, wa_ref, ba_ref, wx_ref, bx_ref, lam_ref,
                       o_ref, nh_ref, ext_ref, a_ref, b_ref, hc_ref, *, tl):
    l = pl.program_id(1)

    @pl.when(l == 0)
    def _():
        ext_ref[0:SUBLANES, :] = jnp.zeros((SUBLANES, D_RNN), F32)
        hc_ref[...] = jnp.zeros((1, D_RNN), F32)

    x = xr_ref[...]
    ext_ref[SUBLANES:, :] = x
    xc = cb_ref[...] + cw_ref[CONV_W - 1:CONV_W, :] * x
    for j in range(1, CONV_W):
        xc = xc + cw_ref[CONV_W - 1 - j:CONV_W - j, :] * ext_ref[SUBLANES - j:SUBLANES - j + tl, :]
    ext_ref[0:SUBLANES, :] = x[tl - SUBLANES:, :]

    row = lax.broadcasted_iota(jnp.int32, (tl, D_RNN), 0)
    a, b = _lru_coeffs(xc, wa_ref, ba_ref, wx_ref, bx_ref, lam_ref, (row == 0) & (l == 0))
    a, b = _scan8(a, b)
    a_ref[...] = a
    b_ref[...] = b

    def chunk(c, h):
        sl = pl.ds(pl.multiple_of(c * SUBLANES, SUBLANES), SUBLANES)
        hc = b_ref[sl, :] + a_ref[sl, :] * h
        b_ref[sl, :] = hc
        return hc[SUBLANES - 1:SUBLANES, :]

    h = lax.fori_loop(0, tl // SUBLANES, chunk, hc_ref[...])
    hc_ref[...] = h
    nh_ref[0] = h
    o_ref[...] = (b_ref[...] * gy_ref[...]).astype(BF16)


def _rnn_prompt(xr_all, gy_all, rnn_w, batch, seq):
    tl =_pick_tile(seq, (256, 128, 64, 32, 16, 8))
    nl = seq // tl
    tok = pl.BlockSpec((tl, D_RNN), lambda b, l: (b * nl + l, 0))
    return pl.pallas_call(
        functools.partial(_rnn_prompt_kernel, tl=tl),
        grid=(batch, nl),
        in_specs=[tok, tok] + [_const_spec(w.shape) for w in rnn_w],
        out_specs=(tok, pl.BlockSpec((1, 1, D_RNN), lambda b, l: (b, 0, 0))),
        out_shape=(jax.ShapeDtypeStruct((batch * seq, D_RNN), BF16), jax.ShapeDtypeStruct((batch, 1, D_RNN), F32)),
        scratch_shapes=[
            pltpu.VMEM((tl + SUBLANES, D_RNN), F32),
            pltpu.VMEM((tl, D_RNN), F32),
            pltpu.VMEM((tl, D_RNN), F32),
            pltpu.VMEM((1, D_RNN), F32),
        ],
        compiler_params=_params(("parallel", "arbitrary")),
        name="rnn_prompt",
    )(xr_all, gy_all, *rnn_w)


def _rnn_sample_kernel(xr_ref, gy_ref, hp_ref, h0_ref, cw_ref, cb_ref, wa_ref, ba_ref, wx_ref, bx_ref, lam_ref,
                       o_ref, nh_ref, *, seqs):
    rows = seqs * SUBLANES
    x = xr_ref[...]
    hp = hp_ref[...]
    r8 = lax.broadcasted_iota(jnp.int32, (rows, D_RNN), 0) % SUBLANES
    xc = cb_ref[...] + cw_ref[CONV_W - 1:CONV_W, :] * x
    for j in range(1, CONV_W):
        shifted = jnp.where(r8 >= j, pltpu.roll(x, j, 0), pltpu.roll(hp, rows - (SUBLANES - j), 0))
        xc = xc + cw_ref[CONV_W - 1 - j:CONV_W - j, :] * shifted
    a, b = _lru_coeffs(xc, wa_ref, ba_ref, wx_ref, bx_ref, lam_ref, None)
    a, b = _scan8(a, b)
    h0 = jnp.broadcast_to(h0_ref[...][:, None, :], (seqs, SUBLANES, D_RNN)).reshape(rows, D_RNN)
    h = b + a * h0
    last = jnp.where(r8 == SUBLANES - 1, h, 0.0).reshape(seqs, SUBLANES, D_RNN)
    nh_ref[...] = jnp.sum(last, axis=1)
    o_ref[...] = (h * gy_ref[...]).astype(BF16)


def _rnn_sample(xr_all, gy_all, hist_pad, h0, rnn_w, row0, dec_batch):
    seqs = _pick_tile(dec_batch, (16, 8))
    rows = seqs * SUBLANES
    assert row0 % rows == 0
    blk0 = row0 // rows
    tok = pl.BlockSpec((rows, D_RNN), lambda i: (blk0 + i, 0))
    return pl.pallas_call(
        functools.partial(_rnn_sample_kernel, seqs=seqs),
        grid=(dec_batch // seqs,),
        in_specs=[tok, tok, pl.BlockSpec((rows, D_RNN), lambda i: (i, 0)), pl.BlockSpec((seqs, D_RNN), lambda i: (i, 0))]
        + [_const_spec(w.shape) for w in rnn_w],
        out_specs=(pl.BlockSpec((rows, D_RNN), lambda i: (i, 0)), pl.BlockSpec((seqs, D_RNN), lambda i: (i, 0))),
        out_shape=(jax.ShapeDtypeStruct((dec_batch * SUBLANES, D_RNN), BF16),
                   jax.ShapeDtypeStruct((dec_batch, D_RNN), F32)),
        compiler_params=_params(("parallel",)),
        name="rnn_sample",
    )(xr_all, gy_all, hist_pad, h0, *rnn_w)


def _layer_norm(z, g, b):
    mu = jnp.mean(z, axis=-1, keepdims=True)
    zc = z - mu
    var = jnp.mean(zc * zc, axis=-1, keepdims=True)
    return zc * lax.rsqrt(var + LN_EPS) * g + b


def _first_index_of_max(vals, iota, axis, sentinel):
    mx = jnp.max(vals, axis=axis, keepdims=True)
    return mx, jnp.min(jnp.where(vals == mx, iota, sentinel), axis=axis, keepdims=True)


def _route(scores, bias):
    t = scores.shape[1]
    grp = scores + bias
    g3 = grp.reshape(N_GROUPS, GROUP_SIZE, t)
    e_in_g = lax.broadcasted_iota(jnp.int32, g3.shape, 1)
    m1, first = _first_index_of_max(g3, e_in_g, 1, GROUP_SIZE)
    m2 = jnp.max(jnp.where(e_in_g == first, -jnp.inf, g3), axis=1, keepdims=True)
    gscore = (m1 + m2).reshape(N_GROUPS, t)
    g_iota = lax.broadcasted_iota(jnp.int32, gscore.shape, 0)
    gmask = jnp.zeros(gscore.shape, jnp.bool_)
    for _ in range(TOPK_GROUPS):
        _, gi = _first_index_of_max(gscore, g_iota, 0, N_GROUPS)
        hit = g_iota == gi
        gmask = gmask | hit
        gscore = jnp.where(hit, -jnp.inf, gscore)
    masked = jnp.where(gmask[:, None, :], g3, -jnp.inf).reshape(N_EXPERTS, t)
    e_iota = lax.broadcasted_iota(jnp.int32, masked.shape, 0)
    idx, wts, hits = [], [], []
    for _ in range(TOP_K):
        _, ei = _first_index_of_max(masked, e_iota, 0, N_EXPERTS)
        hit = e_iota == ei
        idx.append(ei)
        hits.append(hit)
        wts.append(jnp.sum(jnp.where(hit, scores, 0.0), axis=0, keepdims=True))
        masked = jnp.where(hit, -jnp.inf, masked)
    idx = jnp.concatenate(idx, axis=0)
    w = jnp.concatenate(wts, axis=0)
    w = w / jnp.sum(w, axis=0, keepdims=True) * ROUTED_SCALE
    return idx, w, hits


def _merge_kernel(xp_ref, xs_ref, aop_ref, aos_ref, rop_ref, ros_ref, sga_ref, sgr_ref, woa_ref, wor_ref, wout_ref, g1_ref, b1_ref,
                  wrt_ref, rb_ref, wsg_ref, wsu_ref, wsd_ref,
                  x1r_ref, base_ref, idx_ref, wt_ref, rank_ref, cnt_ref, carry_ref, *, n_p, tm):
    i = pl.program_id(0)

    @pl.when(i == 0)
    def _():
        carry_ref[...] = jnp.zeros(carry_ref.shape, F32)

    is_prompt = i < n_p
    x = jnp.where(is_prompt, xp_ref[...], xs_ref[...])
    pa = jnp.dot(jnp.where(is_prompt, aop_ref[...], aos_ref[...]), woa_ref[...], preferred_element_type=F32)
    pr = jnp.dot(jnp.where(is_prompt, rop_ref[...], ros_ref[...]), wor_ref[...], preferred_element_type=F32)
    merged = sga_ref[...] * pa + sgr_ref[...] * pr
    z = ALPHA * x + jnp.dot(merged.astype(BF16), wout_ref[...], preferred_element_type=F32)
    x1 = _layer_norm(z, g1_ref[...], b1_ref[...])
    _store_token_rows(x1r_ref, x1, tm)
    x1b = x1.astype(BF16)
    u = jax.nn.silu(jnp.dot(x1b, wsg_ref[...], preferred_element_type=F32)) * jnp.dot(x1b, wsu_ref[...], preferred_element_type=F32)
    shared = jnp.dot(u.astype(BF16), wsd_ref[...], preferred_element_type=F32)
    base_ref[...] = ALPHA * x1 + shared
    logits = lax.dot_general(wrt_ref[...], x1, (((1,), (1,)), ((), ())), preferred_element_type=F32)
    idx, w, hits = _route(jax.nn.sigmoid(logits), rb_ref[...])
    idx_ref[...] = idx
    wt_ref[...] = w

    chosen = functools.reduce(jnp.logical_or, hits)
    chosen_f = jnp.where(chosen, 1.0, 0.0)
    earlier = (lax.broadcasted_iota(jnp.int32, (tm, tm), 0) < lax.broadcasted_iota(jnp.int32, (tm, tm), 1))
    prefix = jnp.dot(chosen_f.astype(BF16), jnp.where(earlier, 1.0, 0.0).astype(BF16), preferred_element_type=F32)
    before = prefix + carry_ref[...]
    ranks = [jnp.sum(jnp.where(hit, before, 0.0), axis=0, keepdims=True) for hit in hits]
    rank_ref[...] = jnp.concatenate(ranks, axis=0).astype(jnp.int32)
    carry_ref[...] = carry_ref[...] + jnp.sum(chosen_f, axis=1, keepdims=True)
    cnt_ref[...] = carry_ref[...]


def _merge(x_p, x_s, attn_p, attn_s, rnn_p, rnn_s, sga, sgr, weights):
    t_p, t_s = x_p.shape[0], x_s.shape[0]
    t = t_p + t_s
    tm = _pick_tile(math.gcd(t_p, t_s), (256, 128))
    row = lambda width: pl.BlockSpec((tm, width), lambda i: (i, 0))
    col = pl.BlockSpec((TOP_K, tm), lambda i: (0, i))

    return pl.pallas_call(
        functools.partial(_merge_kernel, n_p=t_p // tm, tm=tm),
        grid=(t // tm,),
        in_specs=[*_two_source_specs(t_p, t_s, tm, D_MODEL), *_two_source_specs(t_p, t_s, tm, D_ATTN),
                  *_two_source_specs(t_p, t_s, tm, D_RNN), row(D_MODEL), row(D_MODEL)]
        + [_const_spec(w.shape) for w in weights],
        out_specs=(pl.BlockSpec((tm * ROW_TILE, LANES), lambda i: (i, 0)), row(D_MODEL), col, col, col,
                   _const_spec((N_EXPERTS, 1))),
        out_shape=(
            jax.ShapeDtypeStruct((t * ROW_TILE, LANES), F32),
            jax.ShapeDtypeStruct((t, D_MODEL), F32),
            jax.ShapeDtypeStruct((TOP_K, t), jnp.int32),
            jax.ShapeDtypeStruct((TOP_K, t), F32),
            jax.ShapeDtypeStruct((TOP_K, t), jnp.int32),
            jax.ShapeDtypeStruct((N_EXPERTS, 1), F32),
        ),
        scratch_shapes=[pltpu.VMEM((N_EXPERTS, 1), F32)],
        compiler_params=_params(("arbitrary",)),
        name="merge_ln1_route",
    )(x_p, x_s, attn_p, attn_s, rnn_p, rnn_s, sga, sgr, *weights)


def _expert_layout(counts, n_assign):
    counts = counts.reshape(N_EXPERTS).astype(jnp.int32)
    padded = (counts + EXPERT_ROWS - 1) // EXPERT_ROWS * EXPERT_ROWS
    pend = jnp.cumsum(padded)
    pstart = pend - padded
    n_blocks = (n_assign + N_EXPERTS * (EXPERT_ROWS - 1) + EXPERT_ROWS - 1) // EXPERT_ROWS
    block_start = jnp.arange(n_blocks, dtype=jnp.int32) * EXPERT_ROWS
    block_e = jnp.sum((pend[None, :] <= block_start[:, None]).astype(jnp.int32), axis=1)
    block_e = jnp.minimum(block_e, N_EXPERTS - 1)
    block_used = (block_start < pend[-1]).astype(jnp.int32)
    block_first = jnp.concatenate([jnp.ones((1,), jnp.int32), (block_e[1:] != block_e[:-1]).astype(jnp.int32)])
    next_first = jnp.concatenate([block_first[1:], jnp.ones((1,), jnp.int32)])
    next_unused = jnp.concatenate([1 - block_used[1:], jnp.ones((1,), jnp.int32)])
    zero_block = jnp.maximum(jnp.maximum(next_first, next_unused), 1 - block_used)
    return pstart, zero_block, n_blocks, block_e, block_used, block_first


def _dest_kernel(idx_ref, rank_ref, pstart_ref, dest_ref):
    e_iota = lax.broadcasted_iota(jnp.int32, (N_EXPERTS, idx_ref.shape[1]), 0)
    starts = [jnp.sum(jnp.where(e_iota == idx_ref[k:k + 1, :], pstart_ref[...], 0), axis=0, keepdims=True)
              for k in range(TOP_K)]
    dest_ref[...] = jnp.concatenate(starts, axis=0) + rank_ref[...]


def _dest_rows(idx_t, rank_t, pstart):
    t = idx_t.shape[1]
    tm = _pick_tile(t, (512, 256, 128))
    col = pl.BlockSpec((TOP_K, tm), lambda i: (0, i))
    return pl.pallas_call(
        _dest_kernel,
        grid=(t // tm,),
        in_specs=[col, col, _const_spec((N_EXPERTS, 1))],
        out_specs=col,
        out_shape=jax.ShapeDtypeStruct((TOP_K, t), jnp.int32),
        compiler_params=_params(("parallel",)),
        name="dest_rows",
    )(idx_t, rank_t, pstart.reshape(N_EXPERTS, 1))


def _token_rows(r, n=1):
    return pl.ds(pl.multiple_of(r * ROW_TILE, ROW_TILE), n * ROW_TILE)


def _dispatch_kernel(zero_block_ref, dest_ref, x_ref, xs_ref, zero_ref, sem, *, tm, n_blocks):
    @pl.when(pl.program_id(0) == 0)
    def _():
        zero_ref[...] = jnp.zeros(zero_ref.shape, F32)
        zero_copy = lambda b: pltpu.make_async_copy(zero_ref, xs_ref.at[_token_rows(b * EXPERT_ROWS, EXPERT_ROWS)], sem)

        def start(b, n):
            @pl.when(zero_block_ref[b] != 0)
            def _():
                zero_copy(b).start()

            return n + zero_block_ref[b]

        def wait(_, c):
            zero_copy(0).wait()
            return c

        lax.fori_loop(0, lax.fori_loop(0, n_blocks, start, 0), wait, 0)

    def issue(t, c):
        for k in range(TOP_K):
            pltpu.make_async_copy(x_ref.at[_token_rows(t)], xs_ref.at[_token_rows(dest_ref[k, t])], sem).start()
        return c

    lax.fori_loop(0, tm, issue, 0)
    for _ in range(TOP_K):
        pltpu.make_async_copy(x_ref, xs_ref.at[_token_rows(0, tm)], sem).wait()


def _dispatch(x1r, dest, zero_block, n_blocks):
    t = dest.shape[1]
    tm = _pick_tile(t, (512, 256, 128))
    rows = n_blocks * EXPERT_ROWS
    grid_spec = pltpu.PrefetchScalarGridSpec(
        num_scalar_prefetch=1,
        grid=(t // tm,),
        in_specs=[
            pl.BlockSpec((TOP_K, tm), lambda i, ps: (0, i), memory_space=pltpu.SMEM),
            pl.BlockSpec((tm * ROW_TILE, LANES), lambda i, ps: (i, 0)),
        ],
        out_specs=pl.BlockSpec(memory_space=pl.ANY),
        scratch_shapes=[pltpu.VMEM((EXPERT_ROWS * ROW_TILE, LANES), F32), pltpu.SemaphoreType.DMA],
    )
    return pl.pallas_call(
        functools.partial(_dispatch_kernel, tm=tm, n_blocks=n_blocks),
        grid_spec=grid_spec,
        out_shape=jax.ShapeDtypeStruct((rows * ROW_TILE, LANES), F32),
        compiler_params=_params(("arbitrary",)),
        name="dispatch",
    )(zero_block, dest, x1r)


def _expert_kernel(be_ref, bu_ref, bf_ref, xs_ref, wg_ref, wu_ref, wd_ref, o_ref, wgb_ref, wub_ref, wdb_ref):
    del be_ref
    i = pl.program_id(0)

    @pl.when(bf_ref[i] != 0)
    def _():
        wgb_ref[...] = wg_ref[...].astype(BF16)
        wub_ref[...] = wu_ref[...].astype(BF16)
        wdb_ref[...] = wd_ref[...].astype(BF16)

    @pl.when(bu_ref[i] != 0)
    def _():
        xb = jnp.concatenate(_load_token_rows(xs_ref, EXPERT_ROWS), axis=1).astype(BF16)
        g = jnp.dot(xb, wgb_ref[...], preferred_element_type=F32)
        u = jnp.dot(xb, wub_ref[...], preferred_element_type=F32)
        act = (jax.nn.silu(g) * u).astype(BF16)
        _store_token_rows(o_ref, jnp.dot(act, wdb_ref[...], preferred_element_type=F32), EXPERT_ROWS)

    @pl.when(bu_ref[i] == 0)
    def _():
        o_ref[...] = jnp.zeros(o_ref.shape, F32)


def _expert_ffn(xs, n_blocks, block_e, block_used, block_first, w_e_gate, w_e_up, w_e_down):
    rows_blk = pl.BlockSpec((EXPERT_ROWS * ROW_TILE, LANES), lambda i, be, bu, bf: (i, 0))
    weight = lambda shape: pl.BlockSpec((None, *shape), lambda i, be, bu, bf: (be[i], 0, 0))
    grid_spec = pltpu.PrefetchScalarGridSpec(
        num_scalar_prefetch=3,
        grid=(n_blocks,),
        in_specs=[rows_blk, weight((D_MODEL, D_EXPERT)), weight((D_MODEL, D_EXPERT)), weight((D_EXPERT, D_MODEL))],
        out_specs=rows_blk,
        scratch_shapes=[pltpu.VMEM((D_MODEL, D_EXPERT), BF16), pltpu.VMEM((D_MODEL, D_EXPERT), BF16),
                        pltpu.VMEM((D_EXPERT, D_MODEL), BF16)],
    )
    return pl.pallas_call(
        _expert_kernel,
        grid_spec=grid_spec,
        out_shape=jax.ShapeDtypeStruct((n_blocks * EXPERT_ROWS * ROW_TILE, LANES), F32),
        compiler_params=_params(("arbitrary",)),
        name="expert_ffn",
    )(block_e, block_used, block_first, xs, w_e_gate, w_e_up, w_e_down)


def _combine_kernel(dest_ref, w_ref, base_ref, g_ref, b_ref, outs_ref, yp_ref, ys_ref, buf_ref, sem, *, tm, n_p):
    i = pl.program_id(0)

    def issue(t, c):
        for k in range(TOP_K):
            pltpu.make_async_copy(outs_ref.at[_token_rows(dest_ref[k, t])], buf_ref.at[_token_rows(k * tm + t)], sem).start()
        return c

    lax.fori_loop(0, tm, issue, 0)
    for k in range(TOP_K):
        pltpu.make_async_copy(outs_ref.at[_token_rows(0, tm)], buf_ref.at[_token_rows(k * tm, tm)], sem).wait()

    w = w_ref[...]
    chunks = [None] * ROW_TILE
    for k in range(TOP_K):
        wk = w[:, k:k + 1]
        for s, rows in enumerate(_load_token_rows(buf_ref, tm, row0=k * tm)):
            chunks[s] = wk * rows if chunks[s] is None else chunks[s] + wk * rows
    y = _layer_norm(base_ref[...] + jnp.concatenate(chunks, axis=1), g_ref[...], b_ref[...])

    @pl.when(i < n_p)
    def _():
        yp_ref[...] = y

    @pl.when(i >= n_p)
    def _():
        ys_ref[...] = y


def _combine(out_sorted, dest, w_tok, base, g2, b2, t_p, t_s):
    t = t_p + t_s
    tm = _pick_tile(math.gcd(t_p, t_s), (256, 128))
    row = pl.BlockSpec((tm, D_MODEL), lambda i: (i, 0))
    out_p, out_s = _two_source_specs(t_p, t_s, tm, D_MODEL)
    return pl.pallas_call(
        functools.partial(_combine_kernel, tm=tm, n_p=t_p // tm),
        grid=(t // tm,),
        in_specs=[
            pl.BlockSpec((TOP_K, tm), lambda i: (0, i), memory_space=pltpu.SMEM),
            pl.BlockSpec((tm, TOP_K), lambda i: (i, 0)),
            row, _const_spec((1, D_MODEL)), _const_spec((1, D_MODEL)),
            pl.BlockSpec(memory_space=pl.ANY),
        ],
        out_specs=(out_p, out_s),
        out_shape=(jax.ShapeDtypeStruct((t_p, D_MODEL), F32), jax.ShapeDtypeStruct((t_s, D_MODEL), F32)),
        scratch_shapes=[pltpu.VMEM((TOP_K * tm * ROW_TILE, LANES), F32), pltpu.SemaphoreType.DMA],
        compiler_params=_params(("arbitrary",)),
        name="combine_ln2",
    )(dest, w_tok, base, g2, b2, out_sorted)


def kernel(x_prompt, x_sample, cache_k, cache_v, state_conv, state_rnn, w_in, conv_w, conv_b, w_gate_a, b_gate_a, w_gate_x, b_gate_x, lru_lambda, rel_bias, sinks, w_o_attn, w_o_rnn, w_out, ln1_g, ln1_b, w_router, router_bias, w_e_gate, w_e_up, w_e_down, w_s_gate, w_s_up, w_s_down, ln2_g, ln2_b):
    assert w_in.shape[0] == DEPTH == 1
    batch, seq, _ = x_prompt.shape
    dec_batch, s_len, _ = x_sample.shape
    w_cache = cache_k.shape[2]
    assert s_len == SUBLANES and seq % WINDOW == 0 and w_cache == WINDOW
    t_p = batch * seq
    t_s = dec_batch * s_len
    vec = lambda a: a[0].reshape(1, -1).astype(F32)

    x_p = x_prompt.reshape(t_p, D_MODEL)
    x_s = x_sample.reshape(t_s, D_MODEL)
    q, k, v, xr, gy, sga, sgr = _inproj(x_p, x_s, w_in[0].astype(BF16))

    qi = jnp.arange(WINDOW)
    dist = qi[:, None] + WINDOW - jnp.arange(2 * WINDOW)[None, :]
    bias_p = _bias_table(rel_bias, dist)
    attn_p = _attn_prompt(q, k, v, bias_p, sinks[0], batch, seq)
    attn_s, k_s, v_s = _attn_sample(
        q, k, v, cache_k[0].reshape(dec_batch, w_cache, D_KV), cache_v[0].reshape(dec_batch, w_cache, D_KV),
        rel_bias, sinks[0], t_p, dec_batch, s_len)

    rnn_w = (conv_w[0], vec(conv_b), w_gate_a[0].astype(BF16), vec(b_gate_a), w_gate_x[0].astype(BF16),
             vec(b_gate_x), vec(lru_lambda))
    rnn_p, h_p = _rnn_prompt(xr, gy, rnn_w, batch, seq)
    hist_pad = jnp.pad(state_conv[0], ((0, 0), (SUBLANES - (CONV_W - 1), 0), (0, 0))).reshape(t_s, D_RNN)
    rnn_s, h_s = _rnn_sample(xr, gy, hist_pad, state_rnn[0], rnn_w, t_p, dec_batch)

    merge_w = (w_o_attn[0].astype(BF16), w_o_rnn[0].astype(BF16), w_out[0].astype(BF16), vec(ln1_g), vec(ln1_b),
               w_router[0].T, router_bias[0].reshape(N_EXPERTS, 1), w_s_gate[0].astype(BF16),
               w_s_up[0].astype(BF16), w_s_down[0].astype(BF16))
    x1r, base, idx_t, wt_t, rank_t, counts = _merge(x_p, x_s, attn_p, attn_s, rnn_p, rnn_s, sga, sgr, merge_w)

    pstart, zero_block, n_blocks, block_e, block_used, block_first = _expert_layout(counts, (t_p + t_s) * TOP_K)
    dest = _dest_rows(idx_t, rank_t, pstart)
    xs = _dispatch(x1r, dest, zero_block, n_blocks)
    out_sorted = _expert_ffn(xs, n_blocks, block_e, block_used, block_first, w_e_gate[0], w_e_up[0], w_e_down[0])
    y_p, y_s = _combine(out_sorted, dest, wt_t.T, base, vec(ln2_g), vec(ln2_b), t_p, t_s)
    y_p = y_p.reshape(batch, seq, D_MODEL)
    y_s = y_s.reshape(dec_batch, s_len, D_MODEL)
    kv5 = lambda a, b: a.reshape(1, b, WINDOW, N_KV_HEADS, HEAD_DIM)
    k_p = kv5(k[:t_p].reshape(batch, seq, D_KV)[:, seq - WINDOW:], batch)
    v_p = kv5(v[:t_p].reshape(batch, seq, D_KV)[:, seq - WINDOW:], batch)
    conv_p = xr[:t_p].reshape(batch, seq, D_RNN)[:, seq - (CONV_W - 1):][None]
    conv_s = xr[t_p:].reshape(dec_batch, s_len, D_RNN)[:, s_len - (CONV_W - 1):][None]
    return (y_p, y_s, k_p, v_p, conv_p, h_p.reshape(1, batch, D_RNN),
            kv5(k_s, dec_batch), kv5(v_s, dec_batch), conv_s, h_s.reshape(1, dec_batch, D_RNN))
```

```python
import functools
import math

import jax
import jax.numpy as jnp
from jax import lax
from jax.experimental import pallas as pl
from jax.experimental.pallas import tpu as pltpu

F32 = jnp.float32
BF16 = jnp.bfloat16

D_MODEL = 1024
N_HEADS = 8
N_KV_HEADS = 2
HEAD_DIM = 64
GROUP = N_HEADS // N_KV_HEADS
WINDOW = 128
D_ATTN = N_HEADS * HEAD_DIM
D_KV = N_KV_HEADS * HEAD_DIM
N_BUCKETS = 32
MAX_DISTANCE = 128
D_RNN = D_MODEL
RNN_BLOCK = 256
N_RNN_BLOCKS = D_RNN // RNN_BLOCK
CONV_W = 4
RG_C = 8.0
N_EXPERTS = 256
TOP_K = 8
N_GROUPS = 8
GROUP_SIZE = N_EXPERTS // N_GROUPS
TOPK_GROUPS = 4
D_EXPERT = D_MODEL // 4
ROUTED_SCALE = 2.5
LN_EPS = 1e-5
DEPTH = 1
ALPHA = (2 * DEPTH) ** 0.25
NEG_INF = -1e30
SM_SCALE = HEAD_DIM ** -0.5

O_Q = 0
O_K = D_ATTN
O_V = O_K + D_KV
O_XR = O_V + D_KV
O_YR = O_XR + D_RNN
O_GA = O_YR + D_RNN
O_GR = O_GA + D_MODEL
D_IN = O_GR + D_MODEL

SUBLANES = 8
VMEM_LIMIT_BYTES = 56 * 1024 * 1024
EXPERT_ROWS = 256
EXPERT_CHUNK = 512
ROW_PAD = SUBLANES


def _params(sem):
    return pltpu.CompilerParams(dimension_semantics=sem, vmem_limit_bytes=VMEM_LIMIT_BYTES)


def _pick_tile(n, candidates):
    for c in candidates:
        if n % c == 0:
            return c
    raise ValueError(f"no tile for {n}")


def _const_spec(shape):
    nd = len(shape)
    return pl.BlockSpec(shape, lambda *_: (0,) * nd)


LANES = 128
ROW_TILE = D_MODEL // LANES
assert ROW_TILE == SUBLANES


def _store_token_rows(ref, mat, n, row0=0, base=0):
    for s in range(ROW_TILE):
        ref[pl.ds(base + row0 * ROW_TILE + s, n, stride=ROW_TILE), :] = mat[:, s * LANES:(s + 1) * LANES]


def _load_token_rows(ref, n, row0=0, base=0):
    return [ref[pl.ds(base + row0 * ROW_TILE + s, n, stride=ROW_TILE), :] for s in range(ROW_TILE)]


def _two_source_specs(t_p, t_s, tm, width):
    n_p = t_p // tm
    assert t_p % tm == 0 and t_s % tm == 0
    return (pl.BlockSpec((tm, width), lambda i, *_: (jnp.minimum(i, n_p - 1), 0)),
            pl.BlockSpec((tm, width), lambda i, *_: (jnp.maximum(i - n_p, 0), 0)))


def _inproj_kernel(xp_ref, xs_ref, w_ref, q_ref, k_ref, v_ref, xr_ref, gy_ref, sga_ref, sgr_ref, *, n_p):
    x = jnp.where(pl.program_id(0) < n_p, xp_ref[...], xs_ref[...]).astype(BF16)

    def seg(lo, hi):
        return jnp.dot(x, w_ref[:, lo:hi], preferred_element_type=F32)

    q_ref[...] = seg(O_Q, O_K).astype(BF16)
    k_ref[...] = seg(O_K, O_V)
    v_ref[...] = seg(O_V, O_XR)
    xr_ref[...] = seg(O_XR, O_YR)
    gy_ref[...] = jax.nn.gelu(seg(O_YR, O_GA)).astype(BF16)
    sga_ref[...] = jax.nn.sigmoid(seg(O_GA, O_GR)).astype(BF16)
    sgr_ref[...] = jax.nn.sigmoid(seg(O_GR, D_IN)).astype(BF16)


def _inproj(x_p, x_s, w_in_bf16):
    t_p, t_s = x_p.shape[0], x_s.shape[0]
    t = t_p + t_s
    tm = _pick_tile(math.gcd(t_p, t_s), (256, 128, 64, 32, 16, 8))
    row = lambda width: pl.BlockSpec((tm, width), lambda i: (i, 0))
    out_shape = (
        jax.ShapeDtypeStruct((t, D_ATTN), BF16),
        jax.ShapeDtypeStruct((t, D_KV), F32),
        jax.ShapeDtypeStruct((t, D_KV), F32),
        jax.ShapeDtypeStruct((t, D_RNN), F32),
        jax.ShapeDtypeStruct((t, D_RNN), BF16),
        jax.ShapeDtypeStruct((t, D_MODEL), BF16),
        jax.ShapeDtypeStruct((t, D_MODEL), BF16),
    )
    return pl.pallas_call(
        functools.partial(_inproj_kernel, n_p=t_p // tm),
        grid=(t // tm,),
        in_specs=[*_two_source_specs(t_p, t_s, tm, D_MODEL), _const_spec((D_MODEL, D_IN))],
        out_specs=(row(D_ATTN), row(D_KV), row(D_KV), row(D_RNN), row(D_RNN), row(D_MODEL), row(D_MODEL)),
        out_shape=out_shape,
        compiler_params=_params(("parallel",)),
        name="inproj",
    )(x_p, x_s, w_in_bf16)


def _t5_bucket(dist):
    n = jnp.maximum(dist, 0)
    max_exact = N_BUCKETS // 2
    nf = jnp.maximum(n, 1).astype(F32)
    large = max_exact + (jnp.log(nf / max_exact) / math.log(MAX_DISTANCE / max_exact) * (N_BUCKETS - max_exact)).astype(jnp.int32)
    large = jnp.minimum(large, N_BUCKETS - 1)
    return jnp.where(n < max_exact, n, large)


def _bias_table(rel_bias, dist):
    bucket = _t5_bucket(dist)
    rb = rel_bias.astype(F32)
    out = jnp.zeros((N_HEADS, *dist.shape), F32)
    for j in range(N_BUCKETS):
        out = jnp.where(bucket[None] == j, rb[j][:, None, None], out)
    return out


def _softmax_pv(s, sink, v):
    m = jnp.maximum(jnp.max(s, axis=-1, keepdims=True), sink)
    p = jnp.exp(s - m)
    denom = jnp.sum(p, axis=-1, keepdims=True) + jnp.exp(sink - m)
    return jnp.dot(p.astype(BF16), v, preferred_element_type=F32), denom


def _attn_prompt_kernel(sink_ref, q_ref, kc_ref, kp_ref, vc_ref, vp_ref, bias_ref, o_ref):
    n = pl.program_id(1)
    kk = jnp.concatenate([kp_ref[...], kc_ref[...]], axis=0).astype(BF16)
    vv = jnp.concatenate([vp_ref[...], vc_ref[...]], axis=0).astype(BF16)
    q = q_ref[...]
    rows = lax.broadcasted_iota(jnp.int32, (WINDOW, 2 * WINDOW), 0)
    cols = lax.broadcasted_iota(jnp.int32, (WINDOW, 2 * WINDOW), 1)
    dist = rows + WINDOW - cols
    valid = (dist >= 0) & (dist <= WINDOW) & ((n > 0) | (cols >= WINDOW))
    for g in range(N_KV_HEADS):
        kg = kk[:, g * HEAD_DIM:(g + 1) * HEAD_DIM]
        vg = vv[:, g * HEAD_DIM:(g + 1) * HEAD_DIM]
        for h in range(GROUP):
            hh = g * GROUP + h
            qh = q[:, hh * HEAD_DIM:(hh + 1) * HEAD_DIM]
            s = lax.dot_general(qh, kg, (((1,), (1,)), ((), ())), preferred_element_type=F32) * SM_SCALE
            s = jnp.where(valid, s + bias_ref[hh], NEG_INF)
            o, denom = _softmax_pv(s, sink_ref[0, hh], vg)
            o_ref[:, hh * HEAD_DIM:(hh + 1) * HEAD_DIM] = (o / denom).astype(BF16)


def _attn_prompt(q_all, k_all, v_all, bias, sinks, batch, seq):
    nb = seq // WINDOW
    cur = lambda width: pl.BlockSpec((WINDOW, width), lambda b, n: (b * nb + n, 0))
    prev = lambda width: pl.BlockSpec((WINDOW, width), lambda b, n: (b * nb + jnp.maximum(n - 1, 0), 0))
    return pl.pallas_call(
        _attn_prompt_kernel,
        grid=(batch, nb),
        in_specs=[
            pl.BlockSpec(memory_space=pltpu.SMEM),
            cur(D_ATTN), cur(D_KV), prev(D_KV), cur(D_KV), prev(D_KV),
            _const_spec((N_HEADS, WINDOW, 2 * WINDOW)),
        ],
        out_specs=cur(D_ATTN),
        out_shape=jax.ShapeDtypeStruct((batch * seq, D_ATTN), BF16),
        compiler_params=_params(("parallel", "arbitrary")),
        name="attn_prompt",
    )(sinks.reshape(1, N_HEADS).astype(F32), q_all, k_all, k_all, v_all, v_all, bias)


def _attn_sample_kernel(q_ref, kn_ref, vn_ref, kc_ref, vc_ref, bc_ref, bn_ref, sink_ref,
                        o_ref, ko_ref, vo_ref, *, seqs, s_len):
    w = kc_ref.shape[1]
    rows_c = lax.broadcasted_iota(jnp.int32, (GROUP * s_len, w), 0) % s_len
    cols_c = lax.broadcasted_iota(jnp.int32, (GROUP * s_len, w), 1)
    dist_c = rows_c + w - cols_c
    valid_c = (dist_c >= 0) & (dist_c <= WINDOW)
    rows_n = lax.broadcasted_iota(jnp.int32, (GROUP * s_len, s_len), 0) % s_len
    cols_n = lax.broadcasted_iota(jnp.int32, (GROUP * s_len, s_len), 1)
    dist_n = rows_n - cols_n
    valid_n = (dist_n >= 0) & (dist_n <= WINDOW)
    for j in range(seqs):
        r0 = j * s_len
        qj = q_ref[r0:r0 + s_len, :]
        kc = kc_ref[j]
        vc = vc_ref[j]
        kn = kn_ref[r0:r0 + s_len, :]
        vn = vn_ref[r0:r0 + s_len, :]
        ko_ref[j, 0:w - s_len, :] = kc[s_len:, :]
        ko_ref[j, w - s_len:w, :] = kn
        vo_ref[j, 0:w - s_len, :] = vc[s_len:, :]
        vo_ref[j, w - s_len:w, :] = vn
        kcb, vcb, knb, vnb = kc.astype(BF16), vc.astype(BF16), kn.astype(BF16), vn.astype(BF16)
        for g in range(N_KV_HEADS):
            lo, hi = g * HEAD_DIM, (g + 1) * HEAD_DIM
            qs = jnp.concatenate(
                [qj[:, (g * GROUP + h) * HEAD_DIM:(g * GROUP + h + 1) * HEAD_DIM] for h in range(GROUP)], axis=0)
            nt = (((1,), (1,)), ((), ()))
            s_c = lax.dot_general(qs, kcb[:, lo:hi], nt, preferred_element_type=F32) * SM_SCALE
            s_n = lax.dot_general(qs, knb[:, lo:hi], nt, preferred_element_type=F32) * SM_SCALE
            s_c = jnp.where(valid_c, s_c + bc_ref[g], NEG_INF)
            s_n = jnp.where(valid_n, s_n + bn_ref[g], NEG_INF)
            sink = sink_ref[g]
            m = jnp.maximum(jnp.maximum(jnp.max(s_c, axis=-1, keepdims=True), jnp.max(s_n, axis=-1, keepdims=True)), sink)
            p_c = jnp.exp(s_c - m)
            p_n = jnp.exp(s_n - m)
            denom = jnp.sum(p_c, axis=-1, keepdims=True) + jnp.sum(p_n, axis=-1, keepdims=True) + jnp.exp(sink - m)
            o = jnp.dot(p_c.astype(BF16), vcb[:, lo:hi], preferred_element_type=F32)
            o = o + jnp.dot(p_n.astype(BF16), vnb[:, lo:hi], preferred_element_type=F32)
            o = (o / denom).astype(BF16)
            for h in range(GROUP):
                hh = g * GROUP + h
                o_ref[r0:r0 + s_len, hh * HEAD_DIM:(hh + 1) * HEAD_DIM] = o[h * s_len:(h + 1) * s_len, :]


def _attn_sample(q_all, k_all, v_all, cache_k, cache_v, rel_bias, sinks, row0, dec_batch, s_len):
    w = cache_k.shape[1]
    seqs = _pick_tile(dec_batch, (16, 8, 4, 2, 1))
    rows = seqs * s_len
    blk0 = row0 // rows
    assert row0 % rows == 0
    qi = jnp.arange(s_len)
    dist_c = qi[:, None] + w - jnp.arange(w)[None, :]
    dist_n = qi[:, None] - jnp.arange(s_len)[None, :]
    b_c = _bias_table(rel_bias, dist_c).reshape(N_KV_HEADS, GROUP * s_len, w)
    b_n = _bias_table(rel_bias, dist_n).reshape(N_KV_HEADS, GROUP * s_len, s_len)
    sink = jnp.broadcast_to(sinks.astype(F32).reshape(N_KV_HEADS, GROUP, 1, 1), (N_KV_HEADS, GROUP, s_len, 1))
    sink = sink.reshape(N_KV_HEADS, GROUP * s_len, 1)
    tok = lambda width: pl.BlockSpec((rows, width), lambda i: (blk0 + i, 0))
    cache = pl.BlockSpec((seqs, w, D_KV), lambda i: (i, 0, 0))
    return pl.pallas_call(
        functools.partial(_attn_sample_kernel, seqs=seqs, s_len=s_len),
        grid=(dec_batch // seqs,),
        in_specs=[
            tok(D_ATTN), tok(D_KV), tok(D_KV), cache, cache,
            _const_spec(b_c.shape), _const_spec(b_n.shape), _const_spec(sink.shape),
        ],
        out_specs=(pl.BlockSpec((rows, D_ATTN), lambda i: (i, 0)), cache, cache),
        out_shape=(
            jax.ShapeDtypeStruct((dec_batch * s_len, D_ATTN), BF16),
            jax.ShapeDtypeStruct((dec_batch, w, D_KV), F32),
            jax.ShapeDtypeStruct((dec_batch, w, D_KV), F32),
        ),
        compiler_params=_params(("parallel",)),
        name="attn_sample",
    )(q_all, k_all, v_all, cache_k, cache_v, b_c, b_n, sink)


def _softplus(z):
    return jnp.maximum(z, 0.0) + jnp.log1p(jnp.exp(-jnp.abs(z)))


def _block_gate(xcb, w_ref, b_ref):
    parts = [jnp.dot(xcb[:, n * RNN_BLOCK:(n + 1) * RNN_BLOCK], w_ref[n], preferred_element_type=F32)
             for n in range(N_RNN_BLOCKS)]
    return jax.nn.sigmoid(jnp.concatenate(parts, axis=-1) + b_ref[...])


def _lru_coeffs(xc, wa_ref, ba_ref, wx_ref, bx_ref, lam_ref, first_row_unnormalised):
    xcb = xc.astype(BF16)
    r = _block_gate(xcb, wa_ref, ba_ref)
    i = _block_gate(xcb, wx_ref, bx_ref)
    log_a = -RG_C * r * _softplus(-lam_ref[...])
    a = jnp.exp(log_a)
    mult = jnp.sqrt(-jnp.tanh(log_a) * (a * a + 1.0))
    if first_row_unnormalised is not None:
        mult = jnp.where(first_row_unnormalised, 1.0, mult)
    return a, mult * i * xc


def _scan8(a, b):
    r8 = lax.broadcasted_iota(jnp.int32, a.shape, 0) % SUBLANES
    d = 1
    while d < SUBLANES:
        keep = r8 >= d
        a_sh = jnp.where(keep, pltpu.roll(a, d, 0), 1.0)
        b_sh = jnp.where(keep, pltpu.roll(b, d, 0), 0.0)
        b = a * b_sh + b
        a = a * a_sh
        d *= 2
    return a, b


def _rnn_prompt_kernel(xr_ref, gy_ref, cw_ref, cb_ref, wa_ref, ba_ref, wx_ref, bx_ref, lam_ref,
                       o_ref, nh_ref, ext_ref, a_ref, b_ref, hc_ref, *, tl):
    l = pl.program_id(1)

    @pl.when(l == 0)
    def _():
        ext_ref[0:SUBLANES, :] = jnp.zeros((SUBLANES, D_RNN), F32)
        hc_ref[...] = jnp.zeros((1, D_RNN), F32)

    x = xr_ref[...]
    ext_ref[SUBLANES:, :] = x
    xc = cb_ref[...] + cw_ref[CONV_W - 1:CONV_W, :] * x
    for j in range(1, CONV_W):
        xc = xc + cw_ref[CONV_W - 1 - j:CONV_W - j, :] * ext_ref[SUBLANES - j:SUBLANES - j + tl, :]
    ext_ref[0:SUBLANES, :] = x[tl - SUBLANES:, :]

    row = lax.broadcasted_iota(jnp.int32, (tl, D_RNN), 0)
    a, b = _lru_coeffs(xc, wa_ref, ba_ref, wx_ref, bx_ref, lam_ref, (row == 0) & (l == 0))
    a, b = _scan8(a, b)
    a_ref[...] = a
    b_ref[...] = b

    def chunk(c, h):
        sl = pl.ds(pl.multiple_of(c * SUBLANES, SUBLANES), SUBLANES)
        hc = b_ref[sl, :] + a_ref[sl, :] * h
        b_ref[sl, :] = hc
        return hc[SUBLANES - 1:SUBLANES, :]

    h = lax.fori_loop(0, tl // SUBLANES, chunk, hc_ref[...])
    hc_ref[...] = h
    nh_ref[0] = h
    o_ref[...] = (b_ref[...] * gy_ref[...]).astype(BF16)


def _rnn_prompt(xr_all, gy_all, rnn_w, batch, seq):
    tl =_pick_tile(seq, (256, 128, 64, 32, 16, 8))
    nl = seq // tl
    tok = pl.BlockSpec((tl, D_RNN), lambda b, l: (b * nl + l, 0))
    return pl.pallas_call(
        functools.partial(_rnn_prompt_kernel, tl=tl),
        grid=(batch, nl),
        in_specs=[tok, tok] + [_const_spec(w.shape) for w in rnn_w],
        out_specs=(tok, pl.BlockSpec((1, 1, D_RNN), lambda b, l: (b, 0, 0))),
        out_shape=(jax.ShapeDtypeStruct((batch * seq, D_RNN), BF16), jax.ShapeDtypeStruct((batch, 1, D_RNN), F32)),
        scratch_shapes=[
            pltpu.VMEM((tl + SUBLANES, D_RNN), F32),
            pltpu.VMEM((tl, D_RNN), F32),
            pltpu.VMEM((tl, D_RNN), F32),
            pltpu.VMEM((1, D_RNN), F32),
        ],
        compiler_params=_params(("parallel", "arbitrary")),
        name="rnn_prompt",
    )(xr_all, gy_all, *rnn_w)


def _rnn_sample_kernel(xr_ref, gy_ref, hp_ref, h0_ref, cw_ref, cb_ref, wa_ref, ba_ref, wx_ref, bx_ref, lam_ref,
                       o_ref, nh_ref, *, seqs):
    rows = seqs * SUBLANES
    x = xr_ref[...]
    hp = hp_ref[...]
    r8 = lax.broadcasted_iota(jnp.int32, (rows, D_RNN), 0) % SUBLANES
    xc = cb_ref[...] + cw_ref[CONV_W - 1:CONV_W, :] * x
    for j in range(1, CONV_W):
        shifted = jnp.where(r8 >= j, pltpu.roll(x, j, 0), pltpu.roll(hp, rows - (SUBLANES - j), 0))
        xc = xc + cw_ref[CONV_W - 1 - j:CONV_W - j, :] * shifted
    a, b = _lru_coeffs(xc, wa_ref, ba_ref, wx_ref, bx_ref, lam_ref, None)
    a, b = _scan8(a, b)
    h0 = jnp.broadcast_to(h0_ref[...][:, None, :], (seqs, SUBLANES, D_RNN)).reshape(rows, D_RNN)
    h = b + a * h0
    last = jnp.where(r8 == SUBLANES - 1, h, 0.0).reshape(seqs, SUBLANES, D_RNN)
    nh_ref[...] = jnp.sum(last, axis=1)
    o_ref[...] = (h * gy_ref[...]).astype(BF16)


def _rnn_sample(xr_all, gy_all, hist_pad, h0, rnn_w, row0, dec_batch):
    seqs = _pick_tile(dec_batch, (16, 8))
    rows = seqs * SUBLANES
    assert row0 % rows == 0
    blk0 = row0 // rows
    tok = pl.BlockSpec((rows, D_RNN), lambda i: (blk0 + i, 0))
    return pl.pallas_call(
        functools.partial(_rnn_sample_kernel, seqs=seqs),
        grid=(dec_batch // seqs,),
        in_specs=[tok, tok, pl.BlockSpec((rows, D_RNN), lambda i: (i, 0)), pl.BlockSpec((seqs, D_RNN), lambda i: (i, 0))]
        + [_const_spec(w.shape) for w in rnn_w],
        out_specs=(pl.BlockSpec((rows, D_RNN), lambda i: (i, 0)), pl.BlockSpec((seqs, D_RNN), lambda i: (i, 0))),
        out_shape=(jax.ShapeDtypeStruct((dec_batch * SUBLANES, D_RNN), BF16),
                   jax.ShapeDtypeStruct((dec_batch, D_RNN), F32)),
        compiler_params=_params(("parallel",)),
        name="rnn_sample",
    )(xr_all, gy_all, hist_pad, h0, *rnn_w)


def _layer_norm(z, g, b):
    mu = jnp.mean(z, axis=-1, keepdims=True)
    zc = z - mu
    var = jnp.mean(zc * zc, axis=-1, keepdims=True)
    return zc * lax.rsqrt(var + LN_EPS) * g + b


def _first_index_of_max(vals, iota, axis, sentinel):
    mx = jnp.max(vals, axis=axis, keepdims=True)
    return mx, jnp.min(jnp.where(vals == mx, iota, sentinel), axis=axis, keepdims=True)


def _route(scores, bias):
    t = scores.shape[1]
    grp = scores + bias
    g3 = grp.reshape(N_GROUPS, GROUP_SIZE, t)
    e_in_g = lax.broadcasted_iota(jnp.int32, g3.shape, 1)
    m1, first = _first_index_of_max(g3, e_in_g, 1, GROUP_SIZE)
    m2 = jnp.max(jnp.where(e_in_g == first, -jnp.inf, g3), axis=1, keepdims=True)
    gscore = (m1 + m2).reshape(N_GROUPS, t)
    g_iota = lax.broadcasted_iota(jnp.int32, gscore.shape, 0)
    gmask = jnp.zeros(gscore.shape, jnp.bool_)
    for _ in range(TOPK_GROUPS):
        _, gi = _first_index_of_max(gscore, g_iota, 0, N_GROUPS)
        hit = g_iota == gi
        gmask = gmask | hit
        gscore = jnp.where(hit, -jnp.inf, gscore)
    masked = jnp.where(gmask[:, None, :], g3, -jnp.inf).reshape(N_EXPERTS, t)
    e_iota = lax.broadcasted_iota(jnp.int32, masked.shape, 0)
    idx, wts, hits = [], [], []
    for _ in range(TOP_K):
        _, ei = _first_index_of_max(masked, e_iota, 0, N_EXPERTS)
        hit = e_iota == ei
        idx.append(ei)
        hits.append(hit)
        wts.append(jnp.sum(jnp.where(hit, scores, 0.0), axis=0, keepdims=True))
        masked = jnp.where(hit, -jnp.inf, masked)
    idx = jnp.concatenate(idx, axis=0)
    w = jnp.concatenate(wts, axis=0)
    w = w / jnp.sum(w, axis=0, keepdims=True) * ROUTED_SCALE
    return idx, w, hits


def _merge_kernel(xp_ref, xs_ref, aop_ref, aos_ref, rop_ref, ros_ref, sga_ref, sgr_ref, woa_ref, wor_ref, wout_ref, g1_ref, b1_ref,
                  wrt_ref, rb_ref, wsg_ref, wsu_ref, wsd_ref,
                  x1r_ref, base_ref, idx_ref, wt_ref, rank_ref, cnt_ref, carry_ref, *, n_p, tm):
    i = pl.program_id(0)

    @pl.when(i == 0)
    def _():
        carry_ref[...] = jnp.zeros(carry_ref.shape, F32)

    is_prompt = i < n_p
    x = jnp.where(is_prompt, xp_ref[...], xs_ref[...])
    pa = jnp.dot(jnp.where(is_prompt, aop_ref[...], aos_ref[...]), woa_ref[...], preferred_element_type=F32)
    pr = jnp.dot(jnp.where(is_prompt, rop_ref[...], ros_ref[...]), wor_ref[...], preferred_element_type=F32)
    merged = sga_ref[...] * pa + sgr_ref[...] * pr
    z = ALPHA * x + jnp.dot(merged.astype(BF16), wout_ref[...], preferred_element_type=F32)
    x1 = _layer_norm(z, g1_ref[...], b1_ref[...])
    _store_token_rows(x1r_ref, x1, tm)
    x1b = x1.astype(BF16)
    u = jax.nn.silu(jnp.dot(x1b, wsg_ref[...], preferred_element_type=F32)) * jnp.dot(x1b, wsu_ref[...], preferred_element_type=F32)
    shared = jnp.dot(u.astype(BF16), wsd_ref[...], preferred_element_type=F32)
    base_ref[...] = ALPHA * x1 + shared
    logits = lax.dot_general(wrt_ref[...], x1, (((1,), (1,)), ((), ())), preferred_element_type=F32)
    idx, w, hits = _route(jax.nn.sigmoid(logits), rb_ref[...])
    idx_ref[...] = idx
    wt_ref[...] = w

    chosen = functools.reduce(jnp.logical_or, hits)
    chosen_f = jnp.where(chosen, 1.0, 0.0)
    earlier = (lax.broadcasted_iota(jnp.int32, (tm, tm), 0) < lax.broadcasted_iota(jnp.int32, (tm, tm), 1))
    prefix = jnp.dot(chosen_f.astype(BF16), jnp.where(earlier, 1.0, 0.0).astype(BF16), preferred_element_type=F32)
    before = prefix + carry_ref[...]
    ranks = [jnp.sum(jnp.where(hit, before, 0.0), axis=0, keepdims=True) for hit in hits]
    rank_ref[...] = jnp.concatenate(ranks, axis=0).astype(jnp.int32)
    carry_ref[...] = carry_ref[...] + jnp.sum(chosen_f, axis=1, keepdims=True)
    cnt_ref[...] = carry_ref[...]


def _merge(x_p, x_s, attn_p, attn_s, rnn_p, rnn_s, sga, sgr, weights):
    t_p, t_s = x_p.shape[0], x_s.shape[0]
    t = t_p + t_s
    tm = _pick_tile(math.gcd(t_p, t_s), (256, 128))
    row = lambda width: pl.BlockSpec((tm, width), lambda i: (i, 0))
    col = pl.BlockSpec((TOP_K, tm), lambda i: (0, i))

    return pl.pallas_call(
        functools.partial(_merge_kernel, n_p=t_p // tm, tm=tm),
        grid=(t // tm,),
        in_specs=[*_two_source_specs(t_p, t_s, tm, D_MODEL), *_two_source_specs(t_p, t_s, tm, D_ATTN),
                  *_two_source_specs(t_p, t_s, tm, D_RNN), row(D_MODEL), row(D_MODEL)]
        + [_const_spec(w.shape) for w in weights],
        out_specs=(pl.BlockSpec((tm * ROW_TILE, LANES), lambda i: (i, 0)), row(D_MODEL), col, col, col,
                   _const_spec((N_EXPERTS, 1))),
        out_shape=(
            jax.ShapeDtypeStruct((t * ROW_TILE, LANES), F32),
            jax.ShapeDtypeStruct((t, D_MODEL), F32),
            jax.ShapeDtypeStruct((TOP_K, t), jnp.int32),
            jax.ShapeDtypeStruct((TOP_K, t), F32),
            jax.ShapeDtypeStruct((TOP_K, t), jnp.int32),
            jax.ShapeDtypeStruct((N_EXPERTS, 1), F32),
        ),
        scratch_shapes=[pltpu.VMEM((N_EXPERTS, 1), F32)],
        compiler_params=_params(("arbitrary",)),
        name="merge_ln1_route",
    )(x_p, x_s, attn_p, attn_s, rnn_p, rnn_s, sga, sgr, *weights)


def _expert_layout(counts, n_assign):
    counts = counts.reshape(N_EXPERTS).astype(jnp.int32)
    padded = (counts + ROW_PAD - 1) // ROW_PAD * ROW_PAD
    pend = jnp.cumsum(padded)
    pstart = pend - padded
    rows_alloc = (n_assign + N_EXPERTS * (ROW_PAD - 1)) // ROW_PAD * ROW_PAD + EXPERT_CHUNK
    n_chunks = jnp.maximum((counts + EXPERT_CHUNK - 1) // EXPERT_CHUNK, 1)
    pad_group = jnp.where(counts % ROW_PAD != 0, pstart + counts // ROW_PAD * ROW_PAD, -1)
    zero_meta = jnp.concatenate([pad_group, pend[-1:]]).astype(jnp.int32)
    ffn_end = jnp.max(pstart + n_chunks * EXPERT_CHUNK).reshape(1)
    return pstart, zero_meta, n_chunks, ffn_end, rows_alloc


def _dest_kernel(idx_ref, rank_ref, pstart_ref, dest_ref):
    e_iota = lax.broadcasted_iota(jnp.int32, (N_EXPERTS, idx_ref.shape[1]), 0)
    starts = [jnp.sum(jnp.where(e_iota == idx_ref[k:k + 1, :], pstart_ref[...], 0), axis=0, keepdims=True)
              for k in range(TOP_K)]
    dest_ref[...] = jnp.concatenate(starts, axis=0) + rank_ref[...]


def _dest_rows(idx_t, rank_t, pstart):
    t = idx_t.shape[1]
    tm = _pick_tile(t, (512, 256, 128))
    col = pl.BlockSpec((TOP_K, tm), lambda i: (0, i))
    return pl.pallas_call(
        _dest_kernel,
        grid=(t // tm,),
        in_specs=[col, col, _const_spec((N_EXPERTS, 1))],
        out_specs=col,
        out_shape=jax.ShapeDtypeStruct((TOP_K, t), jnp.int32),
        compiler_params=_params(("parallel",)),
        name="dest_rows",
    )(idx_t, rank_t, pstart.reshape(N_EXPERTS, 1))


def _token_rows(r, n=1):
    return pl.ds(pl.multiple_of(r * ROW_TILE, ROW_TILE), n * ROW_TILE)


def _zero_row_groups(zero_ref, dst_ref, sem, first_group, n_groups):
    def start(g, c):
        pltpu.make_async_copy(zero_ref, dst_ref.at[_token_rows(g * ROW_PAD, ROW_PAD)], sem).start()
        return c

    lax.fori_loop(first_group, n_groups, start, 0)
    return n_groups - first_group


def _wait_zero_copies(zero_ref, dst_ref, sem, n):
    def wait(_, c):
        pltpu.make_async_copy(zero_ref, dst_ref.at[_token_rows(0, ROW_PAD)], sem).wait()
        return c

    lax.fori_loop(0, n, wait, 0)


def _dispatch_kernel(zero_meta_ref, dest_ref, x_ref, xs_ref, zero_ref, sem, *, tm, rows_alloc):
    @pl.when(pl.program_id(0) == 0)
    def _():
        zero_ref[...] = jnp.zeros(zero_ref.shape, F32)

        def pad_group(e, n):
            row = zero_meta_ref[e]

            @pl.when(row >= 0)
            def _():
                pltpu.make_async_copy(zero_ref, xs_ref.at[_token_rows(row, ROW_PAD)], sem).start()

            return n + jnp.where(row >= 0, 1, 0)

        n = lax.fori_loop(0, N_EXPERTS, pad_group, 0)
        n = n + _zero_row_groups(zero_ref, xs_ref, sem, zero_meta_ref[N_EXPERTS] // ROW_PAD, rows_alloc // ROW_PAD)
        _wait_zero_copies(zero_ref, xs_ref, sem, n)

    def issue(t, c):
        for k in range(TOP_K):
            pltpu.make_async_copy(x_ref.at[_token_rows(t)], xs_ref.at[_token_rows(dest_ref[k, t])], sem).start()
        return c

    lax.fori_loop(0, tm, issue, 0)
    for _ in range(TOP_K):
        pltpu.make_async_copy(x_ref, xs_ref.at[_token_rows(0, tm)], sem).wait()


def _dispatch(x1r, dest, zero_meta, rows_alloc):
    t = dest.shape[1]
    tm = _pick_tile(t, (512, 256, 128))
    grid_spec = pltpu.PrefetchScalarGridSpec(
        num_scalar_prefetch=1,
        grid=(t // tm,),
        in_specs=[
            pl.BlockSpec((TOP_K, tm), lambda i, ps: (0, i), memory_space=pltpu.SMEM),
            pl.BlockSpec((tm * ROW_TILE, LANES), lambda i, ps: (i, 0)),
        ],
        out_specs=pl.BlockSpec(memory_space=pl.ANY),
        scratch_shapes=[pltpu.VMEM((ROW_PAD * ROW_TILE, LANES), F32), pltpu.SemaphoreType.DMA],
    )
    return pl.pallas_call(
        functools.partial(_dispatch_kernel, tm=tm, rows_alloc=rows_alloc),
        grid_spec=grid_spec,
        out_shape=jax.ShapeDtypeStruct((rows_alloc * ROW_TILE, LANES), F32),
        compiler_params=_params(("arbitrary",)),
        name="dispatch",
    )(zero_meta, dest, x1r)


def _expert_kernel(pstart_ref, nch_ref, end_ref, wg_ref, wu_ref, wd_ref, xs_ref, o_ref,
                   xbuf_ref, obuf_ref, wgb_ref, wub_ref, wdb_ref, zero_ref, done_ref, in_sem, out_sem, *, rows_alloc):
    e = pl.program_id(0)
    n_e = pl.num_programs(0)
    start = pstart_ref[e]
    nch = nch_ref[e]
    buf_rows = EXPERT_CHUNK * ROW_TILE

    def in_copy(row, slot):
        return pltpu.make_async_copy(xs_ref.at[_token_rows(row, EXPERT_CHUNK)],
                                     xbuf_ref.at[pl.ds(slot * buf_rows, buf_rows)], in_sem.at[slot])

    def out_copy(row, slot):
        return pltpu.make_async_copy(obuf_ref.at[pl.ds(slot * buf_rows, buf_rows)],
                                     o_ref.at[_token_rows(row, EXPERT_CHUNK)], out_sem)

    @pl.when(e == 0)
    def _():
        done_ref[0] = 0
        in_copy(start, 0).start()

    wgb_ref[...] = wg_ref[...].astype(BF16)
    wub_ref[...] = wu_ref[...].astype(BF16)
    wdb_ref[...] = wd_ref[...].astype(BF16)
    done = done_ref[0]

    def chunk(c, carry):
        g = done + c
        slot = g % 2
        row = start + c * EXPERT_CHUNK
        in_copy(row, slot).wait()
        last = c + 1 == nch
        next_row = jnp.where(last, pstart_ref[jnp.minimum(e + 1, n_e - 1)], row + EXPERT_CHUNK)

        @pl.when(jnp.logical_not(last & (e == n_e - 1)))
        def _():
            in_copy(next_row, 1 - slot).start()

        base = pl.multiple_of(slot * buf_rows, buf_rows)
        for h in range(EXPERT_CHUNK // EXPERT_ROWS):
            xb = jnp.concatenate(_load_token_rows(xbuf_ref, EXPERT_ROWS, row0=h * EXPERT_ROWS, base=base),
                                 axis=1).astype(BF16)
            gate = jnp.dot(xb, wgb_ref[...], preferred_element_type=F32)
            up = jnp.dot(xb, wub_ref[...], preferred_element_type=F32)
            act = (jax.nn.silu(gate) * up).astype(BF16)
            _store_token_rows(obuf_ref, jnp.dot(act, wdb_ref[...], preferred_element_type=F32),
                              EXPERT_ROWS, row0=h * EXPERT_ROWS, base=base)

        @pl.when(g > 0)
        def _():
            out_copy(0, 0).wait()

        out_copy(row, slot).start()
        return carry

    lax.fori_loop(0, nch, chunk, 0)
    done_ref[0] = done + nch

    @pl.when(e == n_e - 1)
    def _():
        out_copy(0, 0).wait()
        zero_ref[...] = jnp.zeros(zero_ref.shape, F32)
        n = _zero_row_groups(zero_ref, o_ref, out_sem, end_ref[0] // ROW_PAD, rows_alloc // ROW_PAD)
        _wait_zero_copies(zero_ref, o_ref, out_sem, n)


def _expert_ffn(xs, pstart, n_chunks, ffn_end, rows_alloc, w_e_gate, w_e_up, w_e_down):
    weight = lambda shape: pl.BlockSpec((None, *shape), lambda e, *_: (e, 0, 0))
    grid_spec = pltpu.PrefetchScalarGridSpec(
        num_scalar_prefetch=3,
        grid=(N_EXPERTS,),
        in_specs=[weight((D_MODEL, D_EXPERT)), weight((D_MODEL, D_EXPERT)), weight((D_EXPERT, D_MODEL)),
                  pl.BlockSpec(memory_space=pl.ANY)],
        out_specs=pl.BlockSpec(memory_space=pl.ANY),
        scratch_shapes=[
            pltpu.VMEM((2 * EXPERT_CHUNK * ROW_TILE, LANES), F32),
            pltpu.VMEM((2 * EXPERT_CHUNK * ROW_TILE, LANES), F32),
            pltpu.VMEM((D_MODEL, D_EXPERT), BF16), pltpu.VMEM((D_MODEL, D_EXPERT), BF16),
            pltpu.VMEM((D_EXPERT, D_MODEL), BF16),
            pltpu.VMEM((ROW_PAD * ROW_TILE, LANES), F32),
            pltpu.SMEM((1,), jnp.int32),
            pltpu.SemaphoreType.DMA((2,)), pltpu.SemaphoreType.DMA,
        ],
    )
    return pl.pallas_call(
        functools.partial(_expert_kernel, rows_alloc=rows_alloc),
        grid_spec=grid_spec,
        out_shape=jax.ShapeDtypeStruct((rows_alloc * ROW_TILE, LANES), F32),
        compiler_params=_params(("arbitrary",)),
        name="expert_ffn",
    )(pstart, n_chunks, ffn_end, w_e_gate, w_e_up, w_e_down, xs)


def _combine_kernel(dest_ref, w_ref, base_ref, g_ref, b_ref, outs_ref, yp_ref, ys_ref, buf_ref, sem, *, tm, n_p):
    i = pl.program_id(0)

    def issue(t, c):
        for k in range(TOP_K):
            pltpu.make_async_copy(outs_ref.at[_token_rows(dest_ref[k, t])], buf_ref.at[_token_rows(k * tm + t)], sem).start()
        return c

    lax.fori_loop(0, tm, issue, 0)
    for k in range(TOP_K):
        pltpu.make_async_copy(outs_ref.at[_token_rows(0, tm)], buf_ref.at[_token_rows(k * tm, tm)], sem).wait()

    w = w_ref[...]
    chunks = [None] * ROW_TILE
    for k in range(TOP_K):
        wk = w[:, k:k + 1]
        for s, rows in enumerate(_load_token_rows(buf_ref, tm, row0=k * tm)):
            chunks[s] = wk * rows if chunks[s] is None else chunks[s] + wk * rows
    y = _layer_norm(base_ref[...] + jnp.concatenate(chunks, axis=1), g_ref[...], b_ref[...])

    @pl.when(i < n_p)
    def _():
        yp_ref[...] = y

    @pl.when(i >= n_p)
    def _():
        ys_ref[...] = y


def _combine(out_sorted, dest, w_tok, base, g2, b2, t_p, t_s):
    t = t_p + t_s
    tm = _pick_tile(math.gcd(t_p, t_s), (256, 128))
    row = pl.BlockSpec((tm, D_MODEL), lambda i: (i, 0))
    out_p, out_s = _two_source_specs(t_p, t_s, tm, D_MODEL)
    return pl.pallas_call(
        functools.partial(_combine_kernel, tm=tm, n_p=t_p // tm),
        grid=(t // tm,),
        in_specs=[
            pl.BlockSpec((TOP_K, tm), lambda i: (0, i), memory_space=pltpu.SMEM),
            pl.BlockSpec((tm, TOP_K), lambda i: (i, 0)),
            row, _const_spec((1, D_MODEL)), _const_spec((1, D_MODEL)),
            pl.BlockSpec(memory_space=pl.ANY),
        ],
        out_specs=(out_p, out_s),
        out_shape=(jax.ShapeDtypeStruct((t_p, D_MODEL), F32), jax.ShapeDtypeStruct((t_s, D_MODEL), F32)),
        scratch_shapes=[pltpu.VMEM((TOP_K * tm * ROW_TILE, LANES), F32), pltpu.SemaphoreType.DMA],
        compiler_params=_params(("arbitrary",)),
        name="combine_ln2",
    )(dest, w_tok, base, g2, b2, out_sorted)


def kernel(x_prompt, x_sample, cache_k, cache_v, state_conv, state_rnn, w_in, conv_w, conv_b, w_gate_a, b_gate_a, w_gate_x, b_gate_x, lru_lambda, rel_bias, sinks, w_o_attn, w_o_rnn, w_out, ln1_g, ln1_b, w_router, router_bias, w_e_gate, w_e_up, w_e_down, w_s_gate, w_s_up, w_s_down, ln2_g, ln2_b):
    assert w_in.shape[0] == DEPTH == 1
    batch, seq, _ = x_prompt.shape
    dec_batch, s_len, _ = x_sample.shape
    w_cache = cache_k.shape[2]
    assert s_len == SUBLANES and seq % WINDOW == 0 and w_cache == WINDOW
    t_p = batch * seq
    t_s = dec_batch * s_len
    vec = lambda a: a[0].reshape(1, -1).astype(F32)

    x_p = x_prompt.reshape(t_p, D_MODEL)
    x_s = x_sample.reshape(t_s, D_MODEL)
    q, k, v, xr, gy, sga, sgr = _inproj(x_p, x_s, w_in[0].astype(BF16))

    qi = jnp.arange(WINDOW)
    dist = qi[:, None] + WINDOW - jnp.arange(2 * WINDOW)[None, :]
    bias_p = _bias_table(rel_bias, dist)
    attn_p = _attn_prompt(q, k, v, bias_p, sinks[0], batch, seq)
    attn_s, k_s, v_s = _attn_sample(
        q, k, v, cache_k[0].reshape(dec_batch, w_cache, D_KV), cache_v[0].reshape(dec_batch, w_cache, D_KV),
        rel_bias, sinks[0], t_p, dec_batch, s_len)

    rnn_w = (conv_w[0], vec(conv_b), w_gate_a[0].astype(BF16), vec(b_gate_a), w_gate_x[0].astype(BF16),
             vec(b_gate_x), vec(lru_lambda))
    rnn_p, h_p = _rnn_prompt(xr, gy, rnn_w, batch, seq)
    hist_pad = jnp.pad(state_conv[0], ((0, 0), (SUBLANES - (CONV_W - 1), 0), (0, 0))).reshape(t_s, D_RNN)
    rnn_s, h_s = _rnn_sample(xr, gy, hist_pad, state_rnn[0], rnn_w, t_p, dec_batch)

    merge_w = (w_o_attn[0].astype(BF16), w_o_rnn[0].astype(BF16), w_out[0].astype(BF16), vec(ln1_g), vec(ln1_b),
               w_router[0].T, router_bias[0].reshape(N_EXPERTS, 1), w_s_gate[0].astype(BF16),
               w_s_up[0].astype(BF16), w_s_down[0].astype(BF16))
    x1r, base, idx_t, wt_t, rank_t, counts = _merge(x_p, x_s, attn_p, attn_s, rnn_p, rnn_s, sga, sgr, merge_w)

    pstart, zero_meta, n_chunks, ffn_end, rows_alloc = _expert_layout(counts, (t_p + t_s) * TOP_K)
    dest = _dest_rows(idx_t, rank_t, pstart)
    xs = _dispatch(x1r, dest, zero_meta, rows_alloc)
    out_sorted = _expert_ffn(xs, pstart, n_chunks, ffn_end, rows_alloc, w_e_gate[0], w_e_up[0], w_e_down[0])
    y_p, y_s = _combine(out_sorted, dest, wt_t.T, base, vec(ln2_g), vec(ln2_b), t_p, t_s)
    y_p = y_p.reshape(batch, seq, D_MODEL)
    y_s = y_s.reshape(dec_batch, s_len, D_MODEL)
    kv5 = lambda a, b: a.reshape(1, b, WINDOW, N_KV_HEADS, HEAD_DIM)
    k_p = kv5(k[:t_p].reshape(batch, seq, D_KV)[:, seq - WINDOW:], batch)
    v_p = kv5(v[:t_p].reshape(batch, seq, D_KV)[:, seq - WINDOW:], batch)
    conv_p = xr[:t_p].reshape(batch, seq, D_RNN)[:, seq - (CONV_W - 1):][None]
    conv_s = xr[t_p:].reshape(dec_batch, s_len, D_RNN)[:, s_len - (CONV_W - 1):][None]
    return (y_p, y_s, k_p, v_p, conv_p, h_p.reshape(1, batch, D_RNN),
            kv5(k_s, dec_batch), kv5(v_s, dec_batch), conv_s, h_s.reshape(1, dec_batch, D_RNN))
```

```python
import functools
import math

import jax
import jax.numpy as jnp
from jax import lax
from jax.experimental import pallas as pl
from jax.experimental.pallas import tpu as pltpu

F32 = jnp.float32
BF16 = jnp.bfloat16

D_MODEL = 1024
N_HEADS = 8
N_KV_HEADS = 2
HEAD_DIM = 64
GROUP = N_HEADS // N_KV_HEADS
WINDOW = 128
D_ATTN = N_HEADS * HEAD_DIM
D_KV = N_KV_HEADS * HEAD_DIM
N_BUCKETS = 32
MAX_DISTANCE = 128
D_RNN = D_MODEL
RNN_BLOCK = 256
N_RNN_BLOCKS = D_RNN // RNN_BLOCK
CONV_W = 4
RG_C = 8.0
N_EXPERTS = 256
TOP_K = 8
N_GROUPS = 8
GROUP_SIZE = N_EXPERTS // N_GROUPS
TOPK_GROUPS = 4
D_EXPERT = D_MODEL // 4
ROUTED_SCALE = 2.5
LN_EPS = 1e-5
DEPTH = 1
ALPHA = (2 * DEPTH) ** 0.25
NEG_INF = -1e30
SM_SCALE = HEAD_DIM ** -0.5

O_Q = 0
O_K = D_ATTN
O_V = O_K + D_KV
O_XR = O_V + D_KV
O_YR = O_XR + D_RNN
O_GA = O_YR + D_RNN
O_GR = O_GA + D_MODEL
D_IN = O_GR + D_MODEL

SUBLANES = 8
VMEM_LIMIT_BYTES = 56 * 1024 * 1024
EXPERT_ROWS = 256
EXPERT_CHUNK = 256
ROW_PAD = SUBLANES


def _params(sem):
    return pltpu.CompilerParams(dimension_semantics=sem, vmem_limit_bytes=VMEM_LIMIT_BYTES)


def _pick_tile(n, candidates):
    for c in candidates:
        if n % c == 0:
            return c
    raise ValueError(f"no tile for {n}")


def _const_spec(shape):
    nd = len(shape)
    return pl.BlockSpec(shape, lambda *_: (0,) * nd)


LANES = 128
ROW_TILE = D_MODEL // LANES
assert ROW_TILE == SUBLANES


def _store_token_rows(ref, mat, n, row0=0, base=0):
    for s in range(ROW_TILE):
        ref[pl.ds(base + row0 * ROW_TILE + s, n, stride=ROW_TILE), :] = mat[:, s * LANES:(s + 1) * LANES]


def _load_token_rows(ref, n, row0=0, base=0):
    return [ref[pl.ds(base + row0 * ROW_TILE + s, n, stride=ROW_TILE), :] for s in range(ROW_TILE)]


def _two_source_specs(t_p, t_s, tm, width):
    n_p = t_p // tm
    assert t_p % tm == 0 and t_s % tm == 0
    return (pl.BlockSpec((tm, width), lambda i, *_: (jnp.minimum(i, n_p - 1), 0)),
            pl.BlockSpec((tm, width), lambda i, *_: (jnp.maximum(i - n_p, 0), 0)))


def _inproj_kernel(xp_ref, xs_ref, w_ref, q_ref, k_ref, v_ref, xr_ref, gy_ref, sga_ref, sgr_ref, *, n_p):
    x = jnp.where(pl.program_id(0) < n_p, xp_ref[...], xs_ref[...]).astype(BF16)

    def seg(lo, hi):
        return jnp.dot(x, w_ref[:, lo:hi], preferred_element_type=F32)

    q_ref[...] = seg(O_Q, O_K).astype(BF16)
    k_ref[...] = seg(O_K, O_V)
    v_ref[...] = seg(O_V, O_XR)
    xr_ref[...] = seg(O_XR, O_YR)
    gy_ref[...] = jax.nn.gelu(seg(O_YR, O_GA)).astype(BF16)
    sga_ref[...] = jax.nn.sigmoid(seg(O_GA, O_GR)).astype(BF16)
    sgr_ref[...] = jax.nn.sigmoid(seg(O_GR, D_IN)).astype(BF16)


def _inproj(x_p, x_s, w_in_bf16):
    t_p, t_s = x_p.shape[0], x_s.shape[0]
    t = t_p + t_s
    tm = _pick_tile(math.gcd(t_p, t_s), (256, 128, 64, 32, 16, 8))
    row = lambda width: pl.BlockSpec((tm, width), lambda i: (i, 0))
    out_shape = (
        jax.ShapeDtypeStruct((t, D_ATTN), BF16),
        jax.ShapeDtypeStruct((t, D_KV), F32),
        jax.ShapeDtypeStruct((t, D_KV), F32),
        jax.ShapeDtypeStruct((t, D_RNN), F32),
        jax.ShapeDtypeStruct((t, D_RNN), BF16),
        jax.ShapeDtypeStruct((t, D_MODEL), BF16),
        jax.ShapeDtypeStruct((t, D_MODEL), BF16),
    )
    return pl.pallas_call(
        functools.partial(_inproj_kernel, n_p=t_p // tm),
        grid=(t // tm,),
        in_specs=[*_two_source_specs(t_p, t_s, tm, D_MODEL), _const_spec((D_MODEL, D_IN))],
        out_specs=(row(D_ATTN), row(D_KV), row(D_KV), row(D_RNN), row(D_RNN), row(D_MODEL), row(D_MODEL)),
        out_shape=out_shape,
        compiler_params=_params(("parallel",)),
        name="inproj",
    )(x_p, x_s, w_in_bf16)


def _t5_bucket(dist):
    n = jnp.maximum(dist, 0)
    max_exact = N_BUCKETS // 2
    nf = jnp.maximum(n, 1).astype(F32)
    large = max_exact + (jnp.log(nf / max_exact) / math.log(MAX_DISTANCE / max_exact) * (N_BUCKETS - max_exact)).astype(jnp.int32)
    large = jnp.minimum(large, N_BUCKETS - 1)
    return jnp.where(n < max_exact, n, large)


def _bias_table(rel_bias, dist):
    bucket = _t5_bucket(dist)
    rb = rel_bias.astype(F32)
    out = jnp.zeros((N_HEADS, *dist.shape), F32)
    for j in range(N_BUCKETS):
        out = jnp.where(bucket[None] == j, rb[j][:, None, None], out)
    return out


def _softmax_pv(s, sink, v):
    m = jnp.maximum(jnp.max(s, axis=-1, keepdims=True), sink)
    p = jnp.exp(s - m)
    denom = jnp.sum(p, axis=-1, keepdims=True) + jnp.exp(sink - m)
    return jnp.dot(p.astype(BF16), v, preferred_element_type=F32), denom


def _attn_prompt_kernel(sink_ref, q_ref, kc_ref, kp_ref, vc_ref, vp_ref, bias_ref, o_ref):
    n = pl.program_id(1)
    kk = jnp.concatenate([kp_ref[...], kc_ref[...]], axis=0).astype(BF16)
    vv = jnp.concatenate([vp_ref[...], vc_ref[...]], axis=0).astype(BF16)
    q = q_ref[...]
    rows = lax.broadcasted_iota(jnp.int32, (WINDOW, 2 * WINDOW), 0)
    cols = lax.broadcasted_iota(jnp.int32, (WINDOW, 2 * WINDOW), 1)
    dist = rows + WINDOW - cols
    valid = (dist >= 0) & (dist <= WINDOW) & ((n > 0) | (cols >= WINDOW))
    for g in range(N_KV_HEADS):
        kg = kk[:, g * HEAD_DIM:(g + 1) * HEAD_DIM]
        vg = vv[:, g * HEAD_DIM:(g + 1) * HEAD_DIM]
        for h in range(GROUP):
            hh = g * GROUP + h
            qh = q[:, hh * HEAD_DIM:(hh + 1) * HEAD_DIM]
            s = lax.dot_general(qh, kg, (((1,), (1,)), ((), ())), preferred_element_type=F32) * SM_SCALE
            s = jnp.where(valid, s + bias_ref[hh], NEG_INF)
            o, denom = _softmax_pv(s, sink_ref[0, hh], vg)
            o_ref[:, hh * HEAD_DIM:(hh + 1) * HEAD_DIM] = (o / denom).astype(BF16)


def _attn_prompt(q_all, k_all, v_all, bias, sinks, batch, seq):
    nb = seq // WINDOW
    cur = lambda width: pl.BlockSpec((WINDOW, width), lambda b, n: (b * nb + n, 0))
    prev = lambda width: pl.BlockSpec((WINDOW, width), lambda b, n: (b * nb + jnp.maximum(n - 1, 0), 0))
    return pl.pallas_call(
        _attn_prompt_kernel,
        grid=(batch, nb),
        in_specs=[
            pl.BlockSpec(memory_space=pltpu.SMEM),
            cur(D_ATTN), cur(D_KV), prev(D_KV), cur(D_KV), prev(D_KV),
            _const_spec((N_HEADS, WINDOW, 2 * WINDOW)),
        ],
        out_specs=cur(D_ATTN),
        out_shape=jax.ShapeDtypeStruct((batch * seq, D_ATTN), BF16),
        compiler_params=_params(("parallel", "arbitrary")),
        name="attn_prompt",
    )(sinks.reshape(1, N_HEADS).astype(F32), q_all, k_all, k_all, v_all, v_all, bias)


def _attn_sample_kernel(q_ref, kn_ref, vn_ref, kc_ref, vc_ref, bc_ref, bn_ref, sink_ref,
                        o_ref, ko_ref, vo_ref, *, seqs, s_len):
    w = kc_ref.shape[1]
    rows_c = lax.broadcasted_iota(jnp.int32, (GROUP * s_len, w), 0) % s_len
    cols_c = lax.broadcasted_iota(jnp.int32, (GROUP * s_len, w), 1)
    dist_c = rows_c + w - cols_c
    valid_c = (dist_c >= 0) & (dist_c <= WINDOW)
    rows_n = lax.broadcasted_iota(jnp.int32, (GROUP * s_len, s_len), 0) % s_len
    cols_n = lax.broadcasted_iota(jnp.int32, (GROUP * s_len, s_len), 1)
    dist_n = rows_n - cols_n
    valid_n = (dist_n >= 0) & (dist_n <= WINDOW)
    for j in range(seqs):
        r0 = j * s_len
        qj = q_ref[r0:r0 + s_len, :]
        kc = kc_ref[j]
        vc = vc_ref[j]
        kn = kn_ref[r0:r0 + s_len, :]
        vn = vn_ref[r0:r0 + s_len, :]
        ko_ref[j, 0:w - s_len, :] = kc[s_len:, :]
        ko_ref[j, w - s_len:w, :] = kn
        vo_ref[j, 0:w - s_len, :] = vc[s_len:, :]
        vo_ref[j, w - s_len:w, :] = vn
        kcb, vcb, knb, vnb = kc.astype(BF16), vc.astype(BF16), kn.astype(BF16), vn.astype(BF16)
        for g in range(N_KV_HEADS):
            lo, hi = g * HEAD_DIM, (g + 1) * HEAD_DIM
            qs = jnp.concatenate(
                [qj[:, (g * GROUP + h) * HEAD_DIM:(g * GROUP + h + 1) * HEAD_DIM] for h in range(GROUP)], axis=0)
            nt = (((1,), (1,)), ((), ()))
            s_c = lax.dot_general(qs, kcb[:, lo:hi], nt, preferred_element_type=F32) * SM_SCALE
            s_n = lax.dot_general(qs, knb[:, lo:hi], nt, preferred_element_type=F32) * SM_SCALE
            s_c = jnp.where(valid_c, s_c + bc_ref[g], NEG_INF)
            s_n = jnp.where(valid_n, s_n + bn_ref[g], NEG_INF)
            sink = sink_ref[g]
            m = jnp.maximum(jnp.maximum(jnp.max(s_c, axis=-1, keepdims=True), jnp.max(s_n, axis=-1, keepdims=True)), sink)
            p_c = jnp.exp(s_c - m)
            p_n = jnp.exp(s_n - m)
            denom = jnp.sum(p_c, axis=-1, keepdims=True) + jnp.sum(p_n, axis=-1, keepdims=True) + jnp.exp(sink - m)
            o = jnp.dot(p_c.astype(BF16), vcb[:, lo:hi], preferred_element_type=F32)
            o = o + jnp.dot(p_n.astype(BF16), vnb[:, lo:hi], preferred_element_type=F32)
            o = (o / denom).astype(BF16)
            for h in range(GROUP):
                hh = g * GROUP + h
                o_ref[r0:r0 + s_len, hh * HEAD_DIM:(hh + 1) * HEAD_DIM] = o[h * s_len:(h + 1) * s_len, :]


def _attn_sample(q_all, k_all, v_all, cache_k, cache_v, rel_bias, sinks, row0, dec_batch, s_len):
    w = cache_k.shape[1]
    seqs = _pick_tile(dec_batch, (16, 8, 4, 2, 1))
    rows = seqs * s_len
    blk0 = row0 // rows
    assert row0 % rows == 0
    qi = jnp.arange(s_len)
    dist_c = qi[:, None] + w - jnp.arange(w)[None, :]
    dist_n = qi[:, None] - jnp.arange(s_len)[None, :]
    b_c = _bias_table(rel_bias, dist_c).reshape(N_KV_HEADS, GROUP * s_len, w)
    b_n = _bias_table(rel_bias, dist_n).reshape(N_KV_HEADS, GROUP * s_len, s_len)
    sink = jnp.broadcast_to(sinks.astype(F32).reshape(N_KV_HEADS, GROUP, 1, 1), (N_KV_HEADS, GROUP, s_len, 1))
    sink = sink.reshape(N_KV_HEADS, GROUP * s_len, 1)
    tok = lambda width: pl.BlockSpec((rows, width), lambda i: (blk0 + i, 0))
    cache = pl.BlockSpec((seqs, w, D_KV), lambda i: (i, 0, 0))
    return pl.pallas_call(
        functools.partial(_attn_sample_kernel, seqs=seqs, s_len=s_len),
        grid=(dec_batch // seqs,),
        in_specs=[
            tok(D_ATTN), tok(D_KV), tok(D_KV), cache, cache,
            _const_spec(b_c.shape), _const_spec(b_n.shape), _const_spec(sink.shape),
        ],
        out_specs=(pl.BlockSpec((rows, D_ATTN), lambda i: (i, 0)), cache, cache),
        out_shape=(
            jax.ShapeDtypeStruct((dec_batch * s_len, D_ATTN), BF16),
            jax.ShapeDtypeStruct((dec_batch, w, D_KV), F32),
            jax.ShapeDtypeStruct((dec_batch, w, D_KV), F32),
        ),
        compiler_params=_params(("parallel",)),
        name="attn_sample",
    )(q_all, k_all, v_all, cache_k, cache_v, b_c, b_n, sink)


def _softplus(z):
    return jnp.maximum(z, 0.0) + jnp.log1p(jnp.exp(-jnp.abs(z)))


def _block_gate(xcb, w_ref, b_ref):
    parts = [jnp.dot(xcb[:, n * RNN_BLOCK:(n + 1) * RNN_BLOCK], w_ref[n], preferred_element_type=F32)
             for n in range(N_RNN_BLOCKS)]
    return jax.nn.sigmoid(jnp.concatenate(parts, axis=-1) + b_ref[...])


def _lru_coeffs(xc, wa_ref, ba_ref, wx_ref, bx_ref, lam_ref, first_row_unnormalised):
    xcb = xc.astype(BF16)
    r = _block_gate(xcb, wa_ref, ba_ref)
    i = _block_gate(xcb, wx_ref, bx_ref)
    log_a = -RG_C * r * _softplus(-lam_ref[...])
    a = jnp.exp(log_a)
    mult = jnp.sqrt(-jnp.tanh(log_a) * (a * a + 1.0))
    if first_row_unnormalised is not None:
        mult = jnp.where(first_row_unnormalised, 1.0, mult)
    return a, mult * i * xc


def _scan8(a, b):
    r8 = lax.broadcasted_iota(jnp.int32, a.shape, 0) % SUBLANES
    d = 1
    while d < SUBLANES:
        keep = r8 >= d
        a_sh = jnp.where(keep, pltpu.roll(a, d, 0), 1.0)
        b_sh = jnp.where(keep, pltpu.roll(b, d, 0), 0.0)
        b = a * b_sh + b
        a = a * a_sh
        d *= 2
    return a, b


def _rnn_prompt_kernel(xr_ref, gy_ref, cw_ref, cb_ref, wa_ref, ba_ref, wx_ref, bx_ref, lam_ref,
                       o_ref, nh_ref, ext_ref, a_ref, b_ref, hc_ref, *, tl):
    l = pl.program_id(1)

    @pl.when(l == 0)
    def _():
        ext_ref[0:SUBLANES, :] = jnp.zeros((SUBLANES, D_RNN), F32)
        hc_ref[...] = jnp.zeros((1, D_RNN), F32)

    x = xr_ref[...]
    ext_ref[SUBLANES:, :] = x
    xc = cb_ref[...] + cw_ref[CONV_W - 1:CONV_W, :] * x
    for j in range(1, CONV_W):
        xc = xc + cw_ref[CONV_W - 1 - j:CONV_W - j, :] * ext_ref[SUBLANES - j:SUBLANES - j + tl, :]
    ext_ref[0:SUBLANES, :] = x[tl - SUBLANES:, :]

    row = lax.broadcasted_iota(jnp.int32, (tl, D_RNN), 0)
    a, b = _lru_coeffs(xc, wa_ref, ba_ref, wx_ref, bx_ref, lam_ref, (row == 0) & (l == 0))
    a, b = _scan8(a, b)
    a_ref[...] = a
    b_ref[...] = b

    def chunk(c, h):
        sl = pl.ds(pl.multiple_of(c * SUBLANES, SUBLANES), SUBLANES)
        hc = b_ref[sl, :] + a_ref[sl, :] * h
        b_ref[sl, :] = hc
        return hc[SUBLANES - 1:SUBLANES, :]

    h = lax.fori_loop(0, tl // SUBLANES, chunk, hc_ref[...])
    hc_ref[...] = h
    nh_ref[0] = h
    o_ref[...] = (b_ref[...] * gy_ref[...]).astype(BF16)


def _rnn_prompt(xr_all, gy_all, rnn_w, batch, seq):
    tl =_pick_tile(seq, (256, 128, 64, 32, 16, 8))
    nl = seq // tl
    tok = pl.BlockSpec((tl, D_RNN), lambda b, l: (b * nl + l, 0))
    return pl.pallas_call(
        functools.partial(_rnn_prompt_kernel, tl=tl),
        grid=(batch, nl),
        in_specs=[tok, tok] + [_const_spec(w.shape) for w in rnn_w],
        out_specs=(tok, pl.BlockSpec((1, 1, D_RNN), lambda b, l: (b, 0, 0))),
        out_shape=(jax.ShapeDtypeStruct((batch * seq, D_RNN), BF16), jax.ShapeDtypeStruct((batch, 1, D_RNN), F32)),
        scratch_shapes=[
            pltpu.VMEM((tl + SUBLANES, D_RNN), F32),
            pltpu.VMEM((tl, D_RNN), F32),
            pltpu.VMEM((tl, D_RNN), F32),
            pltpu.VMEM((1, D_RNN), F32),
        ],
        compiler_params=_params(("parallel", "arbitrary")),
        name="rnn_prompt",
    )(xr_all, gy_all, *rnn_w)


def _rnn_sample_kernel(xr_ref, gy_ref, hp_ref, h0_ref, cw_ref, cb_ref, wa_ref, ba_ref, wx_ref, bx_ref, lam_ref,
                       o_ref, nh_ref, *, seqs):
    rows = seqs * SUBLANES
    x = xr_ref[...]
    hp = hp_ref[...]
    r8 = lax.broadcasted_iota(jnp.int32, (rows, D_RNN), 0) % SUBLANES
    xc = cb_ref[...] + cw_ref[CONV_W - 1:CONV_W, :] * x
    for j in range(1, CONV_W):
        shifted = jnp.where(r8 >= j, pltpu.roll(x, j, 0), pltpu.roll(hp, rows - (SUBLANES - j), 0))
        xc = xc + cw_ref[CONV_W - 1 - j:CONV_W - j, :] * shifted
    a, b = _lru_coeffs(xc, wa_ref, ba_ref, wx_ref, bx_ref, lam_ref, None)
    a, b = _scan8(a, b)
    h0 = jnp.broadcast_to(h0_ref[...][:, None, :], (seqs, SUBLANES, D_RNN)).reshape(rows, D_RNN)
    h = b + a * h0
    last = jnp.where(r8 == SUBLANES - 1, h, 0.0).reshape(seqs, SUBLANES, D_RNN)
    nh_ref[...] = jnp.sum(last, axis=1)
    o_ref[...] = (h * gy_ref[...]).astype(BF16)


def _rnn_sample(xr_all, gy_all, hist_pad, h0, rnn_w, row0, dec_batch):
    seqs = _pick_tile(dec_batch, (16, 8))
    rows = seqs * SUBLANES
    assert row0 % rows == 0
    blk0 = row0 // rows
    tok = pl.BlockSpec((rows, D_RNN), lambda i: (blk0 + i, 0))
    return pl.pallas_call(
        functools.partial(_rnn_sample_kernel, seqs=seqs),
        grid=(dec_batch // seqs,),
        in_specs=[tok, tok, pl.BlockSpec((rows, D_RNN), lambda i: (i, 0)), pl.BlockSpec((seqs, D_RNN), lambda i: (i, 0))]
        + [_const_spec(w.shape) for w in rnn_w],
        out_specs=(pl.BlockSpec((rows, D_RNN), lambda i: (i, 0)), pl.BlockSpec((seqs, D_RNN), lambda i: (i, 0))),
        out_shape=(jax.ShapeDtypeStruct((dec_batch * SUBLANES, D_RNN), BF16),
                   jax.ShapeDtypeStruct((dec_batch, D_RNN), F32)),
        compiler_params=_params(("parallel",)),
        name="rnn_sample",
    )(xr_all, gy_all, hist_pad, h0, *rnn_w)


def _layer_norm(z, g, b):
    mu = jnp.mean(z, axis=-1, keepdims=True)
    zc = z - mu
    var = jnp.mean(zc * zc, axis=-1, keepdims=True)
    return zc * lax.rsqrt(var + LN_EPS) * g + b


def _first_index_of_max(vals, iota, axis, sentinel):
    mx = jnp.max(vals, axis=axis, keepdims=True)
    return mx, jnp.min(jnp.where(vals == mx, iota, sentinel), axis=axis, keepdims=True)


def _route(scores, bias):
    t = scores.shape[1]
    grp = scores + bias
    g3 = grp.reshape(N_GROUPS, GROUP_SIZE, t)
    e_in_g = lax.broadcasted_iota(jnp.int32, g3.shape, 1)
    m1, first = _first_index_of_max(g3, e_in_g, 1, GROUP_SIZE)
    m2 = jnp.max(jnp.where(e_in_g == first, -jnp.inf, g3), axis=1, keepdims=True)
    gscore = (m1 + m2).reshape(N_GROUPS, t)
    g_iota = lax.broadcasted_iota(jnp.int32, gscore.shape, 0)
    gmask = jnp.zeros(gscore.shape, jnp.bool_)
    for _ in range(TOPK_GROUPS):
        _, gi = _first_index_of_max(gscore, g_iota, 0, N_GROUPS)
        hit = g_iota == gi
        gmask = gmask | hit
        gscore = jnp.where(hit, -jnp.inf, gscore)
    masked = jnp.where(gmask[:, None, :], g3, -jnp.inf).reshape(N_EXPERTS, t)
    e_iota = lax.broadcasted_iota(jnp.int32, masked.shape, 0)
    idx, wts, hits = [], [], []
    for _ in range(TOP_K):
        _, ei = _first_index_of_max(masked, e_iota, 0, N_EXPERTS)
        hit = e_iota == ei
        idx.append(ei)
        hits.append(hit)
        wts.append(jnp.sum(jnp.where(hit, scores, 0.0), axis=0, keepdims=True))
        masked = jnp.where(hit, -jnp.inf, masked)
    idx = jnp.concatenate(idx, axis=0)
    w = jnp.concatenate(wts, axis=0)
    w = w / jnp.sum(w, axis=0, keepdims=True) * ROUTED_SCALE
    return idx, w, hits


def _merge_kernel(xp_ref, xs_ref, aop_ref, aos_ref, rop_ref, ros_ref, sga_ref, sgr_ref, woa_ref, wor_ref, wout_ref, g1_ref, b1_ref,
                  wrt_ref, rb_ref, wsg_ref, wsu_ref, wsd_ref,
                  x1r_ref, base_ref, idx_ref, wt_ref, rank_ref, cnt_ref, carry_ref, *, n_p, tm):
    i = pl.program_id(0)

    @pl.when(i == 0)
    def _():
        carry_ref[...] = jnp.zeros(carry_ref.shape, F32)

    is_prompt = i < n_p
    x = jnp.where(is_prompt, xp_ref[...], xs_ref[...])
    pa = jnp.dot(jnp.where(is_prompt, aop_ref[...], aos_ref[...]), woa_ref[...], preferred_element_type=F32)
    pr = jnp.dot(jnp.where(is_prompt, rop_ref[...], ros_ref[...]), wor_ref[...], preferred_element_type=F32)
    merged = sga_ref[...] * pa + sgr_ref[...] * pr
    z = ALPHA * x + jnp.dot(merged.astype(BF16), wout_ref[...], preferred_element_type=F32)
    x1 = _layer_norm(z, g1_ref[...], b1_ref[...])
    _store_token_rows(x1r_ref, x1, tm)
    x1b = x1.astype(BF16)
    u = jax.nn.silu(jnp.dot(x1b, wsg_ref[...], preferred_element_type=F32)) * jnp.dot(x1b, wsu_ref[...], preferred_element_type=F32)
    shared = jnp.dot(u.astype(BF16), wsd_ref[...], preferred_element_type=F32)
    base_ref[...] = ALPHA * x1 + shared
    logits = lax.dot_general(wrt_ref[...], x1, (((1,), (1,)), ((), ())), preferred_element_type=F32)
    idx, w, hits = _route(jax.nn.sigmoid(logits), rb_ref[...])
    idx_ref[...] = idx
    wt_ref[...] = w

    chosen = functools.reduce(jnp.logical_or, hits)
    chosen_f = jnp.where(chosen, 1.0, 0.0)
    earlier = (lax.broadcasted_iota(jnp.int32, (tm, tm), 0) < lax.broadcasted_iota(jnp.int32, (tm, tm), 1))
    prefix = jnp.dot(chosen_f.astype(BF16), jnp.where(earlier, 1.0, 0.0).astype(BF16), preferred_element_type=F32)
    before = prefix + carry_ref[...]
    ranks = [jnp.sum(jnp.where(hit, before, 0.0), axis=0, keepdims=True) for hit in hits]
    rank_ref[...] = jnp.concatenate(ranks, axis=0).astype(jnp.int32)
    carry_ref[...] = carry_ref[...] + jnp.sum(chosen_f, axis=1, keepdims=True)
    cnt_ref[...] = carry_ref[...]


def _merge(x_p, x_s, attn_p, attn_s, rnn_p, rnn_s, sga, sgr, weights):
    t_p, t_s = x_p.shape[0], x_s.shape[0]
    t = t_p + t_s
    tm = _pick_tile(math.gcd(t_p, t_s), (256, 128))
    row = lambda width: pl.BlockSpec((tm, width), lambda i: (i, 0))
    col = pl.BlockSpec((TOP_K, tm), lambda i: (0, i))

    return pl.pallas_call(
        functools.partial(_merge_kernel, n_p=t_p // tm, tm=tm),
        grid=(t // tm,),
        in_specs=[*_two_source_specs(t_p, t_s, tm, D_MODEL), *_two_source_specs(t_p, t_s, tm, D_ATTN),
                  *_two_source_specs(t_p, t_s, tm, D_RNN), row(D_MODEL), row(D_MODEL)]
        + [_const_spec(w.shape) for w in weights],
        out_specs=(pl.BlockSpec((tm * ROW_TILE, LANES), lambda i: (i, 0)), row(D_MODEL), col, col, col,
                   _const_spec((N_EXPERTS, 1))),
        out_shape=(
            jax.ShapeDtypeStruct((t * ROW_TILE, LANES), F32),
            jax.ShapeDtypeStruct((t, D_MODEL), F32),
            jax.ShapeDtypeStruct((TOP_K, t), jnp.int32),
            jax.ShapeDtypeStruct((TOP_K, t), F32),
            jax.ShapeDtypeStruct((TOP_K, t), jnp.int32),
            jax.ShapeDtypeStruct((N_EXPERTS, 1), F32),
        ),
        scratch_shapes=[pltpu.VMEM((N_EXPERTS, 1), F32)],
        compiler_params=_params(("arbitrary",)),
        name="merge_ln1_route",
    )(x_p, x_s, attn_p, attn_s, rnn_p, rnn_s, sga, sgr, *weights)


def _expert_layout(counts, n_assign):
    counts = counts.reshape(N_EXPERTS).astype(jnp.int32)
    padded = (counts + ROW_PAD - 1) // ROW_PAD * ROW_PAD
    pend = jnp.cumsum(padded)
    pstart = pend - padded
    rows_alloc = (n_assign + N_EXPERTS * (ROW_PAD - 1)) // ROW_PAD * ROW_PAD + EXPERT_CHUNK
    n_chunks = jnp.maximum((counts + EXPERT_CHUNK - 1) // EXPERT_CHUNK, 1)
    pad_group = jnp.where(counts % ROW_PAD != 0, pstart + counts // ROW_PAD * ROW_PAD, -1)
    zero_meta = jnp.concatenate([pad_group, pend[-1:]]).astype(jnp.int32)
    ffn_end = jnp.max(pstart + n_chunks * EXPERT_CHUNK).reshape(1)
    return pstart, zero_meta, n_chunks, ffn_end, rows_alloc


def _dest_kernel(idx_ref, rank_ref, pstart_ref, dest_ref):
    e_iota = lax.broadcasted_iota(jnp.int32, (N_EXPERTS, idx_ref.shape[1]), 0)
    starts = [jnp.sum(jnp.where(e_iota == idx_ref[k:k + 1, :], pstart_ref[...], 0), axis=0, keepdims=True)
              for k in range(TOP_K)]
    dest_ref[...] = jnp.concatenate(starts, axis=0) + rank_ref[...]


def _dest_rows(idx_t, rank_t, pstart):
    t = idx_t.shape[1]
    tm = _pick_tile(t, (512, 256, 128))
    col = pl.BlockSpec((TOP_K, tm), lambda i: (0, i))
    return pl.pallas_call(
        _dest_kernel,
        grid=(t // tm,),
        in_specs=[col, col, _const_spec((N_EXPERTS, 1))],
        out_specs=col,
        out_shape=jax.ShapeDtypeStruct((TOP_K, t), jnp.int32),
        compiler_params=_params(("parallel",)),
        name="dest_rows",
    )(idx_t, rank_t, pstart.reshape(N_EXPERTS, 1))


def _token_rows(r, n=1):
    return pl.ds(pl.multiple_of(r * ROW_TILE, ROW_TILE), n * ROW_TILE)


def _zero_row_groups(zero_ref, dst_ref, sem, first_group, n_groups):
    def start(g, c):
        pltpu.make_async_copy(zero_ref, dst_ref.at[_token_rows(g * ROW_PAD, ROW_PAD)], sem).start()
        return c

    lax.fori_loop(first_group, n_groups, start, 0)
    return n_groups - first_group


def _wait_zero_copies(zero_ref, dst_ref, sem, n):
    def wait(_, c):
        pltpu.make_async_copy(zero_ref, dst_ref.at[_token_rows(0, ROW_PAD)], sem).wait()
        return c

    lax.fori_loop(0, n, wait, 0)


def _dispatch_kernel(zero_meta_ref, dest_ref, x_ref, xs_ref, zero_ref, sem, *, tm, rows_alloc):
    @pl.when(pl.program_id(0) == 0)
    def _():
        zero_ref[...] = jnp.zeros(zero_ref.shape, F32)

        def pad_group(e, n):
            row = zero_meta_ref[e]

            @pl.when(row >= 0)
            def _():
                pltpu.make_async_copy(zero_ref, xs_ref.at[_token_rows(row, ROW_PAD)], sem).start()

            return n + jnp.where(row >= 0, 1, 0)

        n = lax.fori_loop(0, N_EXPERTS, pad_group, 0)
        n = n + _zero_row_groups(zero_ref, xs_ref, sem, zero_meta_ref[N_EXPERTS] // ROW_PAD, rows_alloc // ROW_PAD)
        _wait_zero_copies(zero_ref, xs_ref, sem, n)

    def issue(t, c):
        for k in range(TOP_K):
            pltpu.make_async_copy(x_ref.at[_token_rows(t)], xs_ref.at[_token_rows(dest_ref[k, t])], sem).start()
        return c

    lax.fori_loop(0, tm, issue, 0)
    for _ in range(TOP_K):
        pltpu.make_async_copy(x_ref, xs_ref.at[_token_rows(0, tm)], sem).wait()


def _dispatch(x1r, dest, zero_meta, rows_alloc):
    t = dest.shape[1]
    tm = _pick_tile(t, (1024, 512, 256, 128))
    grid_spec = pltpu.PrefetchScalarGridSpec(
        num_scalar_prefetch=1,
        grid=(t // tm,),
        in_specs=[
            pl.BlockSpec((TOP_K, tm), lambda i, ps: (0, i), memory_space=pltpu.SMEM),
            pl.BlockSpec((tm * ROW_TILE, LANES), lambda i, ps: (i, 0)),
        ],
        out_specs=pl.BlockSpec(memory_space=pl.ANY),
        scratch_shapes=[pltpu.VMEM((ROW_PAD * ROW_TILE, LANES), F32), pltpu.SemaphoreType.DMA],
    )
    return pl.pallas_call(
        functools.partial(_dispatch_kernel, tm=tm, rows_alloc=rows_alloc),
        grid_spec=grid_spec,
        out_shape=jax.ShapeDtypeStruct((rows_alloc * ROW_TILE, LANES), F32),
        compiler_params=_params(("arbitrary",)),
        name="dispatch",
    )(zero_meta, dest, x1r)


def _expert_kernel(pstart_ref, nch_ref, end_ref, wg_ref, wu_ref, wd_ref, xs_ref, o_ref,
                   xbuf_ref, obuf_ref, wgb_ref, wub_ref, wdb_ref, zero_ref, done_ref, in_sem, out_sem, *, rows_alloc):
    e = pl.program_id(0)
    n_e = pl.num_programs(0)
    start = pstart_ref[e]
    nch = nch_ref[e]
    buf_rows = EXPERT_CHUNK * ROW_TILE

    def in_copy(row, slot):
        return pltpu.make_async_copy(xs_ref.at[_token_rows(row, EXPERT_CHUNK)],
                                     xbuf_ref.at[pl.ds(slot * buf_rows, buf_rows)], in_sem.at[slot])

    def out_copy(row, slot):
        return pltpu.make_async_copy(obuf_ref.at[pl.ds(slot * buf_rows, buf_rows)],
                                     o_ref.at[_token_rows(row, EXPERT_CHUNK)], out_sem)

    @pl.when(e == 0)
    def _():
        done_ref[0] = 0
        in_copy(start, 0).start()

    wgb_ref[...] = wg_ref[...].astype(BF16)
    wub_ref[...] = wu_ref[...].astype(BF16)
    wdb_ref[...] = wd_ref[...].astype(BF16)
    done = done_ref[0]

    def chunk(c, carry):
        g = done + c
        slot = g % 2
        row = start + c * EXPERT_CHUNK
        in_copy(row, slot).wait()
        last = c + 1 == nch
        next_row = jnp.where(last, pstart_ref[jnp.minimum(e + 1, n_e - 1)], row + EXPERT_CHUNK)

        @pl.when(jnp.logical_not(last & (e == n_e - 1)))
        def _():
            in_copy(next_row, 1 - slot).start()

        base = pl.multiple_of(slot * buf_rows, buf_rows)
        for h in range(EXPERT_CHUNK // EXPERT_ROWS):
            xb = jnp.concatenate(_load_token_rows(xbuf_ref, EXPERT_ROWS, row0=h * EXPERT_ROWS, base=base),
                                 axis=1).astype(BF16)
            gate = jnp.dot(xb, wgb_ref[...], preferred_element_type=F32)
            up = jnp.dot(xb, wub_ref[...], preferred_element_type=F32)
            act = (jax.nn.silu(gate) * up).astype(BF16)
            _store_token_rows(obuf_ref, jnp.dot(act, wdb_ref[...], preferred_element_type=F32),
                              EXPERT_ROWS, row0=h * EXPERT_ROWS, base=base)

        @pl.when(g > 0)
        def _():
            out_copy(0, 0).wait()

        out_copy(row, slot).start()
        return carry

    lax.fori_loop(0, nch, chunk, 0)
    done_ref[0] = done + nch

    @pl.when(e == n_e - 1)
    def _():
        out_copy(0, 0).wait()
        zero_ref[...] = jnp.zeros(zero_ref.shape, F32)
        n = _zero_row_groups(zero_ref, o_ref, out_sem, end_ref[0] // ROW_PAD, rows_alloc // ROW_PAD)
        _wait_zero_copies(zero_ref, o_ref, out_sem, n)


def _expert_ffn(xs, pstart, n_chunks, ffn_end, rows_alloc, w_e_gate, w_e_up, w_e_down):
    weight = lambda shape: pl.BlockSpec((None, *shape), lambda e, *_: (e, 0, 0))
    grid_spec = pltpu.PrefetchScalarGridSpec(
        num_scalar_prefetch=3,
        grid=(N_EXPERTS,),
        in_specs=[weight((D_MODEL, D_EXPERT)), weight((D_MODEL, D_EXPERT)), weight((D_EXPERT, D_MODEL)),
                  pl.BlockSpec(memory_space=pl.ANY)],
        out_specs=pl.BlockSpec(memory_space=pl.ANY),
        scratch_shapes=[
            pltpu.VMEM((2 * EXPERT_CHUNK * ROW_TILE, LANES), F32),
            pltpu.VMEM((2 * EXPERT_CHUNK * ROW_TILE, LANES), F32),
            pltpu.VMEM((D_MODEL, D_EXPERT), BF16), pltpu.VMEM((D_MODEL, D_EXPERT), BF16),
            pltpu.VMEM((D_EXPERT, D_MODEL), BF16),
            pltpu.VMEM((ROW_PAD * ROW_TILE, LANES), F32),
            pltpu.SMEM((1,), jnp.int32),
            pltpu.SemaphoreType.DMA((2,)), pltpu.SemaphoreType.DMA,
        ],
    )
    return pl.pallas_call(
        functools.partial(_expert_kernel, rows_alloc=rows_alloc),
        grid_spec=grid_spec,
        out_shape=jax.ShapeDtypeStruct((rows_alloc * ROW_TILE, LANES), F32),
        compiler_params=_params(("arbitrary",)),
        name="expert_ffn",
    )(pstart, n_chunks, ffn_end, w_e_gate, w_e_up, w_e_down, xs)


def _combine_kernel(dest_ref, dest_next_ref, w_ref, base_ref, g_ref, b_ref, outs_ref, yp_ref, ys_ref, buf_ref, sem,
                    *, tm, n_p):
    i = pl.program_id(0)
    slot_rows = TOP_K * tm

    def gather(d_ref, slot):
        def issue(t, c):
            for k in range(TOP_K):
                pltpu.make_async_copy(outs_ref.at[_token_rows(d_ref[k, t])],
                                      buf_ref.at[_token_rows(slot * slot_rows + k * tm + t)], sem.at[slot]).start()
            return c

        lax.fori_loop(0, tm, issue, 0)

    @pl.when(i == 0)
    def _():
        gather(dest_ref, 0)

    @pl.when(i + 1 < pl.num_programs(0))
    def _():
        gather(dest_next_ref, (i + 1) % 2)

    slot = i % 2
    for k in range(TOP_K):
        pltpu.make_async_copy(outs_ref.at[_token_rows(0, tm)], buf_ref.at[_token_rows(slot * slot_rows + k * tm, tm)],
                              sem.at[slot]).wait()

    w = w_ref[...]
    base = pl.multiple_of(slot * slot_rows * ROW_TILE, ROW_TILE)
    chunks = [None] * ROW_TILE
    for k in range(TOP_K):
        wk = w[:, k:k + 1]
        for s, rows in enumerate(_load_token_rows(buf_ref, tm, row0=k * tm, base=base)):
            chunks[s] = wk * rows if chunks[s] is None else chunks[s] + wk * rows
    y = _layer_norm(base_ref[...] + jnp.concatenate(chunks, axis=1), g_ref[...], b_ref[...])

    @pl.when(i < n_p)
    def _():
        yp_ref[...] = y

    @pl.when(i >= n_p)
    def _():
        ys_ref[...] = y


def _combine(out_sorted, dest, w_tok, base, g2, b2, t_p, t_s):
    t = t_p + t_s
    tm = _pick_tile(math.gcd(t_p, t_s), (256, 128))
    row = pl.BlockSpec((tm, D_MODEL), lambda i: (i, 0))
    out_p, out_s = _two_source_specs(t_p, t_s, tm, D_MODEL)
    n_tiles = t // tm
    return pl.pallas_call(
        functools.partial(_combine_kernel, tm=tm, n_p=t_p // tm),
        grid=(n_tiles,),
        in_specs=[
            pl.BlockSpec((TOP_K, tm), lambda i: (0, i), memory_space=pltpu.SMEM),
            pl.BlockSpec((TOP_K, tm), lambda i: (0, jnp.minimum(i + 1, n_tiles - 1)), memory_space=pltpu.SMEM),
            pl.BlockSpec((tm, TOP_K), lambda i: (i, 0)),
            row, _const_spec((1, D_MODEL)), _const_spec((1, D_MODEL)),
            pl.BlockSpec(memory_space=pl.ANY),
        ],
        out_specs=(out_p, out_s),
        out_shape=(jax.ShapeDtypeStruct((t_p, D_MODEL), F32), jax.ShapeDtypeStruct((t_s, D_MODEL), F32)),
        scratch_shapes=[pltpu.VMEM((2 * TOP_K * tm * ROW_TILE, LANES), F32), pltpu.SemaphoreType.DMA((2,))],
        compiler_params=_params(("arbitrary",)),
        name="combine_ln2",
    )(dest, dest, w_tok, base, g2, b2, out_sorted)


def kernel(x_prompt, x_sample, cache_k, cache_v, state_conv, state_rnn, w_in, conv_w, conv_b, w_gate_a, b_gate_a, w_gate_x, b_gate_x, lru_lambda, rel_bias, sinks, w_o_attn, w_o_rnn, w_out, ln1_g, ln1_b, w_router, router_bias, w_e_gate, w_e_up, w_e_down, w_s_gate, w_s_up, w_s_down, ln2_g, ln2_b):
    assert w_in.shape[0] == DEPTH == 1
    batch, seq, _ = x_prompt.shape
    dec_batch, s_len, _ = x_sample.shape
    w_cache = cache_k.shape[2]
    assert s_len == SUBLANES and seq % WINDOW == 0 and w_cache == WINDOW
    t_p = batch * seq
    t_s = dec_batch * s_len
    vec = lambda a: a[0].reshape(1, -1).astype(F32)

    x_p = x_prompt.reshape(t_p, D_MODEL)
    x_s = x_sample.reshape(t_s, D_MODEL)
    q, k, v, xr, gy, sga, sgr = _inproj(x_p, x_s, w_in[0].astype(BF16))

    qi = jnp.arange(WINDOW)
    dist = qi[:, None] + WINDOW - jnp.arange(2 * WINDOW)[None, :]
    bias_p = _bias_table(rel_bias, dist)
    attn_p = _attn_prompt(q, k, v, bias_p, sinks[0], batch, seq)
    attn_s, k_s, v_s = _attn_sample(
        q, k, v, cache_k[0].reshape(dec_batch, w_cache, D_KV), cache_v[0].reshape(dec_batch, w_cache, D_KV),
        rel_bias, sinks[0], t_p, dec_batch, s_len)

    rnn_w = (conv_w[0], vec(conv_b), w_gate_a[0].astype(BF16), vec(b_gate_a), w_gate_x[0].astype(BF16),
             vec(b_gate_x), vec(lru_lambda))
    rnn_p, h_p = _rnn_prompt(xr, gy, rnn_w, batch, seq)
    hist_pad = jnp.pad(state_conv[0], ((0, 0), (SUBLANES - (CONV_W - 1), 0), (0, 0))).reshape(t_s, D_RNN)
    rnn_s, h_s = _rnn_sample(xr, gy, hist_pad, state_rnn[0], rnn_w, t_p, dec_batch)

    merge_w = (w_o_attn[0].astype(BF16), w_o_rnn[0].astype(BF16), w_out[0].astype(BF16), vec(ln1_g), vec(ln1_b),
               w_router[0].T, router_bias[0].reshape(N_EXPERTS, 1), w_s_gate[0].astype(BF16),
               w_s_up[0].astype(BF16), w_s_down[0].astype(BF16))
    x1r, base, idx_t, wt_t, rank_t, counts = _merge(x_p, x_s, attn_p, attn_s, rnn_p, rnn_s, sga, sgr, merge_w)

    pstart, zero_meta, n_chunks, ffn_end, rows_alloc = _expert_layout(counts, (t_p + t_s) * TOP_K)
    dest = _dest_rows(idx_t, rank_t, pstart)
    xs = _dispatch(x1r, dest, zero_meta, rows_alloc)
    out_sorted = _expert_ffn(xs, pstart, n_chunks, ffn_end, rows_alloc, w_e_gate[0], w_e_up[0], w_e_down[0])
    y_p, y_s = _combine(out_sorted, dest, wt_t.T, base, vec(ln2_g), vec(ln2_b), t_p, t_s)
    y_p = y_p.reshape(batch, seq, D_MODEL)
    y_s = y_s.reshape(dec_batch, s_len, D_MODEL)
    kv5 = lambda a, b: a.reshape(1, b, WINDOW, N_KV_HEADS, HEAD_DIM)
    k_p = kv5(k[:t_p].reshape(batch, seq, D_KV)[:, seq - WINDOW:], batch)
    v_p = kv5(v[:t_p].reshape(batch, seq, D_KV)[:, seq - WINDOW:], batch)
    conv_p = xr[:t_p].reshape(batch, seq, D_RNN)[:, seq - (CONV_W - 1):][None]
    conv_s = xr[t_p:].reshape(dec_batch, s_len, D_RNN)[:, s_len - (CONV_W - 1):][None]
    return (y_p, y_s, k_p, v_p, conv_p, h_p.reshape(1, batch, D_RNN),
            kv5(k_s, dec_batch), kv5(v_s, dec_batch), conv_s, h_s.reshape(1, dec_batch, D_RNN))
```

```python
import functools
import math

import jax
import jax.numpy as jnp
from jax import lax
from jax.experimental import pallas as pl
from jax.experimental.pallas import tpu as pltpu

F32 = jnp.float32
BF16 = jnp.bfloat16
U32 = jnp.uint32

D_MODEL = 1024
N_HEADS = 8
N_KV_HEADS = 2
HEAD_DIM = 64
GROUP = N_HEADS // N_KV_HEADS
WINDOW = 128
D_ATTN = N_HEADS * HEAD_DIM
D_KV = N_KV_HEADS * HEAD_DIM
N_BUCKETS = 32
MAX_DISTANCE = 128
D_RNN = D_MODEL
RNN_BLOCK = 256
N_RNN_BLOCKS = D_RNN // RNN_BLOCK
CONV_W = 4
RG_C = 8.0
N_EXPERTS = 256
TOP_K = 8
N_GROUPS = 8
GROUP_SIZE = N_EXPERTS // N_GROUPS
TOPK_GROUPS = 4
D_EXPERT = D_MODEL // 4
ROUTED_SCALE = 2.5
LN_EPS = 1e-5
DEPTH = 1
ALPHA = (2 * DEPTH) ** 0.25
NEG_INF = -1e30
SM_SCALE = HEAD_DIM ** -0.5

O_Q = 0
O_K = D_ATTN
O_V = O_K + D_KV
O_XR = O_V + D_KV
O_YR = O_XR + D_RNN
O_GA = O_YR + D_RNN
O_GR = O_GA + D_MODEL
D_IN = O_GR + D_MODEL

SUBLANES = 8
VMEM_LIMIT_BYTES = 56 * 1024 * 1024
EXPERT_ROWS = 256
EXPERT_CHUNK = 512
ROW_PAD = SUBLANES


def _params(sem):
    return pltpu.CompilerParams(dimension_semantics=sem, vmem_limit_bytes=VMEM_LIMIT_BYTES)


def _pick_tile(n, candidates):
    for c in candidates:
        if n % c == 0:
            return c
    raise ValueError(f"no tile for {n}")


def _const_spec(shape):
    nd = len(shape)
    return pl.BlockSpec(shape, lambda *_: (0,) * nd)


LANES = 128
ROW_WORDS = D_MODEL // 2
ROW_TILE = ROW_WORDS // LANES
HIGH_HALF = 0xFFFF0000


def _store_token_rows(ref, mat, n, row0=0, base=0):
    as_bits = lambda v: pltpu.bitcast(v.astype(BF16).astype(F32), U32)
    words = (as_bits(mat[:, ROW_WORDS:]) & jnp.uint32(HIGH_HALF)) | (as_bits(mat[:, :ROW_WORDS]) >> 16)
    for s in range(ROW_TILE):
        ref[pl.ds(base + row0 * ROW_TILE + s, n, stride=ROW_TILE), :] = words[:, s * LANES:(s + 1) * LANES]


def _load_token_rows(ref, n, row0=0, base=0):
    words = jnp.concatenate(
        [ref[pl.ds(base + row0 * ROW_TILE + s, n, stride=ROW_TILE), :] for s in range(ROW_TILE)], axis=1)
    return pltpu.bitcast(words << 16, F32), pltpu.bitcast(words & jnp.uint32(HIGH_HALF), F32)


def _two_source_specs(t_p, t_s, tm, width):
    n_p = t_p // tm
    assert t_p % tm == 0 and t_s % tm == 0
    return (pl.BlockSpec((tm, width), lambda i, *_: (jnp.minimum(i, n_p - 1), 0)),
            pl.BlockSpec((tm, width), lambda i, *_: (jnp.maximum(i - n_p, 0), 0)))


def _inproj_kernel(xp_ref, xs_ref, w_ref, q_ref, k_ref, v_ref, xr_ref, gy_ref, sga_ref, sgr_ref, *, n_p):
    x = jnp.where(pl.program_id(0) < n_p, xp_ref[...], xs_ref[...]).astype(BF16)

    def seg(lo, hi):
        return jnp.dot(x, w_ref[:, lo:hi], preferred_element_type=F32)

    q_ref[...] = seg(O_Q, O_K).astype(BF16)
    k_ref[...] = seg(O_K, O_V)
    v_ref[...] = seg(O_V, O_XR)
    xr_ref[...] = seg(O_XR, O_YR)
    gy_ref[...] = jax.nn.gelu(seg(O_YR, O_GA)).astype(BF16)
    sga_ref[...] = jax.nn.sigmoid(seg(O_GA, O_GR)).astype(BF16)
    sgr_ref[...] = jax.nn.sigmoid(seg(O_GR, D_IN)).astype(BF16)


def _inproj(x_p, x_s, w_in_bf16):
    t_p, t_s = x_p.shape[0], x_s.shape[0]
    t = t_p + t_s
    tm = _pick_tile(math.gcd(t_p, t_s), (256, 128, 64, 32, 16, 8))
    row = lambda width: pl.BlockSpec((tm, width), lambda i: (i, 0))
    out_shape = (
        jax.ShapeDtypeStruct((t, D_ATTN), BF16),
        jax.ShapeDtypeStruct((t, D_KV), F32),
        jax.ShapeDtypeStruct((t, D_KV), F32),
        jax.ShapeDtypeStruct((t, D_RNN), F32),
        jax.ShapeDtypeStruct((t, D_RNN), BF16),
        jax.ShapeDtypeStruct((t, D_MODEL), BF16),
        jax.ShapeDtypeStruct((t, D_MODEL), BF16),
    )
    return pl.pallas_call(
        functools.partial(_inproj_kernel, n_p=t_p // tm),
        grid=(t // tm,),
        in_specs=[*_two_source_specs(t_p, t_s, tm, D_MODEL), _const_spec((D_MODEL, D_IN))],
        out_specs=(row(D_ATTN), row(D_KV), row(D_KV), row(D_RNN), row(D_RNN), row(D_MODEL), row(D_MODEL)),
        out_shape=out_shape,
        compiler_params=_params(("parallel",)),
        name="inproj",
    )(x_p, x_s, w_in_bf16)


def _t5_bucket(dist):
    n = jnp.maximum(dist, 0)
    max_exact = N_BUCKETS // 2
    nf = jnp.maximum(n, 1).astype(F32)
    large = max_exact + (jnp.log(nf / max_exact) / math.log(MAX_DISTANCE / max_exact) * (N_BUCKETS - max_exact)).astype(jnp.int32)
    large = jnp.minimum(large, N_BUCKETS - 1)
    return jnp.where(n < max_exact, n, large)


def _bias_table(rel_bias, dist):
    bucket = _t5_bucket(dist)
    rb = rel_bias.astype(F32)
    out = jnp.zeros((N_HEADS, *dist.shape), F32)
    for j in range(N_BUCKETS):
        out = jnp.where(bucket[None] == j, rb[j][:, None, None], out)
    return out


def _softmax_pv(s, sink, v):
    m = jnp.maximum(jnp.max(s, axis=-1, keepdims=True), sink)
    p = jnp.exp(s - m)
    denom = jnp.sum(p, axis=-1, keepdims=True) + jnp.exp(sink - m)
    return jnp.dot(p.astype(BF16), v, preferred_element_type=F32), denom


def _attn_prompt_kernel(sink_ref, q_ref, kc_ref, kp_ref, vc_ref, vp_ref, bias_ref, o_ref):
    n = pl.program_id(1)
    kk = jnp.concatenate([kp_ref[...], kc_ref[...]], axis=0).astype(BF16)
    vv = jnp.concatenate([vp_ref[...], vc_ref[...]], axis=0).astype(BF16)
    q = q_ref[...]
    rows = lax.broadcasted_iota(jnp.int32, (WINDOW, 2 * WINDOW), 0)
    cols = lax.broadcasted_iota(jnp.int32, (WINDOW, 2 * WINDOW), 1)
    dist = rows + WINDOW - cols
    valid = (dist >= 0) & (dist <= WINDOW) & ((n > 0) | (cols >= WINDOW))
    for g in range(N_KV_HEADS):
        kg = kk[:, g * HEAD_DIM:(g + 1) * HEAD_DIM]
        vg = vv[:, g * HEAD_DIM:(g + 1) * HEAD_DIM]
        for h in range(GROUP):
            hh = g * GROUP + h
            qh = q[:, hh * HEAD_DIM:(hh + 1) * HEAD_DIM]
            s = lax.dot_general(qh, kg, (((1,), (1,)), ((), ())), preferred_element_type=F32) * SM_SCALE
            s = jnp.where(valid, s + bias_ref[hh], NEG_INF)
            o, denom = _softmax_pv(s, sink_ref[0, hh], vg)
            o_ref[:, hh * HEAD_DIM:(hh + 1) * HEAD_DIM] = (o / denom).astype(BF16)


def _attn_prompt(q_all, k_all, v_all, bias, sinks, batch, seq):
    nb = seq // WINDOW
    cur = lambda width: pl.BlockSpec((WINDOW, width), lambda b, n: (b * nb + n, 0))
    prev = lambda width: pl.BlockSpec((WINDOW, width), lambda b, n: (b * nb + jnp.maximum(n - 1, 0), 0))
    return pl.pallas_call(
        _attn_prompt_kernel,
        grid=(batch, nb),
        in_specs=[
            pl.BlockSpec(memory_space=pltpu.SMEM),
            cur(D_ATTN), cur(D_KV), prev(D_KV), cur(D_KV), prev(D_KV),
            _const_spec((N_HEADS, WINDOW, 2 * WINDOW)),
        ],
        out_specs=cur(D_ATTN),
        out_shape=jax.ShapeDtypeStruct((batch * seq, D_ATTN), BF16),
        compiler_params=_params(("parallel", "arbitrary")),
        name="attn_prompt",
    )(sinks.reshape(1, N_HEADS).astype(F32), q_all, k_all, k_all, v_all, v_all, bias)


def _attn_sample_kernel(q_ref, kn_ref, vn_ref, kc_ref, vc_ref, bc_ref, bn_ref, sink_ref,
                        o_ref, ko_ref, vo_ref, *, seqs, s_len):
    w = kc_ref.shape[1]
    rows_c = lax.broadcasted_iota(jnp.int32, (GROUP * s_len, w), 0) % s_len
    cols_c = lax.broadcasted_iota(jnp.int32, (GROUP * s_len, w), 1)
    dist_c = rows_c + w - cols_c
    valid_c = (dist_c >= 0) & (dist_c <= WINDOW)
    rows_n = lax.broadcasted_iota(jnp.int32, (GROUP * s_len, s_len), 0) % s_len
    cols_n = lax.broadcasted_iota(jnp.int32, (GROUP * s_len, s_len), 1)
    dist_n = rows_n - cols_n
    valid_n = (dist_n >= 0) & (dist_n <= WINDOW)
    for j in range(seqs):
        r0 = j * s_len
        qj = q_ref[r0:r0 + s_len, :]
        kc = kc_ref[j]
        vc = vc_ref[j]
        kn = kn_ref[r0:r0 + s_len, :]
        vn = vn_ref[r0:r0 + s_len, :]
        ko_ref[j, 0:w - s_len, :] = kc[s_len:, :]
        ko_ref[j, w - s_len:w, :] = kn
        vo_ref[j, 0:w - s_len, :] = vc[s_len:, :]
        vo_ref[j, w - s_len:w, :] = vn
        kcb, vcb, knb, vnb = kc.astype(BF16), vc.astype(BF16), kn.astype(BF16), vn.astype(BF16)
        for g in range(N_KV_HEADS):
            lo, hi = g * HEAD_DIM, (g + 1) * HEAD_DIM
            qs = jnp.concatenate(
                [qj[:, (g * GROUP + h) * HEAD_DIM:(g * GROUP + h + 1) * HEAD_DIM] for h in range(GROUP)], axis=0)
            nt = (((1,), (1,)), ((), ()))
            s_c = lax.dot_general(qs, kcb[:, lo:hi], nt, preferred_element_type=F32) * SM_SCALE
            s_n = lax.dot_general(qs, knb[:, lo:hi], nt, preferred_element_type=F32) * SM_SCALE
            s_c = jnp.where(valid_c, s_c + bc_ref[g], NEG_INF)
            s_n = jnp.where(valid_n, s_n + bn_ref[g], NEG_INF)
            sink = sink_ref[g]
            m = jnp.maximum(jnp.maximum(jnp.max(s_c, axis=-1, keepdims=True), jnp.max(s_n, axis=-1, keepdims=True)), sink)
            p_c = jnp.exp(s_c - m)
            p_n = jnp.exp(s_n - m)
            denom = jnp.sum(p_c, axis=-1, keepdims=True) + jnp.sum(p_n, axis=-1, keepdims=True) + jnp.exp(sink - m)
            o = jnp.dot(p_c.astype(BF16), vcb[:, lo:hi], preferred_element_type=F32)
            o = o + jnp.dot(p_n.astype(BF16), vnb[:, lo:hi], preferred_element_type=F32)
            o = (o / denom).astype(BF16)
            for h in range(GROUP):
                hh = g * GROUP + h
                o_ref[r0:r0 + s_len, hh * HEAD_DIM:(hh + 1) * HEAD_DIM] = o[h * s_len:(h + 1) * s_len, :]


def _attn_sample(q_all, k_all, v_all, cache_k, cache_v, rel_bias, sinks, row0, dec_batch, s_len):
    w = cache_k.shape[1]
    seqs = _pick_tile(dec_batch, (16, 8, 4, 2, 1))
    rows = seqs * s_len
    blk0 = row0 // rows
    assert row0 % rows == 0
    qi = jnp.arange(s_len)
    dist_c = qi[:, None] + w - jnp.arange(w)[None, :]
    dist_n = qi[:, None] - jnp.arange(s_len)[None, :]
    b_c = _bias_table(rel_bias, dist_c).reshape(N_KV_HEADS, GROUP * s_len, w)
    b_n = _bias_table(rel_bias, dist_n).reshape(N_KV_HEADS, GROUP * s_len, s_len)
    sink = jnp.broadcast_to(sinks.astype(F32).reshape(N_KV_HEADS, GROUP, 1, 1), (N_KV_HEADS, GROUP, s_len, 1))
    sink = sink.reshape(N_KV_HEADS, GROUP * s_len, 1)
    tok = lambda width: pl.BlockSpec((rows, width), lambda i: (blk0 + i, 0))
    cache = pl.BlockSpec((seqs, w, D_KV), lambda i: (i, 0, 0))
    return pl.pallas_call(
        functools.partial(_attn_sample_kernel, seqs=seqs, s_len=s_len),
        grid=(dec_batch // seqs,),
        in_specs=[
            tok(D_ATTN), tok(D_KV), tok(D_KV), cache, cache,
            _const_spec(b_c.shape), _const_spec(b_n.shape), _const_spec(sink.shape),
        ],
        out_specs=(pl.BlockSpec((rows, D_ATTN), lambda i: (i, 0)), cache, cache),
        out_shape=(
            jax.ShapeDtypeStruct((dec_batch * s_len, D_ATTN), BF16),
            jax.ShapeDtypeStruct((dec_batch, w, D_KV), F32),
            jax.ShapeDtypeStruct((dec_batch, w, D_KV), F32),
        ),
        compiler_params=_params(("parallel",)),
        name="attn_sample",
    )(q_all, k_all, v_all, cache_k, cache_v, b_c, b_n, sink)


def _softplus(z):
    return jnp.maximum(z, 0.0) + jnp.log1p(jnp.exp(-jnp.abs(z)))


def _block_gate(xcb, w_ref, b_ref):
    parts = [jnp.dot(xcb[:, n * RNN_BLOCK:(n + 1) * RNN_BLOCK], w_ref[n], preferred_element_type=F32)
             for n in range(N_RNN_BLOCKS)]
    return jax.nn.sigmoid(jnp.concatenate(parts, axis=-1) + b_ref[...])


def _lru_coeffs(xc, wa_ref, ba_ref, wx_ref, bx_ref, lam_ref, first_row_unnormalised):
    xcb = xc.astype(BF16)
    r = _block_gate(xcb, wa_ref, ba_ref)
    i = _block_gate(xcb, wx_ref, bx_ref)
    log_a = -RG_C * r * _softplus(-lam_ref[...])
    a = jnp.exp(log_a)
    mult = jnp.sqrt(-jnp.tanh(log_a) * (a * a + 1.0))
    if first_row_unnormalised is not None:
        mult = jnp.where(first_row_unnormalised, 1.0, mult)
    return a, mult * i * xc


def _scan8(a, b):
    shape = a.shape
    grouped = (shape[0] // SUBLANES, SUBLANES, shape[1])
    a = a.reshape(grouped)
    b = b.reshape(grouped)
    r8 = lax.broadcasted_iota(jnp.int32, grouped, 1)
    d = 1
    while d < SUBLANES:
        keep = r8 >= d
        a_sh = jnp.where(keep, pltpu.roll(a, d, 1), 1.0)
        b_sh = jnp.where(keep, pltpu.roll(b, d, 1), 0.0)
        b = a * b_sh + b
        a = a * a_sh
        d *= 2
    return a.reshape(shape), b.reshape(shape)


def _rnn_prompt_kernel(xr_ref, gy_ref, cw_ref, cb_ref, wa_ref, ba_ref, wx_ref, bx_ref, lam_ref,
                       o_ref, nh_ref, ext_ref, a_ref, b_ref, hc_ref, *, tl):
    l = pl.program_id(1)

    @pl.when(l == 0)
    def _():
        ext_ref[0:SUBLANES, :] = jnp.zeros((SUBLANES, D_RNN), F32)
        hc_ref[...] = jnp.zeros((1, D_RNN), F32)

    x = xr_ref[...]
    ext_ref[SUBLANES:, :] = x
    xc = cb_ref[...] + cw_ref[CONV_W - 1:CONV_W, :] * x
    for j in range(1, CONV_W):
        xc = xc + cw_ref[CONV_W - 1 - j:CONV_W - j, :] * ext_ref[SUBLANES - j:SUBLANES - j + tl, :]
    ext_ref[0:SUBLANES, :] = x[tl - SUBLANES:, :]

    row = lax.broadcasted_iota(jnp.int32, (tl, D_RNN), 0)
    a, b = _lru_coeffs(xc, wa_ref, ba_ref, wx_ref, bx_ref, lam_ref, (row == 0) & (l == 0))
    a, b = _scan8(a, b)
    a_ref[...] = a
    b_ref[...] = b

    def chunk(c, h):
        sl = pl.ds(pl.multiple_of(c * SUBLANES, SUBLANES), SUBLANES)
        hc = b_ref[sl, :] + a_ref[sl, :] * h
        b_ref[sl, :] = hc
        return hc[SUBLANES - 1:SUBLANES, :]

    h = lax.fori_loop(0, tl // SUBLANES, chunk, hc_ref[...])
    hc_ref[...] = h
    nh_ref[0] = h
    o_ref[...] = (b_ref[...] * gy_ref[...]).astype(BF16)


def _rnn_prompt(xr_all, gy_all, rnn_w, batch, seq):
    tl =_pick_tile(seq, (256, 128, 64, 32, 16, 8))
    nl = seq // tl
    tok = pl.BlockSpec((tl, D_RNN), lambda b, l: (b * nl + l, 0))
    return pl.pallas_call(
        functools.partial(_rnn_prompt_kernel, tl=tl),
        grid=(batch, nl),
        in_specs=[tok, tok] + [_const_spec(w.shape) for w in rnn_w],
        out_specs=(tok, pl.BlockSpec((1, 1, D_RNN), lambda b, l: (b, 0, 0))),
        out_shape=(jax.ShapeDtypeStruct((batch * seq, D_RNN), BF16), jax.ShapeDtypeStruct((batch, 1, D_RNN), F32)),
        scratch_shapes=[
            pltpu.VMEM((tl + SUBLANES, D_RNN), F32),
            pltpu.VMEM((tl, D_RNN), F32),
            pltpu.VMEM((tl, D_RNN), F32),
            pltpu.VMEM((1, D_RNN), F32),
        ],
        compiler_params=_params(("parallel", "arbitrary")),
        name="rnn_prompt",
    )(xr_all, gy_all, *rnn_w)


def _rnn_sample_kernel(xr_ref, gy_ref, hp_ref, h0_ref, cw_ref, cb_ref, wa_ref, ba_ref, wx_ref, bx_ref, lam_ref,
                       o_ref, nh_ref, *, seqs):
    rows = seqs * SUBLANES
    x = xr_ref[...]
    hp = hp_ref[...]
    r8 = lax.broadcasted_iota(jnp.int32, (rows, D_RNN), 0) % SUBLANES
    xc = cb_ref[...] + cw_ref[CONV_W - 1:CONV_W, :] * x
    for j in range(1, CONV_W):
        shifted = jnp.where(r8 >= j, pltpu.roll(x, j, 0), pltpu.roll(hp, rows - (SUBLANES - j), 0))
        xc = xc + cw_ref[CONV_W - 1 - j:CONV_W - j, :] * shifted
    a, b = _lru_coeffs(xc, wa_ref, ba_ref, wx_ref, bx_ref, lam_ref, None)
    a, b = _scan8(a, b)
    h0 = jnp.broadcast_to(h0_ref[...][:, None, :], (seqs, SUBLANES, D_RNN)).reshape(rows, D_RNN)
    h = b + a * h0
    last = jnp.where(r8 == SUBLANES - 1, h, 0.0).reshape(seqs, SUBLANES, D_RNN)
    nh_ref[...] = jnp.sum(last, axis=1)
    o_ref[...] = (h * gy_ref[...]).astype(BF16)


def _rnn_sample(xr_all, gy_all, hist_pad, h0, rnn_w, row0, dec_batch):
    seqs = _pick_tile(dec_batch, (16, 8))
    rows = seqs * SUBLANES
    assert row0 % rows == 0
    blk0 = row0 // rows
    tok = pl.BlockSpec((rows, D_RNN), lambda i: (blk0 + i, 0))
    return pl.pallas_call(
        functools.partial(_rnn_sample_kernel, seqs=seqs),
        grid=(dec_batch // seqs,),
        in_specs=[tok, tok, pl.BlockSpec((rows, D_RNN), lambda i: (i, 0)), pl.BlockSpec((seqs, D_RNN), lambda i: (i, 0))]
        + [_const_spec(w.shape) for w in rnn_w],
        out_specs=(pl.BlockSpec((rows, D_RNN), lambda i: (i, 0)), pl.BlockSpec((seqs, D_RNN), lambda i: (i, 0))),
        out_shape=(jax.ShapeDtypeStruct((dec_batch * SUBLANES, D_RNN), BF16),
                   jax.ShapeDtypeStruct((dec_batch, D_RNN), F32)),
        compiler_params=_params(("parallel",)),
        name="rnn_sample",
    )(xr_all, gy_all, hist_pad, h0, *rnn_w)


def _layer_norm(z, g, b):
    mu = jnp.mean(z, axis=-1, keepdims=True)
    zc = z - mu
    var = jnp.mean(zc * zc, axis=-1, keepdims=True)
    return zc * lax.rsqrt(var + LN_EPS) * g + b


def _first_index_of_max(vals, iota, axis, sentinel):
    mx = jnp.max(vals, axis=axis, keepdims=True)
    return mx, jnp.min(jnp.where(vals == mx, iota, sentinel), axis=axis, keepdims=True)


def _route(scores, bias):
    t = scores.shape[1]
    grp = scores + bias
    g3 = grp.reshape(N_GROUPS, GROUP_SIZE, t)
    e_in_g = lax.broadcasted_iota(jnp.int32, g3.shape, 1)
    m1, first = _first_index_of_max(g3, e_in_g, 1, GROUP_SIZE)
    m2 = jnp.max(jnp.where(e_in_g == first, -jnp.inf, g3), axis=1, keepdims=True)
    gscore = (m1 + m2).reshape(N_GROUPS, t)
    g_iota = lax.broadcasted_iota(jnp.int32, gscore.shape, 0)
    gmask = jnp.zeros(gscore.shape, jnp.bool_)
    for _ in range(TOPK_GROUPS):
        _, gi = _first_index_of_max(gscore, g_iota, 0, N_GROUPS)
        hit = g_iota == gi
        gmask = gmask | hit
        gscore = jnp.where(hit, -jnp.inf, gscore)
    masked = jnp.where(gmask[:, None, :], g3, -jnp.inf).reshape(N_EXPERTS, t)
    e_iota = lax.broadcasted_iota(jnp.int32, masked.shape, 0)
    idx, wts, hits = [], [], []
    for _ in range(TOP_K):
        _, ei = _first_index_of_max(masked, e_iota, 0, N_EXPERTS)
        hit = e_iota == ei
        idx.append(ei)
        hits.append(hit)
        wts.append(jnp.sum(jnp.where(hit, scores, 0.0), axis=0, keepdims=True))
        masked = jnp.where(hit, -jnp.inf, masked)
    idx = jnp.concatenate(idx, axis=0)
    w = jnp.concatenate(wts, axis=0)
    w = w / jnp.sum(w, axis=0, keepdims=True) * ROUTED_SCALE
    return idx, w, hits


def _merge_kernel(xp_ref, xs_ref, aop_ref, aos_ref, rop_ref, ros_ref, sga_ref, sgr_ref, woa_ref, wor_ref, wout_ref, g1_ref, b1_ref,
                  wrt_ref, rb_ref, wsg_ref, wsu_ref, wsd_ref,
                  x1r_ref, base_ref, idx_ref, wt_ref, rank_ref, cnt_ref, carry_ref, *, n_p, tm):
    i = pl.program_id(0)

    @pl.when(i == 0)
    def _():
        carry_ref[...] = jnp.zeros(carry_ref.shape, F32)

    is_prompt = i < n_p
    x = jnp.where(is_prompt, xp_ref[...], xs_ref[...])
    pa = jnp.dot(jnp.where(is_prompt, aop_ref[...], aos_ref[...]), woa_ref[...], preferred_element_type=F32)
    pr = jnp.dot(jnp.where(is_prompt, rop_ref[...], ros_ref[...]), wor_ref[...], preferred_element_type=F32)
    merged = sga_ref[...] * pa + sgr_ref[...] * pr
    z = ALPHA * x + jnp.dot(merged.astype(BF16), wout_ref[...], preferred_element_type=F32)
    x1 = _layer_norm(z, g1_ref[...], b1_ref[...])
    _store_token_rows(x1r_ref, x1, tm)
    x1b = x1.astype(BF16)
    u = jax.nn.silu(jnp.dot(x1b, wsg_ref[...], preferred_element_type=F32)) * jnp.dot(x1b, wsu_ref[...], preferred_element_type=F32)
    shared = jnp.dot(u.astype(BF16), wsd_ref[...], preferred_element_type=F32)
    base_ref[...] = ALPHA * x1 + shared
    logits = lax.dot_general(wrt_ref[...], x1, (((1,), (1,)), ((), ())), preferred_element_type=F32)
    idx, w, hits = _route(jax.nn.sigmoid(logits), rb_ref[...])
    idx_ref[...] = idx
    wt_ref[...] = w

    chosen = functools.reduce(jnp.logical_or, hits)
    chosen_f = jnp.where(chosen, 1.0, 0.0)
    earlier = (lax.broadcasted_iota(jnp.int32, (tm, tm), 0) < lax.broadcasted_iota(jnp.int32, (tm, tm), 1))
    prefix = jnp.dot(chosen_f.astype(BF16), jnp.where(earlier, 1.0, 0.0).astype(BF16), preferred_element_type=F32)
    before = prefix + carry_ref[...]
    ranks = [jnp.sum(jnp.where(hit, before, 0.0), axis=0, keepdims=True) for hit in hits]
    rank_ref[...] = jnp.concatenate(ranks, axis=0).astype(jnp.int32)
    carry_ref[...] = carry_ref[...] + jnp.sum(chosen_f, axis=1, keepdims=True)
    cnt_ref[...] = carry_ref[...]


def _merge(x_p, x_s, attn_p, attn_s, rnn_p, rnn_s, sga, sgr, weights):
    t_p, t_s = x_p.shape[0], x_s.shape[0]
    t = t_p + t_s
    tm = _pick_tile(math.gcd(t_p, t_s), (256, 128))
    row = lambda width: pl.BlockSpec((tm, width), lambda i: (i, 0))
    col = pl.BlockSpec((TOP_K, tm), lambda i: (0, i))

    return pl.pallas_call(
        functools.partial(_merge_kernel, n_p=t_p // tm, tm=tm),
        grid=(t // tm,),
        in_specs=[*_two_source_specs(t_p, t_s, tm, D_MODEL), *_two_source_specs(t_p, t_s, tm, D_ATTN),
                  *_two_source_specs(t_p, t_s, tm, D_RNN), row(D_MODEL), row(D_MODEL)]
        + [_const_spec(w.shape) for w in weights],
        out_specs=(pl.BlockSpec((tm * ROW_TILE, LANES), lambda i: (i, 0)), row(D_MODEL), col, col, col,
                   _const_spec((N_EXPERTS, 1))),
        out_shape=(
            jax.ShapeDtypeStruct((t * ROW_TILE, LANES), U32),
            jax.ShapeDtypeStruct((t, D_MODEL), F32),
            jax.ShapeDtypeStruct((TOP_K, t), jnp.int32),
            jax.ShapeDtypeStruct((TOP_K, t), F32),
            jax.ShapeDtypeStruct((TOP_K, t), jnp.int32),
            jax.ShapeDtypeStruct((N_EXPERTS, 1), F32),
        ),
        scratch_shapes=[pltpu.VMEM((N_EXPERTS, 1), F32)],
        compiler_params=_params(("arbitrary",)),
        name="merge_ln1_route",
    )(x_p, x_s, attn_p, attn_s, rnn_p, rnn_s, sga, sgr, *weights)


def _expert_layout(counts, n_assign):
    counts = counts.reshape(N_EXPERTS).astype(jnp.int32)
    padded = (counts + ROW_PAD - 1) // ROW_PAD * ROW_PAD
    pend = jnp.cumsum(padded)
    pstart = pend - padded
    rows_alloc = (n_assign + N_EXPERTS * (ROW_PAD - 1)) // ROW_PAD * ROW_PAD + EXPERT_CHUNK
    n_chunks = jnp.maximum((counts + EXPERT_CHUNK - 1) // EXPERT_CHUNK, 1)
    pad_group = jnp.where(counts % ROW_PAD != 0, pstart + counts // ROW_PAD * ROW_PAD, -1)
    zero_meta = jnp.concatenate([pad_group, pend[-1:]]).astype(jnp.int32)
    ffn_end = jnp.max(pstart + n_chunks * EXPERT_CHUNK).reshape(1)
    return pstart, zero_meta, n_chunks, counts, ffn_end, rows_alloc


def _dest_kernel(idx_ref, rank_ref, pstart_ref, dest_ref):
    e_iota = lax.broadcasted_iota(jnp.int32, (N_EXPERTS, idx_ref.shape[1]), 0)
    starts = [jnp.sum(jnp.where(e_iota == idx_ref[k:k + 1, :], pstart_ref[...], 0), axis=0, keepdims=True)
              for k in range(TOP_K)]
    dest_ref[...] = jnp.concatenate(starts, axis=0) + rank_ref[...]


def _dest_rows(idx_t, rank_t, pstart):
    t = idx_t.shape[1]
    tm = _pick_tile(t, (512, 256, 128))
    col = pl.BlockSpec((TOP_K, tm), lambda i: (0, i))
    return pl.pallas_call(
        _dest_kernel,
        grid=(t // tm,),
        in_specs=[col, col, _const_spec((N_EXPERTS, 1))],
        out_specs=col,
        out_shape=jax.ShapeDtypeStruct((TOP_K, t), jnp.int32),
        compiler_params=_params(("parallel",)),
        name="dest_rows",
    )(idx_t, rank_t, pstart.reshape(N_EXPERTS, 1))


def _token_rows(r, n=1):
    return pl.ds(pl.multiple_of(r * ROW_TILE, ROW_TILE), n * ROW_TILE)


def _zero_row_groups(zero_ref, dst_ref, sem, first_group, n_groups):
    def start(g, c):
        pltpu.make_async_copy(zero_ref, dst_ref.at[_token_rows(g * ROW_PAD, ROW_PAD)], sem).start()
        return c

    lax.fori_loop(first_group, n_groups, start, 0)
    return n_groups - first_group


def _wait_zero_copies(zero_ref, dst_ref, sem, n):
    def wait(_, c):
        pltpu.make_async_copy(zero_ref, dst_ref.at[_token_rows(0, ROW_PAD)], sem).wait()
        return c

    lax.fori_loop(0, n, wait, 0)


def _dispatch_kernel(zero_meta_ref, dest_ref, x_ref, xs_ref, zero_ref, sem, *, tm, rows_alloc):
    @pl.when(pl.program_id(0) == 0)
    def _():
        zero_ref[...] = jnp.zeros(zero_ref.shape, U32)

        def pad_group(e, n):
            row = zero_meta_ref[e]

            @pl.when(row >= 0)
            def _():
                pltpu.make_async_copy(zero_ref, xs_ref.at[_token_rows(row, ROW_PAD)], sem).start()

            return n + jnp.where(row >= 0, 1, 0)

        n = lax.fori_loop(0, N_EXPERTS, pad_group, 0)
        n = n + _zero_row_groups(zero_ref, xs_ref, sem, zero_meta_ref[N_EXPERTS] // ROW_PAD, rows_alloc // ROW_PAD)
        _wait_zero_copies(zero_ref, xs_ref, sem, n)

    def issue(t, c):
        for k in range(TOP_K):
            pltpu.make_async_copy(x_ref.at[_token_rows(t)], xs_ref.at[_token_rows(dest_ref[k, t])], sem).start()
        return c

    lax.fori_loop(0, tm, issue, 0)
    for _ in range(TOP_K):
        pltpu.make_async_copy(x_ref, xs_ref.at[_token_rows(0, tm)], sem).wait()


def _dispatch(x1r, dest, zero_meta, rows_alloc):
    t = dest.shape[1]
    tm = _pick_tile(t, (1024, 512, 256, 128))
    grid_spec = pltpu.PrefetchScalarGridSpec(
        num_scalar_prefetch=1,
        grid=(t // tm,),
        in_specs=[
            pl.BlockSpec((TOP_K, tm), lambda i, ps: (0, i), memory_space=pltpu.SMEM),
            pl.BlockSpec((tm * ROW_TILE, LANES), lambda i, ps: (i, 0)),
        ],
        out_specs=pl.BlockSpec(memory_space=pl.ANY),
        scratch_shapes=[pltpu.VMEM((ROW_PAD * ROW_TILE, LANES), U32), pltpu.SemaphoreType.DMA],
    )
    return pl.pallas_call(
        functools.partial(_dispatch_kernel, tm=tm, rows_alloc=rows_alloc),
        grid_spec=grid_spec,
        out_shape=jax.ShapeDtypeStruct((rows_alloc * ROW_TILE, LANES), U32),
        compiler_params=_params(("arbitrary",)),
        name="dispatch",
    )(zero_meta, dest, x1r)


def _expert_kernel(pstart_ref, nch_ref, rows_ref, end_ref, wg_ref, wu_ref, wd_ref, xs_ref, o_ref,
                   xbuf_ref, obuf_ref, wgb_ref, wub_ref, wdb_ref, zero_ref, done_ref, in_sem, out_sem, *, rows_alloc):
    e = pl.program_id(0)
    n_e = pl.num_programs(0)
    start = pstart_ref[e]
    nch = nch_ref[e]
    buf_rows = EXPERT_CHUNK * ROW_TILE

    def in_copy(row, slot):
        return pltpu.make_async_copy(xs_ref.at[_token_rows(row, EXPERT_CHUNK)],
                                     xbuf_ref.at[pl.ds(slot * buf_rows, buf_rows)], in_sem.at[slot])

    def out_copy(row, slot):
        return pltpu.make_async_copy(obuf_ref.at[pl.ds(slot * buf_rows, buf_rows)],
                                     o_ref.at[_token_rows(row, EXPERT_CHUNK)], out_sem)

    @pl.when(e == 0)
    def _():
        done_ref[0] = 0
        obuf_ref[...] = jnp.zeros(obuf_ref.shape, U32)
        in_copy(start, 0).start()

    wgb_ref[...] = wg_ref[...].astype(BF16)
    wub_ref[...] = wu_ref[...].astype(BF16)
    wdb_ref[...] = wd_ref[...].astype(BF16)
    done = done_ref[0]

    def chunk(c, carry):
        g = done + c
        slot = g % 2
        row = start + c * EXPERT_CHUNK
        in_copy(row, slot).wait()
        last = c + 1 == nch
        next_row = jnp.where(last, pstart_ref[jnp.minimum(e + 1, n_e - 1)], row + EXPERT_CHUNK)

        @pl.when(jnp.logical_not(last & (e == n_e - 1)))
        def _():
            in_copy(next_row, 1 - slot).start()

        base = pl.multiple_of(slot * buf_rows, buf_rows)
        for h in range(EXPERT_CHUNK // EXPERT_ROWS):
            @pl.when(c * EXPERT_CHUNK + h * EXPERT_ROWS < jnp.maximum(rows_ref[e], 1))
            def _():
                xb = jnp.concatenate(_load_token_rows(xbuf_ref, EXPERT_ROWS, row0=h * EXPERT_ROWS, base=base),
                                     axis=1).astype(BF16)
                gate = jnp.dot(xb, wgb_ref[...], preferred_element_type=F32)
                up = jnp.dot(xb, wub_ref[...], preferred_element_type=F32)
                act = (jax.nn.silu(gate) * up).astype(BF16)
                _store_token_rows(obuf_ref, jnp.dot(act, wdb_ref[...], preferred_element_type=F32),
                                  EXPERT_ROWS, row0=h * EXPERT_ROWS, base=base)

        @pl.when(g > 0)
        def _():
            out_copy(0, 0).wait()

        out_copy(row, slot).start()
        return carry

    lax.fori_loop(0, nch, chunk, 0)
    done_ref[0] = done + nch

    @pl.when(e == n_e - 1)
    def _():
        out_copy(0, 0).wait()
        zero_ref[...] = jnp.zeros(zero_ref.shape, U32)
        n = _zero_row_groups(zero_ref, o_ref, out_sem, end_ref[0] // ROW_PAD, rows_alloc // ROW_PAD)
        _wait_zero_copies(zero_ref, o_ref, out_sem, n)


def _expert_ffn(xs, pstart, n_chunks, rows, ffn_end, rows_alloc, w_e_gate, w_e_up, w_e_down):
    weight = lambda shape: pl.BlockSpec((None, *shape), lambda e, *_: (e, 0, 0))
    grid_spec = pltpu.PrefetchScalarGridSpec(
        num_scalar_prefetch=4,
        grid=(N_EXPERTS,),
        in_specs=[weight((D_MODEL, D_EXPERT)), weight((D_MODEL, D_EXPERT)), weight((D_EXPERT, D_MODEL)),
                  pl.BlockSpec(memory_space=pl.ANY)],
        out_specs=pl.BlockSpec(memory_space=pl.ANY),
        scratch_shapes=[
            pltpu.VMEM((2 * EXPERT_CHUNK * ROW_TILE, LANES), U32),
            pltpu.VMEM((2 * EXPERT_CHUNK * ROW_TILE, LANES), U32),
            pltpu.VMEM((D_MODEL, D_EXPERT), BF16), pltpu.VMEM((D_MODEL, D_EXPERT), BF16),
            pltpu.VMEM((D_EXPERT, D_MODEL), BF16),
            pltpu.VMEM((ROW_PAD * ROW_TILE, LANES), U32),
            pltpu.SMEM((1,), jnp.int32),
            pltpu.SemaphoreType.DMA((2,)), pltpu.SemaphoreType.DMA,
        ],
    )
    return pl.pallas_call(
        functools.partial(_expert_kernel, rows_alloc=rows_alloc),
        grid_spec=grid_spec,
        out_shape=jax.ShapeDtypeStruct((rows_alloc * ROW_TILE, LANES), U32),
        compiler_params=_params(("arbitrary",)),
        name="expert_ffn",
    )(pstart, n_chunks, rows, ffn_end, w_e_gate, w_e_up, w_e_down, xs)


def _combine_kernel(dest_ref, dest_next_ref, w_ref, base_ref, g_ref, b_ref, outs_ref, yp_ref, ys_ref, buf_ref, sem,
                    *, tm, n_p):
    i = pl.program_id(0)
    slot_rows = TOP_K * tm

    def gather(d_ref, slot):
        def issue(t, c):
            for k in range(TOP_K):
                pltpu.make_async_copy(outs_ref.at[_token_rows(d_ref[k, t])],
                                      buf_ref.at[_token_rows(slot * slot_rows + k * tm + t)], sem.at[slot]).start()
            return c

        lax.fori_loop(0, tm, issue, 0)

    @pl.when(i == 0)
    def _():
        gather(dest_ref, 0)

    @pl.when(i + 1 < pl.num_programs(0))
    def _():
        gather(dest_next_ref, (i + 1) % 2)

    slot = i % 2
    for k in range(TOP_K):
        pltpu.make_async_copy(outs_ref.at[_token_rows(0, tm)], buf_ref.at[_token_rows(slot * slot_rows + k * tm, tm)],
                              sem.at[slot]).wait()

    w = w_ref[...]
    base = pl.multiple_of(slot * slot_rows * ROW_TILE, ROW_TILE)
    halves = [None, None]
    for k in range(TOP_K):
        wk = w[:, k:k + 1]
        for j, rows in enumerate(_load_token_rows(buf_ref, tm, row0=k * tm, base=base)):
            halves[j] = wk * rows if halves[j] is None else halves[j] + wk * rows
    y = _layer_norm(base_ref[...] + jnp.concatenate(halves, axis=1), g_ref[...], b_ref[...])

    @pl.when(i < n_p)
    def _():
        yp_ref[...] = y

    @pl.when(i >= n_p)
    def _():
        ys_ref[...] = y


def _combine(out_sorted, dest, w_tok, base, g2, b2, t_p, t_s):
    t = t_p + t_s
    tm = _pick_tile(math.gcd(t_p, t_s), (256, 128))
    row = pl.BlockSpec((tm, D_MODEL), lambda i: (i, 0))
    out_p, out_s = _two_source_specs(t_p, t_s, tm, D_MODEL)
    n_tiles = t // tm
    return pl.pallas_call(
        functools.partial(_combine_kernel, tm=tm, n_p=t_p // tm),
        grid=(n_tiles,),
        in_specs=[
            pl.BlockSpec((TOP_K, tm), lambda i: (0, i), memory_space=pltpu.SMEM),
            pl.BlockSpec((TOP_K, tm), lambda i: (0, jnp.minimum(i + 1, n_tiles - 1)), memory_space=pltpu.SMEM),
            pl.BlockSpec((tm, TOP_K), lambda i: (i, 0)),
            row, _const_spec((1, D_MODEL)), _const_spec((1, D_MODEL)),
            pl.BlockSpec(memory_space=pl.ANY),
        ],
        out_specs=(out_p, out_s),
        out_shape=(jax.ShapeDtypeStruct((t_p, D_MODEL), F32), jax.ShapeDtypeStruct((t_s, D_MODEL), F32)),
        scratch_shapes=[pltpu.VMEM((2 * TOP_K * tm * ROW_TILE, LANES), U32), pltpu.SemaphoreType.DMA((2,))],
        compiler_params=_params(("arbitrary",)),
        name="combine_ln2",
    )(dest, dest, w_tok, base, g2, b2, out_sorted)


def kernel(x_prompt, x_sample, cache_k, cache_v, state_conv, state_rnn, w_in, conv_w, conv_b, w_gate_a, b_gate_a, w_gate_x, b_gate_x, lru_lambda, rel_bias, sinks, w_o_attn, w_o_rnn, w_out, ln1_g, ln1_b, w_router, router_bias, w_e_gate, w_e_up, w_e_down, w_s_gate, w_s_up, w_s_down, ln2_g, ln2_b):
    assert w_in.shape[0] == DEPTH == 1
    batch, seq, _ = x_prompt.shape
    dec_batch, s_len, _ = x_sample.shape
    w_cache = cache_k.shape[2]
    assert s_len == SUBLANES and seq % WINDOW == 0 and w_cache == WINDOW
    t_p = batch * seq
    t_s = dec_batch * s_len
    vec = lambda a: a[0].reshape(1, -1).astype(F32)

    x_p = x_prompt.reshape(t_p, D_MODEL)
    x_s = x_sample.reshape(t_s, D_MODEL)
    q, k, v, xr, gy, sga, sgr = _inproj(x_p, x_s, w_in[0].astype(BF16))

    qi = jnp.arange(WINDOW)
    dist = qi[:, None] + WINDOW - jnp.arange(2 * WINDOW)[None, :]
    bias_p = _bias_table(rel_bias, dist)
    attn_p = _attn_prompt(q, k, v, bias_p, sinks[0], batch, seq)
    attn_s, k_s, v_s = _attn_sample(
        q, k, v, cache_k[0].reshape(dec_batch, w_cache, D_KV), cache_v[0].reshape(dec_batch, w_cache, D_KV),
        rel_bias, sinks[0], t_p, dec_batch, s_len)

    rnn_w = (conv_w[0], vec(conv_b), w_gate_a[0].astype(BF16), vec(b_gate_a), w_gate_x[0].astype(BF16),
             vec(b_gate_x), vec(lru_lambda))
    rnn_p, h_p = _rnn_prompt(xr, gy, rnn_w, batch, seq)
    hist_pad = jnp.pad(state_conv[0], ((0, 0), (SUBLANES - (CONV_W - 1), 0), (0, 0))).reshape(t_s, D_RNN)
    rnn_s, h_s = _rnn_sample(xr, gy, hist_pad, state_rnn[0], rnn_w, t_p, dec_batch)

    merge_w = (w_o_attn[0].astype(BF16), w_o_rnn[0].astype(BF16), w_out[0].astype(BF16), vec(ln1_g), vec(ln1_b),
               w_router[0].T, router_bias[0].reshape(N_EXPERTS, 1), w_s_gate[0].astype(BF16),
               w_s_up[0].astype(BF16), w_s_down[0].astype(BF16))
    x1r, base, idx_t, wt_t, rank_t, counts = _merge(x_p, x_s, attn_p, attn_s, rnn_p, rnn_s, sga, sgr, merge_w)

    pstart, zero_meta, n_chunks, rows, ffn_end, rows_alloc = _expert_layout(counts, (t_p + t_s) * TOP_K)
    dest = _dest_rows(idx_t, rank_t, pstart)
    xs = _dispatch(x1r, dest, zero_meta, rows_alloc)
    out_sorted = _expert_ffn(xs, pstart, n_chunks, rows, ffn_end, rows_alloc, w_e_gate[0], w_e_up[0], w_e_down[0])
    y_p, y_s = _combine(out_sorted, dest, wt_t.T, base, vec(ln2_g), vec(ln2_b), t_p, t_s)
    y_p = y_p.reshape(batch, seq, D_MODEL)
    y_s = y_s.reshape(dec_batch, s_len, D_MODEL)
    kv5 = lambda a, b: a.reshape(1, b, WINDOW, N_KV_HEADS, HEAD_DIM)
    k_p = kv5(k[:t_p].reshape(batch, seq, D_KV)[:, seq - WINDOW:], batch)
    v_p = kv5(v[:t_p].reshape(batch, seq, D_KV)[:, seq - WINDOW:], batch)
    conv_p = xr[:t_p].reshape(batch, seq, D_RNN)[:, seq - (CONV_W - 1):][None]
    conv_s = xr[t_p:].reshape(dec_batch, s_len, D_RNN)[:, s_len - (CONV_W - 1):][None]
    return (y_p, y_s, k_p, v_p, conv_p, h_p.reshape(1, batch, D_RNN),
            kv5(k_s, dec_batch), kv5(v_s, dec_batch), conv_s, h_s.reshape(1, dec_batch, D_RNN))
```

```python
import functools
import math

import jax
import jax.numpy as jnp
from jax import lax
from jax.experimental import pallas as pl
from jax.experimental.pallas import tpu as pltpu
from jax.experimental.pallas import tpu_sc as plsc

F32 = jnp.float32
BF16 = jnp.bfloat16
WORD = jnp.int32

D_MODEL = 1024
N_HEADS = 8
N_KV_HEADS = 2
HEAD_DIM = 64
GROUP = N_HEADS // N_KV_HEADS
WINDOW = 128
D_ATTN = N_HEADS * HEAD_DIM
D_KV = N_KV_HEADS * HEAD_DIM
N_BUCKETS = 32
MAX_DISTANCE = 128
D_RNN = D_MODEL
RNN_BLOCK = 256
N_RNN_BLOCKS = D_RNN // RNN_BLOCK
CONV_W = 4
RG_C = 8.0
N_EXPERTS = 256
TOP_K = 8
N_GROUPS = 8
GROUP_SIZE = N_EXPERTS // N_GROUPS
TOPK_GROUPS = 4
D_EXPERT = D_MODEL // 4
ROUTED_SCALE = 2.5
LN_EPS = 1e-5
DEPTH = 1
ALPHA = (2 * DEPTH) ** 0.25
NEG_INF = -1e30
SM_SCALE = HEAD_DIM ** -0.5

O_Q = 0
O_K = D_ATTN
O_V = O_K + D_KV
O_XR = O_V + D_KV
O_YR = O_XR + D_RNN
O_GA = O_YR + D_RNN
O_GR = O_GA + D_MODEL
D_IN = O_GR + D_MODEL

SUBLANES = 8
VMEM_LIMIT_BYTES = 56 * 1024 * 1024
EXPERT_ROWS = 256
EXPERT_CHUNK = 512
ROW_PAD = SUBLANES


def _params(sem):
    return pltpu.CompilerParams(dimension_semantics=sem, vmem_limit_bytes=VMEM_LIMIT_BYTES)


def _pick_tile(n, candidates):
    for c in candidates:
        if n % c == 0:
            return c
    raise ValueError(f"no tile for {n}")


def _const_spec(shape):
    nd = len(shape)
    return pl.BlockSpec(shape, lambda *_: (0,) * nd)


LANES = 128
ROW_WORDS = D_MODEL // 2
ROW_TILE = ROW_WORDS // LANES
HIGH_HALF = -65536


def _pack_words(mat):
    as_bits = lambda v: pltpu.bitcast(v.astype(BF16).astype(F32), WORD)
    return (as_bits(mat[:, ROW_WORDS:]) & HIGH_HALF) | lax.shift_right_logical(as_bits(mat[:, :ROW_WORDS]), 16)


def _unpack_words(words):
    return pltpu.bitcast(words << 16, F32), pltpu.bitcast(words & HIGH_HALF, F32)


def _store_token_rows(ref, mat, n, row0=0, base=0):
    words = _pack_words(mat)
    for s in range(ROW_TILE):
        ref[pl.ds(base + row0 * ROW_TILE + s, n, stride=ROW_TILE), :] = words[:, s * LANES:(s + 1) * LANES]


def _load_token_rows(ref, n, row0=0, base=0):
    words = jnp.concatenate(
        [ref[pl.ds(base + row0 * ROW_TILE + s, n, stride=ROW_TILE), :] for s in range(ROW_TILE)], axis=1)
    return _unpack_words(words)


def _two_source_specs(t_p, t_s, tm, width):
    n_p = t_p // tm
    assert t_p % tm == 0 and t_s % tm == 0
    return (pl.BlockSpec((tm, width), lambda i, *_: (jnp.minimum(i, n_p - 1), 0)),
            pl.BlockSpec((tm, width), lambda i, *_: (jnp.maximum(i - n_p, 0), 0)))


def _inproj_kernel(xp_ref, xs_ref, w_ref, q_ref, k_ref, v_ref, xr_ref, gy_ref, sga_ref, sgr_ref, *, n_p):
    x = jnp.where(pl.program_id(0) < n_p, xp_ref[...], xs_ref[...]).astype(BF16)

    def seg(lo, hi):
        return jnp.dot(x, w_ref[:, lo:hi], preferred_element_type=F32)

    q_ref[...] = seg(O_Q, O_K).astype(BF16)
    k_ref[...] = seg(O_K, O_V)
    v_ref[...] = seg(O_V, O_XR)
    xr_ref[...] = seg(O_XR, O_YR)
    gy_ref[...] = jax.nn.gelu(seg(O_YR, O_GA)).astype(BF16)
    sga_ref[...] = jax.nn.sigmoid(seg(O_GA, O_GR)).astype(BF16)
    sgr_ref[...] = jax.nn.sigmoid(seg(O_GR, D_IN)).astype(BF16)


def _inproj(x_p, x_s, w_in_bf16):
    t_p, t_s = x_p.shape[0], x_s.shape[0]
    t = t_p + t_s
    tm = _pick_tile(math.gcd(t_p, t_s), (256, 128, 64, 32, 16, 8))
    row = lambda width: pl.BlockSpec((tm, width), lambda i: (i, 0))
    out_shape = (
        jax.ShapeDtypeStruct((t, D_ATTN), BF16),
        jax.ShapeDtypeStruct((t, D_KV), F32),
        jax.ShapeDtypeStruct((t, D_KV), F32),
        jax.ShapeDtypeStruct((t, D_RNN), F32),
        jax.ShapeDtypeStruct((t, D_RNN), BF16),
        jax.ShapeDtypeStruct((t, D_MODEL), BF16),
        jax.ShapeDtypeStruct((t, D_MODEL), BF16),
    )
    return pl.pallas_call(
        functools.partial(_inproj_kernel, n_p=t_p // tm),
        grid=(t // tm,),
        in_specs=[*_two_source_specs(t_p, t_s, tm, D_MODEL), _const_spec((D_MODEL, D_IN))],
        out_specs=(row(D_ATTN), row(D_KV), row(D_KV), row(D_RNN), row(D_RNN), row(D_MODEL), row(D_MODEL)),
        out_shape=out_shape,
        compiler_params=_params(("parallel",)),
        name="inproj",
    )(x_p, x_s, w_in_bf16)


def _t5_bucket(dist):
    n = jnp.maximum(dist, 0)
    max_exact = N_BUCKETS // 2
    nf = jnp.maximum(n, 1).astype(F32)
    large = max_exact + (jnp.log(nf / max_exact) / math.log(MAX_DISTANCE / max_exact) * (N_BUCKETS - max_exact)).astype(jnp.int32)
    large = jnp.minimum(large, N_BUCKETS - 1)
    return jnp.where(n < max_exact, n, large)


def _bias_table(rel_bias, dist):
    bucket = _t5_bucket(dist)
    rb = rel_bias.astype(F32)
    out = jnp.zeros((N_HEADS, *dist.shape), F32)
    for j in range(N_BUCKETS):
        out = jnp.where(bucket[None] == j, rb[j][:, None, None], out)
    return out


def _softmax_pv(s, sink, v):
    m = jnp.maximum(jnp.max(s, axis=-1, keepdims=True), sink)
    p = jnp.exp(s - m)
    denom = jnp.sum(p, axis=-1, keepdims=True) + jnp.exp(sink - m)
    return jnp.dot(p.astype(BF16), v, preferred_element_type=F32), denom


def _attn_prompt_kernel(sink_ref, q_ref, kc_ref, kp_ref, vc_ref, vp_ref, bias_ref, o_ref):
    n = pl.program_id(1)
    kk = jnp.concatenate([kp_ref[...], kc_ref[...]], axis=0).astype(BF16)
    vv = jnp.concatenate([vp_ref[...], vc_ref[...]], axis=0).astype(BF16)
    q = q_ref[...]
    rows = lax.broadcasted_iota(jnp.int32, (WINDOW, 2 * WINDOW), 0)
    cols = lax.broadcasted_iota(jnp.int32, (WINDOW, 2 * WINDOW), 1)
    dist = rows + WINDOW - cols
    valid = (dist >= 0) & (dist <= WINDOW) & ((n > 0) | (cols >= WINDOW))
    for g in range(N_KV_HEADS):
        kg = kk[:, g * HEAD_DIM:(g + 1) * HEAD_DIM]
        vg = vv[:, g * HEAD_DIM:(g + 1) * HEAD_DIM]
        for h in range(GROUP):
            hh = g * GROUP + h
            qh = q[:, hh * HEAD_DIM:(hh + 1) * HEAD_DIM]
            s = lax.dot_general(qh, kg, (((1,), (1,)), ((), ())), preferred_element_type=F32) * SM_SCALE
            s = jnp.where(valid, s + bias_ref[hh], NEG_INF)
            o, denom = _softmax_pv(s, sink_ref[0, hh], vg)
            o_ref[:, hh * HEAD_DIM:(hh + 1) * HEAD_DIM] = (o / denom).astype(BF16)


def _attn_prompt(q_all, k_all, v_all, bias, sinks, batch, seq):
    nb = seq // WINDOW
    cur = lambda width: pl.BlockSpec((WINDOW, width), lambda b, n: (b * nb + n, 0))
    prev = lambda width: pl.BlockSpec((WINDOW, width), lambda b, n: (b * nb + jnp.maximum(n - 1, 0), 0))
    return pl.pallas_call(
        _attn_prompt_kernel,
        grid=(batch, nb),
        in_specs=[
            pl.BlockSpec(memory_space=pltpu.SMEM),
            cur(D_ATTN), cur(D_KV), prev(D_KV), cur(D_KV), prev(D_KV),
            _const_spec((N_HEADS, WINDOW, 2 * WINDOW)),
        ],
        out_specs=cur(D_ATTN),
        out_shape=jax.ShapeDtypeStruct((batch * seq, D_ATTN), BF16),
        compiler_params=_params(("parallel", "arbitrary")),
        name="attn_prompt",
    )(sinks.reshape(1, N_HEADS).astype(F32), q_all, k_all, k_all, v_all, v_all, bias)


def _attn_sample_kernel(q_ref, kn_ref, vn_ref, kc_ref, vc_ref, bc_ref, bn_ref, sink_ref,
                        o_ref, ko_ref, vo_ref, *, seqs, s_len):
    w = kc_ref.shape[1]
    rows_c = lax.broadcasted_iota(jnp.int32, (GROUP * s_len, w), 0) % s_len
    cols_c = lax.broadcasted_iota(jnp.int32, (GROUP * s_len, w), 1)
    dist_c = rows_c + w - cols_c
    valid_c = (dist_c >= 0) & (dist_c <= WINDOW)
    rows_n = lax.broadcasted_iota(jnp.int32, (GROUP * s_len, s_len), 0) % s_len
    cols_n = lax.broadcasted_iota(jnp.int32, (GROUP * s_len, s_len), 1)
    dist_n = rows_n - cols_n
    valid_n = (dist_n >= 0) & (dist_n <= WINDOW)
    for j in range(seqs):
        r0 = j * s_len
        qj = q_ref[r0:r0 + s_len, :]
        kc = kc_ref[j]
        vc = vc_ref[j]
        kn = kn_ref[r0:r0 + s_len, :]
        vn = vn_ref[r0:r0 + s_len, :]
        ko_ref[j, 0:w - s_len, :] = kc[s_len:, :]
        ko_ref[j, w - s_len:w, :] = kn
        vo_ref[j, 0:w - s_len, :] = vc[s_len:, :]
        vo_ref[j, w - s_len:w, :] = vn
        kcb, vcb, knb, vnb = kc.astype(BF16), vc.astype(BF16), kn.astype(BF16), vn.astype(BF16)
        for g in range(N_KV_HEADS):
            lo, hi = g * HEAD_DIM, (g + 1) * HEAD_DIM
            qs = jnp.concatenate(
                [qj[:, (g * GROUP + h) * HEAD_DIM:(g * GROUP + h + 1) * HEAD_DIM] for h in range(GROUP)], axis=0)
            nt = (((1,), (1,)), ((), ()))
            s_c = lax.dot_general(qs, kcb[:, lo:hi], nt, preferred_element_type=F32) * SM_SCALE
            s_n = lax.dot_general(qs, knb[:, lo:hi], nt, preferred_element_type=F32) * SM_SCALE
            s_c = jnp.where(valid_c, s_c + bc_ref[g], NEG_INF)
            s_n = jnp.where(valid_n, s_n + bn_ref[g], NEG_INF)
            sink = sink_ref[g]
            m = jnp.maximum(jnp.maximum(jnp.max(s_c, axis=-1, keepdims=True), jnp.max(s_n, axis=-1, keepdims=True)), sink)
            p_c = jnp.exp(s_c - m)
            p_n = jnp.exp(s_n - m)
            denom = jnp.sum(p_c, axis=-1, keepdims=True) + jnp.sum(p_n, axis=-1, keepdims=True) + jnp.exp(sink - m)
            o = jnp.dot(p_c.astype(BF16), vcb[:, lo:hi], preferred_element_type=F32)
            o = o + jnp.dot(p_n.astype(BF16), vnb[:, lo:hi], preferred_element_type=F32)
            o = (o / denom).astype(BF16)
            for h in range(GROUP):
                hh = g * GROUP + h
                o_ref[r0:r0 + s_len, hh * HEAD_DIM:(hh + 1) * HEAD_DIM] = o[h * s_len:(h + 1) * s_len, :]


def _attn_sample(q_all, k_all, v_all, cache_k, cache_v, rel_bias, sinks, row0, dec_batch, s_len):
    w = cache_k.shape[1]
    seqs = _pick_tile(dec_batch, (16, 8, 4, 2, 1))
    rows = seqs * s_len
    blk0 = row0 // rows
    assert row0 % rows == 0
    qi = jnp.arange(s_len)
    dist_c = qi[:, None] + w - jnp.arange(w)[None, :]
    dist_n = qi[:, None] - jnp.arange(s_len)[None, :]
    b_c = _bias_table(rel_bias, dist_c).reshape(N_KV_HEADS, GROUP * s_len, w)
    b_n = _bias_table(rel_bias, dist_n).reshape(N_KV_HEADS, GROUP * s_len, s_len)
    sink = jnp.broadcast_to(sinks.astype(F32).reshape(N_KV_HEADS, GROUP, 1, 1), (N_KV_HEADS, GROUP, s_len, 1))
    sink = sink.reshape(N_KV_HEADS, GROUP * s_len, 1)
    tok = lambda width: pl.BlockSpec((rows, width), lambda i: (blk0 + i, 0))
    cache = pl.BlockSpec((seqs, w, D_KV), lambda i: (i, 0, 0))
    return pl.pallas_call(
        functools.partial(_attn_sample_kernel, seqs=seqs, s_len=s_len),
        grid=(dec_batch // seqs,),
        in_specs=[
            tok(D_ATTN), tok(D_KV), tok(D_KV), cache, cache,
            _const_spec(b_c.shape), _const_spec(b_n.shape), _const_spec(sink.shape),
        ],
        out_specs=(pl.BlockSpec((rows, D_ATTN), lambda i: (i, 0)), cache, cache),
        out_shape=(
            jax.ShapeDtypeStruct((dec_batch * s_len, D_ATTN), BF16),
            jax.ShapeDtypeStruct((dec_batch, w, D_KV), F32),
            jax.ShapeDtypeStruct((dec_batch, w, D_KV), F32),
        ),
        compiler_params=_params(("parallel",)),
        name="attn_sample",
    )(q_all, k_all, v_all, cache_k, cache_v, b_c, b_n, sink)


def _softplus(z):
    return jnp.maximum(z, 0.0) + jnp.log1p(jnp.exp(-jnp.abs(z)))


def _block_gate(xcb, w_ref, b_ref):
    parts = [jnp.dot(xcb[:, n * RNN_BLOCK:(n + 1) * RNN_BLOCK], w_ref[n], preferred_element_type=F32)
             for n in range(N_RNN_BLOCKS)]
    return jax.nn.sigmoid(jnp.concatenate(parts, axis=-1) + b_ref[...])


def _lru_coeffs(xc, wa_ref, ba_ref, wx_ref, bx_ref, lam_ref, first_row_unnormalised):
    xcb = xc.astype(BF16)
    r = _block_gate(xcb, wa_ref, ba_ref)
    i = _block_gate(xcb, wx_ref, bx_ref)
    log_a = -RG_C * r * _softplus(-lam_ref[...])
    a = jnp.exp(log_a)
    mult = jnp.sqrt(-jnp.tanh(log_a) * (a * a + 1.0))
    if first_row_unnormalised is not None:
        mult = jnp.where(first_row_unnormalised, 1.0, mult)
    return a, mult * i * xc


def _scan8(a, b):
    shape = a.shape
    grouped = (shape[0] // SUBLANES, SUBLANES, shape[1])
    a = a.reshape(grouped)
    b = b.reshape(grouped)
    r8 = lax.broadcasted_iota(jnp.int32, grouped, 1)
    d = 1
    while d < SUBLANES:
        keep = r8 >= d
        a_sh = jnp.where(keep, pltpu.roll(a, d, 1), 1.0)
        b_sh = jnp.where(keep, pltpu.roll(b, d, 1), 0.0)
        b = a * b_sh + b
        a = a * a_sh
        d *= 2
    return a.reshape(shape), b.reshape(shape)


def _rnn_prompt_kernel(xr_ref, gy_ref, cw_ref, cb_ref, wa_ref, ba_ref, wx_ref, bx_ref, lam_ref,
                       o_ref, nh_ref, ext_ref, a_ref, b_ref, hc_ref, *, tl):
    l = pl.program_id(1)

    @pl.when(l == 0)
    def _():
        ext_ref[0:SUBLANES, :] = jnp.zeros((SUBLANES, D_RNN), F32)
        hc_ref[...] = jnp.zeros((1, D_RNN), F32)

    x = xr_ref[...]
    ext_ref[SUBLANES:, :] = x
    xc = cb_ref[...] + cw_ref[CONV_W - 1:CONV_W, :] * x
    for j in range(1, CONV_W):
        xc = xc + cw_ref[CONV_W - 1 - j:CONV_W - j, :] * ext_ref[SUBLANES - j:SUBLANES - j + tl, :]
    ext_ref[0:SUBLANES, :] = x[tl - SUBLANES:, :]

    row = lax.broadcasted_iota(jnp.int32, (tl, D_RNN), 0)
    a, b = _lru_coeffs(xc, wa_ref, ba_ref, wx_ref, bx_ref, lam_ref, (row == 0) & (l == 0))
    a, b = _scan8(a, b)
    a_ref[...] = a
    b_ref[...] = b

    def chunk(c, h):
        sl = pl.ds(pl.multiple_of(c * SUBLANES, SUBLANES), SUBLANES)
        hc = b_ref[sl, :] + a_ref[sl, :] * h
        b_ref[sl, :] = hc
        return hc[SUBLANES - 1:SUBLANES, :]

    h = lax.fori_loop(0, tl // SUBLANES, chunk, hc_ref[...])
    hc_ref[...] = h
    nh_ref[0] = h
    o_ref[...] = (b_ref[...] * gy_ref[...]).astype(BF16)


def _rnn_prompt(xr_all, gy_all, rnn_w, batch, seq):
    tl =_pick_tile(seq, (256, 128, 64, 32, 16, 8))
    nl = seq // tl
    tok = pl.BlockSpec((tl, D_RNN), lambda b, l: (b * nl + l, 0))
    return pl.pallas_call(
        functools.partial(_rnn_prompt_kernel, tl=tl),
        grid=(batch, nl),
        in_specs=[tok, tok] + [_const_spec(w.shape) for w in rnn_w],
        out_specs=(tok, pl.BlockSpec((1, 1, D_RNN), lambda b, l: (b, 0, 0))),
        out_shape=(jax.ShapeDtypeStruct((batch * seq, D_RNN), BF16), jax.ShapeDtypeStruct((batch, 1, D_RNN), F32)),
        scratch_shapes=[
            pltpu.VMEM((tl + SUBLANES, D_RNN), F32),
            pltpu.VMEM((tl, D_RNN), F32),
            pltpu.VMEM((tl, D_RNN), F32),
            pltpu.VMEM((1, D_RNN), F32),
        ],
        compiler_params=_params(("parallel", "arbitrary")),
        name="rnn_prompt",
    )(xr_all, gy_all, *rnn_w)


def _rnn_sample_kernel(xr_ref, gy_ref, hp_ref, h0_ref, cw_ref, cb_ref, wa_ref, ba_ref, wx_ref, bx_ref, lam_ref,
                       o_ref, nh_ref, *, seqs):
    rows = seqs * SUBLANES
    x = xr_ref[...]
    hp = hp_ref[...]
    r8 = lax.broadcasted_iota(jnp.int32, (rows, D_RNN), 0) % SUBLANES
    xc = cb_ref[...] + cw_ref[CONV_W - 1:CONV_W, :] * x
    for j in range(1, CONV_W):
        shifted = jnp.where(r8 >= j, pltpu.roll(x, j, 0), pltpu.roll(hp, rows - (SUBLANES - j), 0))
        xc = xc + cw_ref[CONV_W - 1 - j:CONV_W - j, :] * shifted
    a, b = _lru_coeffs(xc, wa_ref, ba_ref, wx_ref, bx_ref, lam_ref, None)
    a, b = _scan8(a, b)
    h0 = jnp.broadcast_to(h0_ref[...][:, None, :], (seqs, SUBLANES, D_RNN)).reshape(rows, D_RNN)
    h = b + a * h0
    last = jnp.where(r8 == SUBLANES - 1, h, 0.0).reshape(seqs, SUBLANES, D_RNN)
    nh_ref[...] = jnp.sum(last, axis=1)
    o_ref[...] = (h * gy_ref[...]).astype(BF16)


def _rnn_sample(xr_all, gy_all, hist_pad, h0, rnn_w, row0, dec_batch):
    seqs = _pick_tile(dec_batch, (16, 8))
    rows = seqs * SUBLANES
    assert row0 % rows == 0
    blk0 = row0 // rows
    tok = pl.BlockSpec((rows, D_RNN), lambda i: (blk0 + i, 0))
    return pl.pallas_call(
        functools.partial(_rnn_sample_kernel, seqs=seqs),
        grid=(dec_batch // seqs,),
        in_specs=[tok, tok, pl.BlockSpec((rows, D_RNN), lambda i: (i, 0)), pl.BlockSpec((seqs, D_RNN), lambda i: (i, 0))]
        + [_const_spec(w.shape) for w in rnn_w],
        out_specs=(pl.BlockSpec((rows, D_RNN), lambda i: (i, 0)), pl.BlockSpec((seqs, D_RNN), lambda i: (i, 0))),
        out_shape=(jax.ShapeDtypeStruct((dec_batch * SUBLANES, D_RNN), BF16),
                   jax.ShapeDtypeStruct((dec_batch, D_RNN), F32)),
        compiler_params=_params(("parallel",)),
        name="rnn_sample",
    )(xr_all, gy_all, hist_pad, h0, *rnn_w)


def _layer_norm(z, g, b):
    mu = jnp.mean(z, axis=-1, keepdims=True)
    zc = z - mu
    var = jnp.mean(zc * zc, axis=-1, keepdims=True)
    return zc * lax.rsqrt(var + LN_EPS) * g + b


def _first_index_of_max(vals, iota, axis, sentinel):
    mx = jnp.max(vals, axis=axis, keepdims=True)
    return mx, jnp.min(jnp.where(vals == mx, iota, sentinel), axis=axis, keepdims=True)


def _route(scores, bias):
    t = scores.shape[1]
    grp = scores + bias
    g3 = grp.reshape(N_GROUPS, GROUP_SIZE, t)
    e_in_g = lax.broadcasted_iota(jnp.int32, g3.shape, 1)
    m1, first = _first_index_of_max(g3, e_in_g, 1, GROUP_SIZE)
    m2 = jnp.max(jnp.where(e_in_g == first, -jnp.inf, g3), axis=1, keepdims=True)
    gscore = (m1 + m2).reshape(N_GROUPS, t)
    g_iota = lax.broadcasted_iota(jnp.int32, gscore.shape, 0)
    gmask = jnp.zeros(gscore.shape, jnp.bool_)
    for _ in range(TOPK_GROUPS):
        _, gi = _first_index_of_max(gscore, g_iota, 0, N_GROUPS)
        hit = g_iota == gi
        gmask = gmask | hit
        gscore = jnp.where(hit, -jnp.inf, gscore)
    masked = jnp.where(gmask[:, None, :], g3, -jnp.inf).reshape(N_EXPERTS, t)
    e_iota = lax.broadcasted_iota(jnp.int32, masked.shape, 0)
    idx, wts, hits = [], [], []
    for _ in range(TOP_K):
        _, ei = _first_index_of_max(masked, e_iota, 0, N_EXPERTS)
        hit = e_iota == ei
        idx.append(ei)
        hits.append(hit)
        wts.append(jnp.sum(jnp.where(hit, scores, 0.0), axis=0, keepdims=True))
        masked = jnp.where(hit, -jnp.inf, masked)
    idx = jnp.concatenate(idx, axis=0)
    w = jnp.concatenate(wts, axis=0)
    w = w / jnp.sum(w, axis=0, keepdims=True) * ROUTED_SCALE
    return idx, w, hits


def _merge_kernel(xp_ref, xs_ref, aop_ref, aos_ref, rop_ref, ros_ref, sga_ref, sgr_ref, woa_ref, wor_ref, wout_ref, g1_ref, b1_ref,
                  wrt_ref, rb_ref, wsg_ref, wsu_ref, wsd_ref,
                  x1r_ref, base_ref, idx_ref, wt_ref, rank_ref, cnt_ref, carry_ref, *, n_p, tm):
    i = pl.program_id(0)

    @pl.when(i == 0)
    def _():
        carry_ref[...] = jnp.zeros(carry_ref.shape, F32)

    is_prompt = i < n_p
    x = jnp.where(is_prompt, xp_ref[...], xs_ref[...])
    pa = jnp.dot(jnp.where(is_prompt, aop_ref[...], aos_ref[...]), woa_ref[...], preferred_element_type=F32)
    pr = jnp.dot(jnp.where(is_prompt, rop_ref[...], ros_ref[...]), wor_ref[...], preferred_element_type=F32)
    merged = sga_ref[...] * pa + sgr_ref[...] * pr
    z = ALPHA * x + jnp.dot(merged.astype(BF16), wout_ref[...], preferred_element_type=F32)
    x1 = _layer_norm(z, g1_ref[...], b1_ref[...])
    _store_token_rows(x1r_ref, x1, tm)
    x1b = x1.astype(BF16)
    u = jax.nn.silu(jnp.dot(x1b, wsg_ref[...], preferred_element_type=F32)) * jnp.dot(x1b, wsu_ref[...], preferred_element_type=F32)
    shared = jnp.dot(u.astype(BF16), wsd_ref[...], preferred_element_type=F32)
    base_ref[...] = ALPHA * x1 + shared
    logits = lax.dot_general(wrt_ref[...], x1, (((1,), (1,)), ((), ())), preferred_element_type=F32)
    idx, w, hits = _route(jax.nn.sigmoid(logits), rb_ref[...])
    idx_ref[...] = idx
    wt_ref[...] = w

    chosen = functools.reduce(jnp.logical_or, hits)
    chosen_f = jnp.where(chosen, 1.0, 0.0)
    earlier = (lax.broadcasted_iota(jnp.int32, (tm, tm), 0) < lax.broadcasted_iota(jnp.int32, (tm, tm), 1))
    prefix = jnp.dot(chosen_f.astype(BF16), jnp.where(earlier, 1.0, 0.0).astype(BF16), preferred_element_type=F32)
    before = prefix + carry_ref[...]
    ranks = [jnp.sum(jnp.where(hit, before, 0.0), axis=0, keepdims=True) for hit in hits]
    rank_ref[...] = jnp.concatenate(ranks, axis=0).astype(jnp.int32)
    carry_ref[...] = carry_ref[...] + jnp.sum(chosen_f, axis=1, keepdims=True)
    cnt_ref[...] = carry_ref[...]


def _merge(x_p, x_s, attn_p, attn_s, rnn_p, rnn_s, sga, sgr, weights):
    t_p, t_s = x_p.shape[0], x_s.shape[0]
    t = t_p + t_s
    tm = _pick_tile(math.gcd(t_p, t_s), (256, 128))
    row = lambda width: pl.BlockSpec((tm, width), lambda i: (i, 0))
    col = pl.BlockSpec((TOP_K, tm), lambda i: (0, i))

    return pl.pallas_call(
        functools.partial(_merge_kernel, n_p=t_p // tm, tm=tm),
        grid=(t // tm,),
        in_specs=[*_two_source_specs(t_p, t_s, tm, D_MODEL), *_two_source_specs(t_p, t_s, tm, D_ATTN),
                  *_two_source_specs(t_p, t_s, tm, D_RNN), row(D_MODEL), row(D_MODEL)]
        + [_const_spec(w.shape) for w in weights],
        out_specs=(pl.BlockSpec((tm * ROW_TILE, LANES), lambda i: (i, 0)), row(D_MODEL), col, col, col,
                   _const_spec((N_EXPERTS, 1))),
        out_shape=(
            jax.ShapeDtypeStruct((t * ROW_TILE, LANES), WORD),
            jax.ShapeDtypeStruct((t, D_MODEL), F32),
            jax.ShapeDtypeStruct((TOP_K, t), jnp.int32),
            jax.ShapeDtypeStruct((TOP_K, t), F32),
            jax.ShapeDtypeStruct((TOP_K, t), jnp.int32),
            jax.ShapeDtypeStruct((N_EXPERTS, 1), F32),
        ),
        scratch_shapes=[pltpu.VMEM((N_EXPERTS, 1), F32)],
        compiler_params=_params(("arbitrary",)),
        name="merge_ln1_route",
    )(x_p, x_s, attn_p, attn_s, rnn_p, rnn_s, sga, sgr, *weights)


def _expert_layout(counts, n_assign):
    counts = counts.reshape(N_EXPERTS).astype(jnp.int32)
    padded = (counts + ROW_PAD - 1) // ROW_PAD * ROW_PAD
    pend = jnp.cumsum(padded)
    pstart = pend - padded
    rows_alloc = (n_assign + N_EXPERTS * (ROW_PAD - 1)) // ROW_PAD * ROW_PAD + EXPERT_CHUNK
    n_chunks = jnp.maximum((counts + EXPERT_CHUNK - 1) // EXPERT_CHUNK, 1)
    pad_group = jnp.where(counts % ROW_PAD != 0, pstart + counts // ROW_PAD * ROW_PAD, -1)
    zero_meta = jnp.concatenate([pad_group, pend[-1:]]).astype(jnp.int32)
    ffn_end = jnp.max(pstart + n_chunks * EXPERT_CHUNK).reshape(1)
    return pstart, zero_meta, n_chunks, counts, ffn_end, rows_alloc


def _dest_kernel(idx_ref, rank_ref, pstart_ref, dest_ref, word_rows_ref):
    e_iota = lax.broadcasted_iota(jnp.int32, (N_EXPERTS, idx_ref.shape[1]), 0)
    starts = [jnp.sum(jnp.where(e_iota == idx_ref[k:k + 1, :], pstart_ref[...], 0), axis=0, keepdims=True)
              for k in range(TOP_K)]
    dest = jnp.concatenate(starts, axis=0) + rank_ref[...]
    dest_ref[...] = dest
    word_rows_ref[0] = jnp.concatenate(
        [dest[k:k + 1, :] * ROW_TILE + s for k in range(TOP_K) for s in range(ROW_TILE)], axis=0)


def _dest_rows(idx_t, rank_t, pstart, tm):
    t = idx_t.shape[1]
    col = pl.BlockSpec((TOP_K, tm), lambda i: (0, i))
    return pl.pallas_call(
        _dest_kernel,
        grid=(t // tm,),
        in_specs=[col, col, _const_spec((N_EXPERTS, 1))],
        out_specs=(col, pl.BlockSpec((1, TOP_K * ROW_TILE, tm), lambda i: (i, 0, 0))),
        out_shape=(jax.ShapeDtypeStruct((TOP_K, t), jnp.int32),
                   jax.ShapeDtypeStruct((t // tm, TOP_K * ROW_TILE, tm), jnp.int32)),
        compiler_params=_params(("parallel",)),
        name="dest_rows",
    )(idx_t, rank_t, pstart.reshape(N_EXPERTS, 1))


def _token_rows(r, n=1):
    return pl.ds(pl.multiple_of(r * ROW_TILE, ROW_TILE), n * ROW_TILE)


def _zero_row_groups(zero_ref, dst_ref, sem, first_group, n_groups):
    def start(g, c):
        pltpu.make_async_copy(zero_ref, dst_ref.at[_token_rows(g * ROW_PAD, ROW_PAD)], sem).start()
        return c

    lax.fori_loop(first_group, n_groups, start, 0)
    return n_groups - first_group


def _wait_zero_copies(zero_ref, dst_ref, sem, n):
    def wait(_, c):
        pltpu.make_async_copy(zero_ref, dst_ref.at[_token_rows(0, ROW_PAD)], sem).wait()
        return c

    lax.fori_loop(0, n, wait, 0)


def _dispatch_kernel(zero_meta_ref, dest_ref, x_ref, xs_ref, zero_ref, sem, *, tm, rows_alloc):
    @pl.when(pl.program_id(0) == 0)
    def _():
        zero_ref[...] = jnp.zeros(zero_ref.shape, WORD)

        def pad_group(e, n):
            row = zero_meta_ref[e]

            @pl.when(row >= 0)
            def _():
                pltpu.make_async_copy(zero_ref, xs_ref.at[_token_rows(row, ROW_PAD)], sem).start()

            return n + jnp.where(row >= 0, 1, 0)

        n = lax.fori_loop(0, N_EXPERTS, pad_group, 0)
        n = n + _zero_row_groups(zero_ref, xs_ref, sem, zero_meta_ref[N_EXPERTS] // ROW_PAD, rows_alloc // ROW_PAD)
        _wait_zero_copies(zero_ref, xs_ref, sem, n)

    def issue(t, c):
        for k in range(TOP_K):
            pltpu.make_async_copy(x_ref.at[_token_rows(t)], xs_ref.at[_token_rows(dest_ref[k, t])], sem).start()
        return c

    lax.fori_loop(0, tm, issue, 0)
    for _ in range(TOP_K):
        pltpu.make_async_copy(x_ref, xs_ref.at[_token_rows(0, tm)], sem).wait()


def _dispatch(x1r, dest, zero_meta, rows_alloc):
    t = dest.shape[1]
    tm = _pick_tile(t, (1024, 512, 256, 128))
    grid_spec = pltpu.PrefetchScalarGridSpec(
        num_scalar_prefetch=1,
        grid=(t // tm,),
        in_specs=[
            pl.BlockSpec((TOP_K, tm), lambda i, ps: (0, i), memory_space=pltpu.SMEM),
            pl.BlockSpec((tm * ROW_TILE, LANES), lambda i, ps: (i, 0)),
        ],
        out_specs=pl.BlockSpec(memory_space=pl.ANY),
        scratch_shapes=[pltpu.VMEM((ROW_PAD * ROW_TILE, LANES), WORD), pltpu.SemaphoreType.DMA],
    )
    return pl.pallas_call(
        functools.partial(_dispatch_kernel, tm=tm, rows_alloc=rows_alloc),
        grid_spec=grid_spec,
        out_shape=jax.ShapeDtypeStruct((rows_alloc * ROW_TILE, LANES), WORD),
        compiler_params=_params(("arbitrary",)),
        name="dispatch",
    )(zero_meta, dest, x1r)


def _expert_kernel(pstart_ref, nch_ref, rows_ref, end_ref, wg_ref, wu_ref, wd_ref, xs_ref, o_ref,
                   xbuf_ref, obuf_ref, wgb_ref, wub_ref, wdb_ref, zero_ref, done_ref, in_sem, out_sem, *, rows_alloc):
    e = pl.program_id(0)
    n_e = pl.num_programs(0)
    start = pstart_ref[e]
    nch = nch_ref[e]
    buf_rows = EXPERT_CHUNK * ROW_TILE

    def in_copy(row, slot):
        return pltpu.make_async_copy(xs_ref.at[_token_rows(row, EXPERT_CHUNK)],
                                     xbuf_ref.at[pl.ds(slot * buf_rows, buf_rows)], in_sem.at[slot])

    def out_copy(row, slot):
        return pltpu.make_async_copy(obuf_ref.at[pl.ds(slot * buf_rows, buf_rows)],
                                     o_ref.at[_token_rows(row, EXPERT_CHUNK)], out_sem)

    @pl.when(e == 0)
    def _():
        done_ref[0] = 0
        obuf_ref[...] = jnp.zeros(obuf_ref.shape, WORD)
        in_copy(start, 0).start()

    wgb_ref[...] = wg_ref[...].astype(BF16)
    wub_ref[...] = wu_ref[...].astype(BF16)
    wdb_ref[...] = wd_ref[...].astype(BF16)
    done = done_ref[0]

    def chunk(c, carry):
        g = done + c
        slot = g % 2
        row = start + c * EXPERT_CHUNK
        in_copy(row, slot).wait()
        last = c + 1 == nch
        next_row = jnp.where(last, pstart_ref[jnp.minimum(e + 1, n_e - 1)], row + EXPERT_CHUNK)

        @pl.when(jnp.logical_not(last & (e == n_e - 1)))
        def _():
            in_copy(next_row, 1 - slot).start()

        base = pl.multiple_of(slot * buf_rows, buf_rows)
        for h in range(EXPERT_CHUNK // EXPERT_ROWS):
            @pl.when(c * EXPERT_CHUNK + h * EXPERT_ROWS < jnp.maximum(rows_ref[e], 1))
            def _():
                xb = jnp.concatenate(_load_token_rows(xbuf_ref, EXPERT_ROWS, row0=h * EXPERT_ROWS, base=base),
                                     axis=1).astype(BF16)
                gate = jnp.dot(xb, wgb_ref[...], preferred_element_type=F32)
                up = jnp.dot(xb, wub_ref[...], preferred_element_type=F32)
                act = (jax.nn.silu(gate) * up).astype(BF16)
                _store_token_rows(obuf_ref, jnp.dot(act, wdb_ref[...], preferred_element_type=F32),
                                  EXPERT_ROWS, row0=h * EXPERT_ROWS, base=base)

        @pl.when(g > 0)
        def _():
            out_copy(0, 0).wait()

        out_copy(row, slot).start()
        return carry

    lax.fori_loop(0, nch, chunk, 0)
    done_ref[0] = done + nch

    @pl.when(e == n_e - 1)
    def _():
        out_copy(0, 0).wait()
        zero_ref[...] = jnp.zeros(zero_ref.shape, WORD)
        n = _zero_row_groups(zero_ref, o_ref, out_sem, end_ref[0] // ROW_PAD, rows_alloc // ROW_PAD)
        _wait_zero_copies(zero_ref, o_ref, out_sem, n)


def _expert_ffn(xs, pstart, n_chunks, rows, ffn_end, rows_alloc, w_e_gate, w_e_up, w_e_down):
    weight = lambda shape: pl.BlockSpec((None, *shape), lambda e, *_: (e, 0, 0))
    grid_spec = pltpu.PrefetchScalarGridSpec(
        num_scalar_prefetch=4,
        grid=(N_EXPERTS,),
        in_specs=[weight((D_MODEL, D_EXPERT)), weight((D_MODEL, D_EXPERT)), weight((D_EXPERT, D_MODEL)),
                  pl.BlockSpec(memory_space=pl.ANY)],
        out_specs=pl.BlockSpec(memory_space=pl.ANY),
        scratch_shapes=[
            pltpu.VMEM((2 * EXPERT_CHUNK * ROW_TILE, LANES), WORD),
            pltpu.VMEM((2 * EXPERT_CHUNK * ROW_TILE, LANES), WORD),
            pltpu.VMEM((D_MODEL, D_EXPERT), BF16), pltpu.VMEM((D_MODEL, D_EXPERT), BF16),
            pltpu.VMEM((D_EXPERT, D_MODEL), BF16),
            pltpu.VMEM((ROW_PAD * ROW_TILE, LANES), WORD),
            pltpu.SMEM((1,), jnp.int32),
            pltpu.SemaphoreType.DMA((2,)), pltpu.SemaphoreType.DMA,
        ],
    )
    return pl.pallas_call(
        functools.partial(_expert_kernel, rows_alloc=rows_alloc),
        grid_spec=grid_spec,
        out_shape=jax.ShapeDtypeStruct((rows_alloc * ROW_TILE, LANES), WORD),
        compiler_params=_params(("arbitrary",)),
        name="expert_ffn",
    )(pstart, n_chunks, rows, ffn_end, w_e_gate, w_e_up, w_e_down, xs)


def _combine_head_kernel(dest_ref, dest_next_ref, w_ref, base_ref, g_ref, b_ref, outs_ref, y_ref, buf_ref, sem, *, tm):
    i = pl.program_id(0)
    slot_rows = TOP_K * tm

    def gather(d_ref, slot):
        def issue(t, c):
            for k in range(TOP_K):
                pltpu.make_async_copy(outs_ref.at[_token_rows(d_ref[k, t])],
                                      buf_ref.at[_token_rows(slot * slot_rows + k * tm + t)], sem.at[slot]).start()
            return c

        lax.fori_loop(0, tm, issue, 0)

    @pl.when(i == 0)
    def _():
        gather(dest_ref, 0)

    @pl.when(i + 1 < pl.num_programs(0))
    def _():
        gather(dest_next_ref, (i + 1) % 2)

    slot = i % 2
    for k in range(TOP_K):
        pltpu.make_async_copy(outs_ref.at[_token_rows(0, tm)], buf_ref.at[_token_rows(slot * slot_rows + k * tm, tm)],
                              sem.at[slot]).wait()

    w = w_ref[...]
    base = pl.multiple_of(slot * slot_rows * ROW_TILE, ROW_TILE)
    halves = [None, None]
    for k in range(TOP_K):
        wk = w[:, k:k + 1]
        for j, rows in enumerate(_load_token_rows(buf_ref, tm, row0=k * tm, base=base)):
            halves[j] = wk * rows if halves[j] is None else halves[j] + wk * rows
    y_ref[...] = _layer_norm(base_ref[...] + jnp.concatenate(halves, axis=1), g_ref[...], b_ref[...])


def _combine_head(out_sorted, dest, w_tok, base, g2, b2, tm, n_head):
    row = pl.BlockSpec((tm, D_MODEL), lambda i: (i, 0))
    return pl.pallas_call(
        functools.partial(_combine_head_kernel, tm=tm),
        grid=(n_head,),
        in_specs=[
            pl.BlockSpec((TOP_K, tm), lambda i: (0, i), memory_space=pltpu.SMEM),
            pl.BlockSpec((TOP_K, tm), lambda i: (0, jnp.minimum(i + 1, n_head - 1)), memory_space=pltpu.SMEM),
            pl.BlockSpec((tm, TOP_K), lambda i: (i, 0)),
            row, _const_spec((1, D_MODEL)), _const_spec((1, D_MODEL)),
            pl.BlockSpec(memory_space=pl.ANY),
        ],
        out_specs=row,
        out_shape=jax.ShapeDtypeStruct((n_head * tm, D_MODEL), F32),
        scratch_shapes=[pltpu.VMEM((2 * TOP_K * tm * ROW_TILE, LANES), WORD), pltpu.SemaphoreType.DMA((2,))],
        compiler_params=_params(("arbitrary",)),
        name="combine_head",
    )(dest, dest, w_tok, base, g2, b2, out_sorted)


SC_GATHER_ROWS = 128


def _sc_gather_rows(table, rows):
    info = plsc.get_sparse_core_info()
    n_workers = info.num_cores * info.num_subcores
    m = rows.shape[0]
    per_worker = m // n_workers
    assert m % n_workers == 0 and per_worker % SC_GATHER_ROWS == 0
    in_flight = _pick_tile(per_worker // SC_GATHER_ROWS, (4, 2, 1))
    step = SC_GATHER_ROWS * in_flight
    mesh = plsc.VectorSubcoreMesh(core_axis_name="c", subcore_axis_name="s")

    @functools.partial(
        pl.kernel, mesh=mesh,
        out_type=jax.ShapeDtypeStruct((m, LANES), table.dtype),
        scratch_types=[pltpu.VMEM((step,), jnp.int32), pltpu.VMEM((step, LANES), table.dtype), pltpu.SemaphoreType.DMA],
    )
    def gather(table_hbm, rows_hbm, out_hbm, rows_v, data_v, sem):
        worker = lax.axis_index("s") * info.num_cores + lax.axis_index("c")

        @pl.loop(0, per_worker // step)
        def _(it):
            off = worker * per_worker + it * step
            pltpu.sync_copy(rows_hbm.at[pl.ds(off, step)], rows_v)
            copies = [pltpu.async_copy(table_hbm.at[rows_v.at[pl.ds(j * SC_GATHER_ROWS, SC_GATHER_ROWS)]],
                                       data_v.at[pl.ds(j * SC_GATHER_ROWS, SC_GATHER_ROWS)], sem)
                      for j in range(in_flight)]
            for cp in copies:
                cp.wait()
            pltpu.sync_copy(data_v, out_hbm.at[pl.ds(off, step)])

    return gather(table, rows)


def _combine_tail_kernel(yh_ref, rows_ref, w_ref, base_ref, g_ref, b_ref, yp_ref, ys_ref, *, tm, n_head, n_p):
    i = pl.program_id(0)

    @pl.when(i < n_head)
    def _():
        yp_ref[...] = yh_ref[...]

    @pl.when(i >= n_head)
    def _():
        w = w_ref[...]
        lo = [None] * ROW_TILE
        hi = [None] * ROW_TILE
        for k in range(TOP_K):
            wk = w[:, k:k + 1]
            for s in range(ROW_TILE):
                r0 = (k * ROW_TILE + s) * tm
                lo_s, hi_s = _unpack_words(rows_ref[r0:r0 + tm, :])
                lo[s] = wk * lo_s if lo[s] is None else lo[s] + wk * lo_s
                hi[s] = wk * hi_s if hi[s] is None else hi[s] + wk * hi_s
        y = _layer_norm(base_ref[...] + jnp.concatenate(lo + hi, axis=1), g_ref[...], b_ref[...])

        @pl.when(i < n_p)
        def _():
            yp_ref[...] = y

        @pl.when(i >= n_p)
        def _():
            ys_ref[...] = y


def _combine_tail(y_head, gathered, w_tok, base, g2, b2, t_p, t_s, tm, n_head):
    n_p = t_p // tm
    assert n_head <= n_p
    n_tiles = (t_p + t_s) // tm
    blk = TOP_K * ROW_TILE * tm
    row = pl.BlockSpec((tm, D_MODEL), lambda i: (i, 0))
    out_p, out_s = _two_source_specs(t_p, t_s, tm, D_MODEL)
    return pl.pallas_call(
        functools.partial(_combine_tail_kernel, tm=tm, n_head=n_head, n_p=n_p),
        grid=(n_tiles,),
        in_specs=[
            pl.BlockSpec((tm, D_MODEL), lambda i: (jnp.minimum(i, n_head - 1), 0)),
            pl.BlockSpec((blk, LANES), lambda i: (jnp.maximum(i - n_head, 0), 0)),
            pl.BlockSpec((tm, TOP_K), lambda i: (i, 0)),
            row, _const_spec((1, D_MODEL)), _const_spec((1, D_MODEL)),
        ],
        out_specs=(out_p, out_s),
        out_shape=(jax.ShapeDtypeStruct((t_p, D_MODEL), F32), jax.ShapeDtypeStruct((t_s, D_MODEL), F32)),
        compiler_params=_params(("arbitrary",)),
        name="combine_tail",
    )(y_head, gathered, w_tok, base, g2, b2)


def kernel(x_prompt, x_sample, cache_k, cache_v, state_conv, state_rnn, w_in, conv_w, conv_b, w_gate_a, b_gate_a, w_gate_x, b_gate_x, lru_lambda, rel_bias, sinks, w_o_attn, w_o_rnn, w_out, ln1_g, ln1_b, w_router, router_bias, w_e_gate, w_e_up, w_e_down, w_s_gate, w_s_up, w_s_down, ln2_g, ln2_b):
    assert w_in.shape[0] == DEPTH == 1
    batch, seq, _ = x_prompt.shape
    dec_batch, s_len, _ = x_sample.shape
    w_cache = cache_k.shape[2]
    assert s_len == SUBLANES and seq % WINDOW == 0 and w_cache == WINDOW
    t_p = batch * seq
    t_s = dec_batch * s_len
    vec = lambda a: a[0].reshape(1, -1).astype(F32)

    x_p = x_prompt.reshape(t_p, D_MODEL)
    x_s = x_sample.reshape(t_s, D_MODEL)
    q, k, v, xr, gy, sga, sgr = _inproj(x_p, x_s, w_in[0].astype(BF16))

    qi = jnp.arange(WINDOW)
    dist = qi[:, None] + WINDOW - jnp.arange(2 * WINDOW)[None, :]
    bias_p = _bias_table(rel_bias, dist)
    attn_p = _attn_prompt(q, k, v, bias_p, sinks[0], batch, seq)
    attn_s, k_s, v_s = _attn_sample(
        q, k, v, cache_k[0].reshape(dec_batch, w_cache, D_KV), cache_v[0].reshape(dec_batch, w_cache, D_KV),
        rel_bias, sinks[0], t_p, dec_batch, s_len)

    rnn_w = (conv_w[0], vec(conv_b), w_gate_a[0].astype(BF16), vec(b_gate_a), w_gate_x[0].astype(BF16),
             vec(b_gate_x), vec(lru_lambda))
    rnn_p, h_p = _rnn_prompt(xr, gy, rnn_w, batch, seq)
    hist_pad = jnp.pad(state_conv[0], ((0, 0), (SUBLANES - (CONV_W - 1), 0), (0, 0))).reshape(t_s, D_RNN)
    rnn_s, h_s = _rnn_sample(xr, gy, hist_pad, state_rnn[0], rnn_w, t_p, dec_batch)

    merge_w = (w_o_attn[0].astype(BF16), w_o_rnn[0].astype(BF16), w_out[0].astype(BF16), vec(ln1_g), vec(ln1_b),
               w_router[0].T, router_bias[0].reshape(N_EXPERTS, 1), w_s_gate[0].astype(BF16),
               w_s_up[0].astype(BF16), w_s_down[0].astype(BF16))
    x1r, base, idx_t, wt_t, rank_t, counts = _merge(x_p, x_s, attn_p, attn_s, rnn_p, rnn_s, sga, sgr, merge_w)

    pstart, zero_meta, n_chunks, rows, ffn_end, rows_alloc = _expert_layout(counts, (t_p + t_s) * TOP_K)
    tm = _pick_tile(math.gcd(t_p, t_s), (256, 128))
    dest, word_rows = _dest_rows(idx_t, rank_t, pstart, tm)
    xs = _dispatch(x1r, dest, zero_meta, rows_alloc)
    out_sorted = _expert_ffn(xs, pstart, n_chunks, rows, ffn_end, rows_alloc, w_e_gate[0], w_e_up[0], w_e_down[0])
    n_head = min(t_p // tm, ((t_p + t_s) // tm + 1) // 2)
    w_tok = wt_t.T
    g2, b2 = vec(ln2_g), vec(ln2_b)
    y_head = _combine_head(out_sorted, dest, w_tok, base, g2, b2, tm, n_head)
    gathered = _sc_gather_rows(out_sorted, word_rows[n_head:].reshape(-1))
    y_p, y_s = _combine_tail(y_head, gathered, w_tok, base, g2, b2, t_p, t_s, tm, n_head)
    y_p = y_p.reshape(batch, seq, D_MODEL)
    y_s = y_s.reshape(dec_batch, s_len, D_MODEL)
    kv5 = lambda a, b: a.reshape(1, b, WINDOW, N_KV_HEADS, HEAD_DIM)
    tail = lambda a, n: jnp.stack([lax.slice_in_dim(a, (b + 1) * seq - n, (b + 1) * seq) for b in range(batch)])
    k_p = kv5(tail(k, WINDOW), batch)
    v_p = kv5(tail(v, WINDOW), batch)
    conv_p = tail(xr, CONV_W - 1)[None]
    conv_s = xr[t_p:].reshape(dec_batch, s_len, D_RNN)[:, s_len - (CONV_W - 1):][None]
    return (y_p, y_s, k_p, v_p, conv_p, h_p.reshape(1, batch, D_RNN),
            kv5(k_s, dec_batch), kv5(v_s, dec_batch), conv_s, h_s.reshape(1, dec_batch, D_RNN))
```

```python
import functools
import math

import jax
import jax.numpy as jnp
from jax import lax
from jax.experimental import pallas as pl
from jax.experimental.pallas import tpu as pltpu
from jax.experimental.pallas import tpu_sc as plsc

F32 = jnp.float32
BF16 = jnp.bfloat16
WORD = jnp.int32

D_MODEL = 1024
N_HEADS = 8
N_KV_HEADS = 2
HEAD_DIM = 64
GROUP = N_HEADS // N_KV_HEADS
WINDOW = 128
D_ATTN = N_HEADS * HEAD_DIM
D_KV = N_KV_HEADS * HEAD_DIM
N_BUCKETS = 32
MAX_DISTANCE = 128
D_RNN = D_MODEL
RNN_BLOCK = 256
N_RNN_BLOCKS = D_RNN // RNN_BLOCK
CONV_W = 4
RG_C = 8.0
N_EXPERTS = 256
TOP_K = 8
N_GROUPS = 8
GROUP_SIZE = N_EXPERTS // N_GROUPS
TOPK_GROUPS = 4
D_EXPERT = D_MODEL // 4
ROUTED_SCALE = 2.5
LN_EPS = 1e-5
DEPTH = 1
ALPHA = (2 * DEPTH) ** 0.25
NEG_INF = -1e30
SM_SCALE = HEAD_DIM ** -0.5

O_Q = 0
O_K = D_ATTN
O_V = O_K + D_KV
O_XR = O_V + D_KV
O_YR = O_XR + D_RNN
O_GA = O_YR + D_RNN
O_GR = O_GA + D_MODEL
D_IN = O_GR + D_MODEL

SUBLANES = 8
VMEM_LIMIT_BYTES = 56 * 1024 * 1024
EXPERT_ROWS = 256
EXPERT_CHUNK = 512
ROW_PAD = SUBLANES


def _params(sem):
    return pltpu.CompilerParams(dimension_semantics=sem, vmem_limit_bytes=VMEM_LIMIT_BYTES)


def _pick_tile(n, candidates):
    for c in candidates:
        if n % c == 0:
            return c
    raise ValueError(f"no tile for {n}")


def _const_spec(shape):
    nd = len(shape)
    return pl.BlockSpec(shape, lambda *_: (0,) * nd)


LANES = 128
ROW_WORDS = D_MODEL // 2
ROW_TILE = ROW_WORDS // LANES
HIGH_HALF = -65536


def _pack_words(mat):
    as_bits = lambda v: pltpu.bitcast(v.astype(BF16).astype(F32), WORD)
    return (as_bits(mat[:, ROW_WORDS:]) & HIGH_HALF) | lax.shift_right_logical(as_bits(mat[:, :ROW_WORDS]), 16)


def _unpack_words(words):
    return pltpu.bitcast(words << 16, F32), pltpu.bitcast(words & HIGH_HALF, F32)


def _store_token_rows(ref, mat, n, row0=0, base=0):
    words = _pack_words(mat)
    for s in range(ROW_TILE):
        ref[pl.ds(base + row0 * ROW_TILE + s, n, stride=ROW_TILE), :] = words[:, s * LANES:(s + 1) * LANES]


def _load_token_rows(ref, n, row0=0, base=0):
    words = jnp.concatenate(
        [ref[pl.ds(base + row0 * ROW_TILE + s, n, stride=ROW_TILE), :] for s in range(ROW_TILE)], axis=1)
    return _unpack_words(words)


def _two_source_specs(t_p, t_s, tm, width):
    n_p = t_p // tm
    assert t_p % tm == 0 and t_s % tm == 0
    return (pl.BlockSpec((tm, width), lambda i, *_: (jnp.minimum(i, n_p - 1), 0)),
            pl.BlockSpec((tm, width), lambda i, *_: (jnp.maximum(i - n_p, 0), 0)))


def _inproj_kernel(xp_ref, xs_ref, w_ref, q_ref, k_ref, v_ref, xr_ref, gy_ref, sga_ref, sgr_ref, *, n_p):
    x = jnp.where(pl.program_id(0) < n_p, xp_ref[...], xs_ref[...]).astype(BF16)

    def seg(lo, hi):
        return jnp.dot(x, w_ref[:, lo:hi], preferred_element_type=F32)

    q_ref[...] = seg(O_Q, O_K).astype(BF16)
    k_ref[...] = seg(O_K, O_V)
    v_ref[...] = seg(O_V, O_XR)
    xr_ref[...] = seg(O_XR, O_YR)
    gy_ref[...] = jax.nn.gelu(seg(O_YR, O_GA)).astype(BF16)
    sga_ref[...] = jax.nn.sigmoid(seg(O_GA, O_GR)).astype(BF16)
    sgr_ref[...] = jax.nn.sigmoid(seg(O_GR, D_IN)).astype(BF16)


def _inproj(x_p, x_s, w_in_bf16):
    t_p, t_s = x_p.shape[0], x_s.shape[0]
    t = t_p + t_s
    tm = _pick_tile(math.gcd(t_p, t_s), (256, 128, 64, 32, 16, 8))
    row = lambda width: pl.BlockSpec((tm, width), lambda i: (i, 0))
    out_shape = (
        jax.ShapeDtypeStruct((t, D_ATTN), BF16),
        jax.ShapeDtypeStruct((t, D_KV), F32),
        jax.ShapeDtypeStruct((t, D_KV), F32),
        jax.ShapeDtypeStruct((t, D_RNN), F32),
        jax.ShapeDtypeStruct((t, D_RNN), BF16),
        jax.ShapeDtypeStruct((t, D_MODEL), BF16),
        jax.ShapeDtypeStruct((t, D_MODEL), BF16),
    )
    return pl.pallas_call(
        functools.partial(_inproj_kernel, n_p=t_p // tm),
        grid=(t // tm,),
        in_specs=[*_two_source_specs(t_p, t_s, tm, D_MODEL), _const_spec((D_MODEL, D_IN))],
        out_specs=(row(D_ATTN), row(D_KV), row(D_KV), row(D_RNN), row(D_RNN), row(D_MODEL), row(D_MODEL)),
        out_shape=out_shape,
        compiler_params=_params(("parallel",)),
        name="inproj",
    )(x_p, x_s, w_in_bf16)


def _t5_bucket(dist):
    n = jnp.maximum(dist, 0)
    max_exact = N_BUCKETS // 2
    nf = jnp.maximum(n, 1).astype(F32)
    large = max_exact + (jnp.log(nf / max_exact) / math.log(MAX_DISTANCE / max_exact) * (N_BUCKETS - max_exact)).astype(jnp.int32)
    large = jnp.minimum(large, N_BUCKETS - 1)
    return jnp.where(n < max_exact, n, large)


def _bias_table(rel_bias, dist):
    bucket = _t5_bucket(dist)
    rb = rel_bias.astype(F32)
    out = jnp.zeros((N_HEADS, *dist.shape), F32)
    for j in range(N_BUCKETS):
        out = jnp.where(bucket[None] == j, rb[j][:, None, None], out)
    return out


def _softmax_pv(s, sink, v):
    m = jnp.maximum(jnp.max(s, axis=-1, keepdims=True), sink)
    p = jnp.exp(s - m)
    denom = jnp.sum(p, axis=-1, keepdims=True) + jnp.exp(sink - m)
    return jnp.dot(p.astype(BF16), v, preferred_element_type=F32), denom


def _attn_prompt_kernel(sink_ref, q_ref, kc_ref, kp_ref, vc_ref, vp_ref, bias_ref, o_ref):
    n = pl.program_id(1)
    kk = jnp.concatenate([kp_ref[...], kc_ref[...]], axis=0).astype(BF16)
    vv = jnp.concatenate([vp_ref[...], vc_ref[...]], axis=0).astype(BF16)
    q = q_ref[...]
    rows = lax.broadcasted_iota(jnp.int32, (WINDOW, 2 * WINDOW), 0)
    cols = lax.broadcasted_iota(jnp.int32, (WINDOW, 2 * WINDOW), 1)
    dist = rows + WINDOW - cols
    valid = (dist >= 0) & (dist <= WINDOW) & ((n > 0) | (cols >= WINDOW))
    for g in range(N_KV_HEADS):
        kg = kk[:, g * HEAD_DIM:(g + 1) * HEAD_DIM]
        vg = vv[:, g * HEAD_DIM:(g + 1) * HEAD_DIM]
        for h in range(GROUP):
            hh = g * GROUP + h
            qh = q[:, hh * HEAD_DIM:(hh + 1) * HEAD_DIM]
            s = lax.dot_general(qh, kg, (((1,), (1,)), ((), ())), preferred_element_type=F32) * SM_SCALE
            s = jnp.where(valid, s + bias_ref[hh], NEG_INF)
            o, denom = _softmax_pv(s, sink_ref[0, hh], vg)
            o_ref[:, hh * HEAD_DIM:(hh + 1) * HEAD_DIM] = (o / denom).astype(BF16)


def _attn_prompt(q_all, k_all, v_all, bias, sinks, batch, seq):
    nb = seq // WINDOW
    cur = lambda width: pl.BlockSpec((WINDOW, width), lambda b, n: (b * nb + n, 0))
    prev = lambda width: pl.BlockSpec((WINDOW, width), lambda b, n: (b * nb + jnp.maximum(n - 1, 0), 0))
    return pl.pallas_call(
        _attn_prompt_kernel,
        grid=(batch, nb),
        in_specs=[
            pl.BlockSpec(memory_space=pltpu.SMEM),
            cur(D_ATTN), cur(D_KV), prev(D_KV), cur(D_KV), prev(D_KV),
            _const_spec((N_HEADS, WINDOW, 2 * WINDOW)),
        ],
        out_specs=cur(D_ATTN),
        out_shape=jax.ShapeDtypeStruct((batch * seq, D_ATTN), BF16),
        compiler_params=_params(("parallel", "arbitrary")),
        name="attn_prompt",
    )(sinks.reshape(1, N_HEADS).astype(F32), q_all, k_all, k_all, v_all, v_all, bias)


def _attn_sample_kernel(q_ref, kn_ref, vn_ref, kc_ref, vc_ref, bc_ref, bn_ref, sink_ref,
                        o_ref, ko_ref, vo_ref, *, seqs, s_len):
    w = kc_ref.shape[1]
    rows_c = lax.broadcasted_iota(jnp.int32, (GROUP * s_len, w), 0) % s_len
    cols_c = lax.broadcasted_iota(jnp.int32, (GROUP * s_len, w), 1)
    dist_c = rows_c + w - cols_c
    valid_c = (dist_c >= 0) & (dist_c <= WINDOW)
    rows_n = lax.broadcasted_iota(jnp.int32, (GROUP * s_len, s_len), 0) % s_len
    cols_n = lax.broadcasted_iota(jnp.int32, (GROUP * s_len, s_len), 1)
    dist_n = rows_n - cols_n
    valid_n = (dist_n >= 0) & (dist_n <= WINDOW)
    for j in range(seqs):
        r0 = j * s_len
        qj = q_ref[r0:r0 + s_len, :]
        kc = kc_ref[j]
        vc = vc_ref[j]
        kn = kn_ref[r0:r0 + s_len, :]
        vn = vn_ref[r0:r0 + s_len, :]
        ko_ref[j, 0:w - s_len, :] = kc[s_len:, :]
        ko_ref[j, w - s_len:w, :] = kn
        vo_ref[j, 0:w - s_len, :] = vc[s_len:, :]
        vo_ref[j, w - s_len:w, :] = vn
        kcb, vcb, knb, vnb = kc.astype(BF16), vc.astype(BF16), kn.astype(BF16), vn.astype(BF16)
        for g in range(N_KV_HEADS):
            lo, hi = g * HEAD_DIM, (g + 1) * HEAD_DIM
            qs = jnp.concatenate(
                [qj[:, (g * GROUP + h) * HEAD_DIM:(g * GROUP + h + 1) * HEAD_DIM] for h in range(GROUP)], axis=0)
            nt = (((1,), (1,)), ((), ()))
            s_c = lax.dot_general(qs, kcb[:, lo:hi], nt, preferred_element_type=F32) * SM_SCALE
            s_n = lax.dot_general(qs, knb[:, lo:hi], nt, preferred_element_type=F32) * SM_SCALE
            s_c = jnp.where(valid_c, s_c + bc_ref[g], NEG_INF)
            s_n = jnp.where(valid_n, s_n + bn_ref[g], NEG_INF)
            sink = sink_ref[g]
            m = jnp.maximum(jnp.maximum(jnp.max(s_c, axis=-1, keepdims=True), jnp.max(s_n, axis=-1, keepdims=True)), sink)
            p_c = jnp.exp(s_c - m)
            p_n = jnp.exp(s_n - m)
            denom = jnp.sum(p_c, axis=-1, keepdims=True) + jnp.sum(p_n, axis=-1, keepdims=True) + jnp.exp(sink - m)
            o = jnp.dot(p_c.astype(BF16), vcb[:, lo:hi], preferred_element_type=F32)
            o = o + jnp.dot(p_n.astype(BF16), vnb[:, lo:hi], preferred_element_type=F32)
            o = (o / denom).astype(BF16)
            for h in range(GROUP):
                hh = g * GROUP + h
                o_ref[r0:r0 + s_len, hh * HEAD_DIM:(hh + 1) * HEAD_DIM] = o[h * s_len:(h + 1) * s_len, :]


def _attn_sample(q_all, k_all, v_all, cache_k, cache_v, rel_bias, sinks, row0, dec_batch, s_len):
    w = cache_k.shape[1]
    seqs = _pick_tile(dec_batch, (16, 8, 4, 2, 1))
    rows = seqs * s_len
    blk0 = row0 // rows
    assert row0 % rows == 0
    qi = jnp.arange(s_len)
    dist_c = qi[:, None] + w - jnp.arange(w)[None, :]
    dist_n = qi[:, None] - jnp.arange(s_len)[None, :]
    b_c = _bias_table(rel_bias, dist_c).reshape(N_KV_HEADS, GROUP * s_len, w)
    b_n = _bias_table(rel_bias, dist_n).reshape(N_KV_HEADS, GROUP * s_len, s_len)
    sink = jnp.broadcast_to(sinks.astype(F32).reshape(N_KV_HEADS, GROUP, 1, 1), (N_KV_HEADS, GROUP, s_len, 1))
    sink = sink.reshape(N_KV_HEADS, GROUP * s_len, 1)
    tok = lambda width: pl.BlockSpec((rows, width), lambda i: (blk0 + i, 0))
    cache = pl.BlockSpec((seqs, w, D_KV), lambda i: (i, 0, 0))
    return pl.pallas_call(
        functools.partial(_attn_sample_kernel, seqs=seqs, s_len=s_len),
        grid=(dec_batch // seqs,),
        in_specs=[
            tok(D_ATTN), tok(D_KV), tok(D_KV), cache, cache,
            _const_spec(b_c.shape), _const_spec(b_n.shape), _const_spec(sink.shape),
        ],
        out_specs=(pl.BlockSpec((rows, D_ATTN), lambda i: (i, 0)), cache, cache),
        out_shape=(
            jax.ShapeDtypeStruct((dec_batch * s_len, D_ATTN), BF16),
            jax.ShapeDtypeStruct((dec_batch, w, D_KV), F32),
            jax.ShapeDtypeStruct((dec_batch, w, D_KV), F32),
        ),
        compiler_params=_params(("parallel",)),
        name="attn_sample",
    )(q_all, k_all, v_all, cache_k, cache_v, b_c, b_n, sink)


def _softplus(z):
    return jnp.maximum(z, 0.0) + jnp.log1p(jnp.exp(-jnp.abs(z)))


def _block_gate(xcb, w_ref, b_ref):
    parts = [jnp.dot(xcb[:, n * RNN_BLOCK:(n + 1) * RNN_BLOCK], w_ref[n], preferred_element_type=F32)
             for n in range(N_RNN_BLOCKS)]
    return jax.nn.sigmoid(jnp.concatenate(parts, axis=-1) + b_ref[...])


def _lru_coeffs(xc, wa_ref, ba_ref, wx_ref, bx_ref, lam_ref, first_row_unnormalised):
    xcb = xc.astype(BF16)
    r = _block_gate(xcb, wa_ref, ba_ref)
    i = _block_gate(xcb, wx_ref, bx_ref)
    log_a = -RG_C * r * _softplus(-lam_ref[...])
    a = jnp.exp(log_a)
    mult = jnp.sqrt(-jnp.tanh(log_a) * (a * a + 1.0))
    if first_row_unnormalised is not None:
        mult = jnp.where(first_row_unnormalised, 1.0, mult)
    return a, mult * i * xc


def _scan8(a, b):
    shape = a.shape
    grouped = (shape[0] // SUBLANES, SUBLANES, shape[1])
    a = a.reshape(grouped)
    b = b.reshape(grouped)
    r8 = lax.broadcasted_iota(jnp.int32, grouped, 1)
    d = 1
    while d < SUBLANES:
        keep = r8 >= d
        a_sh = jnp.where(keep, pltpu.roll(a, d, 1), 1.0)
        b_sh = jnp.where(keep, pltpu.roll(b, d, 1), 0.0)
        b = a * b_sh + b
        a = a * a_sh
        d *= 2
    return a.reshape(shape), b.reshape(shape)


def _rnn_prompt_kernel(xr_ref, gy_ref, cw_ref, cb_ref, wa_ref, ba_ref, wx_ref, bx_ref, lam_ref,
                       o_ref, nh_ref, ext_ref, a_ref, b_ref, hc_ref, *, tl):
    l = pl.program_id(1)

    @pl.when(l == 0)
    def _():
        ext_ref[0:SUBLANES, :] = jnp.zeros((SUBLANES, D_RNN), F32)
        hc_ref[...] = jnp.zeros((1, D_RNN), F32)

    x = xr_ref[...]
    ext_ref[SUBLANES:, :] = x
    xc = cb_ref[...] + cw_ref[CONV_W - 1:CONV_W, :] * x
    for j in range(1, CONV_W):
        xc = xc + cw_ref[CONV_W - 1 - j:CONV_W - j, :] * ext_ref[SUBLANES - j:SUBLANES - j + tl, :]
    ext_ref[0:SUBLANES, :] = x[tl - SUBLANES:, :]

    row = lax.broadcasted_iota(jnp.int32, (tl, D_RNN), 0)
    a, b = _lru_coeffs(xc, wa_ref, ba_ref, wx_ref, bx_ref, lam_ref, (row == 0) & (l == 0))
    a, b = _scan8(a, b)
    a_ref[...] = a
    b_ref[...] = b

    def chunk(c, h):
        sl = pl.ds(pl.multiple_of(c * SUBLANES, SUBLANES), SUBLANES)
        hc = b_ref[sl, :] + a_ref[sl, :] * h
        b_ref[sl, :] = hc
        return hc[SUBLANES - 1:SUBLANES, :]

    h = lax.fori_loop(0, tl // SUBLANES, chunk, hc_ref[...])
    hc_ref[...] = h
    nh_ref[0] = h
    o_ref[...] = (b_ref[...] * gy_ref[...]).astype(BF16)


def _rnn_prompt(xr_all, gy_all, rnn_w, batch, seq):
    tl =_pick_tile(seq, (256, 128, 64, 32, 16, 8))
    nl = seq // tl
    tok = pl.BlockSpec((tl, D_RNN), lambda b, l: (b * nl + l, 0))
    return pl.pallas_call(
        functools.partial(_rnn_prompt_kernel, tl=tl),
        grid=(batch, nl),
        in_specs=[tok, tok] + [_const_spec(w.shape) for w in rnn_w],
        out_specs=(tok, pl.BlockSpec((1, 1, D_RNN), lambda b, l: (b, 0, 0))),
        out_shape=(jax.ShapeDtypeStruct((batch * seq, D_RNN), BF16), jax.ShapeDtypeStruct((batch, 1, D_RNN), F32)),
        scratch_shapes=[
            pltpu.VMEM((tl + SUBLANES, D_RNN), F32),
            pltpu.VMEM((tl, D_RNN), F32),
            pltpu.VMEM((tl, D_RNN), F32),
            pltpu.VMEM((1, D_RNN), F32),
        ],
        compiler_params=_params(("parallel", "arbitrary")),
        name="rnn_prompt",
    )(xr_all, gy_all, *rnn_w)


def _rnn_sample_kernel(xr_ref, gy_ref, hp_ref, h0_ref, cw_ref, cb_ref, wa_ref, ba_ref, wx_ref, bx_ref, lam_ref,
                       o_ref, nh_ref, *, seqs):
    rows = seqs * SUBLANES
    x = xr_ref[...]
    hp = hp_ref[...]
    r8 = lax.broadcasted_iota(jnp.int32, (rows, D_RNN), 0) % SUBLANES
    xc = cb_ref[...] + cw_ref[CONV_W - 1:CONV_W, :] * x
    for j in range(1, CONV_W):
        shifted = jnp.where(r8 >= j, pltpu.roll(x, j, 0), pltpu.roll(hp, rows - (SUBLANES - j), 0))
        xc = xc + cw_ref[CONV_W - 1 - j:CONV_W - j, :] * shifted
    a, b = _lru_coeffs(xc, wa_ref, ba_ref, wx_ref, bx_ref, lam_ref, None)
    a, b = _scan8(a, b)
    h0 = jnp.broadcast_to(h0_ref[...][:, None, :], (seqs, SUBLANES, D_RNN)).reshape(rows, D_RNN)
    h = b + a * h0
    last = jnp.where(r8 == SUBLANES - 1, h, 0.0).reshape(seqs, SUBLANES, D_RNN)
    nh_ref[...] = jnp.sum(last, axis=1)
    o_ref[...] = (h * gy_ref[...]).astype(BF16)


def _rnn_sample(xr_all, gy_all, hist_pad, h0, rnn_w, row0, dec_batch):
    seqs = _pick_tile(dec_batch, (16, 8))
    rows = seqs * SUBLANES
    assert row0 % rows == 0
    blk0 = row0 // rows
    tok = pl.BlockSpec((rows, D_RNN), lambda i: (blk0 + i, 0))
    return pl.pallas_call(
        functools.partial(_rnn_sample_kernel, seqs=seqs),
        grid=(dec_batch // seqs,),
        in_specs=[tok, tok, pl.BlockSpec((rows, D_RNN), lambda i: (i, 0)), pl.BlockSpec((seqs, D_RNN), lambda i: (i, 0))]
        + [_const_spec(w.shape) for w in rnn_w],
        out_specs=(pl.BlockSpec((rows, D_RNN), lambda i: (i, 0)), pl.BlockSpec((seqs, D_RNN), lambda i: (i, 0))),
        out_shape=(jax.ShapeDtypeStruct((dec_batch * SUBLANES, D_RNN), BF16),
                   jax.ShapeDtypeStruct((dec_batch, D_RNN), F32)),
        compiler_params=_params(("parallel",)),
        name="rnn_sample",
    )(xr_all, gy_all, hist_pad, h0, *rnn_w)


def _layer_norm(z, g, b):
    mu = jnp.mean(z, axis=-1, keepdims=True)
    zc = z - mu
    var = jnp.mean(zc * zc, axis=-1, keepdims=True)
    return zc * lax.rsqrt(var + LN_EPS) * g + b


def _first_index_of_max(vals, iota, axis, sentinel):
    mx = jnp.max(vals, axis=axis, keepdims=True)
    return mx, jnp.min(jnp.where(vals == mx, iota, sentinel), axis=axis, keepdims=True)


def _route(scores, bias):
    t = scores.shape[1]
    grp = scores + bias
    g3 = grp.reshape(N_GROUPS, GROUP_SIZE, t)
    e_in_g = lax.broadcasted_iota(jnp.int32, g3.shape, 1)
    m1, first = _first_index_of_max(g3, e_in_g, 1, GROUP_SIZE)
    m2 = jnp.max(jnp.where(e_in_g == first, -jnp.inf, g3), axis=1, keepdims=True)
    gscore = (m1 + m2).reshape(N_GROUPS, t)
    g_iota = lax.broadcasted_iota(jnp.int32, gscore.shape, 0)
    gmask = jnp.zeros(gscore.shape, jnp.bool_)
    for _ in range(TOPK_GROUPS):
        _, gi = _first_index_of_max(gscore, g_iota, 0, N_GROUPS)
        hit = g_iota == gi
        gmask = gmask | hit
        gscore = jnp.where(hit, -jnp.inf, gscore)
    masked = jnp.where(gmask[:, None, :], g3, -jnp.inf).reshape(N_EXPERTS, t)
    e_iota = lax.broadcasted_iota(jnp.int32, masked.shape, 0)
    idx, wts, hits = [], [], []
    for _ in range(TOP_K):
        _, ei = _first_index_of_max(masked, e_iota, 0, N_EXPERTS)
        hit = e_iota == ei
        idx.append(ei)
        hits.append(hit)
        wts.append(jnp.sum(jnp.where(hit, scores, 0.0), axis=0, keepdims=True))
        masked = jnp.where(hit, -jnp.inf, masked)
    idx = jnp.concatenate(idx, axis=0)
    w = jnp.concatenate(wts, axis=0)
    w = w / jnp.sum(w, axis=0, keepdims=True) * ROUTED_SCALE
    return idx, w, hits


def _merge_kernel(xp_ref, xs_ref, aop_ref, aos_ref, rop_ref, ros_ref, sga_ref, sgr_ref, woa_ref, wor_ref, wout_ref, g1_ref, b1_ref,
                  wrt_ref, rb_ref, wsg_ref, wsu_ref, wsd_ref,
                  x1w_ref, base_ref, idx_ref, wt_ref, rank_ref, cnt_ref, carry_ref, *, n_p, tm):
    i = pl.program_id(0)

    @pl.when(i == 0)
    def _():
        carry_ref[...] = jnp.zeros(carry_ref.shape, F32)

    is_prompt = i < n_p
    x = jnp.where(is_prompt, xp_ref[...], xs_ref[...])
    pa = jnp.dot(jnp.where(is_prompt, aop_ref[...], aos_ref[...]), woa_ref[...], preferred_element_type=F32)
    pr = jnp.dot(jnp.where(is_prompt, rop_ref[...], ros_ref[...]), wor_ref[...], preferred_element_type=F32)
    merged = sga_ref[...] * pa + sgr_ref[...] * pr
    z = ALPHA * x + jnp.dot(merged.astype(BF16), wout_ref[...], preferred_element_type=F32)
    x1 = _layer_norm(z, g1_ref[...], b1_ref[...])
    words = _pack_words(x1)
    for h in range(tm // SC_ROWS):
        for s in range(ROW_TILE):
            x1w_ref[(h * ROW_TILE + s) * SC_ROWS:(h * ROW_TILE + s + 1) * SC_ROWS, :] = (
                words[h * SC_ROWS:(h + 1) * SC_ROWS, s * LANES:(s + 1) * LANES])
    x1b = x1.astype(BF16)
    u = jax.nn.silu(jnp.dot(x1b, wsg_ref[...], preferred_element_type=F32)) * jnp.dot(x1b, wsu_ref[...], preferred_element_type=F32)
    shared = jnp.dot(u.astype(BF16), wsd_ref[...], preferred_element_type=F32)
    base_ref[...] = ALPHA * x1 + shared
    logits = lax.dot_general(wrt_ref[...], x1, (((1,), (1,)), ((), ())), preferred_element_type=F32)
    idx, w, hits = _route(jax.nn.sigmoid(logits), rb_ref[...])
    idx_ref[...] = idx
    wt_ref[...] = w

    chosen = functools.reduce(jnp.logical_or, hits)
    chosen_f = jnp.where(chosen, 1.0, 0.0)
    earlier = (lax.broadcasted_iota(jnp.int32, (tm, tm), 0) < lax.broadcasted_iota(jnp.int32, (tm, tm), 1))
    prefix = jnp.dot(chosen_f.astype(BF16), jnp.where(earlier, 1.0, 0.0).astype(BF16), preferred_element_type=F32)
    before = prefix + carry_ref[...]
    ranks = [jnp.sum(jnp.where(hit, before, 0.0), axis=0, keepdims=True) for hit in hits]
    rank_ref[...] = jnp.concatenate(ranks, axis=0).astype(jnp.int32)
    carry_ref[...] = carry_ref[...] + jnp.sum(chosen_f, axis=1, keepdims=True)
    cnt_ref[...] = carry_ref[...]


def _merge(x_p, x_s, attn_p, attn_s, rnn_p, rnn_s, sga, sgr, weights):
    t_p, t_s = x_p.shape[0], x_s.shape[0]
    t = t_p + t_s
    tm = _pick_tile(math.gcd(t_p, t_s), (256, 128))
    row = lambda width: pl.BlockSpec((tm, width), lambda i: (i, 0))
    col = pl.BlockSpec((TOP_K, tm), lambda i: (0, i))

    return pl.pallas_call(
        functools.partial(_merge_kernel, n_p=t_p // tm, tm=tm),
        grid=(t // tm,),
        in_specs=[*_two_source_specs(t_p, t_s, tm, D_MODEL), *_two_source_specs(t_p, t_s, tm, D_ATTN),
                  *_two_source_specs(t_p, t_s, tm, D_RNN), row(D_MODEL), row(D_MODEL)]
        + [_const_spec(w.shape) for w in weights],
        out_specs=(pl.BlockSpec((tm * ROW_TILE, LANES), lambda i: (i, 0)), row(D_MODEL), col, col, col,
                   _const_spec((N_EXPERTS, 1))),
        out_shape=(
            jax.ShapeDtypeStruct((t * ROW_TILE, LANES), WORD),
            jax.ShapeDtypeStruct((t, D_MODEL), F32),
            jax.ShapeDtypeStruct((TOP_K, t), jnp.int32),
            jax.ShapeDtypeStruct((TOP_K, t), F32),
            jax.ShapeDtypeStruct((TOP_K, t), jnp.int32),
            jax.ShapeDtypeStruct((N_EXPERTS, 1), F32),
        ),
        scratch_shapes=[pltpu.VMEM((N_EXPERTS, 1), F32)],
        compiler_params=_params(("arbitrary",)),
        name="merge_ln1_route",
    )(x_p, x_s, attn_p, attn_s, rnn_p, rnn_s, sga, sgr, *weights)


def _expert_layout(counts, n_assign):
    counts = counts.reshape(N_EXPERTS).astype(jnp.int32)
    padded = (counts + ROW_PAD - 1) // ROW_PAD * ROW_PAD
    pend = jnp.cumsum(padded)
    pstart = pend - padded
    rows_alloc = (n_assign + N_EXPERTS * (ROW_PAD - 1)) // ROW_PAD * ROW_PAD + EXPERT_CHUNK
    n_chunks = jnp.maximum((counts + EXPERT_CHUNK - 1) // EXPERT_CHUNK, 1)
    ffn_end = jnp.max(pstart + n_chunks * EXPERT_CHUNK).reshape(1)
    return pstart, pstart + counts, pend, n_chunks, counts, ffn_end, rows_alloc


def _dest_kernel(idx_ref, rank_ref, pstart_ref, dest_ref, word_rows_ref):
    e_iota = lax.broadcasted_iota(jnp.int32, (N_EXPERTS, idx_ref.shape[1]), 0)
    starts = [jnp.sum(jnp.where(e_iota == idx_ref[k:k + 1, :], pstart_ref[...], 0), axis=0, keepdims=True)
              for k in range(TOP_K)]
    dest = jnp.concatenate(starts, axis=0) + rank_ref[...]
    dest_ref[...] = dest
    tm = dest.shape[1]
    word_rows_ref[0] = jnp.concatenate(
        [dest[k:k + 1, h * SC_ROWS:(h + 1) * SC_ROWS] * ROW_TILE + s
         for h in range(tm // SC_ROWS) for s in range(ROW_TILE) for k in range(TOP_K)], axis=0)


def _dest_rows(idx_t, rank_t, pstart, tm):
    t = idx_t.shape[1]
    col = pl.BlockSpec((TOP_K, tm), lambda i: (0, i))
    vecs = tm // SC_ROWS * ROW_TILE * TOP_K
    return pl.pallas_call(
        _dest_kernel,
        grid=(t // tm,),
        in_specs=[col, col, _const_spec((N_EXPERTS, 1))],
        out_specs=(col, pl.BlockSpec((1, vecs, SC_ROWS), lambda i: (i, 0, 0))),
        out_shape=(jax.ShapeDtypeStruct((TOP_K, t), jnp.int32),
                   jax.ShapeDtypeStruct((t // tm, vecs, SC_ROWS), jnp.int32)),
        compiler_params=_params(("parallel",)),
        name="dest_rows",
    )(idx_t, rank_t, pstart.reshape(N_EXPERTS, 1))


def _token_rows(r, n=1):
    return pl.ds(pl.multiple_of(r * ROW_TILE, ROW_TILE), n * ROW_TILE)


def _zero_row_groups(zero_ref, dst_ref, sem, first_group, n_groups):
    def start(g, c):
        pltpu.make_async_copy(zero_ref, dst_ref.at[_token_rows(g * ROW_PAD, ROW_PAD)], sem).start()
        return c

    lax.fori_loop(first_group, n_groups, start, 0)
    return n_groups - first_group


def _wait_zero_copies(zero_ref, dst_ref, sem, n):
    def wait(_, c):
        pltpu.make_async_copy(zero_ref, dst_ref.at[_token_rows(0, ROW_PAD)], sem).wait()
        return c

    lax.fori_loop(0, n, wait, 0)


SC_ROWS = 128


def _sc_workers():
    info = plsc.get_sparse_core_info()
    mesh = plsc.VectorSubcoreMesh(core_axis_name="c", subcore_axis_name="s")
    worker = lambda: lax.axis_index("s") * info.num_cores + lax.axis_index("c")
    return mesh, info.num_cores * info.num_subcores, worker


def _sc_scatter_rows(src, rows, n_out):
    mesh, n_workers, worker = _sc_workers()
    n_units = src.shape[0] // SC_ROWS
    assert n_units % n_workers == 0 and rows.shape == (n_units * TOP_K, SC_ROWS)
    per_worker = n_units // n_workers

    @functools.partial(
        pl.kernel, mesh=mesh,
        out_type=jax.ShapeDtypeStruct((n_out, LANES), src.dtype),
        scratch_types=[pltpu.VMEM((TOP_K, SC_ROWS), jnp.int32), pltpu.VMEM((SC_ROWS, LANES), src.dtype),
                       pltpu.SemaphoreType.DMA],
    )
    def scatter(src_hbm, rows_hbm, out_hbm, rows_v, data_v, sem):
        first = worker() * per_worker

        @pl.loop(0, per_worker)
        def _(it):
            u = first + it
            pltpu.sync_copy(rows_hbm.at[pl.ds(u * TOP_K, TOP_K)], rows_v)
            pltpu.sync_copy(src_hbm.at[pl.ds(u * SC_ROWS, SC_ROWS)], data_v)
            copies = [pltpu.async_copy(data_v, out_hbm.at[rows_v.at[k]], sem) for k in range(TOP_K)]
            for cp in copies:
                cp.wait()

    return scatter(src, rows)


def _zero_padding_kernel(lo_ref, hi_ref, tail_ref, xs_in_ref, xs_ref, zero_ref, sem, *, rows_alloc):
    del xs_in_ref
    zero_ref[...] = jnp.zeros(zero_ref.shape, WORD)
    row_copy = lambda r: pltpu.make_async_copy(zero_ref.at[pl.ds(0, ROW_TILE)], xs_ref.at[_token_rows(r)], sem)

    def expert(e, n):
        def row(r, c):
            row_copy(r).start()
            return c

        lax.fori_loop(lo_ref[e], hi_ref[e], row, 0)
        return n + hi_ref[e] - lo_ref[e]

    def wait(_, c):
        row_copy(0).wait()
        return c

    lax.fori_loop(0, lax.fori_loop(0, N_EXPERTS, expert, 0), wait, 0)
    n = _zero_row_groups(zero_ref, xs_ref, sem, tail_ref[0] // ROW_PAD, rows_alloc // ROW_PAD)
    _wait_zero_copies(zero_ref, xs_ref, sem, n)


def _zero_padding(xs, pad_lo, pad_hi, total, rows_alloc):
    grid_spec = pltpu.PrefetchScalarGridSpec(
        num_scalar_prefetch=3,
        grid=(1,),
        in_specs=[pl.BlockSpec(memory_space=pl.ANY)],
        out_specs=pl.BlockSpec(memory_space=pl.ANY),
        scratch_shapes=[pltpu.VMEM((ROW_PAD * ROW_TILE, LANES), WORD), pltpu.SemaphoreType.DMA],
    )
    return pl.pallas_call(
        functools.partial(_zero_padding_kernel, rows_alloc=rows_alloc),
        grid_spec=grid_spec,
        out_shape=jax.ShapeDtypeStruct(xs.shape, xs.dtype),
        input_output_aliases={3: 0},
        compiler_params=_params(("arbitrary",)),
        name="zero_padding",
    )(pad_lo, pad_hi, total, xs)


def _expert_kernel(pstart_ref, nch_ref, rows_ref, end_ref, wg_ref, wu_ref, wd_ref, xs_ref, o_ref,
                   xbuf_ref, obuf_ref, wgb_ref, wub_ref, wdb_ref, zero_ref, done_ref, in_sem, out_sem, *, rows_alloc):
    e = pl.program_id(0)
    n_e = pl.num_programs(0)
    start = pstart_ref[e]
    nch = nch_ref[e]
    buf_rows = EXPERT_CHUNK * ROW_TILE

    def in_copy(row, slot):
        return pltpu.make_async_copy(xs_ref.at[_token_rows(row, EXPERT_CHUNK)],
                                     xbuf_ref.at[pl.ds(slot * buf_rows, buf_rows)], in_sem.at[slot])

    def out_copy(row, slot):
        return pltpu.make_async_copy(obuf_ref.at[pl.ds(slot * buf_rows, buf_rows)],
                                     o_ref.at[_token_rows(row, EXPERT_CHUNK)], out_sem)

    @pl.when(e == 0)
    def _():
        done_ref[0] = 0
        obuf_ref[...] = jnp.zeros(obuf_ref.shape, WORD)
        in_copy(start, 0).start()

    wgb_ref[...] = wg_ref[...].astype(BF16)
    wub_ref[...] = wu_ref[...].astype(BF16)
    wdb_ref[...] = wd_ref[...].astype(BF16)
    done = done_ref[0]

    def chunk(c, carry):
        g = done + c
        slot = g % 2
        row = start + c * EXPERT_CHUNK
        in_copy(row, slot).wait()
        last = c + 1 == nch
        next_row = jnp.where(last, pstart_ref[jnp.minimum(e + 1, n_e - 1)], row + EXPERT_CHUNK)

        @pl.when(jnp.logical_not(last & (e == n_e - 1)))
        def _():
            in_copy(next_row, 1 - slot).start()

        base = pl.multiple_of(slot * buf_rows, buf_rows)
        for h in range(EXPERT_CHUNK // EXPERT_ROWS):
            @pl.when(c * EXPERT_CHUNK + h * EXPERT_ROWS < jnp.maximum(rows_ref[e], 1))
            def _():
                xb = jnp.concatenate(_load_token_rows(xbuf_ref, EXPERT_ROWS, row0=h * EXPERT_ROWS, base=base),
                                     axis=1).astype(BF16)
                gate = jnp.dot(xb, wgb_ref[...], preferred_element_type=F32)
                up = jnp.dot(xb, wub_ref[...], preferred_element_type=F32)
                act = (jax.nn.silu(gate) * up).astype(BF16)
                _store_token_rows(obuf_ref, jnp.dot(act, wdb_ref[...], preferred_element_type=F32),
                                  EXPERT_ROWS, row0=h * EXPERT_ROWS, base=base)

        @pl.when(g > 0)
        def _():
            out_copy(0, 0).wait()

        out_copy(row, slot).start()
        return carry

    lax.fori_loop(0, nch, chunk, 0)
    done_ref[0] = done + nch

    @pl.when(e == n_e - 1)
    def _():
        out_copy(0, 0).wait()
        zero_ref[...] = jnp.zeros(zero_ref.shape, WORD)
        n = _zero_row_groups(zero_ref, o_ref, out_sem, end_ref[0] // ROW_PAD, rows_alloc // ROW_PAD)
        _wait_zero_copies(zero_ref, o_ref, out_sem, n)


def _expert_ffn(xs, pstart, n_chunks, rows, ffn_end, rows_alloc, w_e_gate, w_e_up, w_e_down):
    weight = lambda shape: pl.BlockSpec((None, *shape), lambda e, *_: (e, 0, 0))
    grid_spec = pltpu.PrefetchScalarGridSpec(
        num_scalar_prefetch=4,
        grid=(N_EXPERTS,),
        in_specs=[weight((D_MODEL, D_EXPERT)), weight((D_MODEL, D_EXPERT)), weight((D_EXPERT, D_MODEL)),
                  pl.BlockSpec(memory_space=pl.ANY)],
        out_specs=pl.BlockSpec(memory_space=pl.ANY),
        scratch_shapes=[
            pltpu.VMEM((2 * EXPERT_CHUNK * ROW_TILE, LANES), WORD),
            pltpu.VMEM((2 * EXPERT_CHUNK * ROW_TILE, LANES), WORD),
            pltpu.VMEM((D_MODEL, D_EXPERT), BF16), pltpu.VMEM((D_MODEL, D_EXPERT), BF16),
            pltpu.VMEM((D_EXPERT, D_MODEL), BF16),
            pltpu.VMEM((ROW_PAD * ROW_TILE, LANES), WORD),
            pltpu.SMEM((1,), jnp.int32),
            pltpu.SemaphoreType.DMA((2,)), pltpu.SemaphoreType.DMA,
        ],
    )
    return pl.pallas_call(
        functools.partial(_expert_kernel, rows_alloc=rows_alloc),
        grid_spec=grid_spec,
        out_shape=jax.ShapeDtypeStruct((rows_alloc * ROW_TILE, LANES), WORD),
        compiler_params=_params(("arbitrary",)),
        name="expert_ffn",
    )(pstart, n_chunks, rows, ffn_end, w_e_gate, w_e_up, w_e_down, xs)


def _combine_head_kernel(dest_ref, dest_next_ref, w_ref, base_ref, g_ref, b_ref, outs_ref, y_ref, buf_ref, sem, *, tm):
    i = pl.program_id(0)
    slot_rows = TOP_K * tm

    def gather(d_ref, slot):
        def issue(t, c):
            for k in range(TOP_K):
                pltpu.make_async_copy(outs_ref.at[_token_rows(d_ref[k, t])],
                                      buf_ref.at[_token_rows(slot * slot_rows + k * tm + t)], sem.at[slot]).start()
            return c

        lax.fori_loop(0, tm, issue, 0)

    @pl.when(i == 0)
    def _():
        gather(dest_ref, 0)

    @pl.when(i + 1 < pl.num_programs(0))
    def _():
        gather(dest_next_ref, (i + 1) % 2)

    slot = i % 2
    for k in range(TOP_K):
        pltpu.make_async_copy(outs_ref.at[_token_rows(0, tm)], buf_ref.at[_token_rows(slot * slot_rows + k * tm, tm)],
                              sem.at[slot]).wait()

    w = w_ref[...]
    base = pl.multiple_of(slot * slot_rows * ROW_TILE, ROW_TILE)
    halves = [None, None]
    for k in range(TOP_K):
        wk = w[:, k:k + 1]
        for j, rows in enumerate(_load_token_rows(buf_ref, tm, row0=k * tm, base=base)):
            halves[j] = wk * rows if halves[j] is None else halves[j] + wk * rows
    y_ref[...] = _layer_norm(base_ref[...] + jnp.concatenate(halves, axis=1), g_ref[...], b_ref[...])


def _combine_head(out_sorted, dest, w_tok, base, g2, b2, tm, n_head):
    row = pl.BlockSpec((tm, D_MODEL), lambda i: (i, 0))
    return pl.pallas_call(
        functools.partial(_combine_head_kernel, tm=tm),
        grid=(n_head,),
        in_specs=[
            pl.BlockSpec((TOP_K, tm), lambda i: (0, i), memory_space=pltpu.SMEM),
            pl.BlockSpec((TOP_K, tm), lambda i: (0, jnp.minimum(i + 1, n_head - 1)), memory_space=pltpu.SMEM),
            pl.BlockSpec((tm, TOP_K), lambda i: (i, 0)),
            row, _const_spec((1, D_MODEL)), _const_spec((1, D_MODEL)),
            pl.BlockSpec(memory_space=pl.ANY),
        ],
        out_specs=row,
        out_shape=jax.ShapeDtypeStruct((n_head * tm, D_MODEL), F32),
        scratch_shapes=[pltpu.VMEM((2 * TOP_K * tm * ROW_TILE, LANES), WORD), pltpu.SemaphoreType.DMA((2,))],
        compiler_params=_params(("arbitrary",)),
        name="combine_head",
    )(dest, dest, w_tok, base, g2, b2, out_sorted)


def _sc_gather_rows(table, rows):
    mesh, n_workers, worker = _sc_workers()
    m = rows.shape[0]
    per_worker = m // n_workers
    assert m % n_workers == 0 and per_worker % SC_ROWS == 0
    in_flight = _pick_tile(per_worker // SC_ROWS, (4, 2, 1))
    step = SC_ROWS * in_flight

    @functools.partial(
        pl.kernel, mesh=mesh,
        out_type=jax.ShapeDtypeStruct((m, LANES), table.dtype),
        scratch_types=[pltpu.VMEM((step,), jnp.int32), pltpu.VMEM((step, LANES), table.dtype), pltpu.SemaphoreType.DMA],
    )
    def gather(table_hbm, rows_hbm, out_hbm, rows_v, data_v, sem):
        first = worker() * per_worker

        @pl.loop(0, per_worker // step)
        def _(it):
            off = first + it * step
            pltpu.sync_copy(rows_hbm.at[pl.ds(off, step)], rows_v)
            copies = [pltpu.async_copy(table_hbm.at[rows_v.at[pl.ds(j * SC_ROWS, SC_ROWS)]],
                                       data_v.at[pl.ds(j * SC_ROWS, SC_ROWS)], sem)
                      for j in range(in_flight)]
            for cp in copies:
                cp.wait()
            pltpu.sync_copy(data_v, out_hbm.at[pl.ds(off, step)])

    return gather(table, rows)


def _combine_tail_kernel(yh_ref, rows_ref, w_ref, base_ref, g_ref, b_ref, yp_ref, ys_ref, *, tm, n_head, n_p):
    i = pl.program_id(0)

    @pl.when(i < n_head)
    def _():
        yp_ref[...] = yh_ref[...]

    @pl.when(i >= n_head)
    def _():
        w = w_ref[...]
        lo = [None] * ROW_TILE
        hi = [None] * ROW_TILE
        for k in range(TOP_K):
            wk = w[:, k:k + 1]
            for s in range(ROW_TILE):
                vec = lambda h: ((h * ROW_TILE + s) * TOP_K + k) * SC_ROWS
                words = jnp.concatenate([rows_ref[vec(h):vec(h) + SC_ROWS, :] for h in range(tm // SC_ROWS)], axis=0)
                lo_s, hi_s = _unpack_words(words)
                lo[s] = wk * lo_s if lo[s] is None else lo[s] + wk * lo_s
                hi[s] = wk * hi_s if hi[s] is None else hi[s] + wk * hi_s
        y = _layer_norm(base_ref[...] + jnp.concatenate(lo + hi, axis=1), g_ref[...], b_ref[...])

        @pl.when(i < n_p)
        def _():
            yp_ref[...] = y

        @pl.when(i >= n_p)
        def _():
            ys_ref[...] = y


def _combine_tail(y_head, gathered, w_tok, base, g2, b2, t_p, t_s, tm, n_head):
    n_p = t_p // tm
    assert n_head <= n_p
    n_tiles = (t_p + t_s) // tm
    blk = TOP_K * ROW_TILE * tm
    row = pl.BlockSpec((tm, D_MODEL), lambda i: (i, 0))
    out_p, out_s = _two_source_specs(t_p, t_s, tm, D_MODEL)
    return pl.pallas_call(
        functools.partial(_combine_tail_kernel, tm=tm, n_head=n_head, n_p=n_p),
        grid=(n_tiles,),
        in_specs=[
            pl.BlockSpec((tm, D_MODEL), lambda i: (jnp.minimum(i, n_head - 1), 0)),
            pl.BlockSpec((blk, LANES), lambda i: (jnp.maximum(i - n_head, 0), 0)),
            pl.BlockSpec((tm, TOP_K), lambda i: (i, 0)),
            row, _const_spec((1, D_MODEL)), _const_spec((1, D_MODEL)),
        ],
        out_specs=(out_p, out_s),
        out_shape=(jax.ShapeDtypeStruct((t_p, D_MODEL), F32), jax.ShapeDtypeStruct((t_s, D_MODEL), F32)),
        compiler_params=_params(("arbitrary",)),
        name="combine_tail",
    )(y_head, gathered, w_tok, base, g2, b2)


def kernel(x_prompt, x_sample, cache_k, cache_v, state_conv, state_rnn, w_in, conv_w, conv_b, w_gate_a, b_gate_a, w_gate_x, b_gate_x, lru_lambda, rel_bias, sinks, w_o_attn, w_o_rnn, w_out, ln1_g, ln1_b, w_router, router_bias, w_e_gate, w_e_up, w_e_down, w_s_gate, w_s_up, w_s_down, ln2_g, ln2_b):
    assert w_in.shape[0] == DEPTH == 1
    batch, seq, _ = x_prompt.shape
    dec_batch, s_len, _ = x_sample.shape
    w_cache = cache_k.shape[2]
    assert s_len == SUBLANES and seq % WINDOW == 0 and w_cache == WINDOW
    t_p = batch * seq
    t_s = dec_batch * s_len
    vec = lambda a: a[0].reshape(1, -1).astype(F32)

    x_p = x_prompt.reshape(t_p, D_MODEL)
    x_s = x_sample.reshape(t_s, D_MODEL)
    q, k, v, xr, gy, sga, sgr = _inproj(x_p, x_s, w_in[0].astype(BF16))

    qi = jnp.arange(WINDOW)
    dist = qi[:, None] + WINDOW - jnp.arange(2 * WINDOW)[None, :]
    bias_p = _bias_table(rel_bias, dist)
    attn_p = _attn_prompt(q, k, v, bias_p, sinks[0], batch, seq)
    attn_s, k_s, v_s = _attn_sample(
        q, k, v, cache_k[0].reshape(dec_batch, w_cache, D_KV), cache_v[0].reshape(dec_batch, w_cache, D_KV),
        rel_bias, sinks[0], t_p, dec_batch, s_len)

    rnn_w = (conv_w[0], vec(conv_b), w_gate_a[0].astype(BF16), vec(b_gate_a), w_gate_x[0].astype(BF16),
             vec(b_gate_x), vec(lru_lambda))
    rnn_p, h_p = _rnn_prompt(xr, gy, rnn_w, batch, seq)
    hist_pad = jnp.pad(state_conv[0], ((0, 0), (SUBLANES - (CONV_W - 1), 0), (0, 0))).reshape(t_s, D_RNN)
    rnn_s, h_s = _rnn_sample(xr, gy, hist_pad, state_rnn[0], rnn_w, t_p, dec_batch)

    merge_w = (w_o_attn[0].astype(BF16), w_o_rnn[0].astype(BF16), w_out[0].astype(BF16), vec(ln1_g), vec(ln1_b),
               w_router[0].T, router_bias[0].reshape(N_EXPERTS, 1), w_s_gate[0].astype(BF16),
               w_s_up[0].astype(BF16), w_s_down[0].astype(BF16))
    x1w, base, idx_t, wt_t, rank_t, counts = _merge(x_p, x_s, attn_p, attn_s, rnn_p, rnn_s, sga, sgr, merge_w)

    pstart, pad_lo, pad_hi, n_chunks, rows, ffn_end, rows_alloc = _expert_layout(counts, (t_p + t_s) * TOP_K)
    tm = _pick_tile(math.gcd(t_p, t_s), (256, 128))
    dest, word_rows = _dest_rows(idx_t, rank_t, pstart, tm)
    xs = _sc_scatter_rows(x1w, word_rows.reshape(-1, SC_ROWS), rows_alloc * ROW_TILE)
    xs = _zero_padding(xs, pad_lo, pad_hi, pad_hi[N_EXPERTS - 1:], rows_alloc)
    out_sorted = _expert_ffn(xs, pstart, n_chunks, rows, ffn_end, rows_alloc, w_e_gate[0], w_e_up[0], w_e_down[0])
    n_tiles = (t_p + t_s) // tm
    n_head = min(t_p // tm, n_tiles // 3)
    w_tok = wt_t.T
    g2, b2 = vec(ln2_g), vec(ln2_b)
    y_head = _combine_head(out_sorted, dest, w_tok, base, g2, b2, tm, n_head)
    gathered = _sc_gather_rows(out_sorted, word_rows[n_head:].reshape(-1))
    y_p, y_s = _combine_tail(y_head, gathered, w_tok, base, g2, b2, t_p, t_s, tm, n_head)
    y_p = y_p.reshape(batch, seq, D_MODEL)
    y_s = y_s.reshape(dec_batch, s_len, D_MODEL)
    kv5 = lambda a, b: a.reshape(1, b, WINDOW, N_KV_HEADS, HEAD_DIM)
    tail = lambda a, n: jnp.stack([lax.slice_in_dim(a, (b + 1) * seq - n, (b + 1) * seq) for b in range(batch)])
    k_p = kv5(tail(k, WINDOW), batch)
    v_p = kv5(tail(v, WINDOW), batch)
    conv_p = tail(xr, CONV_W - 1)[None]
    conv_s = xr[t_p:].reshape(dec_batch, s_len, D_RNN)[:, s_len - (CONV_W - 1):][None]
    return (y_p, y_s, k_p, v_p, conv_p, h_p.reshape(1, batch, D_RNN),
            kv5(k_s, dec_batch), kv5(v_s, dec_batch), conv_s, h_s.reshape(1, dec_batch, D_RNN))
```

```python
import functools
import math

import jax
import jax.numpy as jnp
from jax import lax
from jax.experimental import pallas as pl
from jax.experimental.pallas import tpu as pltpu
from jax.experimental.pallas import tpu_sc as plsc

F32 = jnp.float32
BF16 = jnp.bfloat16
WORD = jnp.int32

D_MODEL = 1024
N_HEADS = 8
N_KV_HEADS = 2
HEAD_DIM = 64
GROUP = N_HEADS // N_KV_HEADS
WINDOW = 128
D_ATTN = N_HEADS * HEAD_DIM
D_KV = N_KV_HEADS * HEAD_DIM
N_BUCKETS = 32
MAX_DISTANCE = 128
D_RNN = D_MODEL
RNN_BLOCK = 256
N_RNN_BLOCKS = D_RNN // RNN_BLOCK
CONV_W = 4
RG_C = 8.0
N_EXPERTS = 256
TOP_K = 8
N_GROUPS = 8
GROUP_SIZE = N_EXPERTS // N_GROUPS
TOPK_GROUPS = 4
D_EXPERT = D_MODEL // 4
ROUTED_SCALE = 2.5
LN_EPS = 1e-5
DEPTH = 1
ALPHA = (2 * DEPTH) ** 0.25
NEG_INF = -1e30
SM_SCALE = HEAD_DIM ** -0.5

O_Q = 0
O_K = D_ATTN
O_V = O_K + D_KV
O_XR = O_V + D_KV
O_YR = O_XR + D_RNN
O_GA = O_YR + D_RNN
O_GR = O_GA + D_MODEL
D_IN = O_GR + D_MODEL

SUBLANES = 8
VMEM_LIMIT_BYTES = 56 * 1024 * 1024
EXPERT_ROWS = 256
EXPERT_CHUNK = 512
CHUNK_COPIES = 4
ROW_PAD = SUBLANES


def _params(sem):
    return pltpu.CompilerParams(dimension_semantics=sem, vmem_limit_bytes=VMEM_LIMIT_BYTES)


def _pick_tile(n, candidates):
    for c in candidates:
        if n % c == 0:
            return c
    raise ValueError(f"no tile for {n}")


def _const_spec(shape):
    nd = len(shape)
    return pl.BlockSpec(shape, lambda *_: (0,) * nd)


LANES = 128
ROW_WORDS = D_MODEL // 2
ROW_TILE = ROW_WORDS // LANES
HIGH_HALF = -65536


def _pack_words(mat):
    as_bits = lambda v: pltpu.bitcast(v.astype(BF16).astype(F32), WORD)
    return (as_bits(mat[:, ROW_WORDS:]) & HIGH_HALF) | lax.shift_right_logical(as_bits(mat[:, :ROW_WORDS]), 16)


def _unpack_words(words):
    return pltpu.bitcast(words << 16, F32), pltpu.bitcast(words & HIGH_HALF, F32)


def _store_token_rows(ref, mat, n, row0=0, base=0):
    words = _pack_words(mat)
    for s in range(ROW_TILE):
        ref[pl.ds(base + row0 * ROW_TILE + s, n, stride=ROW_TILE), :] = words[:, s * LANES:(s + 1) * LANES]


def _load_token_rows(ref, n, row0=0, base=0):
    words = jnp.concatenate(
        [ref[pl.ds(base + row0 * ROW_TILE + s, n, stride=ROW_TILE), :] for s in range(ROW_TILE)], axis=1)
    return _unpack_words(words)


def _two_source_specs(t_p, t_s, tm, width):
    n_p = t_p // tm
    assert t_p % tm == 0 and t_s % tm == 0
    return (pl.BlockSpec((tm, width), lambda i, *_: (jnp.minimum(i, n_p - 1), 0)),
            pl.BlockSpec((tm, width), lambda i, *_: (jnp.maximum(i - n_p, 0), 0)))


def _inproj_kernel(xp_ref, xs_ref, w_ref, q_ref, k_ref, v_ref, xr_ref, gy_ref, sga_ref, sgr_ref, *, n_p):
    x = jnp.where(pl.program_id(0) < n_p, xp_ref[...], xs_ref[...]).astype(BF16)

    def seg(lo, hi):
        return jnp.dot(x, w_ref[:, lo:hi], preferred_element_type=F32)

    q_ref[...] = seg(O_Q, O_K).astype(BF16)
    k_ref[...] = seg(O_K, O_V)
    v_ref[...] = seg(O_V, O_XR)
    xr_ref[...] = seg(O_XR, O_YR)
    gy_ref[...] = jax.nn.gelu(seg(O_YR, O_GA)).astype(BF16)
    sga_ref[...] = jax.nn.sigmoid(seg(O_GA, O_GR)).astype(BF16)
    sgr_ref[...] = jax.nn.sigmoid(seg(O_GR, D_IN)).astype(BF16)


def _inproj(x_p, x_s, w_in_bf16):
    t_p, t_s = x_p.shape[0], x_s.shape[0]
    t = t_p + t_s
    tm = _pick_tile(math.gcd(t_p, t_s), (256, 128, 64, 32, 16, 8))
    row = lambda width: pl.BlockSpec((tm, width), lambda i: (i, 0))
    out_shape = (
        jax.ShapeDtypeStruct((t, D_ATTN), BF16),
        jax.ShapeDtypeStruct((t, D_KV), F32),
        jax.ShapeDtypeStruct((t, D_KV), F32),
        jax.ShapeDtypeStruct((t, D_RNN), F32),
        jax.ShapeDtypeStruct((t, D_RNN), BF16),
        jax.ShapeDtypeStruct((t, D_MODEL), BF16),
        jax.ShapeDtypeStruct((t, D_MODEL), BF16),
    )
    return pl.pallas_call(
        functools.partial(_inproj_kernel, n_p=t_p // tm),
        grid=(t // tm,),
        in_specs=[*_two_source_specs(t_p, t_s, tm, D_MODEL), _const_spec((D_MODEL, D_IN))],
        out_specs=(row(D_ATTN), row(D_KV), row(D_KV), row(D_RNN), row(D_RNN), row(D_MODEL), row(D_MODEL)),
        out_shape=out_shape,
        compiler_params=_params(("parallel",)),
        name="inproj",
    )(x_p, x_s, w_in_bf16)


def _t5_bucket(dist):
    n = jnp.maximum(dist, 0)
    max_exact = N_BUCKETS // 2
    nf = jnp.maximum(n, 1).astype(F32)
    large = max_exact + (jnp.log(nf / max_exact) / math.log(MAX_DISTANCE / max_exact) * (N_BUCKETS - max_exact)).astype(jnp.int32)
    large = jnp.minimum(large, N_BUCKETS - 1)
    return jnp.where(n < max_exact, n, large)


def _bias_table(rel_bias, dist):
    bucket = _t5_bucket(dist)
    rb = rel_bias.astype(F32)
    out = jnp.zeros((N_HEADS, *dist.shape), F32)
    for j in range(N_BUCKETS):
        out = jnp.where(bucket[None] == j, rb[j][:, None, None], out)
    return out


def _softmax_pv(s, sink, v):
    m = jnp.maximum(jnp.max(s, axis=-1, keepdims=True), sink)
    p = jnp.exp(s - m)
    denom = jnp.sum(p, axis=-1, keepdims=True) + jnp.exp(sink - m)
    return jnp.dot(p.astype(BF16), v, preferred_element_type=F32), denom


def _attn_prompt_kernel(sink_ref, q_ref, kc_ref, kp_ref, vc_ref, vp_ref, bias_ref, o_ref):
    n = pl.program_id(1)
    kk = jnp.concatenate([kp_ref[...], kc_ref[...]], axis=0).astype(BF16)
    vv = jnp.concatenate([vp_ref[...], vc_ref[...]], axis=0).astype(BF16)
    q = q_ref[...]
    rows = lax.broadcasted_iota(jnp.int32, (WINDOW, 2 * WINDOW), 0)
    cols = lax.broadcasted_iota(jnp.int32, (WINDOW, 2 * WINDOW), 1)
    dist = rows + WINDOW - cols
    valid = (dist >= 0) & (dist <= WINDOW) & ((n > 0) | (cols >= WINDOW))
    for g in range(N_KV_HEADS):
        kg = kk[:, g * HEAD_DIM:(g + 1) * HEAD_DIM]
        vg = vv[:, g * HEAD_DIM:(g + 1) * HEAD_DIM]
        for h in range(GROUP):
            hh = g * GROUP + h
            qh = q[:, hh * HEAD_DIM:(hh + 1) * HEAD_DIM]
            s = lax.dot_general(qh, kg, (((1,), (1,)), ((), ())), preferred_element_type=F32) * SM_SCALE
            s = jnp.where(valid, s + bias_ref[hh], NEG_INF)
            o, denom = _softmax_pv(s, sink_ref[0, hh], vg)
            o_ref[:, hh * HEAD_DIM:(hh + 1) * HEAD_DIM] = (o / denom).astype(BF16)


def _attn_prompt(q_all, k_all, v_all, bias, sinks, batch, seq):
    nb = seq // WINDOW
    cur = lambda width: pl.BlockSpec((WINDOW, width), lambda b, n: (b * nb + n, 0))
    prev = lambda width: pl.BlockSpec((WINDOW, width), lambda b, n: (b * nb + jnp.maximum(n - 1, 0), 0))
    return pl.pallas_call(
        _attn_prompt_kernel,
        grid=(batch, nb),
        in_specs=[
            pl.BlockSpec(memory_space=pltpu.SMEM),
            cur(D_ATTN), cur(D_KV), prev(D_KV), cur(D_KV), prev(D_KV),
            _const_spec((N_HEADS, WINDOW, 2 * WINDOW)),
        ],
        out_specs=cur(D_ATTN),
        out_shape=jax.ShapeDtypeStruct((batch * seq, D_ATTN), BF16),
        compiler_params=_params(("parallel", "arbitrary")),
        name="attn_prompt",
    )(sinks.reshape(1, N_HEADS).astype(F32), q_all, k_all, k_all, v_all, v_all, bias)


def _attn_sample_kernel(q_ref, kn_ref, vn_ref, kc_ref, vc_ref, bc_ref, bn_ref, sink_ref,
                        o_ref, ko_ref, vo_ref, *, seqs, s_len):
    w = kc_ref.shape[1]
    rows_c = lax.broadcasted_iota(jnp.int32, (GROUP * s_len, w), 0) % s_len
    cols_c = lax.broadcasted_iota(jnp.int32, (GROUP * s_len, w), 1)
    dist_c = rows_c + w - cols_c
    valid_c = (dist_c >= 0) & (dist_c <= WINDOW)
    rows_n = lax.broadcasted_iota(jnp.int32, (GROUP * s_len, s_len), 0) % s_len
    cols_n = lax.broadcasted_iota(jnp.int32, (GROUP * s_len, s_len), 1)
    dist_n = rows_n - cols_n
    valid_n = (dist_n >= 0) & (dist_n <= WINDOW)
    for j in range(seqs):
        r0 = j * s_len
        qj = q_ref[r0:r0 + s_len, :]
        kc = kc_ref[j]
        vc = vc_ref[j]
        kn = kn_ref[r0:r0 + s_len, :]
        vn = vn_ref[r0:r0 + s_len, :]
        ko_ref[j, 0:w - s_len, :] = kc[s_len:, :]
        ko_ref[j, w - s_len:w, :] = kn
        vo_ref[j, 0:w - s_len, :] = vc[s_len:, :]
        vo_ref[j, w - s_len:w, :] = vn
        kcb, vcb, knb, vnb = kc.astype(BF16), vc.astype(BF16), kn.astype(BF16), vn.astype(BF16)
        for g in range(N_KV_HEADS):
            lo, hi = g * HEAD_DIM, (g + 1) * HEAD_DIM
            qs = jnp.concatenate(
                [qj[:, (g * GROUP + h) * HEAD_DIM:(g * GROUP + h + 1) * HEAD_DIM] for h in range(GROUP)], axis=0)
            nt = (((1,), (1,)), ((), ()))
            s_c = lax.dot_general(qs, kcb[:, lo:hi], nt, preferred_element_type=F32) * SM_SCALE
            s_n = lax.dot_general(qs, knb[:, lo:hi], nt, preferred_element_type=F32) * SM_SCALE
            s_c = jnp.where(valid_c, s_c + bc_ref[g], NEG_INF)
            s_n = jnp.where(valid_n, s_n + bn_ref[g], NEG_INF)
            sink = sink_ref[g]
            m = jnp.maximum(jnp.maximum(jnp.max(s_c, axis=-1, keepdims=True), jnp.max(s_n, axis=-1, keepdims=True)), sink)
            p_c = jnp.exp(s_c - m)
            p_n = jnp.exp(s_n - m)
            denom = jnp.sum(p_c, axis=-1, keepdims=True) + jnp.sum(p_n, axis=-1, keepdims=True) + jnp.exp(sink - m)
            o = jnp.dot(p_c.astype(BF16), vcb[:, lo:hi], preferred_element_type=F32)
            o = o + jnp.dot(p_n.astype(BF16), vnb[:, lo:hi], preferred_element_type=F32)
            o = (o / denom).astype(BF16)
            for h in range(GROUP):
                hh = g * GROUP + h
                o_ref[r0:r0 + s_len, hh * HEAD_DIM:(hh + 1) * HEAD_DIM] = o[h * s_len:(h + 1) * s_len, :]


def _attn_sample(q_all, k_all, v_all, cache_k, cache_v, rel_bias, sinks, row0, dec_batch, s_len):
    w = cache_k.shape[1]
    seqs = _pick_tile(dec_batch, (16, 8, 4, 2, 1))
    rows = seqs * s_len
    blk0 = row0 // rows
    assert row0 % rows == 0
    qi = jnp.arange(s_len)
    dist_c = qi[:, None] + w - jnp.arange(w)[None, :]
    dist_n = qi[:, None] - jnp.arange(s_len)[None, :]
    b_c = _bias_table(rel_bias, dist_c).reshape(N_KV_HEADS, GROUP * s_len, w)
    b_n = _bias_table(rel_bias, dist_n).reshape(N_KV_HEADS, GROUP * s_len, s_len)
    sink = jnp.broadcast_to(sinks.astype(F32).reshape(N_KV_HEADS, GROUP, 1, 1), (N_KV_HEADS, GROUP, s_len, 1))
    sink = sink.reshape(N_KV_HEADS, GROUP * s_len, 1)
    tok = lambda width: pl.BlockSpec((rows, width), lambda i: (blk0 + i, 0))
    cache = pl.BlockSpec((seqs, w, D_KV), lambda i: (i, 0, 0))
    return pl.pallas_call(
        functools.partial(_attn_sample_kernel, seqs=seqs, s_len=s_len),
        grid=(dec_batch // seqs,),
        in_specs=[
            tok(D_ATTN), tok(D_KV), tok(D_KV), cache, cache,
            _const_spec(b_c.shape), _const_spec(b_n.shape), _const_spec(sink.shape),
        ],
        out_specs=(pl.BlockSpec((rows, D_ATTN), lambda i: (i, 0)), cache, cache),
        out_shape=(
            jax.ShapeDtypeStruct((dec_batch * s_len, D_ATTN), BF16),
            jax.ShapeDtypeStruct((dec_batch, w, D_KV), F32),
            jax.ShapeDtypeStruct((dec_batch, w, D_KV), F32),
        ),
        compiler_params=_params(("parallel",)),
        name="attn_sample",
    )(q_all, k_all, v_all, cache_k, cache_v, b_c, b_n, sink)


def _softplus(z):
    return jnp.maximum(z, 0.0) + jnp.log1p(jnp.exp(-jnp.abs(z)))


def _block_gate(xcb, w_ref, b_ref):
    parts = [jnp.dot(xcb[:, n * RNN_BLOCK:(n + 1) * RNN_BLOCK], w_ref[n], preferred_element_type=F32)
             for n in range(N_RNN_BLOCKS)]
    return jax.nn.sigmoid(jnp.concatenate(parts, axis=-1) + b_ref[...])


def _lru_coeffs(xc, wa_ref, ba_ref, wx_ref, bx_ref, lam_ref, first_row_unnormalised):
    xcb = xc.astype(BF16)
    r = _block_gate(xcb, wa_ref, ba_ref)
    i = _block_gate(xcb, wx_ref, bx_ref)
    log_a = -RG_C * r * _softplus(-lam_ref[...])
    a = jnp.exp(log_a)
    mult = jnp.sqrt(-jnp.tanh(log_a) * (a * a + 1.0))
    if first_row_unnormalised is not None:
        mult = jnp.where(first_row_unnormalised, 1.0, mult)
    return a, mult * i * xc


def _scan8(a, b):
    shape = a.shape
    grouped = (shape[0] // SUBLANES, SUBLANES, shape[1])
    a = a.reshape(grouped)
    b = b.reshape(grouped)
    r8 = lax.broadcasted_iota(jnp.int32, grouped, 1)
    d = 1
    while d < SUBLANES:
        keep = r8 >= d
        a_sh = jnp.where(keep, pltpu.roll(a, d, 1), 1.0)
        b_sh = jnp.where(keep, pltpu.roll(b, d, 1), 0.0)
        b = a * b_sh + b
        a = a * a_sh
        d *= 2
    return a.reshape(shape), b.reshape(shape)


def _rnn_prompt_kernel(xr_ref, gy_ref, cw_ref, cb_ref, wa_ref, ba_ref, wx_ref, bx_ref, lam_ref,
                       o_ref, nh_ref, ext_ref, a_ref, b_ref, hc_ref, *, tl):
    l = pl.program_id(1)

    @pl.when(l == 0)
    def _():
        ext_ref[0:SUBLANES, :] = jnp.zeros((SUBLANES, D_RNN), F32)
        hc_ref[...] = jnp.zeros((1, D_RNN), F32)

    x = xr_ref[...]
    ext_ref[SUBLANES:, :] = x
    xc = cb_ref[...] + cw_ref[CONV_W - 1:CONV_W, :] * x
    for j in range(1, CONV_W):
        xc = xc + cw_ref[CONV_W - 1 - j:CONV_W - j, :] * ext_ref[SUBLANES - j:SUBLANES - j + tl, :]
    ext_ref[0:SUBLANES, :] = x[tl - SUBLANES:, :]

    row = lax.broadcasted_iota(jnp.int32, (tl, D_RNN), 0)
    a, b = _lru_coeffs(xc, wa_ref, ba_ref, wx_ref, bx_ref, lam_ref, (row == 0) & (l == 0))
    a, b = _scan8(a, b)
    a_ref[...] = a
    b_ref[...] = b

    def chunk(c, h):
        sl = pl.ds(pl.multiple_of(c * SUBLANES, SUBLANES), SUBLANES)
        hc = b_ref[sl, :] + a_ref[sl, :] * h
        b_ref[sl, :] = hc
        return hc[SUBLANES - 1:SUBLANES, :]

    h = lax.fori_loop(0, tl // SUBLANES, chunk, hc_ref[...])
    hc_ref[...] = h
    nh_ref[0] = h
    o_ref[...] = (b_ref[...] * gy_ref[...]).astype(BF16)


def _rnn_prompt(xr_all, gy_all, rnn_w, batch, seq):
    tl =_pick_tile(seq, (256, 128, 64, 32, 16, 8))
    nl = seq // tl
    tok = pl.BlockSpec((tl, D_RNN), lambda b, l: (b * nl + l, 0))
    return pl.pallas_call(
        functools.partial(_rnn_prompt_kernel, tl=tl),
        grid=(batch, nl),
        in_specs=[tok, tok] + [_const_spec(w.shape) for w in rnn_w],
        out_specs=(tok, pl.BlockSpec((1, 1, D_RNN), lambda b, l: (b, 0, 0))),
        out_shape=(jax.ShapeDtypeStruct((batch * seq, D_RNN), BF16), jax.ShapeDtypeStruct((batch, 1, D_RNN), F32)),
        scratch_shapes=[
            pltpu.VMEM((tl + SUBLANES, D_RNN), F32),
            pltpu.VMEM((tl, D_RNN), F32),
            pltpu.VMEM((tl, D_RNN), F32),
            pltpu.VMEM((1, D_RNN), F32),
        ],
        compiler_params=_params(("parallel", "arbitrary")),
        name="rnn_prompt",
    )(xr_all, gy_all, *rnn_w)


def _rnn_sample_kernel(xr_ref, gy_ref, hp_ref, h0_ref, cw_ref, cb_ref, wa_ref, ba_ref, wx_ref, bx_ref, lam_ref,
                       o_ref, nh_ref, *, seqs):
    rows = seqs * SUBLANES
    x = xr_ref[...]
    hp = hp_ref[...]
    r8 = lax.broadcasted_iota(jnp.int32, (rows, D_RNN), 0) % SUBLANES
    xc = cb_ref[...] + cw_ref[CONV_W - 1:CONV_W, :] * x
    for j in range(1, CONV_W):
        shifted = jnp.where(r8 >= j, pltpu.roll(x, j, 0), pltpu.roll(hp, rows - (SUBLANES - j), 0))
        xc = xc + cw_ref[CONV_W - 1 - j:CONV_W - j, :] * shifted
    a, b = _lru_coeffs(xc, wa_ref, ba_ref, wx_ref, bx_ref, lam_ref, None)
    a, b = _scan8(a, b)
    h0 = jnp.broadcast_to(h0_ref[...][:, None, :], (seqs, SUBLANES, D_RNN)).reshape(rows, D_RNN)
    h = b + a * h0
    last = jnp.where(r8 == SUBLANES - 1, h, 0.0).reshape(seqs, SUBLANES, D_RNN)
    nh_ref[...] = jnp.sum(last, axis=1)
    o_ref[...] = (h * gy_ref[...]).astype(BF16)


def _rnn_sample(xr_all, gy_all, hist_pad, h0, rnn_w, row0, dec_batch):
    seqs = _pick_tile(dec_batch, (16, 8))
    rows = seqs * SUBLANES
    assert row0 % rows == 0
    blk0 = row0 // rows
    tok = pl.BlockSpec((rows, D_RNN), lambda i: (blk0 + i, 0))
    return pl.pallas_call(
        functools.partial(_rnn_sample_kernel, seqs=seqs),
        grid=(dec_batch // seqs,),
        in_specs=[tok, tok, pl.BlockSpec((rows, D_RNN), lambda i: (i, 0)), pl.BlockSpec((seqs, D_RNN), lambda i: (i, 0))]
        + [_const_spec(w.shape) for w in rnn_w],
        out_specs=(pl.BlockSpec((rows, D_RNN), lambda i: (i, 0)), pl.BlockSpec((seqs, D_RNN), lambda i: (i, 0))),
        out_shape=(jax.ShapeDtypeStruct((dec_batch * SUBLANES, D_RNN), BF16),
                   jax.ShapeDtypeStruct((dec_batch, D_RNN), F32)),
        compiler_params=_params(("parallel",)),
        name="rnn_sample",
    )(xr_all, gy_all, hist_pad, h0, *rnn_w)


def _layer_norm(z, g, b):
    mu = jnp.mean(z, axis=-1, keepdims=True)
    zc = z - mu
    var = jnp.mean(zc * zc, axis=-1, keepdims=True)
    return zc * lax.rsqrt(var + LN_EPS) * g + b


def _first_index_of_max(vals, iota, axis, sentinel):
    mx = jnp.max(vals, axis=axis, keepdims=True)
    return mx, jnp.min(jnp.where(vals == mx, iota, sentinel), axis=axis, keepdims=True)


def _route(scores, bias):
    t = scores.shape[1]
    grp = scores + bias
    g3 = grp.reshape(N_GROUPS, GROUP_SIZE, t)
    e_in_g = lax.broadcasted_iota(jnp.int32, g3.shape, 1)
    m1, first = _first_index_of_max(g3, e_in_g, 1, GROUP_SIZE)
    m2 = jnp.max(jnp.where(e_in_g == first, -jnp.inf, g3), axis=1, keepdims=True)
    gscore = (m1 + m2).reshape(N_GROUPS, t)
    g_iota = lax.broadcasted_iota(jnp.int32, gscore.shape, 0)
    gmask = jnp.zeros(gscore.shape, jnp.bool_)
    for _ in range(TOPK_GROUPS):
        _, gi = _first_index_of_max(gscore, g_iota, 0, N_GROUPS)
        hit = g_iota == gi
        gmask = gmask | hit
        gscore = jnp.where(hit, -jnp.inf, gscore)
    masked = jnp.where(gmask[:, None, :], g3, -jnp.inf).reshape(N_EXPERTS, t)
    e_iota = lax.broadcasted_iota(jnp.int32, masked.shape, 0)
    idx, wts, hits = [], [], []
    for _ in range(TOP_K):
        _, ei = _first_index_of_max(masked, e_iota, 0, N_EXPERTS)
        hit = e_iota == ei
        idx.append(ei)
        hits.append(hit)
        wts.append(jnp.sum(jnp.where(hit, scores, 0.0), axis=0, keepdims=True))
        masked = jnp.where(hit, -jnp.inf, masked)
    idx = jnp.concatenate(idx, axis=0)
    w = jnp.concatenate(wts, axis=0)
    w = w / jnp.sum(w, axis=0, keepdims=True) * ROUTED_SCALE
    return idx, w, hits


def _merge_kernel(xp_ref, xs_ref, aop_ref, aos_ref, rop_ref, ros_ref, sga_ref, sgr_ref, woa_ref, wor_ref, wout_ref, g1_ref, b1_ref,
                  wrt_ref, rb_ref, wsg_ref, wsu_ref, wsd_ref,
                  x1w_ref, base_ref, idx_ref, wt_ref, rank_ref, cnt_ref, carry_ref, *, n_p, tm):
    i = pl.program_id(0)

    @pl.when(i == 0)
    def _():
        carry_ref[...] = jnp.zeros(carry_ref.shape, F32)

    is_prompt = i < n_p
    x = jnp.where(is_prompt, xp_ref[...], xs_ref[...])
    pa = jnp.dot(jnp.where(is_prompt, aop_ref[...], aos_ref[...]), woa_ref[...], preferred_element_type=F32)
    pr = jnp.dot(jnp.where(is_prompt, rop_ref[...], ros_ref[...]), wor_ref[...], preferred_element_type=F32)
    merged = sga_ref[...] * pa + sgr_ref[...] * pr
    z = ALPHA * x + jnp.dot(merged.astype(BF16), wout_ref[...], preferred_element_type=F32)
    x1 = _layer_norm(z, g1_ref[...], b1_ref[...])
    words = _pack_words(x1)
    for h in range(tm // SC_ROWS):
        for s in range(ROW_TILE):
            x1w_ref[(h * ROW_TILE + s) * SC_ROWS:(h * ROW_TILE + s + 1) * SC_ROWS, :] = (
                words[h * SC_ROWS:(h + 1) * SC_ROWS, s * LANES:(s + 1) * LANES])
    x1b = x1.astype(BF16)
    u = jax.nn.silu(jnp.dot(x1b, wsg_ref[...], preferred_element_type=F32)) * jnp.dot(x1b, wsu_ref[...], preferred_element_type=F32)
    shared = jnp.dot(u.astype(BF16), wsd_ref[...], preferred_element_type=F32)
    base_ref[...] = ALPHA * x1 + shared
    logits = lax.dot_general(wrt_ref[...], x1, (((1,), (1,)), ((), ())), preferred_element_type=F32)
    idx, w, hits = _route(jax.nn.sigmoid(logits), rb_ref[...])
    idx_ref[...] = idx
    wt_ref[...] = w

    chosen = functools.reduce(jnp.logical_or, hits)
    chosen_f = jnp.where(chosen, 1.0, 0.0)
    earlier = (lax.broadcasted_iota(jnp.int32, (tm, tm), 0) < lax.broadcasted_iota(jnp.int32, (tm, tm), 1))
    prefix = jnp.dot(chosen_f.astype(BF16), jnp.where(earlier, 1.0, 0.0).astype(BF16), preferred_element_type=F32)
    before = prefix + carry_ref[...]
    ranks = [jnp.sum(jnp.where(hit, before, 0.0), axis=0, keepdims=True) for hit in hits]
    rank_ref[...] = jnp.concatenate(ranks, axis=0).astype(jnp.int32)
    carry_ref[...] = carry_ref[...] + jnp.sum(chosen_f, axis=1, keepdims=True)
    cnt_ref[...] = carry_ref[...]


def _merge(x_p, x_s, attn_p, attn_s, rnn_p, rnn_s, sga, sgr, weights):
    t_p, t_s = x_p.shape[0], x_s.shape[0]
    t = t_p + t_s
    tm = _pick_tile(math.gcd(t_p, t_s), (256, 128))
    row = lambda width: pl.BlockSpec((tm, width), lambda i: (i, 0))
    col = pl.BlockSpec((TOP_K, tm), lambda i: (0, i))

    return pl.pallas_call(
        functools.partial(_merge_kernel, n_p=t_p // tm, tm=tm),
        grid=(t // tm,),
        in_specs=[*_two_source_specs(t_p, t_s, tm, D_MODEL), *_two_source_specs(t_p, t_s, tm, D_ATTN),
                  *_two_source_specs(t_p, t_s, tm, D_RNN), row(D_MODEL), row(D_MODEL)]
        + [_const_spec(w.shape) for w in weights],
        out_specs=(pl.BlockSpec((tm * ROW_TILE, LANES), lambda i: (i, 0)), row(D_MODEL), col, col, col,
                   _const_spec((N_EXPERTS, 1))),
        out_shape=(
            jax.ShapeDtypeStruct((t * ROW_TILE, LANES), WORD),
            jax.ShapeDtypeStruct((t, D_MODEL), F32),
            jax.ShapeDtypeStruct((TOP_K, t), jnp.int32),
            jax.ShapeDtypeStruct((TOP_K, t), F32),
            jax.ShapeDtypeStruct((TOP_K, t), jnp.int32),
            jax.ShapeDtypeStruct((N_EXPERTS, 1), F32),
        ),
        scratch_shapes=[pltpu.VMEM((N_EXPERTS, 1), F32)],
        compiler_params=_params(("arbitrary",)),
        name="merge_ln1_route",
    )(x_p, x_s, attn_p, attn_s, rnn_p, rnn_s, sga, sgr, *weights)


def _expert_layout(counts, n_assign):
    counts = counts.reshape(N_EXPERTS).astype(jnp.int32)
    padded = (counts + ROW_PAD - 1) // ROW_PAD * ROW_PAD
    pend = jnp.cumsum(padded)
    pstart = pend - padded
    rows_alloc = (n_assign + N_EXPERTS * (ROW_PAD - 1)) // ROW_PAD * ROW_PAD + EXPERT_CHUNK
    n_chunks = jnp.maximum((counts + EXPERT_CHUNK - 1) // EXPERT_CHUNK, 1)
    ffn_end = jnp.max(pstart + n_chunks * EXPERT_CHUNK).reshape(1)
    return pstart, pstart + counts, pend, n_chunks, counts, ffn_end, rows_alloc


def _dest_kernel(idx_ref, rank_ref, pstart_ref, dest_ref, word_rows_ref):
    e_iota = lax.broadcasted_iota(jnp.int32, (N_EXPERTS, idx_ref.shape[1]), 0)
    starts = [jnp.sum(jnp.where(e_iota == idx_ref[k:k + 1, :], pstart_ref[...], 0), axis=0, keepdims=True)
              for k in range(TOP_K)]
    dest = jnp.concatenate(starts, axis=0) + rank_ref[...]
    dest_ref[...] = dest
    tm = dest.shape[1]
    word_rows_ref[0] = jnp.concatenate(
        [dest[k:k + 1, h * SC_ROWS:(h + 1) * SC_ROWS] * ROW_TILE + s
         for h in range(tm // SC_ROWS) for s in range(ROW_TILE) for k in range(TOP_K)], axis=0)


def _dest_rows(idx_t, rank_t, pstart, tm):
    t = idx_t.shape[1]
    col = pl.BlockSpec((TOP_K, tm), lambda i: (0, i))
    vecs = tm // SC_ROWS * ROW_TILE * TOP_K
    return pl.pallas_call(
        _dest_kernel,
        grid=(t // tm,),
        in_specs=[col, col, _const_spec((N_EXPERTS, 1))],
        out_specs=(col, pl.BlockSpec((1, vecs, SC_ROWS), lambda i: (i, 0, 0))),
        out_shape=(jax.ShapeDtypeStruct((TOP_K, t), jnp.int32),
                   jax.ShapeDtypeStruct((t // tm, vecs, SC_ROWS), jnp.int32)),
        compiler_params=_params(("parallel",)),
        name="dest_rows",
    )(idx_t, rank_t, pstart.reshape(N_EXPERTS, 1))


def _token_rows(r, n=1):
    return pl.ds(pl.multiple_of(r * ROW_TILE, ROW_TILE), n * ROW_TILE)


def _zero_row_groups(zero_ref, dst_ref, sem, first_group, n_groups):
    def start(g, c):
        pltpu.make_async_copy(zero_ref, dst_ref.at[_token_rows(g * ROW_PAD, ROW_PAD)], sem).start()
        return c

    lax.fori_loop(first_group, n_groups, start, 0)
    return n_groups - first_group


def _wait_zero_copies(zero_ref, dst_ref, sem, n):
    def wait(_, c):
        pltpu.make_async_copy(zero_ref, dst_ref.at[_token_rows(0, ROW_PAD)], sem).wait()
        return c

    lax.fori_loop(0, n, wait, 0)


SC_ROWS = 128


def _sc_workers():
    info = plsc.get_sparse_core_info()
    mesh = plsc.VectorSubcoreMesh(core_axis_name="c", subcore_axis_name="s")
    worker = lambda: lax.axis_index("s") * info.num_cores + lax.axis_index("c")
    return mesh, info.num_cores * info.num_subcores, worker


def _sc_scatter_rows(src, rows, n_out):
    mesh, n_workers, worker = _sc_workers()
    n_units = src.shape[0] // SC_ROWS
    assert n_units % n_workers == 0 and rows.shape == (n_units * TOP_K, SC_ROWS)
    per_worker = n_units // n_workers

    @functools.partial(
        pl.kernel, mesh=mesh,
        out_type=jax.ShapeDtypeStruct((n_out, LANES), src.dtype),
        scratch_types=[pltpu.VMEM((TOP_K, SC_ROWS), jnp.int32), pltpu.VMEM((SC_ROWS, LANES), src.dtype),
                       pltpu.SemaphoreType.DMA],
    )
    def scatter(src_hbm, rows_hbm, out_hbm, rows_v, data_v, sem):
        first = worker() * per_worker

        @pl.loop(0, per_worker)
        def _(it):
            u = first + it
            pltpu.sync_copy(rows_hbm.at[pl.ds(u * TOP_K, TOP_K)], rows_v)
            pltpu.sync_copy(src_hbm.at[pl.ds(u * SC_ROWS, SC_ROWS)], data_v)
            copies = [pltpu.async_copy(data_v, out_hbm.at[rows_v.at[k]], sem) for k in range(TOP_K)]
            for cp in copies:
                cp.wait()

    return scatter(src, rows)


def _zero_padding_kernel(lo_ref, hi_ref, tail_ref, xs_in_ref, xs_ref, zero_ref, sem, *, rows_alloc):
    del xs_in_ref
    zero_ref[...] = jnp.zeros(zero_ref.shape, WORD)
    row_copy = lambda r: pltpu.make_async_copy(zero_ref.at[pl.ds(0, ROW_TILE)], xs_ref.at[_token_rows(r)], sem)

    def expert(e, n):
        def row(r, c):
            row_copy(r).start()
            return c

        lax.fori_loop(lo_ref[e], hi_ref[e], row, 0)
        return n + hi_ref[e] - lo_ref[e]

    def wait(_, c):
        row_copy(0).wait()
        return c

    lax.fori_loop(0, lax.fori_loop(0, N_EXPERTS, expert, 0), wait, 0)
    n = _zero_row_groups(zero_ref, xs_ref, sem, tail_ref[0] // ROW_PAD, rows_alloc // ROW_PAD)
    _wait_zero_copies(zero_ref, xs_ref, sem, n)


def _zero_padding(xs, pad_lo, pad_hi, total, rows_alloc):
    grid_spec = pltpu.PrefetchScalarGridSpec(
        num_scalar_prefetch=3,
        grid=(1,),
        in_specs=[pl.BlockSpec(memory_space=pl.ANY)],
        out_specs=pl.BlockSpec(memory_space=pl.ANY),
        scratch_shapes=[pltpu.VMEM((ROW_PAD * ROW_TILE, LANES), WORD), pltpu.SemaphoreType.DMA],
    )
    return pl.pallas_call(
        functools.partial(_zero_padding_kernel, rows_alloc=rows_alloc),
        grid_spec=grid_spec,
        out_shape=jax.ShapeDtypeStruct(xs.shape, xs.dtype),
        input_output_aliases={3: 0},
        compiler_params=_params(("arbitrary",)),
        name="zero_padding",
    )(pad_lo, pad_hi, total, xs)


def _expert_kernel(pstart_ref, nch_ref, rows_ref, end_ref, wg_ref, wu_ref, wd_ref, xs_ref, o_ref,
                   xbuf_ref, obuf_ref, wgb_ref, wub_ref, wdb_ref, zero_ref, done_ref, in_sem, out_sem, *, rows_alloc):
    e = pl.program_id(0)
    n_e = pl.num_programs(0)
    start = pstart_ref[e]
    nch = nch_ref[e]
    buf_rows = EXPERT_CHUNK * ROW_TILE

    part = EXPERT_CHUNK // CHUNK_COPIES
    part_rows = part * ROW_TILE

    class _ChunkCopy:
        def __init__(self, copies):
            self.copies = copies

        def start(self):
            for cp in self.copies:
                cp.start()

        def wait(self):
            for cp in self.copies:
                cp.wait()

    def in_copy(row, slot):
        return _ChunkCopy([
            pltpu.make_async_copy(xs_ref.at[_token_rows(row + p * part, part)],
                                  xbuf_ref.at[pl.ds(slot * buf_rows + p * part_rows, part_rows)], in_sem.at[slot])
            for p in range(CHUNK_COPIES)])

    def out_copy(row, slot):
        return _ChunkCopy([
            pltpu.make_async_copy(obuf_ref.at[pl.ds(slot * buf_rows + p * part_rows, part_rows)],
                                  o_ref.at[_token_rows(row + p * part, part)], out_sem)
            for p in range(CHUNK_COPIES)])

    @pl.when(e == 0)
    def _():
        done_ref[0] = 0
        obuf_ref[...] = jnp.zeros(obuf_ref.shape, WORD)
        in_copy(start, 0).start()

    wgb_ref[...] = wg_ref[...].astype(BF16)
    wub_ref[...] = wu_ref[...].astype(BF16)
    wdb_ref[...] = wd_ref[...].astype(BF16)
    done = done_ref[0]

    def chunk(c, carry):
        g = done + c
        slot = g % 2
        row = start + c * EXPERT_CHUNK
        in_copy(row, slot).wait()
        last = c + 1 == nch
        next_row = jnp.where(last, pstart_ref[jnp.minimum(e + 1, n_e - 1)], row + EXPERT_CHUNK)

        @pl.when(jnp.logical_not(last & (e == n_e - 1)))
        def _():
            in_copy(next_row, 1 - slot).start()

        base = pl.multiple_of(slot * buf_rows, buf_rows)
        def sub_block(h):
            xb = jnp.concatenate(_load_token_rows(xbuf_ref, EXPERT_ROWS, row0=h * EXPERT_ROWS, base=base),
                                 axis=1).astype(BF16)
            gate = jnp.dot(xb, wgb_ref[...], preferred_element_type=F32)
            up = jnp.dot(xb, wub_ref[...], preferred_element_type=F32)
            act = (jax.nn.silu(gate) * up).astype(BF16)
            _store_token_rows(obuf_ref, jnp.dot(act, wdb_ref[...], preferred_element_type=F32),
                              EXPERT_ROWS, row0=h * EXPERT_ROWS, base=base)

        rows_left = jnp.maximum(rows_ref[e], 1) - c * EXPERT_CHUNK
        n_sub = EXPERT_CHUNK // EXPERT_ROWS
        needed = jnp.minimum((rows_left + EXPERT_ROWS - 1) // EXPERT_ROWS, n_sub)
        for count in range(1, n_sub + 1):
            @pl.when(needed == count)
            def _():
                for h in range(count):
                    sub_block(h)

        @pl.when(g > 0)
        def _():
            out_copy(0, 0).wait()

        out_copy(row, slot).start()
        return carry

    lax.fori_loop(0, nch, chunk, 0)
    done_ref[0] = done + nch

    @pl.when(e == n_e - 1)
    def _():
        out_copy(0, 0).wait()
        zero_ref[...] = jnp.zeros(zero_ref.shape, WORD)
        n = _zero_row_groups(zero_ref, o_ref, out_sem, end_ref[0] // ROW_PAD, rows_alloc // ROW_PAD)
        _wait_zero_copies(zero_ref, o_ref, out_sem, n)


def _expert_ffn(xs, pstart, n_chunks, rows, ffn_end, rows_alloc, w_e_gate, w_e_up, w_e_down):
    weight = lambda shape: pl.BlockSpec((None, *shape), lambda e, *_: (e, 0, 0))
    grid_spec = pltpu.PrefetchScalarGridSpec(
        num_scalar_prefetch=4,
        grid=(N_EXPERTS,),
        in_specs=[weight((D_MODEL, D_EXPERT)), weight((D_MODEL, D_EXPERT)), weight((D_EXPERT, D_MODEL)),
                  pl.BlockSpec(memory_space=pl.ANY)],
        out_specs=pl.BlockSpec(memory_space=pl.ANY),
        scratch_shapes=[
            pltpu.VMEM((2 * EXPERT_CHUNK * ROW_TILE, LANES), WORD),
            pltpu.VMEM((2 * EXPERT_CHUNK * ROW_TILE, LANES), WORD),
            pltpu.VMEM((D_MODEL, D_EXPERT), BF16), pltpu.VMEM((D_MODEL, D_EXPERT), BF16),
            pltpu.VMEM((D_EXPERT, D_MODEL), BF16),
            pltpu.VMEM((ROW_PAD * ROW_TILE, LANES), WORD),
            pltpu.SMEM((1,), jnp.int32),
            pltpu.SemaphoreType.DMA((2,)), pltpu.SemaphoreType.DMA,
        ],
    )
    return pl.pallas_call(
        functools.partial(_expert_kernel, rows_alloc=rows_alloc),
        grid_spec=grid_spec,
        out_shape=jax.ShapeDtypeStruct((rows_alloc * ROW_TILE, LANES), WORD),
        compiler_params=_params(("arbitrary",)),
        name="expert_ffn",
    )(pstart, n_chunks, rows, ffn_end, w_e_gate, w_e_up, w_e_down, xs)


def _combine_head_kernel(dest_ref, dest_next_ref, w_ref, base_ref, g_ref, b_ref, outs_ref, y_ref, buf_ref, sem, *, tm):
    i = pl.program_id(0)
    slot_rows = TOP_K * tm

    def gather(d_ref, slot):
        def issue(t, c):
            for k in range(TOP_K):
                pltpu.make_async_copy(outs_ref.at[_token_rows(d_ref[k, t])],
                                      buf_ref.at[_token_rows(slot * slot_rows + k * tm + t)], sem.at[slot]).start()
            return c

        lax.fori_loop(0, tm, issue, 0)

    @pl.when(i == 0)
    def _():
        gather(dest_ref, 0)

    @pl.when(i + 1 < pl.num_programs(0))
    def _():
        gather(dest_next_ref, (i + 1) % 2)

    slot = i % 2
    for k in range(TOP_K):
        pltpu.make_async_copy(outs_ref.at[_token_rows(0, tm)], buf_ref.at[_token_rows(slot * slot_rows + k * tm, tm)],
                              sem.at[slot]).wait()

    w = w_ref[...]
    base = pl.multiple_of(slot * slot_rows * ROW_TILE, ROW_TILE)
    halves = [None, None]
    for k in range(TOP_K):
        wk = w[:, k:k + 1]
        for j, rows in enumerate(_load_token_rows(buf_ref, tm, row0=k * tm, base=base)):
            halves[j] = wk * rows if halves[j] is None else halves[j] + wk * rows
    y_ref[...] = _layer_norm(base_ref[...] + jnp.concatenate(halves, axis=1), g_ref[...], b_ref[...])


def _combine_head(out_sorted, dest, w_tok, base, g2, b2, tm, n_head):
    row = pl.BlockSpec((tm, D_MODEL), lambda i: (i, 0))
    return pl.pallas_call(
        functools.partial(_combine_head_kernel, tm=tm),
        grid=(n_head,),
        in_specs=[
            pl.BlockSpec((TOP_K, tm), lambda i: (0, i), memory_space=pltpu.SMEM),
            pl.BlockSpec((TOP_K, tm), lambda i: (0, jnp.minimum(i + 1, n_head - 1)), memory_space=pltpu.SMEM),
            pl.BlockSpec((tm, TOP_K), lambda i: (i, 0)),
            row, _const_spec((1, D_MODEL)), _const_spec((1, D_MODEL)),
            pl.BlockSpec(memory_space=pl.ANY),
        ],
        out_specs=row,
        out_shape=jax.ShapeDtypeStruct((n_head * tm, D_MODEL), F32),
        scratch_shapes=[pltpu.VMEM((2 * TOP_K * tm * ROW_TILE, LANES), WORD), pltpu.SemaphoreType.DMA((2,))],
        compiler_params=_params(("arbitrary",)),
        name="combine_head",
    )(dest, dest, w_tok, base, g2, b2, out_sorted)


def _sc_gather_rows(table, rows):
    mesh, n_workers, worker = _sc_workers()
    m = rows.shape[0]
    per_worker = m // n_workers
    assert m % n_workers == 0 and per_worker % SC_ROWS == 0
    in_flight = _pick_tile(per_worker // SC_ROWS, (4, 2, 1))
    step = SC_ROWS * in_flight

    @functools.partial(
        pl.kernel, mesh=mesh,
        out_type=jax.ShapeDtypeStruct((m, LANES), table.dtype),
        scratch_types=[pltpu.VMEM((step,), jnp.int32), pltpu.VMEM((step, LANES), table.dtype), pltpu.SemaphoreType.DMA],
    )
    def gather(table_hbm, rows_hbm, out_hbm, rows_v, data_v, sem):
        first = worker() * per_worker

        @pl.loop(0, per_worker // step)
        def _(it):
            off = first + it * step
            pltpu.sync_copy(rows_hbm.at[pl.ds(off, step)], rows_v)
            copies = [pltpu.async_copy(table_hbm.at[rows_v.at[pl.ds(j * SC_ROWS, SC_ROWS)]],
                                       data_v.at[pl.ds(j * SC_ROWS, SC_ROWS)], sem)
                      for j in range(in_flight)]
            for cp in copies:
                cp.wait()
            pltpu.sync_copy(data_v, out_hbm.at[pl.ds(off, step)])

    return gather(table, rows)


def _combine_tail_kernel(yh_ref, rows_ref, w_ref, base_ref, g_ref, b_ref, yp_ref, ys_ref, *, tm, n_head, n_p):
    i = pl.program_id(0)

    @pl.when(i < n_head)
    def _():
        yp_ref[...] = yh_ref[...]

    @pl.when(i >= n_head)
    def _():
        w = w_ref[...]
        lo = [None] * ROW_TILE
        hi = [None] * ROW_TILE
        for k in range(TOP_K):
            wk = w[:, k:k + 1]
            for s in range(ROW_TILE):
                vec = lambda h: ((h * ROW_TILE + s) * TOP_K + k) * SC_ROWS
                words = jnp.concatenate([rows_ref[vec(h):vec(h) + SC_ROWS, :] for h in range(tm // SC_ROWS)], axis=0)
                lo_s, hi_s = _unpack_words(words)
                lo[s] = wk * lo_s if lo[s] is None else lo[s] + wk * lo_s
                hi[s] = wk * hi_s if hi[s] is None else hi[s] + wk * hi_s
        y = _layer_norm(base_ref[...] + jnp.concatenate(lo + hi, axis=1), g_ref[...], b_ref[...])

        @pl.when(i < n_p)
        def _():
            yp_ref[...] = y

        @pl.when(i >= n_p)
        def _():
            ys_ref[...] = y


def _combine_tail(y_head, gathered, w_tok, base, g2, b2, t_p, t_s, tm, n_head):
    n_p = t_p // tm
    assert n_head <= n_p
    n_tiles = (t_p + t_s) // tm
    blk = TOP_K * ROW_TILE * tm
    row = pl.BlockSpec((tm, D_MODEL), lambda i: (i, 0))
    out_p, out_s = _two_source_specs(t_p, t_s, tm, D_MODEL)
    return pl.pallas_call(
        functools.partial(_combine_tail_kernel, tm=tm, n_head=n_head, n_p=n_p),
        grid=(n_tiles,),
        in_specs=[
            pl.BlockSpec((tm, D_MODEL), lambda i: (jnp.minimum(i, n_head - 1), 0)),
            pl.BlockSpec((blk, LANES), lambda i: (jnp.maximum(i - n_head, 0), 0)),
            pl.BlockSpec((tm, TOP_K), lambda i: (i, 0)),
            row, _const_spec((1, D_MODEL)), _const_spec((1, D_MODEL)),
        ],
        out_specs=(out_p, out_s),
        out_shape=(jax.ShapeDtypeStruct((t_p, D_MODEL), F32), jax.ShapeDtypeStruct((t_s, D_MODEL), F32)),
        compiler_params=_params(("arbitrary",)),
        name="combine_tail",
    )(y_head, gathered, w_tok, base, g2, b2)


def kernel(x_prompt, x_sample, cache_k, cache_v, state_conv, state_rnn, w_in, conv_w, conv_b, w_gate_a, b_gate_a, w_gate_x, b_gate_x, lru_lambda, rel_bias, sinks, w_o_attn, w_o_rnn, w_out, ln1_g, ln1_b, w_router, router_bias, w_e_gate, w_e_up, w_e_down, w_s_gate, w_s_up, w_s_down, ln2_g, ln2_b):
    assert w_in.shape[0] == DEPTH == 1
    batch, seq, _ = x_prompt.shape
    dec_batch, s_len, _ = x_sample.shape
    w_cache = cache_k.shape[2]
    assert s_len == SUBLANES and seq % WINDOW == 0 and w_cache == WINDOW
    t_p = batch * seq
    t_s = dec_batch * s_len
    vec = lambda a: a[0].reshape(1, -1).astype(F32)

    x_p = x_prompt.reshape(t_p, D_MODEL)
    x_s = x_sample.reshape(t_s, D_MODEL)
    q, k, v, xr, gy, sga, sgr = _inproj(x_p, x_s, w_in[0].astype(BF16))

    qi = jnp.arange(WINDOW)
    dist = qi[:, None] + WINDOW - jnp.arange(2 * WINDOW)[None, :]
    bias_p = _bias_table(rel_bias, dist)
    attn_p = _attn_prompt(q, k, v, bias_p, sinks[0], batch, seq)
    attn_s, k_s, v_s = _attn_sample(
        q, k, v, cache_k[0].reshape(dec_batch, w_cache, D_KV), cache_v[0].reshape(dec_batch, w_cache, D_KV),
        rel_bias, sinks[0], t_p, dec_batch, s_len)

    rnn_w = (conv_w[0], vec(conv_b), w_gate_a[0].astype(BF16), vec(b_gate_a), w_gate_x[0].astype(BF16),
             vec(b_gate_x), vec(lru_lambda))
    rnn_p, h_p = _rnn_prompt(xr, gy, rnn_w, batch, seq)
    hist_pad = jnp.pad(state_conv[0], ((0, 0), (SUBLANES - (CONV_W - 1), 0), (0, 0))).reshape(t_s, D_RNN)
    rnn_s, h_s = _rnn_sample(xr, gy, hist_pad, state_rnn[0], rnn_w, t_p, dec_batch)

    merge_w = (w_o_attn[0].astype(BF16), w_o_rnn[0].astype(BF16), w_out[0].astype(BF16), vec(ln1_g), vec(ln1_b),
               w_router[0].T, router_bias[0].reshape(N_EXPERTS, 1), w_s_gate[0].astype(BF16),
               w_s_up[0].astype(BF16), w_s_down[0].astype(BF16))
    x1w, base, idx_t, wt_t, rank_t, counts = _merge(x_p, x_s, attn_p, attn_s, rnn_p, rnn_s, sga, sgr, merge_w)

    pstart, pad_lo, pad_hi, n_chunks, rows, ffn_end, rows_alloc = _expert_layout(counts, (t_p + t_s) * TOP_K)
    tm = _pick_tile(math.gcd(t_p, t_s), (256, 128))
    dest, word_rows = _dest_rows(idx_t, rank_t, pstart, tm)
    xs = _sc_scatter_rows(x1w, word_rows.reshape(-1, SC_ROWS), rows_alloc * ROW_TILE)
    xs = _zero_padding(xs, pad_lo, pad_hi, pad_hi[N_EXPERTS - 1:], rows_alloc)
    out_sorted = _expert_ffn(xs, pstart, n_chunks, rows, ffn_end, rows_alloc, w_e_gate[0], w_e_up[0], w_e_down[0])
    n_tiles = (t_p + t_s) // tm
    n_head = min(t_p // tm, n_tiles // 3)
    w_tok = wt_t.T
    g2, b2 = vec(ln2_g), vec(ln2_b)
    y_head = _combine_head(out_sorted, dest, w_tok, base, g2, b2, tm, n_head)
    gathered = _sc_gather_rows(out_sorted, word_rows[n_head:].reshape(-1))
    y_p, y_s = _combine_tail(y_head, gathered, w_tok, base, g2, b2, t_p, t_s, tm, n_head)
    y_p = y_p.reshape(batch, seq, D_MODEL)
    y_s = y_s.reshape(dec_batch, s_len, D_MODEL)
    kv5 = lambda a, b: a.reshape(1, b, WINDOW, N_KV_HEADS, HEAD_DIM)
    tail = lambda a, n: jnp.stack([lax.slice_in_dim(a, (b + 1) * seq - n, (b + 1) * seq) for b in range(batch)])
    k_p = kv5(tail(k, WINDOW), batch)
    v_p = kv5(tail(v, WINDOW), batch)
    conv_p = tail(xr, CONV_W - 1)[None]
    conv_s = xr[t_p:].reshape(dec_batch, s_len, D_RNN)[:, s_len - (CONV_W - 1):][None]
    return (y_p, y_s, k_p, v_p, conv_p, h_p.reshape(1, batch, D_RNN),
            kv5(k_s, dec_batch), kv5(v_s, dec_batch), conv_s, h_s.reshape(1, dec_batch, D_RNN))
```

```python
import functools
import math

import jax
import jax.numpy as jnp
from jax import lax
from jax.experimental import pallas as pl
from jax.experimental.pallas import tpu as pltpu
from jax.experimental.pallas import tpu_sc as plsc

F32 = jnp.float32
BF16 = jnp.bfloat16
WORD = jnp.int32

D_MODEL = 1024
N_HEADS = 8
N_KV_HEADS = 2
HEAD_DIM = 64
GROUP = N_HEADS // N_KV_HEADS
WINDOW = 128
D_ATTN = N_HEADS * HEAD_DIM
D_KV = N_KV_HEADS * HEAD_DIM
N_BUCKETS = 32
MAX_DISTANCE = 128
D_RNN = D_MODEL
RNN_BLOCK = 256
N_RNN_BLOCKS = D_RNN // RNN_BLOCK
CONV_W = 4
RG_C = 8.0
N_EXPERTS = 256
TOP_K = 8
N_GROUPS = 8
GROUP_SIZE = N_EXPERTS // N_GROUPS
TOPK_GROUPS = 4
D_EXPERT = D_MODEL // 4
ROUTED_SCALE = 2.5
LN_EPS = 1e-5
DEPTH = 1
ALPHA = (2 * DEPTH) ** 0.25
NEG_INF = -1e30
SM_SCALE = HEAD_DIM ** -0.5

O_Q = 0
O_K = D_ATTN
O_V = O_K + D_KV
O_XR = O_V + D_KV
O_YR = O_XR + D_RNN
O_GA = O_YR + D_RNN
O_GR = O_GA + D_MODEL
D_IN = O_GR + D_MODEL

SUBLANES = 8
VMEM_LIMIT_BYTES = 56 * 1024 * 1024
EXPERT_ROWS = 256
EXPERT_CHUNK = 512
CHUNK_COPIES = 4
ROW_PAD = SUBLANES


def _params(sem):
    return pltpu.CompilerParams(dimension_semantics=sem, vmem_limit_bytes=VMEM_LIMIT_BYTES)


def _pick_tile(n, candidates):
    for c in candidates:
        if n % c == 0:
            return c
    raise ValueError(f"no tile for {n}")


def _const_spec(shape):
    nd = len(shape)
    return pl.BlockSpec(shape, lambda *_: (0,) * nd)


LANES = 128
ROW_WORDS = D_MODEL // 2
ROW_TILE = ROW_WORDS // LANES
HIGH_HALF = -65536


def _pack_words(mat):
    as_bits = lambda v: pltpu.bitcast(v.astype(BF16).astype(F32), WORD)
    return (as_bits(mat[:, ROW_WORDS:]) & HIGH_HALF) | lax.shift_right_logical(as_bits(mat[:, :ROW_WORDS]), 16)


def _unpack_words(words):
    return pltpu.bitcast(words << 16, F32), pltpu.bitcast(words & HIGH_HALF, F32)


def _store_token_rows(ref, mat, n, row0=0, base=0):
    words = _pack_words(mat)
    for s in range(ROW_TILE):
        ref[pl.ds(base + row0 * ROW_TILE + s, n, stride=ROW_TILE), :] = words[:, s * LANES:(s + 1) * LANES]


def _load_token_rows(ref, n, row0=0, base=0):
    words = jnp.concatenate(
        [ref[pl.ds(base + row0 * ROW_TILE + s, n, stride=ROW_TILE), :] for s in range(ROW_TILE)], axis=1)
    return _unpack_words(words)


def _two_source_specs(t_p, t_s, tm, width):
    n_p = t_p // tm
    assert t_p % tm == 0 and t_s % tm == 0
    return (pl.BlockSpec((tm, width), lambda i, *_: (jnp.minimum(i, n_p - 1), 0)),
            pl.BlockSpec((tm, width), lambda i, *_: (jnp.maximum(i - n_p, 0), 0)))


def _inproj_kernel(xp_ref, xs_ref, w_ref, q_ref, k_ref, v_ref, xr_ref, gy_ref, sga_ref, sgr_ref, *, n_p):
    x = jnp.where(pl.program_id(0) < n_p, xp_ref[...], xs_ref[...]).astype(BF16)

    def seg(lo, hi):
        return jnp.dot(x, w_ref[:, lo:hi], preferred_element_type=F32)

    q_ref[...] = seg(O_Q, O_K).astype(BF16)
    k_ref[...] = seg(O_K, O_V)
    v_ref[...] = seg(O_V, O_XR)
    xr_ref[...] = seg(O_XR, O_YR)
    gy_ref[...] = jax.nn.gelu(seg(O_YR, O_GA)).astype(BF16)
    sga_ref[...] = jax.nn.sigmoid(seg(O_GA, O_GR)).astype(BF16)
    sgr_ref[...] = jax.nn.sigmoid(seg(O_GR, D_IN)).astype(BF16)


def _inproj(x_p, x_s, w_in_bf16):
    t_p, t_s = x_p.shape[0], x_s.shape[0]
    t = t_p + t_s
    tm = _pick_tile(math.gcd(t_p, t_s), (256, 128, 64, 32, 16, 8))
    row = lambda width: pl.BlockSpec((tm, width), lambda i: (i, 0))
    out_shape = (
        jax.ShapeDtypeStruct((t, D_ATTN), BF16),
        jax.ShapeDtypeStruct((t, D_KV), F32),
        jax.ShapeDtypeStruct((t, D_KV), F32),
        jax.ShapeDtypeStruct((t, D_RNN), F32),
        jax.ShapeDtypeStruct((t, D_RNN), BF16),
        jax.ShapeDtypeStruct((t, D_MODEL), BF16),
        jax.ShapeDtypeStruct((t, D_MODEL), BF16),
    )
    return pl.pallas_call(
        functools.partial(_inproj_kernel, n_p=t_p // tm),
        grid=(t // tm,),
        in_specs=[*_two_source_specs(t_p, t_s, tm, D_MODEL), _const_spec((D_MODEL, D_IN))],
        out_specs=(row(D_ATTN), row(D_KV), row(D_KV), row(D_RNN), row(D_RNN), row(D_MODEL), row(D_MODEL)),
        out_shape=out_shape,
        compiler_params=_params(("parallel",)),
        name="inproj",
    )(x_p, x_s, w_in_bf16)


def _t5_bucket(dist):
    n = jnp.maximum(dist, 0)
    max_exact = N_BUCKETS // 2
    nf = jnp.maximum(n, 1).astype(F32)
    large = max_exact + (jnp.log(nf / max_exact) / math.log(MAX_DISTANCE / max_exact) * (N_BUCKETS - max_exact)).astype(jnp.int32)
    large = jnp.minimum(large, N_BUCKETS - 1)
    return jnp.where(n < max_exact, n, large)


def _bias_table(rel_bias, dist):
    bucket = _t5_bucket(dist)
    rb = rel_bias.astype(F32)
    out = jnp.zeros((N_HEADS, *dist.shape), F32)
    for j in range(N_BUCKETS):
        out = jnp.where(bucket[None] == j, rb[j][:, None, None], out)
    return out


def _softmax_pv(s, sink, v):
    m = jnp.maximum(jnp.max(s, axis=-1, keepdims=True), sink)
    p = jnp.exp(s - m)
    denom = jnp.sum(p, axis=-1, keepdims=True) + jnp.exp(sink - m)
    return jnp.dot(p.astype(BF16), v, preferred_element_type=F32), denom


def _attn_prompt_kernel(sink_ref, q_ref, kc_ref, kp_ref, vc_ref, vp_ref, bias_ref, o_ref):
    n = pl.program_id(1)
    kk = jnp.concatenate([kp_ref[...], kc_ref[...]], axis=0).astype(BF16)
    vv = jnp.concatenate([vp_ref[...], vc_ref[...]], axis=0).astype(BF16)
    q = q_ref[...]
    rows = lax.broadcasted_iota(jnp.int32, (WINDOW, 2 * WINDOW), 0)
    cols = lax.broadcasted_iota(jnp.int32, (WINDOW, 2 * WINDOW), 1)
    dist = rows + WINDOW - cols
    valid = (dist >= 0) & (dist <= WINDOW) & ((n > 0) | (cols >= WINDOW))
    for g in range(N_KV_HEADS):
        kg = kk[:, g * HEAD_DIM:(g + 1) * HEAD_DIM]
        vg = vv[:, g * HEAD_DIM:(g + 1) * HEAD_DIM]
        for h in range(GROUP):
            hh = g * GROUP + h
            qh = q[:, hh * HEAD_DIM:(hh + 1) * HEAD_DIM]
            s = lax.dot_general(qh, kg, (((1,), (1,)), ((), ())), preferred_element_type=F32) * SM_SCALE
            s = jnp.where(valid, s + bias_ref[hh], NEG_INF)
            o, denom = _softmax_pv(s, sink_ref[0, hh], vg)
            o_ref[:, hh * HEAD_DIM:(hh + 1) * HEAD_DIM] = (o / denom).astype(BF16)


def _attn_prompt(q_all, k_all, v_all, bias, sinks, batch, seq):
    nb = seq // WINDOW
    cur = lambda width: pl.BlockSpec((WINDOW, width), lambda b, n: (b * nb + n, 0))
    prev = lambda width: pl.BlockSpec((WINDOW, width), lambda b, n: (b * nb + jnp.maximum(n - 1, 0), 0))
    return pl.pallas_call(
        _attn_prompt_kernel,
        grid=(batch, nb),
        in_specs=[
            pl.BlockSpec(memory_space=pltpu.SMEM),
            cur(D_ATTN), cur(D_KV), prev(D_KV), cur(D_KV), prev(D_KV),
            _const_spec((N_HEADS, WINDOW, 2 * WINDOW)),
        ],
        out_specs=cur(D_ATTN),
        out_shape=jax.ShapeDtypeStruct((batch * seq, D_ATTN), BF16),
        compiler_params=_params(("parallel", "arbitrary")),
        name="attn_prompt",
    )(sinks.reshape(1, N_HEADS).astype(F32), q_all, k_all, k_all, v_all, v_all, bias)


def _attn_sample_kernel(q_ref, kn_ref, vn_ref, kc_ref, vc_ref, bc_ref, bn_ref, sink_ref,
                        o_ref, ko_ref, vo_ref, *, seqs, s_len):
    w = kc_ref.shape[1]
    rows_c = lax.broadcasted_iota(jnp.int32, (GROUP * s_len, w), 0) % s_len
    cols_c = lax.broadcasted_iota(jnp.int32, (GROUP * s_len, w), 1)
    dist_c = rows_c + w - cols_c
    valid_c = (dist_c >= 0) & (dist_c <= WINDOW)
    rows_n = lax.broadcasted_iota(jnp.int32, (GROUP * s_len, s_len), 0) % s_len
    cols_n = lax.broadcasted_iota(jnp.int32, (GROUP * s_len, s_len), 1)
    dist_n = rows_n - cols_n
    valid_n = (dist_n >= 0) & (dist_n <= WINDOW)
    for j in range(seqs):
        r0 = j * s_len
        qj = q_ref[r0:r0 + s_len, :]
        kc = kc_ref[j]
        vc = vc_ref[j]
        kn = kn_ref[r0:r0 + s_len, :]
        vn = vn_ref[r0:r0 + s_len, :]
        ko_ref[j, 0:w - s_len, :] = kc[s_len:, :]
        ko_ref[j, w - s_len:w, :] = kn
        vo_ref[j, 0:w - s_len, :] = vc[s_len:, :]
        vo_ref[j, w - s_len:w, :] = vn
        kcb, vcb, knb, vnb = kc.astype(BF16), vc.astype(BF16), kn.astype(BF16), vn.astype(BF16)
        for g in range(N_KV_HEADS):
            lo, hi = g * HEAD_DIM, (g + 1) * HEAD_DIM
            qs = jnp.concatenate(
                [qj[:, (g * GROUP + h) * HEAD_DIM:(g * GROUP + h + 1) * HEAD_DIM] for h in range(GROUP)], axis=0)
            nt = (((1,), (1,)), ((), ()))
            s_c = lax.dot_general(qs, kcb[:, lo:hi], nt, preferred_element_type=F32) * SM_SCALE
            s_n = lax.dot_general(qs, knb[:, lo:hi], nt, preferred_element_type=F32) * SM_SCALE
            s_c = jnp.where(valid_c, s_c + bc_ref[g], NEG_INF)
            s_n = jnp.where(valid_n, s_n + bn_ref[g], NEG_INF)
            sink = sink_ref[g]
            m = jnp.maximum(jnp.maximum(jnp.max(s_c, axis=-1, keepdims=True), jnp.max(s_n, axis=-1, keepdims=True)), sink)
            p_c = jnp.exp(s_c - m)
            p_n = jnp.exp(s_n - m)
            denom = jnp.sum(p_c, axis=-1, keepdims=True) + jnp.sum(p_n, axis=-1, keepdims=True) + jnp.exp(sink - m)
            o = jnp.dot(p_c.astype(BF16), vcb[:, lo:hi], preferred_element_type=F32)
            o = o + jnp.dot(p_n.astype(BF16), vnb[:, lo:hi], preferred_element_type=F32)
            o = (o / denom).astype(BF16)
            for h in range(GROUP):
                hh = g * GROUP + h
                o_ref[r0:r0 + s_len, hh * HEAD_DIM:(hh + 1) * HEAD_DIM] = o[h * s_len:(h + 1) * s_len, :]


def _attn_sample(q_all, k_all, v_all, cache_k, cache_v, rel_bias, sinks, row0, dec_batch, s_len):
    w = cache_k.shape[1]
    seqs = _pick_tile(dec_batch, (16, 8, 4, 2, 1))
    rows = seqs * s_len
    blk0 = row0 // rows
    assert row0 % rows == 0
    qi = jnp.arange(s_len)
    dist_c = qi[:, None] + w - jnp.arange(w)[None, :]
    dist_n = qi[:, None] - jnp.arange(s_len)[None, :]
    b_c = _bias_table(rel_bias, dist_c).reshape(N_KV_HEADS, GROUP * s_len, w)
    b_n = _bias_table(rel_bias, dist_n).reshape(N_KV_HEADS, GROUP * s_len, s_len)
    sink = jnp.broadcast_to(sinks.astype(F32).reshape(N_KV_HEADS, GROUP, 1, 1), (N_KV_HEADS, GROUP, s_len, 1))
    sink = sink.reshape(N_KV_HEADS, GROUP * s_len, 1)
    tok = lambda width: pl.BlockSpec((rows, width), lambda i: (blk0 + i, 0))
    cache = pl.BlockSpec((seqs, w, D_KV), lambda i: (i, 0, 0))
    return pl.pallas_call(
        functools.partial(_attn_sample_kernel, seqs=seqs, s_len=s_len),
        grid=(dec_batch // seqs,),
        in_specs=[
            tok(D_ATTN), tok(D_KV), tok(D_KV), cache, cache,
            _const_spec(b_c.shape), _const_spec(b_n.shape), _const_spec(sink.shape),
        ],
        out_specs=(pl.BlockSpec((rows, D_ATTN), lambda i: (i, 0)), cache, cache),
        out_shape=(
            jax.ShapeDtypeStruct((dec_batch * s_len, D_ATTN), BF16),
            jax.ShapeDtypeStruct((dec_batch, w, D_KV), F32),
            jax.ShapeDtypeStruct((dec_batch, w, D_KV), F32),
        ),
        compiler_params=_params(("parallel",)),
        name="attn_sample",
    )(q_all, k_all, v_all, cache_k, cache_v, b_c, b_n, sink)


def _softplus(z):
    return jnp.maximum(z, 0.0) + jnp.log1p(jnp.exp(-jnp.abs(z)))


def _block_gate(xcb, w_ref, b_ref):
    parts = [jnp.dot(xcb[:, n * RNN_BLOCK:(n + 1) * RNN_BLOCK], w_ref[n], preferred_element_type=F32)
             for n in range(N_RNN_BLOCKS)]
    return jax.nn.sigmoid(jnp.concatenate(parts, axis=-1) + b_ref[...])


def _lru_coeffs(xc, wa_ref, ba_ref, wx_ref, bx_ref, lam_ref, first_row_unnormalised):
    xcb = xc.astype(BF16)
    r = _block_gate(xcb, wa_ref, ba_ref)
    i = _block_gate(xcb, wx_ref, bx_ref)
    log_a = -RG_C * r * _softplus(-lam_ref[...])
    a = jnp.exp(log_a)
    mult = jnp.sqrt(-jnp.tanh(log_a) * (a * a + 1.0))
    if first_row_unnormalised is not None:
        mult = jnp.where(first_row_unnormalised, 1.0, mult)
    return a, mult * i * xc


def _scan8(a, b):
    shape = a.shape
    grouped = (shape[0] // SUBLANES, SUBLANES, shape[1])
    a = a.reshape(grouped)
    b = b.reshape(grouped)
    r8 = lax.broadcasted_iota(jnp.int32, grouped, 1)
    d = 1
    while d < SUBLANES:
        keep = r8 >= d
        a_sh = jnp.where(keep, pltpu.roll(a, d, 1), 1.0)
        b_sh = jnp.where(keep, pltpu.roll(b, d, 1), 0.0)
        b = a * b_sh + b
        a = a * a_sh
        d *= 2
    return a.reshape(shape), b.reshape(shape)


def _rnn_prompt_kernel(xr_ref, gy_ref, cw_ref, cb_ref, wa_ref, ba_ref, wx_ref, bx_ref, lam_ref,
                       o_ref, nh_ref, ext_ref, a_ref, b_ref, hc_ref, *, tl):
    l = pl.program_id(1)

    @pl.when(l == 0)
    def _():
        ext_ref[0:SUBLANES, :] = jnp.zeros((SUBLANES, D_RNN), F32)
        hc_ref[...] = jnp.zeros((1, D_RNN), F32)

    x = xr_ref[...]
    ext_ref[SUBLANES:, :] = x
    xc = cb_ref[...] + cw_ref[CONV_W - 1:CONV_W, :] * x
    for j in range(1, CONV_W):
        xc = xc + cw_ref[CONV_W - 1 - j:CONV_W - j, :] * ext_ref[SUBLANES - j:SUBLANES - j + tl, :]
    ext_ref[0:SUBLANES, :] = x[tl - SUBLANES:, :]

    row = lax.broadcasted_iota(jnp.int32, (tl, D_RNN), 0)
    a, b = _lru_coeffs(xc, wa_ref, ba_ref, wx_ref, bx_ref, lam_ref, (row == 0) & (l == 0))
    a, b = _scan8(a, b)
    a_ref[...] = a
    b_ref[...] = b

    def chunk(c, h):
        sl = pl.ds(pl.multiple_of(c * SUBLANES, SUBLANES), SUBLANES)
        hc = b_ref[sl, :] + a_ref[sl, :] * h
        b_ref[sl, :] = hc
        return hc[SUBLANES - 1:SUBLANES, :]

    h = lax.fori_loop(0, tl // SUBLANES, chunk, hc_ref[...])
    hc_ref[...] = h
    nh_ref[0] = h
    o_ref[...] = (b_ref[...] * gy_ref[...]).astype(BF16)


def _rnn_prompt(xr_all, gy_all, rnn_w, batch, seq):
    tl =_pick_tile(seq, (256, 128, 64, 32, 16, 8))
    nl = seq // tl
    tok = pl.BlockSpec((tl, D_RNN), lambda b, l: (b * nl + l, 0))
    return pl.pallas_call(
        functools.partial(_rnn_prompt_kernel, tl=tl),
        grid=(batch, nl),
        in_specs=[tok, tok] + [_const_spec(w.shape) for w in rnn_w],
        out_specs=(tok, pl.BlockSpec((1, 1, D_RNN), lambda b, l: (b, 0, 0))),
        out_shape=(jax.ShapeDtypeStruct((batch * seq, D_RNN), BF16), jax.ShapeDtypeStruct((batch, 1, D_RNN), F32)),
        scratch_shapes=[
            pltpu.VMEM((tl + SUBLANES, D_RNN), F32),
            pltpu.VMEM((tl, D_RNN), F32),
            pltpu.VMEM((tl, D_RNN), F32),
            pltpu.VMEM((1, D_RNN), F32),
        ],
        compiler_params=_params(("parallel", "arbitrary")),
        name="rnn_prompt",
    )(xr_all, gy_all, *rnn_w)


def _rnn_sample_kernel(xr_ref, gy_ref, hp_ref, h0_ref, cw_ref, cb_ref, wa_ref, ba_ref, wx_ref, bx_ref, lam_ref,
                       o_ref, nh_ref, *, seqs):
    rows = seqs * SUBLANES
    x = xr_ref[...]
    hp = hp_ref[...]
    r8 = lax.broadcasted_iota(jnp.int32, (rows, D_RNN), 0) % SUBLANES
    xc = cb_ref[...] + cw_ref[CONV_W - 1:CONV_W, :] * x
    for j in range(1, CONV_W):
        shifted = jnp.where(r8 >= j, pltpu.roll(x, j, 0), pltpu.roll(hp, rows - (SUBLANES - j), 0))
        xc = xc + cw_ref[CONV_W - 1 - j:CONV_W - j, :] * shifted
    a, b = _lru_coeffs(xc, wa_ref, ba_ref, wx_ref, bx_ref, lam_ref, None)
    a, b = _scan8(a, b)
    h0 = jnp.broadcast_to(h0_ref[...][:, None, :], (seqs, SUBLANES, D_RNN)).reshape(rows, D_RNN)
    h = b + a * h0
    last = jnp.where(r8 == SUBLANES - 1, h, 0.0).reshape(seqs, SUBLANES, D_RNN)
    nh_ref[...] = jnp.sum(last, axis=1)
    o_ref[...] = (h * gy_ref[...]).astype(BF16)


def _rnn_sample(xr_all, gy_all, hist_pad, h0, rnn_w, row0, dec_batch):
    seqs = _pick_tile(dec_batch, (16, 8))
    rows = seqs * SUBLANES
    assert row0 % rows == 0
    blk0 = row0 // rows
    tok = pl.BlockSpec((rows, D_RNN), lambda i: (blk0 + i, 0))
    return pl.pallas_call(
        functools.partial(_rnn_sample_kernel, seqs=seqs),
        grid=(dec_batch // seqs,),
        in_specs=[tok, tok, pl.BlockSpec((rows, D_RNN), lambda i: (i, 0)), pl.BlockSpec((seqs, D_RNN), lambda i: (i, 0))]
        + [_const_spec(w.shape) for w in rnn_w],
        out_specs=(pl.BlockSpec((rows, D_RNN), lambda i: (i, 0)), pl.BlockSpec((seqs, D_RNN), lambda i: (i, 0))),
        out_shape=(jax.ShapeDtypeStruct((dec_batch * SUBLANES, D_RNN), BF16),
                   jax.ShapeDtypeStruct((dec_batch, D_RNN), F32)),
        compiler_params=_params(("parallel",)),
        name="rnn_sample",
    )(xr_all, gy_all, hist_pad, h0, *rnn_w)


def _layer_norm(z, g, b):
    mu = jnp.mean(z, axis=-1, keepdims=True)
    zc = z - mu
    var = jnp.mean(zc * zc, axis=-1, keepdims=True)
    return zc * lax.rsqrt(var + LN_EPS) * g + b


def _first_index_of_max(vals, iota, axis, sentinel):
    mx = jnp.max(vals, axis=axis, keepdims=True)
    return mx, jnp.min(jnp.where(vals == mx, iota, sentinel), axis=axis, keepdims=True)


def _route(scores, bias):
    t = scores.shape[1]
    grp = scores + bias
    g3 = grp.reshape(N_GROUPS, GROUP_SIZE, t)
    e_in_g = lax.broadcasted_iota(jnp.int32, g3.shape, 1)
    m1, first = _first_index_of_max(g3, e_in_g, 1, GROUP_SIZE)
    m2 = jnp.max(jnp.where(e_in_g == first, -jnp.inf, g3), axis=1, keepdims=True)
    gscore = (m1 + m2).reshape(N_GROUPS, t)
    g_iota = lax.broadcasted_iota(jnp.int32, gscore.shape, 0)
    gmask = jnp.zeros(gscore.shape, jnp.bool_)
    for _ in range(TOPK_GROUPS):
        _, gi = _first_index_of_max(gscore, g_iota, 0, N_GROUPS)
        hit = g_iota == gi
        gmask = gmask | hit
        gscore = jnp.where(hit, -jnp.inf, gscore)
    masked = jnp.where(gmask[:, None, :], g3, -jnp.inf).reshape(N_EXPERTS, t)
    e_iota = lax.broadcasted_iota(jnp.int32, masked.shape, 0)
    idx, wts, hits = [], [], []
    for _ in range(TOP_K):
        _, ei = _first_index_of_max(masked, e_iota, 0, N_EXPERTS)
        hit = e_iota == ei
        idx.append(ei)
        hits.append(hit)
        wts.append(jnp.sum(jnp.where(hit, scores, 0.0), axis=0, keepdims=True))
        masked = jnp.where(hit, -jnp.inf, masked)
    idx = jnp.concatenate(idx, axis=0)
    w = jnp.concatenate(wts, axis=0)
    w = w / jnp.sum(w, axis=0, keepdims=True) * ROUTED_SCALE
    return idx, w, hits


def _merge_kernel(xp_ref, xs_ref, aop_ref, aos_ref, rop_ref, ros_ref, sga_ref, sgr_ref, woa_ref, wor_ref, wout_ref, g1_ref, b1_ref,
                  wrt_ref, rb_ref, wsg_ref, wsu_ref, wsd_ref,
                  x1w_ref, base_ref, idx_ref, wt_ref, rank_ref, cnt_ref, carry_ref, *, n_p, tm):
    i = pl.program_id(0)

    @pl.when(i == 0)
    def _():
        carry_ref[...] = jnp.zeros(carry_ref.shape, F32)

    is_prompt = i < n_p
    x = jnp.where(is_prompt, xp_ref[...], xs_ref[...])
    pa = jnp.dot(jnp.where(is_prompt, aop_ref[...], aos_ref[...]), woa_ref[...], preferred_element_type=F32)
    pr = jnp.dot(jnp.where(is_prompt, rop_ref[...], ros_ref[...]), wor_ref[...], preferred_element_type=F32)
    merged = sga_ref[...] * pa + sgr_ref[...] * pr
    z = ALPHA * x + jnp.dot(merged.astype(BF16), wout_ref[...], preferred_element_type=F32)
    x1 = _layer_norm(z, g1_ref[...], b1_ref[...])
    words = _pack_words(x1)
    for h in range(tm // SC_ROWS):
        for s in range(ROW_TILE):
            x1w_ref[(h * ROW_TILE + s) * SC_ROWS:(h * ROW_TILE + s + 1) * SC_ROWS, :] = (
                words[h * SC_ROWS:(h + 1) * SC_ROWS, s * LANES:(s + 1) * LANES])
    x1b = x1.astype(BF16)
    u = jax.nn.silu(jnp.dot(x1b, wsg_ref[...], preferred_element_type=F32)) * jnp.dot(x1b, wsu_ref[...], preferred_element_type=F32)
    shared = jnp.dot(u.astype(BF16), wsd_ref[...], preferred_element_type=F32)
    base_ref[...] = ALPHA * x1 + shared
    logits = lax.dot_general(wrt_ref[...], x1, (((1,), (1,)), ((), ())), preferred_element_type=F32)
    idx, w, hits = _route(jax.nn.sigmoid(logits), rb_ref[...])
    idx_ref[...] = idx
    wt_ref[...] = w

    chosen = functools.reduce(jnp.logical_or, hits)
    chosen_f = jnp.where(chosen, 1.0, 0.0)
    earlier = (lax.broadcasted_iota(jnp.int32, (tm, tm), 0) < lax.broadcasted_iota(jnp.int32, (tm, tm), 1))
    prefix = jnp.dot(chosen_f.astype(BF16), jnp.where(earlier, 1.0, 0.0).astype(BF16), preferred_element_type=F32)
    before = prefix + carry_ref[...]
    ranks = [jnp.sum(jnp.where(hit, before, 0.0), axis=0, keepdims=True) for hit in hits]
    rank_ref[...] = jnp.concatenate(ranks, axis=0).astype(jnp.int32)
    carry_ref[...] = carry_ref[...] + jnp.sum(chosen_f, axis=1, keepdims=True)
    cnt_ref[...] = carry_ref[...]


def _merge(x_p, x_s, attn_p, attn_s, rnn_p, rnn_s, sga, sgr, weights):
    t_p, t_s = x_p.shape[0], x_s.shape[0]
    t = t_p + t_s
    tm = _pick_tile(math.gcd(t_p, t_s), (256, 128))
    row = lambda width: pl.BlockSpec((tm, width), lambda i: (i, 0))
    col = pl.BlockSpec((TOP_K, tm), lambda i: (0, i))

    return pl.pallas_call(
        functools.partial(_merge_kernel, n_p=t_p // tm, tm=tm),
        grid=(t // tm,),
        in_specs=[*_two_source_specs(t_p, t_s, tm, D_MODEL), *_two_source_specs(t_p, t_s, tm, D_ATTN),
                  *_two_source_specs(t_p, t_s, tm, D_RNN), row(D_MODEL), row(D_MODEL)]
        + [_const_spec(w.shape) for w in weights],
        out_specs=(pl.BlockSpec((tm * ROW_TILE, LANES), lambda i: (i, 0)), row(D_MODEL), col, col, col,
                   _const_spec((N_EXPERTS, 1))),
        out_shape=(
            jax.ShapeDtypeStruct((t * ROW_TILE, LANES), WORD),
            jax.ShapeDtypeStruct((t, D_MODEL), F32),
            jax.ShapeDtypeStruct((TOP_K, t), jnp.int32),
            jax.ShapeDtypeStruct((TOP_K, t), F32),
            jax.ShapeDtypeStruct((TOP_K, t), jnp.int32),
            jax.ShapeDtypeStruct((N_EXPERTS, 1), F32),
        ),
        scratch_shapes=[pltpu.VMEM((N_EXPERTS, 1), F32)],
        compiler_params=_params(("arbitrary",)),
        name="merge_ln1_route",
    )(x_p, x_s, attn_p, attn_s, rnn_p, rnn_s, sga, sgr, *weights)


def _expert_layout(counts, n_assign):
    counts = counts.reshape(N_EXPERTS).astype(jnp.int32)
    padded = (counts + ROW_PAD - 1) // ROW_PAD * ROW_PAD
    pend = jnp.cumsum(padded)
    pstart = pend - padded
    rows_alloc = (n_assign + N_EXPERTS * (ROW_PAD - 1)) // ROW_PAD * ROW_PAD + EXPERT_CHUNK
    n_chunks = jnp.maximum((counts + EXPERT_CHUNK - 1) // EXPERT_CHUNK, 1)
    part = EXPERT_CHUNK // CHUNK_COPIES
    written = jnp.maximum((counts + part - 1) // part, 1) * part
    ffn_end = jnp.max(pstart + written).reshape(1)
    return pstart, pstart + counts, pend, n_chunks, counts, ffn_end, rows_alloc


def _dest_kernel(idx_ref, rank_ref, pstart_ref, dest_ref, word_rows_ref):
    e_iota = lax.broadcasted_iota(jnp.int32, (N_EXPERTS, idx_ref.shape[1]), 0)
    starts = [jnp.sum(jnp.where(e_iota == idx_ref[k:k + 1, :], pstart_ref[...], 0), axis=0, keepdims=True)
              for k in range(TOP_K)]
    dest = jnp.concatenate(starts, axis=0) + rank_ref[...]
    dest_ref[...] = dest
    tm = dest.shape[1]
    word_rows_ref[0] = jnp.concatenate(
        [dest[k:k + 1, h * SC_ROWS:(h + 1) * SC_ROWS] * ROW_TILE + s
         for h in range(tm // SC_ROWS) for s in range(ROW_TILE) for k in range(TOP_K)], axis=0)


def _dest_rows(idx_t, rank_t, pstart, tm):
    t = idx_t.shape[1]
    col = pl.BlockSpec((TOP_K, tm), lambda i: (0, i))
    vecs = tm // SC_ROWS * ROW_TILE * TOP_K
    return pl.pallas_call(
        _dest_kernel,
        grid=(t // tm,),
        in_specs=[col, col, _const_spec((N_EXPERTS, 1))],
        out_specs=(col, pl.BlockSpec((1, vecs, SC_ROWS), lambda i: (i, 0, 0))),
        out_shape=(jax.ShapeDtypeStruct((TOP_K, t), jnp.int32),
                   jax.ShapeDtypeStruct((t // tm, vecs, SC_ROWS), jnp.int32)),
        compiler_params=_params(("parallel",)),
        name="dest_rows",
    )(idx_t, rank_t, pstart.reshape(N_EXPERTS, 1))


def _token_rows(r, n=1):
    return pl.ds(pl.multiple_of(r * ROW_TILE, ROW_TILE), n * ROW_TILE)


def _zero_row_groups(zero_ref, dst_ref, sem, first_group, n_groups):
    def start(g, c):
        pltpu.make_async_copy(zero_ref, dst_ref.at[_token_rows(g * ROW_PAD, ROW_PAD)], sem).start()
        return c

    lax.fori_loop(first_group, n_groups, start, 0)
    return n_groups - first_group


def _wait_zero_copies(zero_ref, dst_ref, sem, n):
    def wait(_, c):
        pltpu.make_async_copy(zero_ref, dst_ref.at[_token_rows(0, ROW_PAD)], sem).wait()
        return c

    lax.fori_loop(0, n, wait, 0)


SC_ROWS = 128


def _sc_workers():
    info = plsc.get_sparse_core_info()
    mesh = plsc.VectorSubcoreMesh(core_axis_name="c", subcore_axis_name="s")
    worker = lambda: lax.axis_index("s") * info.num_cores + lax.axis_index("c")
    return mesh, info.num_cores * info.num_subcores, worker


def _sc_scatter_rows(src, rows, n_out):
    mesh, n_workers, worker = _sc_workers()
    n_units = src.shape[0] // SC_ROWS
    assert n_units % n_workers == 0 and rows.shape == (n_units * TOP_K, SC_ROWS)
    per_worker = n_units // n_workers

    @functools.partial(
        pl.kernel, mesh=mesh,
        out_type=jax.ShapeDtypeStruct((n_out, LANES), src.dtype),
        scratch_types=[pltpu.VMEM((TOP_K, SC_ROWS), jnp.int32), pltpu.VMEM((SC_ROWS, LANES), src.dtype),
                       pltpu.SemaphoreType.DMA],
    )
    def scatter(src_hbm, rows_hbm, out_hbm, rows_v, data_v, sem):
        first = worker() * per_worker

        @pl.loop(0, per_worker)
        def _(it):
            u = first + it
            pltpu.sync_copy(rows_hbm.at[pl.ds(u * TOP_K, TOP_K)], rows_v)
            pltpu.sync_copy(src_hbm.at[pl.ds(u * SC_ROWS, SC_ROWS)], data_v)
            copies = [pltpu.async_copy(data_v, out_hbm.at[rows_v.at[k]], sem) for k in range(TOP_K)]
            for cp in copies:
                cp.wait()

    return scatter(src, rows)


def _zero_padding_kernel(lo_ref, hi_ref, tail_ref, xs_in_ref, xs_ref, zero_ref, sem, *, rows_alloc):
    del xs_in_ref
    zero_ref[...] = jnp.zeros(zero_ref.shape, WORD)
    row_copy = lambda r: pltpu.make_async_copy(zero_ref.at[pl.ds(0, ROW_TILE)], xs_ref.at[_token_rows(r)], sem)

    def expert(e, n):
        def row(r, c):
            row_copy(r).start()
            return c

        lax.fori_loop(lo_ref[e], hi_ref[e], row, 0)
        return n + hi_ref[e] - lo_ref[e]

    def wait(_, c):
        row_copy(0).wait()
        return c

    lax.fori_loop(0, lax.fori_loop(0, N_EXPERTS, expert, 0), wait, 0)
    n = _zero_row_groups(zero_ref, xs_ref, sem, tail_ref[0] // ROW_PAD, rows_alloc // ROW_PAD)
    _wait_zero_copies(zero_ref, xs_ref, sem, n)


def _zero_padding(xs, pad_lo, pad_hi, total, rows_alloc):
    grid_spec = pltpu.PrefetchScalarGridSpec(
        num_scalar_prefetch=3,
        grid=(1,),
        in_specs=[pl.BlockSpec(memory_space=pl.ANY)],
        out_specs=pl.BlockSpec(memory_space=pl.ANY),
        scratch_shapes=[pltpu.VMEM((ROW_PAD * ROW_TILE, LANES), WORD), pltpu.SemaphoreType.DMA],
    )
    return pl.pallas_call(
        functools.partial(_zero_padding_kernel, rows_alloc=rows_alloc),
        grid_spec=grid_spec,
        out_shape=jax.ShapeDtypeStruct(xs.shape, xs.dtype),
        input_output_aliases={3: 0},
        compiler_params=_params(("arbitrary",)),
        name="zero_padding",
    )(pad_lo, pad_hi, total, xs)


def _expert_kernel(pstart_ref, nch_ref, rows_ref, end_ref, wg_ref, wu_ref, wd_ref, xs_ref, o_ref,
                   xbuf_ref, obuf_ref, wgb_ref, wub_ref, wdb_ref, zero_ref, done_ref, in_sem, out_sem, *, rows_alloc):
    e = pl.program_id(0)
    n_e = pl.num_programs(0)
    start = pstart_ref[e]
    nch = nch_ref[e]
    buf_rows = EXPERT_CHUNK * ROW_TILE

    part = EXPERT_CHUNK // CHUNK_COPIES
    part_rows = part * ROW_TILE

    class _ChunkCopy:
        def __init__(self, copies, n_parts):
            self.copies = copies
            self.n_parts = n_parts

        def _each(self, act):
            for p, cp in enumerate(self.copies):
                pl.when(p < self.n_parts)(functools.partial(act, cp))

        def start(self):
            self._each(lambda cp: cp.start())

        def wait(self):
            self._each(lambda cp: cp.wait())

    def parts_of(rows_left):
        return jnp.clip((rows_left + part - 1) // part, 1, CHUNK_COPIES)

    def in_copy(row, slot, n_parts):
        return _ChunkCopy([
            pltpu.make_async_copy(xs_ref.at[_token_rows(row + p * part, part)],
                                  xbuf_ref.at[pl.ds(slot * buf_rows + p * part_rows, part_rows)], in_sem.at[slot])
            for p in range(CHUNK_COPIES)], n_parts)

    def out_copy(row, slot, n_parts):
        return _ChunkCopy([
            pltpu.make_async_copy(obuf_ref.at[pl.ds(slot * buf_rows + p * part_rows, part_rows)],
                                  o_ref.at[_token_rows(row + p * part, part)], out_sem)
            for p in range(CHUNK_COPIES)], n_parts)

    @pl.when(e == 0)
    def _():
        done_ref[0] = 0
        xbuf_ref[...] = jnp.zeros(xbuf_ref.shape, WORD)
        obuf_ref[...] = jnp.zeros(obuf_ref.shape, WORD)
        in_copy(start, 0, parts_of(rows_ref[0])).start()

    wgb_ref[...] = wg_ref[...].astype(BF16)
    wub_ref[...] = wu_ref[...].astype(BF16)
    wdb_ref[...] = wd_ref[...].astype(BF16)
    done = done_ref[0]

    def chunk(c, carry):
        g = done + c
        slot = g % 2
        row = start + c * EXPERT_CHUNK
        rows_left = rows_ref[e] - c * EXPERT_CHUNK
        n_parts = parts_of(rows_left)
        in_copy(row, slot, n_parts).wait()
        last = c + 1 == nch
        next_e = jnp.minimum(e + 1, n_e - 1)
        next_row = jnp.where(last, pstart_ref[next_e], row + EXPERT_CHUNK)
        next_left = jnp.where(last, rows_ref[next_e], rows_left - EXPERT_CHUNK)

        @pl.when(jnp.logical_not(last & (e == n_e - 1)))
        def _():
            in_copy(next_row, 1 - slot, parts_of(next_left)).start()

        base = pl.multiple_of(slot * buf_rows, buf_rows)
        def sub_block(h):
            xb = jnp.concatenate(_load_token_rows(xbuf_ref, EXPERT_ROWS, row0=h * EXPERT_ROWS, base=base),
                                 axis=1).astype(BF16)
            gate = jnp.dot(xb, wgb_ref[...], preferred_element_type=F32)
            up = jnp.dot(xb, wub_ref[...], preferred_element_type=F32)
            act = (jax.nn.silu(gate) * up).astype(BF16)
            _store_token_rows(obuf_ref, jnp.dot(act, wdb_ref[...], preferred_element_type=F32),
                              EXPERT_ROWS, row0=h * EXPERT_ROWS, base=base)

        n_sub = EXPERT_CHUNK // EXPERT_ROWS
        needed = jnp.clip((rows_left + EXPERT_ROWS - 1) // EXPERT_ROWS, 1, n_sub)
        for count in range(1, n_sub + 1):
            @pl.when(needed == count)
            def _():
                for h in range(count):
                    sub_block(h)

        @pl.when(g > 0)
        def _():
            out_copy(0, 0, done_ref[1]).wait()

        out_copy(row, slot, n_parts).start()
        done_ref[1] = n_parts
        return carry

    lax.fori_loop(0, nch, chunk, 0)
    done_ref[0] = done + nch

    @pl.when(e == n_e - 1)
    def _():
        out_copy(0, 0, done_ref[1]).wait()
        zero_ref[...] = jnp.zeros(zero_ref.shape, WORD)
        n = _zero_row_groups(zero_ref, o_ref, out_sem, end_ref[0] // ROW_PAD, rows_alloc // ROW_PAD)
        _wait_zero_copies(zero_ref, o_ref, out_sem, n)


def _expert_ffn(xs, pstart, n_chunks, rows, ffn_end, rows_alloc, w_e_gate, w_e_up, w_e_down):
    weight = lambda shape: pl.BlockSpec((None, *shape), lambda e, *_: (e, 0, 0))
    grid_spec = pltpu.PrefetchScalarGridSpec(
        num_scalar_prefetch=4,
        grid=(N_EXPERTS,),
        in_specs=[weight((D_MODEL, D_EXPERT)), weight((D_MODEL, D_EXPERT)), weight((D_EXPERT, D_MODEL)),
                  pl.BlockSpec(memory_space=pl.ANY)],
        out_specs=pl.BlockSpec(memory_space=pl.ANY),
        scratch_shapes=[
            pltpu.VMEM((2 * EXPERT_CHUNK * ROW_TILE, LANES), WORD),
            pltpu.VMEM((2 * EXPERT_CHUNK * ROW_TILE, LANES), WORD),
            pltpu.VMEM((D_MODEL, D_EXPERT), BF16), pltpu.VMEM((D_MODEL, D_EXPERT), BF16),
            pltpu.VMEM((D_EXPERT, D_MODEL), BF16),
            pltpu.VMEM((ROW_PAD * ROW_TILE, LANES), WORD),
            pltpu.SMEM((2,), jnp.int32),
            pltpu.SemaphoreType.DMA((2,)), pltpu.SemaphoreType.DMA,
        ],
    )
    return pl.pallas_call(
        functools.partial(_expert_kernel, rows_alloc=rows_alloc),
        grid_spec=grid_spec,
        out_shape=jax.ShapeDtypeStruct((rows_alloc * ROW_TILE, LANES), WORD),
        compiler_params=_params(("arbitrary",)),
        name="expert_ffn",
    )(pstart, n_chunks, rows, ffn_end, w_e_gate, w_e_up, w_e_down, xs)


def _combine_head_kernel(dest_ref, dest_next_ref, w_ref, base_ref, g_ref, b_ref, outs_ref, y_ref, buf_ref, sem, *, tm):
    i = pl.program_id(0)
    slot_rows = TOP_K * tm

    def gather(d_ref, slot):
        def issue(t, c):
            for k in range(TOP_K):
                pltpu.make_async_copy(outs_ref.at[_token_rows(d_ref[k, t])],
                                      buf_ref.at[_token_rows(slot * slot_rows + k * tm + t)], sem.at[slot]).start()
            return c

        lax.fori_loop(0, tm, issue, 0)

    @pl.when(i == 0)
    def _():
        gather(dest_ref, 0)

    @pl.when(i + 1 < pl.num_programs(0))
    def _():
        gather(dest_next_ref, (i + 1) % 2)

    slot = i % 2
    for k in range(TOP_K):
        pltpu.make_async_copy(outs_ref.at[_token_rows(0, tm)], buf_ref.at[_token_rows(slot * slot_rows + k * tm, tm)],
                              sem.at[slot]).wait()

    w = w_ref[...]
    base = pl.multiple_of(slot * slot_rows * ROW_TILE, ROW_TILE)
    halves = [None, None]
    for k in range(TOP_K):
        wk = w[:, k:k + 1]
        for j, rows in enumerate(_load_token_rows(buf_ref, tm, row0=k * tm, base=base)):
            halves[j] = wk * rows if halves[j] is None else halves[j] + wk * rows
    y_ref[...] = _layer_norm(base_ref[...] + jnp.concatenate(halves, axis=1), g_ref[...], b_ref[...])


def _combine_head(out_sorted, dest, w_tok, base, g2, b2, tm, n_head):
    row = pl.BlockSpec((tm, D_MODEL), lambda i: (i, 0))
    return pl.pallas_call(
        functools.partial(_combine_head_kernel, tm=tm),
        grid=(n_head,),
        in_specs=[
            pl.BlockSpec((TOP_K, tm), lambda i: (0, i), memory_space=pltpu.SMEM),
            pl.BlockSpec((TOP_K, tm), lambda i: (0, jnp.minimum(i + 1, n_head - 1)), memory_space=pltpu.SMEM),
            pl.BlockSpec((tm, TOP_K), lambda i: (i, 0)),
            row, _const_spec((1, D_MODEL)), _const_spec((1, D_MODEL)),
            pl.BlockSpec(memory_space=pl.ANY),
        ],
        out_specs=row,
        out_shape=jax.ShapeDtypeStruct((n_head * tm, D_MODEL), F32),
        scratch_shapes=[pltpu.VMEM((2 * TOP_K * tm * ROW_TILE, LANES), WORD), pltpu.SemaphoreType.DMA((2,))],
        compiler_params=_params(("arbitrary",)),
        name="combine_head",
    )(dest, dest, w_tok, base, g2, b2, out_sorted)


def _sc_gather_rows(table, rows):
    mesh, n_workers, worker = _sc_workers()
    m = rows.shape[0]
    per_worker = m // n_workers
    assert m % n_workers == 0 and per_worker % SC_ROWS == 0
    in_flight = _pick_tile(per_worker // SC_ROWS, (4, 2, 1))
    step = SC_ROWS * in_flight

    @functools.partial(
        pl.kernel, mesh=mesh,
        out_type=jax.ShapeDtypeStruct((m, LANES), table.dtype),
        scratch_types=[pltpu.VMEM((step,), jnp.int32), pltpu.VMEM((step, LANES), table.dtype), pltpu.SemaphoreType.DMA],
    )
    def gather(table_hbm, rows_hbm, out_hbm, rows_v, data_v, sem):
        first = worker() * per_worker

        @pl.loop(0, per_worker // step)
        def _(it):
            off = first + it * step
            pltpu.sync_copy(rows_hbm.at[pl.ds(off, step)], rows_v)
            copies = [pltpu.async_copy(table_hbm.at[rows_v.at[pl.ds(j * SC_ROWS, SC_ROWS)]],
                                       data_v.at[pl.ds(j * SC_ROWS, SC_ROWS)], sem)
                      for j in range(in_flight)]
            for cp in copies:
                cp.wait()
            pltpu.sync_copy(data_v, out_hbm.at[pl.ds(off, step)])

    return gather(table, rows)


def _combine_tail_kernel(yh_ref, rows_ref, w_ref, base_ref, g_ref, b_ref, yp_ref, ys_ref, *, tm, n_head, n_p):
    i = pl.program_id(0)

    @pl.when(i < n_head)
    def _():
        yp_ref[...] = yh_ref[...]

    @pl.when(i >= n_head)
    def _():
        w = w_ref[...]
        lo = [None] * ROW_TILE
        hi = [None] * ROW_TILE
        for k in range(TOP_K):
            wk = w[:, k:k + 1]
            for s in range(ROW_TILE):
                vec = lambda h: ((h * ROW_TILE + s) * TOP_K + k) * SC_ROWS
                words = jnp.concatenate([rows_ref[vec(h):vec(h) + SC_ROWS, :] for h in range(tm // SC_ROWS)], axis=0)
                lo_s, hi_s = _unpack_words(words)
                lo[s] = wk * lo_s if lo[s] is None else lo[s] + wk * lo_s
                hi[s] = wk * hi_s if hi[s] is None else hi[s] + wk * hi_s
        y = _layer_norm(base_ref[...] + jnp.concatenate(lo + hi, axis=1), g_ref[...], b_ref[...])

        @pl.when(i < n_p)
        def _():
            yp_ref[...] = y

        @pl.when(i >= n_p)
        def _():
            ys_ref[...] = y


def _combine_tail(y_head, gathered, w_tok, base, g2, b2, t_p, t_s, tm, n_head):
    n_p = t_p // tm
    assert n_head <= n_p
    n_tiles = (t_p + t_s) // tm
    blk = TOP_K * ROW_TILE * tm
    row = pl.BlockSpec((tm, D_MODEL), lambda i: (i, 0))
    out_p, out_s = _two_source_specs(t_p, t_s, tm, D_MODEL)
    return pl.pallas_call(
        functools.partial(_combine_tail_kernel, tm=tm, n_head=n_head, n_p=n_p),
        grid=(n_tiles,),
        in_specs=[
            pl.BlockSpec((tm, D_MODEL), lambda i: (jnp.minimum(i, n_head - 1), 0)),
            pl.BlockSpec((blk, LANES), lambda i: (jnp.maximum(i - n_head, 0), 0)),
            pl.BlockSpec((tm, TOP_K), lambda i: (i, 0)),
            row, _const_spec((1, D_MODEL)), _const_spec((1, D_MODEL)),
        ],
        out_specs=(out_p, out_s),
        out_shape=(jax.ShapeDtypeStruct((t_p, D_MODEL), F32), jax.ShapeDtypeStruct((t_s, D_MODEL), F32)),
        compiler_params=_params(("arbitrary",)),
        name="combine_tail",
    )(y_head, gathered, w_tok, base, g2, b2)


def kernel(x_prompt, x_sample, cache_k, cache_v, state_conv, state_rnn, w_in, conv_w, conv_b, w_gate_a, b_gate_a, w_gate_x, b_gate_x, lru_lambda, rel_bias, sinks, w_o_attn, w_o_rnn, w_out, ln1_g, ln1_b, w_router, router_bias, w_e_gate, w_e_up, w_e_down, w_s_gate, w_s_up, w_s_down, ln2_g, ln2_b):
    assert w_in.shape[0] == DEPTH == 1
    batch, seq, _ = x_prompt.shape
    dec_batch, s_len, _ = x_sample.shape
    w_cache = cache_k.shape[2]
    assert s_len == SUBLANES and seq % WINDOW == 0 and w_cache == WINDOW
    t_p = batch * seq
    t_s = dec_batch * s_len
    vec = lambda a: a[0].reshape(1, -1).astype(F32)

    x_p = x_prompt.reshape(t_p, D_MODEL)
    x_s = x_sample.reshape(t_s, D_MODEL)
    q, k, v, xr, gy, sga, sgr = _inproj(x_p, x_s, w_in[0].astype(BF16))

    qi = jnp.arange(WINDOW)
    dist = qi[:, None] + WINDOW - jnp.arange(2 * WINDOW)[None, :]
    bias_p = _bias_table(rel_bias, dist)
    attn_p = _attn_prompt(q, k, v, bias_p, sinks[0], batch, seq)
    attn_s, k_s, v_s = _attn_sample(
        q, k, v, cache_k[0].reshape(dec_batch, w_cache, D_KV), cache_v[0].reshape(dec_batch, w_cache, D_KV),
        rel_bias, sinks[0], t_p, dec_batch, s_len)

    rnn_w = (conv_w[0], vec(conv_b), w_gate_a[0].astype(BF16), vec(b_gate_a), w_gate_x[0].astype(BF16),
             vec(b_gate_x), vec(lru_lambda))
    rnn_p, h_p = _rnn_prompt(xr, gy, rnn_w, batch, seq)
    hist_pad = jnp.pad(state_conv[0], ((0, 0), (SUBLANES - (CONV_W - 1), 0), (0, 0))).reshape(t_s, D_RNN)
    rnn_s, h_s = _rnn_sample(xr, gy, hist_pad, state_rnn[0], rnn_w, t_p, dec_batch)

    merge_w = (w_o_attn[0].astype(BF16), w_o_rnn[0].astype(BF16), w_out[0].astype(BF16), vec(ln1_g), vec(ln1_b),
               w_router[0].T, router_bias[0].reshape(N_EXPERTS, 1), w_s_gate[0].astype(BF16),
               w_s_up[0].astype(BF16), w_s_down[0].astype(BF16))
    x1w, base, idx_t, wt_t, rank_t, counts = _merge(x_p, x_s, attn_p, attn_s, rnn_p, rnn_s, sga, sgr, merge_w)

    pstart, pad_lo, pad_hi, n_chunks, rows, ffn_end, rows_alloc = _expert_layout(counts, (t_p + t_s) * TOP_K)
    tm = _pick_tile(math.gcd(t_p, t_s), (256, 128))
    dest, word_rows = _dest_rows(idx_t, rank_t, pstart, tm)
    xs = _sc_scatter_rows(x1w, word_rows.reshape(-1, SC_ROWS), rows_alloc * ROW_TILE)
    xs = _zero_padding(xs, pad_lo, pad_hi, pad_hi[N_EXPERTS - 1:], rows_alloc)
    out_sorted = _expert_ffn(xs, pstart, n_chunks, rows, ffn_end, rows_alloc, w_e_gate[0], w_e_up[0], w_e_down[0])
    n_tiles = (t_p + t_s) // tm
    n_head = min(t_p // tm, n_tiles // 3)
    w_tok = wt_t.T
    g2, b2 = vec(ln2_g), vec(ln2_b)
    y_head = _combine_head(out_sorted, dest, w_tok, base, g2, b2, tm, n_head)
    gathered = _sc_gather_rows(out_sorted, word_rows[n_head:].reshape(-1))
    y_p, y_s = _combine_tail(y_head, gathered, w_tok, base, g2, b2, t_p, t_s, tm, n_head)
    y_p = y_p.reshape(batch, seq, D_MODEL)
    y_s = y_s.reshape(dec_batch, s_len, D_MODEL)
    kv5 = lambda a, b: a.reshape(1, b, WINDOW, N_KV_HEADS, HEAD_DIM)
    tail = lambda a, n: jnp.stack([lax.slice_in_dim(a, (b + 1) * seq - n, (b + 1) * seq) for b in range(batch)])
    k_p = kv5(tail(k, WINDOW), batch)
    v_p = kv5(tail(v, WINDOW), batch)
    conv_p = tail(xr, CONV_W - 1)[None]
    conv_s = xr[t_p:].reshape(dec_batch, s_len, D_RNN)[:, s_len - (CONV_W - 1):][None]
    return (y_p, y_s, k_p, v_p, conv_p, h_p.reshape(1, batch, D_RNN),
            kv5(k_s, dec_batch), kv5(v_s, dec_batch), conv_s, h_s.reshape(1, dec_batch, D_RNN))
```

```python
import functools
import math

import jax
import jax.numpy as jnp
from jax import lax
from jax.experimental import pallas as pl
from jax.experimental.pallas import tpu as pltpu
from jax.experimental.pallas import tpu_sc as plsc

F32 = jnp.float32
BF16 = jnp.bfloat16
WORD = jnp.int32

D_MODEL = 1024
N_HEADS = 8
N_KV_HEADS = 2
HEAD_DIM = 64
GROUP = N_HEADS // N_KV_HEADS
WINDOW = 128
D_ATTN = N_HEADS * HEAD_DIM
D_KV = N_KV_HEADS * HEAD_DIM
N_BUCKETS = 32
MAX_DISTANCE = 128
D_RNN = D_MODEL
RNN_BLOCK = 256
N_RNN_BLOCKS = D_RNN // RNN_BLOCK
CONV_W = 4
RG_C = 8.0
N_EXPERTS = 256
TOP_K = 8
N_GROUPS = 8
GROUP_SIZE = N_EXPERTS // N_GROUPS
TOPK_GROUPS = 4
D_EXPERT = D_MODEL // 4
ROUTED_SCALE = 2.5
LN_EPS = 1e-5
DEPTH = 1
ALPHA = (2 * DEPTH) ** 0.25
NEG_INF = -1e30
SM_SCALE = HEAD_DIM ** -0.5

O_Q = 0
O_K = D_ATTN
O_V = O_K + D_KV
O_XR = O_V + D_KV
O_YR = O_XR + D_RNN
O_GA = O_YR + D_RNN
O_GR = O_GA + D_MODEL
D_IN = O_GR + D_MODEL

SUBLANES = 8
VMEM_LIMIT_BYTES = 56 * 1024 * 1024
EXPERT_ROWS = 256
EXPERT_CHUNK = 1024
CHUNK_COPIES = 8
ROW_PAD = SUBLANES


def _params(sem):
    return pltpu.CompilerParams(dimension_semantics=sem, vmem_limit_bytes=VMEM_LIMIT_BYTES)


def _pick_tile(n, candidates):
    for c in candidates:
        if n % c == 0:
            return c
    raise ValueError(f"no tile for {n}")


def _const_spec(shape):
    nd = len(shape)
    return pl.BlockSpec(shape, lambda *_: (0,) * nd)


LANES = 128
ROW_WORDS = D_MODEL // 2
ROW_TILE = ROW_WORDS // LANES
HIGH_HALF = -65536


def _pack_words(mat):
    as_bits = lambda v: pltpu.bitcast(v.astype(BF16).astype(F32), WORD)
    return (as_bits(mat[:, ROW_WORDS:]) & HIGH_HALF) | lax.shift_right_logical(as_bits(mat[:, :ROW_WORDS]), 16)


def _unpack_words(words):
    return pltpu.bitcast(words << 16, F32), pltpu.bitcast(words & HIGH_HALF, F32)


def _store_token_rows(ref, mat, n, row0=0, base=0):
    words = _pack_words(mat)
    for s in range(ROW_TILE):
        ref[pl.ds(base + row0 * ROW_TILE + s, n, stride=ROW_TILE), :] = words[:, s * LANES:(s + 1) * LANES]


def _load_token_rows(ref, n, row0=0, base=0):
    words = jnp.concatenate(
        [ref[pl.ds(base + row0 * ROW_TILE + s, n, stride=ROW_TILE), :] for s in range(ROW_TILE)], axis=1)
    return _unpack_words(words)


def _two_source_specs(t_p, t_s, tm, width):
    n_p = t_p // tm
    assert t_p % tm == 0 and t_s % tm == 0
    return (pl.BlockSpec((tm, width), lambda i, *_: (jnp.minimum(i, n_p - 1), 0)),
            pl.BlockSpec((tm, width), lambda i, *_: (jnp.maximum(i - n_p, 0), 0)))


def _inproj_kernel(xp_ref, xs_ref, w_ref, q_ref, k_ref, v_ref, xr_ref, gy_ref, sga_ref, sgr_ref, *, n_p):
    x = jnp.where(pl.program_id(0) < n_p, xp_ref[...], xs_ref[...]).astype(BF16)

    def seg(lo, hi):
        return jnp.dot(x, w_ref[:, lo:hi], preferred_element_type=F32)

    q_ref[...] = seg(O_Q, O_K).astype(BF16)
    k_ref[...] = seg(O_K, O_V)
    v_ref[...] = seg(O_V, O_XR)
    xr_ref[...] = seg(O_XR, O_YR)
    gy_ref[...] = jax.nn.gelu(seg(O_YR, O_GA)).astype(BF16)
    sga_ref[...] = jax.nn.sigmoid(seg(O_GA, O_GR)).astype(BF16)
    sgr_ref[...] = jax.nn.sigmoid(seg(O_GR, D_IN)).astype(BF16)


def _inproj(x_p, x_s, w_in_bf16):
    t_p, t_s = x_p.shape[0], x_s.shape[0]
    t = t_p + t_s
    tm = _pick_tile(math.gcd(t_p, t_s), (256, 128, 64, 32, 16, 8))
    row = lambda width: pl.BlockSpec((tm, width), lambda i: (i, 0))
    out_shape = (
        jax.ShapeDtypeStruct((t, D_ATTN), BF16),
        jax.ShapeDtypeStruct((t, D_KV), F32),
        jax.ShapeDtypeStruct((t, D_KV), F32),
        jax.ShapeDtypeStruct((t, D_RNN), F32),
        jax.ShapeDtypeStruct((t, D_RNN), BF16),
        jax.ShapeDtypeStruct((t, D_MODEL), BF16),
        jax.ShapeDtypeStruct((t, D_MODEL), BF16),
    )
    return pl.pallas_call(
        functools.partial(_inproj_kernel, n_p=t_p // tm),
        grid=(t // tm,),
        in_specs=[*_two_source_specs(t_p, t_s, tm, D_MODEL), _const_spec((D_MODEL, D_IN))],
        out_specs=(row(D_ATTN), row(D_KV), row(D_KV), row(D_RNN), row(D_RNN), row(D_MODEL), row(D_MODEL)),
        out_shape=out_shape,
        compiler_params=_params(("parallel",)),
        name="inproj",
    )(x_p, x_s, w_in_bf16)


def _t5_bucket(dist):
    n = jnp.maximum(dist, 0)
    max_exact = N_BUCKETS // 2
    nf = jnp.maximum(n, 1).astype(F32)
    large = max_exact + (jnp.log(nf / max_exact) / math.log(MAX_DISTANCE / max_exact) * (N_BUCKETS - max_exact)).astype(jnp.int32)
    large = jnp.minimum(large, N_BUCKETS - 1)
    return jnp.where(n < max_exact, n, large)


def _bias_table(rel_bias, dist):
    bucket = _t5_bucket(dist)
    rb = rel_bias.astype(F32)
    out = jnp.zeros((N_HEADS, *dist.shape), F32)
    for j in range(N_BUCKETS):
        out = jnp.where(bucket[None] == j, rb[j][:, None, None], out)
    return out


def _softmax_pv(s, sink, v):
    m = jnp.maximum(jnp.max(s, axis=-1, keepdims=True), sink)
    p = jnp.exp(s - m)
    denom = jnp.sum(p, axis=-1, keepdims=True) + jnp.exp(sink - m)
    return jnp.dot(p.astype(BF16), v, preferred_element_type=F32), denom


def _attn_prompt_kernel(sink_ref, q_ref, kc_ref, kp_ref, vc_ref, vp_ref, bias_ref, o_ref):
    n = pl.program_id(1)
    kk = jnp.concatenate([kp_ref[...], kc_ref[...]], axis=0).astype(BF16)
    vv = jnp.concatenate([vp_ref[...], vc_ref[...]], axis=0).astype(BF16)
    q = q_ref[...]
    rows = lax.broadcasted_iota(jnp.int32, (WINDOW, 2 * WINDOW), 0)
    cols = lax.broadcasted_iota(jnp.int32, (WINDOW, 2 * WINDOW), 1)
    dist = rows + WINDOW - cols
    valid = (dist >= 0) & (dist <= WINDOW) & ((n > 0) | (cols >= WINDOW))
    for g in range(N_KV_HEADS):
        kg = kk[:, g * HEAD_DIM:(g + 1) * HEAD_DIM]
        vg = vv[:, g * HEAD_DIM:(g + 1) * HEAD_DIM]
        for h in range(GROUP):
            hh = g * GROUP + h
            qh = q[:, hh * HEAD_DIM:(hh + 1) * HEAD_DIM]
            s = lax.dot_general(qh, kg, (((1,), (1,)), ((), ())), preferred_element_type=F32) * SM_SCALE
            s = jnp.where(valid, s + bias_ref[hh], NEG_INF)
            o, denom = _softmax_pv(s, sink_ref[0, hh], vg)
            o_ref[:, hh * HEAD_DIM:(hh + 1) * HEAD_DIM] = (o / denom).astype(BF16)


def _attn_prompt(q_all, k_all, v_all, bias, sinks, batch, seq):
    nb = seq // WINDOW
    cur = lambda width: pl.BlockSpec((WINDOW, width), lambda b, n: (b * nb + n, 0))
    prev = lambda width: pl.BlockSpec((WINDOW, width), lambda b, n: (b * nb + jnp.maximum(n - 1, 0), 0))
    return pl.pallas_call(
        _attn_prompt_kernel,
        grid=(batch, nb),
        in_specs=[
            pl.BlockSpec(memory_space=pltpu.SMEM),
            cur(D_ATTN), cur(D_KV), prev(D_KV), cur(D_KV), prev(D_KV),
            _const_spec((N_HEADS, WINDOW, 2 * WINDOW)),
        ],
        out_specs=cur(D_ATTN),
        out_shape=jax.ShapeDtypeStruct((batch * seq, D_ATTN), BF16),
        compiler_params=_params(("parallel", "arbitrary")),
        name="attn_prompt",
    )(sinks.reshape(1, N_HEADS).astype(F32), q_all, k_all, k_all, v_all, v_all, bias)


def _attn_sample_kernel(q_ref, kn_ref, vn_ref, kc_ref, vc_ref, bc_ref, bn_ref, sink_ref,
                        o_ref, ko_ref, vo_ref, *, seqs, s_len):
    w = kc_ref.shape[1]
    rows_c = lax.broadcasted_iota(jnp.int32, (GROUP * s_len, w), 0) % s_len
    cols_c = lax.broadcasted_iota(jnp.int32, (GROUP * s_len, w), 1)
    dist_c = rows_c + w - cols_c
    valid_c = (dist_c >= 0) & (dist_c <= WINDOW)
    rows_n = lax.broadcasted_iota(jnp.int32, (GROUP * s_len, s_len), 0) % s_len
    cols_n = lax.broadcasted_iota(jnp.int32, (GROUP * s_len, s_len), 1)
    dist_n = rows_n - cols_n
    valid_n = (dist_n >= 0) & (dist_n <= WINDOW)
    for j in range(seqs):
        r0 = j * s_len
        qj = q_ref[r0:r0 + s_len, :]
        kc = kc_ref[j]
        vc = vc_ref[j]
        kn = kn_ref[r0:r0 + s_len, :]
        vn = vn_ref[r0:r0 + s_len, :]
        ko_ref[j, 0:w - s_len, :] = kc[s_len:, :]
        ko_ref[j, w - s_len:w, :] = kn
        vo_ref[j, 0:w - s_len, :] = vc[s_len:, :]
        vo_ref[j, w - s_len:w, :] = vn
        kcb, vcb, knb, vnb = kc.astype(BF16), vc.astype(BF16), kn.astype(BF16), vn.astype(BF16)
        for g in range(N_KV_HEADS):
            lo, hi = g * HEAD_DIM, (g + 1) * HEAD_DIM
            qs = jnp.concatenate(
                [qj[:, (g * GROUP + h) * HEAD_DIM:(g * GROUP + h + 1) * HEAD_DIM] for h in range(GROUP)], axis=0)
            nt = (((1,), (1,)), ((), ()))
            s_c = lax.dot_general(qs, kcb[:, lo:hi], nt, preferred_element_type=F32) * SM_SCALE
            s_n = lax.dot_general(qs, knb[:, lo:hi], nt, preferred_element_type=F32) * SM_SCALE
            s_c = jnp.where(valid_c, s_c + bc_ref[g], NEG_INF)
            s_n = jnp.where(valid_n, s_n + bn_ref[g], NEG_INF)
            sink = sink_ref[g]
            m = jnp.maximum(jnp.maximum(jnp.max(s_c, axis=-1, keepdims=True), jnp.max(s_n, axis=-1, keepdims=True)), sink)
            p_c = jnp.exp(s_c - m)
            p_n = jnp.exp(s_n - m)
            denom = jnp.sum(p_c, axis=-1, keepdims=True) + jnp.sum(p_n, axis=-1, keepdims=True) + jnp.exp(sink - m)
            o = jnp.dot(p_c.astype(BF16), vcb[:, lo:hi], preferred_element_type=F32)
            o = o + jnp.dot(p_n.astype(BF16), vnb[:, lo:hi], preferred_element_type=F32)
            o = (o / denom).astype(BF16)
            for h in range(GROUP):
                hh = g * GROUP + h
                o_ref[r0:r0 + s_len, hh * HEAD_DIM:(hh + 1) * HEAD_DIM] = o[h * s_len:(h + 1) * s_len, :]


def _attn_sample(q_all, k_all, v_all, cache_k, cache_v, rel_bias, sinks, row0, dec_batch, s_len):
    w = cache_k.shape[1]
    seqs = _pick_tile(dec_batch, (16, 8, 4, 2, 1))
    rows = seqs * s_len
    blk0 = row0 // rows
    assert row0 % rows == 0
    qi = jnp.arange(s_len)
    dist_c = qi[:, None] + w - jnp.arange(w)[None, :]
    dist_n = qi[:, None] - jnp.arange(s_len)[None, :]
    b_c = _bias_table(rel_bias, dist_c).reshape(N_KV_HEADS, GROUP * s_len, w)
    b_n = _bias_table(rel_bias, dist_n).reshape(N_KV_HEADS, GROUP * s_len, s_len)
    sink = jnp.broadcast_to(sinks.astype(F32).reshape(N_KV_HEADS, GROUP, 1, 1), (N_KV_HEADS, GROUP, s_len, 1))
    sink = sink.reshape(N_KV_HEADS, GROUP * s_len, 1)
    tok = lambda width: pl.BlockSpec((rows, width), lambda i: (blk0 + i, 0))
    cache = pl.BlockSpec((seqs, w, D_KV), lambda i: (i, 0, 0))
    return pl.pallas_call(
        functools.partial(_attn_sample_kernel, seqs=seqs, s_len=s_len),
        grid=(dec_batch // seqs,),
        in_specs=[
            tok(D_ATTN), tok(D_KV), tok(D_KV), cache, cache,
            _const_spec(b_c.shape), _const_spec(b_n.shape), _const_spec(sink.shape),
        ],
        out_specs=(pl.BlockSpec((rows, D_ATTN), lambda i: (i, 0)), cache, cache),
        out_shape=(
            jax.ShapeDtypeStruct((dec_batch * s_len, D_ATTN), BF16),
            jax.ShapeDtypeStruct((dec_batch, w, D_KV), F32),
            jax.ShapeDtypeStruct((dec_batch, w, D_KV), F32),
        ),
        compiler_params=_params(("parallel",)),
        name="attn_sample",
    )(q_all, k_all, v_all, cache_k, cache_v, b_c, b_n, sink)


def _softplus(z):
    return jnp.maximum(z, 0.0) + jnp.log1p(jnp.exp(-jnp.abs(z)))


def _block_gate(xcb, w_ref, b_ref):
    parts = [jnp.dot(xcb[:, n * RNN_BLOCK:(n + 1) * RNN_BLOCK], w_ref[n], preferred_element_type=F32)
             for n in range(N_RNN_BLOCKS)]
    return jax.nn.sigmoid(jnp.concatenate(parts, axis=-1) + b_ref[...])


def _lru_coeffs(xc, wa_ref, ba_ref, wx_ref, bx_ref, lam_ref, first_row_unnormalised):
    xcb = xc.astype(BF16)
    r = _block_gate(xcb, wa_ref, ba_ref)
    i = _block_gate(xcb, wx_ref, bx_ref)
    log_a = -RG_C * r * _softplus(-lam_ref[...])
    a = jnp.exp(log_a)
    mult = jnp.sqrt(-jnp.tanh(log_a) * (a * a + 1.0))
    if first_row_unnormalised is not None:
        mult = jnp.where(first_row_unnormalised, 1.0, mult)
    return a, mult * i * xc


def _scan8(a, b):
    shape = a.shape
    grouped = (shape[0] // SUBLANES, SUBLANES, shape[1])
    a = a.reshape(grouped)
    b = b.reshape(grouped)
    r8 = lax.broadcasted_iota(jnp.int32, grouped, 1)
    d = 1
    while d < SUBLANES:
        keep = r8 >= d
        a_sh = jnp.where(keep, pltpu.roll(a, d, 1), 1.0)
        b_sh = jnp.where(keep, pltpu.roll(b, d, 1), 0.0)
        b = a * b_sh + b
        a = a * a_sh
        d *= 2
    return a.reshape(shape), b.reshape(shape)


def _rnn_prompt_kernel(xr_ref, gy_ref, cw_ref, cb_ref, wa_ref, ba_ref, wx_ref, bx_ref, lam_ref,
                       o_ref, nh_ref, ext_ref, a_ref, b_ref, hc_ref, *, tl):
    l = pl.program_id(1)

    @pl.when(l == 0)
    def _():
        ext_ref[0:SUBLANES, :] = jnp.zeros((SUBLANES, D_RNN), F32)
        hc_ref[...] = jnp.zeros((1, D_RNN), F32)

    x = xr_ref[...]
    ext_ref[SUBLANES:, :] = x
    xc = cb_ref[...] + cw_ref[CONV_W - 1:CONV_W, :] * x
    for j in range(1, CONV_W):
        xc = xc + cw_ref[CONV_W - 1 - j:CONV_W - j, :] * ext_ref[SUBLANES - j:SUBLANES - j + tl, :]
    ext_ref[0:SUBLANES, :] = x[tl - SUBLANES:, :]

    row = lax.broadcasted_iota(jnp.int32, (tl, D_RNN), 0)
    a, b = _lru_coeffs(xc, wa_ref, ba_ref, wx_ref, bx_ref, lam_ref, (row == 0) & (l == 0))
    a, b = _scan8(a, b)
    a_ref[...] = a
    b_ref[...] = b

    def chunk(c, h):
        sl = pl.ds(pl.multiple_of(c * SUBLANES, SUBLANES), SUBLANES)
        hc = b_ref[sl, :] + a_ref[sl, :] * h
        b_ref[sl, :] = hc
        return hc[SUBLANES - 1:SUBLANES, :]

    h = lax.fori_loop(0, tl // SUBLANES, chunk, hc_ref[...])
    hc_ref[...] = h
    nh_ref[0] = h
    o_ref[...] = (b_ref[...] * gy_ref[...]).astype(BF16)


def _rnn_prompt(xr_all, gy_all, rnn_w, batch, seq):
    tl =_pick_tile(seq, (256, 128, 64, 32, 16, 8))
    nl = seq // tl
    tok = pl.BlockSpec((tl, D_RNN), lambda b, l: (b * nl + l, 0))
    return pl.pallas_call(
        functools.partial(_rnn_prompt_kernel, tl=tl),
        grid=(batch, nl),
        in_specs=[tok, tok] + [_const_spec(w.shape) for w in rnn_w],
        out_specs=(tok, pl.BlockSpec((1, 1, D_RNN), lambda b, l: (b, 0, 0))),
        out_shape=(jax.ShapeDtypeStruct((batch * seq, D_RNN), BF16), jax.ShapeDtypeStruct((batch, 1, D_RNN), F32)),
        scratch_shapes=[
            pltpu.VMEM((tl + SUBLANES, D_RNN), F32),
            pltpu.VMEM((tl, D_RNN), F32),
            pltpu.VMEM((tl, D_RNN), F32),
            pltpu.VMEM((1, D_RNN), F32),
        ],
        compiler_params=_params(("parallel", "arbitrary")),
        name="rnn_prompt",
    )(xr_all, gy_all, *rnn_w)


def _rnn_sample_kernel(xr_ref, gy_ref, hp_ref, h0_ref, cw_ref, cb_ref, wa_ref, ba_ref, wx_ref, bx_ref, lam_ref,
                       o_ref, nh_ref, *, seqs):
    rows = seqs * SUBLANES
    x = xr_ref[...]
    hp = hp_ref[...]
    r8 = lax.broadcasted_iota(jnp.int32, (rows, D_RNN), 0) % SUBLANES
    xc = cb_ref[...] + cw_ref[CONV_W - 1:CONV_W, :] * x
    for j in range(1, CONV_W):
        shifted = jnp.where(r8 >= j, pltpu.roll(x, j, 0), pltpu.roll(hp, rows - (SUBLANES - j), 0))
        xc = xc + cw_ref[CONV_W - 1 - j:CONV_W - j, :] * shifted
    a, b = _lru_coeffs(xc, wa_ref, ba_ref, wx_ref, bx_ref, lam_ref, None)
    a, b = _scan8(a, b)
    h0 = jnp.broadcast_to(h0_ref[...][:, None, :], (seqs, SUBLANES, D_RNN)).reshape(rows, D_RNN)
    h = b + a * h0
    last = jnp.where(r8 == SUBLANES - 1, h, 0.0).reshape(seqs, SUBLANES, D_RNN)
    nh_ref[...] = jnp.sum(last, axis=1)
    o_ref[...] = (h * gy_ref[...]).astype(BF16)


def _rnn_sample(xr_all, gy_all, hist_pad, h0, rnn_w, row0, dec_batch):
    seqs = _pick_tile(dec_batch, (16, 8))
    rows = seqs * SUBLANES
    assert row0 % rows == 0
    blk0 = row0 // rows
    tok = pl.BlockSpec((rows, D_RNN), lambda i: (blk0 + i, 0))
    return pl.pallas_call(
        functools.partial(_rnn_sample_kernel, seqs=seqs),
        grid=(dec_batch // seqs,),
        in_specs=[tok, tok, pl.BlockSpec((rows, D_RNN), lambda i: (i, 0)), pl.BlockSpec((seqs, D_RNN), lambda i: (i, 0))]
        + [_const_spec(w.shape) for w in rnn_w],
        out_specs=(pl.BlockSpec((rows, D_RNN), lambda i: (i, 0)), pl.BlockSpec((seqs, D_RNN), lambda i: (i, 0))),
        out_shape=(jax.ShapeDtypeStruct((dec_batch * SUBLANES, D_RNN), BF16),
                   jax.ShapeDtypeStruct((dec_batch, D_RNN), F32)),
        compiler_params=_params(("parallel",)),
        name="rnn_sample",
    )(xr_all, gy_all, hist_pad, h0, *rnn_w)


def _layer_norm(z, g, b):
    mu = jnp.mean(z, axis=-1, keepdims=True)
    zc = z - mu
    var = jnp.mean(zc * zc, axis=-1, keepdims=True)
    return zc * lax.rsqrt(var + LN_EPS) * g + b


def _first_index_of_max(vals, iota, axis, sentinel):
    mx = jnp.max(vals, axis=axis, keepdims=True)
    return mx, jnp.min(jnp.where(vals == mx, iota, sentinel), axis=axis, keepdims=True)


def _route(scores, bias):
    t = scores.shape[1]
    grp = scores + bias
    g3 = grp.reshape(N_GROUPS, GROUP_SIZE, t)
    e_in_g = lax.broadcasted_iota(jnp.int32, g3.shape, 1)
    m1, first = _first_index_of_max(g3, e_in_g, 1, GROUP_SIZE)
    m2 = jnp.max(jnp.where(e_in_g == first, -jnp.inf, g3), axis=1, keepdims=True)
    gscore = (m1 + m2).reshape(N_GROUPS, t)
    g_iota = lax.broadcasted_iota(jnp.int32, gscore.shape, 0)
    gmask = jnp.zeros(gscore.shape, jnp.bool_)
    for _ in range(TOPK_GROUPS):
        _, gi = _first_index_of_max(gscore, g_iota, 0, N_GROUPS)
        hit = g_iota == gi
        gmask = gmask | hit
        gscore = jnp.where(hit, -jnp.inf, gscore)
    masked = jnp.where(gmask[:, None, :], g3, -jnp.inf).reshape(N_EXPERTS, t)
    e_iota = lax.broadcasted_iota(jnp.int32, masked.shape, 0)
    idx, wts, hits = [], [], []
    for _ in range(TOP_K):
        _, ei = _first_index_of_max(masked, e_iota, 0, N_EXPERTS)
        hit = e_iota == ei
        idx.append(ei)
        hits.append(hit)
        wts.append(jnp.sum(jnp.where(hit, scores, 0.0), axis=0, keepdims=True))
        masked = jnp.where(hit, -jnp.inf, masked)
    idx = jnp.concatenate(idx, axis=0)
    w = jnp.concatenate(wts, axis=0)
    w = w / jnp.sum(w, axis=0, keepdims=True) * ROUTED_SCALE
    return idx, w, hits


def _merge_kernel(xp_ref, xs_ref, aop_ref, aos_ref, rop_ref, ros_ref, sga_ref, sgr_ref, woa_ref, wor_ref, wout_ref, g1_ref, b1_ref,
                  wrt_ref, rb_ref, wsg_ref, wsu_ref, wsd_ref,
                  x1w_ref, base_ref, idx_ref, wt_ref, rank_ref, cnt_ref, carry_ref, *, n_p, tm):
    i = pl.program_id(0)

    @pl.when(i == 0)
    def _():
        carry_ref[...] = jnp.zeros(carry_ref.shape, F32)

    is_prompt = i < n_p
    x = jnp.where(is_prompt, xp_ref[...], xs_ref[...])
    pa = jnp.dot(jnp.where(is_prompt, aop_ref[...], aos_ref[...]), woa_ref[...], preferred_element_type=F32)
    pr = jnp.dot(jnp.where(is_prompt, rop_ref[...], ros_ref[...]), wor_ref[...], preferred_element_type=F32)
    merged = sga_ref[...] * pa + sgr_ref[...] * pr
    z = ALPHA * x + jnp.dot(merged.astype(BF16), wout_ref[...], preferred_element_type=F32)
    x1 = _layer_norm(z, g1_ref[...], b1_ref[...])
    words = _pack_words(x1)
    for h in range(tm // SC_ROWS):
        for s in range(ROW_TILE):
            x1w_ref[(h * ROW_TILE + s) * SC_ROWS:(h * ROW_TILE + s + 1) * SC_ROWS, :] = (
                words[h * SC_ROWS:(h + 1) * SC_ROWS, s * LANES:(s + 1) * LANES])
    x1b = x1.astype(BF16)
    u = jax.nn.silu(jnp.dot(x1b, wsg_ref[...], preferred_element_type=F32)) * jnp.dot(x1b, wsu_ref[...], preferred_element_type=F32)
    shared = jnp.dot(u.astype(BF16), wsd_ref[...], preferred_element_type=F32)
    base_ref[...] = ALPHA * x1 + shared
    logits = lax.dot_general(wrt_ref[...], x1, (((1,), (1,)), ((), ())), preferred_element_type=F32)
    idx, w, hits = _route(jax.nn.sigmoid(logits), rb_ref[...])
    idx_ref[...] = idx
    wt_ref[...] = w

    chosen = functools.reduce(jnp.logical_or, hits)
    chosen_f = jnp.where(chosen, 1.0, 0.0)
    earlier = (lax.broadcasted_iota(jnp.int32, (tm, tm), 0) < lax.broadcasted_iota(jnp.int32, (tm, tm), 1))
    prefix = jnp.dot(chosen_f.astype(BF16), jnp.where(earlier, 1.0, 0.0).astype(BF16), preferred_element_type=F32)
    before = prefix + carry_ref[...]
    ranks = [jnp.sum(jnp.where(hit, before, 0.0), axis=0, keepdims=True) for hit in hits]
    rank_ref[...] = jnp.concatenate(ranks, axis=0).astype(jnp.int32)
    carry_ref[...] = carry_ref[...] + jnp.sum(chosen_f, axis=1, keepdims=True)
    cnt_ref[...] = carry_ref[...]


def _merge(x_p, x_s, attn_p, attn_s, rnn_p, rnn_s, sga, sgr, weights):
    t_p, t_s = x_p.shape[0], x_s.shape[0]
    t = t_p + t_s
    tm = _pick_tile(math.gcd(t_p, t_s), (256, 128))
    row = lambda width: pl.BlockSpec((tm, width), lambda i: (i, 0))
    col = pl.BlockSpec((TOP_K, tm), lambda i: (0, i))

    return pl.pallas_call(
        functools.partial(_merge_kernel, n_p=t_p // tm, tm=tm),
        grid=(t // tm,),
        in_specs=[*_two_source_specs(t_p, t_s, tm, D_MODEL), *_two_source_specs(t_p, t_s, tm, D_ATTN),
                  *_two_source_specs(t_p, t_s, tm, D_RNN), row(D_MODEL), row(D_MODEL)]
        + [_const_spec(w.shape) for w in weights],
        out_specs=(pl.BlockSpec((tm * ROW_TILE, LANES), lambda i: (i, 0)), row(D_MODEL), col, col, col,
                   _const_spec((N_EXPERTS, 1))),
        out_shape=(
            jax.ShapeDtypeStruct((t * ROW_TILE, LANES), WORD),
            jax.ShapeDtypeStruct((t, D_MODEL), F32),
            jax.ShapeDtypeStruct((TOP_K, t), jnp.int32),
            jax.ShapeDtypeStruct((TOP_K, t), F32),
            jax.ShapeDtypeStruct((TOP_K, t), jnp.int32),
            jax.ShapeDtypeStruct((N_EXPERTS, 1), F32),
        ),
        scratch_shapes=[pltpu.VMEM((N_EXPERTS, 1), F32)],
        compiler_params=_params(("arbitrary",)),
        name="merge_ln1_route",
    )(x_p, x_s, attn_p, attn_s, rnn_p, rnn_s, sga, sgr, *weights)


def _expert_layout(counts, n_assign):
    counts = counts.reshape(N_EXPERTS).astype(jnp.int32)
    padded = (counts + ROW_PAD - 1) // ROW_PAD * ROW_PAD
    pend = jnp.cumsum(padded)
    pstart = pend - padded
    rows_alloc = (n_assign + N_EXPERTS * (ROW_PAD - 1)) // ROW_PAD * ROW_PAD + EXPERT_CHUNK
    n_chunks = jnp.maximum((counts + EXPERT_CHUNK - 1) // EXPERT_CHUNK, 1)
    part = EXPERT_CHUNK // CHUNK_COPIES
    written = jnp.maximum((counts + part - 1) // part, 1) * part
    ffn_end = jnp.max(pstart + written).reshape(1)
    return pstart, pstart + counts, pend, n_chunks, counts, ffn_end, rows_alloc


def _dest_kernel(idx_ref, rank_ref, pstart_ref, dest_ref, word_rows_ref):
    e_iota = lax.broadcasted_iota(jnp.int32, (N_EXPERTS, idx_ref.shape[1]), 0)
    starts = [jnp.sum(jnp.where(e_iota == idx_ref[k:k + 1, :], pstart_ref[...], 0), axis=0, keepdims=True)
              for k in range(TOP_K)]
    dest = jnp.concatenate(starts, axis=0) + rank_ref[...]
    dest_ref[...] = dest
    tm = dest.shape[1]
    word_rows_ref[0] = jnp.concatenate(
        [dest[k:k + 1, h * SC_ROWS:(h + 1) * SC_ROWS] * ROW_TILE + s
         for h in range(tm // SC_ROWS) for s in range(ROW_TILE) for k in range(TOP_K)], axis=0)


def _dest_rows(idx_t, rank_t, pstart, tm):
    t = idx_t.shape[1]
    col = pl.BlockSpec((TOP_K, tm), lambda i: (0, i))
    vecs = tm // SC_ROWS * ROW_TILE * TOP_K
    return pl.pallas_call(
        _dest_kernel,
        grid=(t // tm,),
        in_specs=[col, col, _const_spec((N_EXPERTS, 1))],
        out_specs=(col, pl.BlockSpec((1, vecs, SC_ROWS), lambda i: (i, 0, 0))),
        out_shape=(jax.ShapeDtypeStruct((TOP_K, t), jnp.int32),
                   jax.ShapeDtypeStruct((t // tm, vecs, SC_ROWS), jnp.int32)),
        compiler_params=_params(("parallel",)),
        name="dest_rows",
    )(idx_t, rank_t, pstart.reshape(N_EXPERTS, 1))


def _token_rows(r, n=1):
    return pl.ds(pl.multiple_of(r * ROW_TILE, ROW_TILE), n * ROW_TILE)


def _zero_row_groups(zero_ref, dst_ref, sem, first_group, n_groups):
    def start(g, c):
        pltpu.make_async_copy(zero_ref, dst_ref.at[_token_rows(g * ROW_PAD, ROW_PAD)], sem).start()
        return c

    lax.fori_loop(first_group, n_groups, start, 0)
    return n_groups - first_group


def _wait_zero_copies(zero_ref, dst_ref, sem, n):
    def wait(_, c):
        pltpu.make_async_copy(zero_ref, dst_ref.at[_token_rows(0, ROW_PAD)], sem).wait()
        return c

    lax.fori_loop(0, n, wait, 0)


SC_ROWS = 128


def _sc_workers():
    info = plsc.get_sparse_core_info()
    mesh = plsc.VectorSubcoreMesh(core_axis_name="c", subcore_axis_name="s")
    worker = lambda: lax.axis_index("s") * info.num_cores + lax.axis_index("c")
    return mesh, info.num_cores * info.num_subcores, worker


def _sc_scatter_rows(src, rows, n_out):
    mesh, n_workers, worker = _sc_workers()
    n_units = src.shape[0] // SC_ROWS
    assert n_units % n_workers == 0 and rows.shape == (n_units * TOP_K, SC_ROWS)
    per_worker = n_units // n_workers

    unit_bufs = [pltpu.VMEM((TOP_K, SC_ROWS), jnp.int32), pltpu.VMEM((SC_ROWS, LANES), src.dtype),
                 pltpu.SemaphoreType.DMA]

    @functools.partial(
        pl.kernel, mesh=mesh,
        out_type=jax.ShapeDtypeStruct((n_out, LANES), src.dtype),
        scratch_types=unit_bufs + unit_bufs + [pltpu.SemaphoreType.DMA],
    )
    def scatter(src_hbm, rows_hbm, out_hbm, rows_a, data_a, load_a, rows_b, data_b, load_b, sem):
        first = worker() * per_worker
        bufs = ((rows_a, data_a, load_a), (rows_b, data_b, load_b))

        def loads(u, buf):
            rows_v, data_v, load_sem = buf
            return (pltpu.make_async_copy(rows_hbm.at[pl.ds(u * TOP_K, TOP_K)], rows_v, load_sem),
                    pltpu.make_async_copy(src_hbm.at[pl.ds(u * SC_ROWS, SC_ROWS)], data_v, load_sem))

        def unit(u, buf, other, has_next):
            for cp in loads(u, buf):
                cp.wait()

            @pl.when(has_next)
            def _():
                for cp in loads(u + 1, other):
                    cp.start()

            rows_v, data_v, _ = buf
            copies = [pltpu.async_copy(data_v, out_hbm.at[rows_v.at[k]], sem) for k in range(TOP_K)]
            for cp in copies:
                cp.wait()

        for cp in loads(first, bufs[0]):
            cp.start()

        @pl.loop(0, per_worker // 2)
        def _(pair):
            u = first + 2 * pair
            unit(u, bufs[0], bufs[1], True)
            unit(u + 1, bufs[1], bufs[0], 2 * pair + 2 < per_worker)

        if per_worker % 2:
            unit(first + per_worker - 1, bufs[0], bufs[1], False)

    return scatter(src, rows)


def _zero_padding_kernel(lo_ref, hi_ref, tail_ref, xs_in_ref, xs_ref, zero_ref, sem, *, rows_alloc):
    del xs_in_ref
    zero_ref[...] = jnp.zeros(zero_ref.shape, WORD)
    row_copy = lambda r: pltpu.make_async_copy(zero_ref.at[pl.ds(0, ROW_TILE)], xs_ref.at[_token_rows(r)], sem)

    def expert(e, n):
        def row(r, c):
            row_copy(r).start()
            return c

        lax.fori_loop(lo_ref[e], hi_ref[e], row, 0)
        return n + hi_ref[e] - lo_ref[e]

    def wait(_, c):
        row_copy(0).wait()
        return c

    lax.fori_loop(0, lax.fori_loop(0, N_EXPERTS, expert, 0), wait, 0)
    n = _zero_row_groups(zero_ref, xs_ref, sem, tail_ref[0] // ROW_PAD, rows_alloc // ROW_PAD)
    _wait_zero_copies(zero_ref, xs_ref, sem, n)


def _zero_padding(xs, pad_lo, pad_hi, total, rows_alloc):
    grid_spec = pltpu.PrefetchScalarGridSpec(
        num_scalar_prefetch=3,
        grid=(1,),
        in_specs=[pl.BlockSpec(memory_space=pl.ANY)],
        out_specs=pl.BlockSpec(memory_space=pl.ANY),
        scratch_shapes=[pltpu.VMEM((ROW_PAD * ROW_TILE, LANES), WORD), pltpu.SemaphoreType.DMA],
    )
    return pl.pallas_call(
        functools.partial(_zero_padding_kernel, rows_alloc=rows_alloc),
        grid_spec=grid_spec,
        out_shape=jax.ShapeDtypeStruct(xs.shape, xs.dtype),
        input_output_aliases={3: 0},
        compiler_params=_params(("arbitrary",)),
        name="zero_padding",
    )(pad_lo, pad_hi, total, xs)


def _expert_kernel(pstart_ref, nch_ref, rows_ref, end_ref, wg_ref, wu_ref, wd_ref, xs_ref, o_ref,
                   xbuf_ref, obuf_ref, wgb_ref, wub_ref, wdb_ref, zero_ref, done_ref, in_sem, out_sem, *, rows_alloc):
    e = pl.program_id(0)
    n_e = pl.num_programs(0)
    start = pstart_ref[e]
    nch = nch_ref[e]
    buf_rows = EXPERT_CHUNK * ROW_TILE

    part = EXPERT_CHUNK // CHUNK_COPIES
    part_rows = part * ROW_TILE

    class _ChunkCopy:
        def __init__(self, copies, n_parts):
            self.copies = copies
            self.n_parts = n_parts

        def _each(self, act):
            for p, cp in enumerate(self.copies):
                pl.when(p < self.n_parts)(functools.partial(act, cp))

        def start(self):
            self._each(lambda cp: cp.start())

        def wait(self):
            self._each(lambda cp: cp.wait())

    def parts_of(rows_left):
        return jnp.clip((rows_left + part - 1) // part, 1, CHUNK_COPIES)

    def in_copy(row, slot, n_parts):
        return _ChunkCopy([
            pltpu.make_async_copy(xs_ref.at[_token_rows(row + p * part, part)],
                                  xbuf_ref.at[pl.ds(slot * buf_rows + p * part_rows, part_rows)], in_sem.at[slot])
            for p in range(CHUNK_COPIES)], n_parts)

    def out_copy(row, slot, n_parts):
        return _ChunkCopy([
            pltpu.make_async_copy(obuf_ref.at[pl.ds(slot * buf_rows + p * part_rows, part_rows)],
                                  o_ref.at[_token_rows(row + p * part, part)], out_sem)
            for p in range(CHUNK_COPIES)], n_parts)

    @pl.when(e == 0)
    def _():
        done_ref[0] = 0
        xbuf_ref[...] = jnp.zeros(xbuf_ref.shape, WORD)
        obuf_ref[...] = jnp.zeros(obuf_ref.shape, WORD)
        in_copy(start, 0, parts_of(rows_ref[0])).start()

    wgb_ref[...] = wg_ref[...].astype(BF16)
    wub_ref[...] = wu_ref[...].astype(BF16)
    wdb_ref[...] = wd_ref[...].astype(BF16)
    done = done_ref[0]

    def chunk(c, carry):
        g = done + c
        slot = g % 2
        row = start + c * EXPERT_CHUNK
        rows_left = rows_ref[e] - c * EXPERT_CHUNK
        n_parts = parts_of(rows_left)
        in_copy(row, slot, n_parts).wait()
        last = c + 1 == nch
        next_e = jnp.minimum(e + 1, n_e - 1)
        next_row = jnp.where(last, pstart_ref[next_e], row + EXPERT_CHUNK)
        next_left = jnp.where(last, rows_ref[next_e], rows_left - EXPERT_CHUNK)

        @pl.when(jnp.logical_not(last & (e == n_e - 1)))
        def _():
            in_copy(next_row, 1 - slot, parts_of(next_left)).start()

        base = pl.multiple_of(slot * buf_rows, buf_rows)
        def sub_block(h):
            xb = jnp.concatenate(_load_token_rows(xbuf_ref, EXPERT_ROWS, row0=h * EXPERT_ROWS, base=base),
                                 axis=1).astype(BF16)
            gate = jnp.dot(xb, wgb_ref[...], preferred_element_type=F32)
            up = jnp.dot(xb, wub_ref[...], preferred_element_type=F32)
            act = (jax.nn.silu(gate) * up).astype(BF16)
            _store_token_rows(obuf_ref, jnp.dot(act, wdb_ref[...], preferred_element_type=F32),
                              EXPERT_ROWS, row0=h * EXPERT_ROWS, base=base)

        n_sub = EXPERT_CHUNK // EXPERT_ROWS
        needed = jnp.clip((rows_left + EXPERT_ROWS - 1) // EXPERT_ROWS, 1, n_sub)
        for count in range(1, n_sub + 1):
            @pl.when(needed == count)
            def _():
                for h in range(count):
                    sub_block(h)

        @pl.when(g > 0)
        def _():
            out_copy(0, 0, done_ref[1]).wait()

        out_copy(row, slot, n_parts).start()
        done_ref[1] = n_parts
        return carry

    lax.fori_loop(0, nch, chunk, 0)
    done_ref[0] = done + nch

    @pl.when(e == n_e - 1)
    def _():
        out_copy(0, 0, done_ref[1]).wait()
        zero_ref[...] = jnp.zeros(zero_ref.shape, WORD)
        n = _zero_row_groups(zero_ref, o_ref, out_sem, end_ref[0] // ROW_PAD, rows_alloc // ROW_PAD)
        _wait_zero_copies(zero_ref, o_ref, out_sem, n)


def _expert_ffn(xs, pstart, n_chunks, rows, ffn_end, rows_alloc, w_e_gate, w_e_up, w_e_down):
    weight = lambda shape: pl.BlockSpec((None, *shape), lambda e, *_: (e, 0, 0))
    grid_spec = pltpu.PrefetchScalarGridSpec(
        num_scalar_prefetch=4,
        grid=(N_EXPERTS,),
        in_specs=[weight((D_MODEL, D_EXPERT)), weight((D_MODEL, D_EXPERT)), weight((D_EXPERT, D_MODEL)),
                  pl.BlockSpec(memory_space=pl.ANY)],
        out_specs=pl.BlockSpec(memory_space=pl.ANY),
        scratch_shapes=[
            pltpu.VMEM((2 * EXPERT_CHUNK * ROW_TILE, LANES), WORD),
            pltpu.VMEM((2 * EXPERT_CHUNK * ROW_TILE, LANES), WORD),
            pltpu.VMEM((D_MODEL, D_EXPERT), BF16), pltpu.VMEM((D_MODEL, D_EXPERT), BF16),
            pltpu.VMEM((D_EXPERT, D_MODEL), BF16),
            pltpu.VMEM((ROW_PAD * ROW_TILE, LANES), WORD),
            pltpu.SMEM((2,), jnp.int32),
            pltpu.SemaphoreType.DMA((2,)), pltpu.SemaphoreType.DMA,
        ],
    )
    return pl.pallas_call(
        functools.partial(_expert_kernel, rows_alloc=rows_alloc),
        grid_spec=grid_spec,
        out_shape=jax.ShapeDtypeStruct((rows_alloc * ROW_TILE, LANES), WORD),
        compiler_params=_params(("arbitrary",)),
        name="expert_ffn",
    )(pstart, n_chunks, rows, ffn_end, w_e_gate, w_e_up, w_e_down, xs)


def _combine_head_kernel(dest_ref, dest_next_ref, w_ref, base_ref, g_ref, b_ref, outs_ref, y_ref, buf_ref, sem, *, tm):
    i = pl.program_id(0)
    slot_rows = TOP_K * tm

    def gather(d_ref, slot):
        def issue(t, c):
            for k in range(TOP_K):
                pltpu.make_async_copy(outs_ref.at[_token_rows(d_ref[k, t])],
                                      buf_ref.at[_token_rows(slot * slot_rows + k * tm + t)], sem.at[slot]).start()
            return c

        lax.fori_loop(0, tm, issue, 0)

    @pl.when(i == 0)
    def _():
        gather(dest_ref, 0)

    @pl.when(i + 1 < pl.num_programs(0))
    def _():
        gather(dest_next_ref, (i + 1) % 2)

    slot = i % 2
    for k in range(TOP_K):
        pltpu.make_async_copy(outs_ref.at[_token_rows(0, tm)], buf_ref.at[_token_rows(slot * slot_rows + k * tm, tm)],
                              sem.at[slot]).wait()

    w = w_ref[...]
    base = pl.multiple_of(slot * slot_rows * ROW_TILE, ROW_TILE)
    halves = [None, None]
    for k in range(TOP_K):
        wk = w[:, k:k + 1]
        for j, rows in enumerate(_load_token_rows(buf_ref, tm, row0=k * tm, base=base)):
            halves[j] = wk * rows if halves[j] is None else halves[j] + wk * rows
    y_ref[...] = _layer_norm(base_ref[...] + jnp.concatenate(halves, axis=1), g_ref[...], b_ref[...])


def _combine_head(out_sorted, dest, w_tok, base, g2, b2, tm, n_head):
    row = pl.BlockSpec((tm, D_MODEL), lambda i: (i, 0))
    return pl.pallas_call(
        functools.partial(_combine_head_kernel, tm=tm),
        grid=(n_head,),
        in_specs=[
            pl.BlockSpec((TOP_K, tm), lambda i: (0, i), memory_space=pltpu.SMEM),
            pl.BlockSpec((TOP_K, tm), lambda i: (0, jnp.minimum(i + 1, n_head - 1)), memory_space=pltpu.SMEM),
            pl.BlockSpec((tm, TOP_K), lambda i: (i, 0)),
            row, _const_spec((1, D_MODEL)), _const_spec((1, D_MODEL)),
            pl.BlockSpec(memory_space=pl.ANY),
        ],
        out_specs=row,
        out_shape=jax.ShapeDtypeStruct((n_head * tm, D_MODEL), F32),
        scratch_shapes=[pltpu.VMEM((2 * TOP_K * tm * ROW_TILE, LANES), WORD), pltpu.SemaphoreType.DMA((2,))],
        compiler_params=_params(("arbitrary",)),
        name="combine_head",
    )(dest, dest, w_tok, base, g2, b2, out_sorted)


def _sc_gather_rows(table, rows):
    mesh, n_workers, worker = _sc_workers()
    m = rows.shape[0]
    per_worker = m // n_workers
    assert m % n_workers == 0 and per_worker % SC_ROWS == 0
    in_flight = _pick_tile(per_worker // SC_ROWS, (4, 2, 1))
    step = SC_ROWS * in_flight

    @functools.partial(
        pl.kernel, mesh=mesh,
        out_type=jax.ShapeDtypeStruct((m, LANES), table.dtype),
        scratch_types=[pltpu.VMEM((step,), jnp.int32), pltpu.VMEM((step, LANES), table.dtype), pltpu.SemaphoreType.DMA],
    )
    def gather(table_hbm, rows_hbm, out_hbm, rows_v, data_v, sem):
        first = worker() * per_worker

        @pl.loop(0, per_worker // step)
        def _(it):
            off = first + it * step
            pltpu.sync_copy(rows_hbm.at[pl.ds(off, step)], rows_v)
            copies = [pltpu.async_copy(table_hbm.at[rows_v.at[pl.ds(j * SC_ROWS, SC_ROWS)]],
                                       data_v.at[pl.ds(j * SC_ROWS, SC_ROWS)], sem)
                      for j in range(in_flight)]
            for cp in copies:
                cp.wait()
            pltpu.sync_copy(data_v, out_hbm.at[pl.ds(off, step)])

    return gather(table, rows)


def _combine_tail_kernel(yh_ref, rows_ref, w_ref, base_ref, g_ref, b_ref, yp_ref, ys_ref, *, tm, n_head, n_p):
    i = pl.program_id(0)

    @pl.when(i < n_head)
    def _():
        yp_ref[...] = yh_ref[...]

    @pl.when(i >= n_head)
    def _():
        w = w_ref[...]
        lo = [None] * ROW_TILE
        hi = [None] * ROW_TILE
        for k in range(TOP_K):
            wk = w[:, k:k + 1]
            for s in range(ROW_TILE):
                vec = lambda h: ((h * ROW_TILE + s) * TOP_K + k) * SC_ROWS
                words = jnp.concatenate([rows_ref[vec(h):vec(h) + SC_ROWS, :] for h in range(tm // SC_ROWS)], axis=0)
                lo_s, hi_s = _unpack_words(words)
                lo[s] = wk * lo_s if lo[s] is None else lo[s] + wk * lo_s
                hi[s] = wk * hi_s if hi[s] is None else hi[s] + wk * hi_s
        y = _layer_norm(base_ref[...] + jnp.concatenate(lo + hi, axis=1), g_ref[...], b_ref[...])

        @pl.when(i < n_p)
        def _():
            yp_ref[...] = y

        @pl.when(i >= n_p)
        def _():
            ys_ref[...] = y


def _combine_tail(y_head, gathered, w_tok, base, g2, b2, t_p, t_s, tm, n_head):
    n_p = t_p // tm
    assert n_head <= n_p
    n_tiles = (t_p + t_s) // tm
    blk = TOP_K * ROW_TILE * tm
    row = pl.BlockSpec((tm, D_MODEL), lambda i: (i, 0))
    out_p, out_s = _two_source_specs(t_p, t_s, tm, D_MODEL)
    return pl.pallas_call(
        functools.partial(_combine_tail_kernel, tm=tm, n_head=n_head, n_p=n_p),
        grid=(n_tiles,),
        in_specs=[
            pl.BlockSpec((tm, D_MODEL), lambda i: (jnp.minimum(i, n_head - 1), 0)),
            pl.BlockSpec((blk, LANES), lambda i: (jnp.maximum(i - n_head, 0), 0)),
            pl.BlockSpec((tm, TOP_K), lambda i: (i, 0)),
            row, _const_spec((1, D_MODEL)), _const_spec((1, D_MODEL)),
        ],
        out_specs=(out_p, out_s),
        out_shape=(jax.ShapeDtypeStruct((t_p, D_MODEL), F32), jax.ShapeDtypeStruct((t_s, D_MODEL), F32)),
        compiler_params=_params(("arbitrary",)),
        name="combine_tail",
    )(y_head, gathered, w_tok, base, g2, b2)


def kernel(x_prompt, x_sample, cache_k, cache_v, state_conv, state_rnn, w_in, conv_w, conv_b, w_gate_a, b_gate_a, w_gate_x, b_gate_x, lru_lambda, rel_bias, sinks, w_o_attn, w_o_rnn, w_out, ln1_g, ln1_b, w_router, router_bias, w_e_gate, w_e_up, w_e_down, w_s_gate, w_s_up, w_s_down, ln2_g, ln2_b):
    assert w_in.shape[0] == DEPTH == 1
    batch, seq, _ = x_prompt.shape
    dec_batch, s_len, _ = x_sample.shape
    w_cache = cache_k.shape[2]
    assert s_len == SUBLANES and seq % WINDOW == 0 and w_cache == WINDOW
    t_p = batch * seq
    t_s = dec_batch * s_len
    vec = lambda a: a[0].reshape(1, -1).astype(F32)

    x_p = x_prompt.reshape(t_p, D_MODEL)
    x_s = x_sample.reshape(t_s, D_MODEL)
    q, k, v, xr, gy, sga, sgr = _inproj(x_p, x_s, w_in[0].astype(BF16))

    qi = jnp.arange(WINDOW)
    dist = qi[:, None] + WINDOW - jnp.arange(2 * WINDOW)[None, :]
    bias_p = _bias_table(rel_bias, dist)
    attn_p = _attn_prompt(q, k, v, bias_p, sinks[0], batch, seq)
    attn_s, k_s, v_s = _attn_sample(
        q, k, v, cache_k[0].reshape(dec_batch, w_cache, D_KV), cache_v[0].reshape(dec_batch, w_cache, D_KV),
        rel_bias, sinks[0], t_p, dec_batch, s_len)

    rnn_w = (conv_w[0], vec(conv_b), w_gate_a[0].astype(BF16), vec(b_gate_a), w_gate_x[0].astype(BF16),
             vec(b_gate_x), vec(lru_lambda))
    rnn_p, h_p = _rnn_prompt(xr, gy, rnn_w, batch, seq)
    hist_pad = jnp.pad(state_conv[0], ((0, 0), (SUBLANES - (CONV_W - 1), 0), (0, 0))).reshape(t_s, D_RNN)
    rnn_s, h_s = _rnn_sample(xr, gy, hist_pad, state_rnn[0], rnn_w, t_p, dec_batch)

    merge_w = (w_o_attn[0].astype(BF16), w_o_rnn[0].astype(BF16), w_out[0].astype(BF16), vec(ln1_g), vec(ln1_b),
               w_router[0].T, router_bias[0].reshape(N_EXPERTS, 1), w_s_gate[0].astype(BF16),
               w_s_up[0].astype(BF16), w_s_down[0].astype(BF16))
    x1w, base, idx_t, wt_t, rank_t, counts = _merge(x_p, x_s, attn_p, attn_s, rnn_p, rnn_s, sga, sgr, merge_w)

    pstart, pad_lo, pad_hi, n_chunks, rows, ffn_end, rows_alloc = _expert_layout(counts, (t_p + t_s) * TOP_K)
    tm = _pick_tile(math.gcd(t_p, t_s), (256, 128))
    dest, word_rows = _dest_rows(idx_t, rank_t, pstart, tm)
    xs = _sc_scatter_rows(x1w, word_rows.reshape(-1, SC_ROWS), rows_alloc * ROW_TILE)
    xs = _zero_padding(xs, pad_lo, pad_hi, pad_hi[N_EXPERTS - 1:], rows_alloc)
    out_sorted = _expert_ffn(xs, pstart, n_chunks, rows, ffn_end, rows_alloc, w_e_gate[0], w_e_up[0], w_e_down[0])
    n_tiles = (t_p + t_s) // tm
    n_head = min(t_p // tm, n_tiles // 3)
    w_tok = wt_t.T
    g2, b2 = vec(ln2_g), vec(ln2_b)
    y_head = _combine_head(out_sorted, dest, w_tok, base, g2, b2, tm, n_head)
    gathered = _sc_gather_rows(out_sorted, word_rows[n_head:].reshape(-1))
    y_p, y_s = _combine_tail(y_head, gathered, w_tok, base, g2, b2, t_p, t_s, tm, n_head)
    y_p = y_p.reshape(batch, seq, D_MODEL)
    y_s = y_s.reshape(dec_batch, s_len, D_MODEL)
    kv5 = lambda a, b: a.reshape(1, b, WINDOW, N_KV_HEADS, HEAD_DIM)
    tail = lambda a, n: jnp.stack([lax.slice_in_dim(a, (b + 1) * seq - n, (b + 1) * seq) for b in range(batch)])
    k_p = kv5(tail(k, WINDOW), batch)
    v_p = kv5(tail(v, WINDOW), batch)
    conv_p = tail(xr, CONV_W - 1)[None]
    conv_s = xr[t_p:].reshape(dec_batch, s_len, D_RNN)[:, s_len - (CONV_W - 1):][None]
    return (y_p, y_s, k_p, v_p, conv_p, h_p.reshape(1, batch, D_RNN),
            kv5(k_s, dec_batch), kv5(v_s, dec_batch), conv_s, h_s.reshape(1, dec_batch, D_RNN))
```

```python
import functools
import math

import jax
import jax.numpy as jnp
from jax import lax
from jax.experimental import pallas as pl
from jax.experimental.pallas import tpu as pltpu
from jax.experimental.pallas import tpu_sc as plsc

F32 = jnp.float32
BF16 = jnp.bfloat16
WORD = jnp.int32

D_MODEL = 1024
N_HEADS = 8
N_KV_HEADS = 2
HEAD_DIM = 64
GROUP = N_HEADS // N_KV_HEADS
WINDOW = 128
D_ATTN = N_HEADS * HEAD_DIM
D_KV = N_KV_HEADS * HEAD_DIM
N_BUCKETS = 32
MAX_DISTANCE = 128
D_RNN = D_MODEL
RNN_BLOCK = 256
N_RNN_BLOCKS = D_RNN // RNN_BLOCK
CONV_W = 4
RG_C = 8.0
N_EXPERTS = 256
TOP_K = 8
N_GROUPS = 8
GROUP_SIZE = N_EXPERTS // N_GROUPS
TOPK_GROUPS = 4
D_EXPERT = D_MODEL // 4
ROUTED_SCALE = 2.5
LN_EPS = 1e-5
DEPTH = 1
ALPHA = (2 * DEPTH) ** 0.25
NEG_INF = -1e30
SM_SCALE = HEAD_DIM ** -0.5

O_Q = 0
O_K = D_ATTN
O_V = O_K + D_KV
O_XR = O_V + D_KV
O_YR = O_XR + D_RNN
O_GA = O_YR + D_RNN
O_GR = O_GA + D_MODEL
D_IN = O_GR + D_MODEL

SUBLANES = 8
VMEM_LIMIT_BYTES = 56 * 1024 * 1024
EXPERT_ROWS = 256
EXPERT_CHUNK = 1024
CHUNK_COPIES = 8
ROW_PAD = SUBLANES


def _params(sem):
    return pltpu.CompilerParams(dimension_semantics=sem, vmem_limit_bytes=VMEM_LIMIT_BYTES)


def _pick_tile(n, candidates):
    for c in candidates:
        if n % c == 0:
            return c
    raise ValueError(f"no tile for {n}")


def _const_spec(shape):
    nd = len(shape)
    return pl.BlockSpec(shape, lambda *_: (0,) * nd)


LANES = 128
ROW_WORDS = D_MODEL // 2
ROW_TILE = ROW_WORDS // LANES
HIGH_HALF = -65536


def _pack_words(mat):
    as_bits = lambda v: pltpu.bitcast(v.astype(BF16).astype(F32), WORD)
    return (as_bits(mat[:, ROW_WORDS:]) & HIGH_HALF) | lax.shift_right_logical(as_bits(mat[:, :ROW_WORDS]), 16)


def _unpack_words(words):
    return pltpu.bitcast(words << 16, F32), pltpu.bitcast(words & HIGH_HALF, F32)


def _store_token_rows(ref, mat, n, row0=0, base=0):
    words = _pack_words(mat)
    for s in range(ROW_TILE):
        ref[pl.ds(base + row0 * ROW_TILE + s, n, stride=ROW_TILE), :] = words[:, s * LANES:(s + 1) * LANES]


def _load_token_rows(ref, n, row0=0, base=0):
    words = jnp.concatenate(
        [ref[pl.ds(base + row0 * ROW_TILE + s, n, stride=ROW_TILE), :] for s in range(ROW_TILE)], axis=1)
    return _unpack_words(words)


def _two_source_specs(t_p, t_s, tm, width):
    n_p = t_p // tm
    assert t_p % tm == 0 and t_s % tm == 0
    return (pl.BlockSpec((tm, width), lambda i, *_: (jnp.minimum(i, n_p - 1), 0)),
            pl.BlockSpec((tm, width), lambda i, *_: (jnp.maximum(i - n_p, 0), 0)))


def _inproj_kernel(xp_ref, xs_ref, w_ref, q_ref, k_ref, v_ref, xr_ref, gy_ref, sga_ref, sgr_ref, *, n_p):
    x = jnp.where(pl.program_id(0) < n_p, xp_ref[...], xs_ref[...]).astype(BF16)

    def seg(lo, hi):
        return jnp.dot(x, w_ref[:, lo:hi], preferred_element_type=F32)

    q_ref[...] = seg(O_Q, O_K).astype(BF16)
    k_ref[...] = seg(O_K, O_V)
    v_ref[...] = seg(O_V, O_XR)
    xr_ref[...] = seg(O_XR, O_YR)
    gy_ref[...] = jax.nn.gelu(seg(O_YR, O_GA)).astype(BF16)
    sga_ref[...] = jax.nn.sigmoid(seg(O_GA, O_GR)).astype(BF16)
    sgr_ref[...] = jax.nn.sigmoid(seg(O_GR, D_IN)).astype(BF16)


def _inproj(x_p, x_s, w_in_bf16):
    t_p, t_s = x_p.shape[0], x_s.shape[0]
    t = t_p + t_s
    tm = _pick_tile(math.gcd(t_p, t_s), (256, 128, 64, 32, 16, 8))
    row = lambda width: pl.BlockSpec((tm, width), lambda i: (i, 0))
    out_shape = (
        jax.ShapeDtypeStruct((t, D_ATTN), BF16),
        jax.ShapeDtypeStruct((t, D_KV), F32),
        jax.ShapeDtypeStruct((t, D_KV), F32),
        jax.ShapeDtypeStruct((t, D_RNN), F32),
        jax.ShapeDtypeStruct((t, D_RNN), BF16),
        jax.ShapeDtypeStruct((t, D_MODEL), BF16),
        jax.ShapeDtypeStruct((t, D_MODEL), BF16),
    )
    return pl.pallas_call(
        functools.partial(_inproj_kernel, n_p=t_p // tm),
        grid=(t // tm,),
        in_specs=[*_two_source_specs(t_p, t_s, tm, D_MODEL), _const_spec((D_MODEL, D_IN))],
        out_specs=(row(D_ATTN), row(D_KV), row(D_KV), row(D_RNN), row(D_RNN), row(D_MODEL), row(D_MODEL)),
        out_shape=out_shape,
        compiler_params=_params(("parallel",)),
        name="inproj",
    )(x_p, x_s, w_in_bf16)


def _t5_bucket(dist):
    n = jnp.maximum(dist, 0)
    max_exact = N_BUCKETS // 2
    nf = jnp.maximum(n, 1).astype(F32)
    large = max_exact + (jnp.log(nf / max_exact) / math.log(MAX_DISTANCE / max_exact) * (N_BUCKETS - max_exact)).astype(jnp.int32)
    large = jnp.minimum(large, N_BUCKETS - 1)
    return jnp.where(n < max_exact, n, large)


def _bias_table(rel_bias, dist):
    bucket = _t5_bucket(dist)
    rb = rel_bias.astype(F32)
    out = jnp.zeros((N_HEADS, *dist.shape), F32)
    for j in range(N_BUCKETS):
        out = jnp.where(bucket[None] == j, rb[j][:, None, None], out)
    return out


def _softmax_pv(s, sink, v):
    m = jnp.maximum(jnp.max(s, axis=-1, keepdims=True), sink)
    p = jnp.exp(s - m)
    denom = jnp.sum(p, axis=-1, keepdims=True) + jnp.exp(sink - m)
    return jnp.dot(p.astype(BF16), v, preferred_element_type=F32), denom


def _attn_prompt_kernel(sink_ref, q_ref, kc_ref, kp_ref, vc_ref, vp_ref, bias_ref, o_ref):
    n = pl.program_id(1)
    kk = jnp.concatenate([kp_ref[...], kc_ref[...]], axis=0).astype(BF16)
    vv = jnp.concatenate([vp_ref[...], vc_ref[...]], axis=0).astype(BF16)
    q = q_ref[...]
    rows = lax.broadcasted_iota(jnp.int32, (WINDOW, 2 * WINDOW), 0)
    cols = lax.broadcasted_iota(jnp.int32, (WINDOW, 2 * WINDOW), 1)
    dist = rows + WINDOW - cols
    valid = (dist >= 0) & (dist <= WINDOW) & ((n > 0) | (cols >= WINDOW))
    for g in range(N_KV_HEADS):
        kg = kk[:, g * HEAD_DIM:(g + 1) * HEAD_DIM]
        vg = vv[:, g * HEAD_DIM:(g + 1) * HEAD_DIM]
        for h in range(GROUP):
            hh = g * GROUP + h
            qh = q[:, hh * HEAD_DIM:(hh + 1) * HEAD_DIM]
            s = lax.dot_general(qh, kg, (((1,), (1,)), ((), ())), preferred_element_type=F32) * SM_SCALE
            s = jnp.where(valid, s + bias_ref[hh], NEG_INF)
            o, denom = _softmax_pv(s, sink_ref[0, hh], vg)
            o_ref[:, hh * HEAD_DIM:(hh + 1) * HEAD_DIM] = (o / denom).astype(BF16)


def _attn_prompt(q_all, k_all, v_all, bias, sinks, batch, seq):
    nb = seq // WINDOW
    cur = lambda width: pl.BlockSpec((WINDOW, width), lambda b, n: (b * nb + n, 0))
    prev = lambda width: pl.BlockSpec((WINDOW, width), lambda b, n: (b * nb + jnp.maximum(n - 1, 0), 0))
    return pl.pallas_call(
        _attn_prompt_kernel,
        grid=(batch, nb),
        in_specs=[
            pl.BlockSpec(memory_space=pltpu.SMEM),
            cur(D_ATTN), cur(D_KV), prev(D_KV), cur(D_KV), prev(D_KV),
            _const_spec((N_HEADS, WINDOW, 2 * WINDOW)),
        ],
        out_specs=cur(D_ATTN),
        out_shape=jax.ShapeDtypeStruct((batch * seq, D_ATTN), BF16),
        compiler_params=_params(("parallel", "arbitrary")),
        name="attn_prompt",
    )(sinks.reshape(1, N_HEADS).astype(F32), q_all, k_all, k_all, v_all, v_all, bias)


def _attn_sample_kernel(q_ref, kn_ref, vn_ref, kc_ref, vc_ref, bc_ref, bn_ref, sink_ref,
                        o_ref, ko_ref, vo_ref, *, seqs, s_len):
    w = kc_ref.shape[1]
    rows_c = lax.broadcasted_iota(jnp.int32, (GROUP * s_len, w), 0) % s_len
    cols_c = lax.broadcasted_iota(jnp.int32, (GROUP * s_len, w), 1)
    dist_c = rows_c + w - cols_c
    valid_c = (dist_c >= 0) & (dist_c <= WINDOW)
    rows_n = lax.broadcasted_iota(jnp.int32, (GROUP * s_len, s_len), 0) % s_len
    cols_n = lax.broadcasted_iota(jnp.int32, (GROUP * s_len, s_len), 1)
    dist_n = rows_n - cols_n
    valid_n = (dist_n >= 0) & (dist_n <= WINDOW)
    for j in range(seqs):
        r0 = j * s_len
        qj = q_ref[r0:r0 + s_len, :]
        kc = kc_ref[j]
        vc = vc_ref[j]
        kn = kn_ref[r0:r0 + s_len, :]
        vn = vn_ref[r0:r0 + s_len, :]
        ko_ref[j, 0:w - s_len, :] = kc[s_len:, :]
        ko_ref[j, w - s_len:w, :] = kn
        vo_ref[j, 0:w - s_len, :] = vc[s_len:, :]
        vo_ref[j, w - s_len:w, :] = vn
        kcb, vcb, knb, vnb = kc.astype(BF16), vc.astype(BF16), kn.astype(BF16), vn.astype(BF16)
        for g in range(N_KV_HEADS):
            lo, hi = g * HEAD_DIM, (g + 1) * HEAD_DIM
            qs = jnp.concatenate(
                [qj[:, (g * GROUP + h) * HEAD_DIM:(g * GROUP + h + 1) * HEAD_DIM] for h in range(GROUP)], axis=0)
            nt = (((1,), (1,)), ((), ()))
            s_c = lax.dot_general(qs, kcb[:, lo:hi], nt, preferred_element_type=F32) * SM_SCALE
            s_n = lax.dot_general(qs, knb[:, lo:hi], nt, preferred_element_type=F32) * SM_SCALE
            s_c = jnp.where(valid_c, s_c + bc_ref[g], NEG_INF)
            s_n = jnp.where(valid_n, s_n + bn_ref[g], NEG_INF)
            sink = sink_ref[g]
            m = jnp.maximum(jnp.maximum(jnp.max(s_c, axis=-1, keepdims=True), jnp.max(s_n, axis=-1, keepdims=True)), sink)
            p_c = jnp.exp(s_c - m)
            p_n = jnp.exp(s_n - m)
            denom = jnp.sum(p_c, axis=-1, keepdims=True) + jnp.sum(p_n, axis=-1, keepdims=True) + jnp.exp(sink - m)
            o = jnp.dot(p_c.astype(BF16), vcb[:, lo:hi], preferred_element_type=F32)
            o = o + jnp.dot(p_n.astype(BF16), vnb[:, lo:hi], preferred_element_type=F32)
            o = (o / denom).astype(BF16)
            for h in range(GROUP):
                hh = g * GROUP + h
                o_ref[r0:r0 + s_len, hh * HEAD_DIM:(hh + 1) * HEAD_DIM] = o[h * s_len:(h + 1) * s_len, :]


def _attn_sample(q_all, k_all, v_all, cache_k, cache_v, rel_bias, sinks, row0, dec_batch, s_len):
    w = cache_k.shape[1]
    seqs = _pick_tile(dec_batch, (16, 8, 4, 2, 1))
    rows = seqs * s_len
    blk0 = row0 // rows
    assert row0 % rows == 0
    qi = jnp.arange(s_len)
    dist_c = qi[:, None] + w - jnp.arange(w)[None, :]
    dist_n = qi[:, None] - jnp.arange(s_len)[None, :]
    b_c = _bias_table(rel_bias, dist_c).reshape(N_KV_HEADS, GROUP * s_len, w)
    b_n = _bias_table(rel_bias, dist_n).reshape(N_KV_HEADS, GROUP * s_len, s_len)
    sink = jnp.broadcast_to(sinks.astype(F32).reshape(N_KV_HEADS, GROUP, 1, 1), (N_KV_HEADS, GROUP, s_len, 1))
    sink = sink.reshape(N_KV_HEADS, GROUP * s_len, 1)
    tok = lambda width: pl.BlockSpec((rows, width), lambda i: (blk0 + i, 0))
    cache = pl.BlockSpec((seqs, w, D_KV), lambda i: (i, 0, 0))
    return pl.pallas_call(
        functools.partial(_attn_sample_kernel, seqs=seqs, s_len=s_len),
        grid=(dec_batch // seqs,),
        in_specs=[
            tok(D_ATTN), tok(D_KV), tok(D_KV), cache, cache,
            _const_spec(b_c.shape), _const_spec(b_n.shape), _const_spec(sink.shape),
        ],
        out_specs=(pl.BlockSpec((rows, D_ATTN), lambda i: (i, 0)), cache, cache),
        out_shape=(
            jax.ShapeDtypeStruct((dec_batch * s_len, D_ATTN), BF16),
            jax.ShapeDtypeStruct((dec_batch, w, D_KV), F32),
            jax.ShapeDtypeStruct((dec_batch, w, D_KV), F32),
        ),
        compiler_params=_params(("parallel",)),
        name="attn_sample",
    )(q_all, k_all, v_all, cache_k, cache_v, b_c, b_n, sink)


def _softplus(z):
    return jnp.maximum(z, 0.0) + jnp.log1p(jnp.exp(-jnp.abs(z)))


def _block_gate(xcb, w_ref, b_ref):
    parts = [jnp.dot(xcb[:, n * RNN_BLOCK:(n + 1) * RNN_BLOCK], w_ref[n], preferred_element_type=F32)
             for n in range(N_RNN_BLOCKS)]
    return jax.nn.sigmoid(jnp.concatenate(parts, axis=-1) + b_ref[...])


def _lru_coeffs(xc, wa_ref, ba_ref, wx_ref, bx_ref, lam_ref, first_row_unnormalised):
    xcb = xc.astype(BF16)
    r = _block_gate(xcb, wa_ref, ba_ref)
    i = _block_gate(xcb, wx_ref, bx_ref)
    log_a = -RG_C * r * _softplus(-lam_ref[...])
    a = jnp.exp(log_a)
    mult = jnp.sqrt(-jnp.tanh(log_a) * (a * a + 1.0))
    if first_row_unnormalised is not None:
        mult = jnp.where(first_row_unnormalised, 1.0, mult)
    return a, mult * i * xc


def _scan8(a, b):
    shape = a.shape
    grouped = (shape[0] // SUBLANES, SUBLANES, shape[1])
    a = a.reshape(grouped)
    b = b.reshape(grouped)
    r8 = lax.broadcasted_iota(jnp.int32, grouped, 1)
    d = 1
    while d < SUBLANES:
        keep = r8 >= d
        a_sh = jnp.where(keep, pltpu.roll(a, d, 1), 1.0)
        b_sh = jnp.where(keep, pltpu.roll(b, d, 1), 0.0)
        b = a * b_sh + b
        a = a * a_sh
        d *= 2
    return a.reshape(shape), b.reshape(shape)


def _rnn_prompt_kernel(xr_ref, gy_ref, cw_ref, cb_ref, wa_ref, ba_ref, wx_ref, bx_ref, lam_ref,
                       o_ref, nh_ref, ext_ref, a_ref, b_ref, hc_ref, *, tl):
    l = pl.program_id(1)

    @pl.when(l == 0)
    def _():
        ext_ref[0:SUBLANES, :] = jnp.zeros((SUBLANES, D_RNN), F32)
        hc_ref[...] = jnp.zeros((1, D_RNN), F32)

    x = xr_ref[...]
    ext_ref[SUBLANES:, :] = x
    xc = cb_ref[...] + cw_ref[CONV_W - 1:CONV_W, :] * x
    for j in range(1, CONV_W):
        xc = xc + cw_ref[CONV_W - 1 - j:CONV_W - j, :] * ext_ref[SUBLANES - j:SUBLANES - j + tl, :]
    ext_ref[0:SUBLANES, :] = x[tl - SUBLANES:, :]

    row = lax.broadcasted_iota(jnp.int32, (tl, D_RNN), 0)
    a, b = _lru_coeffs(xc, wa_ref, ba_ref, wx_ref, bx_ref, lam_ref, (row == 0) & (l == 0))
    a, b = _scan8(a, b)
    a_ref[...] = a
    b_ref[...] = b

    def chunk(c, h):
        sl = pl.ds(pl.multiple_of(c * SUBLANES, SUBLANES), SUBLANES)
        hc = b_ref[sl, :] + a_ref[sl, :] * h
        b_ref[sl, :] = hc
        return hc[SUBLANES - 1:SUBLANES, :]

    h = lax.fori_loop(0, tl // SUBLANES, chunk, hc_ref[...])
    hc_ref[...] = h
    nh_ref[0] = h
    o_ref[...] = (b_ref[...] * gy_ref[...]).astype(BF16)


def _rnn_prompt(xr_all, gy_all, rnn_w, batch, seq):
    tl =_pick_tile(seq, (256, 128, 64, 32, 16, 8))
    nl = seq // tl
    tok = pl.BlockSpec((tl, D_RNN), lambda b, l: (b * nl + l, 0))
    return pl.pallas_call(
        functools.partial(_rnn_prompt_kernel, tl=tl),
        grid=(batch, nl),
        in_specs=[tok, tok] + [_const_spec(w.shape) for w in rnn_w],
        out_specs=(tok, pl.BlockSpec((1, 1, D_RNN), lambda b, l: (b, 0, 0))),
        out_shape=(jax.ShapeDtypeStruct((batch * seq, D_RNN), BF16), jax.ShapeDtypeStruct((batch, 1, D_RNN), F32)),
        scratch_shapes=[
            pltpu.VMEM((tl + SUBLANES, D_RNN), F32),
            pltpu.VMEM((tl, D_RNN), F32),
            pltpu.VMEM((tl, D_RNN), F32),
            pltpu.VMEM((1, D_RNN), F32),
        ],
        compiler_params=_params(("parallel", "arbitrary")),
        name="rnn_prompt",
    )(xr_all, gy_all, *rnn_w)


def _rnn_sample_kernel(xr_ref, gy_ref, hp_ref, h0_ref, cw_ref, cb_ref, wa_ref, ba_ref, wx_ref, bx_ref, lam_ref,
                       o_ref, nh_ref, *, seqs):
    rows = seqs * SUBLANES
    x = xr_ref[...]
    hp = hp_ref[...]
    r8 = lax.broadcasted_iota(jnp.int32, (rows, D_RNN), 0) % SUBLANES
    xc = cb_ref[...] + cw_ref[CONV_W - 1:CONV_W, :] * x
    for j in range(1, CONV_W):
        shifted = jnp.where(r8 >= j, pltpu.roll(x, j, 0), pltpu.roll(hp, rows - (SUBLANES - j), 0))
        xc = xc + cw_ref[CONV_W - 1 - j:CONV_W - j, :] * shifted
    a, b = _lru_coeffs(xc, wa_ref, ba_ref, wx_ref, bx_ref, lam_ref, None)
    a, b = _scan8(a, b)
    h0 = jnp.broadcast_to(h0_ref[...][:, None, :], (seqs, SUBLANES, D_RNN)).reshape(rows, D_RNN)
    h = b + a * h0
    last = jnp.where(r8 == SUBLANES - 1, h, 0.0).reshape(seqs, SUBLANES, D_RNN)
    nh_ref[...] = jnp.sum(last, axis=1)
    o_ref[...] = (h * gy_ref[...]).astype(BF16)


def _rnn_sample(xr_all, gy_all, hist_pad, h0, rnn_w, row0, dec_batch):
    seqs = _pick_tile(dec_batch, (16, 8))
    rows = seqs * SUBLANES
    assert row0 % rows == 0
    blk0 = row0 // rows
    tok = pl.BlockSpec((rows, D_RNN), lambda i: (blk0 + i, 0))
    return pl.pallas_call(
        functools.partial(_rnn_sample_kernel, seqs=seqs),
        grid=(dec_batch // seqs,),
        in_specs=[tok, tok, pl.BlockSpec((rows, D_RNN), lambda i: (i, 0)), pl.BlockSpec((seqs, D_RNN), lambda i: (i, 0))]
        + [_const_spec(w.shape) for w in rnn_w],
        out_specs=(pl.BlockSpec((rows, D_RNN), lambda i: (i, 0)), pl.BlockSpec((seqs, D_RNN), lambda i: (i, 0))),
        out_shape=(jax.ShapeDtypeStruct((dec_batch * SUBLANES, D_RNN), BF16),
                   jax.ShapeDtypeStruct((dec_batch, D_RNN), F32)),
        compiler_params=_params(("parallel",)),
        name="rnn_sample",
    )(xr_all, gy_all, hist_pad, h0, *rnn_w)


def _layer_norm(z, g, b):
    mu = jnp.mean(z, axis=-1, keepdims=True)
    zc = z - mu
    var = jnp.mean(zc * zc, axis=-1, keepdims=True)
    return zc * lax.rsqrt(var + LN_EPS) * g + b


def _first_index_of_max(vals, iota, axis, sentinel):
    mx = jnp.max(vals, axis=axis, keepdims=True)
    return mx, jnp.min(jnp.where(vals == mx, iota, sentinel), axis=axis, keepdims=True)


def _route(scores, bias):
    t = scores.shape[1]
    grp = scores + bias
    g3 = grp.reshape(N_GROUPS, GROUP_SIZE, t)
    e_in_g = lax.broadcasted_iota(jnp.int32, g3.shape, 1)
    m1, first = _first_index_of_max(g3, e_in_g, 1, GROUP_SIZE)
    m2 = jnp.max(jnp.where(e_in_g == first, -jnp.inf, g3), axis=1, keepdims=True)
    gscore = (m1 + m2).reshape(N_GROUPS, t)
    g_iota = lax.broadcasted_iota(jnp.int32, gscore.shape, 0)
    gmask = jnp.zeros(gscore.shape, jnp.bool_)
    for _ in range(TOPK_GROUPS):
        _, gi = _first_index_of_max(gscore, g_iota, 0, N_GROUPS)
        hit = g_iota == gi
        gmask = gmask | hit
        gscore = jnp.where(hit, -jnp.inf, gscore)
    masked = jnp.where(gmask[:, None, :], g3, -jnp.inf).reshape(N_EXPERTS, t)
    e_iota = lax.broadcasted_iota(jnp.int32, masked.shape, 0)
    idx, wts, hits = [], [], []
    for _ in range(TOP_K):
        _, ei = _first_index_of_max(masked, e_iota, 0, N_EXPERTS)
        hit = e_iota == ei
        idx.append(ei)
        hits.append(hit)
        wts.append(jnp.sum(jnp.where(hit, scores, 0.0), axis=0, keepdims=True))
        masked = jnp.where(hit, -jnp.inf, masked)
    idx = jnp.concatenate(idx, axis=0)
    w = jnp.concatenate(wts, axis=0)
    w = w / jnp.sum(w, axis=0, keepdims=True) * ROUTED_SCALE
    return idx, w, hits


def _merge_kernel(xp_ref, xs_ref, aop_ref, aos_ref, rop_ref, ros_ref, sga_ref, sgr_ref, woa_ref, wor_ref, wout_ref, g1_ref, b1_ref,
                  wrt_ref, rb_ref, wsg_ref, wsu_ref, wsd_ref,
                  x1w_ref, base_ref, idx_ref, wt_ref, rank_ref, cnt_ref, carry_ref, *, n_p, tm):
    i = pl.program_id(0)

    @pl.when(i == 0)
    def _():
        carry_ref[...] = jnp.zeros(carry_ref.shape, F32)

    is_prompt = i < n_p
    x = jnp.where(is_prompt, xp_ref[...], xs_ref[...])
    pa = jnp.dot(jnp.where(is_prompt, aop_ref[...], aos_ref[...]), woa_ref[...], preferred_element_type=F32)
    pr = jnp.dot(jnp.where(is_prompt, rop_ref[...], ros_ref[...]), wor_ref[...], preferred_element_type=F32)
    merged = sga_ref[...] * pa + sgr_ref[...] * pr
    z = ALPHA * x + jnp.dot(merged.astype(BF16), wout_ref[...], preferred_element_type=F32)
    x1 = _layer_norm(z, g1_ref[...], b1_ref[...])
    words = _pack_words(x1)
    for h in range(tm // SC_ROWS):
        for s in range(ROW_TILE):
            x1w_ref[(h * ROW_TILE + s) * SC_ROWS:(h * ROW_TILE + s + 1) * SC_ROWS, :] = (
                words[h * SC_ROWS:(h + 1) * SC_ROWS, s * LANES:(s + 1) * LANES])
    x1b = x1.astype(BF16)
    u = jax.nn.silu(jnp.dot(x1b, wsg_ref[...], preferred_element_type=F32)) * jnp.dot(x1b, wsu_ref[...], preferred_element_type=F32)
    shared = jnp.dot(u.astype(BF16), wsd_ref[...], preferred_element_type=F32)
    base_ref[...] = ALPHA * x1 + shared
    logits = lax.dot_general(wrt_ref[...], x1, (((1,), (1,)), ((), ())), preferred_element_type=F32)
    idx, w, hits = _route(jax.nn.sigmoid(logits), rb_ref[...])
    idx_ref[...] = idx
    wt_ref[...] = w

    chosen = functools.reduce(jnp.logical_or, hits)
    chosen_f = jnp.where(chosen, 1.0, 0.0)
    earlier = (lax.broadcasted_iota(jnp.int32, (tm, tm), 0) < lax.broadcasted_iota(jnp.int32, (tm, tm), 1))
    prefix = jnp.dot(chosen_f.astype(BF16), jnp.where(earlier, 1.0, 0.0).astype(BF16), preferred_element_type=F32)
    before = prefix + carry_ref[...]
    ranks = [jnp.sum(jnp.where(hit, before, 0.0), axis=0, keepdims=True) for hit in hits]
    rank_ref[...] = jnp.concatenate(ranks, axis=0).astype(jnp.int32)
    carry_ref[...] = carry_ref[...] + jnp.sum(chosen_f, axis=1, keepdims=True)
    cnt_ref[...] = carry_ref[...]


def _merge(x_p, x_s, attn_p, attn_s, rnn_p, rnn_s, sga, sgr, weights):
    t_p, t_s = x_p.shape[0], x_s.shape[0]
    t = t_p + t_s
    tm = _pick_tile(math.gcd(t_p, t_s), (256, 128))
    row = lambda width: pl.BlockSpec((tm, width), lambda i: (i, 0))
    col = pl.BlockSpec((TOP_K, tm), lambda i: (0, i))

    return pl.pallas_call(
        functools.partial(_merge_kernel, n_p=t_p // tm, tm=tm),
        grid=(t // tm,),
        in_specs=[*_two_source_specs(t_p, t_s, tm, D_MODEL), *_two_source_specs(t_p, t_s, tm, D_ATTN),
                  *_two_source_specs(t_p, t_s, tm, D_RNN), row(D_MODEL), row(D_MODEL)]
        + [_const_spec(w.shape) for w in weights],
        out_specs=(pl.BlockSpec((tm * ROW_TILE, LANES), lambda i: (i, 0)), row(D_MODEL), col, col, col,
                   _const_spec((N_EXPERTS, 1))),
        out_shape=(
            jax.ShapeDtypeStruct((t * ROW_TILE, LANES), WORD),
            jax.ShapeDtypeStruct((t, D_MODEL), F32),
            jax.ShapeDtypeStruct((TOP_K, t), jnp.int32),
            jax.ShapeDtypeStruct((TOP_K, t), F32),
            jax.ShapeDtypeStruct((TOP_K, t), jnp.int32),
            jax.ShapeDtypeStruct((N_EXPERTS, 1), F32),
        ),
        scratch_shapes=[pltpu.VMEM((N_EXPERTS, 1), F32)],
        compiler_params=_params(("arbitrary",)),
        name="merge_ln1_route",
    )(x_p, x_s, attn_p, attn_s, rnn_p, rnn_s, sga, sgr, *weights)


def _expert_layout(counts, n_assign):
    counts = counts.reshape(N_EXPERTS).astype(jnp.int32)
    padded = (counts + ROW_PAD - 1) // ROW_PAD * ROW_PAD
    pend = jnp.cumsum(padded)
    pstart = pend - padded
    rows_alloc = (n_assign + N_EXPERTS * (ROW_PAD - 1)) // ROW_PAD * ROW_PAD + EXPERT_CHUNK
    n_chunks = jnp.maximum((counts + EXPERT_CHUNK - 1) // EXPERT_CHUNK, 1)
    part = EXPERT_CHUNK // CHUNK_COPIES
    written = jnp.maximum((counts + part - 1) // part, 1) * part
    ffn_end = jnp.max(pstart + written).reshape(1)
    return pstart, pstart + counts, pend, n_chunks, counts, ffn_end, rows_alloc


def _dest_kernel(idx_ref, rank_ref, pstart_ref, dest_ref, word_rows_ref):
    e_iota = lax.broadcasted_iota(jnp.int32, (N_EXPERTS, idx_ref.shape[1]), 0)
    starts = [jnp.sum(jnp.where(e_iota == idx_ref[k:k + 1, :], pstart_ref[...], 0), axis=0, keepdims=True)
              for k in range(TOP_K)]
    dest = jnp.concatenate(starts, axis=0) + rank_ref[...]
    dest_ref[...] = dest
    tm = dest.shape[1]
    word_rows_ref[0] = jnp.concatenate(
        [dest[k:k + 1, h * SC_ROWS:(h + 1) * SC_ROWS] * ROW_TILE + s
         for h in range(tm // SC_ROWS) for s in range(ROW_TILE) for k in range(TOP_K)], axis=0)


def _dest_rows(idx_t, rank_t, pstart, tm):
    t = idx_t.shape[1]
    col = pl.BlockSpec((TOP_K, tm), lambda i: (0, i))
    vecs = tm // SC_ROWS * ROW_TILE * TOP_K
    return pl.pallas_call(
        _dest_kernel,
        grid=(t // tm,),
        in_specs=[col, col, _const_spec((N_EXPERTS, 1))],
        out_specs=(col, pl.BlockSpec((1, vecs, SC_ROWS), lambda i: (i, 0, 0))),
        out_shape=(jax.ShapeDtypeStruct((TOP_K, t), jnp.int32),
                   jax.ShapeDtypeStruct((t // tm, vecs, SC_ROWS), jnp.int32)),
        compiler_params=_params(("parallel",)),
        name="dest_rows",
    )(idx_t, rank_t, pstart.reshape(N_EXPERTS, 1))


def _token_rows(r, n=1):
    return pl.ds(pl.multiple_of(r * ROW_TILE, ROW_TILE), n * ROW_TILE)


def _zero_row_groups(zero_ref, dst_ref, sem, first_group, n_groups):
    def start(g, c):
        pltpu.make_async_copy(zero_ref, dst_ref.at[_token_rows(g * ROW_PAD, ROW_PAD)], sem).start()
        return c

    lax.fori_loop(first_group, n_groups, start, 0)
    return n_groups - first_group


def _wait_zero_copies(zero_ref, dst_ref, sem, n):
    def wait(_, c):
        pltpu.make_async_copy(zero_ref, dst_ref.at[_token_rows(0, ROW_PAD)], sem).wait()
        return c

    lax.fori_loop(0, n, wait, 0)


SC_ROWS = 128


def _sc_workers():
    info = plsc.get_sparse_core_info()
    mesh = plsc.VectorSubcoreMesh(core_axis_name="c", subcore_axis_name="s")
    worker = lambda: lax.axis_index("s") * info.num_cores + lax.axis_index("c")
    return mesh, info.num_cores * info.num_subcores, worker


def _sc_scatter_rows(src, rows, n_out):
    mesh, n_workers, worker = _sc_workers()
    n_units = src.shape[0] // SC_ROWS
    assert n_units % n_workers == 0 and rows.shape == (n_units * TOP_K, SC_ROWS)
    per_worker = n_units // n_workers

    unit_bufs = [pltpu.VMEM((TOP_K, SC_ROWS), jnp.int32), pltpu.VMEM((SC_ROWS, LANES), src.dtype),
                 pltpu.SemaphoreType.DMA]

    @functools.partial(
        pl.kernel, mesh=mesh,
        out_type=jax.ShapeDtypeStruct((n_out, LANES), src.dtype),
        scratch_types=unit_bufs + unit_bufs + [pltpu.SemaphoreType.DMA],
    )
    def scatter(src_hbm, rows_hbm, out_hbm, rows_a, data_a, load_a, rows_b, data_b, load_b, sem):
        first = worker() * per_worker
        bufs = ((rows_a, data_a, load_a), (rows_b, data_b, load_b))

        def loads(u, buf):
            rows_v, data_v, load_sem = buf
            return (pltpu.make_async_copy(rows_hbm.at[pl.ds(u * TOP_K, TOP_K)], rows_v, load_sem),
                    pltpu.make_async_copy(src_hbm.at[pl.ds(u * SC_ROWS, SC_ROWS)], data_v, load_sem))

        def unit(u, buf, other, has_next):
            for cp in loads(u, buf):
                cp.wait()

            @pl.when(has_next)
            def _():
                for cp in loads(u + 1, other):
                    cp.start()

            rows_v, data_v, _ = buf
            copies = [pltpu.async_copy(data_v, out_hbm.at[rows_v.at[k]], sem) for k in range(TOP_K)]
            for cp in copies:
                cp.wait()

        for cp in loads(first, bufs[0]):
            cp.start()

        @pl.loop(0, per_worker // 2)
        def _(pair):
            u = first + 2 * pair
            unit(u, bufs[0], bufs[1], True)
            unit(u + 1, bufs[1], bufs[0], 2 * pair + 2 < per_worker)

        if per_worker % 2:
            unit(first + per_worker - 1, bufs[0], bufs[1], False)

    return scatter(src, rows)


def _zero_padding_kernel(lo_ref, hi_ref, tail_ref, xs_in_ref, xs_ref, zero_ref, sem, *, rows_alloc):
    del xs_in_ref
    zero_ref[...] = jnp.zeros(zero_ref.shape, WORD)
    row_copy = lambda r: pltpu.make_async_copy(zero_ref.at[pl.ds(0, ROW_TILE)], xs_ref.at[_token_rows(r)], sem)

    def expert(e, n):
        def row(r, c):
            row_copy(r).start()
            return c

        lax.fori_loop(lo_ref[e], hi_ref[e], row, 0)
        return n + hi_ref[e] - lo_ref[e]

    def wait(_, c):
        row_copy(0).wait()
        return c

    lax.fori_loop(0, lax.fori_loop(0, N_EXPERTS, expert, 0), wait, 0)
    n = _zero_row_groups(zero_ref, xs_ref, sem, tail_ref[0] // ROW_PAD, rows_alloc // ROW_PAD)
    _wait_zero_copies(zero_ref, xs_ref, sem, n)


def _zero_padding(xs, pad_lo, pad_hi, total, rows_alloc):
    grid_spec = pltpu.PrefetchScalarGridSpec(
        num_scalar_prefetch=3,
        grid=(1,),
        in_specs=[pl.BlockSpec(memory_space=pl.ANY)],
        out_specs=pl.BlockSpec(memory_space=pl.ANY),
        scratch_shapes=[pltpu.VMEM((ROW_PAD * ROW_TILE, LANES), WORD), pltpu.SemaphoreType.DMA],
    )
    return pl.pallas_call(
        functools.partial(_zero_padding_kernel, rows_alloc=rows_alloc),
        grid_spec=grid_spec,
        out_shape=jax.ShapeDtypeStruct(xs.shape, xs.dtype),
        input_output_aliases={3: 0},
        compiler_params=_params(("arbitrary",)),
        name="zero_padding",
    )(pad_lo, pad_hi, total, xs)


def _expert_kernel(pstart_ref, nch_ref, rows_ref, end_ref, wg_ref, wu_ref, wd_ref, xs_ref, o_ref,
                   xbuf_ref, obuf_ref, wgb_ref, wub_ref, wdb_ref, zero_ref, done_ref, in_sem, out_sem, *, rows_alloc):
    e = pl.program_id(0)
    n_e = pl.num_programs(0)
    start = pstart_ref[e]
    nch = nch_ref[e]
    buf_rows = EXPERT_CHUNK * ROW_TILE

    part = EXPERT_CHUNK // CHUNK_COPIES
    part_rows = part * ROW_TILE

    class _ChunkCopy:
        def __init__(self, whole, parts, n_parts):
            self.whole = whole
            self.parts = parts
            self.n_parts = n_parts

        def _each(self, act):
            pl.when(self.n_parts == CHUNK_COPIES)(functools.partial(act, self.whole))
            for p, cp in enumerate(self.parts):
                pl.when((p < self.n_parts) & (self.n_parts < CHUNK_COPIES))(functools.partial(act, cp))

        def start(self):
            self._each(lambda cp: cp.start())

        def wait(self):
            self._each(lambda cp: cp.wait())

    def parts_of(rows_left):
        return jnp.clip((rows_left + part - 1) // part, 1, CHUNK_COPIES)

    def in_copy(row, slot, n_parts):
        copy = lambda p, n: pltpu.make_async_copy(
            xs_ref.at[_token_rows(row + p * part, n * part)],
            xbuf_ref.at[pl.ds(slot * buf_rows + p * part_rows, n * part_rows)], in_sem.at[slot])
        return _ChunkCopy(copy(0, CHUNK_COPIES), [copy(p, 1) for p in range(CHUNK_COPIES - 1)], n_parts)

    def out_copy(row, slot, n_parts):
        copy = lambda p, n: pltpu.make_async_copy(
            obuf_ref.at[pl.ds(slot * buf_rows + p * part_rows, n * part_rows)],
            o_ref.at[_token_rows(row + p * part, n * part)], out_sem)
        return _ChunkCopy(copy(0, CHUNK_COPIES), [copy(p, 1) for p in range(CHUNK_COPIES - 1)], n_parts)

    @pl.when(e == 0)
    def _():
        done_ref[0] = 0
        xbuf_ref[...] = jnp.zeros(xbuf_ref.shape, WORD)
        obuf_ref[...] = jnp.zeros(obuf_ref.shape, WORD)
        in_copy(start, 0, parts_of(rows_ref[0])).start()

    wgb_ref[...] = wg_ref[...].astype(BF16)
    wub_ref[...] = wu_ref[...].astype(BF16)
    wdb_ref[...] = wd_ref[...].astype(BF16)
    done = done_ref[0]

    def chunk(c, carry):
        g = done + c
        slot = g % 2
        row = start + c * EXPERT_CHUNK
        rows_left = rows_ref[e] - c * EXPERT_CHUNK
        n_parts = parts_of(rows_left)
        in_copy(row, slot, n_parts).wait()
        last = c + 1 == nch
        next_e = jnp.minimum(e + 1, n_e - 1)
        next_row = jnp.where(last, pstart_ref[next_e], row + EXPERT_CHUNK)
        next_left = jnp.where(last, rows_ref[next_e], rows_left - EXPERT_CHUNK)

        @pl.when(jnp.logical_not(last & (e == n_e - 1)))
        def _():
            in_copy(next_row, 1 - slot, parts_of(next_left)).start()

        base = pl.multiple_of(slot * buf_rows, buf_rows)
        def sub_block(h):
            xb = jnp.concatenate(_load_token_rows(xbuf_ref, EXPERT_ROWS, row0=h * EXPERT_ROWS, base=base),
                                 axis=1).astype(BF16)
            gate = jnp.dot(xb, wgb_ref[...], preferred_element_type=F32)
            up = jnp.dot(xb, wub_ref[...], preferred_element_type=F32)
            act = (jax.nn.silu(gate) * up).astype(BF16)
            _store_token_rows(obuf_ref, jnp.dot(act, wdb_ref[...], preferred_element_type=F32),
                              EXPERT_ROWS, row0=h * EXPERT_ROWS, base=base)

        n_sub = EXPERT_CHUNK // EXPERT_ROWS
        needed = jnp.clip((rows_left + EXPERT_ROWS - 1) // EXPERT_ROWS, 1, n_sub)
        for count in range(1, n_sub + 1):
            @pl.when(needed == count)
            def _():
                for h in range(count):
                    sub_block(h)

        @pl.when(g > 0)
        def _():
            out_copy(0, 0, done_ref[1]).wait()

        out_copy(row, slot, n_parts).start()
        done_ref[1] = n_parts
        return carry

    lax.fori_loop(0, nch, chunk, 0)
    done_ref[0] = done + nch

    @pl.when(e == n_e - 1)
    def _():
        out_copy(0, 0, done_ref[1]).wait()
        zero_ref[...] = jnp.zeros(zero_ref.shape, WORD)
        n = _zero_row_groups(zero_ref, o_ref, out_sem, end_ref[0] // ROW_PAD, rows_alloc // ROW_PAD)
        _wait_zero_copies(zero_ref, o_ref, out_sem, n)


def _expert_ffn(xs, pstart, n_chunks, rows, ffn_end, rows_alloc, w_e_gate, w_e_up, w_e_down):
    weight = lambda shape: pl.BlockSpec((None, *shape), lambda e, *_: (e, 0, 0))
    grid_spec = pltpu.PrefetchScalarGridSpec(
        num_scalar_prefetch=4,
        grid=(N_EXPERTS,),
        in_specs=[weight((D_MODEL, D_EXPERT)), weight((D_MODEL, D_EXPERT)), weight((D_EXPERT, D_MODEL)),
                  pl.BlockSpec(memory_space=pl.ANY)],
        out_specs=pl.BlockSpec(memory_space=pl.ANY),
        scratch_shapes=[
            pltpu.VMEM((2 * EXPERT_CHUNK * ROW_TILE, LANES), WORD),
            pltpu.VMEM((2 * EXPERT_CHUNK * ROW_TILE, LANES), WORD),
            pltpu.VMEM((D_MODEL, D_EXPERT), BF16), pltpu.VMEM((D_MODEL, D_EXPERT), BF16),
            pltpu.VMEM((D_EXPERT, D_MODEL), BF16),
            pltpu.VMEM((ROW_PAD * ROW_TILE, LANES), WORD),
            pltpu.SMEM((2,), jnp.int32),
            pltpu.SemaphoreType.DMA((2,)), pltpu.SemaphoreType.DMA,
        ],
    )
    return pl.pallas_call(
        functools.partial(_expert_kernel, rows_alloc=rows_alloc),
        grid_spec=grid_spec,
        out_shape=jax.ShapeDtypeStruct((rows_alloc * ROW_TILE, LANES), WORD),
        compiler_params=_params(("arbitrary",)),
        name="expert_ffn",
    )(pstart, n_chunks, rows, ffn_end, w_e_gate, w_e_up, w_e_down, xs)


def _combine_head_kernel(dest_ref, dest_next_ref, w_ref, base_ref, g_ref, b_ref, outs_ref, y_ref, buf_ref, sem, *, tm):
    i = pl.program_id(0)
    slot_rows = TOP_K * tm

    def gather(d_ref, slot):
        def issue(t, c):
            for k in range(TOP_K):
                pltpu.make_async_copy(outs_ref.at[_token_rows(d_ref[k, t])],
                                      buf_ref.at[_token_rows(slot * slot_rows + k * tm + t)], sem.at[slot]).start()
            return c

        lax.fori_loop(0, tm, issue, 0)

    @pl.when(i == 0)
    def _():
        gather(dest_ref, 0)

    @pl.when(i + 1 < pl.num_programs(0))
    def _():
        gather(dest_next_ref, (i + 1) % 2)

    slot = i % 2
    for k in range(TOP_K):
        pltpu.make_async_copy(outs_ref.at[_token_rows(0, tm)], buf_ref.at[_token_rows(slot * slot_rows + k * tm, tm)],
                              sem.at[slot]).wait()

    w = w_ref[...]
    base = pl.multiple_of(slot * slot_rows * ROW_TILE, ROW_TILE)
    halves = [None, None]
    for k in range(TOP_K):
        wk = w[:, k:k + 1]
        for j, rows in enumerate(_load_token_rows(buf_ref, tm, row0=k * tm, base=base)):
            halves[j] = wk * rows if halves[j] is None else halves[j] + wk * rows
    y_ref[...] = _layer_norm(base_ref[...] + jnp.concatenate(halves, axis=1), g_ref[...], b_ref[...])


def _combine_head(out_sorted, dest, w_tok, base, g2, b2, tm, n_head):
    row = pl.BlockSpec((tm, D_MODEL), lambda i: (i, 0))
    return pl.pallas_call(
        functools.partial(_combine_head_kernel, tm=tm),
        grid=(n_head,),
        in_specs=[
            pl.BlockSpec((TOP_K, tm), lambda i: (0, i), memory_space=pltpu.SMEM),
            pl.BlockSpec((TOP_K, tm), lambda i: (0, jnp.minimum(i + 1, n_head - 1)), memory_space=pltpu.SMEM),
            pl.BlockSpec((tm, TOP_K), lambda i: (i, 0)),
            row, _const_spec((1, D_MODEL)), _const_spec((1, D_MODEL)),
            pl.BlockSpec(memory_space=pl.ANY),
        ],
        out_specs=row,
        out_shape=jax.ShapeDtypeStruct((n_head * tm, D_MODEL), F32),
        scratch_shapes=[pltpu.VMEM((2 * TOP_K * tm * ROW_TILE, LANES), WORD), pltpu.SemaphoreType.DMA((2,))],
        compiler_params=_params(("arbitrary",)),
        name="combine_head",
    )(dest, dest, w_tok, base, g2, b2, out_sorted)


def _sc_gather_rows(table, rows):
    mesh, n_workers, worker = _sc_workers()
    m = rows.shape[0]
    per_worker = m // n_workers
    assert m % n_workers == 0 and per_worker % SC_ROWS == 0
    in_flight = _pick_tile(per_worker // SC_ROWS, (4, 2, 1))
    step = SC_ROWS * in_flight

    @functools.partial(
        pl.kernel, mesh=mesh,
        out_type=jax.ShapeDtypeStruct((m, LANES), table.dtype),
        scratch_types=[pltpu.VMEM((step,), jnp.int32), pltpu.VMEM((step, LANES), table.dtype), pltpu.SemaphoreType.DMA],
    )
    def gather(table_hbm, rows_hbm, out_hbm, rows_v, data_v, sem):
        first = worker() * per_worker

        @pl.loop(0, per_worker // step)
        def _(it):
            off = first + it * step
            pltpu.sync_copy(rows_hbm.at[pl.ds(off, step)], rows_v)
            copies = [pltpu.async_copy(table_hbm.at[rows_v.at[pl.ds(j * SC_ROWS, SC_ROWS)]],
                                       data_v.at[pl.ds(j * SC_ROWS, SC_ROWS)], sem)
                      for j in range(in_flight)]
            for cp in copies:
                cp.wait()
            pltpu.sync_copy(data_v, out_hbm.at[pl.ds(off, step)])

    return gather(table, rows)


def _combine_tail_kernel(yh_ref, rows_ref, w_ref, base_ref, g_ref, b_ref, yp_ref, ys_ref, *, tm, n_head, n_p):
    i = pl.program_id(0)

    @pl.when(i < n_head)
    def _():
        yp_ref[...] = yh_ref[...]

    @pl.when(i >= n_head)
    def _():
        w = w_ref[...]
        lo = [None] * ROW_TILE
        hi = [None] * ROW_TILE
        for k in range(TOP_K):
            wk = w[:, k:k + 1]
            for s in range(ROW_TILE):
                vec = lambda h: ((h * ROW_TILE + s) * TOP_K + k) * SC_ROWS
                words = jnp.concatenate([rows_ref[vec(h):vec(h) + SC_ROWS, :] for h in range(tm // SC_ROWS)], axis=0)
                lo_s, hi_s = _unpack_words(words)
                lo[s] = wk * lo_s if lo[s] is None else lo[s] + wk * lo_s
                hi[s] = wk * hi_s if hi[s] is None else hi[s] + wk * hi_s
        y = _layer_norm(base_ref[...] + jnp.concatenate(lo + hi, axis=1), g_ref[...], b_ref[...])

        @pl.when(i < n_p)
        def _():
            yp_ref[...] = y

        @pl.when(i >= n_p)
        def _():
            ys_ref[...] = y


def _combine_tail(y_head, gathered, w_tok, base, g2, b2, t_p, t_s, tm, n_head):
    n_p = t_p // tm
    assert n_head <= n_p
    n_tiles = (t_p + t_s) // tm
    blk = TOP_K * ROW_TILE * tm
    row = pl.BlockSpec((tm, D_MODEL), lambda i: (i, 0))
    out_p, out_s = _two_source_specs(t_p, t_s, tm, D_MODEL)
    return pl.pallas_call(
        functools.partial(_combine_tail_kernel, tm=tm, n_head=n_head, n_p=n_p),
        grid=(n_tiles,),
        in_specs=[
            pl.BlockSpec((tm, D_MODEL), lambda i: (jnp.minimum(i, n_head - 1), 0)),
            pl.BlockSpec((blk, LANES), lambda i: (jnp.maximum(i - n_head, 0), 0)),
            pl.BlockSpec((tm, TOP_K), lambda i: (i, 0)),
            row, _const_spec((1, D_MODEL)), _const_spec((1, D_MODEL)),
        ],
        out_specs=(out_p, out_s),
        out_shape=(jax.ShapeDtypeStruct((t_p, D_MODEL), F32), jax.ShapeDtypeStruct((t_s, D_MODEL), F32)),
        compiler_params=_params(("arbitrary",)),
        name="combine_tail",
    )(y_head, gathered, w_tok, base, g2, b2)


def kernel(x_prompt, x_sample, cache_k, cache_v, state_conv, state_rnn, w_in, conv_w, conv_b, w_gate_a, b_gate_a, w_gate_x, b_gate_x, lru_lambda, rel_bias, sinks, w_o_attn, w_o_rnn, w_out, ln1_g, ln1_b, w_router, router_bias, w_e_gate, w_e_up, w_e_down, w_s_gate, w_s_up, w_s_down, ln2_g, ln2_b):
    assert w_in.shape[0] == DEPTH == 1
    batch, seq, _ = x_prompt.shape
    dec_batch, s_len, _ = x_sample.shape
    w_cache = cache_k.shape[2]
    assert s_len == SUBLANES and seq % WINDOW == 0 and w_cache == WINDOW
    t_p = batch * seq
    t_s = dec_batch * s_len
    vec = lambda a: a[0].reshape(1, -1).astype(F32)

    x_p = x_prompt.reshape(t_p, D_MODEL)
    x_s = x_sample.reshape(t_s, D_MODEL)
    q, k, v, xr, gy, sga, sgr = _inproj(x_p, x_s, w_in[0].astype(BF16))

    qi = jnp.arange(WINDOW)
    dist = qi[:, None] + WINDOW - jnp.arange(2 * WINDOW)[None, :]
    bias_p = _bias_table(rel_bias, dist)
    attn_p = _attn_prompt(q, k, v, bias_p, sinks[0], batch, seq)
    attn_s, k_s, v_s = _attn_sample(
        q, k, v, cache_k[0].reshape(dec_batch, w_cache, D_KV), cache_v[0].reshape(dec_batch, w_cache, D_KV),
        rel_bias, sinks[0], t_p, dec_batch, s_len)

    rnn_w = (conv_w[0], vec(conv_b), w_gate_a[0].astype(BF16), vec(b_gate_a), w_gate_x[0].astype(BF16),
             vec(b_gate_x), vec(lru_lambda))
    rnn_p, h_p = _rnn_prompt(xr, gy, rnn_w, batch, seq)
    hist_pad = jnp.pad(state_conv[0], ((0, 0), (SUBLANES - (CONV_W - 1), 0), (0, 0))).reshape(t_s, D_RNN)
    rnn_s, h_s = _rnn_sample(xr, gy, hist_pad, state_rnn[0], rnn_w, t_p, dec_batch)

    merge_w = (w_o_attn[0].astype(BF16), w_o_rnn[0].astype(BF16), w_out[0].astype(BF16), vec(ln1_g), vec(ln1_b),
               w_router[0].T, router_bias[0].reshape(N_EXPERTS, 1), w_s_gate[0].astype(BF16),
               w_s_up[0].astype(BF16), w_s_down[0].astype(BF16))
    x1w, base, idx_t, wt_t, rank_t, counts = _merge(x_p, x_s, attn_p, attn_s, rnn_p, rnn_s, sga, sgr, merge_w)

    pstart, pad_lo, pad_hi, n_chunks, rows, ffn_end, rows_alloc = _expert_layout(counts, (t_p + t_s) * TOP_K)
    tm = _pick_tile(math.gcd(t_p, t_s), (256, 128))
    dest, word_rows = _dest_rows(idx_t, rank_t, pstart, tm)
    xs = _sc_scatter_rows(x1w, word_rows.reshape(-1, SC_ROWS), rows_alloc * ROW_TILE)
    xs = _zero_padding(xs, pad_lo, pad_hi, pad_hi[N_EXPERTS - 1:], rows_alloc)
    out_sorted = _expert_ffn(xs, pstart, n_chunks, rows, ffn_end, rows_alloc, w_e_gate[0], w_e_up[0], w_e_down[0])
    n_tiles = (t_p + t_s) // tm
    n_head = min(t_p // tm, n_tiles // 3)
    w_tok = wt_t.T
    g2, b2 = vec(ln2_g), vec(ln2_b)
    y_head = _combine_head(out_sorted, dest, w_tok, base, g2, b2, tm, n_head)
    gathered = _sc_gather_rows(out_sorted, word_rows[n_head:].reshape(-1))
    y_p, y_s = _combine_tail(y_head, gathered, w_tok, base, g2, b2, t_p, t_s, tm, n_head)
    y_p = y_p.reshape(batch, seq, D_MODEL)
    y_s = y_s.reshape(dec_batch, s_len, D_MODEL)
    kv5 = lambda a, b: a.reshape(1, b, WINDOW, N_KV_HEADS, HEAD_DIM)
    tail = lambda a, n: jnp.stack([lax.slice_in_dim(a, (b + 1) * seq - n, (b + 1) * seq) for b in range(batch)])
    k_p = kv5(tail(k, WINDOW), batch)
    v_p = kv5(tail(v, WINDOW), batch)
    conv_p = tail(xr, CONV_W - 1)[None]
    conv_s = xr[t_p:].reshape(dec_batch, s_len, D_RNN)[:, s_len - (CONV_W - 1):][None]
    return (y_p, y_s, k_p, v_p, conv_p, h_p.reshape(1, batch, D_RNN),
            kv5(k_s, dec_batch), kv5(v_s, dec_batch), conv_s, h_s.reshape(1, dec_batch, D_RNN))
```

```python
import functools
import math

import jax
import jax.numpy as jnp
from jax import lax
from jax.experimental import pallas as pl
from jax.experimental.pallas import tpu as pltpu
from jax.experimental.pallas import tpu_sc as plsc

F32 = jnp.float32
BF16 = jnp.bfloat16
WORD = jnp.int32

D_MODEL = 1024
N_HEADS = 8
N_KV_HEADS = 2
HEAD_DIM = 64
GROUP = N_HEADS // N_KV_HEADS
WINDOW = 128
D_ATTN = N_HEADS * HEAD_DIM
D_KV = N_KV_HEADS * HEAD_DIM
N_BUCKETS = 32
MAX_DISTANCE = 128
D_RNN = D_MODEL
RNN_BLOCK = 256
N_RNN_BLOCKS = D_RNN // RNN_BLOCK
CONV_W = 4
RG_C = 8.0
N_EXPERTS = 256
TOP_K = 8
N_GROUPS = 8
GROUP_SIZE = N_EXPERTS // N_GROUPS
TOPK_GROUPS = 4
D_EXPERT = D_MODEL // 4
ROUTED_SCALE = 2.5
LN_EPS = 1e-5
DEPTH = 1
ALPHA = (2 * DEPTH) ** 0.25
NEG_INF = -1e30
SM_SCALE = HEAD_DIM ** -0.5

O_Q = 0
O_K = D_ATTN
O_V = O_K + D_KV
O_XR = O_V + D_KV
O_YR = O_XR + D_RNN
O_GA = O_YR + D_RNN
O_GR = O_GA + D_MODEL
D_IN = O_GR + D_MODEL

SUBLANES = 8
VMEM_LIMIT_BYTES = 56 * 1024 * 1024
EXPERT_ROWS = 256
CHUNK_PART = 128
ROW_PAD = SUBLANES


def _params(sem):
    return pltpu.CompilerParams(dimension_semantics=sem, vmem_limit_bytes=VMEM_LIMIT_BYTES)


def _pick_tile(n, candidates):
    for c in candidates:
        if n % c == 0:
            return c
    raise ValueError(f"no tile for {n}")


def _const_spec(shape):
    nd = len(shape)
    return pl.BlockSpec(shape, lambda *_: (0,) * nd)


LANES = 128
ROW_WORDS = D_MODEL // 2
ROW_TILE = ROW_WORDS // LANES
HIGH_HALF = -65536


def _pack_words(mat):
    as_bits = lambda v: pltpu.bitcast(v.astype(BF16).astype(F32), WORD)
    return (as_bits(mat[:, ROW_WORDS:]) & HIGH_HALF) | lax.shift_right_logical(as_bits(mat[:, :ROW_WORDS]), 16)


def _unpack_words(words):
    return pltpu.bitcast(words << 16, F32), pltpu.bitcast(words & HIGH_HALF, F32)


def _store_token_rows(ref, mat, n, row0=0, base=0):
    words = _pack_words(mat)
    for s in range(ROW_TILE):
        ref[pl.ds(base + row0 * ROW_TILE + s, n, stride=ROW_TILE), :] = words[:, s * LANES:(s + 1) * LANES]


def _load_token_rows(ref, n, row0=0, base=0):
    words = jnp.concatenate(
        [ref[pl.ds(base + row0 * ROW_TILE + s, n, stride=ROW_TILE), :] for s in range(ROW_TILE)], axis=1)
    return _unpack_words(words)


def _two_source_specs(t_p, t_s, tm, width):
    n_p = t_p // tm
    assert t_p % tm == 0 and t_s % tm == 0
    return (pl.BlockSpec((tm, width), lambda i, *_: (jnp.minimum(i, n_p - 1), 0)),
            pl.BlockSpec((tm, width), lambda i, *_: (jnp.maximum(i - n_p, 0), 0)))


def _inproj_kernel(xp_ref, xs_ref, w_ref, q_ref, k_ref, v_ref, xr_ref, gy_ref, sga_ref, sgr_ref, *, n_p):
    x = jnp.where(pl.program_id(0) < n_p, xp_ref[...], xs_ref[...]).astype(BF16)

    def seg(lo, hi):
        return jnp.dot(x, w_ref[:, lo:hi], preferred_element_type=F32)

    q_ref[...] = seg(O_Q, O_K).astype(BF16)
    k_ref[...] = seg(O_K, O_V)
    v_ref[...] = seg(O_V, O_XR)
    xr_ref[...] = seg(O_XR, O_YR)
    gy_ref[...] = jax.nn.gelu(seg(O_YR, O_GA)).astype(BF16)
    sga_ref[...] = jax.nn.sigmoid(seg(O_GA, O_GR)).astype(BF16)
    sgr_ref[...] = jax.nn.sigmoid(seg(O_GR, D_IN)).astype(BF16)


def _inproj(x_p, x_s, w_in_bf16):
    t_p, t_s = x_p.shape[0], x_s.shape[0]
    t = t_p + t_s
    tm = _pick_tile(math.gcd(t_p, t_s), (256, 128, 64, 32, 16, 8))
    row = lambda width: pl.BlockSpec((tm, width), lambda i: (i, 0))
    out_shape = (
        jax.ShapeDtypeStruct((t, D_ATTN), BF16),
        jax.ShapeDtypeStruct((t, D_KV), F32),
        jax.ShapeDtypeStruct((t, D_KV), F32),
        jax.ShapeDtypeStruct((t, D_RNN), F32),
        jax.ShapeDtypeStruct((t, D_RNN), BF16),
        jax.ShapeDtypeStruct((t, D_MODEL), BF16),
        jax.ShapeDtypeStruct((t, D_MODEL), BF16),
    )
    return pl.pallas_call(
        functools.partial(_inproj_kernel, n_p=t_p // tm),
        grid=(t // tm,),
        in_specs=[*_two_source_specs(t_p, t_s, tm, D_MODEL), _const_spec((D_MODEL, D_IN))],
        out_specs=(row(D_ATTN), row(D_KV), row(D_KV), row(D_RNN), row(D_RNN), row(D_MODEL), row(D_MODEL)),
        out_shape=out_shape,
        compiler_params=_params(("parallel",)),
        name="inproj",
    )(x_p, x_s, w_in_bf16)


def _t5_bucket(dist):
    n = jnp.maximum(dist, 0)
    max_exact = N_BUCKETS // 2
    nf = jnp.maximum(n, 1).astype(F32)
    large = max_exact + (jnp.log(nf / max_exact) / math.log(MAX_DISTANCE / max_exact) * (N_BUCKETS - max_exact)).astype(jnp.int32)
    large = jnp.minimum(large, N_BUCKETS - 1)
    return jnp.where(n < max_exact, n, large)


def _bias_table(rel_bias, dist):
    bucket = _t5_bucket(dist)
    rb = rel_bias.astype(F32)
    out = jnp.zeros((N_HEADS, *dist.shape), F32)
    for j in range(N_BUCKETS):
        out = jnp.where(bucket[None] == j, rb[j][:, None, None], out)
    return out


def _softmax_pv(s, sink, v):
    m = jnp.maximum(jnp.max(s, axis=-1, keepdims=True), sink)
    p = jnp.exp(s - m)
    denom = jnp.sum(p, axis=-1, keepdims=True) + jnp.exp(sink - m)
    return jnp.dot(p.astype(BF16), v, preferred_element_type=F32), denom


def _attn_prompt_kernel(sink_ref, q_ref, kc_ref, kp_ref, vc_ref, vp_ref, bias_ref, o_ref):
    n = pl.program_id(1)
    kk = jnp.concatenate([kp_ref[...], kc_ref[...]], axis=0).astype(BF16)
    vv = jnp.concatenate([vp_ref[...], vc_ref[...]], axis=0).astype(BF16)
    q = q_ref[...]
    rows = lax.broadcasted_iota(jnp.int32, (WINDOW, 2 * WINDOW), 0)
    cols = lax.broadcasted_iota(jnp.int32, (WINDOW, 2 * WINDOW), 1)
    dist = rows + WINDOW - cols
    valid = (dist >= 0) & (dist <= WINDOW) & ((n > 0) | (cols >= WINDOW))
    for g in range(N_KV_HEADS):
        kg = kk[:, g * HEAD_DIM:(g + 1) * HEAD_DIM]
        vg = vv[:, g * HEAD_DIM:(g + 1) * HEAD_DIM]
        for h in range(GROUP):
            hh = g * GROUP + h
            qh = q[:, hh * HEAD_DIM:(hh + 1) * HEAD_DIM]
            s = lax.dot_general(qh, kg, (((1,), (1,)), ((), ())), preferred_element_type=F32) * SM_SCALE
            s = jnp.where(valid, s + bias_ref[hh], NEG_INF)
            o, denom = _softmax_pv(s, sink_ref[0, hh], vg)
            o_ref[:, hh * HEAD_DIM:(hh + 1) * HEAD_DIM] = (o / denom).astype(BF16)


def _attn_prompt(q_all, k_all, v_all, bias, sinks, batch, seq):
    nb = seq // WINDOW
    cur = lambda width: pl.BlockSpec((WINDOW, width), lambda b, n: (b * nb + n, 0))
    prev = lambda width: pl.BlockSpec((WINDOW, width), lambda b, n: (b * nb + jnp.maximum(n - 1, 0), 0))
    return pl.pallas_call(
        _attn_prompt_kernel,
        grid=(batch, nb),
        in_specs=[
            pl.BlockSpec(memory_space=pltpu.SMEM),
            cur(D_ATTN), cur(D_KV), prev(D_KV), cur(D_KV), prev(D_KV),
            _const_spec((N_HEADS, WINDOW, 2 * WINDOW)),
        ],
        out_specs=cur(D_ATTN),
        out_shape=jax.ShapeDtypeStruct((batch * seq, D_ATTN), BF16),
        compiler_params=_params(("parallel", "arbitrary")),
        name="attn_prompt",
    )(sinks.reshape(1, N_HEADS).astype(F32), q_all, k_all, k_all, v_all, v_all, bias)


def _attn_sample_kernel(q_ref, kn_ref, vn_ref, kc_ref, vc_ref, bc_ref, bn_ref, sink_ref,
                        o_ref, ko_ref, vo_ref, *, seqs, s_len):
    w = kc_ref.shape[1]
    rows_c = lax.broadcasted_iota(jnp.int32, (GROUP * s_len, w), 0) % s_len
    cols_c = lax.broadcasted_iota(jnp.int32, (GROUP * s_len, w), 1)
    dist_c = rows_c + w - cols_c
    valid_c = (dist_c >= 0) & (dist_c <= WINDOW)
    rows_n = lax.broadcasted_iota(jnp.int32, (GROUP * s_len, s_len), 0) % s_len
    cols_n = lax.broadcasted_iota(jnp.int32, (GROUP * s_len, s_len), 1)
    dist_n = rows_n - cols_n
    valid_n = (dist_n >= 0) & (dist_n <= WINDOW)
    for j in range(seqs):
        r0 = j * s_len
        qj = q_ref[r0:r0 + s_len, :]
        kc = kc_ref[j]
        vc = vc_ref[j]
        kn = kn_ref[r0:r0 + s_len, :]
        vn = vn_ref[r0:r0 + s_len, :]
        ko_ref[j, 0:w - s_len, :] = kc[s_len:, :]
        ko_ref[j, w - s_len:w, :] = kn
        vo_ref[j, 0:w - s_len, :] = vc[s_len:, :]
        vo_ref[j, w - s_len:w, :] = vn
        kcb, vcb, knb, vnb = kc.astype(BF16), vc.astype(BF16), kn.astype(BF16), vn.astype(BF16)
        for g in range(N_KV_HEADS):
            lo, hi = g * HEAD_DIM, (g + 1) * HEAD_DIM
            qs = jnp.concatenate(
                [qj[:, (g * GROUP + h) * HEAD_DIM:(g * GROUP + h + 1) * HEAD_DIM] for h in range(GROUP)], axis=0)
            nt = (((1,), (1,)), ((), ()))
            s_c = lax.dot_general(qs, kcb[:, lo:hi], nt, preferred_element_type=F32) * SM_SCALE
            s_n = lax.dot_general(qs, knb[:, lo:hi], nt, preferred_element_type=F32) * SM_SCALE
            s_c = jnp.where(valid_c, s_c + bc_ref[g], NEG_INF)
            s_n = jnp.where(valid_n, s_n + bn_ref[g], NEG_INF)
            sink = sink_ref[g]
            m = jnp.maximum(jnp.maximum(jnp.max(s_c, axis=-1, keepdims=True), jnp.max(s_n, axis=-1, keepdims=True)), sink)
            p_c = jnp.exp(s_c - m)
            p_n = jnp.exp(s_n - m)
            denom = jnp.sum(p_c, axis=-1, keepdims=True) + jnp.sum(p_n, axis=-1, keepdims=True) + jnp.exp(sink - m)
            o = jnp.dot(p_c.astype(BF16), vcb[:, lo:hi], preferred_element_type=F32)
            o = o + jnp.dot(p_n.astype(BF16), vnb[:, lo:hi], preferred_element_type=F32)
            o = (o / denom).astype(BF16)
            for h in range(GROUP):
                hh = g * GROUP + h
                o_ref[r0:r0 + s_len, hh * HEAD_DIM:(hh + 1) * HEAD_DIM] = o[h * s_len:(h + 1) * s_len, :]


def _attn_sample(q_all, k_all, v_all, cache_k, cache_v, rel_bias, sinks, row0, dec_batch, s_len):
    w = cache_k.shape[1]
    seqs = _pick_tile(dec_batch, (16, 8, 4, 2, 1))
    rows = seqs * s_len
    blk0 = row0 // rows
    assert row0 % rows == 0
    qi = jnp.arange(s_len)
    dist_c = qi[:, None] + w - jnp.arange(w)[None, :]
    dist_n = qi[:, None] - jnp.arange(s_len)[None, :]
    b_c = _bias_table(rel_bias, dist_c).reshape(N_KV_HEADS, GROUP * s_len, w)
    b_n = _bias_table(rel_bias, dist_n).reshape(N_KV_HEADS, GROUP * s_len, s_len)
    sink = jnp.broadcast_to(sinks.astype(F32).reshape(N_KV_HEADS, GROUP, 1, 1), (N_KV_HEADS, GROUP, s_len, 1))
    sink = sink.reshape(N_KV_HEADS, GROUP * s_len, 1)
    tok = lambda width: pl.BlockSpec((rows, width), lambda i: (blk0 + i, 0))
    cache = pl.BlockSpec((seqs, w, D_KV), lambda i: (i, 0, 0))
    return pl.pallas_call(
        functools.partial(_attn_sample_kernel, seqs=seqs, s_len=s_len),
        grid=(dec_batch // seqs,),
        in_specs=[
            tok(D_ATTN), tok(D_KV), tok(D_KV), cache, cache,
            _const_spec(b_c.shape), _const_spec(b_n.shape), _const_spec(sink.shape),
        ],
        out_specs=(pl.BlockSpec((rows, D_ATTN), lambda i: (i, 0)), cache, cache),
        out_shape=(
            jax.ShapeDtypeStruct((dec_batch * s_len, D_ATTN), BF16),
            jax.ShapeDtypeStruct((dec_batch, w, D_KV), F32),
            jax.ShapeDtypeStruct((dec_batch, w, D_KV), F32),
        ),
        compiler_params=_params(("parallel",)),
        name="attn_sample",
    )(q_all, k_all, v_all, cache_k, cache_v, b_c, b_n, sink)


def _softplus(z):
    return jnp.maximum(z, 0.0) + jnp.log1p(jnp.exp(-jnp.abs(z)))


def _block_gate(xcb, w_ref, b_ref):
    parts = [jnp.dot(xcb[:, n * RNN_BLOCK:(n + 1) * RNN_BLOCK], w_ref[n], preferred_element_type=F32)
             for n in range(N_RNN_BLOCKS)]
    return jax.nn.sigmoid(jnp.concatenate(parts, axis=-1) + b_ref[...])


def _lru_coeffs(xc, wa_ref, ba_ref, wx_ref, bx_ref, lam_ref, first_row_unnormalised):
    xcb = xc.astype(BF16)
    r = _block_gate(xcb, wa_ref, ba_ref)
    i = _block_gate(xcb, wx_ref, bx_ref)
    log_a = -RG_C * r * _softplus(-lam_ref[...])
    a = jnp.exp(log_a)
    mult = jnp.sqrt(-jnp.tanh(log_a) * (a * a + 1.0))
    if first_row_unnormalised is not None:
        mult = jnp.where(first_row_unnormalised, 1.0, mult)
    return a, mult * i * xc


def _scan8(a, b):
    shape = a.shape
    grouped = (shape[0] // SUBLANES, SUBLANES, shape[1])
    a = a.reshape(grouped)
    b = b.reshape(grouped)
    r8 = lax.broadcasted_iota(jnp.int32, grouped, 1)
    d = 1
    while d < SUBLANES:
        keep = r8 >= d
        a_sh = jnp.where(keep, pltpu.roll(a, d, 1), 1.0)
        b_sh = jnp.where(keep, pltpu.roll(b, d, 1), 0.0)
        b = a * b_sh + b
        a = a * a_sh
        d *= 2
    return a.reshape(shape), b.reshape(shape)


def _rnn_prompt_kernel(xr_ref, gy_ref, cw_ref, cb_ref, wa_ref, ba_ref, wx_ref, bx_ref, lam_ref,
                       o_ref, nh_ref, ext_ref, a_ref, b_ref, hc_ref, *, tl):
    l = pl.program_id(1)

    @pl.when(l == 0)
    def _():
        ext_ref[0:SUBLANES, :] = jnp.zeros((SUBLANES, D_RNN), F32)
        hc_ref[...] = jnp.zeros((1, D_RNN), F32)

    x = xr_ref[...]
    ext_ref[SUBLANES:, :] = x
    xc = cb_ref[...] + cw_ref[CONV_W - 1:CONV_W, :] * x
    for j in range(1, CONV_W):
        xc = xc + cw_ref[CONV_W - 1 - j:CONV_W - j, :] * ext_ref[SUBLANES - j:SUBLANES - j + tl, :]
    ext_ref[0:SUBLANES, :] = x[tl - SUBLANES:, :]

    row = lax.broadcasted_iota(jnp.int32, (tl, D_RNN), 0)
    a, b = _lru_coeffs(xc, wa_ref, ba_ref, wx_ref, bx_ref, lam_ref, (row == 0) & (l == 0))
    a, b = _scan8(a, b)
    a_ref[...] = a
    b_ref[...] = b

    def chunk(c, h):
        sl = pl.ds(pl.multiple_of(c * SUBLANES, SUBLANES), SUBLANES)
        hc = b_ref[sl, :] + a_ref[sl, :] * h
        b_ref[sl, :] = hc
        return hc[SUBLANES - 1:SUBLANES, :]

    h = lax.fori_loop(0, tl // SUBLANES, chunk, hc_ref[...])
    hc_ref[...] = h
    nh_ref[0] = h
    o_ref[...] = (b_ref[...] * gy_ref[...]).astype(BF16)


def _rnn_prompt(xr_all, gy_all, rnn_w, batch, seq):
    tl =_pick_tile(seq, (256, 128, 64, 32, 16, 8))
    nl = seq // tl
    tok = pl.BlockSpec((tl, D_RNN), lambda b, l: (b * nl + l, 0))
    return pl.pallas_call(
        functools.partial(_rnn_prompt_kernel, tl=tl),
        grid=(batch, nl),
        in_specs=[tok, tok] + [_const_spec(w.shape) for w in rnn_w],
        out_specs=(tok, pl.BlockSpec((1, 1, D_RNN), lambda b, l: (b, 0, 0))),
        out_shape=(jax.ShapeDtypeStruct((batch * seq, D_RNN), BF16), jax.ShapeDtypeStruct((batch, 1, D_RNN), F32)),
        scratch_shapes=[
            pltpu.VMEM((tl + SUBLANES, D_RNN), F32),
            pltpu.VMEM((tl, D_RNN), F32),
            pltpu.VMEM((tl, D_RNN), F32),
            pltpu.VMEM((1, D_RNN), F32),
        ],
        compiler_params=_params(("parallel", "arbitrary")),
        name="rnn_prompt",
    )(xr_all, gy_all, *rnn_w)


def _rnn_sample_kernel(xr_ref, gy_ref, hp_ref, h0_ref, cw_ref, cb_ref, wa_ref, ba_ref, wx_ref, bx_ref, lam_ref,
                       o_ref, nh_ref, *, seqs):
    rows = seqs * SUBLANES
    x = xr_ref[...]
    hp = hp_ref[...]
    r8 = lax.broadcasted_iota(jnp.int32, (rows, D_RNN), 0) % SUBLANES
    xc = cb_ref[...] + cw_ref[CONV_W - 1:CONV_W, :] * x
    for j in range(1, CONV_W):
        shifted = jnp.where(r8 >= j, pltpu.roll(x, j, 0), pltpu.roll(hp, rows - (SUBLANES - j), 0))
        xc = xc + cw_ref[CONV_W - 1 - j:CONV_W - j, :] * shifted
    a, b = _lru_coeffs(xc, wa_ref, ba_ref, wx_ref, bx_ref, lam_ref, None)
    a, b = _scan8(a, b)
    h0 = jnp.broadcast_to(h0_ref[...][:, None, :], (seqs, SUBLANES, D_RNN)).reshape(rows, D_RNN)
    h = b + a * h0
    last = jnp.where(r8 == SUBLANES - 1, h, 0.0).reshape(seqs, SUBLANES, D_RNN)
    nh_ref[...] = jnp.sum(last, axis=1)
    o_ref[...] = (h * gy_ref[...]).astype(BF16)


def _rnn_sample(xr_all, gy_all, hist_pad, h0, rnn_w, row0, dec_batch):
    seqs = _pick_tile(dec_batch, (16, 8))
    rows = seqs * SUBLANES
    assert row0 % rows == 0
    blk0 = row0 // rows
    tok = pl.BlockSpec((rows, D_RNN), lambda i: (blk0 + i, 0))
    return pl.pallas_call(
        functools.partial(_rnn_sample_kernel, seqs=seqs),
        grid=(dec_batch // seqs,),
        in_specs=[tok, tok, pl.BlockSpec((rows, D_RNN), lambda i: (i, 0)), pl.BlockSpec((seqs, D_RNN), lambda i: (i, 0))]
        + [_const_spec(w.shape) for w in rnn_w],
        out_specs=(pl.BlockSpec((rows, D_RNN), lambda i: (i, 0)), pl.BlockSpec((seqs, D_RNN), lambda i: (i, 0))),
        out_shape=(jax.ShapeDtypeStruct((dec_batch * SUBLANES, D_RNN), BF16),
                   jax.ShapeDtypeStruct((dec_batch, D_RNN), F32)),
        compiler_params=_params(("parallel",)),
        name="rnn_sample",
    )(xr_all, gy_all, hist_pad, h0, *rnn_w)


def _layer_norm(z, g, b):
    mu = jnp.mean(z, axis=-1, keepdims=True)
    zc = z - mu
    var = jnp.mean(zc * zc, axis=-1, keepdims=True)
    return zc * lax.rsqrt(var + LN_EPS) * g + b


def _first_index_of_max(vals, iota, axis, sentinel):
    mx = jnp.max(vals, axis=axis, keepdims=True)
    return mx, jnp.min(jnp.where(vals == mx, iota, sentinel), axis=axis, keepdims=True)


def _route(scores, bias):
    t = scores.shape[1]
    grp = scores + bias
    g3 = grp.reshape(N_GROUPS, GROUP_SIZE, t)
    e_in_g = lax.broadcasted_iota(jnp.int32, g3.shape, 1)
    m1, first = _first_index_of_max(g3, e_in_g, 1, GROUP_SIZE)
    m2 = jnp.max(jnp.where(e_in_g == first, -jnp.inf, g3), axis=1, keepdims=True)
    gscore = (m1 + m2).reshape(N_GROUPS, t)
    g_iota = lax.broadcasted_iota(jnp.int32, gscore.shape, 0)
    gmask = jnp.zeros(gscore.shape, jnp.bool_)
    for _ in range(TOPK_GROUPS):
        _, gi = _first_index_of_max(gscore, g_iota, 0, N_GROUPS)
        hit = g_iota == gi
        gmask = gmask | hit
        gscore = jnp.where(hit, -jnp.inf, gscore)
    masked = jnp.where(gmask[:, None, :], g3, -jnp.inf).reshape(N_EXPERTS, t)
    e_iota = lax.broadcasted_iota(jnp.int32, masked.shape, 0)
    idx, wts, hits = [], [], []
    for _ in range(TOP_K):
        _, ei = _first_index_of_max(masked, e_iota, 0, N_EXPERTS)
        hit = e_iota == ei
        idx.append(ei)
        hits.append(hit)
        wts.append(jnp.sum(jnp.where(hit, scores, 0.0), axis=0, keepdims=True))
        masked = jnp.where(hit, -jnp.inf, masked)
    idx = jnp.concatenate(idx, axis=0)
    w = jnp.concatenate(wts, axis=0)
    w = w / jnp.sum(w, axis=0, keepdims=True) * ROUTED_SCALE
    return idx, w, hits


def _merge_kernel(xp_ref, xs_ref, aop_ref, aos_ref, rop_ref, ros_ref, sga_ref, sgr_ref, woa_ref, wor_ref, wout_ref, g1_ref, b1_ref,
                  wrt_ref, rb_ref, wsg_ref, wsu_ref, wsd_ref,
                  x1w_ref, base_ref, idx_ref, wt_ref, rank_ref, cnt_ref, carry_ref, *, n_p, tm):
    i = pl.program_id(0)

    @pl.when(i == 0)
    def _():
        carry_ref[...] = jnp.zeros(carry_ref.shape, F32)

    is_prompt = i < n_p
    x = jnp.where(is_prompt, xp_ref[...], xs_ref[...])
    pa = jnp.dot(jnp.where(is_prompt, aop_ref[...], aos_ref[...]), woa_ref[...], preferred_element_type=F32)
    pr = jnp.dot(jnp.where(is_prompt, rop_ref[...], ros_ref[...]), wor_ref[...], preferred_element_type=F32)
    merged = sga_ref[...] * pa + sgr_ref[...] * pr
    z = ALPHA * x + jnp.dot(merged.astype(BF16), wout_ref[...], preferred_element_type=F32)
    x1 = _layer_norm(z, g1_ref[...], b1_ref[...])
    words = _pack_words(x1)
    for h in range(tm // SC_ROWS):
        for s in range(ROW_TILE):
            x1w_ref[(h * ROW_TILE + s) * SC_ROWS:(h * ROW_TILE + s + 1) * SC_ROWS, :] = (
                words[h * SC_ROWS:(h + 1) * SC_ROWS, s * LANES:(s + 1) * LANES])
    x1b = x1.astype(BF16)
    u = jax.nn.silu(jnp.dot(x1b, wsg_ref[...], preferred_element_type=F32)) * jnp.dot(x1b, wsu_ref[...], preferred_element_type=F32)
    shared = jnp.dot(u.astype(BF16), wsd_ref[...], preferred_element_type=F32)
    base_ref[...] = ALPHA * x1 + shared
    logits = lax.dot_general(wrt_ref[...], x1, (((1,), (1,)), ((), ())), preferred_element_type=F32)
    idx, w, hits = _route(jax.nn.sigmoid(logits), rb_ref[...])
    idx_ref[...] = idx
    wt_ref[...] = w

    chosen = functools.reduce(jnp.logical_or, hits)
    chosen_f = jnp.where(chosen, 1.0, 0.0)
    earlier = (lax.broadcasted_iota(jnp.int32, (tm, tm), 0) < lax.broadcasted_iota(jnp.int32, (tm, tm), 1))
    prefix = jnp.dot(chosen_f.astype(BF16), jnp.where(earlier, 1.0, 0.0).astype(BF16), preferred_element_type=F32)
    before = prefix + carry_ref[...]
    ranks = [jnp.sum(jnp.where(hit, before, 0.0), axis=0, keepdims=True) for hit in hits]
    rank_ref[...] = jnp.concatenate(ranks, axis=0).astype(jnp.int32)
    carry_ref[...] = carry_ref[...] + jnp.sum(chosen_f, axis=1, keepdims=True)
    cnt_ref[...] = carry_ref[...]


def _merge(x_p, x_s, attn_p, attn_s, rnn_p, rnn_s, sga, sgr, weights):
    t_p, t_s = x_p.shape[0], x_s.shape[0]
    t = t_p + t_s
    tm = _pick_tile(math.gcd(t_p, t_s), (256, 128))
    row = lambda width: pl.BlockSpec((tm, width), lambda i: (i, 0))
    col = pl.BlockSpec((TOP_K, tm), lambda i: (0, i))

    return pl.pallas_call(
        functools.partial(_merge_kernel, n_p=t_p // tm, tm=tm),
        grid=(t // tm,),
        in_specs=[*_two_source_specs(t_p, t_s, tm, D_MODEL), *_two_source_specs(t_p, t_s, tm, D_ATTN),
                  *_two_source_specs(t_p, t_s, tm, D_RNN), row(D_MODEL), row(D_MODEL)]
        + [_const_spec(w.shape) for w in weights],
        out_specs=(pl.BlockSpec((tm * ROW_TILE, LANES), lambda i: (i, 0)), row(D_MODEL), col, col, col,
                   _const_spec((N_EXPERTS, 1))),
        out_shape=(
            jax.ShapeDtypeStruct((t * ROW_TILE, LANES), WORD),
            jax.ShapeDtypeStruct((t, D_MODEL), F32),
            jax.ShapeDtypeStruct((TOP_K, t), jnp.int32),
            jax.ShapeDtypeStruct((TOP_K, t), F32),
            jax.ShapeDtypeStruct((TOP_K, t), jnp.int32),
            jax.ShapeDtypeStruct((N_EXPERTS, 1), F32),
        ),
        scratch_shapes=[pltpu.VMEM((N_EXPERTS, 1), F32)],
        compiler_params=_params(("arbitrary",)),
        name="merge_ln1_route",
    )(x_p, x_s, attn_p, attn_s, rnn_p, rnn_s, sga, sgr, *weights)


def _expert_chunk_rows(n_assign):
    return max(EXPERT_ROWS, -(-(n_assign // N_EXPERTS * 9 // 8) // EXPERT_ROWS) * EXPERT_ROWS)


def _expert_layout(counts, n_assign, chunk_rows):
    counts = counts.reshape(N_EXPERTS).astype(jnp.int32)
    padded = (counts + ROW_PAD - 1) // ROW_PAD * ROW_PAD
    pend = jnp.cumsum(padded)
    pstart = pend - padded
    rows_alloc = (n_assign + N_EXPERTS * (ROW_PAD - 1)) // ROW_PAD * ROW_PAD + chunk_rows
    n_chunks = jnp.maximum((counts + chunk_rows - 1) // chunk_rows, 1)
    part = CHUNK_PART
    written = jnp.maximum((counts + part - 1) // part, 1) * part
    ffn_end = jnp.max(pstart + written).reshape(1)
    return pstart, pstart + counts, pend, n_chunks, counts, ffn_end, rows_alloc


def _dest_kernel(idx_ref, rank_ref, pstart_ref, dest_ref, word_rows_ref):
    e_iota = lax.broadcasted_iota(jnp.int32, (N_EXPERTS, idx_ref.shape[1]), 0)
    starts = [jnp.sum(jnp.where(e_iota == idx_ref[k:k + 1, :], pstart_ref[...], 0), axis=0, keepdims=True)
              for k in range(TOP_K)]
    dest = jnp.concatenate(starts, axis=0) + rank_ref[...]
    dest_ref[...] = dest
    tm = dest.shape[1]
    word_rows_ref[0] = jnp.concatenate(
        [dest[k:k + 1, h * SC_ROWS:(h + 1) * SC_ROWS] * ROW_TILE + s
         for h in range(tm // SC_ROWS) for s in range(ROW_TILE) for k in range(TOP_K)], axis=0)


def _dest_rows(idx_t, rank_t, pstart, tm):
    t = idx_t.shape[1]
    col = pl.BlockSpec((TOP_K, tm), lambda i: (0, i))
    vecs = tm // SC_ROWS * ROW_TILE * TOP_K
    return pl.pallas_call(
        _dest_kernel,
        grid=(t // tm,),
        in_specs=[col, col, _const_spec((N_EXPERTS, 1))],
        out_specs=(col, pl.BlockSpec((1, vecs, SC_ROWS), lambda i: (i, 0, 0))),
        out_shape=(jax.ShapeDtypeStruct((TOP_K, t), jnp.int32),
                   jax.ShapeDtypeStruct((t // tm, vecs, SC_ROWS), jnp.int32)),
        compiler_params=_params(("parallel",)),
        name="dest_rows",
    )(idx_t, rank_t, pstart.reshape(N_EXPERTS, 1))


def _token_rows(r, n=1):
    return pl.ds(pl.multiple_of(r * ROW_TILE, ROW_TILE), n * ROW_TILE)


def _zero_row_groups(zero_ref, dst_ref, sem, first_group, n_groups):
    def start(g, c):
        pltpu.make_async_copy(zero_ref, dst_ref.at[_token_rows(g * ROW_PAD, ROW_PAD)], sem).start()
        return c

    lax.fori_loop(first_group, n_groups, start, 0)
    return n_groups - first_group


def _wait_zero_copies(zero_ref, dst_ref, sem, n):
    def wait(_, c):
        pltpu.make_async_copy(zero_ref, dst_ref.at[_token_rows(0, ROW_PAD)], sem).wait()
        return c

    lax.fori_loop(0, n, wait, 0)


SC_ROWS = 128


def _sc_workers():
    info = plsc.get_sparse_core_info()
    mesh = plsc.VectorSubcoreMesh(core_axis_name="c", subcore_axis_name="s")
    worker = lambda: lax.axis_index("s") * info.num_cores + lax.axis_index("c")
    return mesh, info.num_cores * info.num_subcores, worker


def _sc_scatter_rows(src, rows, n_out):
    mesh, n_workers, worker = _sc_workers()
    n_units = src.shape[0] // SC_ROWS
    assert n_units % n_workers == 0 and rows.shape == (n_units * TOP_K, SC_ROWS)
    per_worker = n_units // n_workers

    unit_bufs = [pltpu.VMEM((TOP_K, SC_ROWS), jnp.int32), pltpu.VMEM((SC_ROWS, LANES), src.dtype),
                 pltpu.SemaphoreType.DMA]

    @functools.partial(
        pl.kernel, mesh=mesh,
        out_type=jax.ShapeDtypeStruct((n_out, LANES), src.dtype),
        scratch_types=unit_bufs + unit_bufs + [pltpu.SemaphoreType.DMA],
    )
    def scatter(src_hbm, rows_hbm, out_hbm, rows_a, data_a, load_a, rows_b, data_b, load_b, sem):
        first = worker() * per_worker
        bufs = ((rows_a, data_a, load_a), (rows_b, data_b, load_b))

        def loads(u, buf):
            rows_v, data_v, load_sem = buf
            return (pltpu.make_async_copy(rows_hbm.at[pl.ds(u * TOP_K, TOP_K)], rows_v, load_sem),
                    pltpu.make_async_copy(src_hbm.at[pl.ds(u * SC_ROWS, SC_ROWS)], data_v, load_sem))

        def unit(u, buf, other, has_next):
            for cp in loads(u, buf):
                cp.wait()

            @pl.when(has_next)
            def _():
                for cp in loads(u + 1, other):
                    cp.start()

            rows_v, data_v, _ = buf
            copies = [pltpu.async_copy(data_v, out_hbm.at[rows_v.at[k]], sem) for k in range(TOP_K)]
            for cp in copies:
                cp.wait()

        for cp in loads(first, bufs[0]):
            cp.start()

        @pl.loop(0, per_worker // 2)
        def _(pair):
            u = first + 2 * pair
            unit(u, bufs[0], bufs[1], True)
            unit(u + 1, bufs[1], bufs[0], 2 * pair + 2 < per_worker)

        if per_worker % 2:
            unit(first + per_worker - 1, bufs[0], bufs[1], False)

    return scatter(src, rows)


def _zero_padding_kernel(lo_ref, hi_ref, tail_ref, xs_in_ref, xs_ref, zero_ref, sem, *, rows_alloc):
    del xs_in_ref
    zero_ref[...] = jnp.zeros(zero_ref.shape, WORD)
    row_copy = lambda r: pltpu.make_async_copy(zero_ref.at[pl.ds(0, ROW_TILE)], xs_ref.at[_token_rows(r)], sem)

    def expert(e, n):
        def row(r, c):
            row_copy(r).start()
            return c

        lax.fori_loop(lo_ref[e], hi_ref[e], row, 0)
        return n + hi_ref[e] - lo_ref[e]

    def wait(_, c):
        row_copy(0).wait()
        return c

    lax.fori_loop(0, lax.fori_loop(0, N_EXPERTS, expert, 0), wait, 0)
    n = _zero_row_groups(zero_ref, xs_ref, sem, tail_ref[0] // ROW_PAD, rows_alloc // ROW_PAD)
    _wait_zero_copies(zero_ref, xs_ref, sem, n)


def _zero_padding(xs, pad_lo, pad_hi, total, rows_alloc):
    grid_spec = pltpu.PrefetchScalarGridSpec(
        num_scalar_prefetch=3,
        grid=(1,),
        in_specs=[pl.BlockSpec(memory_space=pl.ANY)],
        out_specs=pl.BlockSpec(memory_space=pl.ANY),
        scratch_shapes=[pltpu.VMEM((ROW_PAD * ROW_TILE, LANES), WORD), pltpu.SemaphoreType.DMA],
    )
    return pl.pallas_call(
        functools.partial(_zero_padding_kernel, rows_alloc=rows_alloc),
        grid_spec=grid_spec,
        out_shape=jax.ShapeDtypeStruct(xs.shape, xs.dtype),
        input_output_aliases={3: 0},
        compiler_params=_params(("arbitrary",)),
        name="zero_padding",
    )(pad_lo, pad_hi, total, xs)


def _expert_kernel(pstart_ref, nch_ref, rows_ref, end_ref, wg_ref, wu_ref, wd_ref, xs_ref, o_ref,
                   xbuf_ref, obuf_ref, wgb_ref, wub_ref, wdb_ref, zero_ref, done_ref, in_sem, out_sem,
                   *, rows_alloc, chunk_rows):
    e = pl.program_id(0)
    n_e = pl.num_programs(0)
    start = pstart_ref[e]
    nch = nch_ref[e]
    buf_rows = chunk_rows * ROW_TILE
    part = CHUNK_PART
    chunk_parts = chunk_rows // part
    part_rows = part * ROW_TILE

    class _ChunkCopy:
        def __init__(self, whole, parts, n_parts):
            self.whole = whole
            self.parts = parts
            self.n_parts = n_parts

        def _each(self, act):
            pl.when(self.n_parts == chunk_parts)(functools.partial(act, self.whole))
            for p, cp in enumerate(self.parts):
                pl.when((p < self.n_parts) & (self.n_parts < chunk_parts))(functools.partial(act, cp))

        def start(self):
            self._each(lambda cp: cp.start())

        def wait(self):
            self._each(lambda cp: cp.wait())

    def parts_of(rows_left):
        return jnp.clip((rows_left + part - 1) // part, 1, chunk_parts)

    def in_copy(row, slot, n_parts):
        copy = lambda p, n: pltpu.make_async_copy(
            xs_ref.at[_token_rows(row + p * part, n * part)],
            xbuf_ref.at[pl.ds(slot * buf_rows + p * part_rows, n * part_rows)], in_sem.at[slot])
        return _ChunkCopy(copy(0, chunk_parts), [copy(p, 1) for p in range(chunk_parts - 1)], n_parts)

    def out_copy(row, slot, n_parts):
        copy = lambda p, n: pltpu.make_async_copy(
            obuf_ref.at[pl.ds(slot * buf_rows + p * part_rows, n * part_rows)],
            o_ref.at[_token_rows(row + p * part, n * part)], out_sem)
        return _ChunkCopy(copy(0, chunk_parts), [copy(p, 1) for p in range(chunk_parts - 1)], n_parts)

    @pl.when(e == 0)
    def _():
        done_ref[0] = 0
        xbuf_ref[...] = jnp.zeros(xbuf_ref.shape, WORD)
        obuf_ref[...] = jnp.zeros(obuf_ref.shape, WORD)
        in_copy(start, 0, parts_of(rows_ref[0])).start()

    wgb_ref[...] = wg_ref[...].astype(BF16)
    wub_ref[...] = wu_ref[...].astype(BF16)
    wdb_ref[...] = wd_ref[...].astype(BF16)
    done = done_ref[0]

    def chunk(c, carry):
        g = done + c
        slot = g % 2
        row = start + c * chunk_rows
        rows_left = rows_ref[e] - c * chunk_rows
        n_parts = parts_of(rows_left)
        in_copy(row, slot, n_parts).wait()
        last = c + 1 == nch
        next_e = jnp.minimum(e + 1, n_e - 1)
        next_row = jnp.where(last, pstart_ref[next_e], row + chunk_rows)
        next_left = jnp.where(last, rows_ref[next_e], rows_left - chunk_rows)

        @pl.when(jnp.logical_not(last & (e == n_e - 1)))
        def _():
            in_copy(next_row, 1 - slot, parts_of(next_left)).start()

        base = pl.multiple_of(slot * buf_rows, buf_rows)
        def sub_block(h):
            xb = jnp.concatenate(_load_token_rows(xbuf_ref, EXPERT_ROWS, row0=h * EXPERT_ROWS, base=base),
                                 axis=1).astype(BF16)
            gate = jnp.dot(xb, wgb_ref[...], preferred_element_type=F32)
            up = jnp.dot(xb, wub_ref[...], preferred_element_type=F32)
            act = (jax.nn.silu(gate) * up).astype(BF16)
            _store_token_rows(obuf_ref, jnp.dot(act, wdb_ref[...], preferred_element_type=F32),
                              EXPERT_ROWS, row0=h * EXPERT_ROWS, base=base)

        n_sub = chunk_rows // EXPERT_ROWS
        needed = jnp.clip((rows_left + EXPERT_ROWS - 1) // EXPERT_ROWS, 1, n_sub)
        for count in range(1, n_sub + 1):
            @pl.when(needed == count)
            def _():
                for h in range(count):
                    sub_block(h)

        @pl.when(g > 0)
        def _():
            out_copy(0, 0, done_ref[1]).wait()

        out_copy(row, slot, n_parts).start()
        done_ref[1] = n_parts
        return carry

    lax.fori_loop(0, nch, chunk, 0)
    done_ref[0] = done + nch

    @pl.when(e == n_e - 1)
    def _():
        out_copy(0, 0, done_ref[1]).wait()
        zero_ref[...] = jnp.zeros(zero_ref.shape, WORD)
        n = _zero_row_groups(zero_ref, o_ref, out_sem, end_ref[0] // ROW_PAD, rows_alloc // ROW_PAD)
        _wait_zero_copies(zero_ref, o_ref, out_sem, n)


def _expert_ffn(xs, pstart, n_chunks, rows, ffn_end, rows_alloc, chunk_rows, w_e_gate, w_e_up, w_e_down):
    weight = lambda shape: pl.BlockSpec((None, *shape), lambda e, *_: (e, 0, 0))
    grid_spec = pltpu.PrefetchScalarGridSpec(
        num_scalar_prefetch=4,
        grid=(N_EXPERTS,),
        in_specs=[weight((D_MODEL, D_EXPERT)), weight((D_MODEL, D_EXPERT)), weight((D_EXPERT, D_MODEL)),
                  pl.BlockSpec(memory_space=pl.ANY)],
        out_specs=pl.BlockSpec(memory_space=pl.ANY),
        scratch_shapes=[
            pltpu.VMEM((2 * chunk_rows * ROW_TILE, LANES), WORD),
            pltpu.VMEM((2 * chunk_rows * ROW_TILE, LANES), WORD),
            pltpu.VMEM((D_MODEL, D_EXPERT), BF16), pltpu.VMEM((D_MODEL, D_EXPERT), BF16),
            pltpu.VMEM((D_EXPERT, D_MODEL), BF16),
            pltpu.VMEM((ROW_PAD * ROW_TILE, LANES), WORD),
            pltpu.SMEM((2,), jnp.int32),
            pltpu.SemaphoreType.DMA((2,)), pltpu.SemaphoreType.DMA,
        ],
    )
    return pl.pallas_call(
        functools.partial(_expert_kernel, rows_alloc=rows_alloc, chunk_rows=chunk_rows),
        grid_spec=grid_spec,
        out_shape=jax.ShapeDtypeStruct((rows_alloc * ROW_TILE, LANES), WORD),
        compiler_params=_params(("arbitrary",)),
        name="expert_ffn",
    )(pstart, n_chunks, rows, ffn_end, w_e_gate, w_e_up, w_e_down, xs)


def _combine_head_kernel(dest_ref, dest_next_ref, w_ref, base_ref, g_ref, b_ref, outs_ref, y_ref, buf_ref, sem, *, tm):
    i = pl.program_id(0)
    slot_rows = TOP_K * tm

    def gather(d_ref, slot):
        def issue(t, c):
            for k in range(TOP_K):
                pltpu.make_async_copy(outs_ref.at[_token_rows(d_ref[k, t])],
                                      buf_ref.at[_token_rows(slot * slot_rows + k * tm + t)], sem.at[slot]).start()
            return c

        lax.fori_loop(0, tm, issue, 0)

    @pl.when(i == 0)
    def _():
        gather(dest_ref, 0)

    @pl.when(i + 1 < pl.num_programs(0))
    def _():
        gather(dest_next_ref, (i + 1) % 2)

    slot = i % 2
    for k in range(TOP_K):
        pltpu.make_async_copy(outs_ref.at[_token_rows(0, tm)], buf_ref.at[_token_rows(slot * slot_rows + k * tm, tm)],
                              sem.at[slot]).wait()

    w = w_ref[...]
    base = pl.multiple_of(slot * slot_rows * ROW_TILE, ROW_TILE)
    halves = [None, None]
    for k in range(TOP_K):
        wk = w[:, k:k + 1]
        for j, rows in enumerate(_load_token_rows(buf_ref, tm, row0=k * tm, base=base)):
            halves[j] = wk * rows if halves[j] is None else halves[j] + wk * rows
    y_ref[...] = _layer_norm(base_ref[...] + jnp.concatenate(halves, axis=1), g_ref[...], b_ref[...])


def _combine_head(out_sorted, dest, w_tok, base, g2, b2, tm, n_head):
    row = pl.BlockSpec((tm, D_MODEL), lambda i: (i, 0))
    return pl.pallas_call(
        functools.partial(_combine_head_kernel, tm=tm),
        grid=(n_head,),
        in_specs=[
            pl.BlockSpec((TOP_K, tm), lambda i: (0, i), memory_space=pltpu.SMEM),
            pl.BlockSpec((TOP_K, tm), lambda i: (0, jnp.minimum(i + 1, n_head - 1)), memory_space=pltpu.SMEM),
            pl.BlockSpec((tm, TOP_K), lambda i: (i, 0)),
            row, _const_spec((1, D_MODEL)), _const_spec((1, D_MODEL)),
            pl.BlockSpec(memory_space=pl.ANY),
        ],
        out_specs=row,
        out_shape=jax.ShapeDtypeStruct((n_head * tm, D_MODEL), F32),
        scratch_shapes=[pltpu.VMEM((2 * TOP_K * tm * ROW_TILE, LANES), WORD), pltpu.SemaphoreType.DMA((2,))],
        compiler_params=_params(("arbitrary",)),
        name="combine_head",
    )(dest, dest, w_tok, base, g2, b2, out_sorted)


def _sc_gather_rows(table, rows):
    mesh, n_workers, worker = _sc_workers()
    m = rows.shape[0]
    per_worker = m // n_workers
    assert m % n_workers == 0 and per_worker % SC_ROWS == 0
    in_flight = _pick_tile(per_worker // SC_ROWS, (4, 2, 1))
    step = SC_ROWS * in_flight

    @functools.partial(
        pl.kernel, mesh=mesh,
        out_type=jax.ShapeDtypeStruct((m, LANES), table.dtype),
        scratch_types=[pltpu.VMEM((step,), jnp.int32), pltpu.VMEM((step, LANES), table.dtype), pltpu.SemaphoreType.DMA],
    )
    def gather(table_hbm, rows_hbm, out_hbm, rows_v, data_v, sem):
        first = worker() * per_worker

        @pl.loop(0, per_worker // step)
        def _(it):
            off = first + it * step
            pltpu.sync_copy(rows_hbm.at[pl.ds(off, step)], rows_v)
            copies = [pltpu.async_copy(table_hbm.at[rows_v.at[pl.ds(j * SC_ROWS, SC_ROWS)]],
                                       data_v.at[pl.ds(j * SC_ROWS, SC_ROWS)], sem)
                      for j in range(in_flight)]
            for cp in copies:
                cp.wait()
            pltpu.sync_copy(data_v, out_hbm.at[pl.ds(off, step)])

    return gather(table, rows)


def _combine_tail_kernel(yh_ref, rows_ref, w_ref, base_ref, g_ref, b_ref, yp_ref, ys_ref, *, tm, n_head, n_p):
    i = pl.program_id(0)

    @pl.when(i < n_head)
    def _():
        yp_ref[...] = yh_ref[...]

    @pl.when(i >= n_head)
    def _():
        w = w_ref[...]
        lo = [None] * ROW_TILE
        hi = [None] * ROW_TILE
        for k in range(TOP_K):
            wk = w[:, k:k + 1]
            for s in range(ROW_TILE):
                vec = lambda h: ((h * ROW_TILE + s) * TOP_K + k) * SC_ROWS
                words = jnp.concatenate([rows_ref[vec(h):vec(h) + SC_ROWS, :] for h in range(tm // SC_ROWS)], axis=0)
                lo_s, hi_s = _unpack_words(words)
                lo[s] = wk * lo_s if lo[s] is None else lo[s] + wk * lo_s
                hi[s] = wk * hi_s if hi[s] is None else hi[s] + wk * hi_s
        y = _layer_norm(base_ref[...] + jnp.concatenate(lo + hi, axis=1), g_ref[...], b_ref[...])

        @pl.when(i < n_p)
        def _():
            yp_ref[...] = y

        @pl.when(i >= n_p)
        def _():
            ys_ref[...] = y


def _combine_tail(y_head, gathered, w_tok, base, g2, b2, t_p, t_s, tm, n_head):
    n_p = t_p // tm
    assert n_head <= n_p
    n_tiles = (t_p + t_s) // tm
    blk = TOP_K * ROW_TILE * tm
    row = pl.BlockSpec((tm, D_MODEL), lambda i: (i, 0))
    out_p, out_s = _two_source_specs(t_p, t_s, tm, D_MODEL)
    return pl.pallas_call(
        functools.partial(_combine_tail_kernel, tm=tm, n_head=n_head, n_p=n_p),
        grid=(n_tiles,),
        in_specs=[
            pl.BlockSpec((tm, D_MODEL), lambda i: (jnp.minimum(i, n_head - 1), 0)),
            pl.BlockSpec((blk, LANES), lambda i: (jnp.maximum(i - n_head, 0), 0)),
            pl.BlockSpec((tm, TOP_K), lambda i: (i, 0)),
            row, _const_spec((1, D_MODEL)), _const_spec((1, D_MODEL)),
        ],
        out_specs=(out_p, out_s),
        out_shape=(jax.ShapeDtypeStruct((t_p, D_MODEL), F32), jax.ShapeDtypeStruct((t_s, D_MODEL), F32)),
        compiler_params=_params(("arbitrary",)),
        name="combine_tail",
    )(y_head, gathered, w_tok, base, g2, b2)


def kernel(x_prompt, x_sample, cache_k, cache_v, state_conv, state_rnn, w_in, conv_w, conv_b, w_gate_a, b_gate_a, w_gate_x, b_gate_x, lru_lambda, rel_bias, sinks, w_o_attn, w_o_rnn, w_out, ln1_g, ln1_b, w_router, router_bias, w_e_gate, w_e_up, w_e_down, w_s_gate, w_s_up, w_s_down, ln2_g, ln2_b):
    assert w_in.shape[0] == DEPTH == 1
    batch, seq, _ = x_prompt.shape
    dec_batch, s_len, _ = x_sample.shape
    w_cache = cache_k.shape[2]
    assert s_len == SUBLANES and seq % WINDOW == 0 and w_cache == WINDOW
    t_p = batch * seq
    t_s = dec_batch * s_len
    vec = lambda a: a[0].reshape(1, -1).astype(F32)

    x_p = x_prompt.reshape(t_p, D_MODEL)
    x_s = x_sample.reshape(t_s, D_MODEL)
    q, k, v, xr, gy, sga, sgr = _inproj(x_p, x_s, w_in[0].astype(BF16))

    qi = jnp.arange(WINDOW)
    dist = qi[:, None] + WINDOW - jnp.arange(2 * WINDOW)[None, :]
    bias_p = _bias_table(rel_bias, dist)
    attn_p = _attn_prompt(q, k, v, bias_p, sinks[0], batch, seq)
    attn_s, k_s, v_s = _attn_sample(
        q, k, v, cache_k[0].reshape(dec_batch, w_cache, D_KV), cache_v[0].reshape(dec_batch, w_cache, D_KV),
        rel_bias, sinks[0], t_p, dec_batch, s_len)

    rnn_w = (conv_w[0], vec(conv_b), w_gate_a[0].astype(BF16), vec(b_gate_a), w_gate_x[0].astype(BF16),
             vec(b_gate_x), vec(lru_lambda))
    rnn_p, h_p = _rnn_prompt(xr, gy, rnn_w, batch, seq)
    hist_pad = jnp.pad(state_conv[0], ((0, 0), (SUBLANES - (CONV_W - 1), 0), (0, 0))).reshape(t_s, D_RNN)
    rnn_s, h_s = _rnn_sample(xr, gy, hist_pad, state_rnn[0], rnn_w, t_p, dec_batch)

    merge_w = (w_o_attn[0].astype(BF16), w_o_rnn[0].astype(BF16), w_out[0].astype(BF16), vec(ln1_g), vec(ln1_b),
               w_router[0].T, router_bias[0].reshape(N_EXPERTS, 1), w_s_gate[0].astype(BF16),
               w_s_up[0].astype(BF16), w_s_down[0].astype(BF16))
    x1w, base, idx_t, wt_t, rank_t, counts = _merge(x_p, x_s, attn_p, attn_s, rnn_p, rnn_s, sga, sgr, merge_w)

    n_assign = (t_p + t_s) * TOP_K
    chunk_rows = _expert_chunk_rows(n_assign)
    pstart, pad_lo, pad_hi, n_chunks, rows, ffn_end, rows_alloc = _expert_layout(counts, n_assign, chunk_rows)
    tm = _pick_tile(math.gcd(t_p, t_s), (256, 128))
    dest, word_rows = _dest_rows(idx_t, rank_t, pstart, tm)
    xs = _sc_scatter_rows(x1w, word_rows.reshape(-1, SC_ROWS), rows_alloc * ROW_TILE)
    xs = _zero_padding(xs, pad_lo, pad_hi, pad_hi[N_EXPERTS - 1:], rows_alloc)
    out_sorted = _expert_ffn(xs, pstart, n_chunks, rows, ffn_end, rows_alloc, chunk_rows,
                             w_e_gate[0], w_e_up[0], w_e_down[0])
    n_tiles = (t_p + t_s) // tm
    n_head = min(t_p // tm, n_tiles // 3)
    w_tok = wt_t.T
    g2, b2 = vec(ln2_g), vec(ln2_b)
    y_head = _combine_head(out_sorted, dest, w_tok, base, g2, b2, tm, n_head)
    gathered = _sc_gather_rows(out_sorted, word_rows[n_head:].reshape(-1))
    y_p, y_s = _combine_tail(y_head, gathered, w_tok, base, g2, b2, t_p, t_s, tm, n_head)
    y_p = y_p.reshape(batch, seq, D_MODEL)
    y_s = y_s.reshape(dec_batch, s_len, D_MODEL)
    kv5 = lambda a, b: a.reshape(1, b, WINDOW, N_KV_HEADS, HEAD_DIM)
    tail = lambda a, n: jnp.stack([lax.slice_in_dim(a, (b + 1) * seq - n, (b + 1) * seq) for b in range(batch)])
    k_p = kv5(tail(k, WINDOW), batch)
    v_p = kv5(tail(v, WINDOW), batch)
    conv_p = tail(xr, CONV_W - 1)[None]
    conv_s = xr[t_p:].reshape(dec_batch, s_len, D_RNN)[:, s_len - (CONV_W - 1):][None]
    return (y_p, y_s, k_p, v_p, conv_p, h_p.reshape(1, batch, D_RNN),
            kv5(k_s, dec_batch), kv5(v_s, dec_batch), conv_s, h_s.reshape(1, dec_batch, D_RNN))
```

```python
import functools
import math

import jax
import jax.numpy as jnp
from jax import lax
from jax.experimental import pallas as pl
from jax.experimental.pallas import tpu as pltpu
from jax.experimental.pallas import tpu_sc as plsc

F32 = jnp.float32
BF16 = jnp.bfloat16
WORD = jnp.int32

D_MODEL = 1024
N_HEADS = 8
N_KV_HEADS = 2
HEAD_DIM = 64
GROUP = N_HEADS // N_KV_HEADS
WINDOW = 128
D_ATTN = N_HEADS * HEAD_DIM
D_KV = N_KV_HEADS * HEAD_DIM
N_BUCKETS = 32
MAX_DISTANCE = 128
D_RNN = D_MODEL
RNN_BLOCK = 256
N_RNN_BLOCKS = D_RNN // RNN_BLOCK
CONV_W = 4
RG_C = 8.0
N_EXPERTS = 256
TOP_K = 8
N_GROUPS = 8
GROUP_SIZE = N_EXPERTS // N_GROUPS
TOPK_GROUPS = 4
D_EXPERT = D_MODEL // 4
ROUTED_SCALE = 2.5
LN_EPS = 1e-5
DEPTH = 1
ALPHA = (2 * DEPTH) ** 0.25
NEG_INF = -1e30
SM_SCALE = HEAD_DIM ** -0.5

O_Q = 0
O_K = D_ATTN
O_V = O_K + D_KV
O_XR = O_V + D_KV
O_YR = O_XR + D_RNN
O_GA = O_YR + D_RNN
O_GR = O_GA + D_MODEL
D_IN = O_GR + D_MODEL

SUBLANES = 8
VMEM_LIMIT_BYTES = 56 * 1024 * 1024
EXPERT_ROWS = 256
CHUNK_PART = 128
ROW_PAD = SUBLANES


def _params(sem):
    return pltpu.CompilerParams(dimension_semantics=sem, vmem_limit_bytes=VMEM_LIMIT_BYTES)


def _pick_tile(n, candidates):
    for c in candidates:
        if n % c == 0:
            return c
    raise ValueError(f"no tile for {n}")


def _const_spec(shape):
    nd = len(shape)
    return pl.BlockSpec(shape, lambda *_: (0,) * nd)


LANES = 128
ROW_WORDS = D_MODEL // 2
ROW_TILE = ROW_WORDS // LANES
HIGH_HALF = -65536


def _pack_words(mat):
    as_bits = lambda v: pltpu.bitcast(v.astype(BF16).astype(F32), WORD)
    return (as_bits(mat[:, ROW_WORDS:]) & HIGH_HALF) | lax.shift_right_logical(as_bits(mat[:, :ROW_WORDS]), 16)


def _unpack_words(words):
    return pltpu.bitcast(words << 16, F32), pltpu.bitcast(words & HIGH_HALF, F32)


def _store_token_rows(ref, mat, n, row0=0, base=0):
    words = _pack_words(mat)
    for s in range(ROW_TILE):
        ref[pl.ds(base + row0 * ROW_TILE + s, n, stride=ROW_TILE), :] = words[:, s * LANES:(s + 1) * LANES]


def _load_token_rows(ref, n, row0=0, base=0):
    words = jnp.concatenate(
        [ref[pl.ds(base + row0 * ROW_TILE + s, n, stride=ROW_TILE), :] for s in range(ROW_TILE)], axis=1)
    return _unpack_words(words)


def _two_source_specs(t_p, t_s, tm, width):
    n_p = t_p // tm
    assert t_p % tm == 0 and t_s % tm == 0
    return (pl.BlockSpec((tm, width), lambda i, *_: (jnp.minimum(i, n_p - 1), 0)),
            pl.BlockSpec((tm, width), lambda i, *_: (jnp.maximum(i - n_p, 0), 0)))


def _inproj_kernel(xp_ref, xs_ref, w_ref, q_ref, k_ref, v_ref, xr_ref, gy_ref, sga_ref, sgr_ref, *, n_p):
    x = jnp.where(pl.program_id(0) < n_p, xp_ref[...], xs_ref[...]).astype(BF16)

    def seg(lo, hi):
        return jnp.dot(x, w_ref[:, lo:hi], preferred_element_type=F32)

    q_ref[...] = seg(O_Q, O_K).astype(BF16)
    k_ref[...] = seg(O_K, O_V)
    v_ref[...] = seg(O_V, O_XR)
    xr_ref[...] = seg(O_XR, O_YR)
    gy_ref[...] = jax.nn.gelu(seg(O_YR, O_GA)).astype(BF16)
    sga_ref[...] = jax.nn.sigmoid(seg(O_GA, O_GR)).astype(BF16)
    sgr_ref[...] = jax.nn.sigmoid(seg(O_GR, D_IN)).astype(BF16)


def _inproj(x_p, x_s, w_in_bf16):
    t_p, t_s = x_p.shape[0], x_s.shape[0]
    t = t_p + t_s
    tm = _pick_tile(math.gcd(t_p, t_s), (256, 128, 64, 32, 16, 8))
    row = lambda width: pl.BlockSpec((tm, width), lambda i: (i, 0))
    out_shape = (
        jax.ShapeDtypeStruct((t, D_ATTN), BF16),
        jax.ShapeDtypeStruct((t, D_KV), F32),
        jax.ShapeDtypeStruct((t, D_KV), F32),
        jax.ShapeDtypeStruct((t, D_RNN), F32),
        jax.ShapeDtypeStruct((t, D_RNN), BF16),
        jax.ShapeDtypeStruct((t, D_MODEL), BF16),
        jax.ShapeDtypeStruct((t, D_MODEL), BF16),
    )
    return pl.pallas_call(
        functools.partial(_inproj_kernel, n_p=t_p // tm),
        grid=(t // tm,),
        in_specs=[*_two_source_specs(t_p, t_s, tm, D_MODEL), _const_spec((D_MODEL, D_IN))],
        out_specs=(row(D_ATTN), row(D_KV), row(D_KV), row(D_RNN), row(D_RNN), row(D_MODEL), row(D_MODEL)),
        out_shape=out_shape,
        compiler_params=_params(("parallel",)),
        name="inproj",
    )(x_p, x_s, w_in_bf16)


def _t5_bucket(dist):
    n = jnp.maximum(dist, 0)
    max_exact = N_BUCKETS // 2
    nf = jnp.maximum(n, 1).astype(F32)
    large = max_exact + (jnp.log(nf / max_exact) / math.log(MAX_DISTANCE / max_exact) * (N_BUCKETS - max_exact)).astype(jnp.int32)
    large = jnp.minimum(large, N_BUCKETS - 1)
    return jnp.where(n < max_exact, n, large)


def _bias_table(rel_bias, dist):
    bucket = _t5_bucket(dist)
    rb = rel_bias.astype(F32)
    out = jnp.zeros((N_HEADS, *dist.shape), F32)
    for j in range(N_BUCKETS):
        out = jnp.where(bucket[None] == j, rb[j][:, None, None], out)
    return out


def _softmax_pv(s, sink, v):
    m = jnp.maximum(jnp.max(s, axis=-1, keepdims=True), sink)
    p = jnp.exp(s - m)
    denom = jnp.sum(p, axis=-1, keepdims=True) + jnp.exp(sink - m)
    return jnp.dot(p.astype(BF16), v, preferred_element_type=F32), denom


def _attn_prompt_kernel(sink_ref, q_ref, kc_ref, kp_ref, vc_ref, vp_ref, bias_ref, o_ref):
    n = pl.program_id(1)
    kk = jnp.concatenate([kp_ref[...], kc_ref[...]], axis=0).astype(BF16)
    vv = jnp.concatenate([vp_ref[...], vc_ref[...]], axis=0).astype(BF16)
    q = q_ref[...]
    rows = lax.broadcasted_iota(jnp.int32, (WINDOW, 2 * WINDOW), 0)
    cols = lax.broadcasted_iota(jnp.int32, (WINDOW, 2 * WINDOW), 1)
    dist = rows + WINDOW - cols
    valid = (dist >= 0) & (dist <= WINDOW) & ((n > 0) | (cols >= WINDOW))
    for g in range(N_KV_HEADS):
        kg = kk[:, g * HEAD_DIM:(g + 1) * HEAD_DIM]
        vg = vv[:, g * HEAD_DIM:(g + 1) * HEAD_DIM]
        for h in range(GROUP):
            hh = g * GROUP + h
            qh = q[:, hh * HEAD_DIM:(hh + 1) * HEAD_DIM]
            s = lax.dot_general(qh, kg, (((1,), (1,)), ((), ())), preferred_element_type=F32) * SM_SCALE
            s = jnp.where(valid, s + bias_ref[hh], NEG_INF)
            o, denom = _softmax_pv(s, sink_ref[0, hh], vg)
            o_ref[:, hh * HEAD_DIM:(hh + 1) * HEAD_DIM] = (o / denom).astype(BF16)


def _attn_prompt(q_all, k_all, v_all, bias, sinks, batch, seq):
    nb = seq // WINDOW
    cur = lambda width: pl.BlockSpec((WINDOW, width), lambda b, n: (b * nb + n, 0))
    prev = lambda width: pl.BlockSpec((WINDOW, width), lambda b, n: (b * nb + jnp.maximum(n - 1, 0), 0))
    return pl.pallas_call(
        _attn_prompt_kernel,
        grid=(batch, nb),
        in_specs=[
            pl.BlockSpec(memory_space=pltpu.SMEM),
            cur(D_ATTN), cur(D_KV), prev(D_KV), cur(D_KV), prev(D_KV),
            _const_spec((N_HEADS, WINDOW, 2 * WINDOW)),
        ],
        out_specs=cur(D_ATTN),
        out_shape=jax.ShapeDtypeStruct((batch * seq, D_ATTN), BF16),
        compiler_params=_params(("parallel", "arbitrary")),
        name="attn_prompt",
    )(sinks.reshape(1, N_HEADS).astype(F32), q_all, k_all, k_all, v_all, v_all, bias)


def _attn_sample_kernel(q_ref, kn_ref, vn_ref, kc_ref, vc_ref, bc_ref, bn_ref, sink_ref,
                        o_ref, ko_ref, vo_ref, *, seqs, s_len):
    w = kc_ref.shape[1]
    rows_c = lax.broadcasted_iota(jnp.int32, (GROUP * s_len, w), 0) % s_len
    cols_c = lax.broadcasted_iota(jnp.int32, (GROUP * s_len, w), 1)
    dist_c = rows_c + w - cols_c
    valid_c = (dist_c >= 0) & (dist_c <= WINDOW)
    rows_n = lax.broadcasted_iota(jnp.int32, (GROUP * s_len, s_len), 0) % s_len
    cols_n = lax.broadcasted_iota(jnp.int32, (GROUP * s_len, s_len), 1)
    dist_n = rows_n - cols_n
    valid_n = (dist_n >= 0) & (dist_n <= WINDOW)
    for j in range(seqs):
        r0 = j * s_len
        qj = q_ref[r0:r0 + s_len, :]
        kc = kc_ref[j]
        vc = vc_ref[j]
        kn = kn_ref[r0:r0 + s_len, :]
        vn = vn_ref[r0:r0 + s_len, :]
        ko_ref[j, 0:w - s_len, :] = kc[s_len:, :]
        ko_ref[j, w - s_len:w, :] = kn
        vo_ref[j, 0:w - s_len, :] = vc[s_len:, :]
        vo_ref[j, w - s_len:w, :] = vn
        kcb, vcb, knb, vnb = kc.astype(BF16), vc.astype(BF16), kn.astype(BF16), vn.astype(BF16)
        for g in range(N_KV_HEADS):
            lo, hi = g * HEAD_DIM, (g + 1) * HEAD_DIM
            qs = jnp.concatenate(
                [qj[:, (g * GROUP + h) * HEAD_DIM:(g * GROUP + h + 1) * HEAD_DIM] for h in range(GROUP)], axis=0)
            nt = (((1,), (1,)), ((), ()))
            s_c = lax.dot_general(qs, kcb[:, lo:hi], nt, preferred_element_type=F32) * SM_SCALE
            s_n = lax.dot_general(qs, knb[:, lo:hi], nt, preferred_element_type=F32) * SM_SCALE
            s_c = jnp.where(valid_c, s_c + bc_ref[g], NEG_INF)
            s_n = jnp.where(valid_n, s_n + bn_ref[g], NEG_INF)
            sink = sink_ref[g]
            m = jnp.maximum(jnp.maximum(jnp.max(s_c, axis=-1, keepdims=True), jnp.max(s_n, axis=-1, keepdims=True)), sink)
            p_c = jnp.exp(s_c - m)
            p_n = jnp.exp(s_n - m)
            denom = jnp.sum(p_c, axis=-1, keepdims=True) + jnp.sum(p_n, axis=-1, keepdims=True) + jnp.exp(sink - m)
            o = jnp.dot(p_c.astype(BF16), vcb[:, lo:hi], preferred_element_type=F32)
            o = o + jnp.dot(p_n.astype(BF16), vnb[:, lo:hi], preferred_element_type=F32)
            o = (o / denom).astype(BF16)
            for h in range(GROUP):
                hh = g * GROUP + h
                o_ref[r0:r0 + s_len, hh * HEAD_DIM:(hh + 1) * HEAD_DIM] = o[h * s_len:(h + 1) * s_len, :]


def _attn_sample(q_all, k_all, v_all, cache_k, cache_v, rel_bias, sinks, row0, dec_batch, s_len):
    w = cache_k.shape[1]
    seqs = _pick_tile(dec_batch, (16, 8, 4, 2, 1))
    rows = seqs * s_len
    blk0 = row0 // rows
    assert row0 % rows == 0
    qi = jnp.arange(s_len)
    dist_c = qi[:, None] + w - jnp.arange(w)[None, :]
    dist_n = qi[:, None] - jnp.arange(s_len)[None, :]
    b_c = _bias_table(rel_bias, dist_c).reshape(N_KV_HEADS, GROUP * s_len, w)
    b_n = _bias_table(rel_bias, dist_n).reshape(N_KV_HEADS, GROUP * s_len, s_len)
    sink = jnp.broadcast_to(sinks.astype(F32).reshape(N_KV_HEADS, GROUP, 1, 1), (N_KV_HEADS, GROUP, s_len, 1))
    sink = sink.reshape(N_KV_HEADS, GROUP * s_len, 1)
    tok = lambda width: pl.BlockSpec((rows, width), lambda i: (blk0 + i, 0))
    cache = pl.BlockSpec((seqs, w, D_KV), lambda i: (i, 0, 0))
    return pl.pallas_call(
        functools.partial(_attn_sample_kernel, seqs=seqs, s_len=s_len),
        grid=(dec_batch // seqs,),
        in_specs=[
            tok(D_ATTN), tok(D_KV), tok(D_KV), cache, cache,
            _const_spec(b_c.shape), _const_spec(b_n.shape), _const_spec(sink.shape),
        ],
        out_specs=(pl.BlockSpec((rows, D_ATTN), lambda i: (i, 0)), cache, cache),
        out_shape=(
            jax.ShapeDtypeStruct((dec_batch * s_len, D_ATTN), BF16),
            jax.ShapeDtypeStruct((dec_batch, w, D_KV), F32),
            jax.ShapeDtypeStruct((dec_batch, w, D_KV), F32),
        ),
        compiler_params=_params(("parallel",)),
        name="attn_sample",
    )(q_all, k_all, v_all, cache_k, cache_v, b_c, b_n, sink)


def _softplus(z):
    return jnp.maximum(z, 0.0) + jnp.log1p(jnp.exp(-jnp.abs(z)))


def _block_gate(xcb, w_ref, b_ref):
    parts = [jnp.dot(xcb[:, n * RNN_BLOCK:(n + 1) * RNN_BLOCK], w_ref[n], preferred_element_type=F32)
             for n in range(N_RNN_BLOCKS)]
    return jax.nn.sigmoid(jnp.concatenate(parts, axis=-1) + b_ref[...])


def _lru_coeffs(xc, wa_ref, ba_ref, wx_ref, bx_ref, lam_ref, first_row_unnormalised):
    xcb = xc.astype(BF16)
    r = _block_gate(xcb, wa_ref, ba_ref)
    i = _block_gate(xcb, wx_ref, bx_ref)
    log_a = -RG_C * r * _softplus(-lam_ref[...])
    a = jnp.exp(log_a)
    mult = jnp.sqrt(-jnp.tanh(log_a) * (a * a + 1.0))
    if first_row_unnormalised is not None:
        mult = jnp.where(first_row_unnormalised, 1.0, mult)
    return a, mult * i * xc


def _scan8(a, b):
    shape = a.shape
    grouped = (shape[0] // SUBLANES, SUBLANES, shape[1])
    a = a.reshape(grouped)
    b = b.reshape(grouped)
    r8 = lax.broadcasted_iota(jnp.int32, grouped, 1)
    d = 1
    while d < SUBLANES:
        keep = r8 >= d
        a_sh = jnp.where(keep, pltpu.roll(a, d, 1), 1.0)
        b_sh = jnp.where(keep, pltpu.roll(b, d, 1), 0.0)
        b = a * b_sh + b
        a = a * a_sh
        d *= 2
    return a.reshape(shape), b.reshape(shape)


def _rnn_prompt_kernel(xr_ref, gy_ref, cw_ref, cb_ref, wa_ref, ba_ref, wx_ref, bx_ref, lam_ref,
                       o_ref, nh_ref, ext_ref, a_ref, b_ref, hc_ref, *, tl):
    l = pl.program_id(1)

    @pl.when(l == 0)
    def _():
        ext_ref[0:SUBLANES, :] = jnp.zeros((SUBLANES, D_RNN), F32)
        hc_ref[...] = jnp.zeros((1, D_RNN), F32)

    x = xr_ref[...]
    ext_ref[SUBLANES:, :] = x
    xc = cb_ref[...] + cw_ref[CONV_W - 1:CONV_W, :] * x
    for j in range(1, CONV_W):
        xc = xc + cw_ref[CONV_W - 1 - j:CONV_W - j, :] * ext_ref[SUBLANES - j:SUBLANES - j + tl, :]
    ext_ref[0:SUBLANES, :] = x[tl - SUBLANES:, :]

    row = lax.broadcasted_iota(jnp.int32, (tl, D_RNN), 0)
    a, b = _lru_coeffs(xc, wa_ref, ba_ref, wx_ref, bx_ref, lam_ref, (row == 0) & (l == 0))
    a, b = _scan8(a, b)
    a_ref[...] = a
    b_ref[...] = b

    def chunk(c, h):
        sl = pl.ds(pl.multiple_of(c * SUBLANES, SUBLANES), SUBLANES)
        hc = b_ref[sl, :] + a_ref[sl, :] * h
        b_ref[sl, :] = hc
        return hc[SUBLANES - 1:SUBLANES, :]

    h = lax.fori_loop(0, tl // SUBLANES, chunk, hc_ref[...])
    hc_ref[...] = h
    nh_ref[0] = h
    o_ref[...] = (b_ref[...] * gy_ref[...]).astype(BF16)


def _rnn_prompt(xr_all, gy_all, rnn_w, batch, seq):
    tl =_pick_tile(seq, (256, 128, 64, 32, 16, 8))
    nl = seq // tl
    tok = pl.BlockSpec((tl, D_RNN), lambda b, l: (b * nl + l, 0))
    return pl.pallas_call(
        functools.partial(_rnn_prompt_kernel, tl=tl),
        grid=(batch, nl),
        in_specs=[tok, tok] + [_const_spec(w.shape) for w in rnn_w],
        out_specs=(tok, pl.BlockSpec((1, 1, D_RNN), lambda b, l: (b, 0, 0))),
        out_shape=(jax.ShapeDtypeStruct((batch * seq, D_RNN), BF16), jax.ShapeDtypeStruct((batch, 1, D_RNN), F32)),
        scratch_shapes=[
            pltpu.VMEM((tl + SUBLANES, D_RNN), F32),
            pltpu.VMEM((tl, D_RNN), F32),
            pltpu.VMEM((tl, D_RNN), F32),
            pltpu.VMEM((1, D_RNN), F32),
        ],
        compiler_params=_params(("parallel", "arbitrary")),
        name="rnn_prompt",
    )(xr_all, gy_all, *rnn_w)


def _rnn_sample_kernel(xr_ref, gy_ref, hp_ref, h0_ref, cw_ref, cb_ref, wa_ref, ba_ref, wx_ref, bx_ref, lam_ref,
                       o_ref, nh_ref, *, seqs):
    rows = seqs * SUBLANES
    x = xr_ref[...]
    hp = hp_ref[...]
    r8 = lax.broadcasted_iota(jnp.int32, (rows, D_RNN), 0) % SUBLANES
    xc = cb_ref[...] + cw_ref[CONV_W - 1:CONV_W, :] * x
    for j in range(1, CONV_W):
        shifted = jnp.where(r8 >= j, pltpu.roll(x, j, 0), pltpu.roll(hp, rows - (SUBLANES - j), 0))
        xc = xc + cw_ref[CONV_W - 1 - j:CONV_W - j, :] * shifted
    a, b = _lru_coeffs(xc, wa_ref, ba_ref, wx_ref, bx_ref, lam_ref, None)
    a, b = _scan8(a, b)
    h0 = jnp.broadcast_to(h0_ref[...][:, None, :], (seqs, SUBLANES, D_RNN)).reshape(rows, D_RNN)
    h = b + a * h0
    last = jnp.where(r8 == SUBLANES - 1, h, 0.0).reshape(seqs, SUBLANES, D_RNN)
    nh_ref[...] = jnp.sum(last, axis=1)
    o_ref[...] = (h * gy_ref[...]).astype(BF16)


def _rnn_sample(xr_all, gy_all, hist_pad, h0, rnn_w, row0, dec_batch):
    seqs = _pick_tile(dec_batch, (16, 8))
    rows = seqs * SUBLANES
    assert row0 % rows == 0
    blk0 = row0 // rows
    tok = pl.BlockSpec((rows, D_RNN), lambda i: (blk0 + i, 0))
    return pl.pallas_call(
        functools.partial(_rnn_sample_kernel, seqs=seqs),
        grid=(dec_batch // seqs,),
        in_specs=[tok, tok, pl.BlockSpec((rows, D_RNN), lambda i: (i, 0)), pl.BlockSpec((seqs, D_RNN), lambda i: (i, 0))]
        + [_const_spec(w.shape) for w in rnn_w],
        out_specs=(pl.BlockSpec((rows, D_RNN), lambda i: (i, 0)), pl.BlockSpec((seqs, D_RNN), lambda i: (i, 0))),
        out_shape=(jax.ShapeDtypeStruct((dec_batch * SUBLANES, D_RNN), BF16),
                   jax.ShapeDtypeStruct((dec_batch, D_RNN), F32)),
        compiler_params=_params(("parallel",)),
        name="rnn_sample",
    )(xr_all, gy_all, hist_pad, h0, *rnn_w)


def _layer_norm(z, g, b):
    mu = jnp.mean(z, axis=-1, keepdims=True)
    zc = z - mu
    var = jnp.mean(zc * zc, axis=-1, keepdims=True)
    return zc * lax.rsqrt(var + LN_EPS) * g + b


def _first_index_of_max(vals, iota, axis, sentinel):
    mx = jnp.max(vals, axis=axis, keepdims=True)
    return mx, jnp.min(jnp.where(vals == mx, iota, sentinel), axis=axis, keepdims=True)


def _route(scores, bias):
    t = scores.shape[1]
    grp = scores + bias
    g3 = grp.reshape(N_GROUPS, GROUP_SIZE, t)
    e_in_g = lax.broadcasted_iota(jnp.int32, g3.shape, 1)
    m1, first = _first_index_of_max(g3, e_in_g, 1, GROUP_SIZE)
    m2 = jnp.max(jnp.where(e_in_g == first, -jnp.inf, g3), axis=1, keepdims=True)
    gscore = (m1 + m2).reshape(N_GROUPS, t)
    g_iota = lax.broadcasted_iota(jnp.int32, gscore.shape, 0)
    gmask = jnp.zeros(gscore.shape, jnp.bool_)
    for _ in range(TOPK_GROUPS):
        _, gi = _first_index_of_max(gscore, g_iota, 0, N_GROUPS)
        hit = g_iota == gi
        gmask = gmask | hit
        gscore = jnp.where(hit, -jnp.inf, gscore)
    masked = jnp.where(gmask[:, None, :], g3, -jnp.inf).reshape(N_EXPERTS, t)
    e_iota = lax.broadcasted_iota(jnp.int32, masked.shape, 0)
    idx, wts, hits = [], [], []
    for _ in range(TOP_K):
        _, ei = _first_index_of_max(masked, e_iota, 0, N_EXPERTS)
        hit = e_iota == ei
        idx.append(ei)
        hits.append(hit)
        wts.append(jnp.sum(jnp.where(hit, scores, 0.0), axis=0, keepdims=True))
        masked = jnp.where(hit, -jnp.inf, masked)
    idx = jnp.concatenate(idx, axis=0)
    w = jnp.concatenate(wts, axis=0)
    w = w / jnp.sum(w, axis=0, keepdims=True) * ROUTED_SCALE
    return idx, w, hits


def _merge_kernel(xp_ref, xs_ref, aop_ref, aos_ref, rop_ref, ros_ref, sga_ref, sgr_ref, woa_ref, wor_ref, wout_ref, g1_ref, b1_ref,
                  wrt_ref, rb_ref,
                  x1w_ref, base_ref, idx_ref, wt_ref, rank_ref, cnt_ref, carry_ref, *, n_p, tm):
    i = pl.program_id(0)

    @pl.when(i == 0)
    def _():
        carry_ref[...] = jnp.zeros(carry_ref.shape, F32)

    is_prompt = i < n_p
    x = jnp.where(is_prompt, xp_ref[...], xs_ref[...])
    pa = jnp.dot(jnp.where(is_prompt, aop_ref[...], aos_ref[...]), woa_ref[...], preferred_element_type=F32)
    pr = jnp.dot(jnp.where(is_prompt, rop_ref[...], ros_ref[...]), wor_ref[...], preferred_element_type=F32)
    merged = sga_ref[...] * pa + sgr_ref[...] * pr
    z = ALPHA * x + jnp.dot(merged.astype(BF16), wout_ref[...], preferred_element_type=F32)
    x1 = _layer_norm(z, g1_ref[...], b1_ref[...])
    words = _pack_words(x1)
    for h in range(tm // SC_ROWS):
        for s in range(ROW_TILE):
            x1w_ref[(h * ROW_TILE + s) * SC_ROWS:(h * ROW_TILE + s + 1) * SC_ROWS, :] = (
                words[h * SC_ROWS:(h + 1) * SC_ROWS, s * LANES:(s + 1) * LANES])
    base_ref[...] = ALPHA * x1
    logits = lax.dot_general(wrt_ref[...], x1, (((1,), (1,)), ((), ())), preferred_element_type=F32)
    idx, w, hits = _route(jax.nn.sigmoid(logits), rb_ref[...])
    idx_ref[...] = idx
    wt_ref[...] = w

    chosen = functools.reduce(jnp.logical_or, hits)
    chosen_f = jnp.where(chosen, 1.0, 0.0)
    earlier = (lax.broadcasted_iota(jnp.int32, (tm, tm), 0) < lax.broadcasted_iota(jnp.int32, (tm, tm), 1))
    prefix = jnp.dot(chosen_f.astype(BF16), jnp.where(earlier, 1.0, 0.0).astype(BF16), preferred_element_type=F32)
    before = prefix + carry_ref[...]
    ranks = [jnp.sum(jnp.where(hit, before, 0.0), axis=0, keepdims=True) for hit in hits]
    rank_ref[...] = jnp.concatenate(ranks, axis=0).astype(jnp.int32)
    carry_ref[...] = carry_ref[...] + jnp.sum(chosen_f, axis=1, keepdims=True)
    cnt_ref[...] = carry_ref[...]


def _merge(x_p, x_s, attn_p, attn_s, rnn_p, rnn_s, sga, sgr, weights):
    t_p, t_s = x_p.shape[0], x_s.shape[0]
    t = t_p + t_s
    tm = _pick_tile(math.gcd(t_p, t_s), (256, 128))
    row = lambda width: pl.BlockSpec((tm, width), lambda i: (i, 0))
    col = pl.BlockSpec((TOP_K, tm), lambda i: (0, i))

    return pl.pallas_call(
        functools.partial(_merge_kernel, n_p=t_p // tm, tm=tm),
        grid=(t // tm,),
        in_specs=[*_two_source_specs(t_p, t_s, tm, D_MODEL), *_two_source_specs(t_p, t_s, tm, D_ATTN),
                  *_two_source_specs(t_p, t_s, tm, D_RNN), row(D_MODEL), row(D_MODEL)]
        + [_const_spec(w.shape) for w in weights],
        out_specs=(pl.BlockSpec((tm * ROW_TILE, LANES), lambda i: (i, 0)), row(D_MODEL), col, col, col,
                   _const_spec((N_EXPERTS, 1))),
        out_shape=(
            jax.ShapeDtypeStruct((t * ROW_TILE, LANES), WORD),
            jax.ShapeDtypeStruct((t, D_MODEL), F32),
            jax.ShapeDtypeStruct((TOP_K, t), jnp.int32),
            jax.ShapeDtypeStruct((TOP_K, t), F32),
            jax.ShapeDtypeStruct((TOP_K, t), jnp.int32),
            jax.ShapeDtypeStruct((N_EXPERTS, 1), F32),
        ),
        scratch_shapes=[pltpu.VMEM((N_EXPERTS, 1), F32)],
        compiler_params=_params(("arbitrary",)),
        name="merge_ln1_route",
    )(x_p, x_s, attn_p, attn_s, rnn_p, rnn_s, sga, sgr, *weights)


def _shared_kernel(x1w_ref, base_ref, wsg_ref, wsu_ref, wsd_ref, out_ref, *, tm):
    halves = []
    for h in range(tm // SC_ROWS):
        unit = lambda s: (h * ROW_TILE + s) * SC_ROWS
        words = jnp.concatenate([x1w_ref[unit(s):unit(s) + SC_ROWS, :] for s in range(ROW_TILE)], axis=1)
        halves.append(jnp.concatenate(_unpack_words(words), axis=1))
    x1b = jnp.concatenate(halves, axis=0).astype(BF16)
    u = jax.nn.silu(jnp.dot(x1b, wsg_ref[...], preferred_element_type=F32)) * jnp.dot(x1b, wsu_ref[...], preferred_element_type=F32)
    out_ref[...] = base_ref[...] + jnp.dot(u.astype(BF16), wsd_ref[...], preferred_element_type=F32)


def _add_shared_expert(x1w, base, weights, tm):
    t = base.shape[0]
    row = pl.BlockSpec((tm, D_MODEL), lambda i: (i, 0))
    return pl.pallas_call(
        functools.partial(_shared_kernel, tm=tm),
        grid=(t // tm,),
        in_specs=[pl.BlockSpec((tm * ROW_TILE, LANES), lambda i: (i, 0)), row] + [_const_spec(w.shape) for w in weights],
        out_specs=row,
        out_shape=jax.ShapeDtypeStruct(base.shape, F32),
        input_output_aliases={1: 0},
        compiler_params=_params(("parallel",)),
        name="shared_expert",
    )(x1w, base, *weights)


def _expert_chunk_rows(n_assign):
    return max(EXPERT_ROWS, -(-(n_assign // N_EXPERTS * 9 // 8) // EXPERT_ROWS) * EXPERT_ROWS)


def _expert_layout(counts, n_assign, chunk_rows):
    counts = counts.reshape(N_EXPERTS).astype(jnp.int32)
    padded = (counts + ROW_PAD - 1) // ROW_PAD * ROW_PAD
    pend = jnp.cumsum(padded)
    pstart = pend - padded
    rows_alloc = (n_assign + N_EXPERTS * (ROW_PAD - 1)) // ROW_PAD * ROW_PAD + chunk_rows
    n_chunks = jnp.maximum((counts + chunk_rows - 1) // chunk_rows, 1)
    part = CHUNK_PART
    written = jnp.maximum((counts + part - 1) // part, 1) * part
    ffn_end = jnp.max(pstart + written).reshape(1)
    return pstart, pstart + counts, pend, n_chunks, counts, ffn_end, rows_alloc


def _dest_kernel(idx_ref, rank_ref, pstart_ref, dest_ref, word_rows_ref):
    e_iota = lax.broadcasted_iota(jnp.int32, (N_EXPERTS, idx_ref.shape[1]), 0)
    starts = [jnp.sum(jnp.where(e_iota == idx_ref[k:k + 1, :], pstart_ref[...], 0), axis=0, keepdims=True)
              for k in range(TOP_K)]
    dest = jnp.concatenate(starts, axis=0) + rank_ref[...]
    dest_ref[...] = dest
    tm = dest.shape[1]
    word_rows_ref[0] = jnp.concatenate(
        [dest[k:k + 1, h * SC_ROWS:(h + 1) * SC_ROWS] * ROW_TILE + s
         for h in range(tm // SC_ROWS) for s in range(ROW_TILE) for k in range(TOP_K)], axis=0)


def _dest_rows(idx_t, rank_t, pstart, tm):
    t = idx_t.shape[1]
    col = pl.BlockSpec((TOP_K, tm), lambda i: (0, i))
    vecs = tm // SC_ROWS * ROW_TILE * TOP_K
    return pl.pallas_call(
        _dest_kernel,
        grid=(t // tm,),
        in_specs=[col, col, _const_spec((N_EXPERTS, 1))],
        out_specs=(col, pl.BlockSpec((1, vecs, SC_ROWS), lambda i: (i, 0, 0))),
        out_shape=(jax.ShapeDtypeStruct((TOP_K, t), jnp.int32),
                   jax.ShapeDtypeStruct((t // tm, vecs, SC_ROWS), jnp.int32)),
        compiler_params=_params(("parallel",)),
        name="dest_rows",
    )(idx_t, rank_t, pstart.reshape(N_EXPERTS, 1))


def _token_rows(r, n=1):
    return pl.ds(pl.multiple_of(r * ROW_TILE, ROW_TILE), n * ROW_TILE)


def _zero_row_groups(zero_ref, dst_ref, sem, first_group, n_groups):
    def start(g, c):
        pltpu.make_async_copy(zero_ref, dst_ref.at[_token_rows(g * ROW_PAD, ROW_PAD)], sem).start()
        return c

    lax.fori_loop(first_group, n_groups, start, 0)
    return n_groups - first_group


def _wait_zero_copies(zero_ref, dst_ref, sem, n):
    def wait(_, c):
        pltpu.make_async_copy(zero_ref, dst_ref.at[_token_rows(0, ROW_PAD)], sem).wait()
        return c

    lax.fori_loop(0, n, wait, 0)


SC_ROWS = 128


def _sc_workers():
    info = plsc.get_sparse_core_info()
    mesh = plsc.VectorSubcoreMesh(core_axis_name="c", subcore_axis_name="s")
    worker = lambda: lax.axis_index("s") * info.num_cores + lax.axis_index("c")
    return mesh, info.num_cores * info.num_subcores, worker


def _sc_scatter_rows(src, rows, n_out):
    mesh, n_workers, worker = _sc_workers()
    n_units = src.shape[0] // SC_ROWS
    assert n_units % n_workers == 0 and rows.shape == (n_units * TOP_K, SC_ROWS)
    per_worker = n_units // n_workers

    unit_bufs = [pltpu.VMEM((TOP_K, SC_ROWS), jnp.int32), pltpu.VMEM((SC_ROWS, LANES), src.dtype),
                 pltpu.SemaphoreType.DMA]

    @functools.partial(
        pl.kernel, mesh=mesh,
        out_type=jax.ShapeDtypeStruct((n_out, LANES), src.dtype),
        scratch_types=unit_bufs + unit_bufs + [pltpu.SemaphoreType.DMA],
    )
    def scatter(src_hbm, rows_hbm, out_hbm, rows_a, data_a, load_a, rows_b, data_b, load_b, sem):
        first = worker() * per_worker
        bufs = ((rows_a, data_a, load_a), (rows_b, data_b, load_b))

        def loads(u, buf):
            rows_v, data_v, load_sem = buf
            return (pltpu.make_async_copy(rows_hbm.at[pl.ds(u * TOP_K, TOP_K)], rows_v, load_sem),
                    pltpu.make_async_copy(src_hbm.at[pl.ds(u * SC_ROWS, SC_ROWS)], data_v, load_sem))

        def unit(u, buf, other, has_next):
            for cp in loads(u, buf):
                cp.wait()

            @pl.when(has_next)
            def _():
                for cp in loads(u + 1, other):
                    cp.start()

            rows_v, data_v, _ = buf
            copies = [pltpu.async_copy(data_v, out_hbm.at[rows_v.at[k]], sem) for k in range(TOP_K)]
            for cp in copies:
                cp.wait()

        for cp in loads(first, bufs[0]):
            cp.start()

        @pl.loop(0, per_worker // 2)
        def _(pair):
            u = first + 2 * pair
            unit(u, bufs[0], bufs[1], True)
            unit(u + 1, bufs[1], bufs[0], 2 * pair + 2 < per_worker)

        if per_worker % 2:
            unit(first + per_worker - 1, bufs[0], bufs[1], False)

    return scatter(src, rows)


def _zero_padding_kernel(lo_ref, hi_ref, tail_ref, xs_in_ref, xs_ref, zero_ref, sem, *, rows_alloc):
    del xs_in_ref
    zero_ref[...] = jnp.zeros(zero_ref.shape, WORD)
    row_copy = lambda r: pltpu.make_async_copy(zero_ref.at[pl.ds(0, ROW_TILE)], xs_ref.at[_token_rows(r)], sem)

    def expert(e, n):
        def row(r, c):
            row_copy(r).start()
            return c

        lax.fori_loop(lo_ref[e], hi_ref[e], row, 0)
        return n + hi_ref[e] - lo_ref[e]

    def wait(_, c):
        row_copy(0).wait()
        return c

    lax.fori_loop(0, lax.fori_loop(0, N_EXPERTS, expert, 0), wait, 0)
    n = _zero_row_groups(zero_ref, xs_ref, sem, tail_ref[0] // ROW_PAD, rows_alloc // ROW_PAD)
    _wait_zero_copies(zero_ref, xs_ref, sem, n)


def _zero_padding(xs, pad_lo, pad_hi, total, rows_alloc):
    grid_spec = pltpu.PrefetchScalarGridSpec(
        num_scalar_prefetch=3,
        grid=(1,),
        in_specs=[pl.BlockSpec(memory_space=pl.ANY)],
        out_specs=pl.BlockSpec(memory_space=pl.ANY),
        scratch_shapes=[pltpu.VMEM((ROW_PAD * ROW_TILE, LANES), WORD), pltpu.SemaphoreType.DMA],
    )
    return pl.pallas_call(
        functools.partial(_zero_padding_kernel, rows_alloc=rows_alloc),
        grid_spec=grid_spec,
        out_shape=jax.ShapeDtypeStruct(xs.shape, xs.dtype),
        input_output_aliases={3: 0},
        compiler_params=_params(("arbitrary",)),
        name="zero_padding",
    )(pad_lo, pad_hi, total, xs)


def _expert_kernel(pstart_ref, nch_ref, rows_ref, end_ref, wg_ref, wu_ref, wd_ref, xs_ref, o_ref,
                   xbuf_ref, obuf_ref, wgb_ref, wub_ref, wdb_ref, zero_ref, done_ref, in_sem, out_sem,
                   *, rows_alloc, chunk_rows):
    e = pl.program_id(0)
    n_e = pl.num_programs(0)
    start = pstart_ref[e]
    nch = nch_ref[e]
    buf_rows = chunk_rows * ROW_TILE
    part = CHUNK_PART
    chunk_parts = chunk_rows // part
    part_rows = part * ROW_TILE

    class _ChunkCopy:
        def __init__(self, whole, parts, n_parts):
            self.whole = whole
            self.parts = parts
            self.n_parts = n_parts

        def _each(self, act):
            pl.when(self.n_parts == chunk_parts)(functools.partial(act, self.whole))
            for p, cp in enumerate(self.parts):
                pl.when((p < self.n_parts) & (self.n_parts < chunk_parts))(functools.partial(act, cp))

        def start(self):
            self._each(lambda cp: cp.start())

        def wait(self):
            self._each(lambda cp: cp.wait())

    def parts_of(rows_left):
        return jnp.clip((rows_left + part - 1) // part, 1, chunk_parts)

    def in_copy(row, slot, n_parts):
        copy = lambda p, n: pltpu.make_async_copy(
            xs_ref.at[_token_rows(row + p * part, n * part)],
            xbuf_ref.at[pl.ds(slot * buf_rows + p * part_rows, n * part_rows)], in_sem.at[slot])
        return _ChunkCopy(copy(0, chunk_parts), [copy(p, 1) for p in range(chunk_parts - 1)], n_parts)

    def out_copy(row, slot, n_parts):
        copy = lambda p, n: pltpu.make_async_copy(
            obuf_ref.at[pl.ds(slot * buf_rows + p * part_rows, n * part_rows)],
            o_ref.at[_token_rows(row + p * part, n * part)], out_sem)
        return _ChunkCopy(copy(0, chunk_parts), [copy(p, 1) for p in range(chunk_parts - 1)], n_parts)

    @pl.when(e == 0)
    def _():
        done_ref[0] = 0
        xbuf_ref[...] = jnp.zeros(xbuf_ref.shape, WORD)
        obuf_ref[...] = jnp.zeros(obuf_ref.shape, WORD)
        in_copy(start, 0, parts_of(rows_ref[0])).start()

    wgb_ref[...] = wg_ref[...].astype(BF16)
    wub_ref[...] = wu_ref[...].astype(BF16)
    wdb_ref[...] = wd_ref[...].astype(BF16)
    done = done_ref[0]

    def chunk(c, carry):
        g = done + c
        slot = g % 2
        row = start + c * chunk_rows
        rows_left = rows_ref[e] - c * chunk_rows
        n_parts = parts_of(rows_left)
        in_copy(row, slot, n_parts).wait()
        last = c + 1 == nch
        next_e = jnp.minimum(e + 1, n_e - 1)
        next_row = jnp.where(last, pstart_ref[next_e], row + chunk_rows)
        next_left = jnp.where(last, rows_ref[next_e], rows_left - chunk_rows)

        @pl.when(jnp.logical_not(last & (e == n_e - 1)))
        def _():
            in_copy(next_row, 1 - slot, parts_of(next_left)).start()

        base = pl.multiple_of(slot * buf_rows, buf_rows)
        def sub_block(h):
            xb = jnp.concatenate(_load_token_rows(xbuf_ref, EXPERT_ROWS, row0=h * EXPERT_ROWS, base=base),
                                 axis=1).astype(BF16)
            gate = jnp.dot(xb, wgb_ref[...], preferred_element_type=F32)
            up = jnp.dot(xb, wub_ref[...], preferred_element_type=F32)
            act = (jax.nn.silu(gate) * up).astype(BF16)
            _store_token_rows(obuf_ref, jnp.dot(act, wdb_ref[...], preferred_element_type=F32),
                              EXPERT_ROWS, row0=h * EXPERT_ROWS, base=base)

        n_sub = chunk_rows // EXPERT_ROWS
        needed = jnp.clip((rows_left + EXPERT_ROWS - 1) // EXPERT_ROWS, 1, n_sub)
        for count in range(1, n_sub + 1):
            @pl.when(needed == count)
            def _():
                for h in range(count):
                    sub_block(h)

        @pl.when(g > 0)
        def _():
            out_copy(0, 0, done_ref[1]).wait()

        out_copy(row, slot, n_parts).start()
        done_ref[1] = n_parts
        return carry

    lax.fori_loop(0, nch, chunk, 0)
    done_ref[0] = done + nch

    @pl.when(e == n_e - 1)
    def _():
        out_copy(0, 0, done_ref[1]).wait()
        zero_ref[...] = jnp.zeros(zero_ref.shape, WORD)
        n = _zero_row_groups(zero_ref, o_ref, out_sem, end_ref[0] // ROW_PAD, rows_alloc // ROW_PAD)
        _wait_zero_copies(zero_ref, o_ref, out_sem, n)


def _expert_ffn(xs, pstart, n_chunks, rows, ffn_end, rows_alloc, chunk_rows, w_e_gate, w_e_up, w_e_down):
    weight = lambda shape: pl.BlockSpec((None, *shape), lambda e, *_: (e, 0, 0))
    grid_spec = pltpu.PrefetchScalarGridSpec(
        num_scalar_prefetch=4,
        grid=(N_EXPERTS,),
        in_specs=[weight((D_MODEL, D_EXPERT)), weight((D_MODEL, D_EXPERT)), weight((D_EXPERT, D_MODEL)),
                  pl.BlockSpec(memory_space=pl.ANY)],
        out_specs=pl.BlockSpec(memory_space=pl.ANY),
        scratch_shapes=[
            pltpu.VMEM((2 * chunk_rows * ROW_TILE, LANES), WORD),
            pltpu.VMEM((2 * chunk_rows * ROW_TILE, LANES), WORD),
            pltpu.VMEM((D_MODEL, D_EXPERT), BF16), pltpu.VMEM((D_MODEL, D_EXPERT), BF16),
            pltpu.VMEM((D_EXPERT, D_MODEL), BF16),
            pltpu.VMEM((ROW_PAD * ROW_TILE, LANES), WORD),
            pltpu.SMEM((2,), jnp.int32),
            pltpu.SemaphoreType.DMA((2,)), pltpu.SemaphoreType.DMA,
        ],
    )
    return pl.pallas_call(
        functools.partial(_expert_kernel, rows_alloc=rows_alloc, chunk_rows=chunk_rows),
        grid_spec=grid_spec,
        out_shape=jax.ShapeDtypeStruct((rows_alloc * ROW_TILE, LANES), WORD),
        compiler_params=_params(("arbitrary",)),
        name="expert_ffn",
    )(pstart, n_chunks, rows, ffn_end, w_e_gate, w_e_up, w_e_down, xs)


def _combine_head_kernel(dest_ref, dest_next_ref, w_ref, base_ref, g_ref, b_ref, outs_ref, y_ref, buf_ref, sem, *, tm):
    i = pl.program_id(0)
    slot_rows = TOP_K * tm

    def gather(d_ref, slot):
        def issue(t, c):
            for k in range(TOP_K):
                pltpu.make_async_copy(outs_ref.at[_token_rows(d_ref[k, t])],
                                      buf_ref.at[_token_rows(slot * slot_rows + k * tm + t)], sem.at[slot]).start()
            return c

        lax.fori_loop(0, tm, issue, 0)

    @pl.when(i == 0)
    def _():
        gather(dest_ref, 0)

    @pl.when(i + 1 < pl.num_programs(0))
    def _():
        gather(dest_next_ref, (i + 1) % 2)

    slot = i % 2
    for k in range(TOP_K):
        pltpu.make_async_copy(outs_ref.at[_token_rows(0, tm)], buf_ref.at[_token_rows(slot * slot_rows + k * tm, tm)],
                              sem.at[slot]).wait()

    w = w_ref[...]
    base = pl.multiple_of(slot * slot_rows * ROW_TILE, ROW_TILE)
    halves = [None, None]
    for k in range(TOP_K):
        wk = w[:, k:k + 1]
        for j, rows in enumerate(_load_token_rows(buf_ref, tm, row0=k * tm, base=base)):
            halves[j] = wk * rows if halves[j] is None else halves[j] + wk * rows
    y_ref[...] = _layer_norm(base_ref[...] + jnp.concatenate(halves, axis=1), g_ref[...], b_ref[...])


def _combine_head(out_sorted, dest, w_tok, base, g2, b2, tm, n_head):
    row = pl.BlockSpec((tm, D_MODEL), lambda i: (i, 0))
    return pl.pallas_call(
        functools.partial(_combine_head_kernel, tm=tm),
        grid=(n_head,),
        in_specs=[
            pl.BlockSpec((TOP_K, tm), lambda i: (0, i), memory_space=pltpu.SMEM),
            pl.BlockSpec((TOP_K, tm), lambda i: (0, jnp.minimum(i + 1, n_head - 1)), memory_space=pltpu.SMEM),
            pl.BlockSpec((tm, TOP_K), lambda i: (i, 0)),
            row, _const_spec((1, D_MODEL)), _const_spec((1, D_MODEL)),
            pl.BlockSpec(memory_space=pl.ANY),
        ],
        out_specs=row,
        out_shape=jax.ShapeDtypeStruct((n_head * tm, D_MODEL), F32),
        scratch_shapes=[pltpu.VMEM((2 * TOP_K * tm * ROW_TILE, LANES), WORD), pltpu.SemaphoreType.DMA((2,))],
        compiler_params=_params(("arbitrary",)),
        name="combine_head",
    )(dest, dest, w_tok, base, g2, b2, out_sorted)


def _sc_gather_rows(table, rows):
    mesh, n_workers, worker = _sc_workers()
    m = rows.shape[0]
    per_worker = m // n_workers
    assert m % n_workers == 0 and per_worker % SC_ROWS == 0
    in_flight = _pick_tile(per_worker // SC_ROWS, (4, 2, 1))
    step = SC_ROWS * in_flight

    @functools.partial(
        pl.kernel, mesh=mesh,
        out_type=jax.ShapeDtypeStruct((m, LANES), table.dtype),
        scratch_types=[pltpu.VMEM((step,), jnp.int32), pltpu.VMEM((step, LANES), table.dtype), pltpu.SemaphoreType.DMA],
    )
    def gather(table_hbm, rows_hbm, out_hbm, rows_v, data_v, sem):
        first = worker() * per_worker

        @pl.loop(0, per_worker // step)
        def _(it):
            off = first + it * step
            pltpu.sync_copy(rows_hbm.at[pl.ds(off, step)], rows_v)
            copies = [pltpu.async_copy(table_hbm.at[rows_v.at[pl.ds(j * SC_ROWS, SC_ROWS)]],
                                       data_v.at[pl.ds(j * SC_ROWS, SC_ROWS)], sem)
                      for j in range(in_flight)]
            for cp in copies:
                cp.wait()
            pltpu.sync_copy(data_v, out_hbm.at[pl.ds(off, step)])

    return gather(table, rows)


def _combine_tail_kernel(yh_ref, rows_ref, w_ref, base_ref, g_ref, b_ref, yp_ref, ys_ref, *, tm, n_head, n_p):
    i = pl.program_id(0)

    @pl.when(i < n_head)
    def _():
        yp_ref[...] = yh_ref[...]

    @pl.when(i >= n_head)
    def _():
        w = w_ref[...]
        lo = [None] * ROW_TILE
        hi = [None] * ROW_TILE
        for k in range(TOP_K):
            wk = w[:, k:k + 1]
            for s in range(ROW_TILE):
                vec = lambda h: ((h * ROW_TILE + s) * TOP_K + k) * SC_ROWS
                words = jnp.concatenate([rows_ref[vec(h):vec(h) + SC_ROWS, :] for h in range(tm // SC_ROWS)], axis=0)
                lo_s, hi_s = _unpack_words(words)
                lo[s] = wk * lo_s if lo[s] is None else lo[s] + wk * lo_s
                hi[s] = wk * hi_s if hi[s] is None else hi[s] + wk * hi_s
        y = _layer_norm(base_ref[...] + jnp.concatenate(lo + hi, axis=1), g_ref[...], b_ref[...])

        @pl.when(i < n_p)
        def _():
            yp_ref[...] = y

        @pl.when(i >= n_p)
        def _():
            ys_ref[...] = y


def _combine_tail(y_head, gathered, w_tok, base, g2, b2, t_p, t_s, tm, n_head):
    n_p = t_p // tm
    assert n_head <= n_p
    n_tiles = (t_p + t_s) // tm
    blk = TOP_K * ROW_TILE * tm
    row = pl.BlockSpec((tm, D_MODEL), lambda i: (i, 0))
    out_p, out_s = _two_source_specs(t_p, t_s, tm, D_MODEL)
    return pl.pallas_call(
        functools.partial(_combine_tail_kernel, tm=tm, n_head=n_head, n_p=n_p),
        grid=(n_tiles,),
        in_specs=[
            pl.BlockSpec((tm, D_MODEL), lambda i: (jnp.minimum(i, n_head - 1), 0)),
            pl.BlockSpec((blk, LANES), lambda i: (jnp.maximum(i - n_head, 0), 0)),
            pl.BlockSpec((tm, TOP_K), lambda i: (i, 0)),
            row, _const_spec((1, D_MODEL)), _const_spec((1, D_MODEL)),
        ],
        out_specs=(out_p, out_s),
        out_shape=(jax.ShapeDtypeStruct((t_p, D_MODEL), F32), jax.ShapeDtypeStruct((t_s, D_MODEL), F32)),
        compiler_params=_params(("arbitrary",)),
        name="combine_tail",
    )(y_head, gathered, w_tok, base, g2, b2)


def kernel(x_prompt, x_sample, cache_k, cache_v, state_conv, state_rnn, w_in, conv_w, conv_b, w_gate_a, b_gate_a, w_gate_x, b_gate_x, lru_lambda, rel_bias, sinks, w_o_attn, w_o_rnn, w_out, ln1_g, ln1_b, w_router, router_bias, w_e_gate, w_e_up, w_e_down, w_s_gate, w_s_up, w_s_down, ln2_g, ln2_b):
    assert w_in.shape[0] == DEPTH == 1
    batch, seq, _ = x_prompt.shape
    dec_batch, s_len, _ = x_sample.shape
    w_cache = cache_k.shape[2]
    assert s_len == SUBLANES and seq % WINDOW == 0 and w_cache == WINDOW
    t_p = batch * seq
    t_s = dec_batch * s_len
    vec = lambda a: a[0].reshape(1, -1).astype(F32)

    x_p = x_prompt.reshape(t_p, D_MODEL)
    x_s = x_sample.reshape(t_s, D_MODEL)
    q, k, v, xr, gy, sga, sgr = _inproj(x_p, x_s, w_in[0].astype(BF16))

    qi = jnp.arange(WINDOW)
    dist = qi[:, None] + WINDOW - jnp.arange(2 * WINDOW)[None, :]
    bias_p = _bias_table(rel_bias, dist)
    attn_p = _attn_prompt(q, k, v, bias_p, sinks[0], batch, seq)
    attn_s, k_s, v_s = _attn_sample(
        q, k, v, cache_k[0].reshape(dec_batch, w_cache, D_KV), cache_v[0].reshape(dec_batch, w_cache, D_KV),
        rel_bias, sinks[0], t_p, dec_batch, s_len)

    rnn_w = (conv_w[0], vec(conv_b), w_gate_a[0].astype(BF16), vec(b_gate_a), w_gate_x[0].astype(BF16),
             vec(b_gate_x), vec(lru_lambda))
    rnn_p, h_p = _rnn_prompt(xr, gy, rnn_w, batch, seq)
    hist_pad = jnp.pad(state_conv[0], ((0, 0), (SUBLANES - (CONV_W - 1), 0), (0, 0))).reshape(t_s, D_RNN)
    rnn_s, h_s = _rnn_sample(xr, gy, hist_pad, state_rnn[0], rnn_w, t_p, dec_batch)

    merge_w = (w_o_attn[0].astype(BF16), w_o_rnn[0].astype(BF16), w_out[0].astype(BF16), vec(ln1_g), vec(ln1_b),
               w_router[0].T, router_bias[0].reshape(N_EXPERTS, 1))
    x1w, base, idx_t, wt_t, rank_t, counts = _merge(x_p, x_s, attn_p, attn_s, rnn_p, rnn_s, sga, sgr, merge_w)

    n_assign = (t_p + t_s) * TOP_K
    chunk_rows = _expert_chunk_rows(n_assign)
    pstart, pad_lo, pad_hi, n_chunks, rows, ffn_end, rows_alloc = _expert_layout(counts, n_assign, chunk_rows)
    tm = _pick_tile(math.gcd(t_p, t_s), (256, 128))
    dest, word_rows = _dest_rows(idx_t, rank_t, pstart, tm)
    xs = _sc_scatter_rows(x1w, word_rows.reshape(-1, SC_ROWS), rows_alloc * ROW_TILE)
    shared_w = (w_s_gate[0].astype(BF16), w_s_up[0].astype(BF16), w_s_down[0].astype(BF16))
    base = _add_shared_expert(x1w, base, shared_w, tm)
    xs = _zero_padding(xs, pad_lo, pad_hi, pad_hi[N_EXPERTS - 1:], rows_alloc)
    out_sorted = _expert_ffn(xs, pstart, n_chunks, rows, ffn_end, rows_alloc, chunk_rows,
                             w_e_gate[0], w_e_up[0], w_e_down[0])
    n_tiles = (t_p + t_s) // tm
    n_head = min(t_p // tm, n_tiles // 3)
    w_tok = wt_t.T
    g2, b2 = vec(ln2_g), vec(ln2_b)
    y_head = _combine_head(out_sorted, dest, w_tok, base, g2, b2, tm, n_head)
    gathered = _sc_gather_rows(out_sorted, word_rows[n_head:].reshape(-1))
    y_p, y_s = _combine_tail(y_head, gathered, w_tok, base, g2, b2, t_p, t_s, tm, n_head)
    y_p = y_p.reshape(batch, seq, D_MODEL)
    y_s = y_s.reshape(dec_batch, s_len, D_MODEL)
    kv5 = lambda a, b: a.reshape(1, b, WINDOW, N_KV_HEADS, HEAD_DIM)
    tail = lambda a, n: jnp.stack([lax.slice_in_dim(a, (b + 1) * seq - n, (b + 1) * seq) for b in range(batch)])
    k_p = kv5(tail(k, WINDOW), batch)
    v_p = kv5(tail(v, WINDOW), batch)
    conv_p = tail(xr, CONV_W - 1)[None]
    conv_s = xr[t_p:].reshape(dec_batch, s_len, D_RNN)[:, s_len - (CONV_W - 1):][None]
    return (y_p, y_s, k_p, v_p, conv_p, h_p.reshape(1, batch, D_RNN),
            kv5(k_s, dec_batch), kv5(v_s, dec_batch), conv_s, h_s.reshape(1, dec_batch, D_RNN))
```

```python
import functools
import math

import jax
import jax.numpy as jnp
from jax import lax
from jax.experimental import pallas as pl
from jax.experimental.pallas import tpu as pltpu
from jax.experimental.pallas import tpu_sc as plsc

F32 = jnp.float32
BF16 = jnp.bfloat16
WORD = jnp.int32

D_MODEL = 1024
N_HEADS = 8
N_KV_HEADS = 2
HEAD_DIM = 64
GROUP = N_HEADS // N_KV_HEADS
WINDOW = 128
D_ATTN = N_HEADS * HEAD_DIM
D_KV = N_KV_HEADS * HEAD_DIM
N_BUCKETS = 32
MAX_DISTANCE = 128
D_RNN = D_MODEL
RNN_BLOCK = 256
N_RNN_BLOCKS = D_RNN // RNN_BLOCK
CONV_W = 4
RG_C = 8.0
N_EXPERTS = 256
TOP_K = 8
N_GROUPS = 8
GROUP_SIZE = N_EXPERTS // N_GROUPS
TOPK_GROUPS = 4
D_EXPERT = D_MODEL // 4
ROUTED_SCALE = 2.5
LN_EPS = 1e-5
DEPTH = 1
ALPHA = (2 * DEPTH) ** 0.25
NEG_INF = -1e30
SM_SCALE = HEAD_DIM ** -0.5
assert math.frexp(SM_SCALE)[0] == 0.5

O_Q = 0
O_K = D_ATTN
O_V = O_K + D_KV
O_XR = O_V + D_KV
O_YR = O_XR + D_RNN
O_GA = O_YR + D_RNN
O_GR = O_GA + D_MODEL
D_IN = O_GR + D_MODEL

SUBLANES = 8
VMEM_LIMIT_BYTES = 56 * 1024 * 1024
EXPERT_ROWS = 256
CHUNK_PART = 128
ROW_PAD = SUBLANES


def _params(sem):
    return pltpu.CompilerParams(dimension_semantics=sem, vmem_limit_bytes=VMEM_LIMIT_BYTES)


def _pick_tile(n, candidates):
    for c in candidates:
        if n % c == 0:
            return c
    raise ValueError(f"no tile for {n}")


def _const_spec(shape):
    nd = len(shape)
    return pl.BlockSpec(shape, lambda *_: (0,) * nd)


LANES = 128
ROW_WORDS = D_MODEL // 2
ROW_TILE = ROW_WORDS // LANES
HIGH_HALF = -65536


def _pack_words(mat):
    as_bits = lambda v: pltpu.bitcast(v.astype(BF16).astype(F32), WORD)
    return (as_bits(mat[:, ROW_WORDS:]) & HIGH_HALF) | lax.shift_right_logical(as_bits(mat[:, :ROW_WORDS]), 16)


def _unpack_words(words):
    return pltpu.bitcast(words << 16, F32), pltpu.bitcast(words & HIGH_HALF, F32)


def _store_token_rows(ref, mat, n, row0=0, base=0):
    words = _pack_words(mat)
    for s in range(ROW_TILE):
        ref[pl.ds(base + row0 * ROW_TILE + s, n, stride=ROW_TILE), :] = words[:, s * LANES:(s + 1) * LANES]


def _load_token_rows(ref, n, row0=0, base=0):
    words = jnp.concatenate(
        [ref[pl.ds(base + row0 * ROW_TILE + s, n, stride=ROW_TILE), :] for s in range(ROW_TILE)], axis=1)
    return _unpack_words(words)


def _two_source_specs(t_p, t_s, tm, width):
    n_p = t_p // tm
    assert t_p % tm == 0 and t_s % tm == 0
    return (pl.BlockSpec((tm, width), lambda i, *_: (jnp.minimum(i, n_p - 1), 0)),
            pl.BlockSpec((tm, width), lambda i, *_: (jnp.maximum(i - n_p, 0), 0)))


def _inproj_kernel(xp_ref, xs_ref, w_ref, q_ref, k_ref, v_ref, xr_ref, gy_ref, sga_ref, sgr_ref, *, n_p):
    x = jnp.where(pl.program_id(0) < n_p, xp_ref[...], xs_ref[...]).astype(BF16)

    def seg(lo, hi):
        return jnp.dot(x, w_ref[:, lo:hi], preferred_element_type=F32)

    q_ref[...] = (seg(O_Q, O_K) * SM_SCALE).astype(BF16)
    k_ref[...] = seg(O_K, O_V)
    v_ref[...] = seg(O_V, O_XR)
    xr_ref[...] = seg(O_XR, O_YR)
    gy_ref[...] = jax.nn.gelu(seg(O_YR, O_GA)).astype(BF16)
    sga_ref[...] = jax.nn.sigmoid(seg(O_GA, O_GR)).astype(BF16)
    sgr_ref[...] = jax.nn.sigmoid(seg(O_GR, D_IN)).astype(BF16)


def _inproj(x_p, x_s, w_in_bf16):
    t_p, t_s = x_p.shape[0], x_s.shape[0]
    t = t_p + t_s
    tm = _pick_tile(math.gcd(t_p, t_s), (256, 128, 64, 32, 16, 8))
    row = lambda width: pl.BlockSpec((tm, width), lambda i: (i, 0))
    out_shape = (
        jax.ShapeDtypeStruct((t, D_ATTN), BF16),
        jax.ShapeDtypeStruct((t, D_KV), F32),
        jax.ShapeDtypeStruct((t, D_KV), F32),
        jax.ShapeDtypeStruct((t, D_RNN), F32),
        jax.ShapeDtypeStruct((t, D_RNN), BF16),
        jax.ShapeDtypeStruct((t, D_MODEL), BF16),
        jax.ShapeDtypeStruct((t, D_MODEL), BF16),
    )
    return pl.pallas_call(
        functools.partial(_inproj_kernel, n_p=t_p // tm),
        grid=(t // tm,),
        in_specs=[*_two_source_specs(t_p, t_s, tm, D_MODEL), _const_spec((D_MODEL, D_IN))],
        out_specs=(row(D_ATTN), row(D_KV), row(D_KV), row(D_RNN), row(D_RNN), row(D_MODEL), row(D_MODEL)),
        out_shape=out_shape,
        compiler_params=_params(("parallel",)),
        name="inproj",
    )(x_p, x_s, w_in_bf16)


def _t5_bucket(dist):
    n = jnp.maximum(dist, 0)
    max_exact = N_BUCKETS // 2
    nf = jnp.maximum(n, 1).astype(F32)
    large = max_exact + (jnp.log(nf / max_exact) / math.log(MAX_DISTANCE / max_exact) * (N_BUCKETS - max_exact)).astype(jnp.int32)
    large = jnp.minimum(large, N_BUCKETS - 1)
    return jnp.where(n < max_exact, n, large)


def _bias_table(rel_bias, dist):
    bucket = _t5_bucket(dist)
    rb = rel_bias.astype(F32)
    out = jnp.zeros((N_HEADS, *dist.shape), F32)
    for j in range(N_BUCKETS):
        out = jnp.where(bucket[None] == j, rb[j][:, None, None], out)
    return out


def _softmax_pv(s, sink, v):
    m = jnp.maximum(jnp.max(s, axis=-1, keepdims=True), sink)
    p = jnp.exp(s - m)
    denom = jnp.sum(p, axis=-1, keepdims=True) + jnp.exp(sink - m)
    return jnp.dot(p.astype(BF16), v, preferred_element_type=F32), denom


def _attn_prompt_kernel(sink_ref, q_ref, kc_ref, kp_ref, vc_ref, vp_ref, bias_ref, o_ref):
    kk = jnp.concatenate([kp_ref[...], kc_ref[...]], axis=0).astype(BF16)
    vv = jnp.concatenate([vp_ref[...], vc_ref[...]], axis=0).astype(BF16)
    q = q_ref[...]
    for g in range(N_KV_HEADS):
        kg = kk[:, g * HEAD_DIM:(g + 1) * HEAD_DIM]
        vg = vv[:, g * HEAD_DIM:(g + 1) * HEAD_DIM]
        for h in range(GROUP):
            hh = g * GROUP + h
            qh = q[:, hh * HEAD_DIM:(hh + 1) * HEAD_DIM]
            s = lax.dot_general(qh, kg, (((1,), (1,)), ((), ())), preferred_element_type=F32) + bias_ref[0, hh]
            o, denom = _softmax_pv(s, sink_ref[0, hh], vg)
            o_ref[:, hh * HEAD_DIM:(hh + 1) * HEAD_DIM] = (o / denom).astype(BF16)


def _attn_prompt(q_all, k_all, v_all, bias, sinks, batch, seq):
    nb = seq // WINDOW
    cur = lambda width: pl.BlockSpec((WINDOW, width), lambda b, n: (b * nb + n, 0))
    prev = lambda width: pl.BlockSpec((WINDOW, width), lambda b, n: (b * nb + jnp.maximum(n - 1, 0), 0))
    return pl.pallas_call(
        _attn_prompt_kernel,
        grid=(batch, nb),
        in_specs=[
            pl.BlockSpec(memory_space=pltpu.SMEM),
            cur(D_ATTN), cur(D_KV), prev(D_KV), cur(D_KV), prev(D_KV),
            pl.BlockSpec((1, N_HEADS, WINDOW, 2 * WINDOW), lambda b, n: (jnp.minimum(n, 1), 0, 0, 0)),
        ],
        out_specs=cur(D_ATTN),
        out_shape=jax.ShapeDtypeStruct((batch * seq, D_ATTN), BF16),
        compiler_params=_params(("parallel", "arbitrary")),
        name="attn_prompt",
    )(sinks.reshape(1, N_HEADS).astype(F32), q_all, k_all, k_all, v_all, v_all, bias)


def _attn_sample_kernel(q_ref, kn_ref, vn_ref, kc_ref, vc_ref, bc_ref, bn_ref, sink_ref,
                        o_ref, ko_ref, vo_ref, *, seqs, s_len):
    w = kc_ref.shape[1]
    rows_c = lax.broadcasted_iota(jnp.int32, (GROUP * s_len, w), 0) % s_len
    cols_c = lax.broadcasted_iota(jnp.int32, (GROUP * s_len, w), 1)
    dist_c = rows_c + w - cols_c
    valid_c = (dist_c >= 0) & (dist_c <= WINDOW)
    rows_n = lax.broadcasted_iota(jnp.int32, (GROUP * s_len, s_len), 0) % s_len
    cols_n = lax.broadcasted_iota(jnp.int32, (GROUP * s_len, s_len), 1)
    dist_n = rows_n - cols_n
    valid_n = (dist_n >= 0) & (dist_n <= WINDOW)
    for j in range(seqs):
        r0 = j * s_len
        qj = q_ref[r0:r0 + s_len, :]
        kc = kc_ref[j]
        vc = vc_ref[j]
        kn = kn_ref[r0:r0 + s_len, :]
        vn = vn_ref[r0:r0 + s_len, :]
        ko_ref[j, 0:w - s_len, :] = kc[s_len:, :]
        ko_ref[j, w - s_len:w, :] = kn
        vo_ref[j, 0:w - s_len, :] = vc[s_len:, :]
        vo_ref[j, w - s_len:w, :] = vn
        kcb, vcb, knb, vnb = kc.astype(BF16), vc.astype(BF16), kn.astype(BF16), vn.astype(BF16)
        for g in range(N_KV_HEADS):
            lo, hi = g * HEAD_DIM, (g + 1) * HEAD_DIM
            qs = jnp.concatenate(
                [qj[:, (g * GROUP + h) * HEAD_DIM:(g * GROUP + h + 1) * HEAD_DIM] for h in range(GROUP)], axis=0)
            nt = (((1,), (1,)), ((), ()))
            s_c = lax.dot_general(qs, kcb[:, lo:hi], nt, preferred_element_type=F32)
            s_n = lax.dot_general(qs, knb[:, lo:hi], nt, preferred_element_type=F32)
            s_c = jnp.where(valid_c, s_c + bc_ref[g], NEG_INF)
            s_n = jnp.where(valid_n, s_n + bn_ref[g], NEG_INF)
            sink = sink_ref[g]
            m = jnp.maximum(jnp.maximum(jnp.max(s_c, axis=-1, keepdims=True), jnp.max(s_n, axis=-1, keepdims=True)), sink)
            p_c = jnp.exp(s_c - m)
            p_n = jnp.exp(s_n - m)
            denom = jnp.sum(p_c, axis=-1, keepdims=True) + jnp.sum(p_n, axis=-1, keepdims=True) + jnp.exp(sink - m)
            o = jnp.dot(p_c.astype(BF16), vcb[:, lo:hi], preferred_element_type=F32)
            o = o + jnp.dot(p_n.astype(BF16), vnb[:, lo:hi], preferred_element_type=F32)
            o = (o / denom).astype(BF16)
            for h in range(GROUP):
                hh = g * GROUP + h
                o_ref[r0:r0 + s_len, hh * HEAD_DIM:(hh + 1) * HEAD_DIM] = o[h * s_len:(h + 1) * s_len, :]


def _attn_sample(q_all, k_all, v_all, cache_k, cache_v, rel_bias, sinks, row0, dec_batch, s_len):
    w = cache_k.shape[1]
    seqs = _pick_tile(dec_batch, (16, 8, 4, 2, 1))
    rows = seqs * s_len
    blk0 = row0 // rows
    assert row0 % rows == 0
    qi = jnp.arange(s_len)
    dist_c = qi[:, None] + w - jnp.arange(w)[None, :]
    dist_n = qi[:, None] - jnp.arange(s_len)[None, :]
    b_c = _bias_table(rel_bias, dist_c).reshape(N_KV_HEADS, GROUP * s_len, w)
    b_n = _bias_table(rel_bias, dist_n).reshape(N_KV_HEADS, GROUP * s_len, s_len)
    sink = jnp.broadcast_to(sinks.astype(F32).reshape(N_KV_HEADS, GROUP, 1, 1), (N_KV_HEADS, GROUP, s_len, 1))
    sink = sink.reshape(N_KV_HEADS, GROUP * s_len, 1)
    tok = lambda width: pl.BlockSpec((rows, width), lambda i: (blk0 + i, 0))
    cache = pl.BlockSpec((seqs, w, D_KV), lambda i: (i, 0, 0))
    return pl.pallas_call(
        functools.partial(_attn_sample_kernel, seqs=seqs, s_len=s_len),
        grid=(dec_batch // seqs,),
        in_specs=[
            tok(D_ATTN), tok(D_KV), tok(D_KV), cache, cache,
            _const_spec(b_c.shape), _const_spec(b_n.shape), _const_spec(sink.shape),
        ],
        out_specs=(pl.BlockSpec((rows, D_ATTN), lambda i: (i, 0)), cache, cache),
        out_shape=(
            jax.ShapeDtypeStruct((dec_batch * s_len, D_ATTN), BF16),
            jax.ShapeDtypeStruct((dec_batch, w, D_KV), F32),
            jax.ShapeDtypeStruct((dec_batch, w, D_KV), F32),
        ),
        compiler_params=_params(("parallel",)),
        name="attn_sample",
    )(q_all, k_all, v_all, cache_k, cache_v, b_c, b_n, sink)


def _softplus(z):
    return jnp.maximum(z, 0.0) + jnp.log1p(jnp.exp(-jnp.abs(z)))


def _block_gate(xcb, w_ref, b_ref):
    parts = [jnp.dot(xcb[:, n * RNN_BLOCK:(n + 1) * RNN_BLOCK], w_ref[n], preferred_element_type=F32)
             for n in range(N_RNN_BLOCKS)]
    return jax.nn.sigmoid(jnp.concatenate(parts, axis=-1) + b_ref[...])


def _lru_coeffs(xc, wa_ref, ba_ref, wx_ref, bx_ref, lam_ref, first_row_unnormalised):
    xcb = xc.astype(BF16)
    r = _block_gate(xcb, wa_ref, ba_ref)
    i = _block_gate(xcb, wx_ref, bx_ref)
    log_a = -RG_C * r * _softplus(-lam_ref[...])
    a = jnp.exp(log_a)
    mult = jnp.sqrt(-jnp.tanh(log_a) * (a * a + 1.0))
    if first_row_unnormalised is not None:
        mult = jnp.where(first_row_unnormalised, 1.0, mult)
    return a, mult * i * xc


def _scan8(a, b):
    shape = a.shape
    grouped = (shape[0] // SUBLANES, SUBLANES, shape[1])
    a = a.reshape(grouped)
    b = b.reshape(grouped)
    r8 = lax.broadcasted_iota(jnp.int32, grouped, 1)
    d = 1
    while d < SUBLANES:
        keep = r8 >= d
        a_sh = jnp.where(keep, pltpu.roll(a, d, 1), 1.0)
        b_sh = jnp.where(keep, pltpu.roll(b, d, 1), 0.0)
        b = a * b_sh + b
        a = a * a_sh
        d *= 2
    return a.reshape(shape), b.reshape(shape)


def _rnn_prompt_kernel(xr_ref, gy_ref, cw_ref, cb_ref, wa_ref, ba_ref, wx_ref, bx_ref, lam_ref,
                       o_ref, nh_ref, ext_ref, a_ref, b_ref, hc_ref, *, tl):
    l = pl.program_id(1)

    @pl.when(l == 0)
    def _():
        ext_ref[0:SUBLANES, :] = jnp.zeros((SUBLANES, D_RNN), F32)
        hc_ref[...] = jnp.zeros((1, D_RNN), F32)

    x = xr_ref[...]
    ext_ref[SUBLANES:, :] = x
    xc = cb_ref[...] + cw_ref[CONV_W - 1:CONV_W, :] * x
    for j in range(1, CONV_W):
        xc = xc + cw_ref[CONV_W - 1 - j:CONV_W - j, :] * ext_ref[SUBLANES - j:SUBLANES - j + tl, :]
    ext_ref[0:SUBLANES, :] = x[tl - SUBLANES:, :]

    row = lax.broadcasted_iota(jnp.int32, (tl, D_RNN), 0)
    a, b = _lru_coeffs(xc, wa_ref, ba_ref, wx_ref, bx_ref, lam_ref, (row == 0) & (l == 0))
    a, b = _scan8(a, b)
    a_ref[...] = a
    b_ref[...] = b

    def chunk(c, h):
        sl = pl.ds(pl.multiple_of(c * SUBLANES, SUBLANES), SUBLANES)
        hc = b_ref[sl, :] + a_ref[sl, :] * h
        b_ref[sl, :] = hc
        return hc[SUBLANES - 1:SUBLANES, :]

    h = lax.fori_loop(0, tl // SUBLANES, chunk, hc_ref[...])
    hc_ref[...] = h
    nh_ref[0] = h
    o_ref[...] = (b_ref[...] * gy_ref[...]).astype(BF16)


def _rnn_prompt(xr_all, gy_all, rnn_w, batch, seq):
    tl =_pick_tile(seq, (256, 128, 64, 32, 16, 8))
    nl = seq // tl
    tok = pl.BlockSpec((tl, D_RNN), lambda b, l: (b * nl + l, 0))
    return pl.pallas_call(
        functools.partial(_rnn_prompt_kernel, tl=tl),
        grid=(batch, nl),
        in_specs=[tok, tok] + [_const_spec(w.shape) for w in rnn_w],
        out_specs=(tok, pl.BlockSpec((1, 1, D_RNN), lambda b, l: (b, 0, 0))),
        out_shape=(jax.ShapeDtypeStruct((batch * seq, D_RNN), BF16), jax.ShapeDtypeStruct((batch, 1, D_RNN), F32)),
        scratch_shapes=[
            pltpu.VMEM((tl + SUBLANES, D_RNN), F32),
            pltpu.VMEM((tl, D_RNN), F32),
            pltpu.VMEM((tl, D_RNN), F32),
            pltpu.VMEM((1, D_RNN), F32),
        ],
        compiler_params=_params(("parallel", "arbitrary")),
        name="rnn_prompt",
    )(xr_all, gy_all, *rnn_w)


def _rnn_sample_kernel(xr_ref, gy_ref, hp_ref, h0_ref, cw_ref, cb_ref, wa_ref, ba_ref, wx_ref, bx_ref, lam_ref,
                       o_ref, nh_ref, *, seqs):
    rows = seqs * SUBLANES
    x = xr_ref[...]
    hp = hp_ref[...]
    r8 = lax.broadcasted_iota(jnp.int32, (rows, D_RNN), 0) % SUBLANES
    xc = cb_ref[...] + cw_ref[CONV_W - 1:CONV_W, :] * x
    for j in range(1, CONV_W):
        shifted = jnp.where(r8 >= j, pltpu.roll(x, j, 0), pltpu.roll(hp, rows - (SUBLANES - j), 0))
        xc = xc + cw_ref[CONV_W - 1 - j:CONV_W - j, :] * shifted
    a, b = _lru_coeffs(xc, wa_ref, ba_ref, wx_ref, bx_ref, lam_ref, None)
    a, b = _scan8(a, b)
    h0 = jnp.broadcast_to(h0_ref[...][:, None, :], (seqs, SUBLANES, D_RNN)).reshape(rows, D_RNN)
    h = b + a * h0
    last = jnp.where(r8 == SUBLANES - 1, h, 0.0).reshape(seqs, SUBLANES, D_RNN)
    nh_ref[...] = jnp.sum(last, axis=1)
    o_ref[...] = (h * gy_ref[...]).astype(BF16)


def _rnn_sample(xr_all, gy_all, hist_pad, h0, rnn_w, row0, dec_batch):
    seqs = _pick_tile(dec_batch, (16, 8))
    rows = seqs * SUBLANES
    assert row0 % rows == 0
    blk0 = row0 // rows
    tok = pl.BlockSpec((rows, D_RNN), lambda i: (blk0 + i, 0))
    return pl.pallas_call(
        functools.partial(_rnn_sample_kernel, seqs=seqs),
        grid=(dec_batch // seqs,),
        in_specs=[tok, tok, pl.BlockSpec((rows, D_RNN), lambda i: (i, 0)), pl.BlockSpec((seqs, D_RNN), lambda i: (i, 0))]
        + [_const_spec(w.shape) for w in rnn_w],
        out_specs=(pl.BlockSpec((rows, D_RNN), lambda i: (i, 0)), pl.BlockSpec((seqs, D_RNN), lambda i: (i, 0))),
        out_shape=(jax.ShapeDtypeStruct((dec_batch * SUBLANES, D_RNN), BF16),
                   jax.ShapeDtypeStruct((dec_batch, D_RNN), F32)),
        compiler_params=_params(("parallel",)),
        name="rnn_sample",
    )(xr_all, gy_all, hist_pad, h0, *rnn_w)


def _layer_norm(z, g, b):
    mu = jnp.mean(z, axis=-1, keepdims=True)
    zc = z - mu
    var = jnp.mean(zc * zc, axis=-1, keepdims=True)
    return zc * lax.rsqrt(var + LN_EPS) * g + b


def _first_index_of_max(vals, iota, axis, sentinel):
    mx = jnp.max(vals, axis=axis, keepdims=True)
    return mx, jnp.min(jnp.where(vals == mx, iota, sentinel), axis=axis, keepdims=True)


def _route(scores, bias):
    t = scores.shape[1]
    grp = scores + bias
    g3 = grp.reshape(N_GROUPS, GROUP_SIZE, t)
    e_in_g = lax.broadcasted_iota(jnp.int32, g3.shape, 1)
    m1, first = _first_index_of_max(g3, e_in_g, 1, GROUP_SIZE)
    m2 = jnp.max(jnp.where(e_in_g == first, -jnp.inf, g3), axis=1, keepdims=True)
    gscore = (m1 + m2).reshape(N_GROUPS, t)
    g_iota = lax.broadcasted_iota(jnp.int32, gscore.shape, 0)
    gmask = jnp.zeros(gscore.shape, jnp.bool_)
    for _ in range(TOPK_GROUPS):
        _, gi = _first_index_of_max(gscore, g_iota, 0, N_GROUPS)
        hit = g_iota == gi
        gmask = gmask | hit
        gscore = jnp.where(hit, -jnp.inf, gscore)
    masked = jnp.where(gmask[:, None, :], g3, -jnp.inf).reshape(N_EXPERTS, t)
    e_iota = lax.broadcasted_iota(jnp.int32, masked.shape, 0)
    idx, wts, hits = [], [], []
    for _ in range(TOP_K):
        _, ei = _first_index_of_max(masked, e_iota, 0, N_EXPERTS)
        hit = e_iota == ei
        idx.append(ei)
        hits.append(hit)
        wts.append(jnp.sum(jnp.where(hit, scores, 0.0), axis=0, keepdims=True))
        masked = jnp.where(hit, -jnp.inf, masked)
    idx = jnp.concatenate(idx, axis=0)
    w = jnp.concatenate(wts, axis=0)
    w = w / jnp.sum(w, axis=0, keepdims=True) * ROUTED_SCALE
    return idx, w, hits


def _merge_kernel(xp_ref, xs_ref, aop_ref, aos_ref, rop_ref, ros_ref, sga_ref, sgr_ref, woa_ref, wor_ref, wout_ref, g1_ref, b1_ref,
                  wrt_ref, rb_ref, wsg_ref, wsu_ref, wsd_ref,
                  x1w_ref, base_ref, idx_ref, wt_ref, rank_ref, cnt_ref, carry_ref, *, n_p, tm):
    i = pl.program_id(0)

    @pl.when(i == 0)
    def _():
        carry_ref[...] = jnp.zeros(carry_ref.shape, F32)

    is_prompt = i < n_p
    x = jnp.where(is_prompt, xp_ref[...], xs_ref[...])
    pa = jnp.dot(jnp.where(is_prompt, aop_ref[...], aos_ref[...]), woa_ref[...], preferred_element_type=F32)
    pr = jnp.dot(jnp.where(is_prompt, rop_ref[...], ros_ref[...]), wor_ref[...], preferred_element_type=F32)
    merged = sga_ref[...] * pa + sgr_ref[...] * pr
    z = ALPHA * x + jnp.dot(merged.astype(BF16), wout_ref[...], preferred_element_type=F32)
    x1 = _layer_norm(z, g1_ref[...], b1_ref[...])
    words = _pack_words(x1)
    for h in range(tm // SC_ROWS):
        for s in range(ROW_TILE):
            x1w_ref[(h * ROW_TILE + s) * SC_ROWS:(h * ROW_TILE + s + 1) * SC_ROWS, :] = (
                words[h * SC_ROWS:(h + 1) * SC_ROWS, s * LANES:(s + 1) * LANES])
    x1b = x1.astype(BF16)
    u = jax.nn.silu(jnp.dot(x1b, wsg_ref[...], preferred_element_type=F32)) * jnp.dot(x1b, wsu_ref[...], preferred_element_type=F32)
    shared = jnp.dot(u.astype(BF16), wsd_ref[...], preferred_element_type=F32)
    base_ref[...] = ALPHA * x1 + shared
    logits = lax.dot_general(wrt_ref[...], x1, (((1,), (1,)), ((), ())), preferred_element_type=F32)
    idx, w, hits = _route(jax.nn.sigmoid(logits), rb_ref[...])
    idx_ref[...] = idx
    wt_ref[...] = w

    chosen = functools.reduce(jnp.logical_or, hits)
    chosen_f = jnp.where(chosen, 1.0, 0.0)
    earlier = (lax.broadcasted_iota(jnp.int32, (tm, tm), 0) < lax.broadcasted_iota(jnp.int32, (tm, tm), 1))
    prefix = jnp.dot(chosen_f.astype(BF16), jnp.where(earlier, 1.0, 0.0).astype(BF16), preferred_element_type=F32)
    before = prefix + carry_ref[...]
    ranks = [jnp.sum(jnp.where(hit, before, 0.0), axis=0, keepdims=True) for hit in hits]
    rank_ref[...] = jnp.concatenate(ranks, axis=0).astype(jnp.int32)
    carry_ref[...] = carry_ref[...] + jnp.sum(chosen_f, axis=1, keepdims=True)
    cnt_ref[...] = carry_ref[...]


def _merge(x_p, x_s, attn_p, attn_s, rnn_p, rnn_s, sga, sgr, weights):
    t_p, t_s = x_p.shape[0], x_s.shape[0]
    t = t_p + t_s
    tm = _pick_tile(math.gcd(t_p, t_s), (512, 256, 128))
    row = lambda width: pl.BlockSpec((tm, width), lambda i: (i, 0))
    col = pl.BlockSpec((TOP_K, tm), lambda i: (0, i))

    return pl.pallas_call(
        functools.partial(_merge_kernel, n_p=t_p // tm, tm=tm),
        grid=(t // tm,),
        in_specs=[*_two_source_specs(t_p, t_s, tm, D_MODEL), *_two_source_specs(t_p, t_s, tm, D_ATTN),
                  *_two_source_specs(t_p, t_s, tm, D_RNN), row(D_MODEL), row(D_MODEL)]
        + [_const_spec(w.shape) for w in weights],
        out_specs=(pl.BlockSpec((tm * ROW_TILE, LANES), lambda i: (i, 0)), row(D_MODEL), col, col, col,
                   _const_spec((N_EXPERTS, 1))),
        out_shape=(
            jax.ShapeDtypeStruct((t * ROW_TILE, LANES), WORD),
            jax.ShapeDtypeStruct((t, D_MODEL), F32),
            jax.ShapeDtypeStruct((TOP_K, t), jnp.int32),
            jax.ShapeDtypeStruct((TOP_K, t), F32),
            jax.ShapeDtypeStruct((TOP_K, t), jnp.int32),
            jax.ShapeDtypeStruct((N_EXPERTS, 1), F32),
        ),
        scratch_shapes=[pltpu.VMEM((N_EXPERTS, 1), F32)],
        compiler_params=_params(("arbitrary",)),
        name="merge_ln1_route",
    )(x_p, x_s, attn_p, attn_s, rnn_p, rnn_s, sga, sgr, *weights)


def _expert_chunk_rows(n_assign):
    return max(EXPERT_ROWS, -(-(n_assign // N_EXPERTS * 9 // 8) // EXPERT_ROWS) * EXPERT_ROWS)


def _expert_layout(counts, n_assign, chunk_rows):
    counts = counts.reshape(N_EXPERTS).astype(jnp.int32)
    padded = (counts + ROW_PAD - 1) // ROW_PAD * ROW_PAD
    pend = jnp.cumsum(padded)
    pstart = pend - padded
    rows_alloc = (n_assign + N_EXPERTS * (ROW_PAD - 1)) // ROW_PAD * ROW_PAD + chunk_rows
    n_chunks = jnp.maximum((counts + chunk_rows - 1) // chunk_rows, 1)
    part = CHUNK_PART
    written = jnp.maximum((counts + part - 1) // part, 1) * part
    ffn_end = jnp.max(pstart + written).reshape(1)
    return pstart, pstart + counts, pend, n_chunks, counts, ffn_end, rows_alloc


def _dest_kernel(idx_ref, rank_ref, pstart_ref, dest_ref, word_rows_ref):
    e_iota = lax.broadcasted_iota(jnp.int32, (N_EXPERTS, idx_ref.shape[1]), 0)
    starts = [jnp.sum(jnp.where(e_iota == idx_ref[k:k + 1, :], pstart_ref[...], 0), axis=0, keepdims=True)
              for k in range(TOP_K)]
    dest = jnp.concatenate(starts, axis=0) + rank_ref[...]
    dest_ref[...] = dest
    tm = dest.shape[1]
    word_rows_ref[0] = jnp.concatenate(
        [dest[k:k + 1, h * SC_ROWS:(h + 1) * SC_ROWS] * ROW_TILE + s
         for h in range(tm // SC_ROWS) for s in range(ROW_TILE) for k in range(TOP_K)], axis=0)


def _dest_rows(idx_t, rank_t, pstart, tm):
    t = idx_t.shape[1]
    col = pl.BlockSpec((TOP_K, tm), lambda i: (0, i))
    vecs = tm // SC_ROWS * ROW_TILE * TOP_K
    return pl.pallas_call(
        _dest_kernel,
        grid=(t // tm,),
        in_specs=[col, col, _const_spec((N_EXPERTS, 1))],
        out_specs=(col, pl.BlockSpec((1, vecs, SC_ROWS), lambda i: (i, 0, 0))),
        out_shape=(jax.ShapeDtypeStruct((TOP_K, t), jnp.int32),
                   jax.ShapeDtypeStruct((t // tm, vecs, SC_ROWS), jnp.int32)),
        compiler_params=_params(("parallel",)),
        name="dest_rows",
    )(idx_t, rank_t, pstart.reshape(N_EXPERTS, 1))


def _token_rows(r, n=1):
    return pl.ds(pl.multiple_of(r * ROW_TILE, ROW_TILE), n * ROW_TILE)


def _zero_row_groups(zero_ref, dst_ref, sem, first_group, n_groups):
    def start(g, c):
        pltpu.make_async_copy(zero_ref, dst_ref.at[_token_rows(g * ROW_PAD, ROW_PAD)], sem).start()
        return c

    lax.fori_loop(first_group, n_groups, start, 0)
    return n_groups - first_group


def _wait_zero_copies(zero_ref, dst_ref, sem, n):
    def wait(_, c):
        pltpu.make_async_copy(zero_ref, dst_ref.at[_token_rows(0, ROW_PAD)], sem).wait()
        return c

    lax.fori_loop(0, n, wait, 0)


SC_ROWS = 128


def _sc_workers():
    info = plsc.get_sparse_core_info()
    mesh = plsc.VectorSubcoreMesh(core_axis_name="c", subcore_axis_name="s")
    worker = lambda: lax.axis_index("s") * info.num_cores + lax.axis_index("c")
    return mesh, info.num_cores * info.num_subcores, worker


def _sc_scatter_rows(src, rows, n_out):
    mesh, n_workers, worker = _sc_workers()
    n_units = src.shape[0] // SC_ROWS
    assert n_units % n_workers == 0 and rows.shape == (n_units * TOP_K, SC_ROWS)
    per_worker = n_units // n_workers

    unit_bufs = [pltpu.VMEM((TOP_K, SC_ROWS), jnp.int32), pltpu.VMEM((SC_ROWS, LANES), src.dtype),
                 pltpu.SemaphoreType.DMA]

    @functools.partial(
        pl.kernel, mesh=mesh,
        out_type=jax.ShapeDtypeStruct((n_out, LANES), src.dtype),
        scratch_types=unit_bufs + unit_bufs + [pltpu.SemaphoreType.DMA],
    )
    def scatter(src_hbm, rows_hbm, out_hbm, rows_a, data_a, load_a, rows_b, data_b, load_b, sem):
        first = worker() * per_worker
        bufs = ((rows_a, data_a, load_a), (rows_b, data_b, load_b))

        def loads(u, buf):
            rows_v, data_v, load_sem = buf
            return (pltpu.make_async_copy(rows_hbm.at[pl.ds(u * TOP_K, TOP_K)], rows_v, load_sem),
                    pltpu.make_async_copy(src_hbm.at[pl.ds(u * SC_ROWS, SC_ROWS)], data_v, load_sem))

        def unit(u, buf, other, has_next):
            for cp in loads(u, buf):
                cp.wait()

            @pl.when(has_next)
            def _():
                for cp in loads(u + 1, other):
                    cp.start()

            rows_v, data_v, _ = buf
            copies = [pltpu.async_copy(data_v, out_hbm.at[rows_v.at[k]], sem) for k in range(TOP_K)]
            for cp in copies:
                cp.wait()

        for cp in loads(first, bufs[0]):
            cp.start()

        @pl.loop(0, per_worker // 2)
        def _(pair):
            u = first + 2 * pair
            unit(u, bufs[0], bufs[1], True)
            unit(u + 1, bufs[1], bufs[0], 2 * pair + 2 < per_worker)

        if per_worker % 2:
            unit(first + per_worker - 1, bufs[0], bufs[1], False)

    return scatter(src, rows)


def _zero_padding_kernel(lo_ref, hi_ref, tail_ref, xs_in_ref, xs_ref, zero_ref, sem, *, rows_alloc):
    del xs_in_ref
    zero_ref[...] = jnp.zeros(zero_ref.shape, WORD)
    row_copy = lambda r: pltpu.make_async_copy(zero_ref.at[pl.ds(0, ROW_TILE)], xs_ref.at[_token_rows(r)], sem)

    def expert(e, n):
        def row(r, c):
            row_copy(r).start()
            return c

        lax.fori_loop(lo_ref[e], hi_ref[e], row, 0)
        return n + hi_ref[e] - lo_ref[e]

    def wait(_, c):
        row_copy(0).wait()
        return c

    lax.fori_loop(0, lax.fori_loop(0, N_EXPERTS, expert, 0), wait, 0)
    n = _zero_row_groups(zero_ref, xs_ref, sem, tail_ref[0] // ROW_PAD, rows_alloc // ROW_PAD)
    _wait_zero_copies(zero_ref, xs_ref, sem, n)


def _zero_padding(xs, pad_lo, pad_hi, total, rows_alloc):
    grid_spec = pltpu.PrefetchScalarGridSpec(
        num_scalar_prefetch=3,
        grid=(1,),
        in_specs=[pl.BlockSpec(memory_space=pl.ANY)],
        out_specs=pl.BlockSpec(memory_space=pl.ANY),
        scratch_shapes=[pltpu.VMEM((ROW_PAD * ROW_TILE, LANES), WORD), pltpu.SemaphoreType.DMA],
    )
    return pl.pallas_call(
        functools.partial(_zero_padding_kernel, rows_alloc=rows_alloc),
        grid_spec=grid_spec,
        out_shape=jax.ShapeDtypeStruct(xs.shape, xs.dtype),
        input_output_aliases={3: 0},
        compiler_params=_params(("arbitrary",)),
        name="zero_padding",
    )(pad_lo, pad_hi, total, xs)


def _expert_kernel(pstart_ref, nch_ref, rows_ref, end_ref, wg_ref, wu_ref, wd_ref, xs_ref, o_ref,
                   xbuf_ref, obuf_ref, wgb_ref, wub_ref, wdb_ref, zero_ref, done_ref, in_sem, out_sem,
                   *, rows_alloc, chunk_rows):
    e = pl.program_id(0)
    n_e = pl.num_programs(0)
    start = pstart_ref[e]
    nch = nch_ref[e]
    buf_rows = chunk_rows * ROW_TILE
    part = CHUNK_PART
    chunk_parts = chunk_rows // part
    part_rows = part * ROW_TILE

    class _ChunkCopy:
        def __init__(self, whole, parts, n_parts):
            self.whole = whole
            self.parts = parts
            self.n_parts = n_parts

        def _each(self, act):
            pl.when(self.n_parts == chunk_parts)(functools.partial(act, self.whole))
            for p, cp in enumerate(self.parts):
                pl.when((p < self.n_parts) & (self.n_parts < chunk_parts))(functools.partial(act, cp))

        def start(self):
            self._each(lambda cp: cp.start())

        def wait(self):
            self._each(lambda cp: cp.wait())

    def parts_of(rows_left):
        return jnp.clip((rows_left + part - 1) // part, 1, chunk_parts)

    def in_copy(row, slot, n_parts):
        copy = lambda p, n: pltpu.make_async_copy(
            xs_ref.at[_token_rows(row + p * part, n * part)],
            xbuf_ref.at[pl.ds(slot * buf_rows + p * part_rows, n * part_rows)], in_sem.at[slot])
        return _ChunkCopy(copy(0, chunk_parts), [copy(p, 1) for p in range(chunk_parts - 1)], n_parts)

    def out_copy(row, slot, n_parts):
        copy = lambda p, n: pltpu.make_async_copy(
            obuf_ref.at[pl.ds(slot * buf_rows + p * part_rows, n * part_rows)],
            o_ref.at[_token_rows(row + p * part, n * part)], out_sem)
        return _ChunkCopy(copy(0, chunk_parts), [copy(p, 1) for p in range(chunk_parts - 1)], n_parts)

    @pl.when(e == 0)
    def _():
        done_ref[0] = 0
        xbuf_ref[...] = jnp.zeros(xbuf_ref.shape, WORD)
        obuf_ref[...] = jnp.zeros(obuf_ref.shape, WORD)
        in_copy(start, 0, parts_of(rows_ref[0])).start()

    wgb_ref[...] = wg_ref[...].astype(BF16)
    wub_ref[...] = wu_ref[...].astype(BF16)
    wdb_ref[...] = wd_ref[...].astype(BF16)
    done = done_ref[0]

    def chunk(c, carry):
        g = done + c
        slot = g % 2
        row = start + c * chunk_rows
        rows_left = rows_ref[e] - c * chunk_rows
        n_parts = parts_of(rows_left)
        in_copy(row, slot, n_parts).wait()
        last = c + 1 == nch
        next_e = jnp.minimum(e + 1, n_e - 1)
        next_row = jnp.where(last, pstart_ref[next_e], row + chunk_rows)
        next_left = jnp.where(last, rows_ref[next_e], rows_left - chunk_rows)

        @pl.when(jnp.logical_not(last & (e == n_e - 1)))
        def _():
            in_copy(next_row, 1 - slot, parts_of(next_left)).start()

        base = pl.multiple_of(slot * buf_rows, buf_rows)
        def sub_block(h):
            xb = jnp.concatenate(_load_token_rows(xbuf_ref, EXPERT_ROWS, row0=h * EXPERT_ROWS, base=base),
                                 axis=1).astype(BF16)
            gate = jnp.dot(xb, wgb_ref[...], preferred_element_type=F32)
            up = jnp.dot(xb, wub_ref[...], preferred_element_type=F32)
            act = (jax.nn.silu(gate) * up).astype(BF16)
            _store_token_rows(obuf_ref, jnp.dot(act, wdb_ref[...], preferred_element_type=F32),
                              EXPERT_ROWS, row0=h * EXPERT_ROWS, base=base)

        n_sub = chunk_rows // EXPERT_ROWS
        needed = jnp.clip((rows_left + EXPERT_ROWS - 1) // EXPERT_ROWS, 1, n_sub)
        for count in range(1, n_sub + 1):
            @pl.when(needed == count)
            def _():
                for h in range(count):
                    sub_block(h)

        @pl.when(g > 0)
        def _():
            out_copy(0, 0, done_ref[1]).wait()

        out_copy(row, slot, n_parts).start()
        done_ref[1] = n_parts
        return carry

    lax.fori_loop(0, nch, chunk, 0)
    done_ref[0] = done + nch

    @pl.when(e == n_e - 1)
    def _():
        out_copy(0, 0, done_ref[1]).wait()
        zero_ref[...] = jnp.zeros(zero_ref.shape, WORD)
        n = _zero_row_groups(zero_ref, o_ref, out_sem, end_ref[0] // ROW_PAD, rows_alloc // ROW_PAD)
        _wait_zero_copies(zero_ref, o_ref, out_sem, n)


def _expert_ffn(xs, pstart, n_chunks, rows, ffn_end, rows_alloc, chunk_rows, w_e_gate, w_e_up, w_e_down):
    weight = lambda shape: pl.BlockSpec((None, *shape), lambda e, *_: (e, 0, 0))
    grid_spec = pltpu.PrefetchScalarGridSpec(
        num_scalar_prefetch=4,
        grid=(N_EXPERTS,),
        in_specs=[weight((D_MODEL, D_EXPERT)), weight((D_MODEL, D_EXPERT)), weight((D_EXPERT, D_MODEL)),
                  pl.BlockSpec(memory_space=pl.ANY)],
        out_specs=pl.BlockSpec(memory_space=pl.ANY),
        scratch_shapes=[
            pltpu.VMEM((2 * chunk_rows * ROW_TILE, LANES), WORD),
            pltpu.VMEM((2 * chunk_rows * ROW_TILE, LANES), WORD),
            pltpu.VMEM((D_MODEL, D_EXPERT), BF16), pltpu.VMEM((D_MODEL, D_EXPERT), BF16),
            pltpu.VMEM((D_EXPERT, D_MODEL), BF16),
            pltpu.VMEM((ROW_PAD * ROW_TILE, LANES), WORD),
            pltpu.SMEM((2,), jnp.int32),
            pltpu.SemaphoreType.DMA((2,)), pltpu.SemaphoreType.DMA,
        ],
    )
    return pl.pallas_call(
        functools.partial(_expert_kernel, rows_alloc=rows_alloc, chunk_rows=chunk_rows),
        grid_spec=grid_spec,
        out_shape=jax.ShapeDtypeStruct((rows_alloc * ROW_TILE, LANES), WORD),
        compiler_params=_params(("arbitrary",)),
        name="expert_ffn",
    )(pstart, n_chunks, rows, ffn_end, w_e_gate, w_e_up, w_e_down, xs)


def _combine_head_kernel(dest_ref, dest_next_ref, w_ref, base_ref, g_ref, b_ref, outs_ref, y_ref, buf_ref, sem, *, tm):
    i = pl.program_id(0)
    slot_rows = TOP_K * tm

    def gather(d_ref, slot):
        def issue(t, c):
            for k in range(TOP_K):
                pltpu.make_async_copy(outs_ref.at[_token_rows(d_ref[k, t])],
                                      buf_ref.at[_token_rows(slot * slot_rows + k * tm + t)], sem.at[slot]).start()
            return c

        lax.fori_loop(0, tm, issue, 0)

    @pl.when(i == 0)
    def _():
        gather(dest_ref, 0)

    @pl.when(i + 1 < pl.num_programs(0))
    def _():
        gather(dest_next_ref, (i + 1) % 2)

    slot = i % 2
    for k in range(TOP_K):
        pltpu.make_async_copy(outs_ref.at[_token_rows(0, tm)], buf_ref.at[_token_rows(slot * slot_rows + k * tm, tm)],
                              sem.at[slot]).wait()

    w = w_ref[...]
    base = pl.multiple_of(slot * slot_rows * ROW_TILE, ROW_TILE)
    halves = [None, None]
    for k in range(TOP_K):
        wk = w[:, k:k + 1]
        for j, rows in enumerate(_load_token_rows(buf_ref, tm, row0=k * tm, base=base)):
            halves[j] = wk * rows if halves[j] is None else halves[j] + wk * rows
    y_ref[...] = _layer_norm(base_ref[...] + jnp.concatenate(halves, axis=1), g_ref[...], b_ref[...])


def _combine_head(out_sorted, dest, w_tok, base, g2, b2, tm, n_head):
    row = pl.BlockSpec((tm, D_MODEL), lambda i: (i, 0))
    return pl.pallas_call(
        functools.partial(_combine_head_kernel, tm=tm),
        grid=(n_head,),
        in_specs=[
            pl.BlockSpec((TOP_K, tm), lambda i: (0, i), memory_space=pltpu.SMEM),
            pl.BlockSpec((TOP_K, tm), lambda i: (0, jnp.minimum(i + 1, n_head - 1)), memory_space=pltpu.SMEM),
            pl.BlockSpec((tm, TOP_K), lambda i: (i, 0)),
            row, _const_spec((1, D_MODEL)), _const_spec((1, D_MODEL)),
            pl.BlockSpec(memory_space=pl.ANY),
        ],
        out_specs=row,
        out_shape=jax.ShapeDtypeStruct((n_head * tm, D_MODEL), F32),
        scratch_shapes=[pltpu.VMEM((2 * TOP_K * tm * ROW_TILE, LANES), WORD), pltpu.SemaphoreType.DMA((2,))],
        compiler_params=_params(("arbitrary",)),
        name="combine_head",
    )(dest, dest, w_tok, base, g2, b2, out_sorted)


def _sc_gather_rows(table, rows):
    mesh, n_workers, worker = _sc_workers()
    m = rows.shape[0]
    per_worker = m // n_workers
    assert m % n_workers == 0 and per_worker % SC_ROWS == 0
    in_flight = _pick_tile(per_worker // SC_ROWS, (4, 2, 1))
    step = SC_ROWS * in_flight

    @functools.partial(
        pl.kernel, mesh=mesh,
        out_type=jax.ShapeDtypeStruct((m, LANES), table.dtype),
        scratch_types=[pltpu.VMEM((step,), jnp.int32), pltpu.VMEM((step, LANES), table.dtype), pltpu.SemaphoreType.DMA],
    )
    def gather(table_hbm, rows_hbm, out_hbm, rows_v, data_v, sem):
        first = worker() * per_worker

        @pl.loop(0, per_worker // step)
        def _(it):
            off = first + it * step
            pltpu.sync_copy(rows_hbm.at[pl.ds(off, step)], rows_v)
            copies = [pltpu.async_copy(table_hbm.at[rows_v.at[pl.ds(j * SC_ROWS, SC_ROWS)]],
                                       data_v.at[pl.ds(j * SC_ROWS, SC_ROWS)], sem)
                      for j in range(in_flight)]
            for cp in copies:
                cp.wait()
            pltpu.sync_copy(data_v, out_hbm.at[pl.ds(off, step)])

    return gather(table, rows)


def _combine_tail_kernel(yh_ref, rows_ref, w_ref, base_ref, g_ref, b_ref, yp_ref, ys_ref, *, tm, n_head, n_p):
    i = pl.program_id(0)

    @pl.when(i < n_head)
    def _():
        yp_ref[...] = yh_ref[...]

    @pl.when(i >= n_head)
    def _():
        w = w_ref[...]
        lo = [None] * ROW_TILE
        hi = [None] * ROW_TILE
        for k in range(TOP_K):
            wk = w[:, k:k + 1]
            for s in range(ROW_TILE):
                vec = lambda h: ((h * ROW_TILE + s) * TOP_K + k) * SC_ROWS
                words = jnp.concatenate([rows_ref[vec(h):vec(h) + SC_ROWS, :] for h in range(tm // SC_ROWS)], axis=0)
                lo_s, hi_s = _unpack_words(words)
                lo[s] = wk * lo_s if lo[s] is None else lo[s] + wk * lo_s
                hi[s] = wk * hi_s if hi[s] is None else hi[s] + wk * hi_s
        y = _layer_norm(base_ref[...] + jnp.concatenate(lo + hi, axis=1), g_ref[...], b_ref[...])

        @pl.when(i < n_p)
        def _():
            yp_ref[...] = y

        @pl.when(i >= n_p)
        def _():
            ys_ref[...] = y


def _combine_tail(y_head, gathered, w_tok, base, g2, b2, t_p, t_s, tm, n_head):
    n_p = t_p // tm
    assert n_head <= n_p
    n_tiles = (t_p + t_s) // tm
    blk = TOP_K * ROW_TILE * tm
    row = pl.BlockSpec((tm, D_MODEL), lambda i: (i, 0))
    out_p, out_s = _two_source_specs(t_p, t_s, tm, D_MODEL)
    return pl.pallas_call(
        functools.partial(_combine_tail_kernel, tm=tm, n_head=n_head, n_p=n_p),
        grid=(n_tiles,),
        in_specs=[
            pl.BlockSpec((tm, D_MODEL), lambda i: (jnp.minimum(i, n_head - 1), 0)),
            pl.BlockSpec((blk, LANES), lambda i: (jnp.maximum(i - n_head, 0), 0)),
            pl.BlockSpec((tm, TOP_K), lambda i: (i, 0)),
            row, _const_spec((1, D_MODEL)), _const_spec((1, D_MODEL)),
        ],
        out_specs=(out_p, out_s),
        out_shape=(jax.ShapeDtypeStruct((t_p, D_MODEL), F32), jax.ShapeDtypeStruct((t_s, D_MODEL), F32)),
        compiler_params=_params(("arbitrary",)),
        name="combine_tail",
    )(y_head, gathered, w_tok, base, g2, b2)


def kernel(x_prompt, x_sample, cache_k, cache_v, state_conv, state_rnn, w_in, conv_w, conv_b, w_gate_a, b_gate_a, w_gate_x, b_gate_x, lru_lambda, rel_bias, sinks, w_o_attn, w_o_rnn, w_out, ln1_g, ln1_b, w_router, router_bias, w_e_gate, w_e_up, w_e_down, w_s_gate, w_s_up, w_s_down, ln2_g, ln2_b):
    assert w_in.shape[0] == DEPTH == 1
    batch, seq, _ = x_prompt.shape
    dec_batch, s_len, _ = x_sample.shape
    w_cache = cache_k.shape[2]
    assert s_len == SUBLANES and seq % WINDOW == 0 and w_cache == WINDOW
    t_p = batch * seq
    t_s = dec_batch * s_len
    vec = lambda a: a[0].reshape(1, -1).astype(F32)

    x_p = x_prompt.reshape(t_p, D_MODEL)
    x_s = x_sample.reshape(t_s, D_MODEL)
    q, k, v, xr, gy, sga, sgr = _inproj(x_p, x_s, w_in[0].astype(BF16))

    qi = jnp.arange(WINDOW)
    dist = qi[:, None] + WINDOW - jnp.arange(2 * WINDOW)[None, :]
    band = (dist >= 0) & (dist <= WINDOW)
    has_prev = jnp.arange(2 * WINDOW)[None, :] >= WINDOW
    bias_p = _bias_table(rel_bias, dist)
    bias_p = jnp.stack([jnp.where(band & has_prev, bias_p, NEG_INF), jnp.where(band, bias_p, NEG_INF)])
    attn_p = _attn_prompt(q, k, v, bias_p, sinks[0], batch, seq)
    attn_s, k_s, v_s = _attn_sample(
        q, k, v, cache_k[0].reshape(dec_batch, w_cache, D_KV), cache_v[0].reshape(dec_batch, w_cache, D_KV),
        rel_bias, sinks[0], t_p, dec_batch, s_len)

    rnn_w = (conv_w[0], vec(conv_b), w_gate_a[0].astype(BF16), vec(b_gate_a), w_gate_x[0].astype(BF16),
             vec(b_gate_x), vec(lru_lambda))
    rnn_p, h_p = _rnn_prompt(xr, gy, rnn_w, batch, seq)
    hist_pad = jnp.pad(state_conv[0], ((0, 0), (SUBLANES - (CONV_W - 1), 0), (0, 0))).reshape(t_s, D_RNN)
    rnn_s, h_s = _rnn_sample(xr, gy, hist_pad, state_rnn[0], rnn_w, t_p, dec_batch)

    merge_w = (w_o_attn[0].astype(BF16), w_o_rnn[0].astype(BF16), w_out[0].astype(BF16), vec(ln1_g), vec(ln1_b),
               w_router[0].T, router_bias[0].reshape(N_EXPERTS, 1), w_s_gate[0].astype(BF16),
               w_s_up[0].astype(BF16), w_s_down[0].astype(BF16))
    x1w, base, idx_t, wt_t, rank_t, counts = _merge(x_p, x_s, attn_p, attn_s, rnn_p, rnn_s, sga, sgr, merge_w)

    n_assign = (t_p + t_s) * TOP_K
    chunk_rows = _expert_chunk_rows(n_assign)
    pstart, pad_lo, pad_hi, n_chunks, rows, ffn_end, rows_alloc = _expert_layout(counts, n_assign, chunk_rows)
    tm = _pick_tile(math.gcd(t_p, t_s), (256, 128))
    dest, word_rows = _dest_rows(idx_t, rank_t, pstart, tm)
    xs = _sc_scatter_rows(x1w, word_rows.reshape(-1, SC_ROWS), rows_alloc * ROW_TILE)
    xs = _zero_padding(xs, pad_lo, pad_hi, pad_hi[N_EXPERTS - 1:], rows_alloc)
    out_sorted = _expert_ffn(xs, pstart, n_chunks, rows, ffn_end, rows_alloc, chunk_rows,
                             w_e_gate[0], w_e_up[0], w_e_down[0])
    n_tiles = (t_p + t_s) // tm
    n_head = min(t_p // tm, n_tiles // 3)
    w_tok = wt_t.T
    g2, b2 = vec(ln2_g), vec(ln2_b)
    y_head = _combine_head(out_sorted, dest, w_tok, base, g2, b2, tm, n_head)
    gathered = _sc_gather_rows(out_sorted, word_rows[n_head:].reshape(-1))
    y_p, y_s = _combine_tail(y_head, gathered, w_tok, base, g2, b2, t_p, t_s, tm, n_head)
    y_p = y_p.reshape(batch, seq, D_MODEL)
    y_s = y_s.reshape(dec_batch, s_len, D_MODEL)
    kv5 = lambda a, b: a.reshape(1, b, WINDOW, N_KV_HEADS, HEAD_DIM)
    tail = lambda a, n: jnp.stack([lax.slice_in_dim(a, (b + 1) * seq - n, (b + 1) * seq) for b in range(batch)])
    k_p = kv5(tail(k, WINDOW), batch)
    v_p = kv5(tail(v, WINDOW), batch)
    conv_p = tail(xr, CONV_W - 1)[None]
    conv_s = xr[t_p:].reshape(dec_batch, s_len, D_RNN)[:, s_len - (CONV_W - 1):][None]
    return (y_p, y_s, k_p, v_p, conv_p, h_p.reshape(1, batch, D_RNN),
            kv5(k_s, dec_batch), kv5(v_s, dec_batch), conv_s, h_s.reshape(1, dec_batch, D_RNN))
```

```python
import functools
import math

import jax
import jax.numpy as jnp
from jax import lax
from jax.experimental import pallas as pl
from jax.experimental.pallas import tpu as pltpu
from jax.experimental.pallas import tpu_sc as plsc

F32 = jnp.float32
BF16 = jnp.bfloat16
WORD = jnp.int32

D_MODEL = 1024
N_HEADS = 8
N_KV_HEADS = 2
HEAD_DIM = 64
GROUP = N_HEADS // N_KV_HEADS
WINDOW = 128
D_ATTN = N_HEADS * HEAD_DIM
D_KV = N_KV_HEADS * HEAD_DIM
N_BUCKETS = 32
MAX_DISTANCE = 128
D_RNN = D_MODEL
RNN_BLOCK = 256
N_RNN_BLOCKS = D_RNN // RNN_BLOCK
CONV_W = 4
RG_C = 8.0
N_EXPERTS = 256
TOP_K = 8
N_GROUPS = 8
GROUP_SIZE = N_EXPERTS // N_GROUPS
TOPK_GROUPS = 4
D_EXPERT = D_MODEL // 4
ROUTED_SCALE = 2.5
LN_EPS = 1e-5
DEPTH = 1
ALPHA = (2 * DEPTH) ** 0.25
NEG_INF = -1e30
SM_SCALE = HEAD_DIM ** -0.5
assert math.frexp(SM_SCALE)[0] == 0.5

O_Q = 0
O_K = D_ATTN
O_V = O_K + D_KV
O_XR = O_V + D_KV
O_YR = O_XR + D_RNN
O_GA = O_YR + D_RNN
O_GR = O_GA + D_MODEL
D_IN = O_GR + D_MODEL

SUBLANES = 8
VMEM_LIMIT_BYTES = 56 * 1024 * 1024
EXPERT_ROWS = 256
CHUNK_PART = 128
ROW_PAD = SUBLANES


def _params(sem):
    return pltpu.CompilerParams(dimension_semantics=sem, vmem_limit_bytes=VMEM_LIMIT_BYTES)


def _pick_tile(n, candidates):
    for c in candidates:
        if n % c == 0:
            return c
    raise ValueError(f"no tile for {n}")


def _const_spec(shape):
    nd = len(shape)
    return pl.BlockSpec(shape, lambda *_: (0,) * nd)


LANES = 128
ROW_WORDS = D_MODEL // 2
ROW_TILE = ROW_WORDS // LANES
HIGH_HALF = -65536


def _pack_words(mat):
    as_bits = lambda v: pltpu.bitcast(v.astype(BF16).astype(F32), WORD)
    return (as_bits(mat[:, ROW_WORDS:]) & HIGH_HALF) | lax.shift_right_logical(as_bits(mat[:, :ROW_WORDS]), 16)


def _unpack_words(words):
    return pltpu.bitcast(words << 16, F32), pltpu.bitcast(words & HIGH_HALF, F32)


def _store_token_rows(ref, mat, n, row0=0, base=0):
    words = _pack_words(mat)
    for s in range(ROW_TILE):
        ref[pl.ds(base + row0 * ROW_TILE + s, n, stride=ROW_TILE), :] = words[:, s * LANES:(s + 1) * LANES]


def _load_token_rows(ref, n, row0=0, base=0):
    words = jnp.concatenate(
        [ref[pl.ds(base + row0 * ROW_TILE + s, n, stride=ROW_TILE), :] for s in range(ROW_TILE)], axis=1)
    return _unpack_words(words)


def _two_source_specs(t_p, t_s, tm, width):
    n_p = t_p // tm
    assert t_p % tm == 0 and t_s % tm == 0
    return (pl.BlockSpec((tm, width), lambda i, *_: (jnp.minimum(i, n_p - 1), 0)),
            pl.BlockSpec((tm, width), lambda i, *_: (jnp.maximum(i - n_p, 0), 0)))


def _inproj_kernel(xp_ref, xs_ref, w_ref, q_ref, k_ref, v_ref, xr_ref, gy_ref, sga_ref, sgr_ref, *, n_p):
    x = jnp.where(pl.program_id(0) < n_p, xp_ref[...], xs_ref[...]).astype(BF16)

    def seg(lo, hi):
        return jnp.dot(x, w_ref[:, lo:hi], preferred_element_type=F32)

    q_ref[...] = (seg(O_Q, O_K) * SM_SCALE).astype(BF16)
    k_ref[...] = seg(O_K, O_V)
    v_ref[...] = seg(O_V, O_XR)
    xr_ref[...] = seg(O_XR, O_YR)
    gy_ref[...] = jax.nn.gelu(seg(O_YR, O_GA)).astype(BF16)
    sga_ref[...] = jax.nn.sigmoid(seg(O_GA, O_GR)).astype(BF16)
    sgr_ref[...] = jax.nn.sigmoid(seg(O_GR, D_IN)).astype(BF16)


def _inproj(x_p, x_s, w_in_bf16):
    t_p, t_s = x_p.shape[0], x_s.shape[0]
    t = t_p + t_s
    tm = _pick_tile(math.gcd(t_p, t_s), (256, 128, 64, 32, 16, 8))
    row = lambda width: pl.BlockSpec((tm, width), lambda i: (i, 0))
    out_shape = (
        jax.ShapeDtypeStruct((t, D_ATTN), BF16),
        jax.ShapeDtypeStruct((t, D_KV), F32),
        jax.ShapeDtypeStruct((t, D_KV), F32),
        jax.ShapeDtypeStruct((t, D_RNN), F32),
        jax.ShapeDtypeStruct((t, D_RNN), BF16),
        jax.ShapeDtypeStruct((t, D_MODEL), BF16),
        jax.ShapeDtypeStruct((t, D_MODEL), BF16),
    )
    return pl.pallas_call(
        functools.partial(_inproj_kernel, n_p=t_p // tm),
        grid=(t // tm,),
        in_specs=[*_two_source_specs(t_p, t_s, tm, D_MODEL), _const_spec((D_MODEL, D_IN))],
        out_specs=(row(D_ATTN), row(D_KV), row(D_KV), row(D_RNN), row(D_RNN), row(D_MODEL), row(D_MODEL)),
        out_shape=out_shape,
        compiler_params=_params(("parallel",)),
        name="inproj",
    )(x_p, x_s, w_in_bf16)


def _t5_bucket(dist):
    n = jnp.maximum(dist, 0)
    max_exact = N_BUCKETS // 2
    nf = jnp.maximum(n, 1).astype(F32)
    large = max_exact + (jnp.log(nf / max_exact) / math.log(MAX_DISTANCE / max_exact) * (N_BUCKETS - max_exact)).astype(jnp.int32)
    large = jnp.minimum(large, N_BUCKETS - 1)
    return jnp.where(n < max_exact, n, large)


def _bias_table(rel_bias, dist):
    bucket = _t5_bucket(dist)
    rb = rel_bias.astype(F32)
    out = jnp.zeros((N_HEADS, *dist.shape), F32)
    for j in range(N_BUCKETS):
        out = jnp.where(bucket[None] == j, rb[j][:, None, None], out)
    return out


def _softmax_pv(s, sink, v):
    m = jnp.maximum(jnp.max(s, axis=-1, keepdims=True), sink)
    p = jnp.exp(s - m)
    denom = jnp.sum(p, axis=-1, keepdims=True) + jnp.exp(sink - m)
    return jnp.dot(p.astype(BF16), v, preferred_element_type=F32), denom


def _attn_prompt_kernel(sink_ref, q_ref, kc_ref, kp_ref, vc_ref, vp_ref, bias_ref, o_ref, *, q_blocks):
    keys = jnp.concatenate([kp_ref[...], kc_ref[...]], axis=0).astype(BF16)
    vals = jnp.concatenate([vp_ref[...], vc_ref[...]], axis=0).astype(BF16)
    first_table = jnp.minimum(pl.program_id(1), 1)
    for j in range(q_blocks):
        table = first_table if j == 0 else 1
        kk = keys[j * WINDOW:(j + 2) * WINDOW]
        vv = vals[j * WINDOW:(j + 2) * WINDOW]
        q = q_ref[j * WINDOW:(j + 1) * WINDOW, :]
        for g in range(N_KV_HEADS):
            kg = kk[:, g * HEAD_DIM:(g + 1) * HEAD_DIM]
            vg = vv[:, g * HEAD_DIM:(g + 1) * HEAD_DIM]
            for h in range(GROUP):
                hh = g * GROUP + h
                qh = q[:, hh * HEAD_DIM:(hh + 1) * HEAD_DIM]
                s = lax.dot_general(qh, kg, (((1,), (1,)), ((), ())), preferred_element_type=F32) + bias_ref[table, hh]
                o, denom = _softmax_pv(s, sink_ref[0, hh], vg)
                o_ref[j * WINDOW:(j + 1) * WINDOW, hh * HEAD_DIM:(hh + 1) * HEAD_DIM] = (o / denom).astype(BF16)


def _attn_prompt(q_all, k_all, v_all, bias, sinks, batch, seq):
    nb = seq // WINDOW
    q_blocks = _pick_tile(nb, (2, 1))
    steps = nb // q_blocks
    cur = lambda width: pl.BlockSpec((q_blocks * WINDOW, width), lambda b, n: (b * steps + n, 0))
    prev = lambda width: pl.BlockSpec((WINDOW, width), lambda b, n: (b * nb + jnp.maximum(n * q_blocks - 1, 0), 0))
    return pl.pallas_call(
        functools.partial(_attn_prompt_kernel, q_blocks=q_blocks),
        grid=(batch, steps),
        in_specs=[
            pl.BlockSpec(memory_space=pltpu.SMEM),
            cur(D_ATTN), cur(D_KV), prev(D_KV), cur(D_KV), prev(D_KV),
            _const_spec((2, N_HEADS, WINDOW, 2 * WINDOW)),
        ],
        out_specs=cur(D_ATTN),
        out_shape=jax.ShapeDtypeStruct((batch * seq, D_ATTN), BF16),
        compiler_params=_params(("parallel", "arbitrary")),
        name="attn_prompt",
    )(sinks.reshape(1, N_HEADS).astype(F32), q_all, k_all, k_all, v_all, v_all, bias)


def _attn_sample_kernel(q_ref, kn_ref, vn_ref, kc_ref, vc_ref, bc_ref, bn_ref, sink_ref,
                        o_ref, ko_ref, vo_ref, *, seqs, s_len):
    w = kc_ref.shape[1]
    rows_c = lax.broadcasted_iota(jnp.int32, (GROUP * s_len, w), 0) % s_len
    cols_c = lax.broadcasted_iota(jnp.int32, (GROUP * s_len, w), 1)
    dist_c = rows_c + w - cols_c
    valid_c = (dist_c >= 0) & (dist_c <= WINDOW)
    rows_n = lax.broadcasted_iota(jnp.int32, (GROUP * s_len, s_len), 0) % s_len
    cols_n = lax.broadcasted_iota(jnp.int32, (GROUP * s_len, s_len), 1)
    dist_n = rows_n - cols_n
    valid_n = (dist_n >= 0) & (dist_n <= WINDOW)
    for j in range(seqs):
        r0 = j * s_len
        qj = q_ref[r0:r0 + s_len, :]
        kc = kc_ref[j]
        vc = vc_ref[j]
        kn = kn_ref[r0:r0 + s_len, :]
        vn = vn_ref[r0:r0 + s_len, :]
        ko_ref[j, 0:w - s_len, :] = kc[s_len:, :]
        ko_ref[j, w - s_len:w, :] = kn
        vo_ref[j, 0:w - s_len, :] = vc[s_len:, :]
        vo_ref[j, w - s_len:w, :] = vn
        kcb, vcb, knb, vnb = kc.astype(BF16), vc.astype(BF16), kn.astype(BF16), vn.astype(BF16)
        for g in range(N_KV_HEADS):
            lo, hi = g * HEAD_DIM, (g + 1) * HEAD_DIM
            qs = jnp.concatenate(
                [qj[:, (g * GROUP + h) * HEAD_DIM:(g * GROUP + h + 1) * HEAD_DIM] for h in range(GROUP)], axis=0)
            nt = (((1,), (1,)), ((), ()))
            s_c = lax.dot_general(qs, kcb[:, lo:hi], nt, preferred_element_type=F32)
            s_n = lax.dot_general(qs, knb[:, lo:hi], nt, preferred_element_type=F32)
            s_c = jnp.where(valid_c, s_c + bc_ref[g], NEG_INF)
            s_n = jnp.where(valid_n, s_n + bn_ref[g], NEG_INF)
            sink = sink_ref[g]
            m = jnp.maximum(jnp.maximum(jnp.max(s_c, axis=-1, keepdims=True), jnp.max(s_n, axis=-1, keepdims=True)), sink)
            p_c = jnp.exp(s_c - m)
            p_n = jnp.exp(s_n - m)
            denom = jnp.sum(p_c, axis=-1, keepdims=True) + jnp.sum(p_n, axis=-1, keepdims=True) + jnp.exp(sink - m)
            o = jnp.dot(p_c.astype(BF16), vcb[:, lo:hi], preferred_element_type=F32)
            o = o + jnp.dot(p_n.astype(BF16), vnb[:, lo:hi], preferred_element_type=F32)
            o = (o / denom).astype(BF16)
            for h in range(GROUP):
                hh = g * GROUP + h
                o_ref[r0:r0 + s_len, hh * HEAD_DIM:(hh + 1) * HEAD_DIM] = o[h * s_len:(h + 1) * s_len, :]


def _attn_sample(q_all, k_all, v_all, cache_k, cache_v, rel_bias, sinks, row0, dec_batch, s_len):
    w = cache_k.shape[1]
    seqs = _pick_tile(dec_batch, (16, 8, 4, 2, 1))
    rows = seqs * s_len
    blk0 = row0 // rows
    assert row0 % rows == 0
    qi = jnp.arange(s_len)
    dist_c = qi[:, None] + w - jnp.arange(w)[None, :]
    dist_n = qi[:, None] - jnp.arange(s_len)[None, :]
    b_c = _bias_table(rel_bias, dist_c).reshape(N_KV_HEADS, GROUP * s_len, w)
    b_n = _bias_table(rel_bias, dist_n).reshape(N_KV_HEADS, GROUP * s_len, s_len)
    sink = jnp.broadcast_to(sinks.astype(F32).reshape(N_KV_HEADS, GROUP, 1, 1), (N_KV_HEADS, GROUP, s_len, 1))
    sink = sink.reshape(N_KV_HEADS, GROUP * s_len, 1)
    tok = lambda width: pl.BlockSpec((rows, width), lambda i: (blk0 + i, 0))
    cache = pl.BlockSpec((seqs, w, D_KV), lambda i: (i, 0, 0))
    return pl.pallas_call(
        functools.partial(_attn_sample_kernel, seqs=seqs, s_len=s_len),
        grid=(dec_batch // seqs,),
        in_specs=[
            tok(D_ATTN), tok(D_KV), tok(D_KV), cache, cache,
            _const_spec(b_c.shape), _const_spec(b_n.shape), _const_spec(sink.shape),
        ],
        out_specs=(pl.BlockSpec((rows, D_ATTN), lambda i: (i, 0)), cache, cache),
        out_shape=(
            jax.ShapeDtypeStruct((dec_batch * s_len, D_ATTN), BF16),
            jax.ShapeDtypeStruct((dec_batch, w, D_KV), F32),
            jax.ShapeDtypeStruct((dec_batch, w, D_KV), F32),
        ),
        compiler_params=_params(("parallel",)),
        name="attn_sample",
    )(q_all, k_all, v_all, cache_k, cache_v, b_c, b_n, sink)


def _softplus(z):
    return jnp.maximum(z, 0.0) + jnp.log1p(jnp.exp(-jnp.abs(z)))


def _block_gate(xcb, w_ref, b_ref):
    parts = [jnp.dot(xcb[:, n * RNN_BLOCK:(n + 1) * RNN_BLOCK], w_ref[n], preferred_element_type=F32)
             for n in range(N_RNN_BLOCKS)]
    return jax.nn.sigmoid(jnp.concatenate(parts, axis=-1) + b_ref[...])


def _lru_coeffs(xc, wa_ref, ba_ref, wx_ref, bx_ref, lam_ref, first_row_unnormalised):
    xcb = xc.astype(BF16)
    r = _block_gate(xcb, wa_ref, ba_ref)
    i = _block_gate(xcb, wx_ref, bx_ref)
    log_a = -RG_C * r * _softplus(-lam_ref[...])
    a = jnp.exp(log_a)
    mult = jnp.sqrt(-jnp.tanh(log_a) * (a * a + 1.0))
    if first_row_unnormalised is not None:
        mult = jnp.where(first_row_unnormalised, 1.0, mult)
    return a, mult * i * xc


def _scan8(a, b):
    shape = a.shape
    grouped = (shape[0] // SUBLANES, SUBLANES, shape[1])
    a = a.reshape(grouped)
    b = b.reshape(grouped)
    r8 = lax.broadcasted_iota(jnp.int32, grouped, 1)
    d = 1
    while d < SUBLANES:
        keep = r8 >= d
        a_sh = jnp.where(keep, pltpu.roll(a, d, 1), 1.0)
        b_sh = jnp.where(keep, pltpu.roll(b, d, 1), 0.0)
        b = a * b_sh + b
        a = a * a_sh
        d *= 2
    return a.reshape(shape), b.reshape(shape)


def _rnn_prompt_kernel(xr_ref, gy_ref, cw_ref, cb_ref, wa_ref, ba_ref, wx_ref, bx_ref, lam_ref,
                       o_ref, nh_ref, ext_ref, a_ref, b_ref, hc_ref, *, tl):
    l = pl.program_id(1)

    @pl.when(l == 0)
    def _():
        ext_ref[0:SUBLANES, :] = jnp.zeros((SUBLANES, D_RNN), F32)
        hc_ref[...] = jnp.zeros((1, D_RNN), F32)

    x = xr_ref[...]
    ext_ref[SUBLANES:, :] = x
    xc = cb_ref[...] + cw_ref[CONV_W - 1:CONV_W, :] * x
    for j in range(1, CONV_W):
        xc = xc + cw_ref[CONV_W - 1 - j:CONV_W - j, :] * ext_ref[SUBLANES - j:SUBLANES - j + tl, :]
    ext_ref[0:SUBLANES, :] = x[tl - SUBLANES:, :]

    row = lax.broadcasted_iota(jnp.int32, (tl, D_RNN), 0)
    a, b = _lru_coeffs(xc, wa_ref, ba_ref, wx_ref, bx_ref, lam_ref, (row == 0) & (l == 0))
    a, b = _scan8(a, b)
    a_ref[...] = a
    b_ref[...] = b

    def chunk(c, h):
        sl = pl.ds(pl.multiple_of(c * SUBLANES, SUBLANES), SUBLANES)
        hc = b_ref[sl, :] + a_ref[sl, :] * h
        b_ref[sl, :] = hc
        return hc[SUBLANES - 1:SUBLANES, :]

    h = lax.fori_loop(0, tl // SUBLANES, chunk, hc_ref[...])
    hc_ref[...] = h
    nh_ref[0] = h
    o_ref[...] = (b_ref[...] * gy_ref[...]).astype(BF16)


def _rnn_prompt(xr_all, gy_all, rnn_w, batch, seq):
    tl =_pick_tile(seq, (256, 128, 64, 32, 16, 8))
    nl = seq // tl
    tok = pl.BlockSpec((tl, D_RNN), lambda b, l: (b * nl + l, 0))
    return pl.pallas_call(
        functools.partial(_rnn_prompt_kernel, tl=tl),
        grid=(batch, nl),
        in_specs=[tok, tok] + [_const_spec(w.shape) for w in rnn_w],
        out_specs=(tok, pl.BlockSpec((1, 1, D_RNN), lambda b, l: (b, 0, 0))),
        out_shape=(jax.ShapeDtypeStruct((batch * seq, D_RNN), BF16), jax.ShapeDtypeStruct((batch, 1, D_RNN), F32)),
        scratch_shapes=[
            pltpu.VMEM((tl + SUBLANES, D_RNN), F32),
            pltpu.VMEM((tl, D_RNN), F32),
            pltpu.VMEM((tl, D_RNN), F32),
            pltpu.VMEM((1, D_RNN), F32),
        ],
        compiler_params=_params(("parallel", "arbitrary")),
        name="rnn_prompt",
    )(xr_all, gy_all, *rnn_w)


def _rnn_sample_kernel(xr_ref, gy_ref, hp_ref, h0_ref, cw_ref, cb_ref, wa_ref, ba_ref, wx_ref, bx_ref, lam_ref,
                       o_ref, nh_ref, *, seqs):
    rows = seqs * SUBLANES
    x = xr_ref[...]
    hp = hp_ref[...]
    r8 = lax.broadcasted_iota(jnp.int32, (rows, D_RNN), 0) % SUBLANES
    xc = cb_ref[...] + cw_ref[CONV_W - 1:CONV_W, :] * x
    for j in range(1, CONV_W):
        shifted = jnp.where(r8 >= j, pltpu.roll(x, j, 0), pltpu.roll(hp, rows - (SUBLANES - j), 0))
        xc = xc + cw_ref[CONV_W - 1 - j:CONV_W - j, :] * shifted
    a, b = _lru_coeffs(xc, wa_ref, ba_ref, wx_ref, bx_ref, lam_ref, None)
    a, b = _scan8(a, b)
    h0 = jnp.broadcast_to(h0_ref[...][:, None, :], (seqs, SUBLANES, D_RNN)).reshape(rows, D_RNN)
    h = b + a * h0
    last = jnp.where(r8 == SUBLANES - 1, h, 0.0).reshape(seqs, SUBLANES, D_RNN)
    nh_ref[...] = jnp.sum(last, axis=1)
    o_ref[...] = (h * gy_ref[...]).astype(BF16)


def _rnn_sample(xr_all, gy_all, hist_pad, h0, rnn_w, row0, dec_batch):
    seqs = _pick_tile(dec_batch, (16, 8))
    rows = seqs * SUBLANES
    assert row0 % rows == 0
    blk0 = row0 // rows
    tok = pl.BlockSpec((rows, D_RNN), lambda i: (blk0 + i, 0))
    return pl.pallas_call(
        functools.partial(_rnn_sample_kernel, seqs=seqs),
        grid=(dec_batch // seqs,),
        in_specs=[tok, tok, pl.BlockSpec((rows, D_RNN), lambda i: (i, 0)), pl.BlockSpec((seqs, D_RNN), lambda i: (i, 0))]
        + [_const_spec(w.shape) for w in rnn_w],
        out_specs=(pl.BlockSpec((rows, D_RNN), lambda i: (i, 0)), pl.BlockSpec((seqs, D_RNN), lambda i: (i, 0))),
        out_shape=(jax.ShapeDtypeStruct((dec_batch * SUBLANES, D_RNN), BF16),
                   jax.ShapeDtypeStruct((dec_batch, D_RNN), F32)),
        compiler_params=_params(("parallel",)),
        name="rnn_sample",
    )(xr_all, gy_all, hist_pad, h0, *rnn_w)


def _layer_norm(z, g, b):
    mu = jnp.mean(z, axis=-1, keepdims=True)
    zc = z - mu
    var = jnp.mean(zc * zc, axis=-1, keepdims=True)
    return zc * lax.rsqrt(var + LN_EPS) * g + b


def _first_index_of_max(vals, iota, axis, sentinel):
    mx = jnp.max(vals, axis=axis, keepdims=True)
    return mx, jnp.min(jnp.where(vals == mx, iota, sentinel), axis=axis, keepdims=True)


def _route(scores, bias):
    t = scores.shape[1]
    grp = scores + bias
    g3 = grp.reshape(N_GROUPS, GROUP_SIZE, t)
    e_in_g = lax.broadcasted_iota(jnp.int32, g3.shape, 1)
    m1, first = _first_index_of_max(g3, e_in_g, 1, GROUP_SIZE)
    m2 = jnp.max(jnp.where(e_in_g == first, -jnp.inf, g3), axis=1, keepdims=True)
    gscore = (m1 + m2).reshape(N_GROUPS, t)
    g_iota = lax.broadcasted_iota(jnp.int32, gscore.shape, 0)
    gmask = jnp.zeros(gscore.shape, jnp.bool_)
    for _ in range(TOPK_GROUPS):
        _, gi = _first_index_of_max(gscore, g_iota, 0, N_GROUPS)
        hit = g_iota == gi
        gmask = gmask | hit
        gscore = jnp.where(hit, -jnp.inf, gscore)
    masked = jnp.where(gmask[:, None, :], g3, -jnp.inf).reshape(N_EXPERTS, t)
    e_iota = lax.broadcasted_iota(jnp.int32, masked.shape, 0)
    idx, wts, hits = [], [], []
    for _ in range(TOP_K):
        _, ei = _first_index_of_max(masked, e_iota, 0, N_EXPERTS)
        hit = e_iota == ei
        idx.append(ei)
        hits.append(hit)
        wts.append(jnp.sum(jnp.where(hit, scores, 0.0), axis=0, keepdims=True))
        masked = jnp.where(hit, -jnp.inf, masked)
    idx = jnp.concatenate(idx, axis=0)
    w = jnp.concatenate(wts, axis=0)
    w = w / jnp.sum(w, axis=0, keepdims=True) * ROUTED_SCALE
    return idx, w, hits


def _merge_kernel(xp_ref, xs_ref, aop_ref, aos_ref, rop_ref, ros_ref, sga_ref, sgr_ref, woa_ref, wor_ref, wout_ref, g1_ref, b1_ref,
                  wrt_ref, rb_ref, wsg_ref, wsu_ref, wsd_ref,
                  x1w_ref, base_ref, idx_ref, wt_ref, rank_ref, cnt_ref, carry_ref, *, n_p, tm):
    i = pl.program_id(0)

    @pl.when(i == 0)
    def _():
        carry_ref[...] = jnp.zeros(carry_ref.shape, F32)

    is_prompt = i < n_p
    x = jnp.where(is_prompt, xp_ref[...], xs_ref[...])
    pa = jnp.dot(jnp.where(is_prompt, aop_ref[...], aos_ref[...]), woa_ref[...], preferred_element_type=F32)
    pr = jnp.dot(jnp.where(is_prompt, rop_ref[...], ros_ref[...]), wor_ref[...], preferred_element_type=F32)
    merged = sga_ref[...] * pa + sgr_ref[...] * pr
    z = ALPHA * x + jnp.dot(merged.astype(BF16), wout_ref[...], preferred_element_type=F32)
    x1 = _layer_norm(z, g1_ref[...], b1_ref[...])
    words = _pack_words(x1)
    for h in range(tm // SC_ROWS):
        for s in range(ROW_TILE):
            x1w_ref[(h * ROW_TILE + s) * SC_ROWS:(h * ROW_TILE + s + 1) * SC_ROWS, :] = (
                words[h * SC_ROWS:(h + 1) * SC_ROWS, s * LANES:(s + 1) * LANES])
    x1b = x1.astype(BF16)
    u = jax.nn.silu(jnp.dot(x1b, wsg_ref[...], preferred_element_type=F32)) * jnp.dot(x1b, wsu_ref[...], preferred_element_type=F32)
    shared = jnp.dot(u.astype(BF16), wsd_ref[...], preferred_element_type=F32)
    base_ref[...] = ALPHA * x1 + shared
    logits = lax.dot_general(wrt_ref[...], x1, (((1,), (1,)), ((), ())), preferred_element_type=F32)
    idx, w, hits = _route(jax.nn.sigmoid(logits), rb_ref[...])
    idx_ref[...] = idx
    wt_ref[...] = w

    chosen = functools.reduce(jnp.logical_or, hits)
    chosen_f = jnp.where(chosen, 1.0, 0.0)
    earlier = (lax.broadcasted_iota(jnp.int32, (tm, tm), 0) < lax.broadcasted_iota(jnp.int32, (tm, tm), 1))
    prefix = jnp.dot(chosen_f.astype(BF16), jnp.where(earlier, 1.0, 0.0).astype(BF16), preferred_element_type=F32)
    before = prefix + carry_ref[...]
    ranks = [jnp.sum(jnp.where(hit, before, 0.0), axis=0, keepdims=True) for hit in hits]
    rank_ref[...] = jnp.concatenate(ranks, axis=0).astype(jnp.int32)
    carry_ref[...] = carry_ref[...] + jnp.sum(chosen_f, axis=1, keepdims=True)
    cnt_ref[...] = carry_ref[...]


def _merge(x_p, x_s, attn_p, attn_s, rnn_p, rnn_s, sga, sgr, weights):
    t_p, t_s = x_p.shape[0], x_s.shape[0]
    t = t_p + t_s
    tm = _pick_tile(math.gcd(t_p, t_s), (512, 256, 128))
    row = lambda width: pl.BlockSpec((tm, width), lambda i: (i, 0))
    col = pl.BlockSpec((TOP_K, tm), lambda i: (0, i))

    return pl.pallas_call(
        functools.partial(_merge_kernel, n_p=t_p // tm, tm=tm),
        grid=(t // tm,),
        in_specs=[*_two_source_specs(t_p, t_s, tm, D_MODEL), *_two_source_specs(t_p, t_s, tm, D_ATTN),
                  *_two_source_specs(t_p, t_s, tm, D_RNN), row(D_MODEL), row(D_MODEL)]
        + [_const_spec(w.shape) for w in weights],
        out_specs=(pl.BlockSpec((tm * ROW_TILE, LANES), lambda i: (i, 0)), row(D_MODEL), col, col, col,
                   _const_spec((N_EXPERTS, 1))),
        out_shape=(
            jax.ShapeDtypeStruct((t * ROW_TILE, LANES), WORD),
            jax.ShapeDtypeStruct((t, D_MODEL), F32),
            jax.ShapeDtypeStruct((TOP_K, t), jnp.int32),
            jax.ShapeDtypeStruct((TOP_K, t), F32),
            jax.ShapeDtypeStruct((TOP_K, t), jnp.int32),
            jax.ShapeDtypeStruct((N_EXPERTS, 1), F32),
        ),
        scratch_shapes=[pltpu.VMEM((N_EXPERTS, 1), F32)],
        compiler_params=_params(("arbitrary",)),
        name="merge_ln1_route",
    )(x_p, x_s, attn_p, attn_s, rnn_p, rnn_s, sga, sgr, *weights)


def _expert_chunk_rows(n_assign):
    return max(EXPERT_ROWS, -(-(n_assign // N_EXPERTS * 9 // 8) // EXPERT_ROWS) * EXPERT_ROWS)


def _expert_layout(counts, n_assign, chunk_rows):
    counts = counts.reshape(N_EXPERTS).astype(jnp.int32)
    padded = (counts + ROW_PAD - 1) // ROW_PAD * ROW_PAD
    pend = jnp.cumsum(padded)
    pstart = pend - padded
    rows_alloc = (n_assign + N_EXPERTS * (ROW_PAD - 1)) // ROW_PAD * ROW_PAD + chunk_rows
    n_chunks = jnp.maximum((counts + chunk_rows - 1) // chunk_rows, 1)
    part = CHUNK_PART
    written = jnp.maximum((counts + part - 1) // part, 1) * part
    ffn_end = jnp.max(pstart + written).reshape(1)
    return pstart, pstart + counts, pend, n_chunks, counts, ffn_end, rows_alloc


def _dest_kernel(idx_ref, rank_ref, pstart_ref, dest_ref, word_rows_ref):
    e_iota = lax.broadcasted_iota(jnp.int32, (N_EXPERTS, idx_ref.shape[1]), 0)
    starts = [jnp.sum(jnp.where(e_iota == idx_ref[k:k + 1, :], pstart_ref[...], 0), axis=0, keepdims=True)
              for k in range(TOP_K)]
    dest = jnp.concatenate(starts, axis=0) + rank_ref[...]
    dest_ref[...] = dest
    tm = dest.shape[1]
    word_rows_ref[0] = jnp.concatenate(
        [dest[k:k + 1, h * SC_ROWS:(h + 1) * SC_ROWS] * ROW_TILE + s
         for h in range(tm // SC_ROWS) for s in range(ROW_TILE) for k in range(TOP_K)], axis=0)


WORD_ROW_VECS = ROW_TILE * TOP_K


def _dest_rows(idx_t, rank_t, pstart):
    t = idx_t.shape[1]
    tm = _pick_tile(t, (1024, 512, 256, 128))
    col = pl.BlockSpec((TOP_K, tm), lambda i: (0, i))
    vecs = tm // SC_ROWS * WORD_ROW_VECS
    dest, word_rows = pl.pallas_call(
        _dest_kernel,
        grid=(t // tm,),
        in_specs=[col, col, _const_spec((N_EXPERTS, 1))],
        out_specs=(col, pl.BlockSpec((1, vecs, SC_ROWS), lambda i: (i, 0, 0))),
        out_shape=(jax.ShapeDtypeStruct((TOP_K, t), jnp.int32),
                   jax.ShapeDtypeStruct((t // tm, vecs, SC_ROWS), jnp.int32)),
        compiler_params=_params(("parallel",)),
        name="dest_rows",
    )(idx_t, rank_t, pstart.reshape(N_EXPERTS, 1))
    return dest, word_rows.reshape(-1, SC_ROWS)


def _token_rows(r, n=1):
    return pl.ds(pl.multiple_of(r * ROW_TILE, ROW_TILE), n * ROW_TILE)


def _zero_row_groups(zero_ref, dst_ref, sem, first_group, n_groups):
    def start(g, c):
        pltpu.make_async_copy(zero_ref, dst_ref.at[_token_rows(g * ROW_PAD, ROW_PAD)], sem).start()
        return c

    lax.fori_loop(first_group, n_groups, start, 0)
    return n_groups - first_group


def _wait_zero_copies(zero_ref, dst_ref, sem, n):
    def wait(_, c):
        pltpu.make_async_copy(zero_ref, dst_ref.at[_token_rows(0, ROW_PAD)], sem).wait()
        return c

    lax.fori_loop(0, n, wait, 0)


SC_ROWS = 128


def _sc_workers():
    info = plsc.get_sparse_core_info()
    mesh = plsc.VectorSubcoreMesh(core_axis_name="c", subcore_axis_name="s")
    worker = lambda: lax.axis_index("s") * info.num_cores + lax.axis_index("c")
    return mesh, info.num_cores * info.num_subcores, worker


def _sc_scatter_rows(src, rows, n_out):
    mesh, n_workers, worker = _sc_workers()
    n_units = src.shape[0] // SC_ROWS
    assert n_units % n_workers == 0 and rows.shape == (n_units * TOP_K, SC_ROWS)
    per_worker = n_units // n_workers

    unit_bufs = [pltpu.VMEM((TOP_K, SC_ROWS), jnp.int32), pltpu.VMEM((SC_ROWS, LANES), src.dtype),
                 pltpu.SemaphoreType.DMA]

    @functools.partial(
        pl.kernel, mesh=mesh,
        out_type=jax.ShapeDtypeStruct((n_out, LANES), src.dtype),
        scratch_types=unit_bufs + unit_bufs + [pltpu.SemaphoreType.DMA],
    )
    def scatter(src_hbm, rows_hbm, out_hbm, rows_a, data_a, load_a, rows_b, data_b, load_b, sem):
        first = worker() * per_worker
        bufs = ((rows_a, data_a, load_a), (rows_b, data_b, load_b))

        def loads(u, buf):
            rows_v, data_v, load_sem = buf
            return (pltpu.make_async_copy(rows_hbm.at[pl.ds(u * TOP_K, TOP_K)], rows_v, load_sem),
                    pltpu.make_async_copy(src_hbm.at[pl.ds(u * SC_ROWS, SC_ROWS)], data_v, load_sem))

        def unit(u, buf, other, has_next):
            for cp in loads(u, buf):
                cp.wait()

            @pl.when(has_next)
            def _():
                for cp in loads(u + 1, other):
                    cp.start()

            rows_v, data_v, _ = buf
            copies = [pltpu.async_copy(data_v, out_hbm.at[rows_v.at[k]], sem) for k in range(TOP_K)]
            for cp in copies:
                cp.wait()

        for cp in loads(first, bufs[0]):
            cp.start()

        @pl.loop(0, per_worker // 2)
        def _(pair):
            u = first + 2 * pair
            unit(u, bufs[0], bufs[1], True)
            unit(u + 1, bufs[1], bufs[0], 2 * pair + 2 < per_worker)

        if per_worker % 2:
            unit(first + per_worker - 1, bufs[0], bufs[1], False)

    return scatter(src, rows)


def _zero_padding_kernel(lo_ref, hi_ref, tail_ref, xs_in_ref, xs_ref, zero_ref, sem, *, rows_alloc):
    del xs_in_ref
    zero_ref[...] = jnp.zeros(zero_ref.shape, WORD)
    row_copy = lambda r: pltpu.make_async_copy(zero_ref.at[pl.ds(0, ROW_TILE)], xs_ref.at[_token_rows(r)], sem)

    def expert(e, n):
        def row(r, c):
            row_copy(r).start()
            return c

        lax.fori_loop(lo_ref[e], hi_ref[e], row, 0)
        return n + hi_ref[e] - lo_ref[e]

    def wait(_, c):
        row_copy(0).wait()
        return c

    lax.fori_loop(0, lax.fori_loop(0, N_EXPERTS, expert, 0), wait, 0)
    n = _zero_row_groups(zero_ref, xs_ref, sem, tail_ref[0] // ROW_PAD, rows_alloc // ROW_PAD)
    _wait_zero_copies(zero_ref, xs_ref, sem, n)


def _zero_padding(xs, pad_lo, pad_hi, total, rows_alloc):
    grid_spec = pltpu.PrefetchScalarGridSpec(
        num_scalar_prefetch=3,
        grid=(1,),
        in_specs=[pl.BlockSpec(memory_space=pl.ANY)],
        out_specs=pl.BlockSpec(memory_space=pl.ANY),
        scratch_shapes=[pltpu.VMEM((ROW_PAD * ROW_TILE, LANES), WORD), pltpu.SemaphoreType.DMA],
    )
    return pl.pallas_call(
        functools.partial(_zero_padding_kernel, rows_alloc=rows_alloc),
        grid_spec=grid_spec,
        out_shape=jax.ShapeDtypeStruct(xs.shape, xs.dtype),
        input_output_aliases={3: 0},
        compiler_params=_params(("arbitrary",)),
        name="zero_padding",
    )(pad_lo, pad_hi, total, xs)


def _expert_kernel(pstart_ref, nch_ref, rows_ref, end_ref, wg_ref, wu_ref, wd_ref, xs_ref, o_ref,
                   xbuf_ref, obuf_ref, wgb_ref, wub_ref, wdb_ref, zero_ref, done_ref, in_sem, out_sem,
                   *, rows_alloc, chunk_rows):
    e = pl.program_id(0)
    n_e = pl.num_programs(0)
    start = pstart_ref[e]
    nch = nch_ref[e]
    buf_rows = chunk_rows * ROW_TILE
    part = CHUNK_PART
    chunk_parts = chunk_rows // part
    part_rows = part * ROW_TILE

    class _ChunkCopy:
        def __init__(self, whole, parts, n_parts):
            self.whole = whole
            self.parts = parts
            self.n_parts = n_parts

        def _each(self, act):
            pl.when(self.n_parts == chunk_parts)(functools.partial(act, self.whole))
            for p, cp in enumerate(self.parts):
                pl.when((p < self.n_parts) & (self.n_parts < chunk_parts))(functools.partial(act, cp))

        def start(self):
            self._each(lambda cp: cp.start())

        def wait(self):
            self._each(lambda cp: cp.wait())

    def parts_of(rows_left):
        return jnp.clip((rows_left + part - 1) // part, 1, chunk_parts)

    def in_copy(row, slot, n_parts):
        copy = lambda p, n: pltpu.make_async_copy(
            xs_ref.at[_token_rows(row + p * part, n * part)],
            xbuf_ref.at[pl.ds(slot * buf_rows + p * part_rows, n * part_rows)], in_sem.at[slot])
        return _ChunkCopy(copy(0, chunk_parts), [copy(p, 1) for p in range(chunk_parts - 1)], n_parts)

    def out_copy(row, slot, n_parts):
        copy = lambda p, n: pltpu.make_async_copy(
            obuf_ref.at[pl.ds(slot * buf_rows + p * part_rows, n * part_rows)],
            o_ref.at[_token_rows(row + p * part, n * part)], out_sem)
        return _ChunkCopy(copy(0, chunk_parts), [copy(p, 1) for p in range(chunk_parts - 1)], n_parts)

    @pl.when(e == 0)
    def _():
        done_ref[0] = 0
        xbuf_ref[...] = jnp.zeros(xbuf_ref.shape, WORD)
        obuf_ref[...] = jnp.zeros(obuf_ref.shape, WORD)
        in_copy(start, 0, parts_of(rows_ref[0])).start()

    wgb_ref[...] = wg_ref[...].astype(BF16)
    wub_ref[...] = wu_ref[...].astype(BF16)
    wdb_ref[...] = wd_ref[...].astype(BF16)
    done = done_ref[0]

    def chunk(c, carry):
        g = done + c
        slot = g % 2
        row = start + c * chunk_rows
        rows_left = rows_ref[e] - c * chunk_rows
        n_parts = parts_of(rows_left)
        in_copy(row, slot, n_parts).wait()
        last = c + 1 == nch
        next_e = jnp.minimum(e + 1, n_e - 1)
        next_row = jnp.where(last, pstart_ref[next_e], row + chunk_rows)
        next_left = jnp.where(last, rows_ref[next_e], rows_left - chunk_rows)

        @pl.when(jnp.logical_not(last & (e == n_e - 1)))
        def _():
            in_copy(next_row, 1 - slot, parts_of(next_left)).start()

        base = pl.multiple_of(slot * buf_rows, buf_rows)
        def sub_block(h):
            xb = jnp.concatenate(_load_token_rows(xbuf_ref, EXPERT_ROWS, row0=h * EXPERT_ROWS, base=base),
                                 axis=1).astype(BF16)
            gate = jnp.dot(xb, wgb_ref[...], preferred_element_type=F32)
            up = jnp.dot(xb, wub_ref[...], preferred_element_type=F32)
            act = (jax.nn.silu(gate) * up).astype(BF16)
            _store_token_rows(obuf_ref, jnp.dot(act, wdb_ref[...], preferred_element_type=F32),
                              EXPERT_ROWS, row0=h * EXPERT_ROWS, base=base)

        n_sub = chunk_rows // EXPERT_ROWS
        needed = jnp.clip((rows_left + EXPERT_ROWS - 1) // EXPERT_ROWS, 1, n_sub)
        for count in range(1, n_sub + 1):
            @pl.when(needed == count)
            def _():
                for h in range(count):
                    sub_block(h)

        @pl.when(g > 0)
        def _():
            out_copy(0, 0, done_ref[1]).wait()

        out_copy(row, slot, n_parts).start()
        done_ref[1] = n_parts
        return carry

    lax.fori_loop(0, nch, chunk, 0)
    done_ref[0] = done + nch

    @pl.when(e == n_e - 1)
    def _():
        out_copy(0, 0, done_ref[1]).wait()
        zero_ref[...] = jnp.zeros(zero_ref.shape, WORD)
        n = _zero_row_groups(zero_ref, o_ref, out_sem, end_ref[0] // ROW_PAD, rows_alloc // ROW_PAD)
        _wait_zero_copies(zero_ref, o_ref, out_sem, n)


def _expert_ffn(xs, pstart, n_chunks, rows, ffn_end, rows_alloc, chunk_rows, w_e_gate, w_e_up, w_e_down):
    weight = lambda shape: pl.BlockSpec((None, *shape), lambda e, *_: (e, 0, 0))
    grid_spec = pltpu.PrefetchScalarGridSpec(
        num_scalar_prefetch=4,
        grid=(N_EXPERTS,),
        in_specs=[weight((D_MODEL, D_EXPERT)), weight((D_MODEL, D_EXPERT)), weight((D_EXPERT, D_MODEL)),
                  pl.BlockSpec(memory_space=pl.ANY)],
        out_specs=pl.BlockSpec(memory_space=pl.ANY),
        scratch_shapes=[
            pltpu.VMEM((2 * chunk_rows * ROW_TILE, LANES), WORD),
            pltpu.VMEM((2 * chunk_rows * ROW_TILE, LANES), WORD),
            pltpu.VMEM((D_MODEL, D_EXPERT), BF16), pltpu.VMEM((D_MODEL, D_EXPERT), BF16),
            pltpu.VMEM((D_EXPERT, D_MODEL), BF16),
            pltpu.VMEM((ROW_PAD * ROW_TILE, LANES), WORD),
            pltpu.SMEM((2,), jnp.int32),
            pltpu.SemaphoreType.DMA((2,)), pltpu.SemaphoreType.DMA,
        ],
    )
    return pl.pallas_call(
        functools.partial(_expert_kernel, rows_alloc=rows_alloc, chunk_rows=chunk_rows),
        grid_spec=grid_spec,
        out_shape=jax.ShapeDtypeStruct((rows_alloc * ROW_TILE, LANES), WORD),
        compiler_params=_params(("arbitrary",)),
        name="expert_ffn",
    )(pstart, n_chunks, rows, ffn_end, w_e_gate, w_e_up, w_e_down, xs)


def _combine_head_kernel(dest_ref, dest_next_ref, w_ref, base_ref, g_ref, b_ref, outs_ref, y_ref, buf_ref, sem, *, tm):
    i = pl.program_id(0)
    slot_rows = TOP_K * tm

    def gather(d_ref, slot):
        def issue(t, c):
            for k in range(TOP_K):
                pltpu.make_async_copy(outs_ref.at[_token_rows(d_ref[k, t])],
                                      buf_ref.at[_token_rows(slot * slot_rows + k * tm + t)], sem.at[slot]).start()
            return c

        lax.fori_loop(0, tm, issue, 0)

    @pl.when(i == 0)
    def _():
        gather(dest_ref, 0)

    @pl.when(i + 1 < pl.num_programs(0))
    def _():
        gather(dest_next_ref, (i + 1) % 2)

    slot = i % 2
    for k in range(TOP_K):
        pltpu.make_async_copy(outs_ref.at[_token_rows(0, tm)], buf_ref.at[_token_rows(slot * slot_rows + k * tm, tm)],
                              sem.at[slot]).wait()

    w = w_ref[...]
    base = pl.multiple_of(slot * slot_rows * ROW_TILE, ROW_TILE)
    halves = [None, None]
    for k in range(TOP_K):
        wk = w[:, k:k + 1]
        for j, rows in enumerate(_load_token_rows(buf_ref, tm, row0=k * tm, base=base)):
            halves[j] = wk * rows if halves[j] is None else halves[j] + wk * rows
    y_ref[...] = _layer_norm(base_ref[...] + jnp.concatenate(halves, axis=1), g_ref[...], b_ref[...])


def _combine_head(out_sorted, dest, w_tok, base, g2, b2, tm, n_head):
    row = pl.BlockSpec((tm, D_MODEL), lambda i: (i, 0))
    return pl.pallas_call(
        functools.partial(_combine_head_kernel, tm=tm),
        grid=(n_head,),
        in_specs=[
            pl.BlockSpec((TOP_K, tm), lambda i: (0, i), memory_space=pltpu.SMEM),
            pl.BlockSpec((TOP_K, tm), lambda i: (0, jnp.minimum(i + 1, n_head - 1)), memory_space=pltpu.SMEM),
            pl.BlockSpec((tm, TOP_K), lambda i: (i, 0)),
            row, _const_spec((1, D_MODEL)), _const_spec((1, D_MODEL)),
            pl.BlockSpec(memory_space=pl.ANY),
        ],
        out_specs=row,
        out_shape=jax.ShapeDtypeStruct((n_head * tm, D_MODEL), F32),
        scratch_shapes=[pltpu.VMEM((2 * TOP_K * tm * ROW_TILE, LANES), WORD), pltpu.SemaphoreType.DMA((2,))],
        compiler_params=_params(("arbitrary",)),
        name="combine_head",
    )(dest, dest, w_tok, base, g2, b2, out_sorted)


def _sc_gather_rows(table, rows):
    mesh, n_workers, worker = _sc_workers()
    m = rows.shape[0]
    per_worker = m // n_workers
    assert m % n_workers == 0 and per_worker % SC_ROWS == 0
    in_flight = _pick_tile(per_worker // SC_ROWS, (4, 2, 1))
    step = SC_ROWS * in_flight

    @functools.partial(
        pl.kernel, mesh=mesh,
        out_type=jax.ShapeDtypeStruct((m, LANES), table.dtype),
        scratch_types=[pltpu.VMEM((step,), jnp.int32), pltpu.VMEM((step, LANES), table.dtype), pltpu.SemaphoreType.DMA],
    )
    def gather(table_hbm, rows_hbm, out_hbm, rows_v, data_v, sem):
        first = worker() * per_worker

        @pl.loop(0, per_worker // step)
        def _(it):
            off = first + it * step
            pltpu.sync_copy(rows_hbm.at[pl.ds(off, step)], rows_v)
            copies = [pltpu.async_copy(table_hbm.at[rows_v.at[pl.ds(j * SC_ROWS, SC_ROWS)]],
                                       data_v.at[pl.ds(j * SC_ROWS, SC_ROWS)], sem)
                      for j in range(in_flight)]
            for cp in copies:
                cp.wait()
            pltpu.sync_copy(data_v, out_hbm.at[pl.ds(off, step)])

    return gather(table, rows)


def _combine_tail_kernel(yh_ref, rows_ref, w_ref, base_ref, g_ref, b_ref, yp_ref, ys_ref, *, tm, n_head, n_p):
    i = pl.program_id(0)

    @pl.when(i < n_head)
    def _():
        yp_ref[...] = yh_ref[...]

    @pl.when(i >= n_head)
    def _():
        w = w_ref[...]
        lo = [None] * ROW_TILE
        hi = [None] * ROW_TILE
        for k in range(TOP_K):
            wk = w[:, k:k + 1]
            for s in range(ROW_TILE):
                vec = lambda h: ((h * ROW_TILE + s) * TOP_K + k) * SC_ROWS
                words = jnp.concatenate([rows_ref[vec(h):vec(h) + SC_ROWS, :] for h in range(tm // SC_ROWS)], axis=0)
                lo_s, hi_s = _unpack_words(words)
                lo[s] = wk * lo_s if lo[s] is None else lo[s] + wk * lo_s
                hi[s] = wk * hi_s if hi[s] is None else hi[s] + wk * hi_s
        y = _layer_norm(base_ref[...] + jnp.concatenate(lo + hi, axis=1), g_ref[...], b_ref[...])

        @pl.when(i < n_p)
        def _():
            yp_ref[...] = y

        @pl.when(i >= n_p)
        def _():
            ys_ref[...] = y


def _combine_tail(y_head, gathered, w_tok, base, g2, b2, t_p, t_s, tm, n_head):
    n_p = t_p // tm
    assert n_head <= n_p
    n_tiles = (t_p + t_s) // tm
    blk = TOP_K * ROW_TILE * tm
    row = pl.BlockSpec((tm, D_MODEL), lambda i: (i, 0))
    out_p, out_s = _two_source_specs(t_p, t_s, tm, D_MODEL)
    return pl.pallas_call(
        functools.partial(_combine_tail_kernel, tm=tm, n_head=n_head, n_p=n_p),
        grid=(n_tiles,),
        in_specs=[
            pl.BlockSpec((tm, D_MODEL), lambda i: (jnp.minimum(i, n_head - 1), 0)),
            pl.BlockSpec((blk, LANES), lambda i: (jnp.maximum(i - n_head, 0), 0)),
            pl.BlockSpec((tm, TOP_K), lambda i: (i, 0)),
            row, _const_spec((1, D_MODEL)), _const_spec((1, D_MODEL)),
        ],
        out_specs=(out_p, out_s),
        out_shape=(jax.ShapeDtypeStruct((t_p, D_MODEL), F32), jax.ShapeDtypeStruct((t_s, D_MODEL), F32)),
        compiler_params=_params(("arbitrary",)),
        name="combine_tail",
    )(y_head, gathered, w_tok, base, g2, b2)


def kernel(x_prompt, x_sample, cache_k, cache_v, state_conv, state_rnn, w_in, conv_w, conv_b, w_gate_a, b_gate_a, w_gate_x, b_gate_x, lru_lambda, rel_bias, sinks, w_o_attn, w_o_rnn, w_out, ln1_g, ln1_b, w_router, router_bias, w_e_gate, w_e_up, w_e_down, w_s_gate, w_s_up, w_s_down, ln2_g, ln2_b):
    assert w_in.shape[0] == DEPTH == 1
    batch, seq, _ = x_prompt.shape
    dec_batch, s_len, _ = x_sample.shape
    w_cache = cache_k.shape[2]
    assert s_len == SUBLANES and seq % WINDOW == 0 and w_cache == WINDOW
    t_p = batch * seq
    t_s = dec_batch * s_len
    vec = lambda a: a[0].reshape(1, -1).astype(F32)

    x_p = x_prompt.reshape(t_p, D_MODEL)
    x_s = x_sample.reshape(t_s, D_MODEL)
    q, k, v, xr, gy, sga, sgr = _inproj(x_p, x_s, w_in[0].astype(BF16))

    qi = jnp.arange(WINDOW)
    dist = qi[:, None] + WINDOW - jnp.arange(2 * WINDOW)[None, :]
    band = (dist >= 0) & (dist <= WINDOW)
    has_prev = jnp.arange(2 * WINDOW)[None, :] >= WINDOW
    bias_p = _bias_table(rel_bias, dist)
    bias_p = jnp.stack([jnp.where(band & has_prev, bias_p, NEG_INF), jnp.where(band, bias_p, NEG_INF)])
    attn_p = _attn_prompt(q, k, v, bias_p, sinks[0], batch, seq)
    attn_s, k_s, v_s = _attn_sample(
        q, k, v, cache_k[0].reshape(dec_batch, w_cache, D_KV), cache_v[0].reshape(dec_batch, w_cache, D_KV),
        rel_bias, sinks[0], t_p, dec_batch, s_len)

    rnn_w = (conv_w[0], vec(conv_b), w_gate_a[0].astype(BF16), vec(b_gate_a), w_gate_x[0].astype(BF16),
             vec(b_gate_x), vec(lru_lambda))
    rnn_p, h_p = _rnn_prompt(xr, gy, rnn_w, batch, seq)
    hist_pad = jnp.pad(state_conv[0], ((0, 0), (SUBLANES - (CONV_W - 1), 0), (0, 0))).reshape(t_s, D_RNN)
    rnn_s, h_s = _rnn_sample(xr, gy, hist_pad, state_rnn[0], rnn_w, t_p, dec_batch)

    merge_w = (w_o_attn[0].astype(BF16), w_o_rnn[0].astype(BF16), w_out[0].astype(BF16), vec(ln1_g), vec(ln1_b),
               w_router[0].T, router_bias[0].reshape(N_EXPERTS, 1), w_s_gate[0].astype(BF16),
               w_s_up[0].astype(BF16), w_s_down[0].astype(BF16))
    x1w, base, idx_t, wt_t, rank_t, counts = _merge(x_p, x_s, attn_p, attn_s, rnn_p, rnn_s, sga, sgr, merge_w)

    n_assign = (t_p + t_s) * TOP_K
    chunk_rows = _expert_chunk_rows(n_assign)
    pstart, pad_lo, pad_hi, n_chunks, rows, ffn_end, rows_alloc = _expert_layout(counts, n_assign, chunk_rows)
    tm = _pick_tile(math.gcd(t_p, t_s), (256, 128))
    dest, word_rows = _dest_rows(idx_t, rank_t, pstart)
    xs = _sc_scatter_rows(x1w, word_rows, rows_alloc * ROW_TILE)
    xs = _zero_padding(xs, pad_lo, pad_hi, pad_hi[N_EXPERTS - 1:], rows_alloc)
    out_sorted = _expert_ffn(xs, pstart, n_chunks, rows, ffn_end, rows_alloc, chunk_rows,
                             w_e_gate[0], w_e_up[0], w_e_down[0])
    n_tiles = (t_p + t_s) // tm
    n_head = min(t_p // tm, n_tiles // 3)
    w_tok = wt_t.T
    g2, b2 = vec(ln2_g), vec(ln2_b)
    y_head = _combine_head(out_sorted, dest, w_tok, base, g2, b2, tm, n_head)
    gathered = _sc_gather_rows(out_sorted, word_rows[n_head * (tm // SC_ROWS) * WORD_ROW_VECS:].reshape(-1))
    y_p, y_s = _combine_tail(y_head, gathered, w_tok, base, g2, b2, t_p, t_s, tm, n_head)
    y_p = y_p.reshape(batch, seq, D_MODEL)
    y_s = y_s.reshape(dec_batch, s_len, D_MODEL)
    kv5 = lambda a, b: a.reshape(1, b, WINDOW, N_KV_HEADS, HEAD_DIM)
    tail = lambda a, n: jnp.stack([lax.slice_in_dim(a, (b + 1) * seq - n, (b + 1) * seq) for b in range(batch)])
    k_p = kv5(tail(k, WINDOW), batch)
    v_p = kv5(tail(v, WINDOW), batch)
    conv_p = tail(xr, CONV_W - 1)[None]
    conv_s = xr[t_p:].reshape(dec_batch, s_len, D_RNN)[:, s_len - (CONV_W - 1):][None]
    return (y_p, y_s, k_p, v_p, conv_p, h_p.reshape(1, batch, D_RNN),
            kv5(k_s, dec_batch), kv5(v_s, dec_batch), conv_s, h_s.reshape(1, dec_batch, D_RNN))
```

```python
import functools
import math

import jax
import jax.numpy as jnp
from jax import lax
from jax.experimental import pallas as pl
from jax.experimental.pallas import tpu as pltpu
from jax.experimental.pallas import tpu_sc as plsc

F32 = jnp.float32
BF16 = jnp.bfloat16
WORD = jnp.int32

D_MODEL = 1024
N_HEADS = 8
N_KV_HEADS = 2
HEAD_DIM = 64
GROUP = N_HEADS // N_KV_HEADS
WINDOW = 128
D_ATTN = N_HEADS * HEAD_DIM
D_KV = N_KV_HEADS * HEAD_DIM
N_BUCKETS = 32
MAX_DISTANCE = 128
D_RNN = D_MODEL
RNN_BLOCK = 256
N_RNN_BLOCKS = D_RNN // RNN_BLOCK
CONV_W = 4
RG_C = 8.0
N_EXPERTS = 256
TOP_K = 8
N_GROUPS = 8
GROUP_SIZE = N_EXPERTS // N_GROUPS
TOPK_GROUPS = 4
D_EXPERT = D_MODEL // 4
ROUTED_SCALE = 2.5
LN_EPS = 1e-5
DEPTH = 1
ALPHA = (2 * DEPTH) ** 0.25
NEG_INF = -1e30
SM_SCALE = HEAD_DIM ** -0.5
assert math.frexp(SM_SCALE)[0] == 0.5

O_Q = 0
O_K = D_ATTN
O_V = O_K + D_KV
O_XR = O_V + D_KV
O_YR = O_XR + D_RNN
O_GA = O_YR + D_RNN
O_GR = O_GA + D_MODEL
D_IN = O_GR + D_MODEL

SUBLANES = 8
VMEM_LIMIT_BYTES = 56 * 1024 * 1024
EXPERT_ROWS = 256
CHUNK_PART = 128
ROW_PAD = SUBLANES


def _params(sem):
    return pltpu.CompilerParams(dimension_semantics=sem, vmem_limit_bytes=VMEM_LIMIT_BYTES)


def _pick_tile(n, candidates):
    for c in candidates:
        if n % c == 0:
            return c
    raise ValueError(f"no tile for {n}")


def _const_spec(shape):
    nd = len(shape)
    return pl.BlockSpec(shape, lambda *_: (0,) * nd)


LANES = 128
ROW_WORDS = D_MODEL // 2
ROW_TILE = ROW_WORDS // LANES
HIGH_HALF = -65536


def _pack_words(mat):
    as_bits = lambda v: pltpu.bitcast(v.astype(BF16).astype(F32), WORD)
    return (as_bits(mat[:, ROW_WORDS:]) & HIGH_HALF) | lax.shift_right_logical(as_bits(mat[:, :ROW_WORDS]), 16)


def _unpack_words(words):
    return pltpu.bitcast(words << 16, F32), pltpu.bitcast(words & HIGH_HALF, F32)


def _store_token_rows(ref, mat, n, row0=0, base=0):
    words = _pack_words(mat)
    for s in range(ROW_TILE):
        ref[pl.ds(base + row0 * ROW_TILE + s, n, stride=ROW_TILE), :] = words[:, s * LANES:(s + 1) * LANES]


def _load_token_rows(ref, n, row0=0, base=0):
    words = jnp.concatenate(
        [ref[pl.ds(base + row0 * ROW_TILE + s, n, stride=ROW_TILE), :] for s in range(ROW_TILE)], axis=1)
    return _unpack_words(words)


def _two_source_specs(t_p, t_s, tm, width):
    n_p = t_p // tm
    assert t_p % tm == 0 and t_s % tm == 0
    return (pl.BlockSpec((tm, width), lambda i, *_: (jnp.minimum(i, n_p - 1), 0)),
            pl.BlockSpec((tm, width), lambda i, *_: (jnp.maximum(i - n_p, 0), 0)))


def _inproj_kernel(xp_ref, xs_ref, w_ref, q_ref, k_ref, v_ref, xr_ref, gy_ref, sga_ref, sgr_ref, *, n_p):
    x = jnp.where(pl.program_id(0) < n_p, xp_ref[...], xs_ref[...]).astype(BF16)

    def seg(lo, hi):
        return jnp.dot(x, w_ref[:, lo:hi], preferred_element_type=F32)

    q_ref[...] = (seg(O_Q, O_K) * SM_SCALE).astype(BF16)
    k_ref[...] = seg(O_K, O_V)
    v_ref[...] = seg(O_V, O_XR)
    xr_ref[...] = seg(O_XR, O_YR)
    gy_ref[...] = jax.nn.gelu(seg(O_YR, O_GA)).astype(BF16)
    sga_ref[...] = jax.nn.sigmoid(seg(O_GA, O_GR)).astype(BF16)
    sgr_ref[...] = jax.nn.sigmoid(seg(O_GR, D_IN)).astype(BF16)


def _inproj(x_p, x_s, w_in_bf16):
    t_p, t_s = x_p.shape[0], x_s.shape[0]
    t = t_p + t_s
    tm = _pick_tile(math.gcd(t_p, t_s), (256, 128, 64, 32, 16, 8))
    row = lambda width: pl.BlockSpec((tm, width), lambda i: (i, 0))
    out_shape = (
        jax.ShapeDtypeStruct((t, D_ATTN), BF16),
        jax.ShapeDtypeStruct((t, D_KV), F32),
        jax.ShapeDtypeStruct((t, D_KV), F32),
        jax.ShapeDtypeStruct((t, D_RNN), F32),
        jax.ShapeDtypeStruct((t, D_RNN), BF16),
        jax.ShapeDtypeStruct((t, D_MODEL), BF16),
        jax.ShapeDtypeStruct((t, D_MODEL), BF16),
    )
    return pl.pallas_call(
        functools.partial(_inproj_kernel, n_p=t_p // tm),
        grid=(t // tm,),
        in_specs=[*_two_source_specs(t_p, t_s, tm, D_MODEL), _const_spec((D_MODEL, D_IN))],
        out_specs=(row(D_ATTN), row(D_KV), row(D_KV), row(D_RNN), row(D_RNN), row(D_MODEL), row(D_MODEL)),
        out_shape=out_shape,
        compiler_params=_params(("parallel",)),
        name="inproj",
    )(x_p, x_s, w_in_bf16)


def _t5_bucket(dist):
    n = jnp.maximum(dist, 0)
    max_exact = N_BUCKETS // 2
    nf = jnp.maximum(n, 1).astype(F32)
    large = max_exact + (jnp.log(nf / max_exact) / math.log(MAX_DISTANCE / max_exact) * (N_BUCKETS - max_exact)).astype(jnp.int32)
    large = jnp.minimum(large, N_BUCKETS - 1)
    return jnp.where(n < max_exact, n, large)


def _bias_table(rel_bias, dist):
    bucket = _t5_bucket(dist)
    rb = rel_bias.astype(F32)
    out = jnp.zeros((N_HEADS, *dist.shape), F32)
    for j in range(N_BUCKETS):
        out = jnp.where(bucket[None] == j, rb[j][:, None, None], out)
    return out


def _softmax_pv(s, sink, v):
    m = jnp.maximum(jnp.max(s, axis=-1, keepdims=True), sink)
    p = jnp.exp(s - m)
    denom = jnp.sum(p, axis=-1, keepdims=True) + jnp.exp(sink - m)
    return jnp.dot(p.astype(BF16), v, preferred_element_type=F32), denom


def _attn_prompt_kernel(sink_ref, q_ref, kc_ref, kp_ref, vc_ref, vp_ref, bias_ref, o_ref, *, q_blocks):
    keys = jnp.concatenate([kp_ref[...], kc_ref[...]], axis=0).astype(BF16)
    vals = jnp.concatenate([vp_ref[...], vc_ref[...]], axis=0).astype(BF16)
    first_table = jnp.minimum(pl.program_id(1), 1)
    for j in range(q_blocks):
        table = first_table if j == 0 else 1
        kk = keys[j * WINDOW:(j + 2) * WINDOW]
        vv = vals[j * WINDOW:(j + 2) * WINDOW]
        q = q_ref[j * WINDOW:(j + 1) * WINDOW, :]
        for g in range(N_KV_HEADS):
            kg = kk[:, g * HEAD_DIM:(g + 1) * HEAD_DIM]
            vg = vv[:, g * HEAD_DIM:(g + 1) * HEAD_DIM]
            for h in range(GROUP):
                hh = g * GROUP + h
                qh = q[:, hh * HEAD_DIM:(hh + 1) * HEAD_DIM]
                s = lax.dot_general(qh, kg, (((1,), (1,)), ((), ())), preferred_element_type=F32) + bias_ref[table, hh]
                o, denom = _softmax_pv(s, sink_ref[0, hh], vg)
                o_ref[j * WINDOW:(j + 1) * WINDOW, hh * HEAD_DIM:(hh + 1) * HEAD_DIM] = (o / denom).astype(BF16)


def _attn_prompt(q_all, k_all, v_all, bias, sinks, batch, seq):
    nb = seq // WINDOW
    q_blocks = _pick_tile(nb, (2, 1))
    steps = nb // q_blocks
    cur = lambda width: pl.BlockSpec((q_blocks * WINDOW, width), lambda b, n: (b * steps + n, 0))
    prev = lambda width: pl.BlockSpec((WINDOW, width), lambda b, n: (b * nb + jnp.maximum(n * q_blocks - 1, 0), 0))
    return pl.pallas_call(
        functools.partial(_attn_prompt_kernel, q_blocks=q_blocks),
        grid=(batch, steps),
        in_specs=[
            pl.BlockSpec(memory_space=pltpu.SMEM),
            cur(D_ATTN), cur(D_KV), prev(D_KV), cur(D_KV), prev(D_KV),
            _const_spec((2, N_HEADS, WINDOW, 2 * WINDOW)),
        ],
        out_specs=cur(D_ATTN),
        out_shape=jax.ShapeDtypeStruct((batch * seq, D_ATTN), BF16),
        compiler_params=_params(("parallel", "arbitrary")),
        name="attn_prompt",
    )(sinks.reshape(1, N_HEADS).astype(F32), q_all, k_all, k_all, v_all, v_all, bias)


def _attn_sample_kernel(q_ref, kn_ref, vn_ref, kc_ref, vc_ref, bc_ref, bn_ref, sink_ref,
                        o_ref, ko_ref, vo_ref, *, seqs, s_len):
    w = kc_ref.shape[1]
    rows_c = lax.broadcasted_iota(jnp.int32, (GROUP * s_len, w), 0) % s_len
    cols_c = lax.broadcasted_iota(jnp.int32, (GROUP * s_len, w), 1)
    dist_c = rows_c + w - cols_c
    valid_c = (dist_c >= 0) & (dist_c <= WINDOW)
    rows_n = lax.broadcasted_iota(jnp.int32, (GROUP * s_len, s_len), 0) % s_len
    cols_n = lax.broadcasted_iota(jnp.int32, (GROUP * s_len, s_len), 1)
    dist_n = rows_n - cols_n
    valid_n = (dist_n >= 0) & (dist_n <= WINDOW)
    for j in range(seqs):
        r0 = j * s_len
        qj = q_ref[r0:r0 + s_len, :]
        kc = kc_ref[j]
        vc = vc_ref[j]
        kn = kn_ref[r0:r0 + s_len, :]
        vn = vn_ref[r0:r0 + s_len, :]
        ko_ref[j, 0:w - s_len, :] = kc[s_len:, :]
        ko_ref[j, w - s_len:w, :] = kn
        vo_ref[j, 0:w - s_len, :] = vc[s_len:, :]
        vo_ref[j, w - s_len:w, :] = vn
        kcb, vcb, knb, vnb = kc.astype(BF16), vc.astype(BF16), kn.astype(BF16), vn.astype(BF16)
        for g in range(N_KV_HEADS):
            lo, hi = g * HEAD_DIM, (g + 1) * HEAD_DIM
            qs = jnp.concatenate(
                [qj[:, (g * GROUP + h) * HEAD_DIM:(g * GROUP + h + 1) * HEAD_DIM] for h in range(GROUP)], axis=0)
            nt = (((1,), (1,)), ((), ()))
            s_c = lax.dot_general(qs, kcb[:, lo:hi], nt, preferred_element_type=F32)
            s_n = lax.dot_general(qs, knb[:, lo:hi], nt, preferred_element_type=F32)
            s_c = jnp.where(valid_c, s_c + bc_ref[g], NEG_INF)
            s_n = jnp.where(valid_n, s_n + bn_ref[g], NEG_INF)
            sink = sink_ref[g]
            m = jnp.maximum(jnp.maximum(jnp.max(s_c, axis=-1, keepdims=True), jnp.max(s_n, axis=-1, keepdims=True)), sink)
            p_c = jnp.exp(s_c - m)
            p_n = jnp.exp(s_n - m)
            denom = jnp.sum(p_c, axis=-1, keepdims=True) + jnp.sum(p_n, axis=-1, keepdims=True) + jnp.exp(sink - m)
            o = jnp.dot(p_c.astype(BF16), vcb[:, lo:hi], preferred_element_type=F32)
            o = o + jnp.dot(p_n.astype(BF16), vnb[:, lo:hi], preferred_element_type=F32)
            o = (o / denom).astype(BF16)
            for h in range(GROUP):
                hh = g * GROUP + h
                o_ref[r0:r0 + s_len, hh * HEAD_DIM:(hh + 1) * HEAD_DIM] = o[h * s_len:(h + 1) * s_len, :]


def _attn_sample(q_all, k_all, v_all, cache_k, cache_v, rel_bias, sinks, row0, dec_batch, s_len):
    w = cache_k.shape[1]
    seqs = _pick_tile(dec_batch, (16, 8, 4, 2, 1))
    rows = seqs * s_len
    blk0 = row0 // rows
    assert row0 % rows == 0
    qi = jnp.arange(s_len)
    dist_c = qi[:, None] + w - jnp.arange(w)[None, :]
    dist_n = qi[:, None] - jnp.arange(s_len)[None, :]
    b_c = _bias_table(rel_bias, dist_c).reshape(N_KV_HEADS, GROUP * s_len, w)
    b_n = _bias_table(rel_bias, dist_n).reshape(N_KV_HEADS, GROUP * s_len, s_len)
    sink = jnp.broadcast_to(sinks.astype(F32).reshape(N_KV_HEADS, GROUP, 1, 1), (N_KV_HEADS, GROUP, s_len, 1))
    sink = sink.reshape(N_KV_HEADS, GROUP * s_len, 1)
    tok = lambda width: pl.BlockSpec((rows, width), lambda i: (blk0 + i, 0))
    cache = pl.BlockSpec((seqs, w, D_KV), lambda i: (i, 0, 0))
    return pl.pallas_call(
        functools.partial(_attn_sample_kernel, seqs=seqs, s_len=s_len),
        grid=(dec_batch // seqs,),
        in_specs=[
            tok(D_ATTN), tok(D_KV), tok(D_KV), cache, cache,
            _const_spec(b_c.shape), _const_spec(b_n.shape), _const_spec(sink.shape),
        ],
        out_specs=(pl.BlockSpec((rows, D_ATTN), lambda i: (i, 0)), cache, cache),
        out_shape=(
            jax.ShapeDtypeStruct((dec_batch * s_len, D_ATTN), BF16),
            jax.ShapeDtypeStruct((dec_batch, w, D_KV), F32),
            jax.ShapeDtypeStruct((dec_batch, w, D_KV), F32),
        ),
        compiler_params=_params(("parallel",)),
        name="attn_sample",
    )(q_all, k_all, v_all, cache_k, cache_v, b_c, b_n, sink)


def _softplus(z):
    return jnp.maximum(z, 0.0) + jnp.log1p(jnp.exp(-jnp.abs(z)))


def _block_gate(xcb, w_ref, b_ref):
    parts = [jnp.dot(xcb[:, n * RNN_BLOCK:(n + 1) * RNN_BLOCK], w_ref[n], preferred_element_type=F32)
             for n in range(N_RNN_BLOCKS)]
    return jax.nn.sigmoid(jnp.concatenate(parts, axis=-1) + b_ref[...])


def _lru_coeffs(xc, wa_ref, ba_ref, wx_ref, bx_ref, lam_ref, first_row_unnormalised):
    xcb = xc.astype(BF16)
    r = _block_gate(xcb, wa_ref, ba_ref)
    i = _block_gate(xcb, wx_ref, bx_ref)
    log_a = -RG_C * r * _softplus(-lam_ref[...])
    a = jnp.exp(log_a)
    mult = jnp.sqrt(-jnp.tanh(log_a) * (a * a + 1.0))
    if first_row_unnormalised is not None:
        mult = jnp.where(first_row_unnormalised, 1.0, mult)
    return a, mult * i * xc


def _scan8(a, b):
    shape = a.shape
    grouped = (shape[0] // SUBLANES, SUBLANES, shape[1])
    a = a.reshape(grouped)
    b = b.reshape(grouped)
    r8 = lax.broadcasted_iota(jnp.int32, grouped, 1)
    d = 1
    while d < SUBLANES:
        keep = r8 >= d
        a_sh = jnp.where(keep, pltpu.roll(a, d, 1), 1.0)
        b_sh = jnp.where(keep, pltpu.roll(b, d, 1), 0.0)
        b = a * b_sh + b
        a = a * a_sh
        d *= 2
    return a.reshape(shape), b.reshape(shape)


def _rnn_prompt_kernel(xr_ref, gy_ref, cw_ref, cb_ref, wa_ref, ba_ref, wx_ref, bx_ref, lam_ref,
                       o_ref, nh_ref, ext_ref, a_ref, b_ref, hc_ref, *, tl):
    l = pl.program_id(1)

    @pl.when(l == 0)
    def _():
        ext_ref[0:SUBLANES, :] = jnp.zeros((SUBLANES, D_RNN), F32)
        hc_ref[...] = jnp.zeros((1, D_RNN), F32)

    x = xr_ref[...]
    ext_ref[SUBLANES:, :] = x
    xc = cb_ref[...] + cw_ref[CONV_W - 1:CONV_W, :] * x
    for j in range(1, CONV_W):
        xc = xc + cw_ref[CONV_W - 1 - j:CONV_W - j, :] * ext_ref[SUBLANES - j:SUBLANES - j + tl, :]
    ext_ref[0:SUBLANES, :] = x[tl - SUBLANES:, :]

    row = lax.broadcasted_iota(jnp.int32, (tl, D_RNN), 0)
    a, b = _lru_coeffs(xc, wa_ref, ba_ref, wx_ref, bx_ref, lam_ref, (row == 0) & (l == 0))
    a, b = _scan8(a, b)
    a_ref[...] = a
    b_ref[...] = b

    def chunk(c, h):
        sl = pl.ds(pl.multiple_of(c * SUBLANES, SUBLANES), SUBLANES)
        hc = b_ref[sl, :] + a_ref[sl, :] * h
        b_ref[sl, :] = hc
        return hc[SUBLANES - 1:SUBLANES, :]

    h = lax.fori_loop(0, tl // SUBLANES, chunk, hc_ref[...])
    hc_ref[...] = h
    nh_ref[0] = h
    o_ref[...] = (b_ref[...] * gy_ref[...]).astype(BF16)


def _rnn_prompt(xr_all, gy_all, rnn_w, batch, seq):
    tl =_pick_tile(seq, (256, 128, 64, 32, 16, 8))
    nl = seq // tl
    tok = pl.BlockSpec((tl, D_RNN), lambda b, l: (b * nl + l, 0))
    return pl.pallas_call(
        functools.partial(_rnn_prompt_kernel, tl=tl),
        grid=(batch, nl),
        in_specs=[tok, tok] + [_const_spec(w.shape) for w in rnn_w],
        out_specs=(tok, pl.BlockSpec((1, 1, D_RNN), lambda b, l: (b, 0, 0))),
        out_shape=(jax.ShapeDtypeStruct((batch * seq, D_RNN), BF16), jax.ShapeDtypeStruct((batch, 1, D_RNN), F32)),
        scratch_shapes=[
            pltpu.VMEM((tl + SUBLANES, D_RNN), F32),
            pltpu.VMEM((tl, D_RNN), F32),
            pltpu.VMEM((tl, D_RNN), F32),
            pltpu.VMEM((1, D_RNN), F32),
        ],
        compiler_params=_params(("parallel", "arbitrary")),
        name="rnn_prompt",
    )(xr_all, gy_all, *rnn_w)


def _rnn_sample_kernel(xr_ref, gy_ref, hp_ref, h0_ref, cw_ref, cb_ref, wa_ref, ba_ref, wx_ref, bx_ref, lam_ref,
                       o_ref, nh_ref, *, seqs):
    rows = seqs * SUBLANES
    x = xr_ref[...]
    hp = hp_ref[...]
    r8 = lax.broadcasted_iota(jnp.int32, (rows, D_RNN), 0) % SUBLANES
    xc = cb_ref[...] + cw_ref[CONV_W - 1:CONV_W, :] * x
    for j in range(1, CONV_W):
        shifted = jnp.where(r8 >= j, pltpu.roll(x, j, 0), pltpu.roll(hp, rows - (SUBLANES - j), 0))
        xc = xc + cw_ref[CONV_W - 1 - j:CONV_W - j, :] * shifted
    a, b = _lru_coeffs(xc, wa_ref, ba_ref, wx_ref, bx_ref, lam_ref, None)
    a, b = _scan8(a, b)
    h0 = jnp.broadcast_to(h0_ref[...][:, None, :], (seqs, SUBLANES, D_RNN)).reshape(rows, D_RNN)
    h = b + a * h0
    last = jnp.where(r8 == SUBLANES - 1, h, 0.0).reshape(seqs, SUBLANES, D_RNN)
    nh_ref[...] = jnp.sum(last, axis=1)
    o_ref[...] = (h * gy_ref[...]).astype(BF16)


def _rnn_sample(xr_all, gy_all, hist_pad, h0, rnn_w, row0, dec_batch):
    seqs = _pick_tile(dec_batch, (16, 8))
    rows = seqs * SUBLANES
    assert row0 % rows == 0
    blk0 = row0 // rows
    tok = pl.BlockSpec((rows, D_RNN), lambda i: (blk0 + i, 0))
    return pl.pallas_call(
        functools.partial(_rnn_sample_kernel, seqs=seqs),
        grid=(dec_batch // seqs,),
        in_specs=[tok, tok, pl.BlockSpec((rows, D_RNN), lambda i: (i, 0)), pl.BlockSpec((seqs, D_RNN), lambda i: (i, 0))]
        + [_const_spec(w.shape) for w in rnn_w],
        out_specs=(pl.BlockSpec((rows, D_RNN), lambda i: (i, 0)), pl.BlockSpec((seqs, D_RNN), lambda i: (i, 0))),
        out_shape=(jax.ShapeDtypeStruct((dec_batch * SUBLANES, D_RNN), BF16),
                   jax.ShapeDtypeStruct((dec_batch, D_RNN), F32)),
        compiler_params=_params(("parallel",)),
        name="rnn_sample",
    )(xr_all, gy_all, hist_pad, h0, *rnn_w)


def _layer_norm(z, g, b):
    mu = jnp.mean(z, axis=-1, keepdims=True)
    zc = z - mu
    var = jnp.mean(zc * zc, axis=-1, keepdims=True)
    return zc * lax.rsqrt(var + LN_EPS) * g + b


def _first_index_of_max(vals, iota, axis, sentinel):
    mx = jnp.max(vals, axis=axis, keepdims=True)
    return mx, jnp.min(jnp.where(vals == mx, iota, sentinel), axis=axis, keepdims=True)


def _route(scores, bias):
    t = scores.shape[1]
    grp = scores + bias
    g3 = grp.reshape(N_GROUPS, GROUP_SIZE, t)
    e_in_g = lax.broadcasted_iota(jnp.int32, g3.shape, 1)
    m1, first = _first_index_of_max(g3, e_in_g, 1, GROUP_SIZE)
    m2 = jnp.max(jnp.where(e_in_g == first, -jnp.inf, g3), axis=1, keepdims=True)
    gscore = (m1 + m2).reshape(N_GROUPS, t)
    g_iota = lax.broadcasted_iota(jnp.int32, gscore.shape, 0)
    gmask = jnp.zeros(gscore.shape, jnp.bool_)
    for _ in range(TOPK_GROUPS):
        _, gi = _first_index_of_max(gscore, g_iota, 0, N_GROUPS)
        hit = g_iota == gi
        gmask = gmask | hit
        gscore = jnp.where(hit, -jnp.inf, gscore)
    masked = jnp.where(gmask[:, None, :], g3, -jnp.inf).reshape(N_EXPERTS, t)
    e_iota = lax.broadcasted_iota(jnp.int32, masked.shape, 0)
    idx, wts, hits = [], [], []
    for _ in range(TOP_K):
        _, ei = _first_index_of_max(masked, e_iota, 0, N_EXPERTS)
        hit = e_iota == ei
        idx.append(ei)
        hits.append(hit)
        wts.append(jnp.sum(jnp.where(hit, scores, 0.0), axis=0, keepdims=True))
        masked = jnp.where(hit, -jnp.inf, masked)
    idx = jnp.concatenate(idx, axis=0)
    w = jnp.concatenate(wts, axis=0)
    w = w / jnp.sum(w, axis=0, keepdims=True) * ROUTED_SCALE
    return idx, w, hits


def _merge_kernel(xp_ref, xs_ref, aop_ref, aos_ref, rop_ref, ros_ref, sga_ref, sgr_ref, woa_ref, wor_ref, wout_ref, g1_ref, b1_ref,
                  wrt_ref, rb_ref, wsg_ref, wsu_ref, wsd_ref,
                  x1w_ref, base_ref, idx_ref, wt_ref, rank_ref, cnt_ref, carry_ref, *, n_p, tm):
    i = pl.program_id(0)

    @pl.when(i == 0)
    def _():
        carry_ref[...] = jnp.zeros(carry_ref.shape, F32)

    is_prompt = i < n_p
    x = jnp.where(is_prompt, xp_ref[...], xs_ref[...])
    pa = jnp.dot(jnp.where(is_prompt, aop_ref[...], aos_ref[...]), woa_ref[...], preferred_element_type=F32)
    pr = jnp.dot(jnp.where(is_prompt, rop_ref[...], ros_ref[...]), wor_ref[...], preferred_element_type=F32)
    merged = sga_ref[...] * pa + sgr_ref[...] * pr
    z = ALPHA * x + jnp.dot(merged.astype(BF16), wout_ref[...], preferred_element_type=F32)
    x1 = _layer_norm(z, g1_ref[...], b1_ref[...])
    words = _pack_words(x1)
    for h in range(tm // SC_ROWS):
        for s in range(ROW_TILE):
            x1w_ref[(h * ROW_TILE + s) * SC_ROWS:(h * ROW_TILE + s + 1) * SC_ROWS, :] = (
                words[h * SC_ROWS:(h + 1) * SC_ROWS, s * LANES:(s + 1) * LANES])
    x1b = x1.astype(BF16)
    u = jax.nn.silu(jnp.dot(x1b, wsg_ref[...], preferred_element_type=F32)) * jnp.dot(x1b, wsu_ref[...], preferred_element_type=F32)
    shared = jnp.dot(u.astype(BF16), wsd_ref[...], preferred_element_type=F32)
    base_ref[...] = ALPHA * x1 + shared
    logits = lax.dot_general(wrt_ref[...], x1, (((1,), (1,)), ((), ())), preferred_element_type=F32)
    idx, w, hits = _route(jax.nn.sigmoid(logits), rb_ref[...])
    idx_ref[...] = idx
    wt_ref[...] = w

    chosen = functools.reduce(jnp.logical_or, hits)
    chosen_f = jnp.where(chosen, 1.0, 0.0)
    earlier = (lax.broadcasted_iota(jnp.int32, (tm, tm), 0) < lax.broadcasted_iota(jnp.int32, (tm, tm), 1))
    prefix = jnp.dot(chosen_f.astype(BF16), jnp.where(earlier, 1.0, 0.0).astype(BF16), preferred_element_type=F32)
    before = prefix + carry_ref[...]
    ranks = [jnp.sum(jnp.where(hit, before, 0.0), axis=0, keepdims=True) for hit in hits]
    rank_ref[...] = jnp.concatenate(ranks, axis=0).astype(jnp.int32)
    carry_ref[...] = carry_ref[...] + jnp.sum(chosen_f, axis=1, keepdims=True)
    cnt_ref[...] = carry_ref[...]


def _merge(x_p, x_s, attn_p, attn_s, rnn_p, rnn_s, sga, sgr, weights):
    t_p, t_s = x_p.shape[0], x_s.shape[0]
    t = t_p + t_s
    tm = _pick_tile(math.gcd(t_p, t_s), (512, 256, 128))
    row = lambda width: pl.BlockSpec((tm, width), lambda i: (i, 0))
    col = pl.BlockSpec((TOP_K, tm), lambda i: (0, i))

    return pl.pallas_call(
        functools.partial(_merge_kernel, n_p=t_p // tm, tm=tm),
        grid=(t // tm,),
        in_specs=[*_two_source_specs(t_p, t_s, tm, D_MODEL), *_two_source_specs(t_p, t_s, tm, D_ATTN),
                  *_two_source_specs(t_p, t_s, tm, D_RNN), row(D_MODEL), row(D_MODEL)]
        + [_const_spec(w.shape) for w in weights],
        out_specs=(pl.BlockSpec((tm * ROW_TILE, LANES), lambda i: (i, 0)), row(D_MODEL), col, col, col,
                   _const_spec((N_EXPERTS, 1))),
        out_shape=(
            jax.ShapeDtypeStruct((t * ROW_TILE, LANES), WORD),
            jax.ShapeDtypeStruct((t, D_MODEL), F32),
            jax.ShapeDtypeStruct((TOP_K, t), jnp.int32),
            jax.ShapeDtypeStruct((TOP_K, t), F32),
            jax.ShapeDtypeStruct((TOP_K, t), jnp.int32),
            jax.ShapeDtypeStruct((N_EXPERTS, 1), F32),
        ),
        scratch_shapes=[pltpu.VMEM((N_EXPERTS, 1), F32)],
        compiler_params=_params(("arbitrary",)),
        name="merge_ln1_route",
    )(x_p, x_s, attn_p, attn_s, rnn_p, rnn_s, sga, sgr, *weights)


def _expert_chunk_rows(n_assign):
    return max(EXPERT_ROWS, -(-(n_assign // N_EXPERTS * 9 // 8) // EXPERT_ROWS) * EXPERT_ROWS)


def _expert_layout(counts, n_assign, chunk_rows):
    counts = counts.reshape(N_EXPERTS).astype(jnp.int32)
    padded = (counts + ROW_PAD - 1) // ROW_PAD * ROW_PAD
    pend = jnp.cumsum(padded)
    pstart = pend - padded
    rows_alloc = (n_assign + N_EXPERTS * (ROW_PAD - 1)) // ROW_PAD * ROW_PAD + chunk_rows
    n_chunks = jnp.maximum((counts + chunk_rows - 1) // chunk_rows, 1)
    part = CHUNK_PART
    written = jnp.maximum((counts + part - 1) // part, 1) * part
    ffn_end = jnp.max(pstart + written).reshape(1)
    return pstart, pstart + counts, pend, n_chunks, counts, ffn_end, rows_alloc


def _dest_kernel(idx_ref, rank_ref, pstart_ref, dest_ref, word_rows_ref):
    e_iota = lax.broadcasted_iota(jnp.int32, (N_EXPERTS, idx_ref.shape[1]), 0)
    starts = [jnp.sum(jnp.where(e_iota == idx_ref[k:k + 1, :], pstart_ref[...], 0), axis=0, keepdims=True)
              for k in range(TOP_K)]
    dest = jnp.concatenate(starts, axis=0) + rank_ref[...]
    dest_ref[...] = dest
    tm = dest.shape[1]
    word_rows_ref[0] = jnp.concatenate(
        [dest[k:k + 1, h * SC_ROWS:(h + 1) * SC_ROWS] * ROW_TILE + s
         for h in range(tm // SC_ROWS) for s in range(ROW_TILE) for k in range(TOP_K)], axis=0)


WORD_ROW_VECS = ROW_TILE * TOP_K


def _dest_rows(idx_t, rank_t, pstart):
    t = idx_t.shape[1]
    tm = _pick_tile(t, (1024, 512, 256, 128))
    col = pl.BlockSpec((TOP_K, tm), lambda i: (0, i))
    vecs = tm // SC_ROWS * WORD_ROW_VECS
    dest, word_rows = pl.pallas_call(
        _dest_kernel,
        grid=(t // tm,),
        in_specs=[col, col, _const_spec((N_EXPERTS, 1))],
        out_specs=(col, pl.BlockSpec((1, vecs, SC_ROWS), lambda i: (i, 0, 0))),
        out_shape=(jax.ShapeDtypeStruct((TOP_K, t), jnp.int32),
                   jax.ShapeDtypeStruct((t // tm, vecs, SC_ROWS), jnp.int32)),
        compiler_params=_params(("parallel",)),
        name="dest_rows",
    )(idx_t, rank_t, pstart.reshape(N_EXPERTS, 1))
    return dest, word_rows.reshape(-1, SC_ROWS)


def _token_rows(r, n=1):
    return pl.ds(pl.multiple_of(r * ROW_TILE, ROW_TILE), n * ROW_TILE)


def _zero_row_groups(zero_ref, dst_ref, sem, first_group, n_groups):
    def start(g, c):
        pltpu.make_async_copy(zero_ref, dst_ref.at[_token_rows(g * ROW_PAD, ROW_PAD)], sem).start()
        return c

    lax.fori_loop(first_group, n_groups, start, 0)
    return n_groups - first_group


def _wait_zero_copies(zero_ref, dst_ref, sem, n):
    def wait(_, c):
        pltpu.make_async_copy(zero_ref, dst_ref.at[_token_rows(0, ROW_PAD)], sem).wait()
        return c

    lax.fori_loop(0, n, wait, 0)


SC_ROWS = 128


def _sc_workers():
    info = plsc.get_sparse_core_info()
    mesh = plsc.VectorSubcoreMesh(core_axis_name="c", subcore_axis_name="s")
    worker = lambda: lax.axis_index("s") * info.num_cores + lax.axis_index("c")
    return mesh, info.num_cores * info.num_subcores, worker


def _sc_scatter_rows(src, rows, n_out):
    mesh, n_workers, worker = _sc_workers()
    n_units = src.shape[0] // SC_ROWS
    assert n_units % n_workers == 0 and rows.shape == (n_units * TOP_K, SC_ROWS)
    per_worker = n_units // n_workers

    unit_bufs = [pltpu.VMEM((TOP_K, SC_ROWS), jnp.int32), pltpu.VMEM((SC_ROWS, LANES), src.dtype),
                 pltpu.SemaphoreType.DMA]

    @functools.partial(
        pl.kernel, mesh=mesh,
        out_type=jax.ShapeDtypeStruct((n_out, LANES), src.dtype),
        scratch_types=unit_bufs + unit_bufs + [pltpu.SemaphoreType.DMA],
    )
    def scatter(src_hbm, rows_hbm, out_hbm, rows_a, data_a, load_a, rows_b, data_b, load_b, sem):
        first = worker() * per_worker
        bufs = ((rows_a, data_a, load_a), (rows_b, data_b, load_b))

        def loads(u, buf):
            rows_v, data_v, load_sem = buf
            return (pltpu.make_async_copy(rows_hbm.at[pl.ds(u * TOP_K, TOP_K)], rows_v, load_sem),
                    pltpu.make_async_copy(src_hbm.at[pl.ds(u * SC_ROWS, SC_ROWS)], data_v, load_sem))

        def unit(u, buf, other, has_next):
            for cp in loads(u, buf):
                cp.wait()

            @pl.when(has_next)
            def _():
                for cp in loads(u + 1, other):
                    cp.start()

            rows_v, data_v, _ = buf
            copies = [pltpu.async_copy(data_v, out_hbm.at[rows_v.at[k]], sem) for k in range(TOP_K)]
            for cp in copies:
                cp.wait()

        for cp in loads(first, bufs[0]):
            cp.start()

        @pl.loop(0, per_worker // 2)
        def _(pair):
            u = first + 2 * pair
            unit(u, bufs[0], bufs[1], True)
            unit(u + 1, bufs[1], bufs[0], 2 * pair + 2 < per_worker)

        if per_worker % 2:
            unit(first + per_worker - 1, bufs[0], bufs[1], False)

    return scatter(src, rows)


def _zero_padding_kernel(lo_ref, hi_ref, tail_ref, xs_in_ref, xs_ref, zero_ref, sem, *, rows_alloc):
    del xs_in_ref
    zero_ref[...] = jnp.zeros(zero_ref.shape, WORD)
    row_copy = lambda r: pltpu.make_async_copy(zero_ref.at[pl.ds(0, ROW_TILE)], xs_ref.at[_token_rows(r)], sem)

    def expert(e, n):
        def row(r, c):
            row_copy(r).start()
            return c

        lax.fori_loop(lo_ref[e], hi_ref[e], row, 0)
        return n + hi_ref[e] - lo_ref[e]

    def wait(_, c):
        row_copy(0).wait()
        return c

    lax.fori_loop(0, lax.fori_loop(0, N_EXPERTS, expert, 0), wait, 0)
    n = _zero_row_groups(zero_ref, xs_ref, sem, tail_ref[0] // ROW_PAD, rows_alloc // ROW_PAD)
    _wait_zero_copies(zero_ref, xs_ref, sem, n)


def _zero_padding(xs, pad_lo, pad_hi, total, rows_alloc):
    grid_spec = pltpu.PrefetchScalarGridSpec(
        num_scalar_prefetch=3,
        grid=(1,),
        in_specs=[pl.BlockSpec(memory_space=pl.ANY)],
        out_specs=pl.BlockSpec(memory_space=pl.ANY),
        scratch_shapes=[pltpu.VMEM((ROW_PAD * ROW_TILE, LANES), WORD), pltpu.SemaphoreType.DMA],
    )
    return pl.pallas_call(
        functools.partial(_zero_padding_kernel, rows_alloc=rows_alloc),
        grid_spec=grid_spec,
        out_shape=jax.ShapeDtypeStruct(xs.shape, xs.dtype),
        input_output_aliases={3: 0},
        compiler_params=_params(("arbitrary",)),
        name="zero_padding",
    )(pad_lo, pad_hi, total, xs)


def _expert_kernel(pstart_ref, nch_ref, rows_ref, end_ref, wg_ref, wu_ref, wd_ref, xs_ref, o_ref,
                   xbuf_ref, obuf_ref, wgb_ref, wub_ref, wdb_ref, zero_ref, done_ref, in_sem, out_sem,
                   *, rows_alloc, chunk_rows):
    e = pl.program_id(0)
    n_e = pl.num_programs(0)
    start = pstart_ref[e]
    nch = nch_ref[e]
    buf_rows = chunk_rows * ROW_TILE
    part = CHUNK_PART
    chunk_parts = chunk_rows // part
    part_rows = part * ROW_TILE

    class _ChunkCopy:
        def __init__(self, whole, parts, n_parts):
            self.whole = whole
            self.parts = parts
            self.n_parts = n_parts

        def _each(self, act):
            pl.when(self.n_parts == chunk_parts)(functools.partial(act, self.whole))
            for p, cp in enumerate(self.parts):
                pl.when((p < self.n_parts) & (self.n_parts < chunk_parts))(functools.partial(act, cp))

        def start(self):
            self._each(lambda cp: cp.start())

        def wait(self):
            self._each(lambda cp: cp.wait())

    def parts_of(rows_left):
        return jnp.clip((rows_left + part - 1) // part, 1, chunk_parts)

    def in_copy(row, slot, n_parts):
        copy = lambda p, n: pltpu.make_async_copy(
            xs_ref.at[_token_rows(row + p * part, n * part)],
            xbuf_ref.at[pl.ds(slot * buf_rows + p * part_rows, n * part_rows)], in_sem.at[slot])
        return _ChunkCopy(copy(0, chunk_parts), [copy(p, 1) for p in range(chunk_parts - 1)], n_parts)

    def out_copy(row, slot, n_parts):
        copy = lambda p, n: pltpu.make_async_copy(
            obuf_ref.at[pl.ds(slot * buf_rows + p * part_rows, n * part_rows)],
            o_ref.at[_token_rows(row + p * part, n * part)], out_sem)
        return _ChunkCopy(copy(0, chunk_parts), [copy(p, 1) for p in range(chunk_parts - 1)], n_parts)

    @pl.when(e == 0)
    def _():
        done_ref[0] = 0
        xbuf_ref[...] = jnp.zeros(xbuf_ref.shape, WORD)
        obuf_ref[...] = jnp.zeros(obuf_ref.shape, WORD)
        in_copy(start, 0, parts_of(rows_ref[0])).start()

    wgb_ref[...] = wg_ref[...].astype(BF16)
    wub_ref[...] = wu_ref[...].astype(BF16)
    wdb_ref[...] = wd_ref[...].astype(BF16)
    done = done_ref[0]

    def chunk(c, carry):
        g = done + c
        slot = g % 2
        row = start + c * chunk_rows
        rows_left = rows_ref[e] - c * chunk_rows
        n_parts = parts_of(rows_left)
        in_copy(row, slot, n_parts).wait()
        last = c + 1 == nch
        next_e = jnp.minimum(e + 1, n_e - 1)
        next_row = jnp.where(last, pstart_ref[next_e], row + chunk_rows)
        next_left = jnp.where(last, rows_ref[next_e], rows_left - chunk_rows)

        @pl.when(jnp.logical_not(last & (e == n_e - 1)))
        def _():
            in_copy(next_row, 1 - slot, parts_of(next_left)).start()

        base = pl.multiple_of(slot * buf_rows, buf_rows)
        def sub_block(h):
            xb = jnp.concatenate(_load_token_rows(xbuf_ref, EXPERT_ROWS, row0=h * EXPERT_ROWS, base=base),
                                 axis=1).astype(BF16)
            gate = jnp.dot(xb, wgb_ref[...], preferred_element_type=F32)
            up = jnp.dot(xb, wub_ref[...], preferred_element_type=F32)
            act = (jax.nn.silu(gate) * up).astype(BF16)
            _store_token_rows(obuf_ref, jnp.dot(act, wdb_ref[...], preferred_element_type=F32),
                              EXPERT_ROWS, row0=h * EXPERT_ROWS, base=base)

        n_sub = chunk_rows // EXPERT_ROWS
        needed = jnp.clip((rows_left + EXPERT_ROWS - 1) // EXPERT_ROWS, 1, n_sub)
        for count in range(1, n_sub + 1):
            @pl.when(needed == count)
            def _():
                for h in range(count):
                    sub_block(h)

        @pl.when(g > 0)
        def _():
            out_copy(0, 0, done_ref[1]).wait()

        out_copy(row, slot, n_parts).start()
        done_ref[1] = n_parts
        return carry

    lax.fori_loop(0, nch, chunk, 0)
    done_ref[0] = done + nch

    @pl.when(e == n_e - 1)
    def _():
        out_copy(0, 0, done_ref[1]).wait()
        zero_ref[...] = jnp.zeros(zero_ref.shape, WORD)
        n = _zero_row_groups(zero_ref, o_ref, out_sem, end_ref[0] // ROW_PAD, rows_alloc // ROW_PAD)
        _wait_zero_copies(zero_ref, o_ref, out_sem, n)


def _expert_ffn(xs, pstart, n_chunks, rows, ffn_end, rows_alloc, chunk_rows, w_e_gate, w_e_up, w_e_down):
    weight = lambda shape: pl.BlockSpec((None, *shape), lambda e, *_: (e, 0, 0))
    grid_spec = pltpu.PrefetchScalarGridSpec(
        num_scalar_prefetch=4,
        grid=(N_EXPERTS,),
        in_specs=[weight((D_MODEL, D_EXPERT)), weight((D_MODEL, D_EXPERT)), weight((D_EXPERT, D_MODEL)),
                  pl.BlockSpec(memory_space=pl.ANY)],
        out_specs=pl.BlockSpec(memory_space=pl.ANY),
        scratch_shapes=[
            pltpu.VMEM((2 * chunk_rows * ROW_TILE, LANES), WORD),
            pltpu.VMEM((2 * chunk_rows * ROW_TILE, LANES), WORD),
            pltpu.VMEM((D_MODEL, D_EXPERT), BF16), pltpu.VMEM((D_MODEL, D_EXPERT), BF16),
            pltpu.VMEM((D_EXPERT, D_MODEL), BF16),
            pltpu.VMEM((ROW_PAD * ROW_TILE, LANES), WORD),
            pltpu.SMEM((2,), jnp.int32),
            pltpu.SemaphoreType.DMA((2,)), pltpu.SemaphoreType.DMA,
        ],
    )
    return pl.pallas_call(
        functools.partial(_expert_kernel, rows_alloc=rows_alloc, chunk_rows=chunk_rows),
        grid_spec=grid_spec,
        out_shape=jax.ShapeDtypeStruct((rows_alloc * ROW_TILE, LANES), WORD),
        compiler_params=_params(("arbitrary",)),
        name="expert_ffn",
    )(pstart, n_chunks, rows, ffn_end, w_e_gate, w_e_up, w_e_down, xs)


def _combine_head_kernel(dest_ref, dest_next_ref, w_ref, base_ref, g_ref, b_ref, outs_ref, y_ref, buf_ref, sem, *, tm):
    i = pl.program_id(0)
    slot_rows = TOP_K * tm

    def gather(d_ref, slot):
        def issue(t, c):
            for k in range(TOP_K):
                pltpu.make_async_copy(outs_ref.at[_token_rows(d_ref[k, t])],
                                      buf_ref.at[_token_rows(slot * slot_rows + k * tm + t)], sem.at[slot]).start()
            return c

        lax.fori_loop(0, tm, issue, 0)

    @pl.when(i == 0)
    def _():
        gather(dest_ref, 0)

    @pl.when(i + 1 < pl.num_programs(0))
    def _():
        gather(dest_next_ref, (i + 1) % 2)

    slot = i % 2
    for k in range(TOP_K):
        pltpu.make_async_copy(outs_ref.at[_token_rows(0, tm)], buf_ref.at[_token_rows(slot * slot_rows + k * tm, tm)],
                              sem.at[slot]).wait()

    w = w_ref[...]
    base = pl.multiple_of(slot * slot_rows * ROW_TILE, ROW_TILE)
    halves = [None, None]
    for k in range(TOP_K):
        wk = w[:, k:k + 1]
        for j, rows in enumerate(_load_token_rows(buf_ref, tm, row0=k * tm, base=base)):
            halves[j] = wk * rows if halves[j] is None else halves[j] + wk * rows
    y_ref[...] = _layer_norm(base_ref[...] + jnp.concatenate(halves, axis=1), g_ref[...], b_ref[...])


def _combine_head(out_sorted, dest, w_tok, base, g2, b2, tm, n_head):
    row = pl.BlockSpec((tm, D_MODEL), lambda i: (i, 0))
    return pl.pallas_call(
        functools.partial(_combine_head_kernel, tm=tm),
        grid=(n_head,),
        in_specs=[
            pl.BlockSpec((TOP_K, tm), lambda i: (0, i), memory_space=pltpu.SMEM),
            pl.BlockSpec((TOP_K, tm), lambda i: (0, jnp.minimum(i + 1, n_head - 1)), memory_space=pltpu.SMEM),
            pl.BlockSpec((tm, TOP_K), lambda i: (i, 0)),
            row, _const_spec((1, D_MODEL)), _const_spec((1, D_MODEL)),
            pl.BlockSpec(memory_space=pl.ANY),
        ],
        out_specs=row,
        out_shape=jax.ShapeDtypeStruct((n_head * tm, D_MODEL), F32),
        scratch_shapes=[pltpu.VMEM((2 * TOP_K * tm * ROW_TILE, LANES), WORD), pltpu.SemaphoreType.DMA((2,))],
        compiler_params=_params(("arbitrary",)),
        name="combine_head",
    )(dest, dest, w_tok, base, g2, b2, out_sorted)


def _sc_gather_rows(table, rows):
    mesh, n_workers, worker = _sc_workers()
    m = rows.shape[0]
    per_worker = m // n_workers
    assert m % n_workers == 0 and per_worker % SC_ROWS == 0
    in_flight = _pick_tile(per_worker // SC_ROWS, (2, 1))
    step = SC_ROWS * in_flight
    n_stages = per_worker // step
    stage_bufs = [pltpu.VMEM((step,), jnp.int32), pltpu.VMEM((step, LANES), table.dtype),
                  pltpu.SemaphoreType.DMA, pltpu.SemaphoreType.DMA]

    @functools.partial(
        pl.kernel, mesh=mesh,
        out_type=jax.ShapeDtypeStruct((m, LANES), table.dtype),
        scratch_types=stage_bufs + stage_bufs,
    )
    def gather(table_hbm, rows_hbm, out_hbm, rows_a, data_a, gsem_a, ssem_a, rows_b, data_b, gsem_b, ssem_b):
        first = worker() * per_worker
        bufs = ((rows_a, data_a, gsem_a, ssem_a), (rows_b, data_b, gsem_b, ssem_b))

        def gathers(buf):
            rows_v, data_v, gsem, _ = buf
            return [pltpu.make_async_copy(table_hbm.at[rows_v.at[pl.ds(j * SC_ROWS, SC_ROWS)]],
                                          data_v.at[pl.ds(j * SC_ROWS, SC_ROWS)], gsem) for j in range(in_flight)]

        def store(off, buf):
            return pltpu.make_async_copy(buf[1], out_hbm.at[pl.ds(off, step)], buf[3])

        def stage(it, buf, other, has_next, other_stored):
            off = first + it * step

            @pl.when(has_next)
            def _():
                @pl.when(other_stored)
                def _():
                    store(off, other).wait()

                pltpu.sync_copy(rows_hbm.at[pl.ds(off + step, step)], other[0])
                for cp in gathers(other):
                    cp.start()

            for cp in gathers(buf):
                cp.wait()
            store(off, buf).start()

        pltpu.sync_copy(rows_hbm.at[pl.ds(first, step)], rows_a)
        for cp in gathers(bufs[0]):
            cp.start()

        @pl.loop(0, n_stages // 2)
        def _(pair):
            it = 2 * pair
            stage(it, bufs[0], bufs[1], True, pair > 0)
            stage(it + 1, bufs[1], bufs[0], it + 2 < n_stages, True)

        if n_stages % 2:
            stage(n_stages - 1, bufs[0], bufs[1], False, False)
        for back in range(min(2, n_stages)):
            store(first, bufs[(n_stages - 1 - back) % 2]).wait()

    return gather(table, rows)


def _combine_tail_kernel(yh_ref, rows_ref, w_ref, base_ref, g_ref, b_ref, yp_ref, ys_ref, *, tm, n_head, n_p):
    i = pl.program_id(0)

    @pl.when(i < n_head)
    def _():
        yp_ref[...] = yh_ref[...]

    @pl.when(i >= n_head)
    def _():
        w = w_ref[...]
        lo = [None] * ROW_TILE
        hi = [None] * ROW_TILE
        for k in range(TOP_K):
            wk = w[:, k:k + 1]
            for s in range(ROW_TILE):
                vec = lambda h: ((h * ROW_TILE + s) * TOP_K + k) * SC_ROWS
                words = jnp.concatenate([rows_ref[vec(h):vec(h) + SC_ROWS, :] for h in range(tm // SC_ROWS)], axis=0)
                lo_s, hi_s = _unpack_words(words)
                lo[s] = wk * lo_s if lo[s] is None else lo[s] + wk * lo_s
                hi[s] = wk * hi_s if hi[s] is None else hi[s] + wk * hi_s
        y = _layer_norm(base_ref[...] + jnp.concatenate(lo + hi, axis=1), g_ref[...], b_ref[...])

        @pl.when(i < n_p)
        def _():
            yp_ref[...] = y

        @pl.when(i >= n_p)
        def _():
            ys_ref[...] = y


def _combine_tail(y_head, gathered, w_tok, base, g2, b2, t_p, t_s, tm, n_head):
    n_p = t_p // tm
    assert n_head <= n_p
    n_tiles = (t_p + t_s) // tm
    blk = TOP_K * ROW_TILE * tm
    row = pl.BlockSpec((tm, D_MODEL), lambda i: (i, 0))
    out_p, out_s = _two_source_specs(t_p, t_s, tm, D_MODEL)
    return pl.pallas_call(
        functools.partial(_combine_tail_kernel, tm=tm, n_head=n_head, n_p=n_p),
        grid=(n_tiles,),
        in_specs=[
            pl.BlockSpec((tm, D_MODEL), lambda i: (jnp.minimum(i, n_head - 1), 0)),
            pl.BlockSpec((blk, LANES), lambda i: (jnp.maximum(i - n_head, 0), 0)),
            pl.BlockSpec((tm, TOP_K), lambda i: (i, 0)),
            row, _const_spec((1, D_MODEL)), _const_spec((1, D_MODEL)),
        ],
        out_specs=(out_p, out_s),
        out_shape=(jax.ShapeDtypeStruct((t_p, D_MODEL), F32), jax.ShapeDtypeStruct((t_s, D_MODEL), F32)),
        compiler_params=_params(("arbitrary",)),
        name="combine_tail",
    )(y_head, gathered, w_tok, base, g2, b2)


def kernel(x_prompt, x_sample, cache_k, cache_v, state_conv, state_rnn, w_in, conv_w, conv_b, w_gate_a, b_gate_a, w_gate_x, b_gate_x, lru_lambda, rel_bias, sinks, w_o_attn, w_o_rnn, w_out, ln1_g, ln1_b, w_router, router_bias, w_e_gate, w_e_up, w_e_down, w_s_gate, w_s_up, w_s_down, ln2_g, ln2_b):
    assert w_in.shape[0] == DEPTH == 1
    batch, seq, _ = x_prompt.shape
    dec_batch, s_len, _ = x_sample.shape
    w_cache = cache_k.shape[2]
    assert s_len == SUBLANES and seq % WINDOW == 0 and w_cache == WINDOW
    t_p = batch * seq
    t_s = dec_batch * s_len
    vec = lambda a: a[0].reshape(1, -1).astype(F32)

    x_p = x_prompt.reshape(t_p, D_MODEL)
    x_s = x_sample.reshape(t_s, D_MODEL)
    q, k, v, xr, gy, sga, sgr = _inproj(x_p, x_s, w_in[0].astype(BF16))

    qi = jnp.arange(WINDOW)
    dist = qi[:, None] + WINDOW - jnp.arange(2 * WINDOW)[None, :]
    band = (dist >= 0) & (dist <= WINDOW)
    has_prev = jnp.arange(2 * WINDOW)[None, :] >= WINDOW
    bias_p = _bias_table(rel_bias, dist)
    bias_p = jnp.stack([jnp.where(band & has_prev, bias_p, NEG_INF), jnp.where(band, bias_p, NEG_INF)])
    attn_p = _attn_prompt(q, k, v, bias_p, sinks[0], batch, seq)
    attn_s, k_s, v_s = _attn_sample(
        q, k, v, cache_k[0].reshape(dec_batch, w_cache, D_KV), cache_v[0].reshape(dec_batch, w_cache, D_KV),
        rel_bias, sinks[0], t_p, dec_batch, s_len)

    rnn_w = (conv_w[0], vec(conv_b), w_gate_a[0].astype(BF16), vec(b_gate_a), w_gate_x[0].astype(BF16),
             vec(b_gate_x), vec(lru_lambda))
    rnn_p, h_p = _rnn_prompt(xr, gy, rnn_w, batch, seq)
    hist_pad = jnp.pad(state_conv[0], ((0, 0), (SUBLANES - (CONV_W - 1), 0), (0, 0))).reshape(t_s, D_RNN)
    rnn_s, h_s = _rnn_sample(xr, gy, hist_pad, state_rnn[0], rnn_w, t_p, dec_batch)

    merge_w = (w_o_attn[0].astype(BF16), w_o_rnn[0].astype(BF16), w_out[0].astype(BF16), vec(ln1_g), vec(ln1_b),
               w_router[0].T, router_bias[0].reshape(N_EXPERTS, 1), w_s_gate[0].astype(BF16),
               w_s_up[0].astype(BF16), w_s_down[0].astype(BF16))
    x1w, base, idx_t, wt_t, rank_t, counts = _merge(x_p, x_s, attn_p, attn_s, rnn_p, rnn_s, sga, sgr, merge_w)

    n_assign = (t_p + t_s) * TOP_K
    chunk_rows = _expert_chunk_rows(n_assign)
    pstart, pad_lo, pad_hi, n_chunks, rows, ffn_end, rows_alloc = _expert_layout(counts, n_assign, chunk_rows)
    tm = _pick_tile(math.gcd(t_p, t_s), (256, 128))
    dest, word_rows = _dest_rows(idx_t, rank_t, pstart)
    xs = _sc_scatter_rows(x1w, word_rows, rows_alloc * ROW_TILE)
    xs = _zero_padding(xs, pad_lo, pad_hi, pad_hi[N_EXPERTS - 1:], rows_alloc)
    out_sorted = _expert_ffn(xs, pstart, n_chunks, rows, ffn_end, rows_alloc, chunk_rows,
                             w_e_gate[0], w_e_up[0], w_e_down[0])
    n_tiles = (t_p + t_s) // tm
    n_head = min(t_p // tm, n_tiles // 4)
    w_tok = wt_t.T
    g2, b2 = vec(ln2_g), vec(ln2_b)
    y_head = _combine_head(out_sorted, dest, w_tok, base, g2, b2, tm, n_head)
    gathered = _sc_gather_rows(out_sorted, word_rows[n_head * (tm // SC_ROWS) * WORD_ROW_VECS:].reshape(-1))
    y_p, y_s = _combine_tail(y_head, gathered, w_tok, base, g2, b2, t_p, t_s, tm, n_head)
    y_p = y_p.reshape(batch, seq, D_MODEL)
    y_s = y_s.reshape(dec_batch, s_len, D_MODEL)
    kv5 = lambda a, b: a.reshape(1, b, WINDOW, N_KV_HEADS, HEAD_DIM)
    tail = lambda a, n: jnp.stack([lax.slice_in_dim(a, (b + 1) * seq - n, (b + 1) * seq) for b in range(batch)])
    k_p = kv5(tail(k, WINDOW), batch)
    v_p = kv5(tail(v, WINDOW), batch)
    conv_p = tail(xr, CONV_W - 1)[None]
    conv_s = xr[t_p:].reshape(dec_batch, s_len, D_RNN)[:, s_len - (CONV_W - 1):][None]
    return (y_p, y_s, k_p, v_p, conv_p, h_p.reshape(1, batch, D_RNN),
            kv5(k_s, dec_batch), kv5(v_s, dec_batch), conv_s, h_s.reshape(1, dec_batch, D_RNN))
```

```python
import functools
import math

import jax
import jax.numpy as jnp
from jax import lax
from jax.experimental import pallas as pl
from jax.experimental.pallas import tpu as pltpu
from jax.experimental.pallas import tpu_sc as plsc

F32 = jnp.float32
BF16 = jnp.bfloat16
WORD = jnp.int32

D_MODEL = 1024
N_HEADS = 8
N_KV_HEADS = 2
HEAD_DIM = 64
GROUP = N_HEADS // N_KV_HEADS
WINDOW = 128
D_ATTN = N_HEADS * HEAD_DIM
D_KV = N_KV_HEADS * HEAD_DIM
N_BUCKETS = 32
MAX_DISTANCE = 128
D_RNN = D_MODEL
RNN_BLOCK = 256
N_RNN_BLOCKS = D_RNN // RNN_BLOCK
CONV_W = 4
RG_C = 8.0
N_EXPERTS = 256
TOP_K = 8
N_GROUPS = 8
GROUP_SIZE = N_EXPERTS // N_GROUPS
TOPK_GROUPS = 4
D_EXPERT = D_MODEL // 4
ROUTED_SCALE = 2.5
LN_EPS = 1e-5
DEPTH = 1
ALPHA = (2 * DEPTH) ** 0.25
NEG_INF = -1e30
SM_SCALE = HEAD_DIM ** -0.5
assert math.frexp(SM_SCALE)[0] == 0.5

O_Q = 0
O_K = D_ATTN
O_V = O_K + D_KV
O_XR = O_V + D_KV
O_YR = O_XR + D_RNN
O_GA = O_YR + D_RNN
O_GR = O_GA + D_MODEL
D_IN = O_GR + D_MODEL

SUBLANES = 8
VMEM_LIMIT_BYTES = 56 * 1024 * 1024
EXPERT_ROWS = 256
CHUNK_PART = 128
ROW_PAD = SUBLANES


def _params(sem):
    return pltpu.CompilerParams(dimension_semantics=sem, vmem_limit_bytes=VMEM_LIMIT_BYTES)


def _pick_tile(n, candidates):
    for c in candidates:
        if n % c == 0:
            return c
    raise ValueError(f"no tile for {n}")


def _const_spec(shape):
    nd = len(shape)
    return pl.BlockSpec(shape, lambda *_: (0,) * nd)


LANES = 128
ROW_WORDS = D_MODEL // 2
ROW_TILE = ROW_WORDS // LANES
HIGH_HALF = -65536


def _pack_words(mat):
    as_bits = lambda v: pltpu.bitcast(v.astype(BF16).astype(F32), WORD)
    return (as_bits(mat[:, ROW_WORDS:]) & HIGH_HALF) | lax.shift_right_logical(as_bits(mat[:, :ROW_WORDS]), 16)


def _unpack_words(words):
    return pltpu.bitcast(words << 16, F32), pltpu.bitcast(words & HIGH_HALF, F32)


def _store_token_rows(ref, mat, n, row0=0, base=0):
    words = _pack_words(mat)
    for s in range(ROW_TILE):
        ref[pl.ds(base + row0 * ROW_TILE + s, n, stride=ROW_TILE), :] = words[:, s * LANES:(s + 1) * LANES]


def _load_token_rows(ref, n, row0=0, base=0):
    words = jnp.concatenate(
        [ref[pl.ds(base + row0 * ROW_TILE + s, n, stride=ROW_TILE), :] for s in range(ROW_TILE)], axis=1)
    return _unpack_words(words)


def _two_source_specs(t_p, t_s, tm, width):
    n_p = t_p // tm
    assert t_p % tm == 0 and t_s % tm == 0
    return (pl.BlockSpec((tm, width), lambda i, *_: (jnp.minimum(i, n_p - 1), 0)),
            pl.BlockSpec((tm, width), lambda i, *_: (jnp.maximum(i - n_p, 0), 0)))


def _inproj_kernel(xp_ref, xs_ref, w_ref, q_ref, k_ref, v_ref, xr_ref, gy_ref, sga_ref, sgr_ref, *, n_p):
    x = jnp.where(pl.program_id(0) < n_p, xp_ref[...], xs_ref[...]).astype(BF16)

    def seg(lo, hi):
        return jnp.dot(x, w_ref[:, lo:hi], preferred_element_type=F32)

    q_ref[...] = (seg(O_Q, O_K) * SM_SCALE).astype(BF16)
    k_ref[...] = seg(O_K, O_V)
    v_ref[...] = seg(O_V, O_XR)
    xr_ref[...] = seg(O_XR, O_YR)
    gy_ref[...] = jax.nn.gelu(seg(O_YR, O_GA)).astype(BF16)
    sga_ref[...] = jax.nn.sigmoid(seg(O_GA, O_GR)).astype(BF16)
    sgr_ref[...] = jax.nn.sigmoid(seg(O_GR, D_IN)).astype(BF16)


def _inproj(x_p, x_s, w_in_bf16):
    t_p, t_s = x_p.shape[0], x_s.shape[0]
    t = t_p + t_s
    tm = _pick_tile(math.gcd(t_p, t_s), (256, 128, 64, 32, 16, 8))
    row = lambda width: pl.BlockSpec((tm, width), lambda i: (i, 0))
    out_shape = (
        jax.ShapeDtypeStruct((t, D_ATTN), BF16),
        jax.ShapeDtypeStruct((t, D_KV), F32),
        jax.ShapeDtypeStruct((t, D_KV), F32),
        jax.ShapeDtypeStruct((t, D_RNN), F32),
        jax.ShapeDtypeStruct((t, D_RNN), BF16),
        jax.ShapeDtypeStruct((t, D_MODEL), BF16),
        jax.ShapeDtypeStruct((t, D_MODEL), BF16),
    )
    return pl.pallas_call(
        functools.partial(_inproj_kernel, n_p=t_p // tm),
        grid=(t // tm,),
        in_specs=[*_two_source_specs(t_p, t_s, tm, D_MODEL), _const_spec((D_MODEL, D_IN))],
        out_specs=(row(D_ATTN), row(D_KV), row(D_KV), row(D_RNN), row(D_RNN), row(D_MODEL), row(D_MODEL)),
        out_shape=out_shape,
        compiler_params=_params(("parallel",)),
        name="inproj",
    )(x_p, x_s, w_in_bf16)


def _t5_bucket(dist):
    n = jnp.maximum(dist, 0)
    max_exact = N_BUCKETS // 2
    nf = jnp.maximum(n, 1).astype(F32)
    large = max_exact + (jnp.log(nf / max_exact) / math.log(MAX_DISTANCE / max_exact) * (N_BUCKETS - max_exact)).astype(jnp.int32)
    large = jnp.minimum(large, N_BUCKETS - 1)
    return jnp.where(n < max_exact, n, large)


def _bias_table(rel_bias, dist):
    bucket = _t5_bucket(dist)
    rb = rel_bias.astype(F32)
    out = jnp.zeros((N_HEADS, *dist.shape), F32)
    for j in range(N_BUCKETS):
        out = jnp.where(bucket[None] == j, rb[j][:, None, None], out)
    return out


def _softmax_pv(s, sink, v):
    m = jnp.maximum(jnp.max(s, axis=-1, keepdims=True), sink)
    p = jnp.exp(s - m)
    denom = jnp.sum(p, axis=-1, keepdims=True) + jnp.exp(sink - m)
    return jnp.dot(p.astype(BF16), v, preferred_element_type=F32), denom


def _attn_prompt_kernel(sink_ref, q_ref, kc_ref, kp_ref, vc_ref, vp_ref, bias_ref, o_ref, *, q_blocks):
    keys = jnp.concatenate([kp_ref[...], kc_ref[...]], axis=0).astype(BF16)
    vals = jnp.concatenate([vp_ref[...], vc_ref[...]], axis=0).astype(BF16)
    first_table = jnp.minimum(pl.program_id(1), 1)
    for j in range(q_blocks):
        table = first_table if j == 0 else 1
        kk = keys[j * WINDOW:(j + 2) * WINDOW]
        vv = vals[j * WINDOW:(j + 2) * WINDOW]
        q = q_ref[j * WINDOW:(j + 1) * WINDOW, :]
        for g in range(N_KV_HEADS):
            kg = kk[:, g * HEAD_DIM:(g + 1) * HEAD_DIM]
            vg = vv[:, g * HEAD_DIM:(g + 1) * HEAD_DIM]
            for h in range(GROUP):
                hh = g * GROUP + h
                qh = q[:, hh * HEAD_DIM:(hh + 1) * HEAD_DIM]
                s = lax.dot_general(qh, kg, (((1,), (1,)), ((), ())), preferred_element_type=F32) + bias_ref[table, hh]
                o, denom = _softmax_pv(s, sink_ref[0, hh], vg)
                o_ref[j * WINDOW:(j + 1) * WINDOW, hh * HEAD_DIM:(hh + 1) * HEAD_DIM] = (o / denom).astype(BF16)


def _attn_prompt(q_all, k_all, v_all, bias, sinks, batch, seq):
    nb = seq // WINDOW
    q_blocks = _pick_tile(nb, (2, 1))
    steps = nb // q_blocks
    cur = lambda width: pl.BlockSpec((q_blocks * WINDOW, width), lambda b, n: (b * steps + n, 0))
    prev = lambda width: pl.BlockSpec((WINDOW, width), lambda b, n: (b * nb + jnp.maximum(n * q_blocks - 1, 0), 0))
    return pl.pallas_call(
        functools.partial(_attn_prompt_kernel, q_blocks=q_blocks),
        grid=(batch, steps),
        in_specs=[
            pl.BlockSpec(memory_space=pltpu.SMEM),
            cur(D_ATTN), cur(D_KV), prev(D_KV), cur(D_KV), prev(D_KV),
            _const_spec((2, N_HEADS, WINDOW, 2 * WINDOW)),
        ],
        out_specs=cur(D_ATTN),
        out_shape=jax.ShapeDtypeStruct((batch * seq, D_ATTN), BF16),
        compiler_params=_params(("parallel", "arbitrary")),
        name="attn_prompt",
    )(sinks.reshape(1, N_HEADS).astype(F32), q_all, k_all, k_all, v_all, v_all, bias)


def _attn_sample_kernel(q_ref, kn_ref, vn_ref, kc_ref, vc_ref, bc_ref, bn_ref, sink_ref,
                        o_ref, ko_ref, vo_ref, *, seqs, s_len):
    w = kc_ref.shape[1]
    rows_c = lax.broadcasted_iota(jnp.int32, (GROUP * s_len, w), 0) % s_len
    cols_c = lax.broadcasted_iota(jnp.int32, (GROUP * s_len, w), 1)
    dist_c = rows_c + w - cols_c
    valid_c = (dist_c >= 0) & (dist_c <= WINDOW)
    rows_n = lax.broadcasted_iota(jnp.int32, (GROUP * s_len, s_len), 0) % s_len
    cols_n = lax.broadcasted_iota(jnp.int32, (GROUP * s_len, s_len), 1)
    dist_n = rows_n - cols_n
    valid_n = (dist_n >= 0) & (dist_n <= WINDOW)
    for j in range(seqs):
        r0 = j * s_len
        qj = q_ref[r0:r0 + s_len, :]
        kc = kc_ref[j]
        vc = vc_ref[j]
        kn = kn_ref[r0:r0 + s_len, :]
        vn = vn_ref[r0:r0 + s_len, :]
        ko_ref[j, 0:w - s_len, :] = kc[s_len:, :]
        ko_ref[j, w - s_len:w, :] = kn
        vo_ref[j, 0:w - s_len, :] = vc[s_len:, :]
        vo_ref[j, w - s_len:w, :] = vn
        kcb, vcb, knb, vnb = kc.astype(BF16), vc.astype(BF16), kn.astype(BF16), vn.astype(BF16)
        for g in range(N_KV_HEADS):
            lo, hi = g * HEAD_DIM, (g + 1) * HEAD_DIM
            qs = jnp.concatenate(
                [qj[:, (g * GROUP + h) * HEAD_DIM:(g * GROUP + h + 1) * HEAD_DIM] for h in range(GROUP)], axis=0)
            nt = (((1,), (1,)), ((), ()))
            s_c = lax.dot_general(qs, kcb[:, lo:hi], nt, preferred_element_type=F32)
            s_n = lax.dot_general(qs, knb[:, lo:hi], nt, preferred_element_type=F32)
            s_c = jnp.where(valid_c, s_c + bc_ref[g], NEG_INF)
            s_n = jnp.where(valid_n, s_n + bn_ref[g], NEG_INF)
            sink = sink_ref[g]
            m = jnp.maximum(jnp.maximum(jnp.max(s_c, axis=-1, keepdims=True), jnp.max(s_n, axis=-1, keepdims=True)), sink)
            p_c = jnp.exp(s_c - m)
            p_n = jnp.exp(s_n - m)
            denom = jnp.sum(p_c, axis=-1, keepdims=True) + jnp.sum(p_n, axis=-1, keepdims=True) + jnp.exp(sink - m)
            o = jnp.dot(p_c.astype(BF16), vcb[:, lo:hi], preferred_element_type=F32)
            o = o + jnp.dot(p_n.astype(BF16), vnb[:, lo:hi], preferred_element_type=F32)
            o = (o / denom).astype(BF16)
            for h in range(GROUP):
                hh = g * GROUP + h
                o_ref[r0:r0 + s_len, hh * HEAD_DIM:(hh + 1) * HEAD_DIM] = o[h * s_len:(h + 1) * s_len, :]


def _attn_sample(q_all, k_all, v_all, cache_k, cache_v, rel_bias, sinks, row0, dec_batch, s_len):
    w = cache_k.shape[1]
    seqs = _pick_tile(dec_batch, (16, 8, 4, 2, 1))
    rows = seqs * s_len
    blk0 = row0 // rows
    assert row0 % rows == 0
    qi = jnp.arange(s_len)
    dist_c = qi[:, None] + w - jnp.arange(w)[None, :]
    dist_n = qi[:, None] - jnp.arange(s_len)[None, :]
    b_c = _bias_table(rel_bias, dist_c).reshape(N_KV_HEADS, GROUP * s_len, w)
    b_n = _bias_table(rel_bias, dist_n).reshape(N_KV_HEADS, GROUP * s_len, s_len)
    sink = jnp.broadcast_to(sinks.astype(F32).reshape(N_KV_HEADS, GROUP, 1, 1), (N_KV_HEADS, GROUP, s_len, 1))
    sink = sink.reshape(N_KV_HEADS, GROUP * s_len, 1)
    tok = lambda width: pl.BlockSpec((rows, width), lambda i: (blk0 + i, 0))
    cache = pl.BlockSpec((seqs, w, D_KV), lambda i: (i, 0, 0))
    return pl.pallas_call(
        functools.partial(_attn_sample_kernel, seqs=seqs, s_len=s_len),
        grid=(dec_batch // seqs,),
        in_specs=[
            tok(D_ATTN), tok(D_KV), tok(D_KV), cache, cache,
            _const_spec(b_c.shape), _const_spec(b_n.shape), _const_spec(sink.shape),
        ],
        out_specs=(pl.BlockSpec((rows, D_ATTN), lambda i: (i, 0)), cache, cache),
        out_shape=(
            jax.ShapeDtypeStruct((dec_batch * s_len, D_ATTN), BF16),
            jax.ShapeDtypeStruct((dec_batch, w, D_KV), F32),
            jax.ShapeDtypeStruct((dec_batch, w, D_KV), F32),
        ),
        compiler_params=_params(("parallel",)),
        name="attn_sample",
    )(q_all, k_all, v_all, cache_k, cache_v, b_c, b_n, sink)


def _softplus(z):
    return jnp.maximum(z, 0.0) + jnp.log1p(jnp.exp(-jnp.abs(z)))


def _block_gate(xcb, w_ref, b_ref):
    parts = [jnp.dot(xcb[:, n * RNN_BLOCK:(n + 1) * RNN_BLOCK], w_ref[n], preferred_element_type=F32)
             for n in range(N_RNN_BLOCKS)]
    return jax.nn.sigmoid(jnp.concatenate(parts, axis=-1) + b_ref[...])


def _lru_coeffs(xc, wa_ref, ba_ref, wx_ref, bx_ref, lam_ref, first_row_unnormalised):
    xcb = xc.astype(BF16)
    r = _block_gate(xcb, wa_ref, ba_ref)
    i = _block_gate(xcb, wx_ref, bx_ref)
    log_a = -RG_C * r * _softplus(-lam_ref[...])
    a = jnp.exp(log_a)
    mult = jnp.sqrt(-jnp.tanh(log_a) * (a * a + 1.0))
    if first_row_unnormalised is not None:
        mult = jnp.where(first_row_unnormalised, 1.0, mult)
    return a, mult * i * xc


def _scan8(a, b):
    shape = a.shape
    grouped = (shape[0] // SUBLANES, SUBLANES, shape[1])
    a = a.reshape(grouped)
    b = b.reshape(grouped)
    r8 = lax.broadcasted_iota(jnp.int32, grouped, 1)
    d = 1
    while d < SUBLANES:
        keep = r8 >= d
        a_sh = jnp.where(keep, pltpu.roll(a, d, 1), 1.0)
        b_sh = jnp.where(keep, pltpu.roll(b, d, 1), 0.0)
        b = a * b_sh + b
        a = a * a_sh
        d *= 2
    return a.reshape(shape), b.reshape(shape)


def _rnn_prompt_kernel(xr_ref, gy_ref, cw_ref, cb_ref, wa_ref, ba_ref, wx_ref, bx_ref, lam_ref,
                       o_ref, nh_ref, ext_ref, a_ref, b_ref, hc_ref, *, tl):
    l = pl.program_id(1)

    @pl.when(l == 0)
    def _():
        ext_ref[0:SUBLANES, :] = jnp.zeros((SUBLANES, D_RNN), F32)
        hc_ref[...] = jnp.zeros((1, D_RNN), F32)

    x = xr_ref[...]
    ext_ref[SUBLANES:, :] = x
    xc = cb_ref[...] + cw_ref[CONV_W - 1:CONV_W, :] * x
    for j in range(1, CONV_W):
        xc = xc + cw_ref[CONV_W - 1 - j:CONV_W - j, :] * ext_ref[SUBLANES - j:SUBLANES - j + tl, :]
    ext_ref[0:SUBLANES, :] = x[tl - SUBLANES:, :]

    row = lax.broadcasted_iota(jnp.int32, (tl, D_RNN), 0)
    a, b = _lru_coeffs(xc, wa_ref, ba_ref, wx_ref, bx_ref, lam_ref, (row == 0) & (l == 0))
    a, b = _scan8(a, b)
    a_ref[...] = a
    b_ref[...] = b

    def chunk(c, h):
        sl = pl.ds(pl.multiple_of(c * SUBLANES, SUBLANES), SUBLANES)
        hc = b_ref[sl, :] + a_ref[sl, :] * h
        b_ref[sl, :] = hc
        return hc[SUBLANES - 1:SUBLANES, :]

    h = lax.fori_loop(0, tl // SUBLANES, chunk, hc_ref[...])
    hc_ref[...] = h
    nh_ref[0] = h
    o_ref[...] = (b_ref[...] * gy_ref[...]).astype(BF16)


def _rnn_prompt(xr_all, gy_all, rnn_w, batch, seq):
    tl =_pick_tile(seq, (256, 128, 64, 32, 16, 8))
    nl = seq // tl
    tok = pl.BlockSpec((tl, D_RNN), lambda b, l: (b * nl + l, 0))
    return pl.pallas_call(
        functools.partial(_rnn_prompt_kernel, tl=tl),
        grid=(batch, nl),
        in_specs=[tok, tok] + [_const_spec(w.shape) for w in rnn_w],
        out_specs=(tok, pl.BlockSpec((1, 1, D_RNN), lambda b, l: (b, 0, 0))),
        out_shape=(jax.ShapeDtypeStruct((batch * seq, D_RNN), BF16), jax.ShapeDtypeStruct((batch, 1, D_RNN), F32)),
        scratch_shapes=[
            pltpu.VMEM((tl + SUBLANES, D_RNN), F32),
            pltpu.VMEM((tl, D_RNN), F32),
            pltpu.VMEM((tl, D_RNN), F32),
            pltpu.VMEM((1, D_RNN), F32),
        ],
        compiler_params=_params(("parallel", "arbitrary")),
        name="rnn_prompt",
    )(xr_all, gy_all, *rnn_w)


def _rnn_sample_kernel(xr_ref, gy_ref, hp_ref, h0_ref, cw_ref, cb_ref, wa_ref, ba_ref, wx_ref, bx_ref, lam_ref,
                       o_ref, nh_ref, *, seqs):
    rows = seqs * SUBLANES
    x = xr_ref[...]
    hp = hp_ref[...]
    r8 = lax.broadcasted_iota(jnp.int32, (rows, D_RNN), 0) % SUBLANES
    xc = cb_ref[...] + cw_ref[CONV_W - 1:CONV_W, :] * x
    for j in range(1, CONV_W):
        shifted = jnp.where(r8 >= j, pltpu.roll(x, j, 0), pltpu.roll(hp, rows - (SUBLANES - j), 0))
        xc = xc + cw_ref[CONV_W - 1 - j:CONV_W - j, :] * shifted
    a, b = _lru_coeffs(xc, wa_ref, ba_ref, wx_ref, bx_ref, lam_ref, None)
    a, b = _scan8(a, b)
    h0 = jnp.broadcast_to(h0_ref[...][:, None, :], (seqs, SUBLANES, D_RNN)).reshape(rows, D_RNN)
    h = b + a * h0
    last = jnp.where(r8 == SUBLANES - 1, h, 0.0).reshape(seqs, SUBLANES, D_RNN)
    nh_ref[...] = jnp.sum(last, axis=1)
    o_ref[...] = (h * gy_ref[...]).astype(BF16)


def _rnn_sample(xr_all, gy_all, hist_pad, h0, rnn_w, row0, dec_batch):
    seqs = _pick_tile(dec_batch, (16, 8))
    rows = seqs * SUBLANES
    assert row0 % rows == 0
    blk0 = row0 // rows
    tok = pl.BlockSpec((rows, D_RNN), lambda i: (blk0 + i, 0))
    return pl.pallas_call(
        functools.partial(_rnn_sample_kernel, seqs=seqs),
        grid=(dec_batch // seqs,),
        in_specs=[tok, tok, pl.BlockSpec((rows, D_RNN), lambda i: (i, 0)), pl.BlockSpec((seqs, D_RNN), lambda i: (i, 0))]
        + [_const_spec(w.shape) for w in rnn_w],
        out_specs=(pl.BlockSpec((rows, D_RNN), lambda i: (i, 0)), pl.BlockSpec((seqs, D_RNN), lambda i: (i, 0))),
        out_shape=(jax.ShapeDtypeStruct((dec_batch * SUBLANES, D_RNN), BF16),
                   jax.ShapeDtypeStruct((dec_batch, D_RNN), F32)),
        compiler_params=_params(("parallel",)),
        name="rnn_sample",
    )(xr_all, gy_all, hist_pad, h0, *rnn_w)


def _layer_norm(z, g, b):
    mu = jnp.mean(z, axis=-1, keepdims=True)
    zc = z - mu
    var = jnp.mean(zc * zc, axis=-1, keepdims=True)
    return zc * lax.rsqrt(var + LN_EPS) * g + b


def _first_index_of_max(vals, iota, axis, sentinel):
    mx = jnp.max(vals, axis=axis, keepdims=True)
    return mx, jnp.min(jnp.where(vals == mx, iota, sentinel), axis=axis, keepdims=True)


def _route(scores, bias):
    t = scores.shape[1]
    grp = scores + bias
    g3 = grp.reshape(N_GROUPS, GROUP_SIZE, t)
    e_in_g = lax.broadcasted_iota(jnp.int32, g3.shape, 1)
    m1, first = _first_index_of_max(g3, e_in_g, 1, GROUP_SIZE)
    m2 = jnp.max(jnp.where(e_in_g == first, -jnp.inf, g3), axis=1, keepdims=True)
    gscore = (m1 + m2).reshape(N_GROUPS, t)
    g_iota = lax.broadcasted_iota(jnp.int32, gscore.shape, 0)
    gmask = jnp.zeros(gscore.shape, jnp.bool_)
    for _ in range(TOPK_GROUPS):
        _, gi = _first_index_of_max(gscore, g_iota, 0, N_GROUPS)
        hit = g_iota == gi
        gmask = gmask | hit
        gscore = jnp.where(hit, -jnp.inf, gscore)
    masked = jnp.where(gmask[:, None, :], g3, -jnp.inf).reshape(N_EXPERTS, t)
    e_iota = lax.broadcasted_iota(jnp.int32, masked.shape, 0)
    idx, wts, hits = [], [], []
    for _ in range(TOP_K):
        _, ei = _first_index_of_max(masked, e_iota, 0, N_EXPERTS)
        hit = e_iota == ei
        idx.append(ei)
        hits.append(hit)
        wts.append(jnp.sum(jnp.where(hit, scores, 0.0), axis=0, keepdims=True))
        masked = jnp.where(hit, -jnp.inf, masked)
    idx = jnp.concatenate(idx, axis=0)
    w = jnp.concatenate(wts, axis=0)
    w = w / jnp.sum(w, axis=0, keepdims=True) * ROUTED_SCALE
    return idx, w, hits


def _merge_kernel(xp_ref, xs_ref, aop_ref, aos_ref, rop_ref, ros_ref, sga_ref, sgr_ref, woa_ref, wor_ref, wout_ref, g1_ref, b1_ref,
                  wrt_ref, rb_ref, wsg_ref, wsu_ref, wsd_ref,
                  x1w_ref, base_ref, idx_ref, wt_ref, rank_ref, cnt_ref, carry_ref, *, n_p, tm):
    i = pl.program_id(0)

    @pl.when(i == 0)
    def _():
        carry_ref[...] = jnp.zeros(carry_ref.shape, F32)

    is_prompt = i < n_p
    x = jnp.where(is_prompt, xp_ref[...], xs_ref[...])
    pa = jnp.dot(jnp.where(is_prompt, aop_ref[...], aos_ref[...]), woa_ref[...], preferred_element_type=F32)
    pr = jnp.dot(jnp.where(is_prompt, rop_ref[...], ros_ref[...]), wor_ref[...], preferred_element_type=F32)
    merged = sga_ref[...] * pa + sgr_ref[...] * pr
    z = ALPHA * x + jnp.dot(merged.astype(BF16), wout_ref[...], preferred_element_type=F32)
    x1 = _layer_norm(z, g1_ref[...], b1_ref[...])
    words = _pack_words(x1)
    for h in range(tm // SC_ROWS):
        for s in range(ROW_TILE):
            x1w_ref[(h * ROW_TILE + s) * SC_ROWS:(h * ROW_TILE + s + 1) * SC_ROWS, :] = (
                words[h * SC_ROWS:(h + 1) * SC_ROWS, s * LANES:(s + 1) * LANES])
    x1b = x1.astype(BF16)
    u = jax.nn.silu(jnp.dot(x1b, wsg_ref[...], preferred_element_type=F32)) * jnp.dot(x1b, wsu_ref[...], preferred_element_type=F32)
    shared = jnp.dot(u.astype(BF16), wsd_ref[...], preferred_element_type=F32)
    base_ref[...] = ALPHA * x1 + shared
    logits = lax.dot_general(wrt_ref[...], x1, (((1,), (1,)), ((), ())), preferred_element_type=F32)
    idx, w, hits = _route(jax.nn.sigmoid(logits), rb_ref[...])
    idx_ref[...] = idx
    wt_ref[...] = w

    chosen = functools.reduce(jnp.logical_or, hits)
    chosen_f = jnp.where(chosen, 1.0, 0.0)
    earlier = (lax.broadcasted_iota(jnp.int32, (tm, tm), 0) < lax.broadcasted_iota(jnp.int32, (tm, tm), 1))
    prefix = jnp.dot(chosen_f.astype(BF16), jnp.where(earlier, 1.0, 0.0).astype(BF16), preferred_element_type=F32)
    before = prefix + carry_ref[...]
    ranks = [jnp.sum(jnp.where(hit, before, 0.0), axis=0, keepdims=True) for hit in hits]
    rank_ref[...] = jnp.concatenate(ranks, axis=0).astype(jnp.int32)
    carry_ref[...] = carry_ref[...] + jnp.sum(chosen_f, axis=1, keepdims=True)
    cnt_ref[...] = carry_ref[...]


def _merge(x_p, x_s, attn_p, attn_s, rnn_p, rnn_s, sga, sgr, weights):
    t_p, t_s = x_p.shape[0], x_s.shape[0]
    t = t_p + t_s
    tm = _pick_tile(math.gcd(t_p, t_s), (512, 256, 128))
    row = lambda width: pl.BlockSpec((tm, width), lambda i: (i, 0))
    col = pl.BlockSpec((TOP_K, tm), lambda i: (0, i))

    return pl.pallas_call(
        functools.partial(_merge_kernel, n_p=t_p // tm, tm=tm),
        grid=(t // tm,),
        in_specs=[*_two_source_specs(t_p, t_s, tm, D_MODEL), *_two_source_specs(t_p, t_s, tm, D_ATTN),
                  *_two_source_specs(t_p, t_s, tm, D_RNN), row(D_MODEL), row(D_MODEL)]
        + [_const_spec(w.shape) for w in weights],
        out_specs=(pl.BlockSpec((tm * ROW_TILE, LANES), lambda i: (i, 0)), row(D_MODEL), col, col, col,
                   _const_spec((N_EXPERTS, 1))),
        out_shape=(
            jax.ShapeDtypeStruct((t * ROW_TILE, LANES), WORD),
            jax.ShapeDtypeStruct((t, D_MODEL), F32),
            jax.ShapeDtypeStruct((TOP_K, t), jnp.int32),
            jax.ShapeDtypeStruct((TOP_K, t), F32),
            jax.ShapeDtypeStruct((TOP_K, t), jnp.int32),
            jax.ShapeDtypeStruct((N_EXPERTS, 1), F32),
        ),
        scratch_shapes=[pltpu.VMEM((N_EXPERTS, 1), F32)],
        compiler_params=_params(("arbitrary",)),
        name="merge_ln1_route",
    )(x_p, x_s, attn_p, attn_s, rnn_p, rnn_s, sga, sgr, *weights)


def _expert_chunk_rows(n_assign):
    return max(EXPERT_ROWS, -(-(n_assign // N_EXPERTS * 9 // 8) // EXPERT_ROWS) * EXPERT_ROWS)


def _expert_layout(counts, n_assign, chunk_rows):
    counts = counts.reshape(N_EXPERTS).astype(jnp.int32)
    padded = (counts + ROW_PAD - 1) // ROW_PAD * ROW_PAD
    pend = jnp.cumsum(padded)
    pstart = pend - padded
    rows_alloc = (n_assign + N_EXPERTS * (ROW_PAD - 1)) // ROW_PAD * ROW_PAD + chunk_rows
    n_chunks = jnp.maximum((counts + chunk_rows - 1) // chunk_rows, 1)
    part = CHUNK_PART
    written = jnp.maximum((counts + part - 1) // part, 1) * part
    ffn_end = jnp.max(pstart + written).reshape(1)
    return pstart, pstart + counts, pend, n_chunks, counts, ffn_end, rows_alloc


def _dest_kernel(idx_ref, rank_ref, pstart_ref, dest_ref, word_rows_ref):
    e_iota = lax.broadcasted_iota(jnp.int32, (N_EXPERTS, idx_ref.shape[1]), 0)
    starts = [jnp.sum(jnp.where(e_iota == idx_ref[k:k + 1, :], pstart_ref[...], 0), axis=0, keepdims=True)
              for k in range(TOP_K)]
    dest = jnp.concatenate(starts, axis=0) + rank_ref[...]
    dest_ref[...] = dest
    tm = dest.shape[1]
    word_rows_ref[0] = jnp.concatenate(
        [dest[k:k + 1, h * SC_ROWS:(h + 1) * SC_ROWS] * ROW_TILE + s
         for h in range(tm // SC_ROWS) for s in range(ROW_TILE) for k in range(TOP_K)], axis=0)


WORD_ROW_VECS = ROW_TILE * TOP_K


def _dest_rows(idx_t, rank_t, pstart):
    t = idx_t.shape[1]
    tm = _pick_tile(t, (1024, 512, 256, 128))
    col = pl.BlockSpec((TOP_K, tm), lambda i: (0, i))
    vecs = tm // SC_ROWS * WORD_ROW_VECS
    dest, word_rows = pl.pallas_call(
        _dest_kernel,
        grid=(t // tm,),
        in_specs=[col, col, _const_spec((N_EXPERTS, 1))],
        out_specs=(col, pl.BlockSpec((1, vecs, SC_ROWS), lambda i: (i, 0, 0))),
        out_shape=(jax.ShapeDtypeStruct((TOP_K, t), jnp.int32),
                   jax.ShapeDtypeStruct((t // tm, vecs, SC_ROWS), jnp.int32)),
        compiler_params=_params(("parallel",)),
        name="dest_rows",
    )(idx_t, rank_t, pstart.reshape(N_EXPERTS, 1))
    return dest, word_rows.reshape(-1, SC_ROWS)


def _token_rows(r, n=1):
    return pl.ds(pl.multiple_of(r * ROW_TILE, ROW_TILE), n * ROW_TILE)


def _zero_row_groups(zero_ref, dst_ref, sem, first_group, n_groups):
    def start(g, c):
        pltpu.make_async_copy(zero_ref, dst_ref.at[_token_rows(g * ROW_PAD, ROW_PAD)], sem).start()
        return c

    lax.fori_loop(first_group, n_groups, start, 0)
    return n_groups - first_group


def _wait_zero_copies(zero_ref, dst_ref, sem, n):
    def wait(_, c):
        pltpu.make_async_copy(zero_ref, dst_ref.at[_token_rows(0, ROW_PAD)], sem).wait()
        return c

    lax.fori_loop(0, n, wait, 0)


SC_ROWS = 128


def _sc_workers():
    info = plsc.get_sparse_core_info()
    mesh = plsc.VectorSubcoreMesh(core_axis_name="c", subcore_axis_name="s")
    worker = lambda: lax.axis_index("s") * info.num_cores + lax.axis_index("c")
    return mesh, info.num_cores * info.num_subcores, worker


def _sc_scatter_rows(src, rows, n_out):
    mesh, n_workers, worker = _sc_workers()
    n_units = src.shape[0] // SC_ROWS
    assert n_units % n_workers == 0 and rows.shape == (n_units * TOP_K, SC_ROWS)
    per_worker = n_units // n_workers

    unit_bufs = [pltpu.VMEM((TOP_K, SC_ROWS), jnp.int32), pltpu.VMEM((SC_ROWS, LANES), src.dtype),
                 pltpu.SemaphoreType.DMA]

    @functools.partial(
        pl.kernel, mesh=mesh,
        out_type=jax.ShapeDtypeStruct((n_out, LANES), src.dtype),
        scratch_types=unit_bufs + unit_bufs + [pltpu.SemaphoreType.DMA],
    )
    def scatter(src_hbm, rows_hbm, out_hbm, rows_a, data_a, load_a, rows_b, data_b, load_b, sem):
        first = worker() * per_worker
        bufs = ((rows_a, data_a, load_a), (rows_b, data_b, load_b))

        def loads(u, buf):
            rows_v, data_v, load_sem = buf
            return (pltpu.make_async_copy(rows_hbm.at[pl.ds(u * TOP_K, TOP_K)], rows_v, load_sem),
                    pltpu.make_async_copy(src_hbm.at[pl.ds(u * SC_ROWS, SC_ROWS)], data_v, load_sem))

        def unit(u, buf, other, has_next):
            for cp in loads(u, buf):
                cp.wait()

            @pl.when(has_next)
            def _():
                for cp in loads(u + 1, other):
                    cp.start()

            rows_v, data_v, _ = buf
            copies = [pltpu.async_copy(data_v, out_hbm.at[rows_v.at[k]], sem) for k in range(TOP_K)]
            for cp in copies:
                cp.wait()

        for cp in loads(first, bufs[0]):
            cp.start()

        @pl.loop(0, per_worker // 2)
        def _(pair):
            u = first + 2 * pair
            unit(u, bufs[0], bufs[1], True)
            unit(u + 1, bufs[1], bufs[0], 2 * pair + 2 < per_worker)

        if per_worker % 2:
            unit(first + per_worker - 1, bufs[0], bufs[1], False)

    return scatter(src, rows)


def _zero_padding_kernel(lo_ref, hi_ref, tail_ref, xs_in_ref, xs_ref, zero_ref, sem, *, rows_alloc):
    del xs_in_ref
    zero_ref[...] = jnp.zeros(zero_ref.shape, WORD)
    row_copy = lambda r: pltpu.make_async_copy(zero_ref.at[pl.ds(0, ROW_TILE)], xs_ref.at[_token_rows(r)], sem)

    def expert(e, n):
        def row(r, c):
            row_copy(r).start()
            return c

        lax.fori_loop(lo_ref[e], hi_ref[e], row, 0)
        return n + hi_ref[e] - lo_ref[e]

    def wait(_, c):
        row_copy(0).wait()
        return c

    lax.fori_loop(0, lax.fori_loop(0, N_EXPERTS, expert, 0), wait, 0)
    n = _zero_row_groups(zero_ref, xs_ref, sem, tail_ref[0] // ROW_PAD, rows_alloc // ROW_PAD)
    _wait_zero_copies(zero_ref, xs_ref, sem, n)


def _zero_padding(xs, pad_lo, pad_hi, total, rows_alloc):
    grid_spec = pltpu.PrefetchScalarGridSpec(
        num_scalar_prefetch=3,
        grid=(1,),
        in_specs=[pl.BlockSpec(memory_space=pl.ANY)],
        out_specs=pl.BlockSpec(memory_space=pl.ANY),
        scratch_shapes=[pltpu.VMEM((ROW_PAD * ROW_TILE, LANES), WORD), pltpu.SemaphoreType.DMA],
    )
    return pl.pallas_call(
        functools.partial(_zero_padding_kernel, rows_alloc=rows_alloc),
        grid_spec=grid_spec,
        out_shape=jax.ShapeDtypeStruct(xs.shape, xs.dtype),
        input_output_aliases={3: 0},
        compiler_params=_params(("arbitrary",)),
        name="zero_padding",
    )(pad_lo, pad_hi, total, xs)


def _expert_kernel(pstart_ref, nch_ref, rows_ref, end_ref, wg_ref, wu_ref, wd_ref, xs_ref, o_ref,
                   xbuf_ref, obuf_ref, wgb_ref, wub_ref, wdb_ref, zero_ref, done_ref, in_sem, out_sem,
                   *, rows_alloc, chunk_rows):
    e = pl.program_id(0)
    n_e = pl.num_programs(0)
    start = pstart_ref[e]
    nch = nch_ref[e]
    buf_rows = chunk_rows * ROW_TILE
    part = CHUNK_PART
    chunk_parts = chunk_rows // part
    part_rows = part * ROW_TILE

    class _ChunkCopy:
        def __init__(self, whole, parts, n_parts):
            self.whole = whole
            self.parts = parts
            self.n_parts = n_parts

        def _each(self, act):
            pl.when(self.n_parts == chunk_parts)(functools.partial(act, self.whole))
            for p, cp in enumerate(self.parts):
                pl.when((p < self.n_parts) & (self.n_parts < chunk_parts))(functools.partial(act, cp))

        def start(self):
            self._each(lambda cp: cp.start())

        def wait(self):
            self._each(lambda cp: cp.wait())

    def parts_of(rows_left):
        return jnp.clip((rows_left + part - 1) // part, 1, chunk_parts)

    def in_copy(row, slot, n_parts):
        copy = lambda p, n: pltpu.make_async_copy(
            xs_ref.at[_token_rows(row + p * part, n * part)],
            xbuf_ref.at[pl.ds(slot * buf_rows + p * part_rows, n * part_rows)], in_sem.at[slot])
        return _ChunkCopy(copy(0, chunk_parts), [copy(p, 1) for p in range(chunk_parts - 1)], n_parts)

    def out_copy(row, slot, n_parts):
        copy = lambda p, n: pltpu.make_async_copy(
            obuf_ref.at[pl.ds(slot * buf_rows + p * part_rows, n * part_rows)],
            o_ref.at[_token_rows(row + p * part, n * part)], out_sem)
        return _ChunkCopy(copy(0, chunk_parts), [copy(p, 1) for p in range(chunk_parts - 1)], n_parts)

    @pl.when(e == 0)
    def _():
        done_ref[0] = 0
        xbuf_ref[...] = jnp.zeros(xbuf_ref.shape, WORD)
        obuf_ref[...] = jnp.zeros(obuf_ref.shape, WORD)
        in_copy(start, 0, parts_of(rows_ref[0])).start()

    wgb_ref[...] = wg_ref[...].astype(BF16)
    wub_ref[...] = wu_ref[...].astype(BF16)
    wdb_ref[...] = wd_ref[...].astype(BF16)
    done = done_ref[0]

    def chunk(c, carry):
        g = done + c
        slot = g % 2
        row = start + c * chunk_rows
        rows_left = rows_ref[e] - c * chunk_rows
        n_parts = parts_of(rows_left)
        in_copy(row, slot, n_parts).wait()
        last = c + 1 == nch
        next_e = jnp.minimum(e + 1, n_e - 1)
        next_row = jnp.where(last, pstart_ref[next_e], row + chunk_rows)
        next_left = jnp.where(last, rows_ref[next_e], rows_left - chunk_rows)

        @pl.when(jnp.logical_not(last & (e == n_e - 1)))
        def _():
            in_copy(next_row, 1 - slot, parts_of(next_left)).start()

        base = pl.multiple_of(slot * buf_rows, buf_rows)
        def sub_block(h):
            xb = jnp.concatenate(_load_token_rows(xbuf_ref, EXPERT_ROWS, row0=h * EXPERT_ROWS, base=base),
                                 axis=1).astype(BF16)
            gate = jnp.dot(xb, wgb_ref[...], preferred_element_type=F32)
            up = jnp.dot(xb, wub_ref[...], preferred_element_type=F32)
            act = (jax.nn.silu(gate) * up).astype(BF16)
            _store_token_rows(obuf_ref, jnp.dot(act, wdb_ref[...], preferred_element_type=F32),
                              EXPERT_ROWS, row0=h * EXPERT_ROWS, base=base)

        n_sub = chunk_rows // EXPERT_ROWS
        needed = jnp.clip((rows_left + EXPERT_ROWS - 1) // EXPERT_ROWS, 1, n_sub)
        for count in range(1, n_sub + 1):
            @pl.when(needed == count)
            def _():
                for h in range(count):
                    sub_block(h)

        @pl.when(g > 0)
        def _():
            out_copy(0, 0, done_ref[1]).wait()

        out_copy(row, slot, n_parts).start()
        done_ref[1] = n_parts
        return carry

    lax.fori_loop(0, nch, chunk, 0)
    done_ref[0] = done + nch

    @pl.when(e == n_e - 1)
    def _():
        out_copy(0, 0, done_ref[1]).wait()
        zero_ref[...] = jnp.zeros(zero_ref.shape, WORD)
        n = _zero_row_groups(zero_ref, o_ref, out_sem, end_ref[0] // ROW_PAD, rows_alloc // ROW_PAD)
        _wait_zero_copies(zero_ref, o_ref, out_sem, n)


def _expert_ffn(xs, pstart, n_chunks, rows, ffn_end, rows_alloc, chunk_rows, w_e_gate, w_e_up, w_e_down):
    weight = lambda shape: pl.BlockSpec((None, *shape), lambda e, *_: (e, 0, 0))
    grid_spec = pltpu.PrefetchScalarGridSpec(
        num_scalar_prefetch=4,
        grid=(N_EXPERTS,),
        in_specs=[weight((D_MODEL, D_EXPERT)), weight((D_MODEL, D_EXPERT)), weight((D_EXPERT, D_MODEL)),
                  pl.BlockSpec(memory_space=pl.ANY)],
        out_specs=pl.BlockSpec(memory_space=pl.ANY),
        scratch_shapes=[
            pltpu.VMEM((2 * chunk_rows * ROW_TILE, LANES), WORD),
            pltpu.VMEM((2 * chunk_rows * ROW_TILE, LANES), WORD),
            pltpu.VMEM((D_MODEL, D_EXPERT), BF16), pltpu.VMEM((D_MODEL, D_EXPERT), BF16),
            pltpu.VMEM((D_EXPERT, D_MODEL), BF16),
            pltpu.VMEM((ROW_PAD * ROW_TILE, LANES), WORD),
            pltpu.SMEM((2,), jnp.int32),
            pltpu.SemaphoreType.DMA((2,)), pltpu.SemaphoreType.DMA,
        ],
    )
    return pl.pallas_call(
        functools.partial(_expert_kernel, rows_alloc=rows_alloc, chunk_rows=chunk_rows),
        grid_spec=grid_spec,
        out_shape=jax.ShapeDtypeStruct((rows_alloc * ROW_TILE, LANES), WORD),
        compiler_params=_params(("arbitrary",)),
        name="expert_ffn",
    )(pstart, n_chunks, rows, ffn_end, w_e_gate, w_e_up, w_e_down, xs)


def _combine_head_kernel(dest_ref, dest_next_ref, w_ref, base_ref, g_ref, b_ref, outs_ref, y_ref, buf_ref, sem, *, tm):
    i = pl.program_id(0)
    slot_rows = TOP_K * tm

    def gather(d_ref, slot):
        def issue(t, c):
            for k in range(TOP_K):
                pltpu.make_async_copy(outs_ref.at[_token_rows(d_ref[k, t])],
                                      buf_ref.at[_token_rows(slot * slot_rows + k * tm + t)], sem.at[slot]).start()
            return c

        lax.fori_loop(0, tm, issue, 0)

    @pl.when(i == 0)
    def _():
        gather(dest_ref, 0)

    @pl.when(i + 1 < pl.num_programs(0))
    def _():
        gather(dest_next_ref, (i + 1) % 2)

    slot = i % 2
    for k in range(TOP_K):
        pltpu.make_async_copy(outs_ref.at[_token_rows(0, tm)], buf_ref.at[_token_rows(slot * slot_rows + k * tm, tm)],
                              sem.at[slot]).wait()

    w = w_ref[...]
    base = pl.multiple_of(slot * slot_rows * ROW_TILE, ROW_TILE)
    halves = [None, None]
    for k in range(TOP_K):
        wk = w[:, k:k + 1]
        for j, rows in enumerate(_load_token_rows(buf_ref, tm, row0=k * tm, base=base)):
            halves[j] = wk * rows if halves[j] is None else halves[j] + wk * rows
    y_ref[...] = _layer_norm(base_ref[...] + jnp.concatenate(halves, axis=1), g_ref[...], b_ref[...])


def _combine_head(out_sorted, dest, w_tok, base, g2, b2, tm, n_head):
    row = pl.BlockSpec((tm, D_MODEL), lambda i: (i, 0))
    return pl.pallas_call(
        functools.partial(_combine_head_kernel, tm=tm),
        grid=(n_head,),
        in_specs=[
            pl.BlockSpec((TOP_K, tm), lambda i: (0, i), memory_space=pltpu.SMEM),
            pl.BlockSpec((TOP_K, tm), lambda i: (0, jnp.minimum(i + 1, n_head - 1)), memory_space=pltpu.SMEM),
            pl.BlockSpec((tm, TOP_K), lambda i: (i, 0)),
            row, _const_spec((1, D_MODEL)), _const_spec((1, D_MODEL)),
            pl.BlockSpec(memory_space=pl.ANY),
        ],
        out_specs=row,
        out_shape=jax.ShapeDtypeStruct((n_head * tm, D_MODEL), F32),
        scratch_shapes=[pltpu.VMEM((2 * TOP_K * tm * ROW_TILE, LANES), WORD), pltpu.SemaphoreType.DMA((2,))],
        compiler_params=_params(("arbitrary",)),
        name="combine_head",
    )(dest, dest, w_tok, base, g2, b2, out_sorted)


def _sc_gather_rows(table, rows):
    mesh, n_workers, worker = _sc_workers()
    m = rows.shape[0]
    per_worker = m // n_workers
    assert m % n_workers == 0 and per_worker % SC_ROWS == 0
    in_flight = _pick_tile(per_worker // SC_ROWS, (2, 1))
    step = SC_ROWS * in_flight
    n_stages = per_worker // step
    stage_bufs = [pltpu.VMEM((step,), jnp.int32), pltpu.VMEM((step, LANES), table.dtype),
                  pltpu.SemaphoreType.DMA, pltpu.SemaphoreType.DMA]

    @functools.partial(
        pl.kernel, mesh=mesh,
        out_type=jax.ShapeDtypeStruct((m, LANES), table.dtype),
        scratch_types=stage_bufs + stage_bufs,
    )
    def gather(table_hbm, rows_hbm, out_hbm, rows_a, data_a, gsem_a, ssem_a, rows_b, data_b, gsem_b, ssem_b):
        first = worker() * per_worker
        bufs = ((rows_a, data_a, gsem_a, ssem_a), (rows_b, data_b, gsem_b, ssem_b))

        def gathers(buf):
            rows_v, data_v, gsem, _ = buf
            return [pltpu.make_async_copy(table_hbm.at[rows_v.at[pl.ds(j * SC_ROWS, SC_ROWS)]],
                                          data_v.at[pl.ds(j * SC_ROWS, SC_ROWS)], gsem) for j in range(in_flight)]

        def store(off, buf):
            return pltpu.make_async_copy(buf[1], out_hbm.at[pl.ds(off, step)], buf[3])

        def stage(it, buf, other, has_next, other_stored):
            off = first + it * step

            @pl.when(has_next)
            def _():
                @pl.when(other_stored)
                def _():
                    store(off, other).wait()

                pltpu.sync_copy(rows_hbm.at[pl.ds(off + step, step)], other[0])
                for cp in gathers(other):
                    cp.start()

            for cp in gathers(buf):
                cp.wait()
            store(off, buf).start()

        pltpu.sync_copy(rows_hbm.at[pl.ds(first, step)], rows_a)
        for cp in gathers(bufs[0]):
            cp.start()

        @pl.loop(0, n_stages // 2)
        def _(pair):
            it = 2 * pair
            stage(it, bufs[0], bufs[1], True, pair > 0)
            stage(it + 1, bufs[1], bufs[0], it + 2 < n_stages, True)

        if n_stages % 2:
            stage(n_stages - 1, bufs[0], bufs[1], False, False)
        for back in range(min(2, n_stages)):
            store(first, bufs[(n_stages - 1 - back) % 2]).wait()

    return gather(table, rows)


def _combine_tail_kernel(yh_ref, rows_ref, w_ref, base_ref, g_ref, b_ref, yp_ref, ys_ref, *, tm, n_head, n_p):
    i = pl.program_id(0)

    @pl.when(i < n_head)
    def _():
        yp_ref[...] = yh_ref[...]

    @pl.when(i >= n_head)
    def _():
        w = w_ref[...]
        lo = [None] * ROW_TILE
        hi = [None] * ROW_TILE
        for k in range(TOP_K):
            wk = w[:, k:k + 1]
            for s in range(ROW_TILE):
                vec = lambda h: ((h * ROW_TILE + s) * TOP_K + k) * SC_ROWS
                words = jnp.concatenate([rows_ref[vec(h):vec(h) + SC_ROWS, :] for h in range(tm // SC_ROWS)], axis=0)
                lo_s, hi_s = _unpack_words(words)
                lo[s] = wk * lo_s if lo[s] is None else lo[s] + wk * lo_s
                hi[s] = wk * hi_s if hi[s] is None else hi[s] + wk * hi_s
        y = _layer_norm(base_ref[...] + jnp.concatenate(lo + hi, axis=1), g_ref[...], b_ref[...])

        @pl.when(i < n_p)
        def _():
            yp_ref[...] = y

        @pl.when(i >= n_p)
        def _():
            ys_ref[...] = y


def _combine_tail(y_head, gathered, w_tok, base, g2, b2, t_p, t_s, tm, n_head):
    n_p = t_p // tm
    assert n_head <= n_p
    n_tiles = (t_p + t_s) // tm
    blk = TOP_K * ROW_TILE * tm
    row = pl.BlockSpec((tm, D_MODEL), lambda i: (i, 0))
    out_p, out_s = _two_source_specs(t_p, t_s, tm, D_MODEL)
    return pl.pallas_call(
        functools.partial(_combine_tail_kernel, tm=tm, n_head=n_head, n_p=n_p),
        grid=(n_tiles,),
        in_specs=[
            pl.BlockSpec((tm, D_MODEL), lambda i: (jnp.minimum(i, n_head - 1), 0)),
            pl.BlockSpec((blk, LANES), lambda i: (jnp.maximum(i - n_head, 0), 0)),
            pl.BlockSpec((tm, TOP_K), lambda i: (i, 0)),
            row, _const_spec((1, D_MODEL)), _const_spec((1, D_MODEL)),
        ],
        out_specs=(out_p, out_s),
        out_shape=(jax.ShapeDtypeStruct((t_p, D_MODEL), F32), jax.ShapeDtypeStruct((t_s, D_MODEL), F32)),
        compiler_params=_params(("arbitrary",)),
        name="combine_tail",
    )(y_head, gathered, w_tok, base, g2, b2)


def kernel(x_prompt, x_sample, cache_k, cache_v, state_conv, state_rnn, w_in, conv_w, conv_b, w_gate_a, b_gate_a, w_gate_x, b_gate_x, lru_lambda, rel_bias, sinks, w_o_attn, w_o_rnn, w_out, ln1_g, ln1_b, w_router, router_bias, w_e_gate, w_e_up, w_e_down, w_s_gate, w_s_up, w_s_down, ln2_g, ln2_b):
    assert w_in.shape[0] == DEPTH == 1
    batch, seq, _ = x_prompt.shape
    dec_batch, s_len, _ = x_sample.shape
    w_cache = cache_k.shape[2]
    assert s_len == SUBLANES and seq % WINDOW == 0 and w_cache == WINDOW
    t_p = batch * seq
    t_s = dec_batch * s_len
    vec = lambda a: a[0].reshape(1, -1).astype(F32)

    x_p = x_prompt.reshape(t_p, D_MODEL)
    x_s = x_sample.reshape(t_s, D_MODEL)
    q, k, v, xr, gy, sga, sgr = _inproj(x_p, x_s, w_in[0].astype(BF16))

    qi = jnp.arange(WINDOW)
    dist = qi[:, None] + WINDOW - jnp.arange(2 * WINDOW)[None, :]
    band = (dist >= 0) & (dist <= WINDOW)
    has_prev = jnp.arange(2 * WINDOW)[None, :] >= WINDOW
    bias_p = _bias_table(rel_bias, dist)
    bias_p = jnp.stack([jnp.where(band & has_prev, bias_p, NEG_INF), jnp.where(band, bias_p, NEG_INF)])
    attn_p = _attn_prompt(q, k, v, bias_p, sinks[0], batch, seq)
    attn_s, k_s, v_s = _attn_sample(
        q, k, v, cache_k[0].reshape(dec_batch, w_cache, D_KV), cache_v[0].reshape(dec_batch, w_cache, D_KV),
        rel_bias, sinks[0], t_p, dec_batch, s_len)

    rnn_w = (conv_w[0], vec(conv_b), w_gate_a[0].astype(BF16), vec(b_gate_a), w_gate_x[0].astype(BF16),
             vec(b_gate_x), vec(lru_lambda))
    rnn_p, h_p = _rnn_prompt(xr, gy, rnn_w, batch, seq)
    hist_pad = jnp.pad(state_conv[0], ((0, 0), (SUBLANES - (CONV_W - 1), 0), (0, 0))).reshape(t_s, D_RNN)
    rnn_s, h_s = _rnn_sample(xr, gy, hist_pad, state_rnn[0], rnn_w, t_p, dec_batch)

    merge_w = (w_o_attn[0].astype(BF16), w_o_rnn[0].astype(BF16), w_out[0].astype(BF16), vec(ln1_g), vec(ln1_b),
               w_router[0].T, router_bias[0].reshape(N_EXPERTS, 1), w_s_gate[0].astype(BF16),
               w_s_up[0].astype(BF16), w_s_down[0].astype(BF16))
    x1w, base, idx_t, wt_t, rank_t, counts = _merge(x_p, x_s, attn_p, attn_s, rnn_p, rnn_s, sga, sgr, merge_w)

    n_assign = (t_p + t_s) * TOP_K
    chunk_rows = _expert_chunk_rows(n_assign)
    pstart, pad_lo, pad_hi, n_chunks, rows, ffn_end, rows_alloc = _expert_layout(counts, n_assign, chunk_rows)
    tm = _pick_tile(math.gcd(t_p, t_s), (256, 128))
    dest, word_rows = _dest_rows(idx_t, rank_t, pstart)
    xs = _sc_scatter_rows(x1w, word_rows, rows_alloc * ROW_TILE)
    xs = _zero_padding(xs, pad_lo, pad_hi, pad_hi[N_EXPERTS - 1:], rows_alloc)
    out_sorted = _expert_ffn(xs, pstart, n_chunks, rows, ffn_end, rows_alloc, chunk_rows,
                             w_e_gate[0], w_e_up[0], w_e_down[0])
    n_tiles = (t_p + t_s) // tm
    n_head = min(t_p // tm, n_tiles * 2 // 7)
    w_tok = wt_t.T
    g2, b2 = vec(ln2_g), vec(ln2_b)
    y_head = _combine_head(out_sorted, dest, w_tok, base, g2, b2, tm, n_head)
    gathered = _sc_gather_rows(out_sorted, word_rows[n_head * (tm // SC_ROWS) * WORD_ROW_VECS:].reshape(-1))
    y_p, y_s = _combine_tail(y_head, gathered, w_tok, base, g2, b2, t_p, t_s, tm, n_head)
    y_p = y_p.reshape(batch, seq, D_MODEL)
    y_s = y_s.reshape(dec_batch, s_len, D_MODEL)
    kv5 = lambda a, b: a.reshape(1, b, WINDOW, N_KV_HEADS, HEAD_DIM)
    tail = lambda a, n: jnp.stack([lax.slice_in_dim(a, (b + 1) * seq - n, (b + 1) * seq) for b in range(batch)])
    k_p = kv5(tail(k, WINDOW), batch)
    v_p = kv5(tail(v, WINDOW), batch)
    conv_p = tail(xr, CONV_W - 1)[None]
    conv_s = xr[t_p:].reshape(dec_batch, s_len, D_RNN)[:, s_len - (CONV_W - 1):][None]
    return (y_p, y_s, k_p, v_p, conv_p, h_p.reshape(1, batch, D_RNN),
            kv5(k_s, dec_batch), kv5(v_s, dec_batch), conv_s, h_s.reshape(1, dec_batch, D_RNN))
```

```python
import functools
import math

import jax
import jax.numpy as jnp
from jax import lax
from jax.experimental import pallas as pl
from jax.experimental.pallas import tpu as pltpu
from jax.experimental.pallas import tpu_sc as plsc

F32 = jnp.float32
BF16 = jnp.bfloat16
WORD = jnp.int32

D_MODEL = 1024
N_HEADS = 8
N_KV_HEADS = 2
HEAD_DIM = 64
GROUP = N_HEADS // N_KV_HEADS
WINDOW = 128
D_ATTN = N_HEADS * HEAD_DIM
D_KV = N_KV_HEADS * HEAD_DIM
N_BUCKETS = 32
MAX_DISTANCE = 128
D_RNN = D_MODEL
RNN_BLOCK = 256
N_RNN_BLOCKS = D_RNN // RNN_BLOCK
CONV_W = 4
RG_C = 8.0
N_EXPERTS = 256
TOP_K = 8
N_GROUPS = 8
GROUP_SIZE = N_EXPERTS // N_GROUPS
TOPK_GROUPS = 4
D_EXPERT = D_MODEL // 4
ROUTED_SCALE = 2.5
LN_EPS = 1e-5
DEPTH = 1
ALPHA = (2 * DEPTH) ** 0.25
NEG_INF = -1e30
SM_SCALE = HEAD_DIM ** -0.5
assert math.frexp(SM_SCALE)[0] == 0.5

O_Q = 0
O_K = D_ATTN
O_V = O_K + D_KV
O_XR = O_V + D_KV
O_YR = O_XR + D_RNN
O_GA = O_YR + D_RNN
O_GR = O_GA + D_MODEL
D_IN = O_GR + D_MODEL

SUBLANES = 8
VMEM_LIMIT_BYTES = 56 * 1024 * 1024
EXPERT_ROWS = 256
CHUNK_PART = 128
ROW_PAD = SUBLANES


def _params(sem):
    return pltpu.CompilerParams(dimension_semantics=sem, vmem_limit_bytes=VMEM_LIMIT_BYTES)


def _pick_tile(n, candidates):
    for c in candidates:
        if n % c == 0:
            return c
    raise ValueError(f"no tile for {n}")


def _const_spec(shape):
    nd = len(shape)
    return pl.BlockSpec(shape, lambda *_: (0,) * nd)


LANES = 128
ROW_WORDS = D_MODEL // 2
ROW_TILE = ROW_WORDS // LANES
HIGH_HALF = -65536


def _pack_words(mat):
    as_bits = lambda v: pltpu.bitcast(v.astype(BF16).astype(F32), WORD)
    return (as_bits(mat[:, ROW_WORDS:]) & HIGH_HALF) | lax.shift_right_logical(as_bits(mat[:, :ROW_WORDS]), 16)


def _unpack_words(words):
    return pltpu.bitcast(words << 16, F32), pltpu.bitcast(words & HIGH_HALF, F32)


def _store_token_rows(ref, mat, n, row0=0, base=0):
    words = _pack_words(mat)
    for s in range(ROW_TILE):
        ref[pl.ds(base + row0 * ROW_TILE + s, n, stride=ROW_TILE), :] = words[:, s * LANES:(s + 1) * LANES]


def _load_token_rows(ref, n, row0=0, base=0):
    words = jnp.concatenate(
        [ref[pl.ds(base + row0 * ROW_TILE + s, n, stride=ROW_TILE), :] for s in range(ROW_TILE)], axis=1)
    return _unpack_words(words)


def _two_source_specs(t_p, t_s, tm, width):
    n_p = t_p // tm
    assert t_p % tm == 0 and t_s % tm == 0
    return (pl.BlockSpec((tm, width), lambda i, *_: (jnp.minimum(i, n_p - 1), 0)),
            pl.BlockSpec((tm, width), lambda i, *_: (jnp.maximum(i - n_p, 0), 0)))


def _inproj_kernel(xp_ref, xs_ref, w_ref, q_ref, k_ref, v_ref, xr_ref, gy_ref, sga_ref, sgr_ref, *, n_p):
    x = jnp.where(pl.program_id(0) < n_p, xp_ref[...], xs_ref[...]).astype(BF16)

    def seg(lo, hi):
        return jnp.dot(x, w_ref[:, lo:hi], preferred_element_type=F32)

    q_ref[...] = (seg(O_Q, O_K) * SM_SCALE).astype(BF16)
    k_ref[...] = seg(O_K, O_V)
    v_ref[...] = seg(O_V, O_XR)
    xr_ref[...] = seg(O_XR, O_YR)
    gy_ref[...] = jax.nn.gelu(seg(O_YR, O_GA)).astype(BF16)
    sga_ref[...] = jax.nn.sigmoid(seg(O_GA, O_GR)).astype(BF16)
    sgr_ref[...] = jax.nn.sigmoid(seg(O_GR, D_IN)).astype(BF16)


def _inproj(x_p, x_s, w_in_bf16):
    t_p, t_s = x_p.shape[0], x_s.shape[0]
    t = t_p + t_s
    tm = _pick_tile(math.gcd(t_p, t_s), (256, 128, 64, 32, 16, 8))
    row = lambda width: pl.BlockSpec((tm, width), lambda i: (i, 0))
    out_shape = (
        jax.ShapeDtypeStruct((t, D_ATTN), BF16),
        jax.ShapeDtypeStruct((t, D_KV), F32),
        jax.ShapeDtypeStruct((t, D_KV), F32),
        jax.ShapeDtypeStruct((t, D_RNN), F32),
        jax.ShapeDtypeStruct((t, D_RNN), BF16),
        jax.ShapeDtypeStruct((t, D_MODEL), BF16),
        jax.ShapeDtypeStruct((t, D_MODEL), BF16),
    )
    return pl.pallas_call(
        functools.partial(_inproj_kernel, n_p=t_p // tm),
        grid=(t // tm,),
        in_specs=[*_two_source_specs(t_p, t_s, tm, D_MODEL), _const_spec((D_MODEL, D_IN))],
        out_specs=(row(D_ATTN), row(D_KV), row(D_KV), row(D_RNN), row(D_RNN), row(D_MODEL), row(D_MODEL)),
        out_shape=out_shape,
        compiler_params=_params(("parallel",)),
        name="inproj",
    )(x_p, x_s, w_in_bf16)


def _t5_bucket(dist):
    n = jnp.maximum(dist, 0)
    max_exact = N_BUCKETS // 2
    nf = jnp.maximum(n, 1).astype(F32)
    large = max_exact + (jnp.log(nf / max_exact) / math.log(MAX_DISTANCE / max_exact) * (N_BUCKETS - max_exact)).astype(jnp.int32)
    large = jnp.minimum(large, N_BUCKETS - 1)
    return jnp.where(n < max_exact, n, large)


def _bias_table(rel_bias, dist):
    bucket = _t5_bucket(dist)
    rb = rel_bias.astype(F32)
    out = jnp.zeros((N_HEADS, *dist.shape), F32)
    for j in range(N_BUCKETS):
        out = jnp.where(bucket[None] == j, rb[j][:, None, None], out)
    return out


def _softmax_pv(s, sink, v):
    m = jnp.maximum(jnp.max(s, axis=-1, keepdims=True), sink)
    p = jnp.exp(s - m)
    denom = jnp.sum(p, axis=-1, keepdims=True) + jnp.exp(sink - m)
    return jnp.dot(p.astype(BF16), v, preferred_element_type=F32), denom


def _attn_prompt_kernel(sink_ref, q_ref, kc_ref, kp_ref, vc_ref, vp_ref, bias_ref, o_ref, *, q_blocks):
    keys = jnp.concatenate([kp_ref[...], kc_ref[...]], axis=0).astype(BF16)
    vals = jnp.concatenate([vp_ref[...], vc_ref[...]], axis=0).astype(BF16)
    first_table = jnp.minimum(pl.program_id(1), 1)
    for j in range(q_blocks):
        table = first_table if j == 0 else 1
        kk = keys[j * WINDOW:(j + 2) * WINDOW]
        vv = vals[j * WINDOW:(j + 2) * WINDOW]
        q = q_ref[j * WINDOW:(j + 1) * WINDOW, :]
        for g in range(N_KV_HEADS):
            kg = kk[:, g * HEAD_DIM:(g + 1) * HEAD_DIM]
            vg = vv[:, g * HEAD_DIM:(g + 1) * HEAD_DIM]
            for h in range(GROUP):
                hh = g * GROUP + h
                qh = q[:, hh * HEAD_DIM:(hh + 1) * HEAD_DIM]
                s = lax.dot_general(qh, kg, (((1,), (1,)), ((), ())), preferred_element_type=F32) + bias_ref[table, hh]
                o, denom = _softmax_pv(s, sink_ref[0, hh], vg)
                o_ref[j * WINDOW:(j + 1) * WINDOW, hh * HEAD_DIM:(hh + 1) * HEAD_DIM] = (o / denom).astype(BF16)


def _attn_prompt(q_all, k_all, v_all, bias, sinks, batch, seq):
    nb = seq // WINDOW
    q_blocks = _pick_tile(nb, (2, 1))
    steps = nb // q_blocks
    cur = lambda width: pl.BlockSpec((q_blocks * WINDOW, width), lambda b, n: (b * steps + n, 0))
    prev = lambda width: pl.BlockSpec((WINDOW, width), lambda b, n: (b * nb + jnp.maximum(n * q_blocks - 1, 0), 0))
    return pl.pallas_call(
        functools.partial(_attn_prompt_kernel, q_blocks=q_blocks),
        grid=(batch, steps),
        in_specs=[
            pl.BlockSpec(memory_space=pltpu.SMEM),
            cur(D_ATTN), cur(D_KV), prev(D_KV), cur(D_KV), prev(D_KV),
            _const_spec((2, N_HEADS, WINDOW, 2 * WINDOW)),
        ],
        out_specs=cur(D_ATTN),
        out_shape=jax.ShapeDtypeStruct((batch * seq, D_ATTN), BF16),
        compiler_params=_params(("parallel", "arbitrary")),
        name="attn_prompt",
    )(sinks.reshape(1, N_HEADS).astype(F32), q_all, k_all, k_all, v_all, v_all, bias)


def _attn_sample_kernel(q_ref, kn_ref, vn_ref, kc_ref, vc_ref, bc_ref, bn_ref, sink_ref,
                        o_ref, ko_ref, vo_ref, *, seqs, s_len):
    w = kc_ref.shape[1]
    rows_c = lax.broadcasted_iota(jnp.int32, (GROUP * s_len, w), 0) % s_len
    cols_c = lax.broadcasted_iota(jnp.int32, (GROUP * s_len, w), 1)
    dist_c = rows_c + w - cols_c
    valid_c = (dist_c >= 0) & (dist_c <= WINDOW)
    rows_n = lax.broadcasted_iota(jnp.int32, (GROUP * s_len, s_len), 0) % s_len
    cols_n = lax.broadcasted_iota(jnp.int32, (GROUP * s_len, s_len), 1)
    dist_n = rows_n - cols_n
    valid_n = (dist_n >= 0) & (dist_n <= WINDOW)
    for j in range(seqs):
        r0 = j * s_len
        qj = q_ref[r0:r0 + s_len, :]
        kc = kc_ref[j]
        vc = vc_ref[j]
        kn = kn_ref[r0:r0 + s_len, :]
        vn = vn_ref[r0:r0 + s_len, :]
        ko_ref[j, 0:w - s_len, :] = kc[s_len:, :]
        ko_ref[j, w - s_len:w, :] = kn
        vo_ref[j, 0:w - s_len, :] = vc[s_len:, :]
        vo_ref[j, w - s_len:w, :] = vn
        kcb, vcb, knb, vnb = kc.astype(BF16), vc.astype(BF16), kn.astype(BF16), vn.astype(BF16)
        for g in range(N_KV_HEADS):
            lo, hi = g * HEAD_DIM, (g + 1) * HEAD_DIM
            qs = jnp.concatenate(
                [qj[:, (g * GROUP + h) * HEAD_DIM:(g * GROUP + h + 1) * HEAD_DIM] for h in range(GROUP)], axis=0)
            nt = (((1,), (1,)), ((), ()))
            s_c = lax.dot_general(qs, kcb[:, lo:hi], nt, preferred_element_type=F32)
            s_n = lax.dot_general(qs, knb[:, lo:hi], nt, preferred_element_type=F32)
            s_c = jnp.where(valid_c, s_c + bc_ref[g], NEG_INF)
            s_n = jnp.where(valid_n, s_n + bn_ref[g], NEG_INF)
            sink = sink_ref[g]
            m = jnp.maximum(jnp.maximum(jnp.max(s_c, axis=-1, keepdims=True), jnp.max(s_n, axis=-1, keepdims=True)), sink)
            p_c = jnp.exp(s_c - m)
            p_n = jnp.exp(s_n - m)
            denom = jnp.sum(p_c, axis=-1, keepdims=True) + jnp.sum(p_n, axis=-1, keepdims=True) + jnp.exp(sink - m)
            o = jnp.dot(p_c.astype(BF16), vcb[:, lo:hi], preferred_element_type=F32)
            o = o + jnp.dot(p_n.astype(BF16), vnb[:, lo:hi], preferred_element_type=F32)
            o = (o / denom).astype(BF16)
            for h in range(GROUP):
                hh = g * GROUP + h
                o_ref[r0:r0 + s_len, hh * HEAD_DIM:(hh + 1) * HEAD_DIM] = o[h * s_len:(h + 1) * s_len, :]


def _attn_sample(q_all, k_all, v_all, cache_k, cache_v, rel_bias, sinks, row0, dec_batch, s_len):
    w = cache_k.shape[1]
    seqs = _pick_tile(dec_batch, (16, 8, 4, 2, 1))
    rows = seqs * s_len
    blk0 = row0 // rows
    assert row0 % rows == 0
    qi = jnp.arange(s_len)
    dist_c = qi[:, None] + w - jnp.arange(w)[None, :]
    dist_n = qi[:, None] - jnp.arange(s_len)[None, :]
    b_c = _bias_table(rel_bias, dist_c).reshape(N_KV_HEADS, GROUP * s_len, w)
    b_n = _bias_table(rel_bias, dist_n).reshape(N_KV_HEADS, GROUP * s_len, s_len)
    sink = jnp.broadcast_to(sinks.astype(F32).reshape(N_KV_HEADS, GROUP, 1, 1), (N_KV_HEADS, GROUP, s_len, 1))
    sink = sink.reshape(N_KV_HEADS, GROUP * s_len, 1)
    tok = lambda width: pl.BlockSpec((rows, width), lambda i: (blk0 + i, 0))
    cache = pl.BlockSpec((seqs, w, D_KV), lambda i: (i, 0, 0))
    return pl.pallas_call(
        functools.partial(_attn_sample_kernel, seqs=seqs, s_len=s_len),
        grid=(dec_batch // seqs,),
        in_specs=[
            tok(D_ATTN), tok(D_KV), tok(D_KV), cache, cache,
            _const_spec(b_c.shape), _const_spec(b_n.shape), _const_spec(sink.shape),
        ],
        out_specs=(pl.BlockSpec((rows, D_ATTN), lambda i: (i, 0)), cache, cache),
        out_shape=(
            jax.ShapeDtypeStruct((dec_batch * s_len, D_ATTN), BF16),
            jax.ShapeDtypeStruct((dec_batch, w, D_KV), F32),
            jax.ShapeDtypeStruct((dec_batch, w, D_KV), F32),
        ),
        compiler_params=_params(("parallel",)),
        name="attn_sample",
    )(q_all, k_all, v_all, cache_k, cache_v, b_c, b_n, sink)


def _softplus(z):
    return jnp.maximum(z, 0.0) + jnp.log1p(jnp.exp(-jnp.abs(z)))


def _block_gate(xcb, w_ref, b_ref):
    parts = [jnp.dot(xcb[:, n * RNN_BLOCK:(n + 1) * RNN_BLOCK], w_ref[n], preferred_element_type=F32)
             for n in range(N_RNN_BLOCKS)]
    return jax.nn.sigmoid(jnp.concatenate(parts, axis=-1) + b_ref[...])


def _lru_coeffs(xc, wa_ref, ba_ref, wx_ref, bx_ref, lam_ref, first_row_unnormalised):
    xcb = xc.astype(BF16)
    r = _block_gate(xcb, wa_ref, ba_ref)
    i = _block_gate(xcb, wx_ref, bx_ref)
    log_a = -RG_C * r * _softplus(-lam_ref[...])
    a = jnp.exp(log_a)
    mult = jnp.sqrt(-jnp.tanh(log_a) * (a * a + 1.0))
    if first_row_unnormalised is not None:
        mult = jnp.where(first_row_unnormalised, 1.0, mult)
    return a, mult * i * xc


def _scan8(a, b):
    shape = a.shape
    grouped = (shape[0] // SUBLANES, SUBLANES, shape[1])
    a = a.reshape(grouped)
    b = b.reshape(grouped)
    r8 = lax.broadcasted_iota(jnp.int32, grouped, 1)
    d = 1
    while d < SUBLANES:
        keep = r8 >= d
        a_sh = jnp.where(keep, pltpu.roll(a, d, 1), 1.0)
        b_sh = jnp.where(keep, pltpu.roll(b, d, 1), 0.0)
        b = a * b_sh + b
        a = a * a_sh
        d *= 2
    return a.reshape(shape), b.reshape(shape)


def _rnn_prompt_kernel(xr_ref, gy_ref, cw_ref, cb_ref, wa_ref, ba_ref, wx_ref, bx_ref, lam_ref,
                       o_ref, nh_ref, ext_ref, a_ref, b_ref, hc_ref, *, tl):
    l = pl.program_id(1)

    @pl.when(l == 0)
    def _():
        ext_ref[0:SUBLANES, :] = jnp.zeros((SUBLANES, D_RNN), F32)
        hc_ref[...] = jnp.zeros((1, D_RNN), F32)

    x = xr_ref[...]
    ext_ref[SUBLANES:, :] = x
    xc = cb_ref[...] + cw_ref[CONV_W - 1:CONV_W, :] * x
    for j in range(1, CONV_W):
        xc = xc + cw_ref[CONV_W - 1 - j:CONV_W - j, :] * ext_ref[SUBLANES - j:SUBLANES - j + tl, :]
    ext_ref[0:SUBLANES, :] = x[tl - SUBLANES:, :]

    row = lax.broadcasted_iota(jnp.int32, (tl, D_RNN), 0)
    a, b = _lru_coeffs(xc, wa_ref, ba_ref, wx_ref, bx_ref, lam_ref, (row == 0) & (l == 0))
    a, b = _scan8(a, b)
    a_ref[...] = a
    b_ref[...] = b

    def chunk(c, h):
        sl = pl.ds(pl.multiple_of(c * SUBLANES, SUBLANES), SUBLANES)
        hc = b_ref[sl, :] + a_ref[sl, :] * h
        b_ref[sl, :] = hc
        return hc[SUBLANES - 1:SUBLANES, :]

    h = lax.fori_loop(0, tl // SUBLANES, chunk, hc_ref[...])
    hc_ref[...] = h
    nh_ref[0] = h
    o_ref[...] = (b_ref[...] * gy_ref[...]).astype(BF16)


def _rnn_prompt(xr_all, gy_all, rnn_w, batch, seq):
    tl =_pick_tile(seq, (256, 128, 64, 32, 16, 8))
    nl = seq // tl
    tok = pl.BlockSpec((tl, D_RNN), lambda b, l: (b * nl + l, 0))
    return pl.pallas_call(
        functools.partial(_rnn_prompt_kernel, tl=tl),
        grid=(batch, nl),
        in_specs=[tok, tok] + [_const_spec(w.shape) for w in rnn_w],
        out_specs=(tok, pl.BlockSpec((1, 1, D_RNN), lambda b, l: (b, 0, 0))),
        out_shape=(jax.ShapeDtypeStruct((batch * seq, D_RNN), BF16), jax.ShapeDtypeStruct((batch, 1, D_RNN), F32)),
        scratch_shapes=[
            pltpu.VMEM((tl + SUBLANES, D_RNN), F32),
            pltpu.VMEM((tl, D_RNN), F32),
            pltpu.VMEM((tl, D_RNN), F32),
            pltpu.VMEM((1, D_RNN), F32),
        ],
        compiler_params=_params(("parallel", "arbitrary")),
        name="rnn_prompt",
    )(xr_all, gy_all, *rnn_w)


def _rnn_sample_kernel(xr_ref, gy_ref, hp_ref, h0_ref, cw_ref, cb_ref, wa_ref, ba_ref, wx_ref, bx_ref, lam_ref,
                       o_ref, nh_ref, *, seqs):
    rows = seqs * SUBLANES
    x = xr_ref[...]
    hp = hp_ref[...]
    r8 = lax.broadcasted_iota(jnp.int32, (rows, D_RNN), 0) % SUBLANES
    xc = cb_ref[...] + cw_ref[CONV_W - 1:CONV_W, :] * x
    for j in range(1, CONV_W):
        shifted = jnp.where(r8 >= j, pltpu.roll(x, j, 0), pltpu.roll(hp, rows - (SUBLANES - j), 0))
        xc = xc + cw_ref[CONV_W - 1 - j:CONV_W - j, :] * shifted
    a, b = _lru_coeffs(xc, wa_ref, ba_ref, wx_ref, bx_ref, lam_ref, None)
    a, b = _scan8(a, b)
    h0 = jnp.broadcast_to(h0_ref[...][:, None, :], (seqs, SUBLANES, D_RNN)).reshape(rows, D_RNN)
    h = b + a * h0
    last = jnp.where(r8 == SUBLANES - 1, h, 0.0).reshape(seqs, SUBLANES, D_RNN)
    nh_ref[...] = jnp.sum(last, axis=1)
    o_ref[...] = (h * gy_ref[...]).astype(BF16)


def _rnn_sample(xr_all, gy_all, hist_pad, h0, rnn_w, row0, dec_batch):
    seqs = _pick_tile(dec_batch, (16, 8))
    rows = seqs * SUBLANES
    assert row0 % rows == 0
    blk0 = row0 // rows
    tok = pl.BlockSpec((rows, D_RNN), lambda i: (blk0 + i, 0))
    return pl.pallas_call(
        functools.partial(_rnn_sample_kernel, seqs=seqs),
        grid=(dec_batch // seqs,),
        in_specs=[tok, tok, pl.BlockSpec((rows, D_RNN), lambda i: (i, 0)), pl.BlockSpec((seqs, D_RNN), lambda i: (i, 0))]
        + [_const_spec(w.shape) for w in rnn_w],
        out_specs=(pl.BlockSpec((rows, D_RNN), lambda i: (i, 0)), pl.BlockSpec((seqs, D_RNN), lambda i: (i, 0))),
        out_shape=(jax.ShapeDtypeStruct((dec_batch * SUBLANES, D_RNN), BF16),
                   jax.ShapeDtypeStruct((dec_batch, D_RNN), F32)),
        compiler_params=_params(("parallel",)),
        name="rnn_sample",
    )(xr_all, gy_all, hist_pad, h0, *rnn_w)


def _layer_norm(z, g, b):
    mu = jnp.mean(z, axis=-1, keepdims=True)
    zc = z - mu
    var = jnp.mean(zc * zc, axis=-1, keepdims=True)
    return zc * lax.rsqrt(var + LN_EPS) * g + b


def _first_index_of_max(vals, iota, axis, sentinel):
    mx = jnp.max(vals, axis=axis, keepdims=True)
    return mx, jnp.min(jnp.where(vals == mx, iota, sentinel), axis=axis, keepdims=True)


def _route(scores, bias):
    t = scores.shape[1]
    grp = scores + bias
    g3 = grp.reshape(N_GROUPS, GROUP_SIZE, t)
    e_in_g = lax.broadcasted_iota(jnp.int32, g3.shape, 1)
    m1, first = _first_index_of_max(g3, e_in_g, 1, GROUP_SIZE)
    m2 = jnp.max(jnp.where(e_in_g == first, -jnp.inf, g3), axis=1, keepdims=True)
    gscore = (m1 + m2).reshape(N_GROUPS, t)
    g_iota = lax.broadcasted_iota(jnp.int32, gscore.shape, 0)
    gmask = jnp.zeros(gscore.shape, jnp.bool_)
    for _ in range(TOPK_GROUPS):
        _, gi = _first_index_of_max(gscore, g_iota, 0, N_GROUPS)
        hit = g_iota == gi
        gmask = gmask | hit
        gscore = jnp.where(hit, -jnp.inf, gscore)
    masked = jnp.where(gmask[:, None, :], g3, -jnp.inf).reshape(N_EXPERTS, t)
    e_iota = lax.broadcasted_iota(jnp.int32, masked.shape, 0)
    idx, wts, hits = [], [], []
    for _ in range(TOP_K):
        _, ei = _first_index_of_max(masked, e_iota, 0, N_EXPERTS)
        hit = e_iota == ei
        idx.append(ei)
        hits.append(hit)
        wts.append(jnp.sum(jnp.where(hit, scores, 0.0), axis=0, keepdims=True))
        masked = jnp.where(hit, -jnp.inf, masked)
    idx = jnp.concatenate(idx, axis=0)
    w = jnp.concatenate(wts, axis=0)
    w = w / jnp.sum(w, axis=0, keepdims=True) * ROUTED_SCALE
    return idx, w, hits


def _merge_kernel(xp_ref, xs_ref, aop_ref, aos_ref, rop_ref, ros_ref, sga_ref, sgr_ref, woa_ref, wor_ref, wout_ref, g1_ref, b1_ref,
                  wrt_ref, rb_ref, wsg_ref, wsu_ref, wsd_ref,
                  x1w_ref, base_ref, idx_ref, wt_ref, rank_ref, cnt_ref, carry_ref, *, n_p, tm):
    i = pl.program_id(0)

    @pl.when(i == 0)
    def _():
        carry_ref[...] = jnp.zeros(carry_ref.shape, F32)

    is_prompt = i < n_p
    x = jnp.where(is_prompt, xp_ref[...], xs_ref[...])
    pa = jnp.dot(jnp.where(is_prompt, aop_ref[...], aos_ref[...]), woa_ref[...], preferred_element_type=F32)
    pr = jnp.dot(jnp.where(is_prompt, rop_ref[...], ros_ref[...]), wor_ref[...], preferred_element_type=F32)
    merged = sga_ref[...] * pa + sgr_ref[...] * pr
    z = ALPHA * x + jnp.dot(merged.astype(BF16), wout_ref[...], preferred_element_type=F32)
    x1 = _layer_norm(z, g1_ref[...], b1_ref[...])
    words = _pack_words(x1)
    for h in range(tm // SC_ROWS):
        for s in range(ROW_TILE):
            x1w_ref[(h * ROW_TILE + s) * SC_ROWS:(h * ROW_TILE + s + 1) * SC_ROWS, :] = (
                words[h * SC_ROWS:(h + 1) * SC_ROWS, s * LANES:(s + 1) * LANES])
    x1b = x1.astype(BF16)
    u = jax.nn.silu(jnp.dot(x1b, wsg_ref[...], preferred_element_type=F32)) * jnp.dot(x1b, wsu_ref[...], preferred_element_type=F32)
    shared = jnp.dot(u.astype(BF16), wsd_ref[...], preferred_element_type=F32)
    base_ref[...] = ALPHA * x1 + shared
    logits = lax.dot_general(wrt_ref[...], x1, (((1,), (1,)), ((), ())), preferred_element_type=F32)
    idx, w, hits = _route(jax.nn.sigmoid(logits), rb_ref[...])
    idx_ref[...] = idx
    wt_ref[...] = w

    chosen = functools.reduce(jnp.logical_or, hits)
    chosen_f = jnp.where(chosen, 1.0, 0.0)
    earlier = (lax.broadcasted_iota(jnp.int32, (tm, tm), 0) < lax.broadcasted_iota(jnp.int32, (tm, tm), 1))
    prefix = jnp.dot(chosen_f.astype(BF16), jnp.where(earlier, 1.0, 0.0).astype(BF16), preferred_element_type=F32)
    before = prefix + carry_ref[...]
    ranks = [jnp.sum(jnp.where(hit, before, 0.0), axis=0, keepdims=True) for hit in hits]
    rank_ref[...] = jnp.concatenate(ranks, axis=0).astype(jnp.int32)
    carry_ref[...] = carry_ref[...] + jnp.sum(chosen_f, axis=1, keepdims=True)
    cnt_ref[...] = carry_ref[...]


def _merge(x_p, x_s, attn_p, attn_s, rnn_p, rnn_s, sga, sgr, weights):
    t_p, t_s = x_p.shape[0], x_s.shape[0]
    t = t_p + t_s
    tm = _pick_tile(math.gcd(t_p, t_s), (512, 256, 128))
    row = lambda width: pl.BlockSpec((tm, width), lambda i: (i, 0))
    col = pl.BlockSpec((TOP_K, tm), lambda i: (0, i))

    return pl.pallas_call(
        functools.partial(_merge_kernel, n_p=t_p // tm, tm=tm),
        grid=(t // tm,),
        in_specs=[*_two_source_specs(t_p, t_s, tm, D_MODEL), *_two_source_specs(t_p, t_s, tm, D_ATTN),
                  *_two_source_specs(t_p, t_s, tm, D_RNN), row(D_MODEL), row(D_MODEL)]
        + [_const_spec(w.shape) for w in weights],
        out_specs=(pl.BlockSpec((tm * ROW_TILE, LANES), lambda i: (i, 0)), row(D_MODEL), col, col, col,
                   _const_spec((N_EXPERTS, 1))),
        out_shape=(
            jax.ShapeDtypeStruct((t * ROW_TILE, LANES), WORD),
            jax.ShapeDtypeStruct((t, D_MODEL), F32),
            jax.ShapeDtypeStruct((TOP_K, t), jnp.int32),
            jax.ShapeDtypeStruct((TOP_K, t), F32),
            jax.ShapeDtypeStruct((TOP_K, t), jnp.int32),
            jax.ShapeDtypeStruct((N_EXPERTS, 1), F32),
        ),
        scratch_shapes=[pltpu.VMEM((N_EXPERTS, 1), F32)],
        compiler_params=_params(("arbitrary",)),
        name="merge_ln1_route",
    )(x_p, x_s, attn_p, attn_s, rnn_p, rnn_s, sga, sgr, *weights)


def _expert_chunk_rows(n_assign):
    return max(EXPERT_ROWS, -(-(n_assign // N_EXPERTS * 9 // 8) // EXPERT_ROWS) * EXPERT_ROWS)


def _expert_layout(counts, n_assign, chunk_rows):
    counts = counts.reshape(N_EXPERTS).astype(jnp.int32)
    padded = (counts + ROW_PAD - 1) // ROW_PAD * ROW_PAD
    pend = jnp.cumsum(padded)
    pstart = pend - padded
    rows_alloc = (n_assign + N_EXPERTS * (ROW_PAD - 1)) // ROW_PAD * ROW_PAD + chunk_rows
    n_chunks = jnp.maximum((counts + chunk_rows - 1) // chunk_rows, 1)
    part = CHUNK_PART
    written = jnp.maximum((counts + part - 1) // part, 1) * part
    ffn_end = jnp.max(pstart + written).reshape(1)
    return pstart, pstart + counts, pend, n_chunks, counts, ffn_end, rows_alloc


def _dest_kernel(idx_ref, rank_ref, pstart_ref, dest_ref, word_rows_ref):
    e_iota = lax.broadcasted_iota(jnp.int32, (N_EXPERTS, idx_ref.shape[1]), 0)
    starts = [jnp.sum(jnp.where(e_iota == idx_ref[k:k + 1, :], pstart_ref[...], 0), axis=0, keepdims=True)
              for k in range(TOP_K)]
    dest = jnp.concatenate(starts, axis=0) + rank_ref[...]
    dest_ref[...] = dest
    tm = dest.shape[1]
    word_rows_ref[0] = jnp.concatenate(
        [dest[k:k + 1, h * SC_ROWS:(h + 1) * SC_ROWS] * ROW_TILE + s
         for h in range(tm // SC_ROWS) for s in range(ROW_TILE) for k in range(TOP_K)], axis=0)


WORD_ROW_VECS = ROW_TILE * TOP_K


def _dest_rows(idx_t, rank_t, pstart):
    t = idx_t.shape[1]
    tm = _pick_tile(t, (1024, 512, 256, 128))
    col = pl.BlockSpec((TOP_K, tm), lambda i: (0, i))
    vecs = tm // SC_ROWS * WORD_ROW_VECS
    dest, word_rows = pl.pallas_call(
        _dest_kernel,
        grid=(t // tm,),
        in_specs=[col, col, _const_spec((N_EXPERTS, 1))],
        out_specs=(col, pl.BlockSpec((1, vecs, SC_ROWS), lambda i: (i, 0, 0))),
        out_shape=(jax.ShapeDtypeStruct((TOP_K, t), jnp.int32),
                   jax.ShapeDtypeStruct((t // tm, vecs, SC_ROWS), jnp.int32)),
        compiler_params=_params(("parallel",)),
        name="dest_rows",
    )(idx_t, rank_t, pstart.reshape(N_EXPERTS, 1))
    return dest, word_rows.reshape(-1, SC_ROWS)


def _token_rows(r, n=1):
    return pl.ds(pl.multiple_of(r * ROW_TILE, ROW_TILE), n * ROW_TILE)


def _zero_row_groups(zero_ref, dst_ref, sem, first_group, n_groups):
    def start(g, c):
        pltpu.make_async_copy(zero_ref, dst_ref.at[_token_rows(g * ROW_PAD, ROW_PAD)], sem).start()
        return c

    lax.fori_loop(first_group, n_groups, start, 0)
    return n_groups - first_group


def _wait_zero_copies(zero_ref, dst_ref, sem, n):
    def wait(_, c):
        pltpu.make_async_copy(zero_ref, dst_ref.at[_token_rows(0, ROW_PAD)], sem).wait()
        return c

    lax.fori_loop(0, n, wait, 0)


SC_ROWS = 128


def _sc_workers():
    info = plsc.get_sparse_core_info()
    mesh = plsc.VectorSubcoreMesh(core_axis_name="c", subcore_axis_name="s")
    worker = lambda: lax.axis_index("s") * info.num_cores + lax.axis_index("c")
    return mesh, info.num_cores * info.num_subcores, worker


def _sc_scatter_rows(src, rows, n_out):
    mesh, n_workers, worker = _sc_workers()
    n_units = src.shape[0] // SC_ROWS
    assert n_units % n_workers == 0 and rows.shape == (n_units * TOP_K, SC_ROWS)
    per_worker = n_units // n_workers

    unit_bufs = [pltpu.VMEM((TOP_K, SC_ROWS), jnp.int32), pltpu.VMEM((SC_ROWS, LANES), src.dtype),
                 pltpu.SemaphoreType.DMA]

    @functools.partial(
        pl.kernel, mesh=mesh,
        out_type=jax.ShapeDtypeStruct((n_out, LANES), src.dtype),
        scratch_types=unit_bufs + unit_bufs + [pltpu.SemaphoreType.DMA],
    )
    def scatter(src_hbm, rows_hbm, out_hbm, rows_a, data_a, load_a, rows_b, data_b, load_b, sem):
        first = worker() * per_worker
        bufs = ((rows_a, data_a, load_a), (rows_b, data_b, load_b))

        def loads(u, buf):
            rows_v, data_v, load_sem = buf
            return (pltpu.make_async_copy(rows_hbm.at[pl.ds(u * TOP_K, TOP_K)], rows_v, load_sem),
                    pltpu.make_async_copy(src_hbm.at[pl.ds(u * SC_ROWS, SC_ROWS)], data_v, load_sem))

        def unit(u, buf, other, has_next):
            for cp in loads(u, buf):
                cp.wait()

            @pl.when(has_next)
            def _():
                for cp in loads(u + 1, other):
                    cp.start()

            rows_v, data_v, _ = buf
            copies = [pltpu.async_copy(data_v, out_hbm.at[rows_v.at[k]], sem) for k in range(TOP_K)]
            for cp in copies:
                cp.wait()

        for cp in loads(first, bufs[0]):
            cp.start()

        @pl.loop(0, per_worker // 2)
        def _(pair):
            u = first + 2 * pair
            unit(u, bufs[0], bufs[1], True)
            unit(u + 1, bufs[1], bufs[0], 2 * pair + 2 < per_worker)

        if per_worker % 2:
            unit(first + per_worker - 1, bufs[0], bufs[1], False)

    return scatter(src, rows)


def _zero_padding_kernel(lo_ref, hi_ref, tail_ref, xs_in_ref, xs_ref, zero_ref, sem, *, rows_alloc):
    del xs_in_ref
    zero_ref[...] = jnp.zeros(zero_ref.shape, WORD)
    row_copy = lambda r: pltpu.make_async_copy(zero_ref.at[pl.ds(0, ROW_TILE)], xs_ref.at[_token_rows(r)], sem)

    def expert(e, n):
        def row(r, c):
            row_copy(r).start()
            return c

        lax.fori_loop(lo_ref[e], hi_ref[e], row, 0)
        return n + hi_ref[e] - lo_ref[e]

    def wait(_, c):
        row_copy(0).wait()
        return c

    lax.fori_loop(0, lax.fori_loop(0, N_EXPERTS, expert, 0), wait, 0)
    n = _zero_row_groups(zero_ref, xs_ref, sem, tail_ref[0] // ROW_PAD, rows_alloc // ROW_PAD)
    _wait_zero_copies(zero_ref, xs_ref, sem, n)


def _zero_padding(xs, pad_lo, pad_hi, total, rows_alloc):
    grid_spec = pltpu.PrefetchScalarGridSpec(
        num_scalar_prefetch=3,
        grid=(1,),
        in_specs=[pl.BlockSpec(memory_space=pl.ANY)],
        out_specs=pl.BlockSpec(memory_space=pl.ANY),
        scratch_shapes=[pltpu.VMEM((ROW_PAD * ROW_TILE, LANES), WORD), pltpu.SemaphoreType.DMA],
    )
    return pl.pallas_call(
        functools.partial(_zero_padding_kernel, rows_alloc=rows_alloc),
        grid_spec=grid_spec,
        out_shape=jax.ShapeDtypeStruct(xs.shape, xs.dtype),
        input_output_aliases={3: 0},
        compiler_params=_params(("arbitrary",)),
        name="zero_padding",
    )(pad_lo, pad_hi, total, xs)


def _expert_kernel(pstart_ref, nch_ref, rows_ref, end_ref, wg_ref, wu_ref, wd_ref, xs_ref, o_ref,
                   xbuf_ref, obuf_ref, wgb_ref, wub_ref, wdb_ref, zero_ref, done_ref, in_sem, out_sem,
                   *, rows_alloc, chunk_rows):
    e = pl.program_id(0)
    n_e = pl.num_programs(0)
    start = pstart_ref[e]
    nch = nch_ref[e]
    buf_rows = chunk_rows * ROW_TILE
    part = CHUNK_PART
    chunk_parts = chunk_rows // part
    part_rows = part * ROW_TILE

    class _ChunkCopy:
        def __init__(self, whole, parts, n_parts):
            self.whole = whole
            self.parts = parts
            self.n_parts = n_parts

        def _each(self, act):
            pl.when(self.n_parts == chunk_parts)(functools.partial(act, self.whole))
            for p, cp in enumerate(self.parts):
                pl.when((p < self.n_parts) & (self.n_parts < chunk_parts))(functools.partial(act, cp))

        def start(self):
            self._each(lambda cp: cp.start())

        def wait(self):
            self._each(lambda cp: cp.wait())

    def parts_of(rows_left):
        return jnp.clip((rows_left + part - 1) // part, 1, chunk_parts)

    def in_copy(row, slot, n_parts):
        copy = lambda p, n: pltpu.make_async_copy(
            xs_ref.at[_token_rows(row + p * part, n * part)],
            xbuf_ref.at[pl.ds(slot * buf_rows + p * part_rows, n * part_rows)], in_sem.at[slot])
        return _ChunkCopy(copy(0, chunk_parts), [copy(p, 1) for p in range(chunk_parts - 1)], n_parts)

    def out_copy(row, slot, n_parts):
        copy = lambda p, n: pltpu.make_async_copy(
            obuf_ref.at[pl.ds(slot * buf_rows + p * part_rows, n * part_rows)],
            o_ref.at[_token_rows(row + p * part, n * part)], out_sem)
        return _ChunkCopy(copy(0, chunk_parts), [copy(p, 1) for p in range(chunk_parts - 1)], n_parts)

    @pl.when(e == 0)
    def _():
        done_ref[0] = 0
        xbuf_ref[...] = jnp.zeros(xbuf_ref.shape, WORD)
        obuf_ref[...] = jnp.zeros(obuf_ref.shape, WORD)
        in_copy(start, 0, parts_of(rows_ref[0])).start()

    wgb_ref[...] = wg_ref[...].astype(BF16)
    wub_ref[...] = wu_ref[...].astype(BF16)
    wdb_ref[...] = wd_ref[...].astype(BF16)
    done = done_ref[0]

    def chunk(c, carry):
        g = done + c
        slot = g % 2
        row = start + c * chunk_rows
        rows_left = rows_ref[e] - c * chunk_rows
        n_parts = parts_of(rows_left)
        in_copy(row, slot, n_parts).wait()
        last = c + 1 == nch
        next_e = jnp.minimum(e + 1, n_e - 1)
        next_row = jnp.where(last, pstart_ref[next_e], row + chunk_rows)
        next_left = jnp.where(last, rows_ref[next_e], rows_left - chunk_rows)

        @pl.when(jnp.logical_not(last & (e == n_e - 1)))
        def _():
            in_copy(next_row, 1 - slot, parts_of(next_left)).start()

        base = pl.multiple_of(slot * buf_rows, buf_rows)
        def sub_block(h):
            xb = jnp.concatenate(_load_token_rows(xbuf_ref, EXPERT_ROWS, row0=h * EXPERT_ROWS, base=base),
                                 axis=1).astype(BF16)
            gate = jnp.dot(xb, wgb_ref[...], preferred_element_type=F32)
            up = jnp.dot(xb, wub_ref[...], preferred_element_type=F32)
            act = (jax.nn.silu(gate) * up).astype(BF16)
            _store_token_rows(obuf_ref, jnp.dot(act, wdb_ref[...], preferred_element_type=F32),
                              EXPERT_ROWS, row0=h * EXPERT_ROWS, base=base)

        n_sub = chunk_rows // EXPERT_ROWS
        needed = jnp.clip((rows_left + EXPERT_ROWS - 1) // EXPERT_ROWS, 1, n_sub)
        for count in range(1, n_sub + 1):
            @pl.when(needed == count)
            def _():
                for h in range(count):
                    sub_block(h)

        @pl.when(g > 0)
        def _():
            out_copy(0, 0, done_ref[1]).wait()

        out_copy(row, slot, n_parts).start()
        done_ref[1] = n_parts
        return carry

    lax.fori_loop(0, nch, chunk, 0)
    done_ref[0] = done + nch

    @pl.when(e == n_e - 1)
    def _():
        out_copy(0, 0, done_ref[1]).wait()
        zero_ref[...] = jnp.zeros(zero_ref.shape, WORD)
        n = _zero_row_groups(zero_ref, o_ref, out_sem, end_ref[0] // ROW_PAD, rows_alloc // ROW_PAD)
        _wait_zero_copies(zero_ref, o_ref, out_sem, n)


def _expert_ffn(xs, pstart, n_chunks, rows, ffn_end, rows_alloc, chunk_rows, w_e_gate, w_e_up, w_e_down):
    weight = lambda shape: pl.BlockSpec((None, *shape), lambda e, *_: (e, 0, 0))
    grid_spec = pltpu.PrefetchScalarGridSpec(
        num_scalar_prefetch=4,
        grid=(N_EXPERTS,),
        in_specs=[weight((D_MODEL, D_EXPERT)), weight((D_MODEL, D_EXPERT)), weight((D_EXPERT, D_MODEL)),
                  pl.BlockSpec(memory_space=pl.ANY)],
        out_specs=pl.BlockSpec(memory_space=pl.ANY),
        scratch_shapes=[
            pltpu.VMEM((2 * chunk_rows * ROW_TILE, LANES), WORD),
            pltpu.VMEM((2 * chunk_rows * ROW_TILE, LANES), WORD),
            pltpu.VMEM((D_MODEL, D_EXPERT), BF16), pltpu.VMEM((D_MODEL, D_EXPERT), BF16),
            pltpu.VMEM((D_EXPERT, D_MODEL), BF16),
            pltpu.VMEM((ROW_PAD * ROW_TILE, LANES), WORD),
            pltpu.SMEM((2,), jnp.int32),
            pltpu.SemaphoreType.DMA((2,)), pltpu.SemaphoreType.DMA,
        ],
    )
    return pl.pallas_call(
        functools.partial(_expert_kernel, rows_alloc=rows_alloc, chunk_rows=chunk_rows),
        grid_spec=grid_spec,
        out_shape=jax.ShapeDtypeStruct((rows_alloc * ROW_TILE, LANES), WORD),
        compiler_params=_params(("arbitrary",)),
        name="expert_ffn",
    )(pstart, n_chunks, rows, ffn_end, w_e_gate, w_e_up, w_e_down, xs)


def _combine_head_kernel(dest_ref, dest_next_ref, w_ref, base_ref, g_ref, b_ref, outs_ref, y_ref, buf_ref, sem, *, tm):
    i = pl.program_id(0)
    slot_rows = TOP_K * tm

    def gather(d_ref, slot):
        def issue(t, c):
            for k in range(TOP_K):
                pltpu.make_async_copy(outs_ref.at[_token_rows(d_ref[k, t])],
                                      buf_ref.at[_token_rows(slot * slot_rows + k * tm + t)], sem.at[slot]).start()
            return c

        lax.fori_loop(0, tm, issue, 0)

    @pl.when(i == 0)
    def _():
        gather(dest_ref, 0)

    @pl.when(i + 1 < pl.num_programs(0))
    def _():
        gather(dest_next_ref, (i + 1) % 2)

    slot = i % 2
    for k in range(TOP_K):
        pltpu.make_async_copy(outs_ref.at[_token_rows(0, tm)], buf_ref.at[_token_rows(slot * slot_rows + k * tm, tm)],
                              sem.at[slot]).wait()

    w = w_ref[...]
    base = pl.multiple_of(slot * slot_rows * ROW_TILE, ROW_TILE)
    halves = [None, None]
    for k in range(TOP_K):
        wk = w[:, k:k + 1]
        for j, rows in enumerate(_load_token_rows(buf_ref, tm, row0=k * tm, base=base)):
            halves[j] = wk * rows if halves[j] is None else halves[j] + wk * rows
    y_ref[...] = _layer_norm(base_ref[...] + jnp.concatenate(halves, axis=1), g_ref[...], b_ref[...])


def _combine_head(out_sorted, dest, w_tok, base, g2, b2, tm, n_head):
    row = pl.BlockSpec((tm, D_MODEL), lambda i: (i, 0))
    return pl.pallas_call(
        functools.partial(_combine_head_kernel, tm=tm),
        grid=(n_head,),
        in_specs=[
            pl.BlockSpec((TOP_K, tm), lambda i: (0, i), memory_space=pltpu.SMEM),
            pl.BlockSpec((TOP_K, tm), lambda i: (0, jnp.minimum(i + 1, n_head - 1)), memory_space=pltpu.SMEM),
            pl.BlockSpec((tm, TOP_K), lambda i: (i, 0)),
            row, _const_spec((1, D_MODEL)), _const_spec((1, D_MODEL)),
            pl.BlockSpec(memory_space=pl.ANY),
        ],
        out_specs=row,
        out_shape=jax.ShapeDtypeStruct((n_head * tm, D_MODEL), F32),
        scratch_shapes=[pltpu.VMEM((2 * TOP_K * tm * ROW_TILE, LANES), WORD), pltpu.SemaphoreType.DMA((2,))],
        compiler_params=_params(("arbitrary",)),
        name="combine_head",
    )(dest, dest, w_tok, base, g2, b2, out_sorted)


def _sc_gather_rows(table, rows):
    mesh, n_workers, worker = _sc_workers()
    m = rows.shape[0]
    per_worker = m // n_workers
    assert m % n_workers == 0 and per_worker % SC_ROWS == 0
    in_flight = _pick_tile(per_worker // SC_ROWS, (2, 1))
    step = SC_ROWS * in_flight
    n_stages = per_worker // step
    stage_bufs = [pltpu.VMEM((step,), jnp.int32), pltpu.VMEM((step, LANES), table.dtype),
                  pltpu.SemaphoreType.DMA, pltpu.SemaphoreType.DMA]

    @functools.partial(
        pl.kernel, mesh=mesh,
        out_type=jax.ShapeDtypeStruct((m, LANES), table.dtype),
        scratch_types=stage_bufs + stage_bufs,
    )
    def gather(table_hbm, rows_hbm, out_hbm, rows_a, data_a, gsem_a, ssem_a, rows_b, data_b, gsem_b, ssem_b):
        first = worker() * per_worker
        bufs = ((rows_a, data_a, gsem_a, ssem_a), (rows_b, data_b, gsem_b, ssem_b))

        def gathers(buf):
            rows_v, data_v, gsem, _ = buf
            return [pltpu.make_async_copy(table_hbm.at[rows_v.at[pl.ds(j * SC_ROWS, SC_ROWS)]],
                                          data_v.at[pl.ds(j * SC_ROWS, SC_ROWS)], gsem) for j in range(in_flight)]

        def store(off, buf):
            return pltpu.make_async_copy(buf[1], out_hbm.at[pl.ds(off, step)], buf[3])

        def stage(it, buf, other, has_next, other_stored):
            off = first + it * step

            @pl.when(has_next)
            def _():
                @pl.when(other_stored)
                def _():
                    store(off, other).wait()

                pltpu.sync_copy(rows_hbm.at[pl.ds(off + step, step)], other[0])
                for cp in gathers(other):
                    cp.start()

            for cp in gathers(buf):
                cp.wait()
            store(off, buf).start()

        pltpu.sync_copy(rows_hbm.at[pl.ds(first, step)], rows_a)
        for cp in gathers(bufs[0]):
            cp.start()

        @pl.loop(0, n_stages // 2)
        def _(pair):
            it = 2 * pair
            stage(it, bufs[0], bufs[1], True, pair > 0)
            stage(it + 1, bufs[1], bufs[0], it + 2 < n_stages, True)

        if n_stages % 2:
            stage(n_stages - 1, bufs[0], bufs[1], False, False)
        for back in range(min(2, n_stages)):
            store(first, bufs[(n_stages - 1 - back) % 2]).wait()

    return gather(table, rows)


def _combine_tail_kernel(yh_ref, rows_ref, w_ref, base_ref, g_ref, b_ref, yp_ref, ys_ref, *, tm, n_head, n_p):
    i = pl.program_id(0)

    @pl.when(i < n_head)
    def _():
        yp_ref[...] = yh_ref[...]

    @pl.when(i >= n_head)
    def _():
        w = w_ref[...]
        lo = [None] * ROW_TILE
        hi = [None] * ROW_TILE
        for k in range(TOP_K):
            wk = w[:, k:k + 1]
            for s in range(ROW_TILE):
                vec = lambda h: ((h * ROW_TILE + s) * TOP_K + k) * SC_ROWS
                words = jnp.concatenate([rows_ref[vec(h):vec(h) + SC_ROWS, :] for h in range(tm // SC_ROWS)], axis=0)
                lo_s, hi_s = _unpack_words(words)
                lo[s] = wk * lo_s if lo[s] is None else lo[s] + wk * lo_s
                hi[s] = wk * hi_s if hi[s] is None else hi[s] + wk * hi_s
        y = _layer_norm(base_ref[...] + jnp.concatenate(lo + hi, axis=1), g_ref[...], b_ref[...])

        @pl.when(i < n_p)
        def _():
            yp_ref[...] = y

        @pl.when(i >= n_p)
        def _():
            ys_ref[...] = y


def _combine_tail(y_head, gathered, w_tok, base, g2, b2, t_p, t_s, tm, n_head):
    n_p = t_p // tm
    assert n_head <= n_p
    n_tiles = (t_p + t_s) // tm
    blk = TOP_K * ROW_TILE * tm
    row = pl.BlockSpec((tm, D_MODEL), lambda i: (i, 0))
    out_p, out_s = _two_source_specs(t_p, t_s, tm, D_MODEL)
    return pl.pallas_call(
        functools.partial(_combine_tail_kernel, tm=tm, n_head=n_head, n_p=n_p),
        grid=(n_tiles,),
        in_specs=[
            pl.BlockSpec((tm, D_MODEL), lambda i: (jnp.minimum(i, n_head - 1), 0)),
            pl.BlockSpec((blk, LANES), lambda i: (jnp.maximum(i - n_head, 0), 0)),
            pl.BlockSpec((tm, TOP_K), lambda i: (i, 0)),
            row, _const_spec((1, D_MODEL)), _const_spec((1, D_MODEL)),
        ],
        out_specs=(out_p, out_s),
        out_shape=(jax.ShapeDtypeStruct((t_p, D_MODEL), F32), jax.ShapeDtypeStruct((t_s, D_MODEL), F32)),
        compiler_params=_params(("arbitrary",)),
        name="combine_tail",
    )(y_head, gathered, w_tok, base, g2, b2)


def kernel(x_prompt, x_sample, cache_k, cache_v, state_conv, state_rnn, w_in, conv_w, conv_b, w_gate_a, b_gate_a, w_gate_x, b_gate_x, lru_lambda, rel_bias, sinks, w_o_attn, w_o_rnn, w_out, ln1_g, ln1_b, w_router, router_bias, w_e_gate, w_e_up, w_e_down, w_s_gate, w_s_up, w_s_down, ln2_g, ln2_b):
    assert w_in.shape[0] == DEPTH == 1
    batch, seq, _ = x_prompt.shape
    dec_batch, s_len, _ = x_sample.shape
    w_cache = cache_k.shape[2]
    assert s_len == SUBLANES and seq % WINDOW == 0 and w_cache == WINDOW
    t_p = batch * seq
    t_s = dec_batch * s_len
    vec = lambda a: a[0].reshape(1, -1).astype(F32)

    x_p = x_prompt.reshape(t_p, D_MODEL)
    x_s = x_sample.reshape(t_s, D_MODEL)
    q, k, v, xr, gy, sga, sgr = _inproj(x_p, x_s, w_in[0].astype(BF16))

    qi = jnp.arange(WINDOW)
    dist = qi[:, None] + WINDOW - jnp.arange(2 * WINDOW)[None, :]
    band = (dist >= 0) & (dist <= WINDOW)
    has_prev = jnp.arange(2 * WINDOW)[None, :] >= WINDOW
    bias_p = _bias_table(rel_bias, dist)
    bias_p = jnp.stack([jnp.where(band & has_prev, bias_p, NEG_INF), jnp.where(band, bias_p, NEG_INF)])
    attn_p = _attn_prompt(q, k, v, bias_p, sinks[0], batch, seq)
    attn_s, k_s, v_s = _attn_sample(
        q, k, v, cache_k[0].reshape(dec_batch, w_cache, D_KV), cache_v[0].reshape(dec_batch, w_cache, D_KV),
        rel_bias, sinks[0], t_p, dec_batch, s_len)

    rnn_w = (conv_w[0], vec(conv_b), w_gate_a[0].astype(BF16), vec(b_gate_a), w_gate_x[0].astype(BF16),
             vec(b_gate_x), vec(lru_lambda))
    rnn_p, h_p = _rnn_prompt(xr, gy, rnn_w, batch, seq)
    hist_pad = jnp.pad(state_conv[0], ((0, 0), (SUBLANES - (CONV_W - 1), 0), (0, 0))).reshape(t_s, D_RNN)
    rnn_s, h_s = _rnn_sample(xr, gy, hist_pad, state_rnn[0], rnn_w, t_p, dec_batch)

    merge_w = (w_o_attn[0].astype(BF16), w_o_rnn[0].astype(BF16), w_out[0].astype(BF16), vec(ln1_g), vec(ln1_b),
               w_router[0].T, router_bias[0].reshape(N_EXPERTS, 1), w_s_gate[0].astype(BF16),
               w_s_up[0].astype(BF16), w_s_down[0].astype(BF16))
    x1w, base, idx_t, wt_t, rank_t, counts = _merge(x_p, x_s, attn_p, attn_s, rnn_p, rnn_s, sga, sgr, merge_w)

    n_assign = (t_p + t_s) * TOP_K
    chunk_rows = _expert_chunk_rows(n_assign)
    pstart, pad_lo, pad_hi, n_chunks, rows, ffn_end, rows_alloc = _expert_layout(counts, n_assign, chunk_rows)
    tm = _pick_tile(math.gcd(t_p, t_s), (256, 128))
    dest, word_rows = _dest_rows(idx_t, rank_t, pstart)
    xs = _sc_scatter_rows(x1w, word_rows, rows_alloc * ROW_TILE)
    xs = _zero_padding(xs, pad_lo, pad_hi, pad_hi[N_EXPERTS - 1:], rows_alloc)
    out_sorted = _expert_ffn(xs, pstart, n_chunks, rows, ffn_end, rows_alloc, chunk_rows,
                             w_e_gate[0], w_e_up[0], w_e_down[0])
    n_tiles = (t_p + t_s) // tm
    n_head = min(t_p // tm, n_tiles // 4)
    w_tok = wt_t.T
    g2, b2 = vec(ln2_g), vec(ln2_b)
    y_head = _combine_head(out_sorted, dest, w_tok, base, g2, b2, tm, n_head)
    gathered = _sc_gather_rows(out_sorted, word_rows[n_head * (tm // SC_ROWS) * WORD_ROW_VECS:].reshape(-1))
    y_p, y_s = _combine_tail(y_head, gathered, w_tok, base, g2, b2, t_p, t_s, tm, n_head)
    y_p = y_p.reshape(batch, seq, D_MODEL)
    y_s = y_s.reshape(dec_batch, s_len, D_MODEL)
    kv5 = lambda a, b: a.reshape(1, b, WINDOW, N_KV_HEADS, HEAD_DIM)
    tail = lambda a, n: jnp.stack([lax.slice_in_dim(a, (b + 1) * seq - n, (b + 1) * seq) for b in range(batch)])
    k_p = kv5(tail(k, WINDOW), batch)
    v_p = kv5(tail(v, WINDOW), batch)
    conv_p = tail(xr, CONV_W - 1)[None]
    conv_s = xr[t_p:].reshape(dec_batch, s_len, D_RNN)[:, s_len - (CONV_W - 1):][None]
    return (y_p, y_s, k_p, v_p, conv_p, h_p.reshape(1, batch, D_RNN),
            kv5(k_s, dec_batch), kv5(v_s, dec_batch), conv_s, h_s.reshape(1, dec_batch, D_RNN))
```

```python
import functools
import math

import jax
import jax.numpy as jnp
from jax import lax
from jax.experimental import pallas as pl
from jax.experimental.pallas import tpu as pltpu
from jax.experimental.pallas import tpu_sc as plsc

F32 = jnp.float32
BF16 = jnp.bfloat16
WORD = jnp.int32

D_MODEL = 1024
N_HEADS = 8
N_KV_HEADS = 2
HEAD_DIM = 64
GROUP = N_HEADS // N_KV_HEADS
WINDOW = 128
D_ATTN = N_HEADS * HEAD_DIM
D_KV = N_KV_HEADS * HEAD_DIM
N_BUCKETS = 32
MAX_DISTANCE = 128
D_RNN = D_MODEL
RNN_BLOCK = 256
N_RNN_BLOCKS = D_RNN // RNN_BLOCK
CONV_W = 4
RG_C = 8.0
N_EXPERTS = 256
TOP_K = 8
N_GROUPS = 8
GROUP_SIZE = N_EXPERTS // N_GROUPS
TOPK_GROUPS = 4
D_EXPERT = D_MODEL // 4
ROUTED_SCALE = 2.5
LN_EPS = 1e-5
DEPTH = 1
ALPHA = (2 * DEPTH) ** 0.25
NEG_INF = -1e30
SM_SCALE = HEAD_DIM ** -0.5
assert math.frexp(SM_SCALE)[0] == 0.5

O_Q = 0
O_K = D_ATTN
O_V = O_K + D_KV
O_XR = O_V + D_KV
O_YR = O_XR + D_RNN
O_GA = O_YR + D_RNN
O_GR = O_GA + D_MODEL
D_IN = O_GR + D_MODEL

SUBLANES = 8
VMEM_LIMIT_BYTES = 56 * 1024 * 1024
EXPERT_ROWS = 256
CHUNK_PART = 128
ROW_PAD = SUBLANES


def _params(sem):
    return pltpu.CompilerParams(dimension_semantics=sem, vmem_limit_bytes=VMEM_LIMIT_BYTES)


def _pick_tile(n, candidates):
    for c in candidates:
        if n % c == 0:
            return c
    raise ValueError(f"no tile for {n}")


def _const_spec(shape):
    nd = len(shape)
    return pl.BlockSpec(shape, lambda *_: (0,) * nd)


LANES = 128
ROW_WORDS = D_MODEL // 2
ROW_TILE = ROW_WORDS // LANES
HIGH_HALF = -65536


def _pack_words(mat):
    as_bits = lambda v: pltpu.bitcast(v.astype(BF16).astype(F32), WORD)
    return (as_bits(mat[:, ROW_WORDS:]) & HIGH_HALF) | lax.shift_right_logical(as_bits(mat[:, :ROW_WORDS]), 16)


def _unpack_words(words):
    return pltpu.bitcast(words << 16, F32), pltpu.bitcast(words & HIGH_HALF, F32)


def _store_token_rows(ref, mat, n, row0=0, base=0):
    words = _pack_words(mat)
    for s in range(ROW_TILE):
        ref[pl.ds(base + row0 * ROW_TILE + s, n, stride=ROW_TILE), :] = words[:, s * LANES:(s + 1) * LANES]


def _load_token_rows(ref, n, row0=0, base=0):
    words = jnp.concatenate(
        [ref[pl.ds(base + row0 * ROW_TILE + s, n, stride=ROW_TILE), :] for s in range(ROW_TILE)], axis=1)
    return _unpack_words(words)


def _two_source_specs(t_p, t_s, tm, width):
    n_p = t_p // tm
    assert t_p % tm == 0 and t_s % tm == 0
    return (pl.BlockSpec((tm, width), lambda i, *_: (jnp.minimum(i, n_p - 1), 0)),
            pl.BlockSpec((tm, width), lambda i, *_: (jnp.maximum(i - n_p, 0), 0)))


def _inproj_kernel(xp_ref, xs_ref, w_ref, q_ref, k_ref, v_ref, xr_ref, gy_ref, sga_ref, sgr_ref, *, n_p):
    x = jnp.where(pl.program_id(0) < n_p, xp_ref[...], xs_ref[...]).astype(BF16)

    def seg(lo, hi):
        return jnp.dot(x, w_ref[:, lo:hi], preferred_element_type=F32)

    q_ref[...] = (seg(O_Q, O_K) * SM_SCALE).astype(BF16)
    k_ref[...] = seg(O_K, O_V)
    v_ref[...] = seg(O_V, O_XR)
    xr_ref[...] = seg(O_XR, O_YR)
    gy_ref[...] = jax.nn.gelu(seg(O_YR, O_GA)).astype(BF16)
    sga_ref[...] = jax.nn.sigmoid(seg(O_GA, O_GR)).astype(BF16)
    sgr_ref[...] = jax.nn.sigmoid(seg(O_GR, D_IN)).astype(BF16)


def _inproj(x_p, x_s, w_in_bf16):
    t_p, t_s = x_p.shape[0], x_s.shape[0]
    t = t_p + t_s
    tm = _pick_tile(math.gcd(t_p, t_s), (256, 128, 64, 32, 16, 8))
    row = lambda width: pl.BlockSpec((tm, width), lambda i: (i, 0))
    out_shape = (
        jax.ShapeDtypeStruct((t, D_ATTN), BF16),
        jax.ShapeDtypeStruct((t, D_KV), F32),
        jax.ShapeDtypeStruct((t, D_KV), F32),
        jax.ShapeDtypeStruct((t, D_RNN), F32),
        jax.ShapeDtypeStruct((t, D_RNN), BF16),
        jax.ShapeDtypeStruct((t, D_MODEL), BF16),
        jax.ShapeDtypeStruct((t, D_MODEL), BF16),
    )
    return pl.pallas_call(
        functools.partial(_inproj_kernel, n_p=t_p // tm),
        grid=(t // tm,),
        in_specs=[*_two_source_specs(t_p, t_s, tm, D_MODEL), _const_spec((D_MODEL, D_IN))],
        out_specs=(row(D_ATTN), row(D_KV), row(D_KV), row(D_RNN), row(D_RNN), row(D_MODEL), row(D_MODEL)),
        out_shape=out_shape,
        compiler_params=_params(("parallel",)),
        name="inproj",
    )(x_p, x_s, w_in_bf16)


def _t5_bucket(dist):
    n = jnp.maximum(dist, 0)
    max_exact = N_BUCKETS // 2
    nf = jnp.maximum(n, 1).astype(F32)
    large = max_exact + (jnp.log(nf / max_exact) / math.log(MAX_DISTANCE / max_exact) * (N_BUCKETS - max_exact)).astype(jnp.int32)
    large = jnp.minimum(large, N_BUCKETS - 1)
    return jnp.where(n < max_exact, n, large)


def _bias_table(rel_bias, dist):
    bucket = _t5_bucket(dist)
    rb = rel_bias.astype(F32)
    out = jnp.zeros((N_HEADS, *dist.shape), F32)
    for j in range(N_BUCKETS):
        out = jnp.where(bucket[None] == j, rb[j][:, None, None], out)
    return out


def _softmax_pv(s, sink, v):
    m = jnp.maximum(jnp.max(s, axis=-1, keepdims=True), sink)
    p = jnp.exp(s - m)
    denom = jnp.sum(p, axis=-1, keepdims=True) + jnp.exp(sink - m)
    return jnp.dot(p.astype(BF16), v, preferred_element_type=F32), denom


def _attn_prompt_kernel(sink_ref, q_ref, kc_ref, kp_ref, vc_ref, vp_ref, bias_ref, o_ref, *, q_blocks):
    keys = jnp.concatenate([kp_ref[...], kc_ref[...]], axis=0).astype(BF16)
    vals = jnp.concatenate([vp_ref[...], vc_ref[...]], axis=0).astype(BF16)
    first_table = jnp.minimum(pl.program_id(1), 1)
    for j in range(q_blocks):
        table = first_table if j == 0 else 1
        kk = keys[j * WINDOW:(j + 2) * WINDOW]
        vv = vals[j * WINDOW:(j + 2) * WINDOW]
        q = q_ref[j * WINDOW:(j + 1) * WINDOW, :]
        for g in range(N_KV_HEADS):
            kg = kk[:, g * HEAD_DIM:(g + 1) * HEAD_DIM]
            vg = vv[:, g * HEAD_DIM:(g + 1) * HEAD_DIM]
            for h in range(GROUP):
                hh = g * GROUP + h
                qh = q[:, hh * HEAD_DIM:(hh + 1) * HEAD_DIM]
                s = lax.dot_general(qh, kg, (((1,), (1,)), ((), ())), preferred_element_type=F32) + bias_ref[table, hh]
                o, denom = _softmax_pv(s, sink_ref[0, hh], vg)
                o_ref[j * WINDOW:(j + 1) * WINDOW, hh * HEAD_DIM:(hh + 1) * HEAD_DIM] = (o / denom).astype(BF16)


def _attn_prompt(q_all, k_all, v_all, bias, sinks, batch, seq):
    nb = seq // WINDOW
    q_blocks = _pick_tile(nb, (2, 1))
    steps = nb // q_blocks
    cur = lambda width: pl.BlockSpec((q_blocks * WINDOW, width), lambda b, n: (b * steps + n, 0))
    prev = lambda width: pl.BlockSpec((WINDOW, width), lambda b, n: (b * nb + jnp.maximum(n * q_blocks - 1, 0), 0))
    return pl.pallas_call(
        functools.partial(_attn_prompt_kernel, q_blocks=q_blocks),
        grid=(batch, steps),
        in_specs=[
            pl.BlockSpec(memory_space=pltpu.SMEM),
            cur(D_ATTN), cur(D_KV), prev(D_KV), cur(D_KV), prev(D_KV),
            _const_spec((2, N_HEADS, WINDOW, 2 * WINDOW)),
        ],
        out_specs=cur(D_ATTN),
        out_shape=jax.ShapeDtypeStruct((batch * seq, D_ATTN), BF16),
        compiler_params=_params(("parallel", "arbitrary")),
        name="attn_prompt",
    )(sinks.reshape(1, N_HEADS).astype(F32), q_all, k_all, k_all, v_all, v_all, bias)


def _attn_sample_kernel(q_ref, kn_ref, vn_ref, kc_ref, vc_ref, bc_ref, bn_ref, sink_ref,
                        o_ref, ko_ref, vo_ref, *, seqs, s_len):
    w = kc_ref.shape[1]
    rows_c = lax.broadcasted_iota(jnp.int32, (GROUP * s_len, w), 0) % s_len
    cols_c = lax.broadcasted_iota(jnp.int32, (GROUP * s_len, w), 1)
    dist_c = rows_c + w - cols_c
    valid_c = (dist_c >= 0) & (dist_c <= WINDOW)
    rows_n = lax.broadcasted_iota(jnp.int32, (GROUP * s_len, s_len), 0) % s_len
    cols_n = lax.broadcasted_iota(jnp.int32, (GROUP * s_len, s_len), 1)
    dist_n = rows_n - cols_n
    valid_n = (dist_n >= 0) & (dist_n <= WINDOW)
    for j in range(seqs):
        r0 = j * s_len
        qj = q_ref[r0:r0 + s_len, :]
        kc = kc_ref[j]
        vc = vc_ref[j]
        kn = kn_ref[r0:r0 + s_len, :]
        vn = vn_ref[r0:r0 + s_len, :]
        ko_ref[j, 0:w - s_len, :] = kc[s_len:, :]
        ko_ref[j, w - s_len:w, :] = kn
        vo_ref[j, 0:w - s_len, :] = vc[s_len:, :]
        vo_ref[j, w - s_len:w, :] = vn
        kcb, vcb, knb, vnb = kc.astype(BF16), vc.astype(BF16), kn.astype(BF16), vn.astype(BF16)
        for g in range(N_KV_HEADS):
            lo, hi = g * HEAD_DIM, (g + 1) * HEAD_DIM
            qs = jnp.concatenate(
                [qj[:, (g * GROUP + h) * HEAD_DIM:(g * GROUP + h + 1) * HEAD_DIM] for h in range(GROUP)], axis=0)
            nt = (((1,), (1,)), ((), ()))
            s_c = lax.dot_general(qs, kcb[:, lo:hi], nt, preferred_element_type=F32)
            s_n = lax.dot_general(qs, knb[:, lo:hi], nt, preferred_element_type=F32)
            s_c = jnp.where(valid_c, s_c + bc_ref[g], NEG_INF)
            s_n = jnp.where(valid_n, s_n + bn_ref[g], NEG_INF)
            sink = sink_ref[g]
            m = jnp.maximum(jnp.maximum(jnp.max(s_c, axis=-1, keepdims=True), jnp.max(s_n, axis=-1, keepdims=True)), sink)
            p_c = jnp.exp(s_c - m)
            p_n = jnp.exp(s_n - m)
            denom = jnp.sum(p_c, axis=-1, keepdims=True) + jnp.sum(p_n, axis=-1, keepdims=True) + jnp.exp(sink - m)
            o = jnp.dot(p_c.astype(BF16), vcb[:, lo:hi], preferred_element_type=F32)
            o = o + jnp.dot(p_n.astype(BF16), vnb[:, lo:hi], preferred_element_type=F32)
            o = (o / denom).astype(BF16)
            for h in range(GROUP):
                hh = g * GROUP + h
                o_ref[r0:r0 + s_len, hh * HEAD_DIM:(hh + 1) * HEAD_DIM] = o[h * s_len:(h + 1) * s_len, :]


def _attn_sample(q_all, k_all, v_all, cache_k, cache_v, rel_bias, sinks, row0, dec_batch, s_len):
    w = cache_k.shape[1]
    seqs = _pick_tile(dec_batch, (16, 8, 4, 2, 1))
    rows = seqs * s_len
    blk0 = row0 // rows
    assert row0 % rows == 0
    qi = jnp.arange(s_len)
    dist_c = qi[:, None] + w - jnp.arange(w)[None, :]
    dist_n = qi[:, None] - jnp.arange(s_len)[None, :]
    b_c = _bias_table(rel_bias, dist_c).reshape(N_KV_HEADS, GROUP * s_len, w)
    b_n = _bias_table(rel_bias, dist_n).reshape(N_KV_HEADS, GROUP * s_len, s_len)
    sink = jnp.broadcast_to(sinks.astype(F32).reshape(N_KV_HEADS, GROUP, 1, 1), (N_KV_HEADS, GROUP, s_len, 1))
    sink = sink.reshape(N_KV_HEADS, GROUP * s_len, 1)
    tok = lambda width: pl.BlockSpec((rows, width), lambda i: (blk0 + i, 0))
    cache = pl.BlockSpec((seqs, w, D_KV), lambda i: (i, 0, 0))
    return pl.pallas_call(
        functools.partial(_attn_sample_kernel, seqs=seqs, s_len=s_len),
        grid=(dec_batch // seqs,),
        in_specs=[
            tok(D_ATTN), tok(D_KV), tok(D_KV), cache, cache,
            _const_spec(b_c.shape), _const_spec(b_n.shape), _const_spec(sink.shape),
        ],
        out_specs=(pl.BlockSpec((rows, D_ATTN), lambda i: (i, 0)), cache, cache),
        out_shape=(
            jax.ShapeDtypeStruct((dec_batch * s_len, D_ATTN), BF16),
            jax.ShapeDtypeStruct((dec_batch, w, D_KV), F32),
            jax.ShapeDtypeStruct((dec_batch, w, D_KV), F32),
        ),
        compiler_params=_params(("parallel",)),
        name="attn_sample",
    )(q_all, k_all, v_all, cache_k, cache_v, b_c, b_n, sink)


def _softplus(z):
    return jnp.maximum(z, 0.0) + jnp.log1p(jnp.exp(-jnp.abs(z)))


def _block_gate(xcb, w_ref, b_ref):
    parts = [jnp.dot(xcb[:, n * RNN_BLOCK:(n + 1) * RNN_BLOCK], w_ref[n], preferred_element_type=F32)
             for n in range(N_RNN_BLOCKS)]
    return jax.nn.sigmoid(jnp.concatenate(parts, axis=-1) + b_ref[...])


def _lru_coeffs(xc, wa_ref, ba_ref, wx_ref, bx_ref, lam_ref, first_row_unnormalised):
    xcb = xc.astype(BF16)
    r = _block_gate(xcb, wa_ref, ba_ref)
    i = _block_gate(xcb, wx_ref, bx_ref)
    log_a = -RG_C * r * _softplus(-lam_ref[...])
    a = jnp.exp(log_a)
    mult = jnp.sqrt(-jnp.tanh(log_a) * (a * a + 1.0))
    if first_row_unnormalised is not None:
        mult = jnp.where(first_row_unnormalised, 1.0, mult)
    return a, mult * i * xc


def _scan8(a, b):
    shape = a.shape
    grouped = (shape[0] // SUBLANES, SUBLANES, shape[1])
    a = a.reshape(grouped)
    b = b.reshape(grouped)
    r8 = lax.broadcasted_iota(jnp.int32, grouped, 1)
    d = 1
    while d < SUBLANES:
        keep = r8 >= d
        a_sh = jnp.where(keep, pltpu.roll(a, d, 1), 1.0)
        b_sh = jnp.where(keep, pltpu.roll(b, d, 1), 0.0)
        b = a * b_sh + b
        a = a * a_sh
        d *= 2
    return a.reshape(shape), b.reshape(shape)


def _rnn_prompt_kernel(xr_ref, gy_ref, cw_ref, cb_ref, wa_ref, ba_ref, wx_ref, bx_ref, lam_ref,
                       o_ref, nh_ref, ext_ref, a_ref, b_ref, hc_ref, *, tl):
    l = pl.program_id(1)

    @pl.when(l == 0)
    def _():
        ext_ref[0:SUBLANES, :] = jnp.zeros((SUBLANES, D_RNN), F32)
        hc_ref[...] = jnp.zeros((1, D_RNN), F32)

    x = xr_ref[...]
    ext_ref[SUBLANES:, :] = x
    xc = cb_ref[...] + cw_ref[CONV_W - 1:CONV_W, :] * x
    for j in range(1, CONV_W):
        xc = xc + cw_ref[CONV_W - 1 - j:CONV_W - j, :] * ext_ref[SUBLANES - j:SUBLANES - j + tl, :]
    ext_ref[0:SUBLANES, :] = x[tl - SUBLANES:, :]

    row = lax.broadcasted_iota(jnp.int32, (tl, D_RNN), 0)
    a, b = _lru_coeffs(xc, wa_ref, ba_ref, wx_ref, bx_ref, lam_ref, (row == 0) & (l == 0))
    a, b = _scan8(a, b)
    a_ref[...] = a
    b_ref[...] = b

    def chunk(c, h):
        sl = pl.ds(pl.multiple_of(c * SUBLANES, SUBLANES), SUBLANES)
        hc = b_ref[sl, :] + a_ref[sl, :] * h
        b_ref[sl, :] = hc
        return hc[SUBLANES - 1:SUBLANES, :]

    h = lax.fori_loop(0, tl // SUBLANES, chunk, hc_ref[...])
    hc_ref[...] = h
    nh_ref[0] = h
    o_ref[...] = (b_ref[...] * gy_ref[...]).astype(BF16)


def _rnn_prompt(xr_all, gy_all, rnn_w, batch, seq):
    tl = _pick_tile(seq, (512, 256, 128, 64, 32, 16, 8))
    nl = seq // tl
    tok = pl.BlockSpec((tl, D_RNN), lambda b, l: (b * nl + l, 0))
    return pl.pallas_call(
        functools.partial(_rnn_prompt_kernel, tl=tl),
        grid=(batch, nl),
        in_specs=[tok, tok] + [_const_spec(w.shape) for w in rnn_w],
        out_specs=(tok, pl.BlockSpec((1, 1, D_RNN), lambda b, l: (b, 0, 0))),
        out_shape=(jax.ShapeDtypeStruct((batch * seq, D_RNN), BF16), jax.ShapeDtypeStruct((batch, 1, D_RNN), F32)),
        scratch_shapes=[
            pltpu.VMEM((tl + SUBLANES, D_RNN), F32),
            pltpu.VMEM((tl, D_RNN), F32),
            pltpu.VMEM((tl, D_RNN), F32),
            pltpu.VMEM((1, D_RNN), F32),
        ],
        compiler_params=_params(("parallel", "arbitrary")),
        name="rnn_prompt",
    )(xr_all, gy_all, *rnn_w)


def _rnn_sample_kernel(xr_ref, gy_ref, hp_ref, h0_ref, cw_ref, cb_ref, wa_ref, ba_ref, wx_ref, bx_ref, lam_ref,
                       o_ref, nh_ref, *, seqs):
    rows = seqs * SUBLANES
    x = xr_ref[...]
    hp = hp_ref[...]
    r8 = lax.broadcasted_iota(jnp.int32, (rows, D_RNN), 0) % SUBLANES
    xc = cb_ref[...] + cw_ref[CONV_W - 1:CONV_W, :] * x
    for j in range(1, CONV_W):
        shifted = jnp.where(r8 >= j, pltpu.roll(x, j, 0), pltpu.roll(hp, rows - (SUBLANES - j), 0))
        xc = xc + cw_ref[CONV_W - 1 - j:CONV_W - j, :] * shifted
    a, b = _lru_coeffs(xc, wa_ref, ba_ref, wx_ref, bx_ref, lam_ref, None)
    a, b = _scan8(a, b)
    h0 = jnp.broadcast_to(h0_ref[...][:, None, :], (seqs, SUBLANES, D_RNN)).reshape(rows, D_RNN)
    h = b + a * h0
    last = jnp.where(r8 == SUBLANES - 1, h, 0.0).reshape(seqs, SUBLANES, D_RNN)
    nh_ref[...] = jnp.sum(last, axis=1)
    o_ref[...] = (h * gy_ref[...]).astype(BF16)


def _rnn_sample(xr_all, gy_all, hist_pad, h0, rnn_w, row0, dec_batch):
    seqs = _pick_tile(dec_batch, (16, 8))
    rows = seqs * SUBLANES
    assert row0 % rows == 0
    blk0 = row0 // rows
    tok = pl.BlockSpec((rows, D_RNN), lambda i: (blk0 + i, 0))
    return pl.pallas_call(
        functools.partial(_rnn_sample_kernel, seqs=seqs),
        grid=(dec_batch // seqs,),
        in_specs=[tok, tok, pl.BlockSpec((rows, D_RNN), lambda i: (i, 0)), pl.BlockSpec((seqs, D_RNN), lambda i: (i, 0))]
        + [_const_spec(w.shape) for w in rnn_w],
        out_specs=(pl.BlockSpec((rows, D_RNN), lambda i: (i, 0)), pl.BlockSpec((seqs, D_RNN), lambda i: (i, 0))),
        out_shape=(jax.ShapeDtypeStruct((dec_batch * SUBLANES, D_RNN), BF16),
                   jax.ShapeDtypeStruct((dec_batch, D_RNN), F32)),
        compiler_params=_params(("parallel",)),
        name="rnn_sample",
    )(xr_all, gy_all, hist_pad, h0, *rnn_w)


def _layer_norm(z, g, b):
    mu = jnp.mean(z, axis=-1, keepdims=True)
    zc = z - mu
    var = jnp.mean(zc * zc, axis=-1, keepdims=True)
    return zc * lax.rsqrt(var + LN_EPS) * g + b


def _first_index_of_max(vals, iota, axis, sentinel):
    mx = jnp.max(vals, axis=axis, keepdims=True)
    return mx, jnp.min(jnp.where(vals == mx, iota, sentinel), axis=axis, keepdims=True)


def _route(scores, bias):
    t = scores.shape[1]
    grp = scores + bias
    g3 = grp.reshape(N_GROUPS, GROUP_SIZE, t)
    e_in_g = lax.broadcasted_iota(jnp.int32, g3.shape, 1)
    m1, first = _first_index_of_max(g3, e_in_g, 1, GROUP_SIZE)
    m2 = jnp.max(jnp.where(e_in_g == first, -jnp.inf, g3), axis=1, keepdims=True)
    gscore = (m1 + m2).reshape(N_GROUPS, t)
    g_iota = lax.broadcasted_iota(jnp.int32, gscore.shape, 0)
    gmask = jnp.zeros(gscore.shape, jnp.bool_)
    for _ in range(TOPK_GROUPS):
        _, gi = _first_index_of_max(gscore, g_iota, 0, N_GROUPS)
        hit = g_iota == gi
        gmask = gmask | hit
        gscore = jnp.where(hit, -jnp.inf, gscore)
    masked = jnp.where(gmask[:, None, :], g3, -jnp.inf).reshape(N_EXPERTS, t)
    e_iota = lax.broadcasted_iota(jnp.int32, masked.shape, 0)
    idx, wts, hits = [], [], []
    for _ in range(TOP_K):
        _, ei = _first_index_of_max(masked, e_iota, 0, N_EXPERTS)
        hit = e_iota == ei
        idx.append(ei)
        hits.append(hit)
        wts.append(jnp.sum(jnp.where(hit, scores, 0.0), axis=0, keepdims=True))
        masked = jnp.where(hit, -jnp.inf, masked)
    idx = jnp.concatenate(idx, axis=0)
    w = jnp.concatenate(wts, axis=0)
    w = w / jnp.sum(w, axis=0, keepdims=True) * ROUTED_SCALE
    return idx, w, hits


def _merge_kernel(xp_ref, xs_ref, aop_ref, aos_ref, rop_ref, ros_ref, sga_ref, sgr_ref, woa_ref, wor_ref, wout_ref, g1_ref, b1_ref,
                  wrt_ref, rb_ref, wsg_ref, wsu_ref, wsd_ref,
                  x1w_ref, base_ref, idx_ref, wt_ref, rank_ref, cnt_ref, carry_ref, *, n_p, tm):
    i = pl.program_id(0)

    @pl.when(i == 0)
    def _():
        carry_ref[...] = jnp.zeros(carry_ref.shape, F32)

    is_prompt = i < n_p
    x = jnp.where(is_prompt, xp_ref[...], xs_ref[...])
    pa = jnp.dot(jnp.where(is_prompt, aop_ref[...], aos_ref[...]), woa_ref[...], preferred_element_type=F32)
    pr = jnp.dot(jnp.where(is_prompt, rop_ref[...], ros_ref[...]), wor_ref[...], preferred_element_type=F32)
    merged = sga_ref[...] * pa + sgr_ref[...] * pr
    z = ALPHA * x + jnp.dot(merged.astype(BF16), wout_ref[...], preferred_element_type=F32)
    x1 = _layer_norm(z, g1_ref[...], b1_ref[...])
    words = _pack_words(x1)
    for h in range(tm // SC_ROWS):
        for s in range(ROW_TILE):
            x1w_ref[(h * ROW_TILE + s) * SC_ROWS:(h * ROW_TILE + s + 1) * SC_ROWS, :] = (
                words[h * SC_ROWS:(h + 1) * SC_ROWS, s * LANES:(s + 1) * LANES])
    x1b = x1.astype(BF16)
    u = jax.nn.silu(jnp.dot(x1b, wsg_ref[...], preferred_element_type=F32)) * jnp.dot(x1b, wsu_ref[...], preferred_element_type=F32)
    shared = jnp.dot(u.astype(BF16), wsd_ref[...], preferred_element_type=F32)
    base_ref[...] = ALPHA * x1 + shared
    logits = lax.dot_general(wrt_ref[...], x1, (((1,), (1,)), ((), ())), preferred_element_type=F32)
    idx, w, hits = _route(jax.nn.sigmoid(logits), rb_ref[...])
    idx_ref[...] = idx
    wt_ref[...] = w

    chosen = functools.reduce(jnp.logical_or, hits)
    chosen_f = jnp.where(chosen, 1.0, 0.0)
    earlier = (lax.broadcasted_iota(jnp.int32, (tm, tm), 0) < lax.broadcasted_iota(jnp.int32, (tm, tm), 1))
    prefix = jnp.dot(chosen_f.astype(BF16), jnp.where(earlier, 1.0, 0.0).astype(BF16), preferred_element_type=F32)
    before = prefix + carry_ref[...]
    ranks = [jnp.sum(jnp.where(hit, before, 0.0), axis=0, keepdims=True) for hit in hits]
    rank_ref[...] = jnp.concatenate(ranks, axis=0).astype(jnp.int32)
    carry_ref[...] = carry_ref[...] + jnp.sum(chosen_f, axis=1, keepdims=True)
    cnt_ref[...] = carry_ref[...]


def _merge(x_p, x_s, attn_p, attn_s, rnn_p, rnn_s, sga, sgr, weights):
    t_p, t_s = x_p.shape[0], x_s.shape[0]
    t = t_p + t_s
    tm = _pick_tile(math.gcd(t_p, t_s), (512, 256, 128))
    row = lambda width: pl.BlockSpec((tm, width), lambda i: (i, 0))
    col = pl.BlockSpec((TOP_K, tm), lambda i: (0, i))

    return pl.pallas_call(
        functools.partial(_merge_kernel, n_p=t_p // tm, tm=tm),
        grid=(t // tm,),
        in_specs=[*_two_source_specs(t_p, t_s, tm, D_MODEL), *_two_source_specs(t_p, t_s, tm, D_ATTN),
                  *_two_source_specs(t_p, t_s, tm, D_RNN), row(D_MODEL), row(D_MODEL)]
        + [_const_spec(w.shape) for w in weights],
        out_specs=(pl.BlockSpec((tm * ROW_TILE, LANES), lambda i: (i, 0)), row(D_MODEL), col, col, col,
                   _const_spec((N_EXPERTS, 1))),
        out_shape=(
            jax.ShapeDtypeStruct((t * ROW_TILE, LANES), WORD),
            jax.ShapeDtypeStruct((t, D_MODEL), F32),
            jax.ShapeDtypeStruct((TOP_K, t), jnp.int32),
            jax.ShapeDtypeStruct((TOP_K, t), F32),
            jax.ShapeDtypeStruct((TOP_K, t), jnp.int32),
            jax.ShapeDtypeStruct((N_EXPERTS, 1), F32),
        ),
        scratch_shapes=[pltpu.VMEM((N_EXPERTS, 1), F32)],
        compiler_params=_params(("arbitrary",)),
        name="merge_ln1_route",
    )(x_p, x_s, attn_p, attn_s, rnn_p, rnn_s, sga, sgr, *weights)


def _expert_chunk_rows(n_assign):
    return max(EXPERT_ROWS, -(-(n_assign // N_EXPERTS * 9 // 8) // EXPERT_ROWS) * EXPERT_ROWS)


def _expert_layout(counts, n_assign, chunk_rows):
    counts = counts.reshape(N_EXPERTS).astype(jnp.int32)
    padded = (counts + ROW_PAD - 1) // ROW_PAD * ROW_PAD
    pend = jnp.cumsum(padded)
    pstart = pend - padded
    rows_alloc = (n_assign + N_EXPERTS * (ROW_PAD - 1)) // ROW_PAD * ROW_PAD + chunk_rows
    n_chunks = jnp.maximum((counts + chunk_rows - 1) // chunk_rows, 1)
    part = CHUNK_PART
    written = jnp.maximum((counts + part - 1) // part, 1) * part
    ffn_end = jnp.max(pstart + written).reshape(1)
    return pstart, pstart + counts, pend, n_chunks, counts, ffn_end, rows_alloc


def _dest_kernel(idx_ref, rank_ref, pstart_ref, dest_ref, word_rows_ref):
    e_iota = lax.broadcasted_iota(jnp.int32, (N_EXPERTS, idx_ref.shape[1]), 0)
    starts = [jnp.sum(jnp.where(e_iota == idx_ref[k:k + 1, :], pstart_ref[...], 0), axis=0, keepdims=True)
              for k in range(TOP_K)]
    dest = jnp.concatenate(starts, axis=0) + rank_ref[...]
    dest_ref[...] = dest
    tm = dest.shape[1]
    word_rows_ref[0] = jnp.concatenate(
        [dest[k:k + 1, h * SC_ROWS:(h + 1) * SC_ROWS] * ROW_TILE + s
         for h in range(tm // SC_ROWS) for s in range(ROW_TILE) for k in range(TOP_K)], axis=0)


WORD_ROW_VECS = ROW_TILE * TOP_K


def _dest_rows(idx_t, rank_t, pstart):
    t = idx_t.shape[1]
    tm = _pick_tile(t, (1024, 512, 256, 128))
    col = pl.BlockSpec((TOP_K, tm), lambda i: (0, i))
    vecs = tm // SC_ROWS * WORD_ROW_VECS
    dest, word_rows = pl.pallas_call(
        _dest_kernel,
        grid=(t // tm,),
        in_specs=[col, col, _const_spec((N_EXPERTS, 1))],
        out_specs=(col, pl.BlockSpec((1, vecs, SC_ROWS), lambda i: (i, 0, 0))),
        out_shape=(jax.ShapeDtypeStruct((TOP_K, t), jnp.int32),
                   jax.ShapeDtypeStruct((t // tm, vecs, SC_ROWS), jnp.int32)),
        compiler_params=_params(("parallel",)),
        name="dest_rows",
    )(idx_t, rank_t, pstart.reshape(N_EXPERTS, 1))
    return dest, word_rows.reshape(-1, SC_ROWS)


def _token_rows(r, n=1):
    return pl.ds(pl.multiple_of(r * ROW_TILE, ROW_TILE), n * ROW_TILE)


def _zero_row_groups(zero_ref, dst_ref, sem, first_group, n_groups):
    def start(g, c):
        pltpu.make_async_copy(zero_ref, dst_ref.at[_token_rows(g * ROW_PAD, ROW_PAD)], sem).start()
        return c

    lax.fori_loop(first_group, n_groups, start, 0)
    return n_groups - first_group


def _wait_zero_copies(zero_ref, dst_ref, sem, n):
    def wait(_, c):
        pltpu.make_async_copy(zero_ref, dst_ref.at[_token_rows(0, ROW_PAD)], sem).wait()
        return c

    lax.fori_loop(0, n, wait, 0)


SC_ROWS = 128


def _sc_workers():
    info = plsc.get_sparse_core_info()
    mesh = plsc.VectorSubcoreMesh(core_axis_name="c", subcore_axis_name="s")
    worker = lambda: lax.axis_index("s") * info.num_cores + lax.axis_index("c")
    return mesh, info.num_cores * info.num_subcores, worker


def _sc_scatter_rows(src, rows, n_out):
    mesh, n_workers, worker = _sc_workers()
    n_units = src.shape[0] // SC_ROWS
    assert n_units % n_workers == 0 and rows.shape == (n_units * TOP_K, SC_ROWS)
    per_worker = n_units // n_workers

    unit_bufs = [pltpu.VMEM((TOP_K, SC_ROWS), jnp.int32), pltpu.VMEM((SC_ROWS, LANES), src.dtype),
                 pltpu.SemaphoreType.DMA]

    @functools.partial(
        pl.kernel, mesh=mesh,
        out_type=jax.ShapeDtypeStruct((n_out, LANES), src.dtype),
        scratch_types=unit_bufs + unit_bufs + [pltpu.SemaphoreType.DMA],
    )
    def scatter(src_hbm, rows_hbm, out_hbm, rows_a, data_a, load_a, rows_b, data_b, load_b, sem):
        first = worker() * per_worker
        bufs = ((rows_a, data_a, load_a), (rows_b, data_b, load_b))

        def loads(u, buf):
            rows_v, data_v, load_sem = buf
            return (pltpu.make_async_copy(rows_hbm.at[pl.ds(u * TOP_K, TOP_K)], rows_v, load_sem),
                    pltpu.make_async_copy(src_hbm.at[pl.ds(u * SC_ROWS, SC_ROWS)], data_v, load_sem))

        def unit(u, buf, other, has_next):
            for cp in loads(u, buf):
                cp.wait()

            @pl.when(has_next)
            def _():
                for cp in loads(u + 1, other):
                    cp.start()

            rows_v, data_v, _ = buf
            copies = [pltpu.async_copy(data_v, out_hbm.at[rows_v.at[k]], sem) for k in range(TOP_K)]
            for cp in copies:
                cp.wait()

        for cp in loads(first, bufs[0]):
            cp.start()

        @pl.loop(0, per_worker // 2)
        def _(pair):
            u = first + 2 * pair
            unit(u, bufs[0], bufs[1], True)
            unit(u + 1, bufs[1], bufs[0], 2 * pair + 2 < per_worker)

        if per_worker % 2:
            unit(first + per_worker - 1, bufs[0], bufs[1], False)

    return scatter(src, rows)


def _zero_padding_kernel(lo_ref, hi_ref, tail_ref, xs_in_ref, xs_ref, zero_ref, sem, *, rows_alloc):
    del xs_in_ref
    zero_ref[...] = jnp.zeros(zero_ref.shape, WORD)
    row_copy = lambda r: pltpu.make_async_copy(zero_ref.at[pl.ds(0, ROW_TILE)], xs_ref.at[_token_rows(r)], sem)

    def expert(e, n):
        def row(r, c):
            row_copy(r).start()
            return c

        lax.fori_loop(lo_ref[e], hi_ref[e], row, 0)
        return n + hi_ref[e] - lo_ref[e]

    def wait(_, c):
        row_copy(0).wait()
        return c

    lax.fori_loop(0, lax.fori_loop(0, N_EXPERTS, expert, 0), wait, 0)
    n = _zero_row_groups(zero_ref, xs_ref, sem, tail_ref[0] // ROW_PAD, rows_alloc // ROW_PAD)
    _wait_zero_copies(zero_ref, xs_ref, sem, n)


def _zero_padding(xs, pad_lo, pad_hi, total, rows_alloc):
    grid_spec = pltpu.PrefetchScalarGridSpec(
        num_scalar_prefetch=3,
        grid=(1,),
        in_specs=[pl.BlockSpec(memory_space=pl.ANY)],
        out_specs=pl.BlockSpec(memory_space=pl.ANY),
        scratch_shapes=[pltpu.VMEM((ROW_PAD * ROW_TILE, LANES), WORD), pltpu.SemaphoreType.DMA],
    )
    return pl.pallas_call(
        functools.partial(_zero_padding_kernel, rows_alloc=rows_alloc),
        grid_spec=grid_spec,
        out_shape=jax.ShapeDtypeStruct(xs.shape, xs.dtype),
        input_output_aliases={3: 0},
        compiler_params=_params(("arbitrary",)),
        name="zero_padding",
    )(pad_lo, pad_hi, total, xs)


def _expert_kernel(pstart_ref, nch_ref, rows_ref, end_ref, wg_ref, wu_ref, wd_ref, xs_ref, o_ref,
                   xbuf_ref, obuf_ref, wgb_ref, wub_ref, wdb_ref, zero_ref, done_ref, in_sem, out_sem,
                   *, rows_alloc, chunk_rows):
    e = pl.program_id(0)
    n_e = pl.num_programs(0)
    start = pstart_ref[e]
    nch = nch_ref[e]
    buf_rows = chunk_rows * ROW_TILE
    part = CHUNK_PART
    chunk_parts = chunk_rows // part
    part_rows = part * ROW_TILE

    class _ChunkCopy:
        def __init__(self, whole, parts, n_parts):
            self.whole = whole
            self.parts = parts
            self.n_parts = n_parts

        def _each(self, act):
            pl.when(self.n_parts == chunk_parts)(functools.partial(act, self.whole))
            for p, cp in enumerate(self.parts):
                pl.when((p < self.n_parts) & (self.n_parts < chunk_parts))(functools.partial(act, cp))

        def start(self):
            self._each(lambda cp: cp.start())

        def wait(self):
            self._each(lambda cp: cp.wait())

    def parts_of(rows_left):
        return jnp.clip((rows_left + part - 1) // part, 1, chunk_parts)

    def in_copy(row, slot, n_parts):
        copy = lambda p, n: pltpu.make_async_copy(
            xs_ref.at[_token_rows(row + p * part, n * part)],
            xbuf_ref.at[pl.ds(slot * buf_rows + p * part_rows, n * part_rows)], in_sem.at[slot])
        return _ChunkCopy(copy(0, chunk_parts), [copy(p, 1) for p in range(chunk_parts - 1)], n_parts)

    def out_copy(row, slot, n_parts):
        copy = lambda p, n: pltpu.make_async_copy(
            obuf_ref.at[pl.ds(slot * buf_rows + p * part_rows, n * part_rows)],
            o_ref.at[_token_rows(row + p * part, n * part)], out_sem)
        return _ChunkCopy(copy(0, chunk_parts), [copy(p, 1) for p in range(chunk_parts - 1)], n_parts)

    @pl.when(e == 0)
    def _():
        done_ref[0] = 0
        xbuf_ref[...] = jnp.zeros(xbuf_ref.shape, WORD)
        obuf_ref[...] = jnp.zeros(obuf_ref.shape, WORD)
        in_copy(start, 0, parts_of(rows_ref[0])).start()

    wgb_ref[...] = wg_ref[...].astype(BF16)
    wub_ref[...] = wu_ref[...].astype(BF16)
    wdb_ref[...] = wd_ref[...].astype(BF16)
    done = done_ref[0]

    def chunk(c, carry):
        g = done + c
        slot = g % 2
        row = start + c * chunk_rows
        rows_left = rows_ref[e] - c * chunk_rows
        n_parts = parts_of(rows_left)
        in_copy(row, slot, n_parts).wait()
        last = c + 1 == nch
        next_e = jnp.minimum(e + 1, n_e - 1)
        next_row = jnp.where(last, pstart_ref[next_e], row + chunk_rows)
        next_left = jnp.where(last, rows_ref[next_e], rows_left - chunk_rows)

        @pl.when(jnp.logical_not(last & (e == n_e - 1)))
        def _():
            in_copy(next_row, 1 - slot, parts_of(next_left)).start()

        base = pl.multiple_of(slot * buf_rows, buf_rows)
        def sub_block(h):
            xb = jnp.concatenate(_load_token_rows(xbuf_ref, EXPERT_ROWS, row0=h * EXPERT_ROWS, base=base),
                                 axis=1).astype(BF16)
            gate = jnp.dot(xb, wgb_ref[...], preferred_element_type=F32)
            up = jnp.dot(xb, wub_ref[...], preferred_element_type=F32)
            act = (jax.nn.silu(gate) * up).astype(BF16)
            _store_token_rows(obuf_ref, jnp.dot(act, wdb_ref[...], preferred_element_type=F32),
                              EXPERT_ROWS, row0=h * EXPERT_ROWS, base=base)

        n_sub = chunk_rows // EXPERT_ROWS
        needed = jnp.clip((rows_left + EXPERT_ROWS - 1) // EXPERT_ROWS, 1, n_sub)
        for count in range(1, n_sub + 1):
            @pl.when(needed == count)
            def _():
                for h in range(count):
                    sub_block(h)

        @pl.when(g > 0)
        def _():
            out_copy(0, 0, done_ref[1]).wait()

        out_copy(row, slot, n_parts).start()
        done_ref[1] = n_parts
        return carry

    lax.fori_loop(0, nch, chunk, 0)
    done_ref[0] = done + nch

    @pl.when(e == n_e - 1)
    def _():
        out_copy(0, 0, done_ref[1]).wait()
        zero_ref[...] = jnp.zeros(zero_ref.shape, WORD)
        n = _zero_row_groups(zero_ref, o_ref, out_sem, end_ref[0] // ROW_PAD, rows_alloc // ROW_PAD)
        _wait_zero_copies(zero_ref, o_ref, out_sem, n)


def _expert_ffn(xs, pstart, n_chunks, rows, ffn_end, rows_alloc, chunk_rows, w_e_gate, w_e_up, w_e_down):
    weight = lambda shape: pl.BlockSpec((None, *shape), lambda e, *_: (e, 0, 0))
    grid_spec = pltpu.PrefetchScalarGridSpec(
        num_scalar_prefetch=4,
        grid=(N_EXPERTS,),
        in_specs=[weight((D_MODEL, D_EXPERT)), weight((D_MODEL, D_EXPERT)), weight((D_EXPERT, D_MODEL)),
                  pl.BlockSpec(memory_space=pl.ANY)],
        out_specs=pl.BlockSpec(memory_space=pl.ANY),
        scratch_shapes=[
            pltpu.VMEM((2 * chunk_rows * ROW_TILE, LANES), WORD),
            pltpu.VMEM((2 * chunk_rows * ROW_TILE, LANES), WORD),
            pltpu.VMEM((D_MODEL, D_EXPERT), BF16), pltpu.VMEM((D_MODEL, D_EXPERT), BF16),
            pltpu.VMEM((D_EXPERT, D_MODEL), BF16),
            pltpu.VMEM((ROW_PAD * ROW_TILE, LANES), WORD),
            pltpu.SMEM((2,), jnp.int32),
            pltpu.SemaphoreType.DMA((2,)), pltpu.SemaphoreType.DMA,
        ],
    )
    return pl.pallas_call(
        functools.partial(_expert_kernel, rows_alloc=rows_alloc, chunk_rows=chunk_rows),
        grid_spec=grid_spec,
        out_shape=jax.ShapeDtypeStruct((rows_alloc * ROW_TILE, LANES), WORD),
        compiler_params=_params(("arbitrary",)),
        name="expert_ffn",
    )(pstart, n_chunks, rows, ffn_end, w_e_gate, w_e_up, w_e_down, xs)


def _combine_head_kernel(dest_ref, dest_next_ref, w_ref, base_ref, g_ref, b_ref, outs_ref, y_ref, buf_ref, sem, *, tm):
    i = pl.program_id(0)
    slot_rows = TOP_K * tm

    def gather(d_ref, slot):
        def issue(t, c):
            for k in range(TOP_K):
                pltpu.make_async_copy(outs_ref.at[_token_rows(d_ref[k, t])],
                                      buf_ref.at[_token_rows(slot * slot_rows + k * tm + t)], sem.at[slot]).start()
            return c

        lax.fori_loop(0, tm, issue, 0)

    @pl.when(i == 0)
    def _():
        gather(dest_ref, 0)

    @pl.when(i + 1 < pl.num_programs(0))
    def _():
        gather(dest_next_ref, (i + 1) % 2)

    slot = i % 2
    for k in range(TOP_K):
        pltpu.make_async_copy(outs_ref.at[_token_rows(0, tm)], buf_ref.at[_token_rows(slot * slot_rows + k * tm, tm)],
                              sem.at[slot]).wait()

    w = w_ref[...]
    base = pl.multiple_of(slot * slot_rows * ROW_TILE, ROW_TILE)
    halves = [None, None]
    for k in range(TOP_K):
        wk = w[:, k:k + 1]
        for j, rows in enumerate(_load_token_rows(buf_ref, tm, row0=k * tm, base=base)):
            halves[j] = wk * rows if halves[j] is None else halves[j] + wk * rows
    y_ref[...] = _layer_norm(base_ref[...] + jnp.concatenate(halves, axis=1), g_ref[...], b_ref[...])


def _combine_head(out_sorted, dest, w_tok, base, g2, b2, tm, n_head):
    row = pl.BlockSpec((tm, D_MODEL), lambda i: (i, 0))
    return pl.pallas_call(
        functools.partial(_combine_head_kernel, tm=tm),
        grid=(n_head,),
        in_specs=[
            pl.BlockSpec((TOP_K, tm), lambda i: (0, i), memory_space=pltpu.SMEM),
            pl.BlockSpec((TOP_K, tm), lambda i: (0, jnp.minimum(i + 1, n_head - 1)), memory_space=pltpu.SMEM),
            pl.BlockSpec((tm, TOP_K), lambda i: (i, 0)),
            row, _const_spec((1, D_MODEL)), _const_spec((1, D_MODEL)),
            pl.BlockSpec(memory_space=pl.ANY),
        ],
        out_specs=row,
        out_shape=jax.ShapeDtypeStruct((n_head * tm, D_MODEL), F32),
        scratch_shapes=[pltpu.VMEM((2 * TOP_K * tm * ROW_TILE, LANES), WORD), pltpu.SemaphoreType.DMA((2,))],
        compiler_params=_params(("arbitrary",)),
        name="combine_head",
    )(dest, dest, w_tok, base, g2, b2, out_sorted)


def _sc_gather_rows(table, rows):
    mesh, n_workers, worker = _sc_workers()
    m = rows.shape[0]
    per_worker = m // n_workers
    assert m % n_workers == 0 and per_worker % SC_ROWS == 0
    in_flight = _pick_tile(per_worker // SC_ROWS, (2, 1))
    step = SC_ROWS * in_flight
    n_stages = per_worker // step
    stage_bufs = [pltpu.VMEM((step,), jnp.int32), pltpu.VMEM((step, LANES), table.dtype),
                  pltpu.SemaphoreType.DMA, pltpu.SemaphoreType.DMA]

    @functools.partial(
        pl.kernel, mesh=mesh,
        out_type=jax.ShapeDtypeStruct((m, LANES), table.dtype),
        scratch_types=stage_bufs + stage_bufs,
    )
    def gather(table_hbm, rows_hbm, out_hbm, rows_a, data_a, gsem_a, ssem_a, rows_b, data_b, gsem_b, ssem_b):
        first = worker() * per_worker
        bufs = ((rows_a, data_a, gsem_a, ssem_a), (rows_b, data_b, gsem_b, ssem_b))

        def gathers(buf):
            rows_v, data_v, gsem, _ = buf
            return [pltpu.make_async_copy(table_hbm.at[rows_v.at[pl.ds(j * SC_ROWS, SC_ROWS)]],
                                          data_v.at[pl.ds(j * SC_ROWS, SC_ROWS)], gsem) for j in range(in_flight)]

        def store(off, buf):
            return pltpu.make_async_copy(buf[1], out_hbm.at[pl.ds(off, step)], buf[3])

        def stage(it, buf, other, has_next, other_stored):
            off = first + it * step

            @pl.when(has_next)
            def _():
                @pl.when(other_stored)
                def _():
                    store(off, other).wait()

                pltpu.sync_copy(rows_hbm.at[pl.ds(off + step, step)], other[0])
                for cp in gathers(other):
                    cp.start()

            for cp in gathers(buf):
                cp.wait()
            store(off, buf).start()

        pltpu.sync_copy(rows_hbm.at[pl.ds(first, step)], rows_a)
        for cp in gathers(bufs[0]):
            cp.start()

        @pl.loop(0, n_stages // 2)
        def _(pair):
            it = 2 * pair
            stage(it, bufs[0], bufs[1], True, pair > 0)
            stage(it + 1, bufs[1], bufs[0], it + 2 < n_stages, True)

        if n_stages % 2:
            stage(n_stages - 1, bufs[0], bufs[1], False, False)
        for back in range(min(2, n_stages)):
            store(first, bufs[(n_stages - 1 - back) % 2]).wait()

    return gather(table, rows)


def _combine_tail_kernel(yh_ref, rows_ref, w_ref, base_ref, g_ref, b_ref, yp_ref, ys_ref, *, tm, n_head, n_p):
    i = pl.program_id(0)

    @pl.when(i < n_head)
    def _():
        yp_ref[...] = yh_ref[...]

    @pl.when(i >= n_head)
    def _():
        w = w_ref[...]
        lo = [None] * ROW_TILE
        hi = [None] * ROW_TILE
        for k in range(TOP_K):
            wk = w[:, k:k + 1]
            for s in range(ROW_TILE):
                vec = lambda h: ((h * ROW_TILE + s) * TOP_K + k) * SC_ROWS
                words = jnp.concatenate([rows_ref[vec(h):vec(h) + SC_ROWS, :] for h in range(tm // SC_ROWS)], axis=0)
                lo_s, hi_s = _unpack_words(words)
                lo[s] = wk * lo_s if lo[s] is None else lo[s] + wk * lo_s
                hi[s] = wk * hi_s if hi[s] is None else hi[s] + wk * hi_s
        y = _layer_norm(base_ref[...] + jnp.concatenate(lo + hi, axis=1), g_ref[...], b_ref[...])

        @pl.when(i < n_p)
        def _():
            yp_ref[...] = y

        @pl.when(i >= n_p)
        def _():
            ys_ref[...] = y


def _combine_tail(y_head, gathered, w_tok, base, g2, b2, t_p, t_s, tm, n_head):
    n_p = t_p // tm
    assert n_head <= n_p
    n_tiles = (t_p + t_s) // tm
    blk = TOP_K * ROW_TILE * tm
    row = pl.BlockSpec((tm, D_MODEL), lambda i: (i, 0))
    out_p, out_s = _two_source_specs(t_p, t_s, tm, D_MODEL)
    return pl.pallas_call(
        functools.partial(_combine_tail_kernel, tm=tm, n_head=n_head, n_p=n_p),
        grid=(n_tiles,),
        in_specs=[
            pl.BlockSpec((tm, D_MODEL), lambda i: (jnp.minimum(i, n_head - 1), 0)),
            pl.BlockSpec((blk, LANES), lambda i: (jnp.maximum(i - n_head, 0), 0)),
            pl.BlockSpec((tm, TOP_K), lambda i: (i, 0)),
            row, _const_spec((1, D_MODEL)), _const_spec((1, D_MODEL)),
        ],
        out_specs=(out_p, out_s),
        out_shape=(jax.ShapeDtypeStruct((t_p, D_MODEL), F32), jax.ShapeDtypeStruct((t_s, D_MODEL), F32)),
        compiler_params=_params(("arbitrary",)),
        name="combine_tail",
    )(y_head, gathered, w_tok, base, g2, b2)


def kernel(x_prompt, x_sample, cache_k, cache_v, state_conv, state_rnn, w_in, conv_w, conv_b, w_gate_a, b_gate_a, w_gate_x, b_gate_x, lru_lambda, rel_bias, sinks, w_o_attn, w_o_rnn, w_out, ln1_g, ln1_b, w_router, router_bias, w_e_gate, w_e_up, w_e_down, w_s_gate, w_s_up, w_s_down, ln2_g, ln2_b):
    assert w_in.shape[0] == DEPTH == 1
    batch, seq, _ = x_prompt.shape
    dec_batch, s_len, _ = x_sample.shape
    w_cache = cache_k.shape[2]
    assert s_len == SUBLANES and seq % WINDOW == 0 and w_cache == WINDOW
    t_p = batch * seq
    t_s = dec_batch * s_len
    vec = lambda a: a[0].reshape(1, -1).astype(F32)

    x_p = x_prompt.reshape(t_p, D_MODEL)
    x_s = x_sample.reshape(t_s, D_MODEL)
    q, k, v, xr, gy, sga, sgr = _inproj(x_p, x_s, w_in[0].astype(BF16))

    qi = jnp.arange(WINDOW)
    dist = qi[:, None] + WINDOW - jnp.arange(2 * WINDOW)[None, :]
    band = (dist >= 0) & (dist <= WINDOW)
    has_prev = jnp.arange(2 * WINDOW)[None, :] >= WINDOW
    bias_p = _bias_table(rel_bias, dist)
    bias_p = jnp.stack([jnp.where(band & has_prev, bias_p, NEG_INF), jnp.where(band, bias_p, NEG_INF)])
    attn_p = _attn_prompt(q, k, v, bias_p, sinks[0], batch, seq)
    attn_s, k_s, v_s = _attn_sample(
        q, k, v, cache_k[0].reshape(dec_batch, w_cache, D_KV), cache_v[0].reshape(dec_batch, w_cache, D_KV),
        rel_bias, sinks[0], t_p, dec_batch, s_len)

    rnn_w = (conv_w[0], vec(conv_b), w_gate_a[0].astype(BF16), vec(b_gate_a), w_gate_x[0].astype(BF16),
             vec(b_gate_x), vec(lru_lambda))
    rnn_p, h_p = _rnn_prompt(xr, gy, rnn_w, batch, seq)
    hist_pad = jnp.pad(state_conv[0], ((0, 0), (SUBLANES - (CONV_W - 1), 0), (0, 0))).reshape(t_s, D_RNN)
    rnn_s, h_s = _rnn_sample(xr, gy, hist_pad, state_rnn[0], rnn_w, t_p, dec_batch)

    merge_w = (w_o_attn[0].astype(BF16), w_o_rnn[0].astype(BF16), w_out[0].astype(BF16), vec(ln1_g), vec(ln1_b),
               w_router[0].T, router_bias[0].reshape(N_EXPERTS, 1), w_s_gate[0].astype(BF16),
               w_s_up[0].astype(BF16), w_s_down[0].astype(BF16))
    x1w, base, idx_t, wt_t, rank_t, counts = _merge(x_p, x_s, attn_p, attn_s, rnn_p, rnn_s, sga, sgr, merge_w)

    n_assign = (t_p + t_s) * TOP_K
    chunk_rows = _expert_chunk_rows(n_assign)
    pstart, pad_lo, pad_hi, n_chunks, rows, ffn_end, rows_alloc = _expert_layout(counts, n_assign, chunk_rows)
    tm = _pick_tile(math.gcd(t_p, t_s), (256, 128))
    dest, word_rows = _dest_rows(idx_t, rank_t, pstart)
    xs = _sc_scatter_rows(x1w, word_rows, rows_alloc * ROW_TILE)
    xs = _zero_padding(xs, pad_lo, pad_hi, pad_hi[N_EXPERTS - 1:], rows_alloc)
    out_sorted = _expert_ffn(xs, pstart, n_chunks, rows, ffn_end, rows_alloc, chunk_rows,
                             w_e_gate[0], w_e_up[0], w_e_down[0])
    n_tiles = (t_p + t_s) // tm
    n_head = min(t_p // tm, n_tiles // 4)
    w_tok = wt_t.T
    g2, b2 = vec(ln2_g), vec(ln2_b)
    y_head = _combine_head(out_sorted, dest, w_tok, base, g2, b2, tm, n_head)
    gathered = _sc_gather_rows(out_sorted, word_rows[n_head * (tm // SC_ROWS) * WORD_ROW_VECS:].reshape(-1))
    y_p, y_s = _combine_tail(y_head, gathered, w_tok, base, g2, b2, t_p, t_s, tm, n_head)
    y_p = y_p.reshape(batch, seq, D_MODEL)
    y_s = y_s.reshape(dec_batch, s_len, D_MODEL)
    kv5 = lambda a, b: a.reshape(1, b, WINDOW, N_KV_HEADS, HEAD_DIM)
    tail = lambda a, n: jnp.stack([lax.slice_in_dim(a, (b + 1) * seq - n, (b + 1) * seq) for b in range(batch)])
    k_p = kv5(tail(k, WINDOW), batch)
    v_p = kv5(tail(v, WINDOW), batch)
    conv_p = tail(xr, CONV_W - 1)[None]
    conv_s = xr[t_p:].reshape(dec_batch, s_len, D_RNN)[:, s_len - (CONV_W - 1):][None]
    return (y_p, y_s, k_p, v_p, conv_p, h_p.reshape(1, batch, D_RNN),
            kv5(k_s, dec_batch), kv5(v_s, dec_batch), conv_s, h_s.reshape(1, dec_batch, D_RNN))
```

```python
import functools
import math

import jax
import jax.numpy as jnp
from jax import lax
from jax.experimental import pallas as pl
from jax.experimental.pallas import tpu as pltpu
from jax.experimental.pallas import tpu_sc as plsc

F32 = jnp.float32
BF16 = jnp.bfloat16
WORD = jnp.int32

D_MODEL = 1024
N_HEADS = 8
N_KV_HEADS = 2
HEAD_DIM = 64
GROUP = N_HEADS // N_KV_HEADS
WINDOW = 128
D_ATTN = N_HEADS * HEAD_DIM
D_KV = N_KV_HEADS * HEAD_DIM
N_BUCKETS = 32
MAX_DISTANCE = 128
D_RNN = D_MODEL
RNN_BLOCK = 256
N_RNN_BLOCKS = D_RNN // RNN_BLOCK
CONV_W = 4
RG_C = 8.0
N_EXPERTS = 256
TOP_K = 8
N_GROUPS = 8
GROUP_SIZE = N_EXPERTS // N_GROUPS
TOPK_GROUPS = 4
D_EXPERT = D_MODEL // 4
ROUTED_SCALE = 2.5
LN_EPS = 1e-5
DEPTH = 1
ALPHA = (2 * DEPTH) ** 0.25
NEG_INF = -1e30
SM_SCALE = HEAD_DIM ** -0.5
assert math.frexp(SM_SCALE)[0] == 0.5

O_Q = 0
O_K = D_ATTN
O_V = O_K + D_KV
O_XR = O_V + D_KV
O_YR = O_XR + D_RNN
O_GA = O_YR + D_RNN
O_GR = O_GA + D_MODEL
D_IN = O_GR + D_MODEL

SUBLANES = 8
VMEM_LIMIT_BYTES = 56 * 1024 * 1024
EXPERT_ROWS = 256
CHUNK_PART = 128
ROW_PAD = SUBLANES


def _params(sem):
    return pltpu.CompilerParams(dimension_semantics=sem, vmem_limit_bytes=VMEM_LIMIT_BYTES)


def _pick_tile(n, candidates):
    for c in candidates:
        if n % c == 0:
            return c
    raise ValueError(f"no tile for {n}")


def _const_spec(shape):
    nd = len(shape)
    return pl.BlockSpec(shape, lambda *_: (0,) * nd)


LANES = 128
ROW_WORDS = D_MODEL // 2
ROW_TILE = ROW_WORDS // LANES
HIGH_HALF = -65536


def _pack_words(mat):
    as_bits = lambda v: pltpu.bitcast(v.astype(BF16).astype(F32), WORD)
    return (as_bits(mat[:, ROW_WORDS:]) & HIGH_HALF) | lax.shift_right_logical(as_bits(mat[:, :ROW_WORDS]), 16)


def _unpack_words(words):
    return pltpu.bitcast(words << 16, F32), pltpu.bitcast(words & HIGH_HALF, F32)


def _store_token_rows(ref, mat, n, row0=0, base=0):
    words = _pack_words(mat)
    for s in range(ROW_TILE):
        ref[pl.ds(base + row0 * ROW_TILE + s, n, stride=ROW_TILE), :] = words[:, s * LANES:(s + 1) * LANES]


def _load_token_rows(ref, n, row0=0, base=0):
    words = jnp.concatenate(
        [ref[pl.ds(base + row0 * ROW_TILE + s, n, stride=ROW_TILE), :] for s in range(ROW_TILE)], axis=1)
    return _unpack_words(words)


def _two_source_specs(t_p, t_s, tm, width):
    n_p = t_p // tm
    assert t_p % tm == 0 and t_s % tm == 0
    return (pl.BlockSpec((tm, width), lambda i, *_: (jnp.minimum(i, n_p - 1), 0)),
            pl.BlockSpec((tm, width), lambda i, *_: (jnp.maximum(i - n_p, 0), 0)))


def _inproj_kernel(xp_ref, xs_ref, w_ref, q_ref, k_ref, v_ref, xr_ref, gy_ref, sga_ref, sgr_ref, *, n_p):
    x = jnp.where(pl.program_id(0) < n_p, xp_ref[...], xs_ref[...]).astype(BF16)

    def seg(lo, hi):
        return jnp.dot(x, w_ref[:, lo:hi], preferred_element_type=F32)

    q_ref[...] = (seg(O_Q, O_K) * SM_SCALE).astype(BF16)
    k_ref[...] = seg(O_K, O_V)
    v_ref[...] = seg(O_V, O_XR)
    xr_ref[...] = seg(O_XR, O_YR)
    gy_ref[...] = jax.nn.gelu(seg(O_YR, O_GA)).astype(BF16)
    sga_ref[...] = jax.nn.sigmoid(seg(O_GA, O_GR)).astype(BF16)
    sgr_ref[...] = jax.nn.sigmoid(seg(O_GR, D_IN)).astype(BF16)


def _inproj(x_p, x_s, w_in_bf16):
    t_p, t_s = x_p.shape[0], x_s.shape[0]
    t = t_p + t_s
    tm = _pick_tile(math.gcd(t_p, t_s), (256, 128, 64, 32, 16, 8))
    row = lambda width: pl.BlockSpec((tm, width), lambda i: (i, 0))
    out_shape = (
        jax.ShapeDtypeStruct((t, D_ATTN), BF16),
        jax.ShapeDtypeStruct((t, D_KV), F32),
        jax.ShapeDtypeStruct((t, D_KV), F32),
        jax.ShapeDtypeStruct((t, D_RNN), F32),
        jax.ShapeDtypeStruct((t, D_RNN), BF16),
        jax.ShapeDtypeStruct((t, D_MODEL), BF16),
        jax.ShapeDtypeStruct((t, D_MODEL), BF16),
    )
    return pl.pallas_call(
        functools.partial(_inproj_kernel, n_p=t_p // tm),
        grid=(t // tm,),
        in_specs=[*_two_source_specs(t_p, t_s, tm, D_MODEL), _const_spec((D_MODEL, D_IN))],
        out_specs=(row(D_ATTN), row(D_KV), row(D_KV), row(D_RNN), row(D_RNN), row(D_MODEL), row(D_MODEL)),
        out_shape=out_shape,
        compiler_params=_params(("parallel",)),
        name="inproj",
    )(x_p, x_s, w_in_bf16)


def _t5_bucket(dist):
    n = jnp.maximum(dist, 0)
    max_exact = N_BUCKETS // 2
    nf = jnp.maximum(n, 1).astype(F32)
    large = max_exact + (jnp.log(nf / max_exact) / math.log(MAX_DISTANCE / max_exact) * (N_BUCKETS - max_exact)).astype(jnp.int32)
    large = jnp.minimum(large, N_BUCKETS - 1)
    return jnp.where(n < max_exact, n, large)


def _bias_table(rel_bias, dist):
    bucket = _t5_bucket(dist)
    rb = rel_bias.astype(F32)
    out = jnp.zeros((N_HEADS, *dist.shape), F32)
    for j in range(N_BUCKETS):
        out = jnp.where(bucket[None] == j, rb[j][:, None, None], out)
    return out


def _softmax_pv(s, sink, v):
    m = jnp.maximum(jnp.max(s, axis=-1, keepdims=True), sink)
    p = jnp.exp(s - m)
    denom = jnp.sum(p, axis=-1, keepdims=True) + jnp.exp(sink - m)
    return jnp.dot(p.astype(BF16), v, preferred_element_type=F32), denom


def _attn_prompt_kernel(sink_ref, q_ref, kc_ref, kp_ref, vc_ref, vp_ref, bias_ref, o_ref, *, q_blocks):
    keys = jnp.concatenate([kp_ref[...], kc_ref[...]], axis=0).astype(BF16)
    vals = jnp.concatenate([vp_ref[...], vc_ref[...]], axis=0).astype(BF16)
    first_table = jnp.minimum(pl.program_id(1), 1)
    for j in range(q_blocks):
        table = first_table if j == 0 else 1
        kk = keys[j * WINDOW:(j + 2) * WINDOW]
        vv = vals[j * WINDOW:(j + 2) * WINDOW]
        q = q_ref[j * WINDOW:(j + 1) * WINDOW, :]
        for g in range(N_KV_HEADS):
            kg = kk[:, g * HEAD_DIM:(g + 1) * HEAD_DIM]
            vg = vv[:, g * HEAD_DIM:(g + 1) * HEAD_DIM]
            for h in range(GROUP):
                hh = g * GROUP + h
                qh = q[:, hh * HEAD_DIM:(hh + 1) * HEAD_DIM]
                s = lax.dot_general(qh, kg, (((1,), (1,)), ((), ())), preferred_element_type=F32) + bias_ref[table, hh]
                o, denom = _softmax_pv(s, sink_ref[0, hh], vg)
                o_ref[j * WINDOW:(j + 1) * WINDOW, hh * HEAD_DIM:(hh + 1) * HEAD_DIM] = (o / denom).astype(BF16)


def _attn_prompt(q_all, k_all, v_all, bias, sinks, batch, seq):
    nb = seq // WINDOW
    q_blocks = _pick_tile(nb, (2, 1))
    steps = nb // q_blocks
    cur = lambda width: pl.BlockSpec((q_blocks * WINDOW, width), lambda b, n: (b * steps + n, 0))
    prev = lambda width: pl.BlockSpec((WINDOW, width), lambda b, n: (b * nb + jnp.maximum(n * q_blocks - 1, 0), 0))
    return pl.pallas_call(
        functools.partial(_attn_prompt_kernel, q_blocks=q_blocks),
        grid=(batch, steps),
        in_specs=[
            pl.BlockSpec(memory_space=pltpu.SMEM),
            cur(D_ATTN), cur(D_KV), prev(D_KV), cur(D_KV), prev(D_KV),
            _const_spec((2, N_HEADS, WINDOW, 2 * WINDOW)),
        ],
        out_specs=cur(D_ATTN),
        out_shape=jax.ShapeDtypeStruct((batch * seq, D_ATTN), BF16),
        compiler_params=_params(("parallel", "arbitrary")),
        name="attn_prompt",
    )(sinks.reshape(1, N_HEADS).astype(F32), q_all, k_all, k_all, v_all, v_all, bias)


def _attn_sample_kernel(q_ref, kn_ref, vn_ref, kc_ref, vc_ref, bc_ref, bn_ref, sink_ref,
                        o_ref, ko_ref, vo_ref, *, seqs, s_len):
    w = kc_ref.shape[1]
    rows_c = lax.broadcasted_iota(jnp.int32, (GROUP * s_len, w), 0) % s_len
    cols_c = lax.broadcasted_iota(jnp.int32, (GROUP * s_len, w), 1)
    dist_c = rows_c + w - cols_c
    valid_c = (dist_c >= 0) & (dist_c <= WINDOW)
    rows_n = lax.broadcasted_iota(jnp.int32, (GROUP * s_len, s_len), 0) % s_len
    cols_n = lax.broadcasted_iota(jnp.int32, (GROUP * s_len, s_len), 1)
    dist_n = rows_n - cols_n
    valid_n = (dist_n >= 0) & (dist_n <= WINDOW)
    for j in range(seqs):
        r0 = j * s_len
        qj = q_ref[r0:r0 + s_len, :]
        kc = kc_ref[j]
        vc = vc_ref[j]
        kn = kn_ref[r0:r0 + s_len, :]
        vn = vn_ref[r0:r0 + s_len, :]
        ko_ref[j, 0:w - s_len, :] = kc[s_len:, :]
        ko_ref[j, w - s_len:w, :] = kn
        vo_ref[j, 0:w - s_len, :] = vc[s_len:, :]
        vo_ref[j, w - s_len:w, :] = vn
        kcb, vcb, knb, vnb = kc.astype(BF16), vc.astype(BF16), kn.astype(BF16), vn.astype(BF16)
        for g in range(N_KV_HEADS):
            lo, hi = g * HEAD_DIM, (g + 1) * HEAD_DIM
            qs = jnp.concatenate(
                [qj[:, (g * GROUP + h) * HEAD_DIM:(g * GROUP + h + 1) * HEAD_DIM] for h in range(GROUP)], axis=0)
            nt = (((1,), (1,)), ((), ()))
            s_c = lax.dot_general(qs, kcb[:, lo:hi], nt, preferred_element_type=F32)
            s_n = lax.dot_general(qs, knb[:, lo:hi], nt, preferred_element_type=F32)
            s_c = jnp.where(valid_c, s_c + bc_ref[g], NEG_INF)
            s_n = jnp.where(valid_n, s_n + bn_ref[g], NEG_INF)
            sink = sink_ref[g]
            m = jnp.maximum(jnp.maximum(jnp.max(s_c, axis=-1, keepdims=True), jnp.max(s_n, axis=-1, keepdims=True)), sink)
            p_c = jnp.exp(s_c - m)
            p_n = jnp.exp(s_n - m)
            denom = jnp.sum(p_c, axis=-1, keepdims=True) + jnp.sum(p_n, axis=-1, keepdims=True) + jnp.exp(sink - m)
            o = jnp.dot(p_c.astype(BF16), vcb[:, lo:hi], preferred_element_type=F32)
            o = o + jnp.dot(p_n.astype(BF16), vnb[:, lo:hi], preferred_element_type=F32)
            o = (o / denom).astype(BF16)
            for h in range(GROUP):
                hh = g * GROUP + h
                o_ref[r0:r0 + s_len, hh * HEAD_DIM:(hh + 1) * HEAD_DIM] = o[h * s_len:(h + 1) * s_len, :]


def _attn_sample(q_all, k_all, v_all, cache_k, cache_v, rel_bias, sinks, row0, dec_batch, s_len):
    w = cache_k.shape[1]
    seqs = _pick_tile(dec_batch, (16, 8, 4, 2, 1))
    rows = seqs * s_len
    blk0 = row0 // rows
    assert row0 % rows == 0
    qi = jnp.arange(s_len)
    dist_c = qi[:, None] + w - jnp.arange(w)[None, :]
    dist_n = qi[:, None] - jnp.arange(s_len)[None, :]
    b_c = _bias_table(rel_bias, dist_c).reshape(N_KV_HEADS, GROUP * s_len, w)
    b_n = _bias_table(rel_bias, dist_n).reshape(N_KV_HEADS, GROUP * s_len, s_len)
    sink = jnp.broadcast_to(sinks.astype(F32).reshape(N_KV_HEADS, GROUP, 1, 1), (N_KV_HEADS, GROUP, s_len, 1))
    sink = sink.reshape(N_KV_HEADS, GROUP * s_len, 1)
    tok = lambda width: pl.BlockSpec((rows, width), lambda i: (blk0 + i, 0))
    cache = pl.BlockSpec((seqs, w, D_KV), lambda i: (i, 0, 0))
    return pl.pallas_call(
        functools.partial(_attn_sample_kernel, seqs=seqs, s_len=s_len),
        grid=(dec_batch // seqs,),
        in_specs=[
            tok(D_ATTN), tok(D_KV), tok(D_KV), cache, cache,
            _const_spec(b_c.shape), _const_spec(b_n.shape), _const_spec(sink.shape),
        ],
        out_specs=(pl.BlockSpec((rows, D_ATTN), lambda i: (i, 0)), cache, cache),
        out_shape=(
            jax.ShapeDtypeStruct((dec_batch * s_len, D_ATTN), BF16),
            jax.ShapeDtypeStruct((dec_batch, w, D_KV), F32),
            jax.ShapeDtypeStruct((dec_batch, w, D_KV), F32),
        ),
        compiler_params=_params(("parallel",)),
        name="attn_sample",
    )(q_all, k_all, v_all, cache_k, cache_v, b_c, b_n, sink)


def _softplus(z):
    return jnp.maximum(z, 0.0) + jnp.log1p(jnp.exp(-jnp.abs(z)))


def _block_gate(xcb, w_ref, b_ref):
    parts = [jnp.dot(xcb[:, n * RNN_BLOCK:(n + 1) * RNN_BLOCK], w_ref[n], preferred_element_type=F32)
             for n in range(N_RNN_BLOCKS)]
    return jax.nn.sigmoid(jnp.concatenate(parts, axis=-1) + b_ref[...])


def _lru_coeffs(xc, wa_ref, ba_ref, wx_ref, bx_ref, lam_ref, first_row_unnormalised):
    xcb = xc.astype(BF16)
    r = _block_gate(xcb, wa_ref, ba_ref)
    i = _block_gate(xcb, wx_ref, bx_ref)
    log_a = -RG_C * r * _softplus(-lam_ref[...])
    a = jnp.exp(log_a)
    mult = jnp.sqrt(-jnp.tanh(log_a) * (a * a + 1.0))
    if first_row_unnormalised is not None:
        mult = jnp.where(first_row_unnormalised, 1.0, mult)
    return a, mult * i * xc


def _scan8(a, b):
    shape = a.shape
    grouped = (shape[0] // SUBLANES, SUBLANES, shape[1])
    a = a.reshape(grouped)
    b = b.reshape(grouped)
    r8 = lax.broadcasted_iota(jnp.int32, grouped, 1)
    d = 1
    while d < SUBLANES:
        keep = r8 >= d
        a_sh = jnp.where(keep, pltpu.roll(a, d, 1), 1.0)
        b_sh = jnp.where(keep, pltpu.roll(b, d, 1), 0.0)
        b = a * b_sh + b
        a = a * a_sh
        d *= 2
    return a.reshape(shape), b.reshape(shape)


def _rnn_prompt_kernel(xr_ref, gy_ref, cw_ref, cb_ref, wa_ref, ba_ref, wx_ref, bx_ref, lam_ref,
                       o_ref, nh_ref, ext_ref, a_ref, b_ref, hc_ref, *, tl):
    l = pl.program_id(1)

    @pl.when(l == 0)
    def _():
        ext_ref[0:SUBLANES, :] = jnp.zeros((SUBLANES, D_RNN), F32)
        hc_ref[...] = jnp.zeros((1, D_RNN), F32)

    x = xr_ref[...]
    ext_ref[SUBLANES:, :] = x
    xc = cb_ref[...] + cw_ref[CONV_W - 1:CONV_W, :] * x
    for j in range(1, CONV_W):
        xc = xc + cw_ref[CONV_W - 1 - j:CONV_W - j, :] * ext_ref[SUBLANES - j:SUBLANES - j + tl, :]
    ext_ref[0:SUBLANES, :] = x[tl - SUBLANES:, :]

    row = lax.broadcasted_iota(jnp.int32, (tl, D_RNN), 0)
    a, b = _lru_coeffs(xc, wa_ref, ba_ref, wx_ref, bx_ref, lam_ref, (row == 0) & (l == 0))
    a, b = _scan8(a, b)
    a_ref[...] = a
    b_ref[...] = b

    def chunk(c, h):
        sl = pl.ds(pl.multiple_of(c * SUBLANES, SUBLANES), SUBLANES)
        hc = b_ref[sl, :] + a_ref[sl, :] * h
        b_ref[sl, :] = hc
        return hc[SUBLANES - 1:SUBLANES, :]

    h = lax.fori_loop(0, tl // SUBLANES, chunk, hc_ref[...])
    hc_ref[...] = h
    nh_ref[0] = h
    o_ref[...] = (b_ref[...] * gy_ref[...]).astype(BF16)


def _rnn_prompt(xr_all, gy_all, rnn_w, batch, seq):
    tl = _pick_tile(seq, (512, 256, 128, 64, 32, 16, 8))
    nl = seq // tl
    tok = pl.BlockSpec((tl, D_RNN), lambda b, l: (b * nl + l, 0))
    return pl.pallas_call(
        functools.partial(_rnn_prompt_kernel, tl=tl),
        grid=(batch, nl),
        in_specs=[tok, tok] + [_const_spec(w.shape) for w in rnn_w],
        out_specs=(tok, pl.BlockSpec((1, 1, D_RNN), lambda b, l: (b, 0, 0))),
        out_shape=(jax.ShapeDtypeStruct((batch * seq, D_RNN), BF16), jax.ShapeDtypeStruct((batch, 1, D_RNN), F32)),
        scratch_shapes=[
            pltpu.VMEM((tl + SUBLANES, D_RNN), F32),
            pltpu.VMEM((tl, D_RNN), F32),
            pltpu.VMEM((tl, D_RNN), F32),
            pltpu.VMEM((1, D_RNN), F32),
        ],
        compiler_params=_params(("parallel", "arbitrary")),
        name="rnn_prompt",
    )(xr_all, gy_all, *rnn_w)


def _rnn_sample_kernel(xr_ref, gy_ref, hp_ref, h0_ref, cw_ref, cb_ref, wa_ref, ba_ref, wx_ref, bx_ref, lam_ref,
                       o_ref, nh_ref, *, seqs):
    rows = seqs * SUBLANES
    x = xr_ref[...]
    hp = hp_ref[...]
    r8 = lax.broadcasted_iota(jnp.int32, (rows, D_RNN), 0) % SUBLANES
    xc = cb_ref[...] + cw_ref[CONV_W - 1:CONV_W, :] * x
    for j in range(1, CONV_W):
        shifted = jnp.where(r8 >= j, pltpu.roll(x, j, 0), pltpu.roll(hp, rows - (SUBLANES - j), 0))
        xc = xc + cw_ref[CONV_W - 1 - j:CONV_W - j, :] * shifted
    a, b = _lru_coeffs(xc, wa_ref, ba_ref, wx_ref, bx_ref, lam_ref, None)
    a, b = _scan8(a, b)
    h0 = jnp.broadcast_to(h0_ref[...][:, None, :], (seqs, SUBLANES, D_RNN)).reshape(rows, D_RNN)
    h = b + a * h0
    last = jnp.where(r8 == SUBLANES - 1, h, 0.0).reshape(seqs, SUBLANES, D_RNN)
    nh_ref[...] = jnp.sum(last, axis=1)
    o_ref[...] = (h * gy_ref[...]).astype(BF16)


def _rnn_sample(xr_all, gy_all, hist_pad, h0, rnn_w, row0, dec_batch):
    seqs = _pick_tile(dec_batch, (16, 8))
    rows = seqs * SUBLANES
    assert row0 % rows == 0
    blk0 = row0 // rows
    tok = pl.BlockSpec((rows, D_RNN), lambda i: (blk0 + i, 0))
    return pl.pallas_call(
        functools.partial(_rnn_sample_kernel, seqs=seqs),
        grid=(dec_batch // seqs,),
        in_specs=[tok, tok, pl.BlockSpec((rows, D_RNN), lambda i: (i, 0)), pl.BlockSpec((seqs, D_RNN), lambda i: (i, 0))]
        + [_const_spec(w.shape) for w in rnn_w],
        out_specs=(pl.BlockSpec((rows, D_RNN), lambda i: (i, 0)), pl.BlockSpec((seqs, D_RNN), lambda i: (i, 0))),
        out_shape=(jax.ShapeDtypeStruct((dec_batch * SUBLANES, D_RNN), BF16),
                   jax.ShapeDtypeStruct((dec_batch, D_RNN), F32)),
        compiler_params=_params(("parallel",)),
        name="rnn_sample",
    )(xr_all, gy_all, hist_pad, h0, *rnn_w)


def _layer_norm(z, g, b):
    mu = jnp.mean(z, axis=-1, keepdims=True)
    zc = z - mu
    var = jnp.mean(zc * zc, axis=-1, keepdims=True)
    return zc * lax.rsqrt(var + LN_EPS) * g + b


def _first_index_of_max(vals, iota, axis, sentinel):
    mx = jnp.max(vals, axis=axis, keepdims=True)
    return mx, jnp.min(jnp.where(vals == mx, iota, sentinel), axis=axis, keepdims=True)


def _route(scores, bias):
    t = scores.shape[1]
    grp = scores + bias
    g3 = grp.reshape(N_GROUPS, GROUP_SIZE, t)
    e_in_g = lax.broadcasted_iota(jnp.int32, g3.shape, 1)
    m1, first = _first_index_of_max(g3, e_in_g, 1, GROUP_SIZE)
    m2 = jnp.max(jnp.where(e_in_g == first, -jnp.inf, g3), axis=1, keepdims=True)
    gscore = (m1 + m2).reshape(N_GROUPS, t)
    g_iota = lax.broadcasted_iota(jnp.int32, gscore.shape, 0)
    gmask = jnp.zeros(gscore.shape, jnp.bool_)
    for _ in range(TOPK_GROUPS):
        _, gi = _first_index_of_max(gscore, g_iota, 0, N_GROUPS)
        hit = g_iota == gi
        gmask = gmask | hit
        gscore = jnp.where(hit, -jnp.inf, gscore)
    masked = jnp.where(gmask[:, None, :], g3, -jnp.inf).reshape(N_EXPERTS, t)
    e_iota = lax.broadcasted_iota(jnp.int32, masked.shape, 0)
    idx, wts, hits = [], [], []
    for _ in range(TOP_K):
        _, ei = _first_index_of_max(masked, e_iota, 0, N_EXPERTS)
        hit = e_iota == ei
        idx.append(ei)
        hits.append(hit)
        wts.append(jnp.sum(jnp.where(hit, scores, 0.0), axis=0, keepdims=True))
        masked = jnp.where(hit, -jnp.inf, masked)
    idx = jnp.concatenate(idx, axis=0)
    w = jnp.concatenate(wts, axis=0)
    w = w / jnp.sum(w, axis=0, keepdims=True) * ROUTED_SCALE
    return idx, w, hits


def _merge_kernel(xp_ref, xs_ref, aop_ref, aos_ref, rop_ref, ros_ref, sga_ref, sgr_ref, woa_ref, wor_ref, wout_ref, g1_ref, b1_ref,
                  wrt_ref, rb_ref, wsg_ref, wsu_ref, wsd_ref,
                  x1w_ref, base_ref, idx_ref, wt_ref, rank_ref, cnt_ref, carry_ref, *, n_p, tm):
    i = pl.program_id(0)

    @pl.when(i == 0)
    def _():
        carry_ref[...] = jnp.zeros(carry_ref.shape, F32)

    is_prompt = i < n_p
    x = jnp.where(is_prompt, xp_ref[...], xs_ref[...])
    pa = jnp.dot(jnp.where(is_prompt, aop_ref[...], aos_ref[...]), woa_ref[...], preferred_element_type=F32)
    pr = jnp.dot(jnp.where(is_prompt, rop_ref[...], ros_ref[...]), wor_ref[...], preferred_element_type=F32)
    merged = sga_ref[...] * pa + sgr_ref[...] * pr
    z = ALPHA * x + jnp.dot(merged.astype(BF16), wout_ref[...], preferred_element_type=F32)
    x1 = _layer_norm(z, g1_ref[...], b1_ref[...])
    words = _pack_words(x1)
    for h in range(tm // SC_ROWS):
        for s in range(ROW_TILE):
            x1w_ref[(h * ROW_TILE + s) * SC_ROWS:(h * ROW_TILE + s + 1) * SC_ROWS, :] = (
                words[h * SC_ROWS:(h + 1) * SC_ROWS, s * LANES:(s + 1) * LANES])
    x1b = x1.astype(BF16)
    u = jax.nn.silu(jnp.dot(x1b, wsg_ref[...], preferred_element_type=F32)) * jnp.dot(x1b, wsu_ref[...], preferred_element_type=F32)
    shared = jnp.dot(u.astype(BF16), wsd_ref[...], preferred_element_type=F32)
    base_ref[...] = ALPHA * x1 + shared
    logits = lax.dot_general(wrt_ref[...], x1, (((1,), (1,)), ((), ())), preferred_element_type=F32)
    idx, w, hits = _route(jax.nn.sigmoid(logits), rb_ref[...])
    idx_ref[...] = idx
    wt_ref[...] = w

    chosen = functools.reduce(jnp.logical_or, hits)
    chosen_f = jnp.where(chosen, 1.0, 0.0)
    earlier = (lax.broadcasted_iota(jnp.int32, (tm, tm), 0) < lax.broadcasted_iota(jnp.int32, (tm, tm), 1))
    prefix = jnp.dot(chosen_f.astype(BF16), jnp.where(earlier, 1.0, 0.0).astype(BF16), preferred_element_type=F32)
    before = prefix + carry_ref[...]
    ranks = [jnp.sum(jnp.where(hit, before, 0.0), axis=0, keepdims=True) for hit in hits]
    rank_ref[...] = jnp.concatenate(ranks, axis=0).astype(jnp.int32)
    carry_ref[...] = carry_ref[...] + jnp.sum(chosen_f, axis=1, keepdims=True)
    cnt_ref[...] = carry_ref[...]


def _merge(x_p, x_s, attn_p, attn_s, rnn_p, rnn_s, sga, sgr, weights):
    t_p, t_s = x_p.shape[0], x_s.shape[0]
    t = t_p + t_s
    tm = _pick_tile(math.gcd(t_p, t_s), (512, 256, 128))
    row = lambda width: pl.BlockSpec((tm, width), lambda i: (i, 0))
    col = pl.BlockSpec((TOP_K, tm), lambda i: (0, i))

    return pl.pallas_call(
        functools.partial(_merge_kernel, n_p=t_p // tm, tm=tm),
        grid=(t // tm,),
        in_specs=[*_two_source_specs(t_p, t_s, tm, D_MODEL), *_two_source_specs(t_p, t_s, tm, D_ATTN),
                  *_two_source_specs(t_p, t_s, tm, D_RNN), row(D_MODEL), row(D_MODEL)]
        + [_const_spec(w.shape) for w in weights],
        out_specs=(pl.BlockSpec((tm * ROW_TILE, LANES), lambda i: (i, 0)), row(D_MODEL), col, col, col,
                   _const_spec((N_EXPERTS, 1))),
        out_shape=(
            jax.ShapeDtypeStruct((t * ROW_TILE, LANES), WORD),
            jax.ShapeDtypeStruct((t, D_MODEL), F32),
            jax.ShapeDtypeStruct((TOP_K, t), jnp.int32),
            jax.ShapeDtypeStruct((TOP_K, t), F32),
            jax.ShapeDtypeStruct((TOP_K, t), jnp.int32),
            jax.ShapeDtypeStruct((N_EXPERTS, 1), F32),
        ),
        scratch_shapes=[pltpu.VMEM((N_EXPERTS, 1), F32)],
        compiler_params=_params(("arbitrary",)),
        name="merge_ln1_route",
    )(x_p, x_s, attn_p, attn_s, rnn_p, rnn_s, sga, sgr, *weights)


def _expert_chunk_rows(n_assign):
    return max(EXPERT_ROWS, -(-(n_assign // N_EXPERTS * 9 // 8) // EXPERT_ROWS) * EXPERT_ROWS)


def _expert_layout(counts, n_assign, chunk_rows):
    counts = counts.reshape(N_EXPERTS).astype(jnp.int32)
    padded = (counts + ROW_PAD - 1) // ROW_PAD * ROW_PAD
    pend = jnp.cumsum(padded)
    pstart = pend - padded
    rows_alloc = (n_assign + N_EXPERTS * (ROW_PAD - 1)) // ROW_PAD * ROW_PAD + chunk_rows
    n_chunks = jnp.maximum((counts + chunk_rows - 1) // chunk_rows, 1)
    part = CHUNK_PART
    written = jnp.maximum((counts + part - 1) // part, 1) * part
    ffn_end = jnp.max(pstart + written).reshape(1)
    return pstart, pstart + counts, pend, n_chunks, counts, ffn_end, rows_alloc


def _dest_kernel(idx_ref, rank_ref, pstart_ref, dest_ref, word_rows_ref):
    e_iota = lax.broadcasted_iota(jnp.int32, (N_EXPERTS, idx_ref.shape[1]), 0)
    starts = [jnp.sum(jnp.where(e_iota == idx_ref[k:k + 1, :], pstart_ref[...], 0), axis=0, keepdims=True)
              for k in range(TOP_K)]
    dest = jnp.concatenate(starts, axis=0) + rank_ref[...]
    dest_ref[...] = dest
    tm = dest.shape[1]
    word_rows_ref[0] = jnp.concatenate(
        [dest[k:k + 1, h * SC_ROWS:(h + 1) * SC_ROWS] * ROW_TILE + s
         for h in range(tm // SC_ROWS) for s in range(ROW_TILE) for k in range(TOP_K)], axis=0)


WORD_ROW_VECS = ROW_TILE * TOP_K


def _dest_rows(idx_t, rank_t, pstart):
    t = idx_t.shape[1]
    tm = _pick_tile(t, (1024, 512, 256, 128))
    col = pl.BlockSpec((TOP_K, tm), lambda i: (0, i))
    vecs = tm // SC_ROWS * WORD_ROW_VECS
    dest, word_rows = pl.pallas_call(
        _dest_kernel,
        grid=(t // tm,),
        in_specs=[col, col, _const_spec((N_EXPERTS, 1))],
        out_specs=(col, pl.BlockSpec((1, vecs, SC_ROWS), lambda i: (i, 0, 0))),
        out_shape=(jax.ShapeDtypeStruct((TOP_K, t), jnp.int32),
                   jax.ShapeDtypeStruct((t // tm, vecs, SC_ROWS), jnp.int32)),
        compiler_params=_params(("parallel",)),
        name="dest_rows",
    )(idx_t, rank_t, pstart.reshape(N_EXPERTS, 1))
    return dest, word_rows.reshape(-1, SC_ROWS)


def _token_rows(r, n=1):
    return pl.ds(pl.multiple_of(r * ROW_TILE, ROW_TILE), n * ROW_TILE)


def _zero_row_groups(zero_ref, dst_ref, sem, first_group, n_groups):
    def start(g, c):
        pltpu.make_async_copy(zero_ref, dst_ref.at[_token_rows(g * ROW_PAD, ROW_PAD)], sem).start()
        return c

    lax.fori_loop(first_group, n_groups, start, 0)
    return n_groups - first_group


def _wait_zero_copies(zero_ref, dst_ref, sem, n):
    def wait(_, c):
        pltpu.make_async_copy(zero_ref, dst_ref.at[_token_rows(0, ROW_PAD)], sem).wait()
        return c

    lax.fori_loop(0, n, wait, 0)


SC_ROWS = 128


def _sc_workers():
    info = plsc.get_sparse_core_info()
    mesh = plsc.VectorSubcoreMesh(core_axis_name="c", subcore_axis_name="s")
    worker = lambda: lax.axis_index("s") * info.num_cores + lax.axis_index("c")
    return mesh, info.num_cores * info.num_subcores, worker


def _sc_scatter_rows(src, rows, n_out):
    mesh, n_workers, worker = _sc_workers()
    n_units = src.shape[0] // SC_ROWS
    assert n_units % n_workers == 0 and rows.shape == (n_units * TOP_K, SC_ROWS)
    per_worker = n_units // n_workers

    unit_bufs = [pltpu.VMEM((TOP_K, SC_ROWS), jnp.int32), pltpu.VMEM((SC_ROWS, LANES), src.dtype),
                 pltpu.SemaphoreType.DMA]

    @functools.partial(
        pl.kernel, mesh=mesh,
        out_type=jax.ShapeDtypeStruct((n_out, LANES), src.dtype),
        scratch_types=unit_bufs + unit_bufs + [pltpu.SemaphoreType.DMA],
    )
    def scatter(src_hbm, rows_hbm, out_hbm, rows_a, data_a, load_a, rows_b, data_b, load_b, sem):
        first = worker() * per_worker
        bufs = ((rows_a, data_a, load_a), (rows_b, data_b, load_b))

        def loads(u, buf):
            rows_v, data_v, load_sem = buf
            return (pltpu.make_async_copy(rows_hbm.at[pl.ds(u * TOP_K, TOP_K)], rows_v, load_sem),
                    pltpu.make_async_copy(src_hbm.at[pl.ds(u * SC_ROWS, SC_ROWS)], data_v, load_sem))

        def unit(u, buf, other, has_next):
            for cp in loads(u, buf):
                cp.wait()

            @pl.when(has_next)
            def _():
                for cp in loads(u + 1, other):
                    cp.start()

            rows_v, data_v, _ = buf
            copies = [pltpu.async_copy(data_v, out_hbm.at[rows_v.at[k]], sem) for k in range(TOP_K)]
            for cp in copies:
                cp.wait()

        for cp in loads(first, bufs[0]):
            cp.start()

        @pl.loop(0, per_worker // 2)
        def _(pair):
            u = first + 2 * pair
            unit(u, bufs[0], bufs[1], True)
            unit(u + 1, bufs[1], bufs[0], 2 * pair + 2 < per_worker)

        if per_worker % 2:
            unit(first + per_worker - 1, bufs[0], bufs[1], False)

    return scatter(src, rows)


def _zero_padding_kernel(lo_ref, hi_ref, tail_ref, xs_in_ref, xs_ref, zero_ref, sem, *, rows_alloc):
    del xs_in_ref
    zero_ref[...] = jnp.zeros(zero_ref.shape, WORD)
    row_copy = lambda r: pltpu.make_async_copy(zero_ref.at[pl.ds(0, ROW_TILE)], xs_ref.at[_token_rows(r)], sem)

    def expert(e, n):
        def row(r, c):
            row_copy(r).start()
            return c

        lax.fori_loop(lo_ref[e], hi_ref[e], row, 0)
        return n + hi_ref[e] - lo_ref[e]

    def wait(_, c):
        row_copy(0).wait()
        return c

    lax.fori_loop(0, lax.fori_loop(0, N_EXPERTS, expert, 0), wait, 0)
    n = _zero_row_groups(zero_ref, xs_ref, sem, tail_ref[0] // ROW_PAD, rows_alloc // ROW_PAD)
    _wait_zero_copies(zero_ref, xs_ref, sem, n)


def _zero_padding(xs, pad_lo, pad_hi, total, rows_alloc):
    grid_spec = pltpu.PrefetchScalarGridSpec(
        num_scalar_prefetch=3,
        grid=(1,),
        in_specs=[pl.BlockSpec(memory_space=pl.ANY)],
        out_specs=pl.BlockSpec(memory_space=pl.ANY),
        scratch_shapes=[pltpu.VMEM((ROW_PAD * ROW_TILE, LANES), WORD), pltpu.SemaphoreType.DMA],
    )
    return pl.pallas_call(
        functools.partial(_zero_padding_kernel, rows_alloc=rows_alloc),
        grid_spec=grid_spec,
        out_shape=jax.ShapeDtypeStruct(xs.shape, xs.dtype),
        input_output_aliases={3: 0},
        compiler_params=_params(("arbitrary",)),
        name="zero_padding",
    )(pad_lo, pad_hi, total, xs)


def _expert_kernel(pstart_ref, nch_ref, rows_ref, end_ref, wg_ref, wu_ref, wd_ref, xs_ref, o_ref,
                   xbuf_ref, obuf_ref, wgb_ref, wub_ref, wdb_ref, zero_ref, done_ref, in_sem, out_sem,
                   *, rows_alloc, chunk_rows):
    e = pl.program_id(0)
    n_e = pl.num_programs(0)
    start = pstart_ref[e]
    nch = nch_ref[e]
    buf_rows = chunk_rows * ROW_TILE
    part = CHUNK_PART
    chunk_parts = chunk_rows // part
    part_rows = part * ROW_TILE

    class _ChunkCopy:
        def __init__(self, whole, parts, n_parts):
            self.whole = whole
            self.parts = parts
            self.n_parts = n_parts

        def _each(self, act):
            pl.when(self.n_parts == chunk_parts)(functools.partial(act, self.whole))
            for p, cp in enumerate(self.parts):
                pl.when((p < self.n_parts) & (self.n_parts < chunk_parts))(functools.partial(act, cp))

        def start(self):
            self._each(lambda cp: cp.start())

        def wait(self):
            self._each(lambda cp: cp.wait())

    def parts_of(rows_left):
        return jnp.clip((rows_left + part - 1) // part, 1, chunk_parts)

    def in_copy(row, slot, n_parts):
        copy = lambda p, n: pltpu.make_async_copy(
            xs_ref.at[_token_rows(row + p * part, n * part)],
            xbuf_ref.at[pl.ds(slot * buf_rows + p * part_rows, n * part_rows)], in_sem.at[slot])
        return _ChunkCopy(copy(0, chunk_parts), [copy(p, 1) for p in range(chunk_parts - 1)], n_parts)

    def out_copy(row, slot, n_parts):
        copy = lambda p, n: pltpu.make_async_copy(
            obuf_ref.at[pl.ds(slot * buf_rows + p * part_rows, n * part_rows)],
            o_ref.at[_token_rows(row + p * part, n * part)], out_sem)
        return _ChunkCopy(copy(0, chunk_parts), [copy(p, 1) for p in range(chunk_parts - 1)], n_parts)

    @pl.when(e == 0)
    def _():
        done_ref[0] = 0
        xbuf_ref[...] = jnp.zeros(xbuf_ref.shape, WORD)
        obuf_ref[...] = jnp.zeros(obuf_ref.shape, WORD)
        in_copy(start, 0, parts_of(rows_ref[0])).start()

    wgb_ref[...] = wg_ref[...].astype(BF16)
    wub_ref[...] = wu_ref[...].astype(BF16)
    wdb_ref[...] = wd_ref[...].astype(BF16)
    done = done_ref[0]

    def chunk(c, carry):
        g = done + c
        slot = g % 2
        row = start + c * chunk_rows
        rows_left = rows_ref[e] - c * chunk_rows
        n_parts = parts_of(rows_left)
        in_copy(row, slot, n_parts).wait()
        last = c + 1 == nch
        next_e = jnp.minimum(e + 1, n_e - 1)
        next_row = jnp.where(last, pstart_ref[next_e], row + chunk_rows)
        next_left = jnp.where(last, rows_ref[next_e], rows_left - chunk_rows)

        @pl.when(jnp.logical_not(last & (e == n_e - 1)))
        def _():
            in_copy(next_row, 1 - slot, parts_of(next_left)).start()

        base = pl.multiple_of(slot * buf_rows, buf_rows)
        def sub_block(h):
            xb = jnp.concatenate(_load_token_rows(xbuf_ref, EXPERT_ROWS, row0=h * EXPERT_ROWS, base=base),
                                 axis=1).astype(BF16)
            gate = jnp.dot(xb, wgb_ref[...], preferred_element_type=F32)
            up = jnp.dot(xb, wub_ref[...], preferred_element_type=F32)
            act = (jax.nn.silu(gate) * up).astype(BF16)
            _store_token_rows(obuf_ref, jnp.dot(act, wdb_ref[...], preferred_element_type=F32),
                              EXPERT_ROWS, row0=h * EXPERT_ROWS, base=base)

        n_sub = chunk_rows // EXPERT_ROWS
        needed = jnp.clip((rows_left + EXPERT_ROWS - 1) // EXPERT_ROWS, 1, n_sub)
        for count in range(1, n_sub + 1):
            @pl.when(needed == count)
            def _():
                for h in range(count):
                    sub_block(h)

        @pl.when(g > 0)
        def _():
            out_copy(0, 0, done_ref[1]).wait()

        out_copy(row, slot, n_parts).start()
        done_ref[1] = n_parts
        return carry

    lax.fori_loop(0, nch, chunk, 0)
    done_ref[0] = done + nch

    @pl.when(e == n_e - 1)
    def _():
        out_copy(0, 0, done_ref[1]).wait()
        zero_ref[...] = jnp.zeros(zero_ref.shape, WORD)
        n = _zero_row_groups(zero_ref, o_ref, out_sem, end_ref[0] // ROW_PAD, rows_alloc // ROW_PAD)
        _wait_zero_copies(zero_ref, o_ref, out_sem, n)


def _expert_ffn(xs, pstart, n_chunks, rows, ffn_end, rows_alloc, chunk_rows, w_e_gate, w_e_up, w_e_down):
    weight = lambda shape: pl.BlockSpec((None, *shape), lambda e, *_: (e, 0, 0))
    grid_spec = pltpu.PrefetchScalarGridSpec(
        num_scalar_prefetch=4,
        grid=(N_EXPERTS,),
        in_specs=[weight((D_MODEL, D_EXPERT)), weight((D_MODEL, D_EXPERT)), weight((D_EXPERT, D_MODEL)),
                  pl.BlockSpec(memory_space=pl.ANY)],
        out_specs=pl.BlockSpec(memory_space=pl.ANY),
        scratch_shapes=[
            pltpu.VMEM((2 * chunk_rows * ROW_TILE, LANES), WORD),
            pltpu.VMEM((2 * chunk_rows * ROW_TILE, LANES), WORD),
            pltpu.VMEM((D_MODEL, D_EXPERT), BF16), pltpu.VMEM((D_MODEL, D_EXPERT), BF16),
            pltpu.VMEM((D_EXPERT, D_MODEL), BF16),
            pltpu.VMEM((ROW_PAD * ROW_TILE, LANES), WORD),
            pltpu.SMEM((2,), jnp.int32),
            pltpu.SemaphoreType.DMA((2,)), pltpu.SemaphoreType.DMA,
        ],
    )
    return pl.pallas_call(
        functools.partial(_expert_kernel, rows_alloc=rows_alloc, chunk_rows=chunk_rows),
        grid_spec=grid_spec,
        out_shape=jax.ShapeDtypeStruct((rows_alloc * ROW_TILE, LANES), WORD),
        compiler_params=_params(("arbitrary",)),
        name="expert_ffn",
    )(pstart, n_chunks, rows, ffn_end, w_e_gate, w_e_up, w_e_down, xs)


def _combine_head_kernel(dest_ref, dest_next_ref, w_ref, base_ref, g_ref, b_ref, outs_ref, y_ref, buf_ref, sem, *, tm):
    i = pl.program_id(0)
    slot_rows = TOP_K * tm

    def gather(d_ref, slot):
        def issue(t, c):
            for k in range(TOP_K):
                pltpu.make_async_copy(outs_ref.at[_token_rows(d_ref[k, t])],
                                      buf_ref.at[_token_rows(slot * slot_rows + k * tm + t)],
                                      sem.at[slot]).start(priority=k % 2)
            return c

        lax.fori_loop(0, tm, issue, 0)

    @pl.when(i == 0)
    def _():
        gather(dest_ref, 0)

    @pl.when(i + 1 < pl.num_programs(0))
    def _():
        gather(dest_next_ref, (i + 1) % 2)

    slot = i % 2
    for k in range(TOP_K):
        pltpu.make_async_copy(outs_ref.at[_token_rows(0, tm)], buf_ref.at[_token_rows(slot * slot_rows + k * tm, tm)],
                              sem.at[slot]).wait()

    w = w_ref[...]
    base = pl.multiple_of(slot * slot_rows * ROW_TILE, ROW_TILE)
    halves = [None, None]
    for k in range(TOP_K):
        wk = w[:, k:k + 1]
        for j, rows in enumerate(_load_token_rows(buf_ref, tm, row0=k * tm, base=base)):
            halves[j] = wk * rows if halves[j] is None else halves[j] + wk * rows
    y_ref[...] = _layer_norm(base_ref[...] + jnp.concatenate(halves, axis=1), g_ref[...], b_ref[...])


def _combine_head(out_sorted, dest, w_tok, base, g2, b2, tm, n_head):
    row = pl.BlockSpec((tm, D_MODEL), lambda i: (i, 0))
    return pl.pallas_call(
        functools.partial(_combine_head_kernel, tm=tm),
        grid=(n_head,),
        in_specs=[
            pl.BlockSpec((TOP_K, tm), lambda i: (0, i), memory_space=pltpu.SMEM),
            pl.BlockSpec((TOP_K, tm), lambda i: (0, jnp.minimum(i + 1, n_head - 1)), memory_space=pltpu.SMEM),
            pl.BlockSpec((tm, TOP_K), lambda i: (i, 0)),
            row, _const_spec((1, D_MODEL)), _const_spec((1, D_MODEL)),
            pl.BlockSpec(memory_space=pl.ANY),
        ],
        out_specs=row,
        out_shape=jax.ShapeDtypeStruct((n_head * tm, D_MODEL), F32),
        scratch_shapes=[pltpu.VMEM((2 * TOP_K * tm * ROW_TILE, LANES), WORD), pltpu.SemaphoreType.DMA((2,))],
        compiler_params=_params(("arbitrary",)),
        name="combine_head",
    )(dest, dest, w_tok, base, g2, b2, out_sorted)


def _sc_gather_rows(table, rows):
    mesh, n_workers, worker = _sc_workers()
    m = rows.shape[0]
    per_worker = m // n_workers
    assert m % n_workers == 0 and per_worker % SC_ROWS == 0
    in_flight = _pick_tile(per_worker // SC_ROWS, (2, 1))
    step = SC_ROWS * in_flight
    n_stages = per_worker // step
    stage_bufs = [pltpu.VMEM((step,), jnp.int32), pltpu.VMEM((step, LANES), table.dtype),
                  pltpu.SemaphoreType.DMA, pltpu.SemaphoreType.DMA]

    @functools.partial(
        pl.kernel, mesh=mesh,
        out_type=jax.ShapeDtypeStruct((m, LANES), table.dtype),
        scratch_types=stage_bufs + stage_bufs,
    )
    def gather(table_hbm, rows_hbm, out_hbm, rows_a, data_a, gsem_a, ssem_a, rows_b, data_b, gsem_b, ssem_b):
        first = worker() * per_worker
        bufs = ((rows_a, data_a, gsem_a, ssem_a), (rows_b, data_b, gsem_b, ssem_b))

        def gathers(buf):
            rows_v, data_v, gsem, _ = buf
            return [pltpu.make_async_copy(table_hbm.at[rows_v.at[pl.ds(j * SC_ROWS, SC_ROWS)]],
                                          data_v.at[pl.ds(j * SC_ROWS, SC_ROWS)], gsem) for j in range(in_flight)]

        def store(off, buf):
            return pltpu.make_async_copy(buf[1], out_hbm.at[pl.ds(off, step)], buf[3])

        def stage(it, buf, other, has_next, other_stored):
            off = first + it * step

            @pl.when(has_next)
            def _():
                @pl.when(other_stored)
                def _():
                    store(off, other).wait()

                pltpu.sync_copy(rows_hbm.at[pl.ds(off + step, step)], other[0])
                for cp in gathers(other):
                    cp.start()

            for cp in gathers(buf):
                cp.wait()
            store(off, buf).start()

        pltpu.sync_copy(rows_hbm.at[pl.ds(first, step)], rows_a)
        for cp in gathers(bufs[0]):
            cp.start()

        @pl.loop(0, n_stages // 2)
        def _(pair):
            it = 2 * pair
            stage(it, bufs[0], bufs[1], True, pair > 0)
            stage(it + 1, bufs[1], bufs[0], it + 2 < n_stages, True)

        if n_stages % 2:
            stage(n_stages - 1, bufs[0], bufs[1], False, False)
        for back in range(min(2, n_stages)):
            store(first, bufs[(n_stages - 1 - back) % 2]).wait()

    return gather(table, rows)


def _combine_tail_kernel(yh_ref, rows_ref, w_ref, base_ref, g_ref, b_ref, yp_ref, ys_ref, *, tm, n_head, n_p):
    i = pl.program_id(0)

    @pl.when(i < n_head)
    def _():
        yp_ref[...] = yh_ref[...]

    @pl.when(i >= n_head)
    def _():
        w = w_ref[...]
        lo = [None] * ROW_TILE
        hi = [None] * ROW_TILE
        for k in range(TOP_K):
            wk = w[:, k:k + 1]
            for s in range(ROW_TILE):
                vec = lambda h: ((h * ROW_TILE + s) * TOP_K + k) * SC_ROWS
                words = jnp.concatenate([rows_ref[vec(h):vec(h) + SC_ROWS, :] for h in range(tm // SC_ROWS)], axis=0)
                lo_s, hi_s = _unpack_words(words)
                lo[s] = wk * lo_s if lo[s] is None else lo[s] + wk * lo_s
                hi[s] = wk * hi_s if hi[s] is None else hi[s] + wk * hi_s
        y = _layer_norm(base_ref[...] + jnp.concatenate(lo + hi, axis=1), g_ref[...], b_ref[...])

        @pl.when(i < n_p)
        def _():
            yp_ref[...] = y

        @pl.when(i >= n_p)
        def _():
            ys_ref[...] = y


def _combine_tail(y_head, gathered, w_tok, base, g2, b2, t_p, t_s, tm, n_head):
    n_p = t_p // tm
    assert n_head <= n_p
    n_tiles = (t_p + t_s) // tm
    blk = TOP_K * ROW_TILE * tm
    row = pl.BlockSpec((tm, D_MODEL), lambda i: (i, 0))
    out_p, out_s = _two_source_specs(t_p, t_s, tm, D_MODEL)
    return pl.pallas_call(
        functools.partial(_combine_tail_kernel, tm=tm, n_head=n_head, n_p=n_p),
        grid=(n_tiles,),
        in_specs=[
            pl.BlockSpec((tm, D_MODEL), lambda i: (jnp.minimum(i, n_head - 1), 0)),
            pl.BlockSpec((blk, LANES), lambda i: (jnp.maximum(i - n_head, 0), 0)),
            pl.BlockSpec((tm, TOP_K), lambda i: (i, 0)),
            row, _const_spec((1, D_MODEL)), _const_spec((1, D_MODEL)),
        ],
        out_specs=(out_p, out_s),
        out_shape=(jax.ShapeDtypeStruct((t_p, D_MODEL), F32), jax.ShapeDtypeStruct((t_s, D_MODEL), F32)),
        compiler_params=_params(("arbitrary",)),
        name="combine_tail",
    )(y_head, gathered, w_tok, base, g2, b2)


def kernel(x_prompt, x_sample, cache_k, cache_v, state_conv, state_rnn, w_in, conv_w, conv_b, w_gate_a, b_gate_a, w_gate_x, b_gate_x, lru_lambda, rel_bias, sinks, w_o_attn, w_o_rnn, w_out, ln1_g, ln1_b, w_router, router_bias, w_e_gate, w_e_up, w_e_down, w_s_gate, w_s_up, w_s_down, ln2_g, ln2_b):
    assert w_in.shape[0] == DEPTH == 1
    batch, seq, _ = x_prompt.shape
    dec_batch, s_len, _ = x_sample.shape
    w_cache = cache_k.shape[2]
    assert s_len == SUBLANES and seq % WINDOW == 0 and w_cache == WINDOW
    t_p = batch * seq
    t_s = dec_batch * s_len
    vec = lambda a: a[0].reshape(1, -1).astype(F32)

    x_p = x_prompt.reshape(t_p, D_MODEL)
    x_s = x_sample.reshape(t_s, D_MODEL)
    q, k, v, xr, gy, sga, sgr = _inproj(x_p, x_s, w_in[0].astype(BF16))

    qi = jnp.arange(WINDOW)
    dist = qi[:, None] + WINDOW - jnp.arange(2 * WINDOW)[None, :]
    band = (dist >= 0) & (dist <= WINDOW)
    has_prev = jnp.arange(2 * WINDOW)[None, :] >= WINDOW
    bias_p = _bias_table(rel_bias, dist)
    bias_p = jnp.stack([jnp.where(band & has_prev, bias_p, NEG_INF), jnp.where(band, bias_p, NEG_INF)])
    attn_p = _attn_prompt(q, k, v, bias_p, sinks[0], batch, seq)
    attn_s, k_s, v_s = _attn_sample(
        q, k, v, cache_k[0].reshape(dec_batch, w_cache, D_KV), cache_v[0].reshape(dec_batch, w_cache, D_KV),
        rel_bias, sinks[0], t_p, dec_batch, s_len)

    rnn_w = (conv_w[0], vec(conv_b), w_gate_a[0].astype(BF16), vec(b_gate_a), w_gate_x[0].astype(BF16),
             vec(b_gate_x), vec(lru_lambda))
    rnn_p, h_p = _rnn_prompt(xr, gy, rnn_w, batch, seq)
    hist_pad = jnp.pad(state_conv[0], ((0, 0), (SUBLANES - (CONV_W - 1), 0), (0, 0))).reshape(t_s, D_RNN)
    rnn_s, h_s = _rnn_sample(xr, gy, hist_pad, state_rnn[0], rnn_w, t_p, dec_batch)

    merge_w = (w_o_attn[0].astype(BF16), w_o_rnn[0].astype(BF16), w_out[0].astype(BF16), vec(ln1_g), vec(ln1_b),
               w_router[0].T, router_bias[0].reshape(N_EXPERTS, 1), w_s_gate[0].astype(BF16),
               w_s_up[0].astype(BF16), w_s_down[0].astype(BF16))
    x1w, base, idx_t, wt_t, rank_t, counts = _merge(x_p, x_s, attn_p, attn_s, rnn_p, rnn_s, sga, sgr, merge_w)

    n_assign = (t_p + t_s) * TOP_K
    chunk_rows = _expert_chunk_rows(n_assign)
    pstart, pad_lo, pad_hi, n_chunks, rows, ffn_end, rows_alloc = _expert_layout(counts, n_assign, chunk_rows)
    tm = _pick_tile(math.gcd(t_p, t_s), (256, 128))
    dest, word_rows = _dest_rows(idx_t, rank_t, pstart)
    xs = _sc_scatter_rows(x1w, word_rows, rows_alloc * ROW_TILE)
    xs = _zero_padding(xs, pad_lo, pad_hi, pad_hi[N_EXPERTS - 1:], rows_alloc)
    out_sorted = _expert_ffn(xs, pstart, n_chunks, rows, ffn_end, rows_alloc, chunk_rows,
                             w_e_gate[0], w_e_up[0], w_e_down[0])
    n_tiles = (t_p + t_s) // tm
    n_head = min(t_p // tm, n_tiles // 4)
    w_tok = wt_t.T
    g2, b2 = vec(ln2_g), vec(ln2_b)
    y_head = _combine_head(out_sorted, dest, w_tok, base, g2, b2, tm, n_head)
    gathered = _sc_gather_rows(out_sorted, word_rows[n_head * (tm // SC_ROWS) * WORD_ROW_VECS:].reshape(-1))
    y_p, y_s = _combine_tail(y_head, gathered, w_tok, base, g2, b2, t_p, t_s, tm, n_head)
    y_p = y_p.reshape(batch, seq, D_MODEL)
    y_s = y_s.reshape(dec_batch, s_len, D_MODEL)
    kv5 = lambda a, b: a.reshape(1, b, WINDOW, N_KV_HEADS, HEAD_DIM)
    tail = lambda a, n: jnp.stack([lax.slice_in_dim(a, (b + 1) * seq - n, (b + 1) * seq) for b in range(batch)])
    k_p = kv5(tail(k, WINDOW), batch)
    v_p = kv5(tail(v, WINDOW), batch)
    conv_p = tail(xr, CONV_W - 1)[None]
    conv_s = xr[t_p:].reshape(dec_batch, s_len, D_RNN)[:, s_len - (CONV_W - 1):][None]
    return (y_p, y_s, k_p, v_p, conv_p, h_p.reshape(1, batch, D_RNN),
            kv5(k_s, dec_batch), kv5(v_s, dec_batch), conv_s, h_s.reshape(1, dec_batch, D_RNN))
```

```python
import functools
import math

import jax
import jax.numpy as jnp
from jax import lax
from jax.experimental import pallas as pl
from jax.experimental.pallas import tpu as pltpu
from jax.experimental.pallas import tpu_sc as plsc

F32 = jnp.float32
BF16 = jnp.bfloat16
WORD = jnp.int32

D_MODEL = 1024
N_HEADS = 8
N_KV_HEADS = 2
HEAD_DIM = 64
GROUP = N_HEADS // N_KV_HEADS
WINDOW = 128
D_ATTN = N_HEADS * HEAD_DIM
D_KV = N_KV_HEADS * HEAD_DIM
N_BUCKETS = 32
MAX_DISTANCE = 128
D_RNN = D_MODEL
RNN_BLOCK = 256
N_RNN_BLOCKS = D_RNN // RNN_BLOCK
CONV_W = 4
RG_C = 8.0
N_EXPERTS = 256
TOP_K = 8
N_GROUPS = 8
GROUP_SIZE = N_EXPERTS // N_GROUPS
TOPK_GROUPS = 4
D_EXPERT = D_MODEL // 4
ROUTED_SCALE = 2.5
LN_EPS = 1e-5
DEPTH = 1
ALPHA = (2 * DEPTH) ** 0.25
NEG_INF = -1e30
SM_SCALE = HEAD_DIM ** -0.5
assert math.frexp(SM_SCALE)[0] == 0.5

O_Q = 0
O_K = D_ATTN
O_V = O_K + D_KV
O_XR = O_V + D_KV
O_YR = O_XR + D_RNN
O_GA = O_YR + D_RNN
O_GR = O_GA + D_MODEL
D_IN = O_GR + D_MODEL

SUBLANES = 8
VMEM_LIMIT_BYTES = 56 * 1024 * 1024
EXPERT_ROWS = 256
CHUNK_PART = 128
ROW_PAD = SUBLANES


def _params(sem):
    return pltpu.CompilerParams(dimension_semantics=sem, vmem_limit_bytes=VMEM_LIMIT_BYTES)


def _pick_tile(n, candidates):
    for c in candidates:
        if n % c == 0:
            return c
    raise ValueError(f"no tile for {n}")


def _const_spec(shape):
    nd = len(shape)
    return pl.BlockSpec(shape, lambda *_: (0,) * nd)


LANES = 128
ROW_WORDS = D_MODEL // 2
ROW_TILE = ROW_WORDS // LANES
HIGH_HALF = -65536


def _pack_words(mat):
    as_bits = lambda v: pltpu.bitcast(v.astype(BF16).astype(F32), WORD)
    return (as_bits(mat[:, ROW_WORDS:]) & HIGH_HALF) | lax.shift_right_logical(as_bits(mat[:, :ROW_WORDS]), 16)


def _unpack_words(words):
    return pltpu.bitcast(words << 16, F32), pltpu.bitcast(words & HIGH_HALF, F32)


def _store_token_rows(ref, mat, n, row0=0, base=0):
    words = _pack_words(mat)
    for s in range(ROW_TILE):
        ref[pl.ds(base + row0 * ROW_TILE + s, n, stride=ROW_TILE), :] = words[:, s * LANES:(s + 1) * LANES]


def _load_token_rows(ref, n, row0=0, base=0):
    words = jnp.concatenate(
        [ref[pl.ds(base + row0 * ROW_TILE + s, n, stride=ROW_TILE), :] for s in range(ROW_TILE)], axis=1)
    return _unpack_words(words)


def _two_source_specs(t_p, t_s, tm, width):
    n_p = t_p // tm
    assert t_p % tm == 0 and t_s % tm == 0
    return (pl.BlockSpec((tm, width), lambda i, *_: (jnp.minimum(i, n_p - 1), 0)),
            pl.BlockSpec((tm, width), lambda i, *_: (jnp.maximum(i - n_p, 0), 0)))


def _inproj_kernel(xp_ref, xs_ref, w_ref, q_ref, k_ref, v_ref, xr_ref, gy_ref, sga_ref, sgr_ref, *, n_p):
    x = jnp.where(pl.program_id(0) < n_p, xp_ref[...], xs_ref[...]).astype(BF16)

    def seg(lo, hi):
        return jnp.dot(x, w_ref[:, lo:hi], preferred_element_type=F32)

    q_ref[...] = (seg(O_Q, O_K) * SM_SCALE).astype(BF16)
    k_ref[...] = seg(O_K, O_V)
    v_ref[...] = seg(O_V, O_XR)
    xr_ref[...] = seg(O_XR, O_YR)
    gy_ref[...] = jax.nn.gelu(seg(O_YR, O_GA)).astype(BF16)
    sga_ref[...] = jax.nn.sigmoid(seg(O_GA, O_GR)).astype(BF16)
    sgr_ref[...] = jax.nn.sigmoid(seg(O_GR, D_IN)).astype(BF16)


def _inproj(x_p, x_s, w_in_bf16):
    t_p, t_s = x_p.shape[0], x_s.shape[0]
    t = t_p + t_s
    tm = _pick_tile(math.gcd(t_p, t_s), (256, 128, 64, 32, 16, 8))
    row = lambda width: pl.BlockSpec((tm, width), lambda i: (i, 0))
    out_shape = (
        jax.ShapeDtypeStruct((t, D_ATTN), BF16),
        jax.ShapeDtypeStruct((t, D_KV), F32),
        jax.ShapeDtypeStruct((t, D_KV), F32),
        jax.ShapeDtypeStruct((t, D_RNN), F32),
        jax.ShapeDtypeStruct((t, D_RNN), BF16),
        jax.ShapeDtypeStruct((t, D_MODEL), BF16),
        jax.ShapeDtypeStruct((t, D_MODEL), BF16),
    )
    return pl.pallas_call(
        functools.partial(_inproj_kernel, n_p=t_p // tm),
        grid=(t // tm,),
        in_specs=[*_two_source_specs(t_p, t_s, tm, D_MODEL), _const_spec((D_MODEL, D_IN))],
        out_specs=(row(D_ATTN), row(D_KV), row(D_KV), row(D_RNN), row(D_RNN), row(D_MODEL), row(D_MODEL)),
        out_shape=out_shape,
        compiler_params=_params(("parallel",)),
        name="inproj",
    )(x_p, x_s, w_in_bf16)


def _t5_bucket(dist):
    n = jnp.maximum(dist, 0)
    max_exact = N_BUCKETS // 2
    nf = jnp.maximum(n, 1).astype(F32)
    large = max_exact + (jnp.log(nf / max_exact) / math.log(MAX_DISTANCE / max_exact) * (N_BUCKETS - max_exact)).astype(jnp.int32)
    large = jnp.minimum(large, N_BUCKETS - 1)
    return jnp.where(n < max_exact, n, large)


def _bias_table(rel_bias, dist):
    bucket = _t5_bucket(dist)
    rb = rel_bias.astype(F32)
    out = jnp.zeros((N_HEADS, *dist.shape), F32)
    for j in range(N_BUCKETS):
        out = jnp.where(bucket[None] == j, rb[j][:, None, None], out)
    return out


def _softmax_pv(s, sink, v):
    m = jnp.maximum(jnp.max(s, axis=-1, keepdims=True), sink)
    p = jnp.exp(s - m)
    denom = jnp.sum(p, axis=-1, keepdims=True) + jnp.exp(sink - m)
    return jnp.dot(p.astype(BF16), v, preferred_element_type=F32), denom


def _attn_prompt_kernel(sink_ref, q_ref, kc_ref, kp_ref, vc_ref, vp_ref, bias_ref, o_ref, *, q_blocks):
    keys = jnp.concatenate([kp_ref[...], kc_ref[...]], axis=0).astype(BF16)
    vals = jnp.concatenate([vp_ref[...], vc_ref[...]], axis=0).astype(BF16)
    first_table = jnp.minimum(pl.program_id(1), 1)
    for j in range(q_blocks):
        table = first_table if j == 0 else 1
        kk = keys[j * WINDOW:(j + 2) * WINDOW]
        vv = vals[j * WINDOW:(j + 2) * WINDOW]
        q = q_ref[j * WINDOW:(j + 1) * WINDOW, :]
        for g in range(N_KV_HEADS):
            kg = kk[:, g * HEAD_DIM:(g + 1) * HEAD_DIM]
            vg = vv[:, g * HEAD_DIM:(g + 1) * HEAD_DIM]
            for h in range(GROUP):
                hh = g * GROUP + h
                qh = q[:, hh * HEAD_DIM:(hh + 1) * HEAD_DIM]
                s = lax.dot_general(qh, kg, (((1,), (1,)), ((), ())), preferred_element_type=F32) + bias_ref[table, hh]
                o, denom = _softmax_pv(s, sink_ref[0, hh], vg)
                o_ref[j * WINDOW:(j + 1) * WINDOW, hh * HEAD_DIM:(hh + 1) * HEAD_DIM] = (o / denom).astype(BF16)


def _attn_prompt(q_all, k_all, v_all, bias, sinks, batch, seq):
    nb = seq // WINDOW
    q_blocks = _pick_tile(nb, (2, 1))
    steps = nb // q_blocks
    cur = lambda width: pl.BlockSpec((q_blocks * WINDOW, width), lambda b, n: (b * steps + n, 0))
    prev = lambda width: pl.BlockSpec((WINDOW, width), lambda b, n: (b * nb + jnp.maximum(n * q_blocks - 1, 0), 0))
    return pl.pallas_call(
        functools.partial(_attn_prompt_kernel, q_blocks=q_blocks),
        grid=(batch, steps),
        in_specs=[
            pl.BlockSpec(memory_space=pltpu.SMEM),
            cur(D_ATTN), cur(D_KV), prev(D_KV), cur(D_KV), prev(D_KV),
            _const_spec((2, N_HEADS, WINDOW, 2 * WINDOW)),
        ],
        out_specs=cur(D_ATTN),
        out_shape=jax.ShapeDtypeStruct((batch * seq, D_ATTN), BF16),
        compiler_params=_params(("parallel", "arbitrary")),
        name="attn_prompt",
    )(sinks.reshape(1, N_HEADS).astype(F32), q_all, k_all, k_all, v_all, v_all, bias)


def _attn_sample_kernel(q_ref, kn_ref, vn_ref, kc_ref, vc_ref, bc_ref, bn_ref, sink_ref,
                        o_ref, ko_ref, vo_ref, *, seqs, s_len):
    w = kc_ref.shape[1]
    rows_c = lax.broadcasted_iota(jnp.int32, (GROUP * s_len, w), 0) % s_len
    cols_c = lax.broadcasted_iota(jnp.int32, (GROUP * s_len, w), 1)
    dist_c = rows_c + w - cols_c
    valid_c = (dist_c >= 0) & (dist_c <= WINDOW)
    rows_n = lax.broadcasted_iota(jnp.int32, (GROUP * s_len, s_len), 0) % s_len
    cols_n = lax.broadcasted_iota(jnp.int32, (GROUP * s_len, s_len), 1)
    dist_n = rows_n - cols_n
    valid_n = (dist_n >= 0) & (dist_n <= WINDOW)
    for j in range(seqs):
        r0 = j * s_len
        qj = q_ref[r0:r0 + s_len, :]
        kc = kc_ref[j]
        vc = vc_ref[j]
        kn = kn_ref[r0:r0 + s_len, :]
        vn = vn_ref[r0:r0 + s_len, :]
        ko_ref[j, 0:w - s_len, :] = kc[s_len:, :]
        ko_ref[j, w - s_len:w, :] = kn
        vo_ref[j, 0:w - s_len, :] = vc[s_len:, :]
        vo_ref[j, w - s_len:w, :] = vn
        kcb, vcb, knb, vnb = kc.astype(BF16), vc.astype(BF16), kn.astype(BF16), vn.astype(BF16)
        for g in range(N_KV_HEADS):
            lo, hi = g * HEAD_DIM, (g + 1) * HEAD_DIM
            qs = jnp.concatenate(
                [qj[:, (g * GROUP + h) * HEAD_DIM:(g * GROUP + h + 1) * HEAD_DIM] for h in range(GROUP)], axis=0)
            nt = (((1,), (1,)), ((), ()))
            s_c = lax.dot_general(qs, kcb[:, lo:hi], nt, preferred_element_type=F32)
            s_n = lax.dot_general(qs, knb[:, lo:hi], nt, preferred_element_type=F32)
            s_c = jnp.where(valid_c, s_c + bc_ref[g], NEG_INF)
            s_n = jnp.where(valid_n, s_n + bn_ref[g], NEG_INF)
            sink = sink_ref[g]
            m = jnp.maximum(jnp.maximum(jnp.max(s_c, axis=-1, keepdims=True), jnp.max(s_n, axis=-1, keepdims=True)), sink)
            p_c = jnp.exp(s_c - m)
            p_n = jnp.exp(s_n - m)
            denom = jnp.sum(p_c, axis=-1, keepdims=True) + jnp.sum(p_n, axis=-1, keepdims=True) + jnp.exp(sink - m)
            o = jnp.dot(p_c.astype(BF16), vcb[:, lo:hi], preferred_element_type=F32)
            o = o + jnp.dot(p_n.astype(BF16), vnb[:, lo:hi], preferred_element_type=F32)
            o = (o / denom).astype(BF16)
            for h in range(GROUP):
                hh = g * GROUP + h
                o_ref[r0:r0 + s_len, hh * HEAD_DIM:(hh + 1) * HEAD_DIM] = o[h * s_len:(h + 1) * s_len, :]


def _attn_sample(q_all, k_all, v_all, cache_k, cache_v, rel_bias, sinks, row0, dec_batch, s_len):
    w = cache_k.shape[1]
    seqs = _pick_tile(dec_batch, (16, 8, 4, 2, 1))
    rows = seqs * s_len
    blk0 = row0 // rows
    assert row0 % rows == 0
    qi = jnp.arange(s_len)
    dist_c = qi[:, None] + w - jnp.arange(w)[None, :]
    dist_n = qi[:, None] - jnp.arange(s_len)[None, :]
    b_c = _bias_table(rel_bias, dist_c).reshape(N_KV_HEADS, GROUP * s_len, w)
    b_n = _bias_table(rel_bias, dist_n).reshape(N_KV_HEADS, GROUP * s_len, s_len)
    sink = jnp.broadcast_to(sinks.astype(F32).reshape(N_KV_HEADS, GROUP, 1, 1), (N_KV_HEADS, GROUP, s_len, 1))
    sink = sink.reshape(N_KV_HEADS, GROUP * s_len, 1)
    tok = lambda width: pl.BlockSpec((rows, width), lambda i: (blk0 + i, 0))
    cache = pl.BlockSpec((seqs, w, D_KV), lambda i: (i, 0, 0))
    return pl.pallas_call(
        functools.partial(_attn_sample_kernel, seqs=seqs, s_len=s_len),
        grid=(dec_batch // seqs,),
        in_specs=[
            tok(D_ATTN), tok(D_KV), tok(D_KV), cache, cache,
            _const_spec(b_c.shape), _const_spec(b_n.shape), _const_spec(sink.shape),
        ],
        out_specs=(pl.BlockSpec((rows, D_ATTN), lambda i: (i, 0)), cache, cache),
        out_shape=(
            jax.ShapeDtypeStruct((dec_batch * s_len, D_ATTN), BF16),
            jax.ShapeDtypeStruct((dec_batch, w, D_KV), F32),
            jax.ShapeDtypeStruct((dec_batch, w, D_KV), F32),
        ),
        compiler_params=_params(("parallel",)),
        name="attn_sample",
    )(q_all, k_all, v_all, cache_k, cache_v, b_c, b_n, sink)


def _softplus(z):
    return jnp.maximum(z, 0.0) + jnp.log1p(jnp.exp(-jnp.abs(z)))


def _block_gate(xcb, w_ref, b_ref):
    parts = [jnp.dot(xcb[:, n * RNN_BLOCK:(n + 1) * RNN_BLOCK], w_ref[n], preferred_element_type=F32)
             for n in range(N_RNN_BLOCKS)]
    return jax.nn.sigmoid(jnp.concatenate(parts, axis=-1) + b_ref[...])


def _lru_coeffs(xc, wa_ref, ba_ref, wx_ref, bx_ref, lam_ref, first_row_unnormalised):
    xcb = xc.astype(BF16)
    r = _block_gate(xcb, wa_ref, ba_ref)
    i = _block_gate(xcb, wx_ref, bx_ref)
    log_a = -RG_C * r * _softplus(-lam_ref[...])
    a = jnp.exp(log_a)
    mult = jnp.sqrt(-jnp.tanh(log_a) * (a * a + 1.0))
    if first_row_unnormalised is not None:
        mult = jnp.where(first_row_unnormalised, 1.0, mult)
    return a, mult * i * xc


def _scan8(a, b):
    shape = a.shape
    grouped = (shape[0] // SUBLANES, SUBLANES, shape[1])
    a = a.reshape(grouped)
    b = b.reshape(grouped)
    r8 = lax.broadcasted_iota(jnp.int32, grouped, 1)
    d = 1
    while d < SUBLANES:
        keep = r8 >= d
        a_sh = jnp.where(keep, pltpu.roll(a, d, 1), 1.0)
        b_sh = jnp.where(keep, pltpu.roll(b, d, 1), 0.0)
        b = a * b_sh + b
        a = a * a_sh
        d *= 2
    return a.reshape(shape), b.reshape(shape)


def _rnn_prompt_kernel(xr_ref, gy_ref, cw_ref, cb_ref, wa_ref, ba_ref, wx_ref, bx_ref, lam_ref,
                       o_ref, nh_ref, ext_ref, a_ref, b_ref, hc_ref, *, tl):
    l = pl.program_id(1)

    @pl.when(l == 0)
    def _():
        ext_ref[0:SUBLANES, :] = jnp.zeros((SUBLANES, D_RNN), F32)
        hc_ref[...] = jnp.zeros((1, D_RNN), F32)

    x = xr_ref[...]
    ext_ref[SUBLANES:, :] = x
    xc = cb_ref[...] + cw_ref[CONV_W - 1:CONV_W, :] * x
    for j in range(1, CONV_W):
        xc = xc + cw_ref[CONV_W - 1 - j:CONV_W - j, :] * ext_ref[SUBLANES - j:SUBLANES - j + tl, :]
    ext_ref[0:SUBLANES, :] = x[tl - SUBLANES:, :]

    row = lax.broadcasted_iota(jnp.int32, (tl, D_RNN), 0)
    a, b = _lru_coeffs(xc, wa_ref, ba_ref, wx_ref, bx_ref, lam_ref, (row == 0) & (l == 0))
    a, b = _scan8(a, b)
    a_ref[...] = a
    b_ref[...] = b

    def chunk(c, h):
        sl = pl.ds(pl.multiple_of(c * SUBLANES, SUBLANES), SUBLANES)
        hc = b_ref[sl, :] + a_ref[sl, :] * h
        b_ref[sl, :] = hc
        return hc[SUBLANES - 1:SUBLANES, :]

    h = lax.fori_loop(0, tl // SUBLANES, chunk, hc_ref[...])
    hc_ref[...] = h
    nh_ref[0] = h
    o_ref[...] = (b_ref[...] * gy_ref[...]).astype(BF16)


def _rnn_prompt(xr_all, gy_all, rnn_w, batch, seq):
    tl = _pick_tile(seq, (512, 256, 128, 64, 32, 16, 8))
    nl = seq // tl
    tok = pl.BlockSpec((tl, D_RNN), lambda b, l: (b * nl + l, 0))
    return pl.pallas_call(
        functools.partial(_rnn_prompt_kernel, tl=tl),
        grid=(batch, nl),
        in_specs=[tok, tok] + [_const_spec(w.shape) for w in rnn_w],
        out_specs=(tok, pl.BlockSpec((1, 1, D_RNN), lambda b, l: (b, 0, 0))),
        out_shape=(jax.ShapeDtypeStruct((batch * seq, D_RNN), BF16), jax.ShapeDtypeStruct((batch, 1, D_RNN), F32)),
        scratch_shapes=[
            pltpu.VMEM((tl + SUBLANES, D_RNN), F32),
            pltpu.VMEM((tl, D_RNN), F32),
            pltpu.VMEM((tl, D_RNN), F32),
            pltpu.VMEM((1, D_RNN), F32),
        ],
        compiler_params=_params(("parallel", "arbitrary")),
        name="rnn_prompt",
    )(xr_all, gy_all, *rnn_w)


def _rnn_sample_kernel(xr_ref, gy_ref, hp_ref, h0_ref, cw_ref, cb_ref, wa_ref, ba_ref, wx_ref, bx_ref, lam_ref,
                       o_ref, nh_ref, *, seqs):
    rows = seqs * SUBLANES
    x = xr_ref[...]
    hp = hp_ref[...]
    r8 = lax.broadcasted_iota(jnp.int32, (rows, D_RNN), 0) % SUBLANES
    xc = cb_ref[...] + cw_ref[CONV_W - 1:CONV_W, :] * x
    for j in range(1, CONV_W):
        shifted = jnp.where(r8 >= j, pltpu.roll(x, j, 0), pltpu.roll(hp, rows - (SUBLANES - j), 0))
        xc = xc + cw_ref[CONV_W - 1 - j:CONV_W - j, :] * shifted
    a, b = _lru_coeffs(xc, wa_ref, ba_ref, wx_ref, bx_ref, lam_ref, None)
    a, b = _scan8(a, b)
    h0 = jnp.broadcast_to(h0_ref[...][:, None, :], (seqs, SUBLANES, D_RNN)).reshape(rows, D_RNN)
    h = b + a * h0
    last = jnp.where(r8 == SUBLANES - 1, h, 0.0).reshape(seqs, SUBLANES, D_RNN)
    nh_ref[...] = jnp.sum(last, axis=1)
    o_ref[...] = (h * gy_ref[...]).astype(BF16)


def _rnn_sample(xr_all, gy_all, hist_pad, h0, rnn_w, row0, dec_batch):
    seqs = _pick_tile(dec_batch, (16, 8))
    rows = seqs * SUBLANES
    assert row0 % rows == 0
    blk0 = row0 // rows
    tok = pl.BlockSpec((rows, D_RNN), lambda i: (blk0 + i, 0))
    return pl.pallas_call(
        functools.partial(_rnn_sample_kernel, seqs=seqs),
        grid=(dec_batch // seqs,),
        in_specs=[tok, tok, pl.BlockSpec((rows, D_RNN), lambda i: (i, 0)), pl.BlockSpec((seqs, D_RNN), lambda i: (i, 0))]
        + [_const_spec(w.shape) for w in rnn_w],
        out_specs=(pl.BlockSpec((rows, D_RNN), lambda i: (i, 0)), pl.BlockSpec((seqs, D_RNN), lambda i: (i, 0))),
        out_shape=(jax.ShapeDtypeStruct((dec_batch * SUBLANES, D_RNN), BF16),
                   jax.ShapeDtypeStruct((dec_batch, D_RNN), F32)),
        compiler_params=_params(("parallel",)),
        name="rnn_sample",
    )(xr_all, gy_all, hist_pad, h0, *rnn_w)


def _layer_norm(z, g, b):
    mu = jnp.mean(z, axis=-1, keepdims=True)
    zc = z - mu
    var = jnp.mean(zc * zc, axis=-1, keepdims=True)
    return zc * lax.rsqrt(var + LN_EPS) * g + b


def _first_index_of_max(vals, iota, axis, sentinel):
    mx = jnp.max(vals, axis=axis, keepdims=True)
    return mx, jnp.min(jnp.where(vals == mx, iota, sentinel), axis=axis, keepdims=True)


def _route(scores, bias):
    t = scores.shape[1]
    grp = scores + bias
    g3 = grp.reshape(N_GROUPS, GROUP_SIZE, t)
    e_in_g = lax.broadcasted_iota(jnp.int32, g3.shape, 1)
    m1, first = _first_index_of_max(g3, e_in_g, 1, GROUP_SIZE)
    m2 = jnp.max(jnp.where(e_in_g == first, -jnp.inf, g3), axis=1, keepdims=True)
    gscore = (m1 + m2).reshape(N_GROUPS, t)
    g_iota = lax.broadcasted_iota(jnp.int32, gscore.shape, 0)
    gmask = jnp.zeros(gscore.shape, jnp.bool_)
    for _ in range(TOPK_GROUPS):
        _, gi = _first_index_of_max(gscore, g_iota, 0, N_GROUPS)
        hit = g_iota == gi
        gmask = gmask | hit
        gscore = jnp.where(hit, -jnp.inf, gscore)
    masked = jnp.where(gmask[:, None, :], g3, -jnp.inf).reshape(N_EXPERTS, t)
    e_iota = lax.broadcasted_iota(jnp.int32, masked.shape, 0)
    idx, wts, hits = [], [], []
    for _ in range(TOP_K):
        _, ei = _first_index_of_max(masked, e_iota, 0, N_EXPERTS)
        hit = e_iota == ei
        idx.append(ei)
        hits.append(hit)
        wts.append(jnp.sum(jnp.where(hit, scores, 0.0), axis=0, keepdims=True))
        masked = jnp.where(hit, -jnp.inf, masked)
    idx = jnp.concatenate(idx, axis=0)
    w = jnp.concatenate(wts, axis=0)
    w = w / jnp.sum(w, axis=0, keepdims=True) * ROUTED_SCALE
    return idx, w, hits


def _merge_kernel(xp_ref, xs_ref, aop_ref, aos_ref, rop_ref, ros_ref, sga_ref, sgr_ref, woa_ref, wor_ref, wout_ref, g1_ref, b1_ref,
                  wrt_ref, rb_ref, wsg_ref, wsu_ref, wsd_ref,
                  x1w_ref, base_ref, idx_ref, wt_ref, rank_ref, cnt_ref, carry_ref, *, n_p, tm):
    i = pl.program_id(0)

    @pl.when(i == 0)
    def _():
        carry_ref[...] = jnp.zeros(carry_ref.shape, F32)

    is_prompt = i < n_p
    x = jnp.where(is_prompt, xp_ref[...], xs_ref[...])
    pa = jnp.dot(jnp.where(is_prompt, aop_ref[...], aos_ref[...]), woa_ref[...], preferred_element_type=F32)
    pr = jnp.dot(jnp.where(is_prompt, rop_ref[...], ros_ref[...]), wor_ref[...], preferred_element_type=F32)
    merged = sga_ref[...] * pa + sgr_ref[...] * pr
    z = ALPHA * x + jnp.dot(merged.astype(BF16), wout_ref[...], preferred_element_type=F32)
    x1 = _layer_norm(z, g1_ref[...], b1_ref[...])
    words = _pack_words(x1)
    for h in range(tm // SC_ROWS):
        for s in range(ROW_TILE):
            x1w_ref[(h * ROW_TILE + s) * SC_ROWS:(h * ROW_TILE + s + 1) * SC_ROWS, :] = (
                words[h * SC_ROWS:(h + 1) * SC_ROWS, s * LANES:(s + 1) * LANES])
    x1b = x1.astype(BF16)
    u = jax.nn.silu(jnp.dot(x1b, wsg_ref[...], preferred_element_type=F32)) * jnp.dot(x1b, wsu_ref[...], preferred_element_type=F32)
    shared = jnp.dot(u.astype(BF16), wsd_ref[...], preferred_element_type=F32)
    base_ref[...] = ALPHA * x1 + shared
    logits = lax.dot_general(wrt_ref[...], x1, (((1,), (1,)), ((), ())), preferred_element_type=F32)
    idx, w, hits = _route(jax.nn.sigmoid(logits), rb_ref[...])
    idx_ref[...] = idx
    wt_ref[...] = w

    chosen = functools.reduce(jnp.logical_or, hits)
    chosen_f = jnp.where(chosen, 1.0, 0.0)
    earlier = (lax.broadcasted_iota(jnp.int32, (tm, tm), 0) < lax.broadcasted_iota(jnp.int32, (tm, tm), 1))
    prefix = jnp.dot(chosen_f.astype(BF16), jnp.where(earlier, 1.0, 0.0).astype(BF16), preferred_element_type=F32)
    before = prefix + carry_ref[...]
    ranks = [jnp.sum(jnp.where(hit, before, 0.0), axis=0, keepdims=True) for hit in hits]
    rank_ref[...] = jnp.concatenate(ranks, axis=0).astype(jnp.int32)
    carry_ref[...] = carry_ref[...] + jnp.sum(chosen_f, axis=1, keepdims=True)
    cnt_ref[...] = carry_ref[...]


def _merge(x_p, x_s, attn_p, attn_s, rnn_p, rnn_s, sga, sgr, weights):
    t_p, t_s = x_p.shape[0], x_s.shape[0]
    t = t_p + t_s
    tm = _pick_tile(math.gcd(t_p, t_s), (512, 256, 128))
    row = lambda width: pl.BlockSpec((tm, width), lambda i: (i, 0))
    col = pl.BlockSpec((TOP_K, tm), lambda i: (0, i))

    return pl.pallas_call(
        functools.partial(_merge_kernel, n_p=t_p // tm, tm=tm),
        grid=(t // tm,),
        in_specs=[*_two_source_specs(t_p, t_s, tm, D_MODEL), *_two_source_specs(t_p, t_s, tm, D_ATTN),
                  *_two_source_specs(t_p, t_s, tm, D_RNN), row(D_MODEL), row(D_MODEL)]
        + [_const_spec(w.shape) for w in weights],
        out_specs=(pl.BlockSpec((tm * ROW_TILE, LANES), lambda i: (i, 0)), row(D_MODEL), col, col, col,
                   _const_spec((N_EXPERTS, 1))),
        out_shape=(
            jax.ShapeDtypeStruct((t * ROW_TILE, LANES), WORD),
            jax.ShapeDtypeStruct((t, D_MODEL), F32),
            jax.ShapeDtypeStruct((TOP_K, t), jnp.int32),
            jax.ShapeDtypeStruct((TOP_K, t), F32),
            jax.ShapeDtypeStruct((TOP_K, t), jnp.int32),
            jax.ShapeDtypeStruct((N_EXPERTS, 1), F32),
        ),
        scratch_shapes=[pltpu.VMEM((N_EXPERTS, 1), F32)],
        compiler_params=_params(("arbitrary",)),
        name="merge_ln1_route",
    )(x_p, x_s, attn_p, attn_s, rnn_p, rnn_s, sga, sgr, *weights)


def _expert_chunk_rows(n_assign):
    return max(EXPERT_ROWS, -(-(n_assign // N_EXPERTS * 9 // 8) // EXPERT_ROWS) * EXPERT_ROWS)


def _expert_layout(counts, n_assign, chunk_rows):
    counts = counts.reshape(N_EXPERTS).astype(jnp.int32)
    padded = (counts + ROW_PAD - 1) // ROW_PAD * ROW_PAD
    pend = jnp.cumsum(padded)
    pstart = pend - padded
    rows_alloc = (n_assign + N_EXPERTS * (ROW_PAD - 1)) // ROW_PAD * ROW_PAD + chunk_rows
    n_chunks = jnp.maximum((counts + chunk_rows - 1) // chunk_rows, 1)
    part = CHUNK_PART
    written = jnp.maximum((counts + part - 1) // part, 1) * part
    ffn_end = jnp.max(pstart + written).reshape(1)
    return pstart, pstart + counts, pend, n_chunks, counts, ffn_end, rows_alloc


def _dest_kernel(idx_ref, rank_ref, pstart_ref, dest_ref, word_rows_ref):
    e_iota = lax.broadcasted_iota(jnp.int32, (N_EXPERTS, idx_ref.shape[1]), 0)
    starts = [jnp.sum(jnp.where(e_iota == idx_ref[k:k + 1, :], pstart_ref[...], 0), axis=0, keepdims=True)
              for k in range(TOP_K)]
    dest = jnp.concatenate(starts, axis=0) + rank_ref[...]
    dest_ref[...] = dest
    tm = dest.shape[1]
    word_rows_ref[0] = jnp.concatenate(
        [dest[k:k + 1, h * SC_ROWS:(h + 1) * SC_ROWS] * ROW_TILE + s
         for h in range(tm // SC_ROWS) for s in range(ROW_TILE) for k in range(TOP_K)], axis=0)


WORD_ROW_VECS = ROW_TILE * TOP_K


def _dest_rows(idx_t, rank_t, pstart):
    t = idx_t.shape[1]
    tm = _pick_tile(t, (1024, 512, 256, 128))
    col = pl.BlockSpec((TOP_K, tm), lambda i: (0, i))
    vecs = tm // SC_ROWS * WORD_ROW_VECS
    dest, word_rows = pl.pallas_call(
        _dest_kernel,
        grid=(t // tm,),
        in_specs=[col, col, _const_spec((N_EXPERTS, 1))],
        out_specs=(col, pl.BlockSpec((1, vecs, SC_ROWS), lambda i: (i, 0, 0))),
        out_shape=(jax.ShapeDtypeStruct((TOP_K, t), jnp.int32),
                   jax.ShapeDtypeStruct((t // tm, vecs, SC_ROWS), jnp.int32)),
        compiler_params=_params(("parallel",)),
        name="dest_rows",
    )(idx_t, rank_t, pstart.reshape(N_EXPERTS, 1))
    return dest, word_rows.reshape(-1, SC_ROWS)


def _token_rows(r, n=1):
    return pl.ds(pl.multiple_of(r * ROW_TILE, ROW_TILE), n * ROW_TILE)


def _zero_row_groups(zero_ref, dst_ref, sem, first_group, n_groups):
    def start(g, c):
        pltpu.make_async_copy(zero_ref, dst_ref.at[_token_rows(g * ROW_PAD, ROW_PAD)], sem).start()
        return c

    lax.fori_loop(first_group, n_groups, start, 0)
    return n_groups - first_group


def _wait_zero_copies(zero_ref, dst_ref, sem, n):
    def wait(_, c):
        pltpu.make_async_copy(zero_ref, dst_ref.at[_token_rows(0, ROW_PAD)], sem).wait()
        return c

    lax.fori_loop(0, n, wait, 0)


SC_ROWS = 128


def _sc_workers():
    info = plsc.get_sparse_core_info()
    mesh = plsc.VectorSubcoreMesh(core_axis_name="c", subcore_axis_name="s")
    worker = lambda: lax.axis_index("s") * info.num_cores + lax.axis_index("c")
    return mesh, info.num_cores * info.num_subcores, worker


def _sc_scatter_rows(src, rows, n_out):
    mesh, n_workers, worker = _sc_workers()
    n_units = src.shape[0] // SC_ROWS
    assert n_units % n_workers == 0 and rows.shape == (n_units * TOP_K, SC_ROWS)
    per_worker = n_units // n_workers

    unit_bufs = [pltpu.VMEM((TOP_K, SC_ROWS), jnp.int32), pltpu.VMEM((SC_ROWS, LANES), src.dtype),
                 pltpu.SemaphoreType.DMA]

    @functools.partial(
        pl.kernel, mesh=mesh,
        out_type=jax.ShapeDtypeStruct((n_out, LANES), src.dtype),
        scratch_types=unit_bufs + unit_bufs + [pltpu.SemaphoreType.DMA],
    )
    def scatter(src_hbm, rows_hbm, out_hbm, rows_a, data_a, load_a, rows_b, data_b, load_b, sem):
        first = worker() * per_worker
        bufs = ((rows_a, data_a, load_a), (rows_b, data_b, load_b))

        def loads(u, buf):
            rows_v, data_v, load_sem = buf
            return (pltpu.make_async_copy(rows_hbm.at[pl.ds(u * TOP_K, TOP_K)], rows_v, load_sem),
                    pltpu.make_async_copy(src_hbm.at[pl.ds(u * SC_ROWS, SC_ROWS)], data_v, load_sem))

        def unit(u, buf, other, has_next):
            for cp in loads(u, buf):
                cp.wait()

            @pl.when(has_next)
            def _():
                for cp in loads(u + 1, other):
                    cp.start()

            rows_v, data_v, _ = buf
            copies = [pltpu.async_copy(data_v, out_hbm.at[rows_v.at[k]], sem) for k in range(TOP_K)]
            for cp in copies:
                cp.wait()

        for cp in loads(first, bufs[0]):
            cp.start()

        @pl.loop(0, per_worker // 2)
        def _(pair):
            u = first + 2 * pair
            unit(u, bufs[0], bufs[1], True)
            unit(u + 1, bufs[1], bufs[0], 2 * pair + 2 < per_worker)

        if per_worker % 2:
            unit(first + per_worker - 1, bufs[0], bufs[1], False)

    return scatter(src, rows)


def _zero_padding_kernel(lo_ref, hi_ref, tail_ref, xs_in_ref, xs_ref, zero_ref, sem, *, rows_alloc):
    del xs_in_ref
    zero_ref[...] = jnp.zeros(zero_ref.shape, WORD)
    row_copy = lambda r: pltpu.make_async_copy(zero_ref.at[pl.ds(0, ROW_TILE)], xs_ref.at[_token_rows(r)], sem)

    def expert(e, n):
        def row(r, c):
            row_copy(r).start()
            return c

        lax.fori_loop(lo_ref[e], hi_ref[e], row, 0)
        return n + hi_ref[e] - lo_ref[e]

    def wait(_, c):
        row_copy(0).wait()
        return c

    lax.fori_loop(0, lax.fori_loop(0, N_EXPERTS, expert, 0), wait, 0)
    n = _zero_row_groups(zero_ref, xs_ref, sem, tail_ref[0] // ROW_PAD, rows_alloc // ROW_PAD)
    _wait_zero_copies(zero_ref, xs_ref, sem, n)


def _zero_padding(xs, pad_lo, pad_hi, total, rows_alloc):
    grid_spec = pltpu.PrefetchScalarGridSpec(
        num_scalar_prefetch=3,
        grid=(1,),
        in_specs=[pl.BlockSpec(memory_space=pl.ANY)],
        out_specs=pl.BlockSpec(memory_space=pl.ANY),
        scratch_shapes=[pltpu.VMEM((ROW_PAD * ROW_TILE, LANES), WORD), pltpu.SemaphoreType.DMA],
    )
    return pl.pallas_call(
        functools.partial(_zero_padding_kernel, rows_alloc=rows_alloc),
        grid_spec=grid_spec,
        out_shape=jax.ShapeDtypeStruct(xs.shape, xs.dtype),
        input_output_aliases={3: 0},
        compiler_params=_params(("arbitrary",)),
        name="zero_padding",
    )(pad_lo, pad_hi, total, xs)


def _expert_kernel(pstart_ref, nch_ref, rows_ref, end_ref, wg_ref, wu_ref, wd_ref, xs_ref, o_ref,
                   xbuf_ref, obuf_ref, wgb_ref, wub_ref, wdb_ref, zero_ref, done_ref, in_sem, out_sem,
                   *, rows_alloc, chunk_rows):
    e = pl.program_id(0)
    n_e = pl.num_programs(0)
    start = pstart_ref[e]
    nch = nch_ref[e]
    buf_rows = chunk_rows * ROW_TILE
    part = CHUNK_PART
    chunk_parts = chunk_rows // part
    part_rows = part * ROW_TILE

    class _ChunkCopy:
        def __init__(self, whole, parts, n_parts):
            self.whole = whole
            self.parts = parts
            self.n_parts = n_parts

        def _each(self, act):
            pl.when(self.n_parts == chunk_parts)(functools.partial(act, self.whole))
            for p, cp in enumerate(self.parts):
                pl.when((p < self.n_parts) & (self.n_parts < chunk_parts))(functools.partial(act, cp))

        def start(self):
            self._each(lambda cp: cp.start())

        def wait(self):
            self._each(lambda cp: cp.wait())

    def parts_of(rows_left):
        return jnp.clip((rows_left + part - 1) // part, 1, chunk_parts)

    def in_copy(row, slot, n_parts):
        copy = lambda p, n: pltpu.make_async_copy(
            xs_ref.at[_token_rows(row + p * part, n * part)],
            xbuf_ref.at[pl.ds(slot * buf_rows + p * part_rows, n * part_rows)], in_sem.at[slot])
        return _ChunkCopy(copy(0, chunk_parts), [copy(p, 1) for p in range(chunk_parts - 1)], n_parts)

    def out_copy(row, slot, n_parts):
        copy = lambda p, n: pltpu.make_async_copy(
            obuf_ref.at[pl.ds(slot * buf_rows + p * part_rows, n * part_rows)],
            o_ref.at[_token_rows(row + p * part, n * part)], out_sem)
        return _ChunkCopy(copy(0, chunk_parts), [copy(p, 1) for p in range(chunk_parts - 1)], n_parts)

    @pl.when(e == 0)
    def _():
        done_ref[0] = 0
        xbuf_ref[...] = jnp.zeros(xbuf_ref.shape, WORD)
        obuf_ref[...] = jnp.zeros(obuf_ref.shape, WORD)
        in_copy(start, 0, parts_of(rows_ref[0])).start()

    wgb_ref[...] = wg_ref[...].astype(BF16)
    wub_ref[...] = wu_ref[...].astype(BF16)
    wdb_ref[...] = wd_ref[...].astype(BF16)
    done = done_ref[0]

    def chunk(c, carry):
        g = done + c
        slot = g % 2
        row = start + c * chunk_rows
        rows_left = rows_ref[e] - c * chunk_rows
        n_parts = parts_of(rows_left)
        in_copy(row, slot, n_parts).wait()
        last = c + 1 == nch
        next_e = jnp.minimum(e + 1, n_e - 1)
        next_row = jnp.where(last, pstart_ref[next_e], row + chunk_rows)
        next_left = jnp.where(last, rows_ref[next_e], rows_left - chunk_rows)

        @pl.when(jnp.logical_not(last & (e == n_e - 1)))
        def _():
            in_copy(next_row, 1 - slot, parts_of(next_left)).start()

        base = pl.multiple_of(slot * buf_rows, buf_rows)
        def sub_block(h):
            xb = jnp.concatenate(_load_token_rows(xbuf_ref, EXPERT_ROWS, row0=h * EXPERT_ROWS, base=base),
                                 axis=1).astype(BF16)
            gate = jnp.dot(xb, wgb_ref[...], preferred_element_type=F32)
            up = jnp.dot(xb, wub_ref[...], preferred_element_type=F32)
            act = (jax.nn.silu(gate) * up).astype(BF16)
            _store_token_rows(obuf_ref, jnp.dot(act, wdb_ref[...], preferred_element_type=F32),
                              EXPERT_ROWS, row0=h * EXPERT_ROWS, base=base)

        n_sub = chunk_rows // EXPERT_ROWS
        needed = jnp.clip((rows_left + EXPERT_ROWS - 1) // EXPERT_ROWS, 1, n_sub)
        for count in range(1, n_sub + 1):
            @pl.when(needed == count)
            def _():
                for h in range(count):
                    sub_block(h)

        @pl.when(g > 0)
        def _():
            out_copy(0, 0, done_ref[1]).wait()

        out_copy(row, slot, n_parts).start()
        done_ref[1] = n_parts
        return carry

    lax.fori_loop(0, nch, chunk, 0)
    done_ref[0] = done + nch

    @pl.when(e == n_e - 1)
    def _():
        out_copy(0, 0, done_ref[1]).wait()
        zero_ref[...] = jnp.zeros(zero_ref.shape, WORD)
        n = _zero_row_groups(zero_ref, o_ref, out_sem, end_ref[0] // ROW_PAD, rows_alloc // ROW_PAD)
        _wait_zero_copies(zero_ref, o_ref, out_sem, n)


def _expert_ffn(xs, pstart, n_chunks, rows, ffn_end, rows_alloc, chunk_rows, w_e_gate, w_e_up, w_e_down):
    weight = lambda shape: pl.BlockSpec((None, *shape), lambda e, *_: (e, 0, 0))
    grid_spec = pltpu.PrefetchScalarGridSpec(
        num_scalar_prefetch=4,
        grid=(N_EXPERTS,),
        in_specs=[weight((D_MODEL, D_EXPERT)), weight((D_MODEL, D_EXPERT)), weight((D_EXPERT, D_MODEL)),
                  pl.BlockSpec(memory_space=pl.ANY)],
        out_specs=pl.BlockSpec(memory_space=pl.ANY),
        scratch_shapes=[
            pltpu.VMEM((2 * chunk_rows * ROW_TILE, LANES), WORD),
            pltpu.VMEM((2 * chunk_rows * ROW_TILE, LANES), WORD),
            pltpu.VMEM((D_MODEL, D_EXPERT), BF16), pltpu.VMEM((D_MODEL, D_EXPERT), BF16),
            pltpu.VMEM((D_EXPERT, D_MODEL), BF16),
            pltpu.VMEM((ROW_PAD * ROW_TILE, LANES), WORD),
            pltpu.SMEM((2,), jnp.int32),
            pltpu.SemaphoreType.DMA((2,)), pltpu.SemaphoreType.DMA,
        ],
    )
    return pl.pallas_call(
        functools.partial(_expert_kernel, rows_alloc=rows_alloc, chunk_rows=chunk_rows),
        grid_spec=grid_spec,
        out_shape=jax.ShapeDtypeStruct((rows_alloc * ROW_TILE, LANES), WORD),
        compiler_params=_params(("arbitrary",)),
        name="expert_ffn",
    )(pstart, n_chunks, rows, ffn_end, w_e_gate, w_e_up, w_e_down, xs)


def _combine_head_kernel(dest_ref, dest_next_ref, w_ref, base_ref, g_ref, b_ref, outs_ref, y_ref, buf_ref, sem, *, tm):
    i = pl.program_id(0)
    slot_rows = TOP_K * tm

    def gather(d_ref, slot):
        def issue(t, c):
            for k in range(TOP_K):
                pltpu.make_async_copy(outs_ref.at[_token_rows(d_ref[k, t])],
                                      buf_ref.at[_token_rows(slot * slot_rows + k * tm + t)],
                                      sem.at[slot]).start(priority=k % 2)
            return c

        lax.fori_loop(0, tm, issue, 0)

    @pl.when(i == 0)
    def _():
        gather(dest_ref, 0)

    @pl.when(i + 1 < pl.num_programs(0))
    def _():
        gather(dest_next_ref, (i + 1) % 2)

    slot = i % 2
    for k in range(TOP_K):
        pltpu.make_async_copy(outs_ref.at[_token_rows(0, tm)], buf_ref.at[_token_rows(slot * slot_rows + k * tm, tm)],
                              sem.at[slot]).wait()

    w = w_ref[...]
    base = pl.multiple_of(slot * slot_rows * ROW_TILE, ROW_TILE)
    halves = [None, None]
    for k in range(TOP_K):
        wk = w[:, k:k + 1]
        for j, rows in enumerate(_load_token_rows(buf_ref, tm, row0=k * tm, base=base)):
            halves[j] = wk * rows if halves[j] is None else halves[j] + wk * rows
    y_ref[...] = _layer_norm(base_ref[...] + jnp.concatenate(halves, axis=1), g_ref[...], b_ref[...])


def _combine_head(out_sorted, dest, w_tok, base, g2, b2, tm, n_head):
    row = pl.BlockSpec((tm, D_MODEL), lambda i: (i, 0))
    return pl.pallas_call(
        functools.partial(_combine_head_kernel, tm=tm),
        grid=(n_head,),
        in_specs=[
            pl.BlockSpec((TOP_K, tm), lambda i: (0, i), memory_space=pltpu.SMEM),
            pl.BlockSpec((TOP_K, tm), lambda i: (0, jnp.minimum(i + 1, n_head - 1)), memory_space=pltpu.SMEM),
            pl.BlockSpec((tm, TOP_K), lambda i: (i, 0)),
            row, _const_spec((1, D_MODEL)), _const_spec((1, D_MODEL)),
            pl.BlockSpec(memory_space=pl.ANY),
        ],
        out_specs=row,
        out_shape=jax.ShapeDtypeStruct((n_head * tm, D_MODEL), F32),
        scratch_shapes=[pltpu.VMEM((2 * TOP_K * tm * ROW_TILE, LANES), WORD), pltpu.SemaphoreType.DMA((2,))],
        compiler_params=_params(("arbitrary",)),
        name="combine_head",
    )(dest, dest, w_tok, base, g2, b2, out_sorted)


def _sc_gather_rows(table, rows):
    mesh, n_workers, worker = _sc_workers()
    m = rows.shape[0]
    per_worker = m // n_workers
    assert m % n_workers == 0 and per_worker % SC_ROWS == 0
    in_flight = _pick_tile(per_worker // SC_ROWS, (2, 1))
    step = SC_ROWS * in_flight
    n_stages = per_worker // step
    stage_bufs = [pltpu.VMEM((step,), jnp.int32), pltpu.VMEM((step, LANES), table.dtype),
                  pltpu.SemaphoreType.DMA, pltpu.SemaphoreType.DMA]

    @functools.partial(
        pl.kernel, mesh=mesh,
        out_type=jax.ShapeDtypeStruct((m, LANES), table.dtype),
        scratch_types=stage_bufs + stage_bufs,
    )
    def gather(table_hbm, rows_hbm, out_hbm, rows_a, data_a, gsem_a, ssem_a, rows_b, data_b, gsem_b, ssem_b):
        first = worker() * per_worker
        bufs = ((rows_a, data_a, gsem_a, ssem_a), (rows_b, data_b, gsem_b, ssem_b))

        def gathers(buf):
            rows_v, data_v, gsem, _ = buf
            return [pltpu.make_async_copy(table_hbm.at[rows_v.at[pl.ds(j * SC_ROWS, SC_ROWS)]],
                                          data_v.at[pl.ds(j * SC_ROWS, SC_ROWS)], gsem) for j in range(in_flight)]

        def store(off, buf):
            return pltpu.make_async_copy(buf[1], out_hbm.at[pl.ds(off, step)], buf[3])

        def stage(it, buf, other, has_next, other_stored):
            off = first + it * step

            @pl.when(has_next)
            def _():
                @pl.when(other_stored)
                def _():
                    store(off, other).wait()

                pltpu.sync_copy(rows_hbm.at[pl.ds(off + step, step)], other[0])
                for cp in gathers(other):
                    cp.start()

            for cp in gathers(buf):
                cp.wait()
            store(off, buf).start()

        pltpu.sync_copy(rows_hbm.at[pl.ds(first, step)], rows_a)
        for cp in gathers(bufs[0]):
            cp.start()

        @pl.loop(0, n_stages // 2)
        def _(pair):
            it = 2 * pair
            stage(it, bufs[0], bufs[1], True, pair > 0)
            stage(it + 1, bufs[1], bufs[0], it + 2 < n_stages, True)

        if n_stages % 2:
            stage(n_stages - 1, bufs[0], bufs[1], False, False)
        for back in range(min(2, n_stages)):
            store(first, bufs[(n_stages - 1 - back) % 2]).wait()

    return gather(table, rows)


def _combine_tail_kernel(yh_ref, rows_ref, w_ref, base_ref, g_ref, b_ref, yp_ref, ys_ref, *, tm, n_head, n_p):
    i = pl.program_id(0)

    @pl.when(i < n_head)
    def _():
        yp_ref[...] = yh_ref[...]

    @pl.when(i >= n_head)
    def _():
        w = w_ref[...]
        lo = [None] * ROW_TILE
        hi = [None] * ROW_TILE
        for k in range(TOP_K):
            wk = w[:, k:k + 1]
            for s in range(ROW_TILE):
                vec = lambda h: ((h * ROW_TILE + s) * TOP_K + k) * SC_ROWS
                words = jnp.concatenate([rows_ref[vec(h):vec(h) + SC_ROWS, :] for h in range(tm // SC_ROWS)], axis=0)
                lo_s, hi_s = _unpack_words(words)
                lo[s] = wk * lo_s if lo[s] is None else lo[s] + wk * lo_s
                hi[s] = wk * hi_s if hi[s] is None else hi[s] + wk * hi_s
        y = _layer_norm(base_ref[...] + jnp.concatenate(lo + hi, axis=1), g_ref[...], b_ref[...])

        @pl.when(i < n_p)
        def _():
            yp_ref[...] = y

        @pl.when(i >= n_p)
        def _():
            ys_ref[...] = y


def _combine_tail(y_head, gathered, w_tok, base, g2, b2, t_p, t_s, tm, n_head):
    n_p = t_p // tm
    assert n_head <= n_p
    n_tiles = (t_p + t_s) // tm
    blk = TOP_K * ROW_TILE * tm
    row = pl.BlockSpec((tm, D_MODEL), lambda i: (i, 0))
    out_p, out_s = _two_source_specs(t_p, t_s, tm, D_MODEL)
    return pl.pallas_call(
        functools.partial(_combine_tail_kernel, tm=tm, n_head=n_head, n_p=n_p),
        grid=(n_tiles,),
        in_specs=[
            pl.BlockSpec((tm, D_MODEL), lambda i: (jnp.minimum(i, n_head - 1), 0)),
            pl.BlockSpec((blk, LANES), lambda i: (jnp.maximum(i - n_head, 0), 0)),
            pl.BlockSpec((tm, TOP_K), lambda i: (i, 0)),
            row, _const_spec((1, D_MODEL)), _const_spec((1, D_MODEL)),
        ],
        out_specs=(out_p, out_s),
        out_shape=(jax.ShapeDtypeStruct((t_p, D_MODEL), F32), jax.ShapeDtypeStruct((t_s, D_MODEL), F32)),
        compiler_params=_params(("arbitrary",)),
        name="combine_tail",
    )(y_head, gathered, w_tok, base, g2, b2)


def kernel(x_prompt, x_sample, cache_k, cache_v, state_conv, state_rnn, w_in, conv_w, conv_b, w_gate_a, b_gate_a, w_gate_x, b_gate_x, lru_lambda, rel_bias, sinks, w_o_attn, w_o_rnn, w_out, ln1_g, ln1_b, w_router, router_bias, w_e_gate, w_e_up, w_e_down, w_s_gate, w_s_up, w_s_down, ln2_g, ln2_b):
    assert w_in.shape[0] == DEPTH == 1
    batch, seq, _ = x_prompt.shape
    dec_batch, s_len, _ = x_sample.shape
    w_cache = cache_k.shape[2]
    assert s_len == SUBLANES and seq % WINDOW == 0 and w_cache == WINDOW
    t_p = batch * seq
    t_s = dec_batch * s_len
    vec = lambda a: a[0].reshape(1, -1).astype(F32)

    x_p = x_prompt.reshape(t_p, D_MODEL)
    x_s = x_sample.reshape(t_s, D_MODEL)
    q, k, v, xr, gy, sga, sgr = _inproj(x_p, x_s, w_in[0].astype(BF16))

    qi = jnp.arange(WINDOW)
    dist = qi[:, None] + WINDOW - jnp.arange(2 * WINDOW)[None, :]
    band = (dist >= 0) & (dist <= WINDOW)
    has_prev = jnp.arange(2 * WINDOW)[None, :] >= WINDOW
    bias_p = _bias_table(rel_bias, dist)
    bias_p = jnp.stack([jnp.where(band & has_prev, bias_p, NEG_INF), jnp.where(band, bias_p, NEG_INF)])
    attn_p = _attn_prompt(q, k, v, bias_p, sinks[0], batch, seq)
    attn_s, k_s, v_s = _attn_sample(
        q, k, v, cache_k[0].reshape(dec_batch, w_cache, D_KV), cache_v[0].reshape(dec_batch, w_cache, D_KV),
        rel_bias, sinks[0], t_p, dec_batch, s_len)

    rnn_w = (conv_w[0], vec(conv_b), w_gate_a[0].astype(BF16), vec(b_gate_a), w_gate_x[0].astype(BF16),
             vec(b_gate_x), vec(lru_lambda))
    rnn_p, h_p = _rnn_prompt(xr, gy, rnn_w, batch, seq)
    hist_pad = jnp.pad(state_conv[0], ((0, 0), (SUBLANES - (CONV_W - 1), 0), (0, 0))).reshape(t_s, D_RNN)
    rnn_s, h_s = _rnn_sample(xr, gy, hist_pad, state_rnn[0], rnn_w, t_p, dec_batch)

    merge_w = (w_o_attn[0].astype(BF16), w_o_rnn[0].astype(BF16), w_out[0].astype(BF16), vec(ln1_g), vec(ln1_b),
               w_router[0].T, router_bias[0].reshape(N_EXPERTS, 1), w_s_gate[0].astype(BF16),
               w_s_up[0].astype(BF16), w_s_down[0].astype(BF16))
    x1w, base, idx_t, wt_t, rank_t, counts = _merge(x_p, x_s, attn_p, attn_s, rnn_p, rnn_s, sga, sgr, merge_w)

    n_assign = (t_p + t_s) * TOP_K
    chunk_rows = _expert_chunk_rows(n_assign)
    pstart, pad_lo, pad_hi, n_chunks, rows, ffn_end, rows_alloc = _expert_layout(counts, n_assign, chunk_rows)
    tm = _pick_tile(math.gcd(t_p, t_s), (256, 128))
    dest, word_rows = _dest_rows(idx_t, rank_t, pstart)
    xs = _sc_scatter_rows(x1w, word_rows, rows_alloc * ROW_TILE)
    xs = _zero_padding(xs, pad_lo, pad_hi, pad_hi[N_EXPERTS - 1:], rows_alloc)
    out_sorted = _expert_ffn(xs, pstart, n_chunks, rows, ffn_end, rows_alloc, chunk_rows,
                             w_e_gate[0], w_e_up[0], w_e_down[0])
    n_tiles = (t_p + t_s) // tm
    n_head = min(t_p // tm, n_tiles * 5 // 16)
    w_tok = wt_t.T
    g2, b2 = vec(ln2_g), vec(ln2_b)
    y_head = _combine_head(out_sorted, dest, w_tok, base, g2, b2, tm, n_head)
    gathered = _sc_gather_rows(out_sorted, word_rows[n_head * (tm // SC_ROWS) * WORD_ROW_VECS:].reshape(-1))
    y_p, y_s = _combine_tail(y_head, gathered, w_tok, base, g2, b2, t_p, t_s, tm, n_head)
    y_p = y_p.reshape(batch, seq, D_MODEL)
    y_s = y_s.reshape(dec_batch, s_len, D_MODEL)
    kv5 = lambda a, b: a.reshape(1, b, WINDOW, N_KV_HEADS, HEAD_DIM)
    tail = lambda a, n: jnp.stack([lax.slice_in_dim(a, (b + 1) * seq - n, (b + 1) * seq) for b in range(batch)])
    k_p = kv5(tail(k, WINDOW), batch)
    v_p = kv5(tail(v, WINDOW), batch)
    conv_p = tail(xr, CONV_W - 1)[None]
    conv_s = xr[t_p:].reshape(dec_batch, s_len, D_RNN)[:, s_len - (CONV_W - 1):][None]
    return (y_p, y_s, k_p, v_p, conv_p, h_p.reshape(1, batch, D_RNN),
            kv5(k_s, dec_batch), kv5(v_s, dec_batch), conv_s, h_s.reshape(1, dec_batch, D_RNN))
```
